```python
import math
import numpy as np
import jax
import jax.numpy as jnp
from jax import lax

D_MODEL = 2048
BATCH = 4
SEQ = 4096
DEPTH = 2

F32 = jnp.float32
N_EVEN = (DEPTH + 1) // 2
N_ODD = DEPTH // 2
NORM_EPS = 1e-6
NEG_BIG = -1e30

SSM_HEADS = 32
SSM_HEAD_DIM = 64
D_SSM = SSM_HEADS * SSM_HEAD_DIM
SSM_GROUPS = 4
D_STATE = 128
CONV_WIDTH = 4
SSD_CHUNK = 128
D_CONV = D_SSM + 2 * SSM_GROUPS * D_STATE

NSA_HEADS = 16
NSA_KV_HEADS = 4
NSA_HEAD_DIM = 128
NSA_Q_PER_KV = NSA_HEADS // NSA_KV_HEADS
D_NSA = NSA_HEADS * NSA_HEAD_DIM
D_NSA_KV = NSA_KV_HEADS * NSA_HEAD_DIM
CMP_BLOCK = 32
CMP_STRIDE = 16
CMP_HIDDEN = 256
SEL_BLOCK = 64
N_SELECT = 16
WINDOW = 512
NSA_QUERY_BLOCK = 32
ROPE_DIM = NSA_HEAD_DIM // 4
ROPE_THETA = 500000.0

EVEN_SPLITS = (D_SSM, D_CONV, SSM_HEADS, D_NSA, D_NSA_KV, D_NSA_KV, D_NSA_KV, D_NSA_KV, D_NSA_KV, D_NSA_KV, 3 * NSA_HEADS)
D_IN_EVEN = sum(EVEN_SPLITS)
D_MIX_EVEN = D_SSM + D_NSA

RWKV_HEAD_DIM = 64
RWKV_HEADS = D_MODEL // RWKV_HEAD_DIM
DECAY_LORA = max(32, int(round(1.8 * D_MODEL ** 0.5 / 32)) * 32)
AAA_LORA = DECAY_LORA
GATE_LORA = max(32, int(round(0.6 * D_MODEL ** 0.8 / 32)) * 32)
RWKV_GN_EPS = 1e-5 * RWKV_HEAD_DIM
N_TOKEN_MIX = 6

MEM_LEN = 256
XATTN_HEADS = 4
XATTN_HEAD_DIM = 128
D_XATTN = XATTN_HEADS * XATTN_HEAD_DIM

D_FF_DENSE = 5632
N_EXPERTS = 8
TOP_K = 2
D_FF_EXPERT = 7168
MOE_BLOCK = 256

kernel_name = 'hybrid_ssd_nsa_rwkv7_moe_trunk'


def rms_norm(x, gain, eps=NORM_EPS):
    xf = x.astype(F32)
    y = xf * lax.rsqrt(jnp.mean(xf * xf, axis=-1, keepdims=True) + eps)
    return (y * gain.astype(F32)).astype(x.dtype)


def masked_softmax(scores, mask):
    s = jnp.where(mask, scores.astype(F32), NEG_BIG)
    return jax.nn.softmax(s, axis=-1) * mask


def partial_rope(x, positions):
    half = ROPE_DIM // 2
    inv_freq = jnp.exp(-math.log(ROPE_THETA) * jnp.arange(0, ROPE_DIM, 2, dtype=F32) / ROPE_DIM)
    ang = positions.astype(F32)[..., None] * inv_freq
    cos = jnp.cos(ang)[:, :, None, :]
    sin = jnp.sin(ang)[:, :, None, :]
    xf = x.astype(F32)
    x1 = xf[..., :half]
    x2 = xf[..., half:ROPE_DIM]
    out = jnp.concatenate([x1 * cos - x2 * sin, x2 * cos + x1 * sin, xf[..., ROPE_DIM:]], axis=-1)
    return out.astype(x.dtype)


def causal_depthwise_conv(x, w, b):
    rhs = jnp.transpose(w)[:, None, :].astype(x.dtype)
    y = lax.conv_general_dilated(x, rhs, window_strides=(1,), padding=[(CONV_WIDTH - 1, 0)],
                                 dimension_numbers=('NWC', 'WIO', 'NWC'),
                                 feature_group_count=x.shape[-1])
    return y + b.astype(x.dtype)


def ssd_chunked_scan(xs, dt, a, bm, cm):
    bsz, seq = xs.shape[0], xs.shape[1]
    G, R, P, N, L = SSM_GROUPS, SSM_HEADS // SSM_GROUPS, SSM_HEAD_DIM, D_STATE, SSD_CHUNK
    nc = seq // L
    xdt = (xs.astype(F32) * dt[..., None]).reshape(bsz, nc, L, G, R, P)
    adt = (dt * a).reshape(bsz, nc, L, G, R).transpose(0, 1, 3, 4, 2)
    bc = bm.astype(F32).reshape(bsz, nc, L, G, N)
    cc = cm.astype(F32).reshape(bsz, nc, L, G, N)
    a_cum = jnp.cumsum(adt, axis=-1)
    causal = jnp.tril(jnp.ones((L, L), dtype=bool))
    decay = jnp.exp(jnp.where(causal, a_cum[..., :, None] - a_cum[..., None, :], NEG_BIG))
    cb = jnp.einsum('bclgn,bcsgn->bcgls', cc, bc)
    y_diag = jnp.einsum('bcgls,bcgrls,bcsgrp->bclgrp', cb, decay, xdt)
    decay_to_end = jnp.exp(a_cum[..., -1:] - a_cum)
    chunk_states = jnp.einsum('bclgn,bcgrl,bclgrp->bcgrpn', bc, decay_to_end, xdt)
    chunk_decay = jnp.exp(a_cum[..., -1])

    def carry_state(h, inp):
        st, dec = inp
        return h * dec[..., None, None] + st, h

    h0 = jnp.zeros((bsz, G, R, P, N), F32)
    _, h_in = lax.scan(carry_state, h0, (jnp.moveaxis(chunk_states, 1, 0), jnp.moveaxis(chunk_decay, 1, 0)))
    h_in = jnp.moveaxis(h_in, 0, 1)
    y_off = jnp.einsum('bclgn,bcgrpn,bcgrl->bclgrp', cc, h_in, jnp.exp(a_cum))
    return (y_diag + y_off).reshape(bsz, seq, SSM_HEADS, P)


def mamba2_group(z, xbc, dt_raw, conv_w, conv_b, dt_bias, a_log, d_skip, norm_w):
    bsz, seq, _ = z.shape
    xbc = jax.nn.silu(causal_depthwise_conv(xbc, conv_w, conv_b))
    xs, bm, cm = jnp.split(xbc, [D_SSM, D_SSM + SSM_GROUPS * D_STATE], axis=-1)
    xs = xs.reshape(bsz, seq, SSM_HEADS, SSM_HEAD_DIM)
    bm = bm.reshape(bsz, seq, SSM_GROUPS, D_STATE)
    cm = cm.reshape(bsz, seq, SSM_GROUPS, D_STATE)
    dt = jax.nn.softplus(dt_raw.astype(F32) + dt_bias.astype(F32))
    a = -jnp.exp(a_log.astype(F32))
    y = ssd_chunked_scan(xs, dt, a, bm, cm) + xs.astype(F32) * d_skip.astype(F32)[:, None]
    y = y.reshape(bsz, seq, D_SSM) * jax.nn.silu(z.astype(F32))
    y = y.reshape(bsz, seq, SSM_GROUPS, D_SSM // SSM_GROUPS)
    y = y * lax.rsqrt(jnp.mean(y * y, axis=-1, keepdims=True) + NORM_EPS)
    return (y.reshape(bsz, seq, D_SSM) * norm_w.astype(F32)).astype(z.dtype)


def compress_blocks(u, pe, w1, w2):
    bsz, seq, nkv, hd = u.shape
    n_cmp = (seq - CMP_BLOCK) // CMP_STRIDE + 1
    idx = jnp.arange(n_cmp)[:, None] * CMP_STRIDE + jnp.arange(CMP_BLOCK)[None, :]
    blocks = u[:, idx] + pe[:, None, :].astype(u.dtype)
    blocks = blocks.transpose(0, 1, 3, 2, 4).reshape(bsz, n_cmp, nkv, CMP_BLOCK * hd)
    return jax.nn.silu(blocks @ w1) @ w2


def selection_overlap(n_cmp, n_blk):
    c0 = np.arange(n_cmp)[:, None] * CMP_STRIDE
    s0 = np.arange(n_blk)[None, :] * SEL_BLOCK
    ov = np.minimum(c0 + CMP_BLOCK, s0 + SEL_BLOCK) - np.maximum(c0, s0)
    return jnp.asarray(np.clip(ov, 0, None) / CMP_STRIDE, dtype=F32)


def nsa_group(q, k_cmp, v_cmp, k_sel, v_sel, k_win, v_win, gate_logits, kc_gain, pe_k, pe_v, wk1, wk2, wv1, wv2):
    bsz, seq = q.shape[0], q.shape[1]
    G, R, hd = NSA_KV_HEADS, NSA_Q_PER_KV, NSA_HEAD_DIM
    scale = hd ** -0.5
    t = jnp.arange(seq)
    qg = q.reshape(bsz, seq, G, R, hd)
    kc = rms_norm(compress_blocks(k_cmp, pe_k, wk1, wk2), kc_gain)
    vc = compress_blocks(v_cmp, pe_v, wv1, wv2)
    n_cmp = kc.shape[1]
    cmp_visible = (jnp.arange(n_cmp) * CMP_STRIDE + CMP_BLOCK - 1)[None, :] <= t[:, None]
    p_cmp = masked_softmax(jnp.einsum('bsgrd,bcgd->bgrsc', qg, kc) * scale, cmp_visible)
    o_cmp = jnp.einsum('bgrsc,bcgd->bsgrd', p_cmp.astype(vc.dtype), vc)
    n_blk = seq // SEL_BLOCK
    n_sel = min(N_SELECT, n_blk)
    imp = jnp.einsum('bgrsc,cj->bsgj', p_cmp, selection_overlap(n_cmp, n_blk))
    cur = (t // SEL_BLOCK)[:, None]
    blk = jnp.arange(n_blk)[None, :]
    causal_blk = (blk <= cur)[:, None, :]
    forced = ((blk == 0) | (blk >= cur - 1))[:, None, :] & causal_blk
    score = jnp.where(forced, jnp.inf, jnp.where(causal_blk, imp, -jnp.inf))
    top_score, sel_idx = lax.top_k(score, n_sel)
    sel_valid = top_score > -jnp.inf
    ks_blocks = k_sel.reshape(bsz, n_blk, SEL_BLOCK, G, hd).transpose(0, 3, 1, 2, 4)
    vs_blocks = v_sel.reshape(bsz, n_blk, SEL_BLOCK, G, hd).transpose(0, 3, 1, 2, 4)
    pad = ((0, 0), (WINDOW, 0), (0, 0), (0, 0))
    kw_pad = jnp.pad(k_win, pad)
    vw_pad = jnp.pad(v_win, pad)
    b_ix = jnp.arange(bsz)[:, None, None, None]
    g_ix = jnp.arange(G)[None, None, :, None]
    qb = NSA_QUERY_BLOCK

    def query_block(i):
        start = i * qb
        qi = lax.dynamic_slice_in_dim(qg, start, qb, axis=1)
        ti = start + jnp.arange(qb)
        idx = lax.dynamic_slice_in_dim(sel_idx, start, qb, axis=1)
        valid = lax.dynamic_slice_in_dim(sel_valid, start, qb, axis=1)
        kg = ks_blocks[b_ix, g_ix, idx]
        vg = vs_blocks[b_ix, g_ix, idx]
        tok = idx[..., None] * SEL_BLOCK + jnp.arange(SEL_BLOCK)
        m_sel = (valid[..., None] & (tok <= ti[None, :, None, None, None])).reshape(bsz, qb, G, 1, n_sel * SEL_BLOCK)
        s_sel = (jnp.einsum('bqgrd,bqgnkd->bqgrnk', qi, kg) * scale).reshape(bsz, qb, G, R, n_sel * SEL_BLOCK)
        p_sel = masked_softmax(s_sel, m_sel).reshape(bsz, qb, G, R, n_sel, SEL_BLOCK)
        o_sel = jnp.einsum('bqgrnk,bqgnkd->bqgrd', p_sel.astype(vg.dtype), vg)
        kw = lax.dynamic_slice_in_dim(kw_pad, start, qb + WINDOW, axis=1)
        vw = lax.dynamic_slice_in_dim(vw_pad, start, qb + WINDOW, axis=1)
        kp = start - WINDOW + jnp.arange(qb + WINDOW)
        m_win = (kp[None, :] <= ti[:, None]) & (kp[None, :] > ti[:, None] - WINDOW) & (kp[None, :] >= 0)
        s_win = jnp.einsum('bqgrd,bkgd->bqgrk', qi, kw) * scale
        p_win = masked_softmax(s_win, m_win[None, :, None, None, :])
        o_win = jnp.einsum('bqgrk,bkgd->bqgrd', p_win.astype(vw.dtype), vw)
        return o_sel, o_win

    o_sel, o_win = lax.map(query_block, jnp.arange(seq // qb))

    def unblock(o):
        return jnp.moveaxis(o, 0, 1).reshape(bsz, seq, G, R, hd)

    gate = jax.nn.sigmoid(gate_logits).reshape(bsz, seq, G, R, 3)
    out = gate[..., 0:1] * o_cmp + gate[..., 1:2] * unblock(o_sel) + gate[..., 2:3] * unblock(o_win)
    return out.reshape(bsz, seq, D_NSA)


def even_mixer(h, positions, w_in, conv_w, conv_b, dt_bias, a_log, d_skip, ssm_norm, q_gain, kc_gain, ks_gain, kw_gain, pe_k, pe_v, wk1, wk2, wv1, wv2, w_out):
    bsz, seq, _ = h.shape
    cuts = [int(c) for c in np.cumsum(EVEN_SPLITS)[:-1]]
    z, xbc, dt_raw, q, kc, vc, ks, vs, kw, vw, gl = jnp.split(h @ w_in, cuts, axis=-1)
    y_ssm = mamba2_group(z, xbc, dt_raw, conv_w, conv_b, dt_bias, a_log, d_skip, ssm_norm)

    def heads(u, n):
        return u.reshape(bsz, seq, n, NSA_HEAD_DIM)

    q = partial_rope(rms_norm(heads(q, NSA_HEADS), q_gain), positions)
    ks = partial_rope(rms_norm(heads(ks, NSA_KV_HEADS), ks_gain), positions)
    kw = partial_rope(rms_norm(heads(kw, NSA_KV_HEADS), kw_gain), positions)
    y_nsa = nsa_group(q, heads(kc, NSA_KV_HEADS), heads(vc, NSA_KV_HEADS), ks, heads(vs, NSA_KV_HEADS),
                      kw, heads(vw, NSA_KV_HEADS), gl, kc_gain, pe_k, pe_v, wk1, wk2, wv1, wv2)
    return jnp.concatenate([y_ssm, y_nsa], axis=-1) @ w_out


def wkv7_scan(r, w, k, v, a, b):
    def step(state, inp):
        r_t, w_t, k_t, v_t, a_t, b_t = inp
        sa = jnp.einsum('bhij,bhj->bhi', state, a_t)
        state = state * w_t[:, :, None, :] + sa[..., None] * b_t[:, :, None, :] + v_t[..., None] * k_t[:, :, None, :]
        return state, jnp.einsum('bhij,bhj->bhi', state, r_t)

    bsz, _, nh, n = r.shape
    xs = (jnp.moveaxis(r, 1, 0), jnp.moveaxis(w, 1, 0), jnp.moveaxis(k, 1, 0),
          jnp.moveaxis(v, 1, 0), jnp.moveaxis(a, 1, 0), jnp.moveaxis(b, 1, 0))
    _, y = lax.scan(step, jnp.zeros((bsz, nh, n, n), F32), xs)
    return jnp.moveaxis(y, 0, 1)


def rwkv7_time_mix(h, mu, w_r, w_k, w_v, w_o, w0, w1, w2, a0, a1, a2, g1, g2, k_k, k_a, r_k, ln_w, ln_b):
    bsz, seq, _ = h.shape
    xx = jnp.pad(h, ((0, 0), (1, 0), (0, 0)))[:, :-1] - h

    def mix(j):
        return h + xx * mu[j]

    def heads(u):
        return u.astype(F32).reshape(bsz, seq, RWKV_HEADS, RWKV_HEAD_DIM)

    r = mix(0) @ w_r
    w = -jax.nn.softplus(-(w0 + jnp.tanh(mix(1) @ w1) @ w2)) - 0.5
    k = mix(2) @ w_k
    v = mix(3) @ w_v
    a = jax.nn.sigmoid(a0 + (mix(4) @ a1) @ a2)
    g = jax.nn.sigmoid(mix(5) @ g1) @ g2
    kk = heads(k * k_k)
    kk = kk / jnp.maximum(jnp.sqrt(jnp.sum(kk * kk, axis=-1, keepdims=True)), 1e-12)
    kh = heads(k * (1 + (a - 1) * k_a))
    ah = heads(a)
    rh = heads(r)
    vh = heads(v)
    decay = jnp.exp(-jnp.exp(heads(w)))
    y = wkv7_scan(rh, decay, kh, vh, -kk, kk * ah)
    mean = jnp.mean(y, axis=-1, keepdims=True)
    var = jnp.mean(jnp.square(y - mean), axis=-1, keepdims=True)
    y = ((y - mean) * lax.rsqrt(var + RWKV_GN_EPS)).reshape(bsz, seq, D_MODEL) * ln_w.astype(F32) + ln_b.astype(F32)
    bonus = jnp.sum(rh * kh * r_k.astype(F32), axis=-1, keepdims=True) * vh
    y = (y + bonus.reshape(bsz, seq, D_MODEL)) * g.astype(F32)
    return y.astype(h.dtype) @ w_o


def memory_cross_attention(h, mem_n, w_q, w_kv, w_o, q_gain, k_gain):
    bsz, seq, _ = h.shape
    n_mem = mem_n.shape[1]
    q = rms_norm((h @ w_q).reshape(bsz, seq, XATTN_HEADS, XATTN_HEAD_DIM), q_gain)
    k, v = jnp.split(mem_n @ w_kv, 2, axis=-1)
    k = rms_norm(k.reshape(bsz, n_mem, XATTN_HEADS, XATTN_HEAD_DIM), k_gain)
    v = v.reshape(bsz, n_mem, XATTN_HEADS, XATTN_HEAD_DIM)
    s = jnp.einsum('bshd,bmhd->bhsm', q, k).astype(F32) * XATTN_HEAD_DIM ** -0.5
    p = jax.nn.softmax(s, axis=-1).astype(v.dtype)
    o = jnp.einsum('bhsm,bmhd->bshd', p, v).reshape(bsz, seq, D_XATTN)
    return o @ w_o


def swiglu(h, w1, w3, w2):
    return (jax.nn.silu(h @ w1) * (h @ w3)) @ w2


def moe_swiglu(h, w_router, w1, w3, w2):
    bsz, seq, d = h.shape
    hf = h.reshape(-1, d)
    n_tok = hf.shape[0]
    logits = (hf @ w_router).astype(F32)
    top_logit, top_e = lax.top_k(logits, TOP_K)
    gate = jax.nn.softmax(top_logit, axis=-1)
    n_assign = n_tok * TOP_K
    flat_e = top_e.reshape(n_assign)
    flat_tok = jnp.arange(n_assign, dtype=jnp.int32) // TOP_K
    flat_g = gate.reshape(n_assign)
    order = jnp.argsort(flat_e)
    e_sorted = flat_e[order]
    counts = jnp.bincount(flat_e, length=N_EXPERTS)
    padded = (counts + MOE_BLOCK - 1) // MOE_BLOCK * MOE_BLOCK
    start = jnp.cumsum(counts) - counts
    pstart = jnp.cumsum(padded) - padded
    dest = pstart[e_sorted] + jnp.arange(n_assign, dtype=jnp.int32) - start[e_sorted]
    n_blocks = -(-n_assign // MOE_BLOCK) + N_EXPERTS
    n_slots = n_blocks * MOE_BLOCK
    slot_tok = jnp.zeros((n_slots,), jnp.int32).at[dest].set(flat_tok[order])
    slot_gate = jnp.zeros((n_slots,), F32).at[dest].set(flat_g[order])
    block_e = jnp.minimum(jnp.sum(jnp.arange(n_blocks)[:, None] * MOE_BLOCK >= jnp.cumsum(padded)[None, :], axis=1), N_EXPERTS - 1)
    xb = hf[slot_tok].reshape(n_blocks, MOE_BLOCK, d)

    def expert_block(args):
        xblk, e = args
        return (jax.nn.silu(xblk @ w1[e]) * (xblk @ w3[e])) @ w2[e]

    yb = lax.map(expert_block, (xb, block_e)).reshape(n_slots, d)
    out = jnp.zeros_like(hf).at[slot_tok].add(yb * slot_gate[:, None].astype(yb.dtype))
    return out.reshape(bsz, seq, d)


def setup_inputs(seed: int = 0) -> dict:
    key = jax.random.key(seed)
    keys = iter(jax.random.split(key, 80))
    D, NE, NO, E = D_MODEL, N_EVEN, N_ODD, N_EXPERTS

    def nrm(shape, scale):
        return jax.random.normal(next(keys), shape, F32) * scale

    def gain(shape):
        return 1.0 + 0.02 * jax.random.normal(next(keys), shape, F32)

    def unif(shape, lo, hi):
        return jax.random.uniform(next(keys), shape, F32, lo, hi)

    dt_init = jnp.exp(unif((NE, SSM_HEADS), math.log(1e-3), math.log(1e-1)))
    return {
        'x': nrm((BATCH, SEQ, D), 1.0),
        'mem': nrm((BATCH, MEM_LEN, D), 1.0),
        'positions': jnp.arange(SEQ, dtype=jnp.int32)[None, :] + jax.random.randint(next(keys), (BATCH, 1), 0, 1024, dtype=jnp.int32),
        'norm_mix': gain((DEPTH, D)),
        'norm_xattn': gain((DEPTH, D)),
        'norm_mem': gain((DEPTH, D)),
        'norm_ffn': gain((DEPTH, D)),
        'xattn_wq': nrm((DEPTH, D, D_XATTN), D ** -0.5),
        'xattn_wkv': nrm((DEPTH, D, 2 * D_XATTN), D ** -0.5),
        'xattn_wo': nrm((DEPTH, D_XATTN, D), D_XATTN ** -0.5),
        'xattn_q_gain': gain((DEPTH, XATTN_HEAD_DIM)),
        'xattn_k_gain': gain((DEPTH, XATTN_HEAD_DIM)),
        'ev_w_in': nrm((NE, D, D_IN_EVEN), D ** -0.5),
        'ev_conv_w': nrm((NE, D_CONV, CONV_WIDTH), CONV_WIDTH ** -0.5),
        'ev_conv_b': nrm((NE, D_CONV), 0.02),
        'ev_dt_bias': dt_init + jnp.log(-jnp.expm1(-dt_init)),
        'ev_a_log': jnp.log(unif((NE, SSM_HEADS), 1.0, 16.0)),
        'ev_d_skip': gain((NE, SSM_HEADS)),
        'ev_ssm_norm': gain((NE, D_SSM)),
        'ev_q_gain': gain((NE, NSA_HEAD_DIM)),
        'ev_kc_gain': gain((NE, NSA_HEAD_DIM)),
        'ev_ks_gain': gain((NE, NSA_HEAD_DIM)),
        'ev_kw_gain': gain((NE, NSA_HEAD_DIM)),
        'ev_pe_k': nrm((NE, CMP_BLOCK, NSA_HEAD_DIM), 0.02),
        'ev_pe_v': nrm((NE, CMP_BLOCK, NSA_HEAD_DIM), 0.02),
        'ev_cmp_wk1': nrm((NE, CMP_BLOCK * NSA_HEAD_DIM, CMP_HIDDEN), (CMP_BLOCK * NSA_HEAD_DIM) ** -0.5),
        'ev_cmp_wk2': nrm((NE, CMP_HIDDEN, NSA_HEAD_DIM), CMP_HIDDEN ** -0.5),
        'ev_cmp_wv1': nrm((NE, CMP_BLOCK * NSA_HEAD_DIM, CMP_HIDDEN), (CMP_BLOCK * NSA_HEAD_DIM) ** -0.5),
        'ev_cmp_wv2': nrm((NE, CMP_HIDDEN, NSA_HEAD_DIM), CMP_HIDDEN ** -0.5),
        'ev_w_out': nrm((NE, D_MIX_EVEN, D), D_MIX_EVEN ** -0.5),
        'ev_ffn_w1': nrm((NE, D, D_FF_DENSE), D ** -0.5),
        'ev_ffn_w3': nrm((NE, D, D_FF_DENSE), D ** -0.5),
        'ev_ffn_w2': nrm((NE, D_FF_DENSE, D), D_FF_DENSE ** -0.5),
        'od_mu': unif((NO, N_TOKEN_MIX, D), 0.0, 1.0),
        'od_w_r': nrm((NO, D, D), D ** -0.5),
        'od_w_k': nrm((NO, D, D), D ** -0.5),
        'od_w_v': nrm((NO, D, D), D ** -0.5),
        'od_w_o': nrm((NO, D, D), D ** -0.5),
        'od_w0': unif((NO, D), -5.5, -0.5),
        'od_w1': nrm((NO, D, DECAY_LORA), D ** -0.5),
        'od_w2': nrm((NO, DECAY_LORA, D), 0.5 * DECAY_LORA ** -0.5),
        'od_a0': nrm((NO, D), 0.1),
        'od_a1': nrm((NO, D, AAA_LORA), D ** -0.5),
        'od_a2': nrm((NO, AAA_LORA, D), 0.5 * AAA_LORA ** -0.5),
        'od_g1': nrm((NO, D, GATE_LORA), D ** -0.5),
        'od_g2': nrm((NO, GATE_LORA, D), GATE_LORA ** -0.5),
        'od_k_k': 0.85 * gain((NO, D)),
        'od_k_a': gain((NO, D)),
        'od_r_k': nrm((NO, RWKV_HEADS, RWKV_HEAD_DIM), 0.1),
        'od_ln_w': gain((NO, D)),
        'od_ln_b': nrm((NO, D), 0.02),
        'od_router': nrm((NO, D, E), D ** -0.5),
        'od_moe_w1': nrm((NO, E, D, D_FF_EXPERT), D ** -0.5),
        'od_moe_w3': nrm((NO, E, D, D_FF_EXPERT), D ** -0.5),
        'od_moe_w2': nrm((NO, E, D_FF_EXPERT, D), D_FF_EXPERT ** -0.5),
    }


def reference(x, mem, positions, norm_mix, norm_xattn, norm_mem, norm_ffn,
              xattn_wq, xattn_wkv, xattn_wo, xattn_q_gain, xattn_k_gain,
              ev_w_in, ev_conv_w, ev_conv_b, ev_dt_bias, ev_a_log, ev_d_skip, ev_ssm_norm,
              ev_q_gain, ev_kc_gain, ev_ks_gain, ev_kw_gain,
              ev_pe_k, ev_pe_v, ev_cmp_wk1, ev_cmp_wk2, ev_cmp_wv1, ev_cmp_wv2,
              ev_w_out, ev_ffn_w1, ev_ffn_w3, ev_ffn_w2,
              od_mu, od_w_r, od_w_k, od_w_v, od_w_o, od_w0, od_w1, od_w2,
              od_a0, od_a1, od_a2, od_g1, od_g2, od_k_k, od_k_a, od_r_k, od_ln_w, od_ln_b,
              od_router, od_moe_w1, od_moe_w3, od_moe_w2):
    h = x
    for layer in range(DEPTH):
        i = layer // 2
        hn = rms_norm(h, norm_mix[layer])
        if layer % 2 == 0:
            h = h + even_mixer(hn, positions, ev_w_in[i], ev_conv_w[i], ev_conv_b[i], ev_dt_bias[i],
                               ev_a_log[i], ev_d_skip[i], ev_ssm_norm[i], ev_q_gain[i], ev_kc_gain[i],
                               ev_ks_gain[i], ev_kw_gain[i], ev_pe_k[i], ev_pe_v[i], ev_cmp_wk1[i],
                               ev_cmp_wk2[i], ev_cmp_wv1[i], ev_cmp_wv2[i], ev_w_out[i])
        else:
            h = h + rwkv7_time_mix(hn, od_mu[i], od_w_r[i], od_w_k[i], od_w_v[i], od_w_o[i], od_w0[i],
                                   od_w1[i], od_w2[i], od_a0[i], od_a1[i], od_a2[i], od_g1[i], od_g2[i],
                                   od_k_k[i], od_k_a[i], od_r_k[i], od_ln_w[i], od_ln_b[i])
        h = h + memory_cross_attention(rms_norm(h, norm_xattn[layer]), rms_norm(mem, norm_mem[layer]),
                                       xattn_wq[layer], xattn_wkv[layer], xattn_wo[layer],
                                       xattn_q_gain[layer], xattn_k_gain[layer])
        hn = rms_norm(h, norm_ffn[layer])
        if layer % 2 == 0:
            h = h + swiglu(hn, ev_ffn_w1[i], ev_ffn_w3[i], ev_ffn_w2[i])
        else:
            h = h + moe_swiglu(hn, od_router[i], od_moe_w1[i], od_moe_w3[i], od_moe_w2[i])
    return h
```

```python
import functools
import math

import numpy as np
import jax
import jax.numpy as jnp
from jax import lax
from jax.experimental import pallas as pl
from jax.experimental.pallas import tpu as pltpu

F32 = jnp.float32
BF16 = jnp.bfloat16
I32 = jnp.int32

D_MODEL = 2048
NORM_EPS = 1e-6
NEG_BIG = -1e30

SSM_HEADS = 32
SSM_HEAD_DIM = 64
D_SSM = SSM_HEADS * SSM_HEAD_DIM
SSM_GROUPS = 4
D_STATE = 128
CONV_WIDTH = 4
SSD_CHUNK = 128
D_CONV = D_SSM + 2 * SSM_GROUPS * D_STATE
D_SSM_GROUP = D_SSM // SSM_GROUPS

NSA_HEADS = 16
NSA_KV_HEADS = 4
NSA_HEAD_DIM = 128
NSA_Q_PER_KV = NSA_HEADS // NSA_KV_HEADS
D_NSA = NSA_HEADS * NSA_HEAD_DIM
D_NSA_KV = NSA_KV_HEADS * NSA_HEAD_DIM
CMP_BLOCK = 32
CMP_STRIDE = 16
CMP_HIDDEN = 256
SEL_BLOCK = 64
N_SELECT = 16
WINDOW = 512
ROPE_DIM = NSA_HEAD_DIM // 4
ROPE_THETA = 500000.0

RWKV_HEAD_DIM = 64
RWKV_HEADS = D_MODEL // RWKV_HEAD_DIM
RWKV_GN_EPS = 1e-5 * RWKV_HEAD_DIM

XATTN_HEADS = 4
XATTN_HEAD_DIM = 128
D_XATTN = XATTN_HEADS * XATTN_HEAD_DIM

N_EXPERTS = 8
TOP_K = 2

LANES = 128
SUBLANES = 8
VMEM_LIMIT_BYTES = 56 * 1024 * 1024

ROW_TILE = 512
MM_TM = 1024
MM_TN = 512
FFN_TM = 512
FFN_TF = 512
MOE_FFN_ROWS = 1024
MOE_TF = 256
COMBINE_TM = 256
MOE_TILE = 256
ATT_TQ = 256
ATT_TK = 512
CMP_TQ = 128
WKV_T = 32


def _params(*sem):
    return pltpu.CompilerParams(dimension_semantics=sem, vmem_limit_bytes=VMEM_LIMIT_BYTES)


def _sigmoid(x):
    return 1.0 / (1.0 + jnp.exp(-x))


def _silu(x):
    return x * _sigmoid(x)


def _softplus(x):
    return jnp.maximum(x, 0.0) + jnp.log(1.0 + jnp.exp(-jnp.abs(x)))


def _split3(x):
    h = x.astype(BF16)
    r = x - h.astype(F32)
    m = r.astype(BF16)
    l = (r - m.astype(F32)).astype(BF16)
    return h, m, l


def _dot(a, b):
    return jnp.dot(a, b, preferred_element_type=F32)


def _dot_nt(a, b):
    return lax.dot_general(a, b, (((1,), (1,)), ((), ())), preferred_element_type=F32)


def _dot_f32_sel(x, e):
    h, m, l = _split3(x)
    return _dot(h, e) + _dot(m, e) + _dot(l, e)


def _sel_dot_f32(e, x):
    h, m, l = _split3(x)
    return _dot(e, h) + _dot(e, m) + _dot(e, l)


def _rms(x, eps=NORM_EPS):
    return x * lax.rsqrt(jnp.mean(x * x, axis=-1, keepdims=True) + eps)


def _rmsnorm_kernel(x_ref, g_ref, o_ref):
    o_ref[...] = (_rms(x_ref[...]) * g_ref[...]).astype(o_ref.dtype)


def rmsnorm(x, gain, out_dtype):
    m, d = x.shape
    tm = min(ROW_TILE, m)
    return pl.pallas_call(
        _rmsnorm_kernel,
        grid=(m // tm,),
        in_specs=[pl.BlockSpec((tm, d), lambda i: (i, 0)), pl.BlockSpec((1, d), lambda i: (0, 0))],
        out_specs=pl.BlockSpec((tm, d), lambda i: (i, 0)),
        out_shape=jax.ShapeDtypeStruct((m, d), out_dtype),
        compiler_params=_params("parallel"),
        name="rmsnorm",
    )(x, gain.reshape(1, d))


def _mm_kernel(*refs, n_pairs, has_bias, has_res, act):
    o_ref = refs[-1]
    acc = None
    for p in range(n_pairs):
        a = refs[2 * p][...].astype(BF16)
        d = _dot(a, refs[2 * p + 1][...])
        acc = d if acc is None else acc + d
    k = 2 * n_pairs
    if has_bias:
        acc = acc + refs[k][...]
        k += 1
    if act == "tanh":
        acc = jnp.tanh(acc)
    elif act == "sigmoid":
        acc = _sigmoid(acc)
    if has_res:
        acc = acc + refs[k][...]
    o_ref[...] = acc.astype(o_ref.dtype)


def matmul(pairs, bias=None, residual=None, act=None, out_dtype=F32):
    m = pairs[0][0].shape[0]
    n = pairs[0][1].shape[1]
    kmax = max(a.shape[1] for a, _ in pairs)
    tm = min(MM_TM if kmax <= 2048 else MM_TM // 2, m)
    tn = MM_TN if n % MM_TN == 0 else (256 if n % 256 == 0 else LANES)
    tn = min(tn, n)
    in_specs, args = [], []
    for a, w in pairs:
        kk = a.shape[1]
        in_specs += [pl.BlockSpec((tm, kk), lambda j, i: (i, 0)), pl.BlockSpec((kk, tn), lambda j, i: (0, j))]
        args += [a, w]
    if bias is not None:
        in_specs.append(pl.BlockSpec((1, tn), lambda j, i: (0, j)))
        args.append(bias.reshape(1, n).astype(F32))
    if residual is not None:
        in_specs.append(pl.BlockSpec((tm, tn), lambda j, i: (i, j)))
        args.append(residual)
    kern = functools.partial(_mm_kernel, n_pairs=len(pairs), has_bias=bias is not None,
                             has_res=residual is not None, act=act)
    return pl.pallas_call(
        kern,
        grid=(n // tn, m // tm),
        in_specs=in_specs,
        out_specs=pl.BlockSpec((tm, tn), lambda j, i: (i, j)),
        out_shape=jax.ShapeDtypeStruct((m, n), out_dtype),
        compiler_params=_params("parallel", "parallel"),
        name="matmul",
    )(*args)


def _ffn_kernel(x_ref, w1_ref, w3_ref, w2_ref, res_ref, o_ref):
    f = pl.program_id(1)

    @pl.when(f == 0)
    def _():
        o_ref[...] = res_ref[...]

    x = x_ref[...]
    h = _silu(_dot(x, w1_ref[...])) * _dot(x, w3_ref[...])
    o_ref[...] += _dot(h.astype(BF16), w2_ref[...])


def swiglu_ffn(x, w1, w3, w2, residual):
    m, d = x.shape
    ff = w1.shape[1]
    tm = min(FFN_TM, m)
    tf = FFN_TF
    return pl.pallas_call(
        _ffn_kernel,
        grid=(m // tm, ff // tf),
        in_specs=[pl.BlockSpec((tm, d), lambda i, f: (i, 0)),
                  pl.BlockSpec((d, tf), lambda i, f: (0, f)),
                  pl.BlockSpec((d, tf), lambda i, f: (0, f)),
                  pl.BlockSpec((tf, d), lambda i, f: (f, 0)),
                  pl.BlockSpec((tm, d), lambda i, f: (i, 0))],
        out_specs=pl.BlockSpec((tm, d), lambda i, f: (i, 0)),
        out_shape=jax.ShapeDtypeStruct((m, d), F32),
        compiler_params=_params("parallel", "arbitrary"),
        name="swiglu_ffn",
    )(x, w1, w3, w2, residual)


def _ssd_kernel(z_ref, xbc_ref, dt_ref, cw_ref, cb_ref, dtb_ref, alog_ref, dskip_ref, nw_ref, eh_ref,
                o_ref, ext_ref, st_ref):
    L = SSD_CHUNK
    G = SSM_GROUPS
    GW = D_SSM_GROUP
    c = pl.program_id(1)

    @pl.when(c == 0)
    def _():
        ext_ref[L:L + SUBLANES, :] = jnp.zeros((SUBLANES, D_CONV), F32)
        st_ref[...] = jnp.zeros_like(st_ref)

    ext_ref[0:SUBLANES, :] = ext_ref[L:L + SUBLANES, :]
    ext_ref[SUBLANES:L + SUBLANES, :] = xbc_ref[...]
    acc = jnp.zeros((L, D_CONV), F32) + cb_ref[...]
    for k in range(CONV_WIDTH):
        off = SUBLANES - (CONV_WIDTH - 1) + k
        acc = acc + cw_ref[k:k + 1, :] * ext_ref[off:off + L, :]
    xc = _silu(acc)

    dt = _softplus(dt_ref[...] + dtb_ref[...])
    a = -jnp.exp(alog_ref[...])
    adt = dt * a
    row = lax.broadcasted_iota(I32, (L, L), 0)
    col = lax.broadcasted_iota(I32, (L, L), 1)
    causal = row >= col
    tril = jnp.where(causal, 1.0, 0.0).astype(BF16)
    a_cum = _sel_dot_f32(tril, adt)
    a_cum_t = a_cum.T
    eh = eh_ref[...]
    acx = _dot_f32_sel(a_cum, eh)
    dtx = _dot_f32_sel(dt, eh)
    a_end = acx[L - 1:L, :]
    xs = xc[:, :D_SSM]
    xdt = xs * dtx
    xdte = (xdt * jnp.exp(a_end - acx)).astype(BF16)
    from_start = jnp.exp(acx)
    lane = lax.broadcasted_iota(I32, (L, LANES), 1)
    lo_half = lane < SSM_HEAD_DIM
    z = z_ref[...]

    for g in range(G):
        bm = xc[:, D_SSM + g * D_STATE:D_SSM + (g + 1) * D_STATE]
        cm = xc[:, D_SSM + G * D_STATE + g * D_STATE:D_SSM + G * D_STATE + (g + 1) * D_STATE]
        bm_b = bm.astype(BF16)
        cm_b = cm.astype(BF16)
        cb = _dot_nt(cm_b, bm_b)
        st = st_ref[g]
        y_off = _dot(cm_b, st.astype(BF16)) * from_start[:, g * GW:(g + 1) * GW]
        pieces = []
        for pp in range(GW // LANES):
            h0 = g * (GW // SSM_HEAD_DIM) + 2 * pp
            ms = []
            for hh in (h0, h0 + 1):
                diff = a_cum[:, hh:hh + 1] - a_cum_t[hh:hh + 1, :]
                dec = jnp.exp(jnp.where(causal, diff, NEG_BIG))
                ms.append((cb * dec).astype(BF16))
            lhs = jnp.concatenate(ms, axis=1)
            c0 = g * GW + pp * LANES
            xp = xdt[:, c0:c0 + LANES]
            rhs = jnp.concatenate([jnp.where(lo_half, xp, 0.0), jnp.where(lo_half, 0.0, xp)], axis=0)
            pieces.append(_dot(lhs, rhs.astype(BF16)))
        y = jnp.concatenate(pieces, axis=1) + y_off
        y = y + xs[:, g * GW:(g + 1) * GW] * dskip_ref[:, g * GW:(g + 1) * GW]
        y = y * _silu(z[:, g * GW:(g + 1) * GW])
        y = _rms(y) * nw_ref[:, g * GW:(g + 1) * GW]
        o_ref[:, g * GW:(g + 1) * GW] = y.astype(o_ref.dtype)
        s_new = _dot(bm.T.astype(BF16), xdte[:, g * GW:(g + 1) * GW])
        st_ref[g] = st * jnp.exp(a_end[:, g * GW:(g + 1) * GW]) + s_new


def ssd_group(z, xbc, small, conv_w, conv_b, dt_bias, a_log, d_skip, norm_w, bsz, seq):
    m = bsz * seq
    L = SSD_CHUNK
    nc = seq // L
    pad = LANES - SSM_HEADS
    eh = np.zeros((LANES, D_SSM), np.float32)
    for h in range(SSM_HEADS):
        eh[h, h * SSM_HEAD_DIM:(h + 1) * SSM_HEAD_DIM] = 1.0
    row_spec = lambda w: pl.BlockSpec((L, w), lambda b, c: (b * nc + c, 0))
    const = lambda r, w: pl.BlockSpec((r, w), lambda b, c: (0, 0))
    return pl.pallas_call(
        _ssd_kernel,
        grid=(bsz, nc),
        in_specs=[row_spec(D_SSM), row_spec(D_CONV), row_spec(LANES),
                  const(CONV_WIDTH, D_CONV), const(1, D_CONV), const(1, LANES), const(1, LANES),
                  const(1, D_SSM), const(1, D_SSM), const(LANES, D_SSM)],
        out_specs=row_spec(D_SSM),
        out_shape=jax.ShapeDtypeStruct((m, D_SSM), BF16),
        scratch_shapes=[pltpu.VMEM((L + SUBLANES, D_CONV), F32),
                        pltpu.VMEM((SSM_GROUPS, D_STATE, D_SSM_GROUP), F32)],
        compiler_params=_params("parallel", "arbitrary"),
        name="ssd_group",
    )(z, xbc, small, conv_w.T, conv_b.reshape(1, D_CONV),
      jnp.pad(dt_bias, (0, pad)).reshape(1, LANES), jnp.pad(a_log, (0, pad)).reshape(1, LANES),
      jnp.repeat(d_skip, SSM_HEAD_DIM).reshape(1, D_SSM), norm_w.reshape(1, D_SSM), jnp.asarray(eh, BF16))


def _rope_norm_kernel(pos_ref, invf_ref, q_ref, kv_ref, qg_ref, ksg_ref, kwg_ref, qo_ref, kso_ref, kwo_ref):
    ang = pos_ref[...] * invf_ref[...]
    cos = jnp.cos(ang)
    sin = jnp.sin(ang)
    lane = lax.broadcasted_iota(I32, ang.shape, 1)
    first = lane < ROPE_DIM // 2
    sin_signed = jnp.where(first, -sin, sin)

    def head(x, gain):
        y = _rms(x) * gain
        partner = jnp.where(first, pltpu.roll(y, LANES - ROPE_DIM // 2, axis=1), pltpu.roll(y, ROPE_DIM // 2, axis=1))
        return y * cos + partner * sin_signed

    hd = NSA_HEAD_DIM
    for h in range(NSA_HEADS):
        qo_ref[:, h * hd:(h + 1) * hd] = head(q_ref[:, h * hd:(h + 1) * hd], qg_ref[...]).astype(qo_ref.dtype)
    for h in range(NSA_KV_HEADS):
        ks = kv_ref[:, 2 * D_NSA_KV + h * hd:2 * D_NSA_KV + (h + 1) * hd]
        kw = kv_ref[:, 4 * D_NSA_KV + h * hd:4 * D_NSA_KV + (h + 1) * hd]
        kso_ref[:, h * hd:(h + 1) * hd] = head(ks, ksg_ref[...]).astype(kso_ref.dtype)
        kwo_ref[:, h * hd:(h + 1) * hd] = head(kw, kwg_ref[...]).astype(kwo_ref.dtype)


def rope_norm(posf, q, kv6, q_gain, ks_gain, kw_gain):
    m = q.shape[0]
    tm = min(ROW_TILE, m)
    half = ROPE_DIM // 2
    inv = np.exp(-math.log(ROPE_THETA) * np.arange(0, ROPE_DIM, 2, dtype=np.float32) / ROPE_DIM).astype(np.float32)
    invf = np.zeros((1, LANES), np.float32)
    invf[0, :half] = inv
    invf[0, half:ROPE_DIM] = inv
    row = lambda w: pl.BlockSpec((tm, w), lambda i: (i, 0))
    const = pl.BlockSpec((1, LANES), lambda i: (0, 0))
    return pl.pallas_call(
        _rope_norm_kernel,
        grid=(m // tm,),
        in_specs=[row(1), const, row(D_NSA), row(6 * D_NSA_KV), const, const, const],
        out_specs=[row(D_NSA), row(D_NSA_KV), row(D_NSA_KV)],
        out_shape=[jax.ShapeDtypeStruct((m, D_NSA), BF16), jax.ShapeDtypeStruct((m, D_NSA_KV), BF16),
                   jax.ShapeDtypeStruct((m, D_NSA_KV), BF16)],
        compiler_params=_params("parallel"),
        name="rope_norm",
    )(posf, jnp.asarray(invf), q, kv6, q_gain.reshape(1, LANES), ks_gain.reshape(1, LANES), kw_gain.reshape(1, LANES))


def _compress_kernel(uk_ref, uv_ref, pek_ref, pev_ref, wk1_ref, wk2_ref, wv1_ref, wv2_ref, g_ref, kc_ref, vc_ref):
    def mlp(u, pe_ref, w1_ref, w2_ref):
        n = u.shape[0]
        h1 = _dot((u + pe_ref[0:1, :]).astype(BF16), w1_ref[0])
        h2 = _dot((u + pe_ref[1:2, :]).astype(BF16), w1_ref[1])
        pre = h1 + pltpu.roll(h2, n - 1, axis=0)
        return _dot(_silu(pre).astype(BF16), w2_ref[...])

    kc = mlp(uk_ref[0, 0], pek_ref, wk1_ref, wk2_ref)
    kc_ref[0, 0] = _rms(kc) * g_ref[...]
    vc_ref[0, 0] = mlp(uv_ref[0, 0], pev_ref, wv1_ref, wv2_ref)


def compress(uk, uv, pe_k, pe_v, wk1, wk2, wv1, wv2, kc_gain):
    bsz, nkv, nch, width = uk.shape
    half = CMP_STRIDE * NSA_HEAD_DIM
    u_spec = pl.BlockSpec((1, 1, nch, width), lambda b, g: (b, g, 0, 0))
    o_spec = pl.BlockSpec((1, 1, nch, NSA_HEAD_DIM), lambda b, g: (b, g, 0, 0))
    c2 = lambda s: pl.BlockSpec(s, lambda b, g: (0, 0))
    c3 = lambda s: pl.BlockSpec(s, lambda b, g: (0, 0, 0))
    o_shape = jax.ShapeDtypeStruct((bsz, nkv, nch, NSA_HEAD_DIM), F32)
    return pl.pallas_call(
        _compress_kernel,
        grid=(bsz, nkv),
        in_specs=[u_spec, u_spec, c2((2, half)), c2((2, half)),
                  c3((2, half, CMP_HIDDEN)), c2((CMP_HIDDEN, NSA_HEAD_DIM)),
                  c3((2, half, CMP_HIDDEN)), c2((CMP_HIDDEN, NSA_HEAD_DIM)), c2((1, NSA_HEAD_DIM))],
        out_specs=[o_spec, o_spec],
        out_shape=[o_shape, o_shape],
        compiler_params=_params("parallel", "parallel"),
        name="nsa_compress",
    )(uk, uv, pe_k.reshape(2, half), pe_v.reshape(2, half),
      wk1.reshape(2, half, CMP_HIDDEN).astype(BF16), wk2.astype(BF16),
      wv1.reshape(2, half, CMP_HIDDEN).astype(BF16), wv2.astype(BF16), kc_gain.reshape(1, NSA_HEAD_DIM))


def _stack_heads(q):
    return jnp.concatenate([q[:, r * NSA_HEAD_DIM:(r + 1) * NSA_HEAD_DIM] for r in range(NSA_Q_PER_KV)], axis=0)


def _unstack_heads(o, tq):
    return jnp.concatenate([o[r * tq:(r + 1) * tq, :] for r in range(NSA_Q_PER_KV)], axis=1)


def _cmp_select_kernel(q_ref, kc_ref, vc_ref, ovt_ref, o_ref, sel_ref, *, tq, n_blk):
    R = NSA_Q_PER_KV
    scale = NSA_HEAD_DIM ** -0.5
    q0 = pl.program_id(2) * tq
    q4 = _stack_heads(q_ref[...])
    kc = kc_ref[0, 0].astype(BF16)
    ncmp = kc.shape[0]
    s = _dot_nt(q4, kc) * scale
    t_row = q0 + lax.broadcasted_iota(I32, (tq, ncmp), 0)
    c_col = lax.broadcasted_iota(I32, (tq, ncmp), 1)
    visf = jnp.where((c_col * CMP_STRIDE + CMP_BLOCK - 1) <= t_row, 1.0, 0.0)
    vis = jnp.concatenate([visf] * R, axis=0) > 0.5
    s = jnp.where(vis, s, NEG_BIG)
    mx = jnp.max(s, axis=-1, keepdims=True)
    e = jnp.where(vis, jnp.exp(s - mx), 0.0)
    den = jnp.sum(e, axis=-1, keepdims=True)
    p = e / jnp.maximum(den, 1e-30)
    o = _dot(p.astype(BF16), vc_ref[0, 0].astype(BF16))
    o_ref[...] = _unstack_heads(o, tq)

    psum = p[0:tq]
    for r in range(1, R):
        psum = psum + p[r * tq:(r + 1) * tq]
    h, m, l = _split3(psum)
    ovt = ovt_ref[...]
    imp = _dot_nt(ovt, h) + _dot_nt(ovt, m) + _dot_nt(ovt, l)
    jdx = lax.broadcasted_iota(I32, (n_blk, tq), 0)
    t_lane = q0 + lax.broadcasted_iota(I32, (n_blk, tq), 1)
    cur = t_lane // SEL_BLOCK
    causal = jdx <= cur
    forced = ((jdx == 0) | (jdx >= cur - 1)) & causal
    score = jnp.where(forced, jnp.inf, jnp.where(causal, imp, -jnp.inf))
    rank = jnp.zeros((n_blk, tq), F32)
    for jp in range(n_blk):
        rowv = score[jp:jp + 1, :]
        beats = (rowv > score) | ((rowv == score) & (jdx > jp))
        rank = rank + jnp.where(beats, 1.0, 0.0)
    sel = jnp.where((rank < float(min(N_SELECT, n_blk))) & causal, 1.0, 0.0)
    if n_blk < LANES:
        sel = jnp.concatenate([sel, jnp.zeros((LANES - n_blk, tq), F32)], axis=0)
    sel_ref[...] = sel.T.astype(sel_ref.dtype)


def cmp_select(qn, kc, vc, bsz, seq):
    m = bsz * seq
    G = NSA_KV_HEADS
    tq = CMP_TQ
    nq = seq // tq
    ncmp = kc.shape[2]
    n_blk = seq // SEL_BLOCK
    assert n_blk <= LANES and tq == LANES
    c0 = np.arange(ncmp)[None, :] * CMP_STRIDE
    s0 = np.arange(n_blk)[:, None] * SEL_BLOCK
    ov = np.clip(np.minimum(c0 + CMP_BLOCK, s0 + SEL_BLOCK) - np.maximum(c0, s0), 0, None) / CMP_STRIDE
    n_cmp_valid = (seq - CMP_BLOCK) // CMP_STRIDE + 1
    ov[:, n_cmp_valid:] = 0.0
    kv_spec = pl.BlockSpec((1, 1, ncmp, NSA_HEAD_DIM), lambda b, g, i: (b, g, 0, 0))
    kern = functools.partial(_cmp_select_kernel, tq=tq, n_blk=n_blk)
    return pl.pallas_call(
        kern,
        grid=(bsz, G, nq),
        in_specs=[pl.BlockSpec((tq, NSA_Q_PER_KV * NSA_HEAD_DIM), lambda b, g, i: (b * nq + i, g)),
                  kv_spec, kv_spec, pl.BlockSpec((n_blk, ncmp), lambda b, g, i: (0, 0))],
        out_specs=[pl.BlockSpec((tq, NSA_Q_PER_KV * NSA_HEAD_DIM), lambda b, g, i: (b * nq + i, g)),
                   pl.BlockSpec((tq, LANES), lambda b, g, i: ((b * G + g) * nq + i, 0))],
        out_shape=[jax.ShapeDtypeStruct((m, D_NSA), F32), jax.ShapeDtypeStruct((bsz * G * seq, LANES), BF16)],
        compiler_params=_params("parallel", "parallel", "parallel"),
        name="nsa_cmp_select",
    )(qn, kc, vc, jnp.asarray(ov, BF16))


def _masked_attn_kernel(q_ref, k_ref, v_ref, *rest, mode, tq, tk):
    if mode == "sel":
        sel_ref, e_ref, o_ref, m_ref, l_ref, acc_ref = rest
    else:
        o_ref, m_ref, l_ref, acc_ref = rest
    R = NSA_Q_PER_KV
    scale = NSA_HEAD_DIM ** -0.5
    i = pl.program_id(2)
    j = pl.program_id(3)
    q0 = i * tq
    kt = j if mode == "sel" else q0 // tk - 1 + j

    @pl.when(j == 0)
    def _():
        m_ref[...] = jnp.full(m_ref.shape, NEG_BIG, F32)
        l_ref[...] = jnp.zeros(l_ref.shape, F32)
        acc_ref[...] = jnp.zeros(acc_ref.shape, F32)

    @pl.when((kt >= 0) & (kt * tk <= q0 + tq - 1))
    def _():
        q4 = _stack_heads(q_ref[...])
        s = _dot_nt(q4, k_ref[...].astype(BF16)) * scale
        t_row = q0 + lax.broadcasted_iota(I32, (tq, tk), 0)
        key = kt * tk + lax.broadcasted_iota(I32, (tq, tk), 1)
        if mode == "sel":
            picked = _dot(sel_ref[...], e_ref[...])
            maskf = jnp.where(key <= t_row, picked, 0.0)
        else:
            maskf = jnp.where((key <= t_row) & (key > t_row - WINDOW), 1.0, 0.0)
        mask = jnp.concatenate([maskf] * R, axis=0) > 0.5
        s = jnp.where(mask, s, NEG_BIG)
        m_prev = m_ref[...]
        m_new = jnp.maximum(m_prev, jnp.max(s, axis=-1, keepdims=True))
        p = jnp.where(mask, jnp.exp(s - m_new), 0.0)
        alpha = jnp.exp(m_prev - m_new)
        l_ref[...] = alpha * l_ref[...] + jnp.sum(p, axis=-1, keepdims=True)
        m_ref[...] = m_new
        pv = _dot(p.astype(BF16), v_ref[...].astype(BF16))
        acc_ref[...] = alpha * acc_ref[...] + pv

    @pl.when(j == pl.num_programs(3) - 1)
    def _():
        o_ref[...] = _unstack_heads(acc_ref[...] / l_ref[...], tq)


def masked_attention(qn, k, k_col0, v, v_col0, bsz, seq, mode, sel=None):
    m = bsz * seq
    G = NSA_KV_HEADS
    tq = min(ATT_TQ, seq)
    tk = min(ATT_TK, seq)
    nq, nk = seq // tq, seq // tk
    hd = NSA_HEAD_DIM
    if mode == "sel":
        nj = nk
        kt_of = lambda i, j: jnp.minimum(j, (i * tq + tq - 1) // tk)
    else:
        nj = WINDOW // tk + 1
        kt_of = lambda i, j: jnp.maximum((i * tq) // tk - 1 + j, 0)
    q_spec = pl.BlockSpec((tq, NSA_Q_PER_KV * hd), lambda b, g, i, j: (b * nq + i, g))
    in_specs = [q_spec,
                pl.BlockSpec((tk, hd), lambda b, g, i, j: (b * nk + kt_of(i, j), k_col0 + g)),
                pl.BlockSpec((tk, hd), lambda b, g, i, j: (b * nk + kt_of(i, j), v_col0 + g))]
    args = [qn, k, v]
    if mode == "sel":
        n_blk = seq // SEL_BLOCK
        e = np.zeros((LANES, seq), np.float32)
        e[np.arange(seq) // SEL_BLOCK, np.arange(seq)] = 1.0
        in_specs += [pl.BlockSpec((tq, LANES), lambda b, g, i, j: ((b * G + g) * nq + i, 0)),
                     pl.BlockSpec((LANES, tk), lambda b, g, i, j: (0, kt_of(i, j)))]
        args += [sel, jnp.asarray(e, BF16)]
    kern = functools.partial(_masked_attn_kernel, mode=mode, tq=tq, tk=tk)
    rows = NSA_Q_PER_KV * tq
    return pl.pallas_call(
        kern,
        grid=(bsz, G, nq, nj),
        in_specs=in_specs,
        out_specs=q_spec,
        out_shape=jax.ShapeDtypeStruct((m, D_NSA), F32),
        scratch_shapes=[pltpu.VMEM((rows, 1), F32), pltpu.VMEM((rows, 1), F32), pltpu.VMEM((rows, hd), F32)],
        compiler_params=_params("parallel", "parallel", "parallel", "arbitrary"),
        name="nsa_attn_" + mode,
    )(*args)


def _nsa_combine_kernel(g_ref, e_ref, oc_ref, os_ref, ow_ref, o_ref):
    gate = _sigmoid(g_ref[...])
    gx = _dot_f32_sel(gate, e_ref[...])
    y = gx[:, :D_NSA] * oc_ref[...] + gx[:, D_NSA:2 * D_NSA] * os_ref[...] + gx[:, 2 * D_NSA:] * ow_ref[...]
    o_ref[...] = y.astype(o_ref.dtype)


def nsa_combine(small, o_cmp, o_sel, o_win):
    m = small.shape[0]
    tm = min(COMBINE_TM, m)
    e = np.zeros((LANES, 3 * D_NSA), np.float32)
    for h in range(NSA_HEADS):
        for br in range(3):
            e[SSM_HEADS + 3 * h + br, br * D_NSA + h * NSA_HEAD_DIM:br * D_NSA + (h + 1) * NSA_HEAD_DIM] = 1.0
    row = lambda w: pl.BlockSpec((tm, w), lambda i: (i, 0))
    return pl.pallas_call(
        _nsa_combine_kernel,
        grid=(m // tm,),
        in_specs=[row(LANES), pl.BlockSpec((LANES, 3 * D_NSA), lambda i: (0, 0)), row(D_NSA), row(D_NSA), row(D_NSA)],
        out_specs=row(D_NSA),
        out_shape=jax.ShapeDtypeStruct((m, D_NSA), BF16),
        compiler_params=_params("parallel"),
        name="nsa_combine",
    )(small, jnp.asarray(e, BF16), o_cmp, o_sel, o_win)


def _xattn_kernel(q_ref, kv_ref, qg_ref, kg_ref, o_ref):
    hd = XATTN_HEAD_DIM
    scale = hd ** -0.5
    outs = []
    for h in range(XATTN_HEADS):
        q = (_rms(q_ref[:, h * hd:(h + 1) * hd]) * qg_ref[...]).astype(BF16)
        k = (_rms(kv_ref[:, h * hd:(h + 1) * hd]) * kg_ref[...]).astype(BF16)
        v = kv_ref[:, D_XATTN + h * hd:D_XATTN + (h + 1) * hd].astype(BF16)
        s = _dot_nt(q, k) * scale
        e = jnp.exp(s - jnp.max(s, axis=-1, keepdims=True))
        p = e / jnp.sum(e, axis=-1, keepdims=True)
        outs.append(_dot(p.astype(BF16), v))
    o_ref[...] = jnp.concatenate(outs, axis=1).astype(o_ref.dtype)


def xattn_core(q, kv, q_gain, k_gain, bsz, seq):
    m = bsz * seq
    n_mem = kv.shape[0] // bsz
    tm = min(ROW_TILE, seq)
    nt = seq // tm
    return pl.pallas_call(
        _xattn_kernel,
        grid=(bsz, nt),
        in_specs=[pl.BlockSpec((tm, D_XATTN), lambda b, i: (b * nt + i, 0)),
                  pl.BlockSpec((n_mem, 2 * D_XATTN), lambda b, i: (b, 0)),
                  pl.BlockSpec((1, XATTN_HEAD_DIM), lambda b, i: (0, 0)),
                  pl.BlockSpec((1, XATTN_HEAD_DIM), lambda b, i: (0, 0))],
        out_specs=pl.BlockSpec((tm, D_XATTN), lambda b, i: (b * nt + i, 0)),
        out_shape=jax.ShapeDtypeStruct((m, D_XATTN), BF16),
        compiler_params=_params("parallel", "parallel"),
        name="xattn_core",
    )(q, kv, q_gain.reshape(1, XATTN_HEAD_DIM), k_gain.reshape(1, XATTN_HEAD_DIM))


def memory_xattn(h, mem2d, norm_x, norm_m, wq, wkv, wo, q_gain, k_gain, bsz, seq):
    hq = rmsnorm(h, norm_x, BF16)
    q = matmul([(hq, wq.astype(BF16))])
    mem_n = rmsnorm(mem2d, norm_m, BF16)
    kv = matmul([(mem_n, wkv.astype(BF16))])
    o = xattn_core(q, kv, q_gain, k_gain, bsz, seq)
    return matmul([(o, wo.astype(BF16))], residual=h)


def _mix_kernel(h_ref, hp_ref, g_ref, mu_ref, *rest, tm, tiles_per_seq):
    outs, ext_ref = rest[:-1], rest[-1]
    i = pl.program_id(0)
    hn = _rms(h_ref[...]) * g_ref[...]
    hp = _rms(hp_ref[...]) * g_ref[...]
    hp = jnp.where(i % tiles_per_seq == 0, 0.0, hp)
    ext_ref[0:SUBLANES, :] = hp
    ext_ref[SUBLANES:tm + SUBLANES, :] = hn
    xx = ext_ref[SUBLANES - 1:tm + SUBLANES - 1, :] - hn
    for j, o_ref in enumerate(outs):
        o_ref[...] = (hn + xx * mu_ref[j:j + 1, :]).astype(o_ref.dtype)


def rwkv_mix(h, gain, mu, seq):
    m, d = h.shape
    tm = min(ROW_TILE, seq)
    n_mix = mu.shape[0]
    kern = functools.partial(_mix_kernel, tm=tm, tiles_per_seq=seq // tm)
    row = pl.BlockSpec((tm, d), lambda i: (i, 0))
    return pl.pallas_call(
        kern,
        grid=(m // tm,),
        in_specs=[row, pl.BlockSpec((SUBLANES, d), lambda i: (jnp.maximum(i * (tm // SUBLANES) - 1, 0), 0)),
                  pl.BlockSpec((1, d), lambda i: (0, 0)), pl.BlockSpec((n_mix, d), lambda i: (0, 0))],
        out_specs=[row] * n_mix,
        out_shape=[jax.ShapeDtypeStruct((m, d), BF16)] * n_mix,
        scratch_shapes=[pltpu.VMEM((tm + SUBLANES, d), F32)],
        compiler_params=_params("parallel"),
        name="rwkv_mix",
    )(h, h, gain.reshape(1, d), mu)


def _head_sums(x, bd):
    return jnp.concatenate([_dot_f32_sel(x[:, c * LANES:(c + 1) * LANES], bd) for c in range(x.shape[1] // LANES)], axis=1)


def _rwkv_prep_kernel(k_ref, w_ref, a_ref, kk_ref, ka_ref, bd_ref, wd_ref, kh_ref, kkn_ref, b_ref):
    k = k_ref[...]
    a = a_ref[...]
    w = -_softplus(-w_ref[...]) - 0.5
    wd_ref[...] = jnp.exp(-jnp.exp(w))
    kk = k * kk_ref[...]
    ss = _head_sums(kk * kk, bd_ref[...])
    kk = kk / jnp.maximum(jnp.sqrt(ss), 1e-12)
    kkn_ref[...] = kk
    b_ref[...] = kk * a
    kh_ref[...] = k * (1.0 + (a - 1.0) * ka_ref[...])


def _head_block_diag():
    bd = np.zeros((LANES, LANES), np.float32)
    for s in range(LANES // RWKV_HEAD_DIM):
        bd[s * RWKV_HEAD_DIM:(s + 1) * RWKV_HEAD_DIM, s * RWKV_HEAD_DIM:(s + 1) * RWKV_HEAD_DIM] = 1.0
    return jnp.asarray(bd, BF16)


def rwkv_prep(k, w_raw, a_sig, k_k, k_a):
    m, d = k.shape
    tm = min(ROW_TILE // 2, m)
    row = pl.BlockSpec((tm, d), lambda i: (i, 0))
    vec = pl.BlockSpec((1, d), lambda i: (0, 0))
    shp = jax.ShapeDtypeStruct((m, d), F32)
    return pl.pallas_call(
        _rwkv_prep_kernel,
        grid=(m // tm,),
        in_specs=[row, row, row, vec, vec, pl.BlockSpec((LANES, LANES), lambda i: (0, 0))],
        out_specs=[row] * 4,
        out_shape=[shp] * 4,
        compiler_params=_params("parallel"),
        name="rwkv_prep",
    )(k, w_raw, a_sig, k_k.reshape(1, d), k_a.reshape(1, d), _head_block_diag())


def _wkv_kernel(r_ref, w_ref, k_ref, v_ref, kk_ref, b_ref, y_ref, s_ref, *, steps):
    n = RWKV_HEAD_DIM

    @pl.when(pl.program_id(0) == 0)
    def _():
        s_ref[...] = jnp.zeros_like(s_ref)

    lanes = s_ref.shape[-1]
    q0 = jnp.zeros((n, lanes), F32)
    for j in range(n):
        q0 = q0 + s_ref[j] * kk_ref[0, j:j + 1, :]

    def step(t, q):
        tn = jnp.minimum(t + 1, steps - 1)
        v = v_ref[t]
        y = jnp.zeros((n, lanes), F32)
        qn = jnp.zeros((n, lanes), F32)
        for j in range(n):
            s_new = s_ref[j] * w_ref[t, j:j + 1, :] - q * b_ref[t, j:j + 1, :] + v * k_ref[t, j:j + 1, :]
            s_ref[j] = s_new
            y = y + s_new * r_ref[t, j:j + 1, :]
            qn = qn + s_new * kk_ref[tn, j:j + 1, :]
        y_ref[t] = y
        return qn

    lax.fori_loop(0, steps, step, q0)


def wkv_scan(r, w, k, v, kk, b):
    seq, n, lanes = r.shape
    steps = min(WKV_T, seq)
    blk = pl.BlockSpec((steps, n, lanes), lambda i: (i, 0, 0))
    return pl.pallas_call(
        functools.partial(_wkv_kernel, steps=steps),
        grid=(seq // steps,),
        in_specs=[blk] * 6,
        out_specs=blk,
        out_shape=jax.ShapeDtypeStruct((seq, n, lanes), F32),
        scratch_shapes=[pltpu.VMEM((n, n, lanes), F32)],
        compiler_params=_params("arbitrary"),
        name="wkv_scan",
    )(r, w, k, v, kk, b)


def _rwkv_post_kernel(y_ref, r_ref, kh_ref, v_ref, g_ref, rk_ref, lw_ref, lb_ref, bd_ref, o_ref):
    bd = bd_ref[...]
    inv_n = 1.0 / RWKV_HEAD_DIM
    y = y_ref[...]
    mean = _head_sums(y, bd) * inv_n
    yc = y - mean
    var = _head_sums(yc * yc, bd) * inv_n
    yn = yc * lax.rsqrt(var + RWKV_GN_EPS) * lw_ref[...] + lb_ref[...]
    v = v_ref[...]
    bonus = _head_sums(r_ref[...] * kh_ref[...] * rk_ref[...], bd) * v
    o_ref[...] = ((yn + bonus) * g_ref[...]).astype(o_ref.dtype)


def rwkv_post(y, r, kh, v, g, r_k, ln_w, ln_b):
    m, d = y.shape
    tm = min(ROW_TILE // 2, m)
    row = pl.BlockSpec((tm, d), lambda i: (i, 0))
    vec = pl.BlockSpec((1, d), lambda i: (0, 0))
    return pl.pallas_call(
        _rwkv_post_kernel,
        grid=(m // tm,),
        in_specs=[row] * 5 + [vec] * 3 + [pl.BlockSpec((LANES, LANES), lambda i: (0, 0))],
        out_specs=row,
        out_shape=jax.ShapeDtypeStruct((m, d), BF16),
        compiler_params=_params("parallel"),
        name="rwkv_post",
    )(y, r, kh, v, g, r_k.reshape(1, d), ln_w.reshape(1, d), ln_b.reshape(1, d), _head_block_diag())


def _router_kernel(h_ref, g_ref, wr_ref, info_ref, tile_cnt_ref, total_ref, carry_ref, *, tt):
    E = N_EXPERTS
    i = pl.program_id(0)

    @pl.when(i == 0)
    def _():
        carry_ref[...] = jnp.zeros_like(carry_ref)

    xh, xm, xl = _split3(_rms(h_ref[...]) * g_ref[...])
    wh, wm, wl = _split3(wr_ref[...])
    logits = (_dot_nt(wh, xh) + _dot_nt(wh, xm) + _dot_nt(wm, xh)
              + _dot_nt(wh, xl) + _dot_nt(wl, xh) + _dot_nt(wm, xm))
    eidx = lax.broadcasted_iota(I32, (E, tt), 0)
    v1 = jnp.max(logits, axis=0, keepdims=True)
    i1 = jnp.min(jnp.where(logits == v1, eidx, E), axis=0, keepdims=True)
    rest = jnp.where(eidx == i1, -jnp.inf, logits)
    v2 = jnp.max(rest, axis=0, keepdims=True)
    i2 = jnp.min(jnp.where(rest == v2, eidx, E), axis=0, keepdims=True)
    e2 = jnp.exp(v2 - v1)
    g1 = 1.0 / (1.0 + e2)
    g2 = e2 / (1.0 + e2)
    hit1 = eidx == i1
    hit2 = eidx == i2
    onehot = jnp.where(hit1 | hit2, 1.0, 0.0)
    rr = lax.broadcasted_iota(I32, (tt, tt), 0)
    cc = lax.broadcasted_iota(I32, (tt, tt), 1)
    before = jnp.where(rr < cc, 1.0, 0.0).astype(BF16)
    carry = carry_ref[...]
    carry_wide = jnp.concatenate([carry] * (tt // LANES), axis=1)
    cum = _dot(onehot.astype(BF16), before) + carry_wide
    rank1 = jnp.sum(jnp.where(hit1, cum, 0.0), axis=0, keepdims=True)
    rank2 = jnp.sum(jnp.where(hit2, cum, 0.0), axis=0, keepdims=True)
    zero = jnp.zeros((1, tt), F32)
    info_ref[...] = jnp.concatenate([i1.astype(F32), i2.astype(F32), rank1, rank2, g1, g2, zero, zero], axis=0)
    tile_cnt_ref[0] = carry
    carry = carry + _dot(onehot.astype(BF16), jnp.ones((tt, LANES), BF16))
    carry_ref[...] = carry
    total_ref[...] = carry


def moe_router(h, norm_gain, w_router):
    m, d = h.shape
    tt = min(MOE_TILE, m)
    nt = m // tt
    return pl.pallas_call(
        functools.partial(_router_kernel, tt=tt),
        grid=(nt,),
        in_specs=[pl.BlockSpec((tt, d), lambda i: (i, 0)), pl.BlockSpec((1, d), lambda i: (0, 0)),
                  pl.BlockSpec((N_EXPERTS, d), lambda i: (0, 0))],
        out_specs=[pl.BlockSpec((SUBLANES, tt), lambda i: (0, i)),
                   pl.BlockSpec((1, N_EXPERTS, LANES), lambda i: (i, 0, 0)),
                   pl.BlockSpec((N_EXPERTS, LANES), lambda i: (0, 0))],
        out_shape=[jax.ShapeDtypeStruct((SUBLANES, m), F32), jax.ShapeDtypeStruct((nt, N_EXPERTS, LANES), F32),
                   jax.ShapeDtypeStruct((N_EXPERTS, LANES), F32)],
        scratch_shapes=[pltpu.VMEM((N_EXPERTS, LANES), F32)],
        compiler_params=_params("arbitrary"),
        name="moe_router",
    )(h, norm_gain.reshape(1, d), w_router.T)


def _slots_kernel(start_ref, info_ref, o_ref):
    info = info_ref[...]
    e1, e2, r1, r2 = info[0:1], info[1:2], info[2:3], info[3:4]
    s1 = r1
    s2 = r2
    for e in range(N_EXPERTS):
        st = start_ref[e].astype(F32)
        s1 = s1 + jnp.where(e1 == float(e), st, 0.0)
        s2 = s2 + jnp.where(e2 == float(e), st, 0.0)
    o_ref[...] = jnp.concatenate([s1, s2, info[4:5], info[5:6], info[6:8], info[6:8]], axis=0)


def moe_slots(info, slot_start):
    m = info.shape[1]
    tt = min(2048, m)
    return pl.pallas_call(
        _slots_kernel,
        grid_spec=pltpu.PrefetchScalarGridSpec(
            num_scalar_prefetch=1, grid=(m // tt,),
            in_specs=[pl.BlockSpec((SUBLANES, tt), lambda i, st: (0, i))],
            out_specs=pl.BlockSpec((SUBLANES, tt), lambda i, st: (0, i))),
        out_shape=jax.ShapeDtypeStruct((SUBLANES, m), F32),
        compiler_params=_params("parallel"),
        name="moe_slots",
    )(slot_start, info)


def _dispatch_kernel(blk_ref, til_ref, flg_ref, slots_ref, x_ref, o_ref, acc_ref, *, bs):
    w = pl.program_id(0)
    flags = flg_ref[w]

    @pl.when((flags & 1) != 0)
    def _():
        acc_ref[...] = jnp.zeros_like(acc_ref)

    @pl.when((flags & 4) != 0)
    def _():
        sl = slots_ref[...]
        sid = (blk_ref[w] * bs + lax.broadcasted_iota(I32, (bs, sl.shape[1]), 0)).astype(F32)
        onehot = jnp.where((sid == sl[0:1, :]) | (sid == sl[1:2, :]), 1.0, 0.0).astype(BF16)
        acc_ref[...] += _dot(onehot, x_ref[...])

    @pl.when((flags & 2) != 0)
    def _():
        o_ref[...] = acc_ref[...].astype(o_ref.dtype)


def moe_dispatch(xn, slots, wl_blk, wl_til, wl_flg, n_slots):
    m, d = xn.shape
    bs = tt = MOE_TILE
    n_items = wl_blk.shape[0]
    return pl.pallas_call(
        functools.partial(_dispatch_kernel, bs=bs),
        grid_spec=pltpu.PrefetchScalarGridSpec(
            num_scalar_prefetch=3, grid=(n_items,),
            in_specs=[pl.BlockSpec((SUBLANES, tt), lambda w, b, t, f: (0, t[w])),
                      pl.BlockSpec((tt, d), lambda w, b, t, f: (t[w], 0))],
            out_specs=pl.BlockSpec((bs, d), lambda w, b, t, f: (b[w], 0)),
            scratch_shapes=[pltpu.VMEM((bs, d), F32)]),
        out_shape=jax.ShapeDtypeStruct((n_slots, d), BF16),
        compiler_params=_params("arbitrary"),
        name="moe_dispatch",
    )(wl_blk, wl_til, wl_flg, slots, xn)


def _moe_ffn_kernel(be_ref, nb_ref, x_ref, w1_ref, w3_ref, w2_ref, o_ref, acc_ref):
    b = pl.program_id(0)
    f = pl.program_id(1)

    @pl.when(f == 0)
    def _():
        acc_ref[...] = jnp.zeros_like(acc_ref)

    @pl.when(b < nb_ref[0])
    def _():
        x = x_ref[...]
        h = _silu(_dot(x, w1_ref[0].astype(BF16))) * _dot(x, w3_ref[0].astype(BF16))
        acc_ref[...] += _dot(h.astype(BF16), w2_ref[0].astype(BF16))

    @pl.when(f == pl.num_programs(1) - 1)
    def _():
        o_ref[...] = acc_ref[...].astype(o_ref.dtype)


def moe_ffn(xs, w1, w3, w2, block_expert, n_used):
    n_slots, d = xs.shape
    ff = w1.shape[2]
    bm = MOE_FFN_ROWS
    tf = MOE_TF
    nf = ff // tf

    def f_of(b, f, nb):
        return jnp.where(b < nb[0], f, nf - 1)

    return pl.pallas_call(
        _moe_ffn_kernel,
        grid_spec=pltpu.PrefetchScalarGridSpec(
            num_scalar_prefetch=2, grid=(n_slots // bm, nf),
            in_specs=[pl.BlockSpec((bm, d), lambda b, f, be, nb: (b, 0)),
                      pl.BlockSpec((1, d, tf), lambda b, f, be, nb: (be[b], 0, f_of(b, f, nb))),
                      pl.BlockSpec((1, d, tf), lambda b, f, be, nb: (be[b], 0, f_of(b, f, nb))),
                      pl.BlockSpec((1, tf, d), lambda b, f, be, nb: (be[b], f_of(b, f, nb), 0))],
            out_specs=pl.BlockSpec((bm, d), lambda b, f, be, nb: (b, 0)),
            scratch_shapes=[pltpu.VMEM((bm, d), F32)]),
        out_shape=jax.ShapeDtypeStruct((n_slots, d), BF16),
        compiler_params=_params("arbitrary", "arbitrary"),
        name="moe_ffn",
    )(block_expert, n_used, xs, w1, w3, w2)


def _combine_kernel(til_ref, blk_ref, flg_ref, tok_ref, y_ref, res_ref, o_ref, acc_ref, *, bs):
    w = pl.program_id(0)
    flags = flg_ref[w]

    @pl.when((flags & 1) != 0)
    def _():
        acc_ref[...] = res_ref[...]

    @pl.when((flags & 4) != 0)
    def _():
        tok = tok_ref[...]
        tt = tok.shape[0]
        sid = (blk_ref[w] * bs + lax.broadcasted_iota(I32, (tt, bs), 1)).astype(F32)
        y = y_ref[...]
        oh1 = jnp.where(sid == tok[:, 0:1], 1.0, 0.0).astype(BF16)
        oh2 = jnp.where(sid == tok[:, 1:2], 1.0, 0.0).astype(BF16)
        acc_ref[...] += tok[:, 2:3] * _dot(oh1, y) + tok[:, 3:4] * _dot(oh2, y)

    @pl.when((flags & 2) != 0)
    def _():
        o_ref[...] = acc_ref[...]


def moe_combine(y, tok_info, residual, wl_til, wl_blk, wl_flg):
    m, d = residual.shape
    bs = tt = MOE_TILE
    n_items = wl_til.shape[0]
    return pl.pallas_call(
        functools.partial(_combine_kernel, bs=bs),
        grid_spec=pltpu.PrefetchScalarGridSpec(
            num_scalar_prefetch=3, grid=(n_items,),
            in_specs=[pl.BlockSpec((tt, SUBLANES), lambda w, t, b, f: (t[w], 0)),
                      pl.BlockSpec((bs, d), lambda w, t, b, f: (b[w], 0)),
                      pl.BlockSpec((tt, d), lambda w, t, b, f: (t[w], 0))],
            out_specs=pl.BlockSpec((tt, d), lambda w, t, b, f: (t[w], 0)),
            scratch_shapes=[pltpu.VMEM((tt, d), F32)]),
        out_shape=jax.ShapeDtypeStruct((m, d), F32),
        compiler_params=_params("arbitrary"),
        name="moe_combine",
    )(wl_til, wl_blk, wl_flg, tok_info, y, residual)


def _work_list(overlap, n_items):
    n_major, n_minor = overlap.shape
    has = jnp.any(overlap, axis=1)
    marked = jnp.concatenate([overlap[:, :1] | ~has[:, None], overlap[:, 1:]], axis=1)
    maj, mino = jnp.nonzero(marked, size=n_items, fill_value=-1)
    valid = maj >= 0
    real = valid & overlap[jnp.maximum(maj, 0), jnp.maximum(mino, 0)]
    prev = jnp.concatenate([jnp.full((1,), -2, maj.dtype), maj[:-1]])
    nxt = jnp.concatenate([maj[1:], jnp.full((1,), -2, maj.dtype)])
    flags = (valid & (maj != prev)) * 1 + (valid & (maj != nxt)) * 2 + real * 4
    last_maj = jnp.max(jnp.where(valid, maj, 0))
    last_min = jnp.sum(jnp.where(valid & (maj == last_maj) & (maj != nxt), mino, 0))
    maj = jnp.where(valid, maj, last_maj)
    mino = jnp.where(valid, mino, last_min)
    return maj.astype(I32), mino.astype(I32), flags.astype(I32)


def moe_layer(h, norm_gain, w_router, w1, w3, w2):
    m, d = h.shape
    E = N_EXPERTS
    tile = min(MOE_TILE, m)
    n_tiles = m // tile
    ffn_rows = MOE_FFN_ROWS
    xn = rmsnorm(h, norm_gain, BF16)
    info, tile_cnt, total = moe_router(h, norm_gain, w_router)
    before = tile_cnt[:, :, 0]
    counts = total[:, 0].astype(I32)
    padded = (counts + ffn_rows - 1) // ffn_rows * ffn_rows
    ends = jnp.cumsum(padded)
    starts = ends - padded
    n_slots = (m * TOP_K // ffn_rows + E) * ffn_rows
    n_ffn_blocks = n_slots // ffn_rows
    n_dblocks = n_slots // tile
    slots = moe_slots(info, starts.astype(I32))
    db_start = jnp.arange(n_dblocks, dtype=I32) * tile
    db_e = jnp.minimum(jnp.sum(db_start[:, None] >= ends[None, :], axis=1), E - 1)
    r0 = db_start - starts[db_e]
    r1 = jnp.minimum(r0 + tile, counts[db_e])
    after = jnp.concatenate([before[1:], counts[None, :].astype(F32)], axis=0).astype(I32)
    bef = before.astype(I32)
    ov = (bef[:, db_e].T < r1[:, None]) & (after[:, db_e].T > r0[:, None]) & (db_start < ends[-1])[:, None]
    n_items = n_dblocks + E * n_tiles
    d_blk, d_til, d_flg = _work_list(ov, n_items)
    c_til, c_blk, c_flg = _work_list(ov.T, n_items)
    xs = moe_dispatch(xn, slots, d_blk, d_til, d_flg, n_slots)
    fb_start = jnp.arange(n_ffn_blocks, dtype=I32) * ffn_rows
    block_expert = jnp.minimum(jnp.sum(fb_start[:, None] >= ends[None, :], axis=1), E - 1).astype(I32)
    n_used = (ends[-1] // ffn_rows).astype(I32).reshape(1)
    ys = moe_ffn(xs, w1, w3, w2, block_expert, n_used)
    return moe_combine(ys, slots.T, h, c_til, c_blk, c_flg)


def even_mixer_layer(h, posf, p, bsz, seq):
    m = bsz * seq
    hn = rmsnorm(h, p["norm_mix"], BF16)
    w_in = p["w_in"]
    c = np.cumsum([0, D_SSM, D_CONV, SSM_HEADS, D_NSA] + [D_NSA_KV] * 6 + [3 * NSA_HEADS])
    z = matmul([(hn, w_in[:, c[0]:c[1]].astype(BF16))])
    xbc = matmul([(hn, w_in[:, c[1]:c[2]].astype(BF16))])
    q = matmul([(hn, w_in[:, c[3]:c[4]].astype(BF16))])
    kv6 = matmul([(hn, w_in[:, c[4]:c[10]].astype(BF16))])
    w_small = jnp.concatenate([w_in[:, c[2]:c[3]], w_in[:, c[10]:c[11]]], axis=1)
    w_small = jnp.pad(w_small, ((0, 0), (0, LANES - w_small.shape[1]))).astype(BF16)
    small = matmul([(hn, w_small)])
    y_ssm = ssd_group(z, xbc, small, p["conv_w"], p["conv_b"], p["dt_bias"], p["a_log"], p["d_skip"],
                      p["ssm_norm"], bsz, seq)
    qn, ksn, kwn = rope_norm(posf, q, kv6, p["q_gain"], p["ks_gain"], p["kw_gain"])
    nch = seq // CMP_STRIDE

    def chunks(col0):
        u = kv6[:, col0:col0 + D_NSA_KV].reshape(bsz, nch, CMP_STRIDE, NSA_KV_HEADS, NSA_HEAD_DIM)
        return u.transpose(0, 3, 1, 2, 4).reshape(bsz, NSA_KV_HEADS, nch, CMP_STRIDE * NSA_HEAD_DIM)

    kc, vc = compress(chunks(0), chunks(D_NSA_KV), p["pe_k"], p["pe_v"], p["wk1"], p["wk2"], p["wv1"], p["wv2"],
                      p["kc_gain"])
    o_cmp, sel = cmp_select(qn, kc, vc, bsz, seq)
    o_sel = masked_attention(qn, ksn, 0, kv6, 3 * NSA_KV_HEADS, bsz, seq, "sel", sel)
    o_win = masked_attention(qn, kwn, 0, kv6, 5 * NSA_KV_HEADS, bsz, seq, "win")
    y_nsa = nsa_combine(small, o_cmp, o_sel, o_win)
    w_out = p["w_out"].astype(BF16)
    return matmul([(y_ssm, w_out[:D_SSM]), (y_nsa, w_out[D_SSM:])], residual=h)


def rwkv_layer(h, p, bsz, seq):
    m, d = h.shape
    mixes = rwkv_mix(h, p["norm_mix"], p["mu"], seq)
    bf = lambda w: w.astype(BF16)

    def lora_pad(w_a, w_b):
        r = w_a.shape[1]
        rp = -(-r // LANES) * LANES
        return bf(jnp.pad(w_a, ((0, 0), (0, rp - r)))), bf(jnp.pad(w_b, ((0, rp - r), (0, 0))))

    r = matmul([(mixes[0], bf(p["w_r"]))])
    k = matmul([(mixes[2], bf(p["w_k"]))])
    v = matmul([(mixes[3], bf(p["w_v"]))])
    w1, w2 = lora_pad(p["w1"], p["w2"])
    w_raw = matmul([(matmul([(mixes[1], w1)], act="tanh", out_dtype=BF16), w2)], bias=p["w0"])
    a1, a2 = lora_pad(p["a1"], p["a2"])
    a_sig = matmul([(matmul([(mixes[4], a1)], out_dtype=BF16), a2)], bias=p["a0"], act="sigmoid")
    g1, g2 = lora_pad(p["g1"], p["g2"])
    g = matmul([(matmul([(mixes[5], g1)], act="sigmoid", out_dtype=BF16), g2)])
    wd, kh, kk, b = rwkv_prep(k, w_raw, a_sig, p["k_k"], p["k_a"])

    def to_scan(u):
        return u.reshape(bsz, seq, RWKV_HEADS, RWKV_HEAD_DIM).transpose(1, 3, 0, 2).reshape(seq, RWKV_HEAD_DIM, bsz * RWKV_HEADS)

    y = wkv_scan(to_scan(r), to_scan(wd), to_scan(kh), to_scan(v), to_scan(kk), to_scan(b))
    y = y.reshape(seq, RWKV_HEAD_DIM, bsz, RWKV_HEADS).transpose(2, 0, 3, 1).reshape(m, d)
    o = rwkv_post(y, r, kh, v, g, p["r_k"].reshape(-1), p["ln_w"], p["ln_b"])
    return matmul([(o, bf(p["w_o"]))], residual=h)


def kernel(x, mem, positions, norm_mix, norm_xattn, norm_mem, norm_ffn, xattn_wq, xattn_wkv, xattn_wo, xattn_q_gain, xattn_k_gain, ev_w_in, ev_conv_w, ev_conv_b, ev_dt_bias, ev_a_log, ev_d_skip, ev_ssm_norm, ev_q_gain, ev_kc_gain, ev_ks_gain, ev_kw_gain, ev_pe_k, ev_pe_v, ev_cmp_wk1, ev_cmp_wk2, ev_cmp_wv1, ev_cmp_wv2, ev_w_out, ev_ffn_w1, ev_ffn_w3, ev_ffn_w2, od_mu, od_w_r, od_w_k, od_w_v, od_w_o, od_w0, od_w1, od_w2, od_a0, od_a1, od_a2, od_g1, od_g2, od_k_k, od_k_a, od_r_k, od_ln_w, od_ln_b, od_router, od_moe_w1, od_moe_w3, od_moe_w2):
    bsz, seq, d = x.shape
    m = bsz * seq
    depth = norm_mix.shape[0]
    h = x.reshape(m, d)
    mem2d = mem.reshape(-1, d)
    posf = positions.astype(F32).reshape(m, 1)
    for layer in range(depth):
        i = layer // 2
        if layer % 2 == 0:
            p = dict(norm_mix=norm_mix[layer], w_in=ev_w_in[i], conv_w=ev_conv_w[i], conv_b=ev_conv_b[i],
                     dt_bias=ev_dt_bias[i], a_log=ev_a_log[i], d_skip=ev_d_skip[i], ssm_norm=ev_ssm_norm[i],
                     q_gain=ev_q_gain[i], kc_gain=ev_kc_gain[i], ks_gain=ev_ks_gain[i], kw_gain=ev_kw_gain[i],
                     pe_k=ev_pe_k[i], pe_v=ev_pe_v[i], wk1=ev_cmp_wk1[i], wk2=ev_cmp_wk2[i], wv1=ev_cmp_wv1[i],
                     wv2=ev_cmp_wv2[i], w_out=ev_w_out[i])
            h = even_mixer_layer(h, posf, p, bsz, seq)
        else:
            p = dict(norm_mix=norm_mix[layer], mu=od_mu[i], w_r=od_w_r[i], w_k=od_w_k[i], w_v=od_w_v[i], w_o=od_w_o[i],
                     w0=od_w0[i], w1=od_w1[i], w2=od_w2[i], a0=od_a0[i], a1=od_a1[i], a2=od_a2[i], g1=od_g1[i],
                     g2=od_g2[i], k_k=od_k_k[i], k_a=od_k_a[i], r_k=od_r_k[i], ln_w=od_ln_w[i], ln_b=od_ln_b[i])
            h = rwkv_layer(h, p, bsz, seq)
        h = memory_xattn(h, mem2d, norm_xattn[layer], norm_mem[layer], xattn_wq[layer], xattn_wkv[layer],
                         xattn_wo[layer], xattn_q_gain[layer], xattn_k_gain[layer], bsz, seq)
        if layer % 2 == 0:
            hn = rmsnorm(h, norm_ffn[layer], BF16)
            h = swiglu_ffn(hn, ev_ffn_w1[i].astype(BF16), ev_ffn_w3[i].astype(BF16), ev_ffn_w2[i].astype(BF16), h)
        else:
            h = moe_layer(h, norm_ffn[layer], od_router[i], od_moe_w1[i], od_moe_w3[i], od_moe_w2[i])
    return h.reshape(bsz, seq, d)
```

```python
import functools
import math

import numpy as np
import jax
import jax.numpy as jnp
from jax import lax
from jax.experimental import pallas as pl
from jax.experimental.pallas import tpu as pltpu

F32 = jnp.float32
BF16 = jnp.bfloat16
I32 = jnp.int32

D_MODEL = 2048
NORM_EPS = 1e-6
NEG_BIG = -1e30

SSM_HEADS = 32
SSM_HEAD_DIM = 64
D_SSM = SSM_HEADS * SSM_HEAD_DIM
SSM_GROUPS = 4
D_STATE = 128
CONV_WIDTH = 4
SSD_CHUNK = 128
D_CONV = D_SSM + 2 * SSM_GROUPS * D_STATE
D_SSM_GROUP = D_SSM // SSM_GROUPS

NSA_HEADS = 16
NSA_KV_HEADS = 4
NSA_HEAD_DIM = 128
NSA_Q_PER_KV = NSA_HEADS // NSA_KV_HEADS
D_NSA = NSA_HEADS * NSA_HEAD_DIM
D_NSA_KV = NSA_KV_HEADS * NSA_HEAD_DIM
CMP_BLOCK = 32
CMP_STRIDE = 16
CMP_HIDDEN = 256
SEL_BLOCK = 64
N_SELECT = 16
WINDOW = 512
ROPE_DIM = NSA_HEAD_DIM // 4
ROPE_THETA = 500000.0

RWKV_HEAD_DIM = 64
RWKV_HEADS = D_MODEL // RWKV_HEAD_DIM
RWKV_GN_EPS = 1e-5 * RWKV_HEAD_DIM

XATTN_HEADS = 4
XATTN_HEAD_DIM = 128
D_XATTN = XATTN_HEADS * XATTN_HEAD_DIM

N_EXPERTS = 8
TOP_K = 2

LANES = 128
SUBLANES = 8
VMEM_LIMIT_BYTES = 56 * 1024 * 1024

ROW_TILE = 512
MM_TM = 1024
MM_TN = 512
FFN_TM = 512
FFN_TF = 512
MOE_FFN_ROWS = 1024
MOE_TF = 256
COMBINE_TM = 256
MOE_TILE = 256
ATT_TQ = 256
ATT_TK = 512
CMP_TQ = 128
WKV_T = 32


def _params(*sem):
    return pltpu.CompilerParams(dimension_semantics=sem, vmem_limit_bytes=VMEM_LIMIT_BYTES)


def _sigmoid(x):
    return 1.0 / (1.0 + jnp.exp(-x))


def _silu(x):
    return x * _sigmoid(x)


def _softplus(x):
    return jnp.maximum(x, 0.0) + jnp.log(1.0 + jnp.exp(-jnp.abs(x)))


def _split3(x):
    h = x.astype(BF16)
    r = x - h.astype(F32)
    m = r.astype(BF16)
    l = (r - m.astype(F32)).astype(BF16)
    return h, m, l


def _dot(a, b):
    return jnp.dot(a, b, preferred_element_type=F32)


def _dot_nt(a, b):
    return lax.dot_general(a, b, (((1,), (1,)), ((), ())), preferred_element_type=F32)


def _dot_f32_sel(x, e):
    h, m, l = _split3(x)
    return _dot(h, e) + _dot(m, e) + _dot(l, e)


def _sel_dot_f32(e, x):
    h, m, l = _split3(x)
    return _dot(e, h) + _dot(e, m) + _dot(e, l)


def _rms(x, eps=NORM_EPS):
    return x * lax.rsqrt(jnp.mean(x * x, axis=-1, keepdims=True) + eps)


def _rmsnorm_kernel(x_ref, g_ref, o_ref):
    o_ref[...] = (_rms(x_ref[...]) * g_ref[...]).astype(o_ref.dtype)


def rmsnorm(x, gain, out_dtype):
    m, d = x.shape
    tm = min(ROW_TILE, m)
    return pl.pallas_call(
        _rmsnorm_kernel,
        grid=(m // tm,),
        in_specs=[pl.BlockSpec((tm, d), lambda i: (i, 0)), pl.BlockSpec((1, d), lambda i: (0, 0))],
        out_specs=pl.BlockSpec((tm, d), lambda i: (i, 0)),
        out_shape=jax.ShapeDtypeStruct((m, d), out_dtype),
        compiler_params=_params("parallel"),
        name="rmsnorm",
    )(x, gain.reshape(1, d))


def _mm_kernel(*refs, n_pairs, has_bias, has_res, act):
    o_ref = refs[-1]
    acc = None
    for p in range(n_pairs):
        a = refs[2 * p][...].astype(BF16)
        d = _dot(a, refs[2 * p + 1][...])
        acc = d if acc is None else acc + d
    k = 2 * n_pairs
    if has_bias:
        acc = acc + refs[k][...]
        k += 1
    if act == "tanh":
        acc = jnp.tanh(acc)
    elif act == "sigmoid":
        acc = _sigmoid(acc)
    if has_res:
        acc = acc + refs[k][...]
    o_ref[...] = acc.astype(o_ref.dtype)


def matmul(pairs, bias=None, residual=None, act=None, out_dtype=F32):
    m = pairs[0][0].shape[0]
    n = pairs[0][1].shape[1]
    kmax = max(a.shape[1] for a, _ in pairs)
    tm = min(MM_TM if kmax <= 2048 else MM_TM // 2, m)
    tn = MM_TN if n % MM_TN == 0 else (256 if n % 256 == 0 else LANES)
    tn = min(tn, n)
    in_specs, args = [], []
    for a, w in pairs:
        kk = a.shape[1]
        in_specs += [pl.BlockSpec((tm, kk), lambda i, j: (i, 0)), pl.BlockSpec((kk, tn), lambda i, j: (0, j))]
        args += [a, w]
    if bias is not None:
        in_specs.append(pl.BlockSpec((1, tn), lambda i, j: (0, j)))
        args.append(bias.reshape(1, n).astype(F32))
    if residual is not None:
        in_specs.append(pl.BlockSpec((tm, tn), lambda i, j: (i, j)))
        args.append(residual)
    kern = functools.partial(_mm_kernel, n_pairs=len(pairs), has_bias=bias is not None,
                             has_res=residual is not None, act=act)
    return pl.pallas_call(
        kern,
        grid=(m // tm, n // tn),
        in_specs=in_specs,
        out_specs=pl.BlockSpec((tm, tn), lambda i, j: (i, j)),
        out_shape=jax.ShapeDtypeStruct((m, n), out_dtype),
        compiler_params=_params("parallel", "parallel"),
        name="matmul",
    )(*args)


def _ffn_kernel(x_ref, w1_ref, w3_ref, w2_ref, res_ref, o_ref):
    f = pl.program_id(1)

    @pl.when(f == 0)
    def _():
        o_ref[...] = res_ref[...]

    x = x_ref[...]
    h = _silu(_dot(x, w1_ref[...])) * _dot(x, w3_ref[...])
    o_ref[...] += _dot(h.astype(BF16), w2_ref[...])


def swiglu_ffn(x, w1, w3, w2, residual):
    m, d = x.shape
    ff = w1.shape[1]
    tm = min(FFN_TM, m)
    tf = FFN_TF
    return pl.pallas_call(
        _ffn_kernel,
        grid=(m // tm, ff // tf),
        in_specs=[pl.BlockSpec((tm, d), lambda i, f: (i, 0)),
                  pl.BlockSpec((d, tf), lambda i, f: (0, f)),
                  pl.BlockSpec((d, tf), lambda i, f: (0, f)),
                  pl.BlockSpec((tf, d), lambda i, f: (f, 0)),
                  pl.BlockSpec((tm, d), lambda i, f: (i, 0))],
        out_specs=pl.BlockSpec((tm, d), lambda i, f: (i, 0)),
        out_shape=jax.ShapeDtypeStruct((m, d), F32),
        compiler_params=_params("parallel", "arbitrary"),
        name="swiglu_ffn",
    )(x, w1, w3, w2, residual)


def _ssd_kernel(z_ref, xbc_ref, dt_ref, cw_ref, cb_ref, dtb_ref, alog_ref, dskip_ref, nw_ref, eh_ref,
                o_ref, ext_ref, st_ref):
    L = SSD_CHUNK
    G = SSM_GROUPS
    GW = D_SSM_GROUP
    c = pl.program_id(1)

    @pl.when(c == 0)
    def _():
        ext_ref[L:L + SUBLANES, :] = jnp.zeros((SUBLANES, D_CONV), F32)
        st_ref[...] = jnp.zeros_like(st_ref)

    ext_ref[0:SUBLANES, :] = ext_ref[L:L + SUBLANES, :]
    ext_ref[SUBLANES:L + SUBLANES, :] = xbc_ref[...]
    acc = jnp.zeros((L, D_CONV), F32) + cb_ref[...]
    for k in range(CONV_WIDTH):
        off = SUBLANES - (CONV_WIDTH - 1) + k
        acc = acc + cw_ref[k:k + 1, :] * ext_ref[off:off + L, :]
    xc = _silu(acc)

    dt = _softplus(dt_ref[...] + dtb_ref[...])
    a = -jnp.exp(alog_ref[...])
    adt = dt * a
    row = lax.broadcasted_iota(I32, (L, L), 0)
    col = lax.broadcasted_iota(I32, (L, L), 1)
    causal = row >= col
    tril = jnp.where(causal, 1.0, 0.0).astype(BF16)
    a_cum = _sel_dot_f32(tril, adt)
    a_cum_t = a_cum.T
    eh = eh_ref[...]
    acx = _dot_f32_sel(a_cum, eh)
    dtx = _dot_f32_sel(dt, eh)
    a_end = acx[L - 1:L, :]
    xs = xc[:, :D_SSM]
    xdt = xs * dtx
    xdte = (xdt * jnp.exp(a_end - acx)).astype(BF16)
    from_start = jnp.exp(acx)
    lane = lax.broadcasted_iota(I32, (L, LANES), 1)
    lo_half = lane < SSM_HEAD_DIM
    z = z_ref[...]

    for g in range(G):
        bm = xc[:, D_SSM + g * D_STATE:D_SSM + (g + 1) * D_STATE]
        cm = xc[:, D_SSM + G * D_STATE + g * D_STATE:D_SSM + G * D_STATE + (g + 1) * D_STATE]
        bm_b = bm.astype(BF16)
        cm_b = cm.astype(BF16)
        cb = _dot_nt(cm_b, bm_b)
        st = st_ref[g]
        y_off = _dot(cm_b, st.astype(BF16)) * from_start[:, g * GW:(g + 1) * GW]
        pieces = []
        for pp in range(GW // LANES):
            h0 = g * (GW // SSM_HEAD_DIM) + 2 * pp
            ms = []
            for hh in (h0, h0 + 1):
                diff = a_cum[:, hh:hh + 1] - a_cum_t[hh:hh + 1, :]
                dec = jnp.exp(jnp.where(causal, diff, NEG_BIG))
                ms.append((cb * dec).astype(BF16))
            lhs = jnp.concatenate(ms, axis=1)
            c0 = g * GW + pp * LANES
            xp = xdt[:, c0:c0 + LANES]
            rhs = jnp.concatenate([jnp.where(lo_half, xp, 0.0), jnp.where(lo_half, 0.0, xp)], axis=0)
            pieces.append(_dot(lhs, rhs.astype(BF16)))
        y = jnp.concatenate(pieces, axis=1) + y_off
        y = y + xs[:, g * GW:(g + 1) * GW] * dskip_ref[:, g * GW:(g + 1) * GW]
        y = y * _silu(z[:, g * GW:(g + 1) * GW])
        y = _rms(y) * nw_ref[:, g * GW:(g + 1) * GW]
        o_ref[:, g * GW:(g + 1) * GW] = y.astype(o_ref.dtype)
        s_new = _dot(bm.T.astype(BF16), xdte[:, g * GW:(g + 1) * GW])
        st_ref[g] = st * jnp.exp(a_end[:, g * GW:(g + 1) * GW]) + s_new


def ssd_group(z, xbc, small, conv_w, conv_b, dt_bias, a_log, d_skip, norm_w, bsz, seq):
    m = bsz * seq
    L = SSD_CHUNK
    nc = seq // L
    pad = LANES - SSM_HEADS
    eh = np.zeros((LANES, D_SSM), np.float32)
    for h in range(SSM_HEADS):
        eh[h, h * SSM_HEAD_DIM:(h + 1) * SSM_HEAD_DIM] = 1.0
    row_spec = lambda w: pl.BlockSpec((L, w), lambda b, c: (b * nc + c, 0))
    const = lambda r, w: pl.BlockSpec((r, w), lambda b, c: (0, 0))
    return pl.pallas_call(
        _ssd_kernel,
        grid=(bsz, nc),
        in_specs=[row_spec(D_SSM), row_spec(D_CONV), row_spec(LANES),
                  const(CONV_WIDTH, D_CONV), const(1, D_CONV), const(1, LANES), const(1, LANES),
                  const(1, D_SSM), const(1, D_SSM), const(LANES, D_SSM)],
        out_specs=row_spec(D_SSM),
        out_shape=jax.ShapeDtypeStruct((m, D_SSM), BF16),
        scratch_shapes=[pltpu.VMEM((L + SUBLANES, D_CONV), F32),
                        pltpu.VMEM((SSM_GROUPS, D_STATE, D_SSM_GROUP), F32)],
        compiler_params=_params("parallel", "arbitrary"),
        name="ssd_group",
    )(z, xbc, small, conv_w.T, conv_b.reshape(1, D_CONV),
      jnp.pad(dt_bias, (0, pad)).reshape(1, LANES), jnp.pad(a_log, (0, pad)).reshape(1, LANES),
      jnp.repeat(d_skip, SSM_HEAD_DIM).reshape(1, D_SSM), norm_w.reshape(1, D_SSM), jnp.asarray(eh, BF16))


def _rope_norm_kernel(pos_ref, invf_ref, q_ref, kv_ref, qg_ref, ksg_ref, kwg_ref, qo_ref, kso_ref, kwo_ref):
    ang = pos_ref[...] * invf_ref[...]
    cos = jnp.cos(ang)
    sin = jnp.sin(ang)
    lane = lax.broadcasted_iota(I32, ang.shape, 1)
    first = lane < ROPE_DIM // 2
    sin_signed = jnp.where(first, -sin, sin)

    def head(x, gain):
        y = _rms(x) * gain
        partner = jnp.where(first, pltpu.roll(y, LANES - ROPE_DIM // 2, axis=1), pltpu.roll(y, ROPE_DIM // 2, axis=1))
        return y * cos + partner * sin_signed

    hd = NSA_HEAD_DIM
    scale = hd ** -0.5
    for h in range(NSA_HEADS):
        qo_ref[:, h * hd:(h + 1) * hd] = (head(q_ref[:, h * hd:(h + 1) * hd], qg_ref[...]) * scale).astype(qo_ref.dtype)
    for h in range(NSA_KV_HEADS):
        ks = kv_ref[:, 2 * D_NSA_KV + h * hd:2 * D_NSA_KV + (h + 1) * hd]
        kw = kv_ref[:, 4 * D_NSA_KV + h * hd:4 * D_NSA_KV + (h + 1) * hd]
        kso_ref[:, h * hd:(h + 1) * hd] = head(ks, ksg_ref[...]).astype(kso_ref.dtype)
        kwo_ref[:, h * hd:(h + 1) * hd] = head(kw, kwg_ref[...]).astype(kwo_ref.dtype)


def rope_norm(posf, q, kv6, q_gain, ks_gain, kw_gain):
    m = q.shape[0]
    tm = min(ROW_TILE, m)
    half = ROPE_DIM // 2
    inv = np.exp(-math.log(ROPE_THETA) * np.arange(0, ROPE_DIM, 2, dtype=np.float32) / ROPE_DIM).astype(np.float32)
    invf = np.zeros((1, LANES), np.float32)
    invf[0, :half] = inv
    invf[0, half:ROPE_DIM] = inv
    row = lambda w: pl.BlockSpec((tm, w), lambda i: (i, 0))
    const = pl.BlockSpec((1, LANES), lambda i: (0, 0))
    return pl.pallas_call(
        _rope_norm_kernel,
        grid=(m // tm,),
        in_specs=[row(1), const, row(D_NSA), row(6 * D_NSA_KV), const, const, const],
        out_specs=[row(D_NSA), row(D_NSA_KV), row(D_NSA_KV)],
        out_shape=[jax.ShapeDtypeStruct((m, D_NSA), BF16), jax.ShapeDtypeStruct((m, D_NSA_KV), BF16),
                   jax.ShapeDtypeStruct((m, D_NSA_KV), BF16)],
        compiler_params=_params("parallel"),
        name="rope_norm",
    )(posf, jnp.asarray(invf), q, kv6, q_gain.reshape(1, LANES), ks_gain.reshape(1, LANES), kw_gain.reshape(1, LANES))


def _compress_kernel(uk_ref, uv_ref, pek_ref, pev_ref, wk1_ref, wk2_ref, wv1_ref, wv2_ref, g_ref, kc_ref, vc_ref):
    def mlp(u, pe_ref, w1_ref, w2_ref):
        n = u.shape[0]
        h1 = _dot((u + pe_ref[0:1, :]).astype(BF16), w1_ref[0])
        h2 = _dot((u + pe_ref[1:2, :]).astype(BF16), w1_ref[1])
        pre = h1 + pltpu.roll(h2, n - 1, axis=0)
        return _dot(_silu(pre).astype(BF16), w2_ref[...])

    kc = mlp(uk_ref[0, 0], pek_ref, wk1_ref, wk2_ref)
    kc_ref[0, 0] = _rms(kc) * g_ref[...]
    vc_ref[0, 0] = mlp(uv_ref[0, 0], pev_ref, wv1_ref, wv2_ref)


def compress(uk, uv, pe_k, pe_v, wk1, wk2, wv1, wv2, kc_gain):
    bsz, nkv, nch, width = uk.shape
    half = CMP_STRIDE * NSA_HEAD_DIM
    u_spec = pl.BlockSpec((1, 1, nch, width), lambda b, g: (b, g, 0, 0))
    o_spec = pl.BlockSpec((1, 1, nch, NSA_HEAD_DIM), lambda b, g: (b, g, 0, 0))
    c2 = lambda s: pl.BlockSpec(s, lambda b, g: (0, 0))
    c3 = lambda s: pl.BlockSpec(s, lambda b, g: (0, 0, 0))
    o_shape = jax.ShapeDtypeStruct((bsz, nkv, nch, NSA_HEAD_DIM), F32)
    return pl.pallas_call(
        _compress_kernel,
        grid=(bsz, nkv),
        in_specs=[u_spec, u_spec, c2((2, half)), c2((2, half)),
                  c3((2, half, CMP_HIDDEN)), c2((CMP_HIDDEN, NSA_HEAD_DIM)),
                  c3((2, half, CMP_HIDDEN)), c2((CMP_HIDDEN, NSA_HEAD_DIM)), c2((1, NSA_HEAD_DIM))],
        out_specs=[o_spec, o_spec],
        out_shape=[o_shape, o_shape],
        compiler_params=_params("parallel", "parallel"),
        name="nsa_compress",
    )(uk, uv, pe_k.reshape(2, half), pe_v.reshape(2, half),
      wk1.reshape(2, half, CMP_HIDDEN).astype(BF16), wk2.astype(BF16),
      wv1.reshape(2, half, CMP_HIDDEN).astype(BF16), wv2.astype(BF16), kc_gain.reshape(1, NSA_HEAD_DIM))


def _stack_heads(q):
    return jnp.concatenate([q[:, r * NSA_HEAD_DIM:(r + 1) * NSA_HEAD_DIM] for r in range(NSA_Q_PER_KV)], axis=0)


def _unstack_heads(o, tq):
    return jnp.concatenate([o[r * tq:(r + 1) * tq, :] for r in range(NSA_Q_PER_KV)], axis=1)


def _cmp_select_kernel(q_ref, kc_ref, vc_ref, ovt_ref, o_ref, sel_ref, *, tq, n_blk):
    R = NSA_Q_PER_KV
    q0 = pl.program_id(2) * tq
    q4 = _stack_heads(q_ref[...])
    kc = kc_ref[0, 0].astype(BF16)
    ncmp = kc.shape[0]
    s = _dot_nt(q4, kc)
    t_row = q0 + lax.broadcasted_iota(I32, (tq, ncmp), 0)
    c_col = lax.broadcasted_iota(I32, (tq, ncmp), 1)
    visf = jnp.where((c_col * CMP_STRIDE + CMP_BLOCK - 1) <= t_row, 1.0, 0.0)
    vis = jnp.concatenate([visf] * R, axis=0) > 0.5
    s = jnp.where(vis, s, NEG_BIG)
    mx = jnp.max(s, axis=-1, keepdims=True)
    e = jnp.where(vis, jnp.exp(s - mx), 0.0)
    den = jnp.sum(e, axis=-1, keepdims=True)
    p = e / jnp.maximum(den, 1e-30)
    o = _dot(p.astype(BF16), vc_ref[0, 0].astype(BF16))
    o_ref[...] = _unstack_heads(o, tq)

    psum = p[0:tq]
    for r in range(1, R):
        psum = psum + p[r * tq:(r + 1) * tq]
    h, m, l = _split3(psum)
    ovt = ovt_ref[...]
    imp = _dot_nt(ovt, h) + _dot_nt(ovt, m) + _dot_nt(ovt, l)
    jdx = lax.broadcasted_iota(I32, (n_blk, tq), 0)
    t_lane = q0 + lax.broadcasted_iota(I32, (n_blk, tq), 1)
    cur = t_lane // SEL_BLOCK
    causal = jdx <= cur
    forced = ((jdx == 0) | (jdx >= cur - 1)) & causal
    score = jnp.where(forced, jnp.inf, jnp.where(causal, imp, -jnp.inf))
    rank = jnp.zeros((n_blk, tq), F32)
    for jp in range(n_blk):
        rowv = score[jp:jp + 1, :]
        beats = (rowv > score) | ((rowv == score) & (jdx > jp))
        rank = rank + jnp.where(beats, 1.0, 0.0)
    sel = jnp.where((rank < float(min(N_SELECT, n_blk))) & causal, 1.0, 0.0)
    if n_blk < LANES:
        sel = jnp.concatenate([sel, jnp.zeros((LANES - n_blk, tq), F32)], axis=0)
    sel_ref[...] = sel.T.astype(sel_ref.dtype)


def cmp_select(qn, kc, vc, bsz, seq):
    m = bsz * seq
    G = NSA_KV_HEADS
    tq = CMP_TQ
    nq = seq // tq
    ncmp = kc.shape[2]
    n_blk = seq // SEL_BLOCK
    assert n_blk <= LANES and tq == LANES
    c0 = np.arange(ncmp)[None, :] * CMP_STRIDE
    s0 = np.arange(n_blk)[:, None] * SEL_BLOCK
    ov = np.clip(np.minimum(c0 + CMP_BLOCK, s0 + SEL_BLOCK) - np.maximum(c0, s0), 0, None) / CMP_STRIDE
    n_cmp_valid = (seq - CMP_BLOCK) // CMP_STRIDE + 1
    ov[:, n_cmp_valid:] = 0.0
    kv_spec = pl.BlockSpec((1, 1, ncmp, NSA_HEAD_DIM), lambda b, g, i: (b, g, 0, 0))
    kern = functools.partial(_cmp_select_kernel, tq=tq, n_blk=n_blk)
    return pl.pallas_call(
        kern,
        grid=(bsz, G, nq),
        in_specs=[pl.BlockSpec((tq, NSA_Q_PER_KV * NSA_HEAD_DIM), lambda b, g, i: (b * nq + i, g)),
                  kv_spec, kv_spec, pl.BlockSpec((n_blk, ncmp), lambda b, g, i: (0, 0))],
        out_specs=[pl.BlockSpec((tq, NSA_Q_PER_KV * NSA_HEAD_DIM), lambda b, g, i: (b * nq + i, g)),
                   pl.BlockSpec((tq, LANES), lambda b, g, i: ((b * G + g) * nq + i, 0))],
        out_shape=[jax.ShapeDtypeStruct((m, D_NSA), F32), jax.ShapeDtypeStruct((bsz * G * seq, LANES), BF16)],
        compiler_params=_params("parallel", "parallel", "parallel"),
        name="nsa_cmp_select",
    )(qn, kc, vc, jnp.asarray(ov, BF16))


def _masked_attn_kernel(q_ref, k_ref, v_ref, *rest, mode, tq, tk):
    if mode == "sel":
        sel_ref, o_ref = rest
    else:
        (o_ref,) = rest
    R = NSA_Q_PER_KV
    hd = NSA_HEAD_DIM
    q0 = pl.program_id(2) * tq
    q = q_ref[...]
    qs = [q[:, r * hd:(r + 1) * hd] for r in range(R)]
    t_row = q0 + lax.broadcasted_iota(I32, (tq, tk), 0)
    key_lane = lax.broadcasted_iota(I32, (tq, tk), 1)
    if mode == "sel":
        sel_t = sel_ref[...]
        blk_row = lax.broadcasted_iota(I32, (LANES, tk), 0)
        key_col = lax.broadcasted_iota(I32, (LANES, tk), 1)
        lo = 0
    else:
        lo = jnp.maximum(q0 - (WINDOW - 1), 0) // tk
    hi = (q0 + tq - 1) // tk + 1

    def body(kt, carry):
        k0 = pl.multiple_of(kt * tk, tk)
        k_t = k_ref[pl.ds(k0, tk), :]
        v_t = v_ref[pl.ds(k0, tk), :].astype(BF16)
        key = k0 + key_lane
        if mode == "sel":
            expand = jnp.where((k0 + key_col) // SEL_BLOCK == blk_row, 1.0, 0.0).astype(BF16)
            picked = _dot(sel_t, expand)
            ok = (picked > 0.5) & (key <= t_row)
        else:
            ok = (key <= t_row) & (key > t_row - WINDOW)
        bias = jnp.where(ok, 0.0, NEG_BIG)
        new = []
        for r in range(R):
            m_prev, l_prev, acc = carry[3 * r:3 * r + 3]
            s = _dot_nt(qs[r], k_t) + bias
            m_new = jnp.maximum(m_prev, jnp.max(s, axis=-1, keepdims=True))
            p = jnp.exp(s - m_new)
            alpha = jnp.exp(m_prev - m_new)
            new += [m_new, alpha * l_prev + jnp.sum(p, axis=-1, keepdims=True),
                    alpha * acc + _dot(p.astype(BF16), v_t)]
        return tuple(new)

    init = (jnp.full((tq, 1), NEG_BIG, F32), jnp.zeros((tq, 1), F32), jnp.zeros((tq, hd), F32)) * R
    out = lax.fori_loop(lo, hi, body, init)
    o_ref[...] = jnp.concatenate([out[3 * r + 2] / out[3 * r + 1] for r in range(R)], axis=1)


def masked_attention(qn, k, k_col0, v, v_col0, bsz, seq, mode, sel=None):
    m = bsz * seq
    G = NSA_KV_HEADS
    tq = min(ATT_TQ, seq)
    tk = min(ATT_TK, seq)
    nq = seq // tq
    hd = NSA_HEAD_DIM
    q_spec = pl.BlockSpec((tq, NSA_Q_PER_KV * hd), lambda b, g, i: (b * nq + i, g))
    in_specs = [q_spec,
                pl.BlockSpec((seq, hd), lambda b, g, i: (b, k_col0 + g)),
                pl.BlockSpec((seq, hd), lambda b, g, i: (b, v_col0 + g))]
    args = [qn, k, v]
    if mode == "sel":
        in_specs.append(pl.BlockSpec((tq, LANES), lambda b, g, i: ((b * G + g) * nq + i, 0)))
        args.append(sel)
    return pl.pallas_call(
        functools.partial(_masked_attn_kernel, mode=mode, tq=tq, tk=tk),
        grid=(bsz, G, nq),
        in_specs=in_specs,
        out_specs=q_spec,
        out_shape=jax.ShapeDtypeStruct((m, D_NSA), F32),
        compiler_params=_params("parallel", "parallel", "arbitrary"),
        name="nsa_attn_" + mode,
    )(*args)


def _nsa_combine_kernel(g_ref, e_ref, oc_ref, os_ref, ow_ref, o_ref):
    gate = _sigmoid(g_ref[...])
    gx = _dot_f32_sel(gate, e_ref[...])
    y = gx[:, :D_NSA] * oc_ref[...] + gx[:, D_NSA:2 * D_NSA] * os_ref[...] + gx[:, 2 * D_NSA:] * ow_ref[...]
    o_ref[...] = y.astype(o_ref.dtype)


def nsa_combine(small, o_cmp, o_sel, o_win):
    m = small.shape[0]
    tm = min(COMBINE_TM, m)
    e = np.zeros((LANES, 3 * D_NSA), np.float32)
    for h in range(NSA_HEADS):
        for br in range(3):
            e[SSM_HEADS + 3 * h + br, br * D_NSA + h * NSA_HEAD_DIM:br * D_NSA + (h + 1) * NSA_HEAD_DIM] = 1.0
    row = lambda w: pl.BlockSpec((tm, w), lambda i: (i, 0))
    return pl.pallas_call(
        _nsa_combine_kernel,
        grid=(m // tm,),
        in_specs=[row(LANES), pl.BlockSpec((LANES, 3 * D_NSA), lambda i: (0, 0)), row(D_NSA), row(D_NSA), row(D_NSA)],
        out_specs=row(D_NSA),
        out_shape=jax.ShapeDtypeStruct((m, D_NSA), BF16),
        compiler_params=_params("parallel"),
        name="nsa_combine",
    )(small, jnp.asarray(e, BF16), o_cmp, o_sel, o_win)


def _xattn_kernel(q_ref, kv_ref, qg_ref, kg_ref, o_ref):
    hd = XATTN_HEAD_DIM
    scale = hd ** -0.5
    outs = []
    for h in range(XATTN_HEADS):
        q = (_rms(q_ref[:, h * hd:(h + 1) * hd]) * qg_ref[...]).astype(BF16)
        k = (_rms(kv_ref[:, h * hd:(h + 1) * hd]) * kg_ref[...]).astype(BF16)
        v = kv_ref[:, D_XATTN + h * hd:D_XATTN + (h + 1) * hd].astype(BF16)
        s = _dot_nt(q, k) * scale
        e = jnp.exp(s - jnp.max(s, axis=-1, keepdims=True))
        p = e / jnp.sum(e, axis=-1, keepdims=True)
        outs.append(_dot(p.astype(BF16), v))
    o_ref[...] = jnp.concatenate(outs, axis=1).astype(o_ref.dtype)


def xattn_core(q, kv, q_gain, k_gain, bsz, seq):
    m = bsz * seq
    n_mem = kv.shape[0] // bsz
    tm = min(ROW_TILE, seq)
    nt = seq // tm
    return pl.pallas_call(
        _xattn_kernel,
        grid=(bsz, nt),
        in_specs=[pl.BlockSpec((tm, D_XATTN), lambda b, i: (b * nt + i, 0)),
                  pl.BlockSpec((n_mem, 2 * D_XATTN), lambda b, i: (b, 0)),
                  pl.BlockSpec((1, XATTN_HEAD_DIM), lambda b, i: (0, 0)),
                  pl.BlockSpec((1, XATTN_HEAD_DIM), lambda b, i: (0, 0))],
        out_specs=pl.BlockSpec((tm, D_XATTN), lambda b, i: (b * nt + i, 0)),
        out_shape=jax.ShapeDtypeStruct((m, D_XATTN), BF16),
        compiler_params=_params("parallel", "parallel"),
        name="xattn_core",
    )(q, kv, q_gain.reshape(1, XATTN_HEAD_DIM), k_gain.reshape(1, XATTN_HEAD_DIM))


def memory_xattn(h, mem2d, norm_x, norm_m, wq, wkv, wo, q_gain, k_gain, bsz, seq):
    hq = rmsnorm(h, norm_x, BF16)
    q = matmul([(hq, wq.astype(BF16))])
    mem_n = rmsnorm(mem2d, norm_m, BF16)
    kv = matmul([(mem_n, wkv.astype(BF16))])
    o = xattn_core(q, kv, q_gain, k_gain, bsz, seq)
    return matmul([(o, wo.astype(BF16))], residual=h)


def _mix_kernel(h_ref, hp_ref, g_ref, mu_ref, *rest, tm, tiles_per_seq):
    outs, ext_ref = rest[:-1], rest[-1]
    i = pl.program_id(0)
    hn = _rms(h_ref[...]) * g_ref[...]
    hp = _rms(hp_ref[...]) * g_ref[...]
    hp = jnp.where(i % tiles_per_seq == 0, 0.0, hp)
    ext_ref[0:SUBLANES, :] = hp
    ext_ref[SUBLANES:tm + SUBLANES, :] = hn
    xx = ext_ref[SUBLANES - 1:tm + SUBLANES - 1, :] - hn
    for j, o_ref in enumerate(outs):
        o_ref[...] = (hn + xx * mu_ref[j:j + 1, :]).astype(o_ref.dtype)


def rwkv_mix(h, gain, mu, seq):
    m, d = h.shape
    tm = min(ROW_TILE, seq)
    n_mix = mu.shape[0]
    kern = functools.partial(_mix_kernel, tm=tm, tiles_per_seq=seq // tm)
    row = pl.BlockSpec((tm, d), lambda i: (i, 0))
    return pl.pallas_call(
        kern,
        grid=(m // tm,),
        in_specs=[row, pl.BlockSpec((SUBLANES, d), lambda i: (jnp.maximum(i * (tm // SUBLANES) - 1, 0), 0)),
                  pl.BlockSpec((1, d), lambda i: (0, 0)), pl.BlockSpec((n_mix, d), lambda i: (0, 0))],
        out_specs=[row] * n_mix,
        out_shape=[jax.ShapeDtypeStruct((m, d), BF16)] * n_mix,
        scratch_shapes=[pltpu.VMEM((tm + SUBLANES, d), F32)],
        compiler_params=_params("parallel"),
        name="rwkv_mix",
    )(h, h, gain.reshape(1, d), mu)


WKV_LANE_GROUP = LANES // 4
WKV_ROWS_PER_BATCH = D_MODEL // LANES


def _wkv_fused_kernel(r_ref, k_ref, v_ref, w_ref, a_ref, g_ref, kkp_ref, kap_ref, rkp_ref, lnw_ref, lnb_ref,
                      o_ref, s_ref, wd_s, kh_s, kk_s, b_s, r_s, v_s, g_s, y_s, *, steps):
    n = RWKV_HEAD_DIM
    nb = LANES // WKV_LANE_GROUP
    lane_grp = lax.broadcasted_iota(I32, (steps, SUBLANES, LANES), 2) // WKV_LANE_GROUP

    @pl.when(pl.program_id(0) == 0)
    def _():
        s_ref[...] = jnp.zeros_like(s_ref)

    def to_scan(x_ref):
        pieces = []
        for n4 in range(nb):
            for v in range(WKV_ROWS_PER_BATCH // SUBLANES):
                acc = None
                for b in range(nb):
                    x = x_ref[b, :, SUBLANES * v:SUBLANES * (v + 1), :]
                    shift = (WKV_LANE_GROUP * (b - n4)) % LANES
                    if shift:
                        x = pltpu.roll(x, shift, axis=2)
                    acc = x if acc is None else jnp.where(lane_grp == b, x, acc)
                pieces.append(acc)
        return pieces

    rs, ks, vs, ws, as_, gs = (to_scan(x) for x in (r_ref, k_ref, v_ref, w_ref, a_ref, g_ref))
    kks = [ks[u] * kkp_ref[SUBLANES * u:SUBLANES * (u + 1), :] for u in range(len(ks))]
    ss = kks[0] * kks[0]
    for u in range(1, len(kks)):
        ss = ss + kks[u] * kks[u]
    inv_norm = 1.0 / jnp.maximum(jnp.sqrt(jnp.sum(ss, axis=1, keepdims=True)), 1e-12)
    for u in range(len(ks)):
        rows = slice(SUBLANES * u, SUBLANES * (u + 1))
        kk = kks[u] * inv_norm
        wd_s[:, rows, :] = jnp.exp(-jnp.exp(-_softplus(-ws[u]) - 0.5))
        kk_s[:, rows, :] = kk
        b_s[:, rows, :] = kk * as_[u]
        kh_s[:, rows, :] = ks[u] * (1.0 + (as_[u] - 1.0) * kap_ref[rows, :])
        r_s[:, rows, :] = rs[u]
        v_s[:, rows, :] = vs[u]
        g_s[:, rows, :] = gs[u]

    q0 = jnp.zeros((n, LANES), F32)
    for j in range(n):
        q0 = q0 + s_ref[j] * kk_s[0, j:j + 1, :]

    def step(t, q):
        tn = jnp.minimum(t + 1, steps - 1)
        v = v_s[t]
        y = jnp.zeros((n, LANES), F32)
        qn = jnp.zeros((n, LANES), F32)
        for j in range(n):
            s_new = s_ref[j] * wd_s[t, j:j + 1, :] - q * b_s[t, j:j + 1, :] + v * kh_s[t, j:j + 1, :]
            s_ref[j] = s_new
            y = y + s_new * r_s[t, j:j + 1, :]
            qn = qn + s_new * kk_s[tn, j:j + 1, :]
        y_s[t] = y
        return qn

    lax.fori_loop(0, steps, step, q0)

    y = y_s[...]
    inv_n = 1.0 / n
    mean = jnp.sum(y, axis=1, keepdims=True) * inv_n
    yc = y - mean
    var = jnp.sum(yc * yc, axis=1, keepdims=True) * inv_n
    bonus = jnp.sum(r_s[...] * kh_s[...] * rkp_ref[...], axis=1, keepdims=True) * v_s[...]
    o = (yc * lax.rsqrt(var + RWKV_GN_EPS) * lnw_ref[...] + lnb_ref[...] + bonus) * g_s[...]
    for b in range(nb):
        for v in range(WKV_ROWS_PER_BATCH // SUBLANES):
            acc = None
            for n4 in range(nb):
                u = n4 * (WKV_ROWS_PER_BATCH // SUBLANES) + v
                x = o[:, SUBLANES * u:SUBLANES * (u + 1), :]
                shift = (WKV_LANE_GROUP * (n4 - b)) % LANES
                if shift:
                    x = pltpu.roll(x, shift, axis=2)
                acc = x if acc is None else jnp.where(lane_grp == n4, x, acc)
            o_ref[b, :, SUBLANES * v:SUBLANES * (v + 1), :] = acc


def _wkv_column_order():
    c, n4, h = np.meshgrid(np.arange(WKV_ROWS_PER_BATCH), np.arange(LANES // WKV_LANE_GROUP), np.arange(RWKV_HEADS),
                           indexing="ij")
    return (h * RWKV_HEAD_DIM + n4 * WKV_ROWS_PER_BATCH + c).reshape(-1)


def _scan_tile(p, bsz):
    return jnp.tile(p.reshape(RWKV_HEADS, RWKV_HEAD_DIM).T, (1, bsz))


def wkv_fused(r, k, v, w_raw, a_sig, g, k_k, k_a, r_k, ln_w, ln_b, bsz, seq):
    assert bsz * RWKV_HEADS == LANES and RWKV_HEADS == WKV_LANE_GROUP
    steps = min(WKV_T, seq)
    view = lambda x: x.reshape(bsz, seq, WKV_ROWS_PER_BATCH, LANES)
    blk = pl.BlockSpec((bsz, steps, WKV_ROWS_PER_BATCH, LANES), lambda i: (0, i, 0, 0))
    par = pl.BlockSpec((RWKV_HEAD_DIM, LANES), lambda i: (0, 0))
    scr = pltpu.VMEM((steps, RWKV_HEAD_DIM, LANES), F32)
    out = pl.pallas_call(
        functools.partial(_wkv_fused_kernel, steps=steps),
        grid=(seq // steps,),
        in_specs=[blk] * 6 + [par] * 5,
        out_specs=blk,
        out_shape=jax.ShapeDtypeStruct((bsz, seq, WKV_ROWS_PER_BATCH, LANES), F32),
        scratch_shapes=[pltpu.VMEM((RWKV_HEAD_DIM, RWKV_HEAD_DIM, LANES), F32)] + [scr] * 8,
        compiler_params=_params("arbitrary"),
        name="wkv_fused",
    )(view(r), view(k), view(v), view(w_raw), view(a_sig), view(g),
      _scan_tile(k_k, bsz), _scan_tile(k_a, bsz), _scan_tile(r_k, bsz), _scan_tile(ln_w, bsz), _scan_tile(ln_b, bsz))
    return out.reshape(bsz * seq, D_MODEL)


def _router_kernel(h_ref, g_ref, wr_ref, info_ref, tile_cnt_ref, total_ref, carry_ref, *, tt):
    E = N_EXPERTS
    i = pl.program_id(0)

    @pl.when(i == 0)
    def _():
        carry_ref[...] = jnp.zeros_like(carry_ref)

    xh, xm, xl = _split3(_rms(h_ref[...]) * g_ref[...])
    wh, wm, wl = _split3(wr_ref[...])
    logits = (_dot_nt(wh, xh) + _dot_nt(wh, xm) + _dot_nt(wm, xh)
              + _dot_nt(wh, xl) + _dot_nt(wl, xh) + _dot_nt(wm, xm))
    eidx = lax.broadcasted_iota(I32, (E, tt), 0)
    v1 = jnp.max(logits, axis=0, keepdims=True)
    i1 = jnp.min(jnp.where(logits == v1, eidx, E), axis=0, keepdims=True)
    rest = jnp.where(eidx == i1, -jnp.inf, logits)
    v2 = jnp.max(rest, axis=0, keepdims=True)
    i2 = jnp.min(jnp.where(rest == v2, eidx, E), axis=0, keepdims=True)
    e2 = jnp.exp(v2 - v1)
    g1 = 1.0 / (1.0 + e2)
    g2 = e2 / (1.0 + e2)
    hit1 = eidx == i1
    hit2 = eidx == i2
    onehot = jnp.where(hit1 | hit2, 1.0, 0.0)
    rr = lax.broadcasted_iota(I32, (tt, tt), 0)
    cc = lax.broadcasted_iota(I32, (tt, tt), 1)
    before = jnp.where(rr < cc, 1.0, 0.0).astype(BF16)
    carry = carry_ref[...]
    carry_wide = jnp.concatenate([carry] * (tt // LANES), axis=1)
    cum = _dot(onehot.astype(BF16), before) + carry_wide
    rank1 = jnp.sum(jnp.where(hit1, cum, 0.0), axis=0, keepdims=True)
    rank2 = jnp.sum(jnp.where(hit2, cum, 0.0), axis=0, keepdims=True)
    zero = jnp.zeros((1, tt), F32)
    info_ref[...] = jnp.concatenate([i1.astype(F32), i2.astype(F32), rank1, rank2, g1, g2, zero, zero], axis=0)
    tile_cnt_ref[0] = carry
    carry = carry + _dot(onehot.astype(BF16), jnp.ones((tt, LANES), BF16))
    carry_ref[...] = carry
    total_ref[...] = carry


def moe_router(h, norm_gain, w_router):
    m, d = h.shape
    tt = min(MOE_TILE, m)
    nt = m // tt
    return pl.pallas_call(
        functools.partial(_router_kernel, tt=tt),
        grid=(nt,),
        in_specs=[pl.BlockSpec((tt, d), lambda i: (i, 0)), pl.BlockSpec((1, d), lambda i: (0, 0)),
                  pl.BlockSpec((N_EXPERTS, d), lambda i: (0, 0))],
        out_specs=[pl.BlockSpec((SUBLANES, tt), lambda i: (0, i)),
                   pl.BlockSpec((1, N_EXPERTS, LANES), lambda i: (i, 0, 0)),
                   pl.BlockSpec((N_EXPERTS, LANES), lambda i: (0, 0))],
        out_shape=[jax.ShapeDtypeStruct((SUBLANES, m), F32), jax.ShapeDtypeStruct((nt, N_EXPERTS, LANES), F32),
                   jax.ShapeDtypeStruct((N_EXPERTS, LANES), F32)],
        scratch_shapes=[pltpu.VMEM((N_EXPERTS, LANES), F32)],
        compiler_params=_params("arbitrary"),
        name="moe_router",
    )(h, norm_gain.reshape(1, d), w_router.T)


def _slots_kernel(start_ref, info_ref, o_ref):
    info = info_ref[...]
    e1, e2, r1, r2 = info[0:1], info[1:2], info[2:3], info[3:4]
    s1 = r1
    s2 = r2
    for e in range(N_EXPERTS):
        st = start_ref[e].astype(F32)
        s1 = s1 + jnp.where(e1 == float(e), st, 0.0)
        s2 = s2 + jnp.where(e2 == float(e), st, 0.0)
    o_ref[...] = jnp.concatenate([s1, s2, info[4:5], info[5:6], info[6:8], info[6:8]], axis=0)


def moe_slots(info, slot_start):
    m = info.shape[1]
    tt = min(2048, m)
    return pl.pallas_call(
        _slots_kernel,
        grid_spec=pltpu.PrefetchScalarGridSpec(
            num_scalar_prefetch=1, grid=(m // tt,),
            in_specs=[pl.BlockSpec((SUBLANES, tt), lambda i, st: (0, i))],
            out_specs=pl.BlockSpec((SUBLANES, tt), lambda i, st: (0, i))),
        out_shape=jax.ShapeDtypeStruct((SUBLANES, m), F32),
        compiler_params=_params("parallel"),
        name="moe_slots",
    )(slot_start, info)


def _dispatch_kernel(blk_ref, til_ref, flg_ref, slots_ref, x_ref, o_ref, acc_ref, *, bs):
    w = pl.program_id(0)
    flags = flg_ref[w]

    @pl.when((flags & 1) != 0)
    def _():
        acc_ref[...] = jnp.zeros_like(acc_ref)

    @pl.when((flags & 4) != 0)
    def _():
        sl = slots_ref[...]
        sid = (blk_ref[w] * bs + lax.broadcasted_iota(I32, (bs, sl.shape[1]), 0)).astype(F32)
        onehot = jnp.where((sid == sl[0:1, :]) | (sid == sl[1:2, :]), 1.0, 0.0).astype(BF16)
        acc_ref[...] += _dot(onehot, x_ref[...])

    @pl.when((flags & 2) != 0)
    def _():
        o_ref[...] = acc_ref[...].astype(o_ref.dtype)


def moe_dispatch(xn, slots, wl_blk, wl_til, wl_flg, n_slots):
    m, d = xn.shape
    bs = tt = MOE_TILE
    n_items = wl_blk.shape[0]
    return pl.pallas_call(
        functools.partial(_dispatch_kernel, bs=bs),
        grid_spec=pltpu.PrefetchScalarGridSpec(
            num_scalar_prefetch=3, grid=(n_items,),
            in_specs=[pl.BlockSpec((SUBLANES, tt), lambda w, b, t, f: (0, t[w])),
                      pl.BlockSpec((tt, d), lambda w, b, t, f: (t[w], 0))],
            out_specs=pl.BlockSpec((bs, d), lambda w, b, t, f: (b[w], 0)),
            scratch_shapes=[pltpu.VMEM((bs, d), F32)]),
        out_shape=jax.ShapeDtypeStruct((n_slots, d), BF16),
        compiler_params=_params("arbitrary"),
        name="moe_dispatch",
    )(wl_blk, wl_til, wl_flg, slots, xn)


def _moe_ffn_kernel(be_ref, nb_ref, x_ref, w1_ref, w3_ref, w2_ref, o_ref, acc_ref):
    b = pl.program_id(0)
    f = pl.program_id(1)

    @pl.when(f == 0)
    def _():
        acc_ref[...] = jnp.zeros_like(acc_ref)

    @pl.when(b < nb_ref[0])
    def _():
        x = x_ref[...]
        h = _silu(_dot(x, w1_ref[0].astype(BF16))) * _dot(x, w3_ref[0].astype(BF16))
        acc_ref[...] += _dot(h.astype(BF16), w2_ref[0].astype(BF16))

    @pl.when(f == pl.num_programs(1) - 1)
    def _():
        o_ref[...] = acc_ref[...].astype(o_ref.dtype)


def moe_ffn(xs, w1, w3, w2, block_expert, n_used):
    n_slots, d = xs.shape
    ff = w1.shape[2]
    bm = MOE_FFN_ROWS
    tf = MOE_TF
    nf = ff // tf

    def f_of(b, f, nb):
        return jnp.where(b < nb[0], f, nf - 1)

    return pl.pallas_call(
        _moe_ffn_kernel,
        grid_spec=pltpu.PrefetchScalarGridSpec(
            num_scalar_prefetch=2, grid=(n_slots // bm, nf),
            in_specs=[pl.BlockSpec((bm, d), lambda b, f, be, nb: (b, 0)),
                      pl.BlockSpec((1, d, tf), lambda b, f, be, nb: (be[b], 0, f_of(b, f, nb))),
                      pl.BlockSpec((1, d, tf), lambda b, f, be, nb: (be[b], 0, f_of(b, f, nb))),
                      pl.BlockSpec((1, tf, d), lambda b, f, be, nb: (be[b], f_of(b, f, nb), 0))],
            out_specs=pl.BlockSpec((bm, d), lambda b, f, be, nb: (b, 0)),
            scratch_shapes=[pltpu.VMEM((bm, d), F32)]),
        out_shape=jax.ShapeDtypeStruct((n_slots, d), BF16),
        compiler_params=_params("arbitrary", "arbitrary"),
        name="moe_ffn",
    )(block_expert, n_used, xs, w1, w3, w2)


def _combine_kernel(til_ref, blk_ref, flg_ref, tok_ref, y_ref, res_ref, o_ref, acc_ref, *, bs):
    w = pl.program_id(0)
    flags = flg_ref[w]

    @pl.when((flags & 1) != 0)
    def _():
        acc_ref[...] = res_ref[...]

    @pl.when((flags & 4) != 0)
    def _():
        tok = tok_ref[...]
        tt = tok.shape[0]
        sid = (blk_ref[w] * bs + lax.broadcasted_iota(I32, (tt, bs), 1)).astype(F32)
        y = y_ref[...]
        oh1 = jnp.where(sid == tok[:, 0:1], 1.0, 0.0).astype(BF16)
        oh2 = jnp.where(sid == tok[:, 1:2], 1.0, 0.0).astype(BF16)
        acc_ref[...] += tok[:, 2:3] * _dot(oh1, y) + tok[:, 3:4] * _dot(oh2, y)

    @pl.when((flags & 2) != 0)
    def _():
        o_ref[...] = acc_ref[...]


def moe_combine(y, tok_info, residual, wl_til, wl_blk, wl_flg):
    m, d = residual.shape
    bs = tt = MOE_TILE
    n_items = wl_til.shape[0]
    return pl.pallas_call(
        functools.partial(_combine_kernel, bs=bs),
        grid_spec=pltpu.PrefetchScalarGridSpec(
            num_scalar_prefetch=3, grid=(n_items,),
            in_specs=[pl.BlockSpec((tt, SUBLANES), lambda w, t, b, f: (t[w], 0)),
                      pl.BlockSpec((bs, d), lambda w, t, b, f: (b[w], 0)),
                      pl.BlockSpec((tt, d), lambda w, t, b, f: (t[w], 0))],
            out_specs=pl.BlockSpec((tt, d), lambda w, t, b, f: (t[w], 0)),
            scratch_shapes=[pltpu.VMEM((tt, d), F32)]),
        out_shape=jax.ShapeDtypeStruct((m, d), F32),
        compiler_params=_params("arbitrary"),
        name="moe_combine",
    )(wl_til, wl_blk, wl_flg, tok_info, y, residual)


def _work_list(overlap, n_items):
    n_major, n_minor = overlap.shape
    has = jnp.any(overlap, axis=1)
    marked = jnp.concatenate([overlap[:, :1] | ~has[:, None], overlap[:, 1:]], axis=1)
    maj, mino = jnp.nonzero(marked, size=n_items, fill_value=-1)
    valid = maj >= 0
    real = valid & overlap[jnp.maximum(maj, 0), jnp.maximum(mino, 0)]
    prev = jnp.concatenate([jnp.full((1,), -2, maj.dtype), maj[:-1]])
    nxt = jnp.concatenate([maj[1:], jnp.full((1,), -2, maj.dtype)])
    flags = (valid & (maj != prev)) * 1 + (valid & (maj != nxt)) * 2 + real * 4
    last_maj = jnp.max(jnp.where(valid, maj, 0))
    last_min = jnp.sum(jnp.where(valid & (maj == last_maj) & (maj != nxt), mino, 0))
    maj = jnp.where(valid, maj, last_maj)
    mino = jnp.where(valid, mino, last_min)
    return maj.astype(I32), mino.astype(I32), flags.astype(I32)


def moe_layer(h, norm_gain, w_router, w1, w3, w2):
    m, d = h.shape
    E = N_EXPERTS
    tile = min(MOE_TILE, m)
    n_tiles = m // tile
    ffn_rows = MOE_FFN_ROWS
    xn = rmsnorm(h, norm_gain, BF16)
    info, tile_cnt, total = moe_router(h, norm_gain, w_router)
    before = tile_cnt[:, :, 0]
    counts = total[:, 0].astype(I32)
    padded = (counts + ffn_rows - 1) // ffn_rows * ffn_rows
    ends = jnp.cumsum(padded)
    starts = ends - padded
    n_slots = (m * TOP_K // ffn_rows + E) * ffn_rows
    n_ffn_blocks = n_slots // ffn_rows
    n_dblocks = n_slots // tile
    slots = moe_slots(info, starts.astype(I32))
    db_start = jnp.arange(n_dblocks, dtype=I32) * tile
    db_e = jnp.minimum(jnp.sum(db_start[:, None] >= ends[None, :], axis=1), E - 1)
    r0 = db_start - starts[db_e]
    r1 = jnp.minimum(r0 + tile, counts[db_e])
    after = jnp.concatenate([before[1:], counts[None, :].astype(F32)], axis=0).astype(I32)
    bef = before.astype(I32)
    ov = (bef[:, db_e].T < r1[:, None]) & (after[:, db_e].T > r0[:, None]) & (db_start < ends[-1])[:, None]
    n_items = n_dblocks + E * n_tiles
    d_blk, d_til, d_flg = _work_list(ov, n_items)
    c_til, c_blk, c_flg = _work_list(ov.T, n_items)
    xs = moe_dispatch(xn, slots, d_blk, d_til, d_flg, n_slots)
    fb_start = jnp.arange(n_ffn_blocks, dtype=I32) * ffn_rows
    block_expert = jnp.minimum(jnp.sum(fb_start[:, None] >= ends[None, :], axis=1), E - 1).astype(I32)
    n_used = (ends[-1] // ffn_rows).astype(I32).reshape(1)
    ys = moe_ffn(xs, w1, w3, w2, block_expert, n_used)
    return moe_combine(ys, slots.T, h, c_til, c_blk, c_flg)


def even_mixer_layer(h, posf, p, bsz, seq):
    m = bsz * seq
    hn = rmsnorm(h, p["norm_mix"], BF16)
    w_in = p["w_in"]
    c = np.cumsum([0, D_SSM, D_CONV, SSM_HEADS, D_NSA] + [D_NSA_KV] * 6 + [3 * NSA_HEADS])
    z = matmul([(hn, w_in[:, c[0]:c[1]].astype(BF16))])
    xbc = matmul([(hn, w_in[:, c[1]:c[2]].astype(BF16))])
    q = matmul([(hn, w_in[:, c[3]:c[4]].astype(BF16))])
    kv6 = matmul([(hn, w_in[:, c[4]:c[10]].astype(BF16))])
    w_small = jnp.concatenate([w_in[:, c[2]:c[3]], w_in[:, c[10]:c[11]]], axis=1)
    w_small = jnp.pad(w_small, ((0, 0), (0, LANES - w_small.shape[1]))).astype(BF16)
    small = matmul([(hn, w_small)])
    y_ssm = ssd_group(z, xbc, small, p["conv_w"], p["conv_b"], p["dt_bias"], p["a_log"], p["d_skip"],
                      p["ssm_norm"], bsz, seq)
    qn, ksn, kwn = rope_norm(posf, q, kv6, p["q_gain"], p["ks_gain"], p["kw_gain"])
    nch = seq // CMP_STRIDE

    def chunks(col0):
        u = kv6[:, col0:col0 + D_NSA_KV].reshape(bsz, nch, CMP_STRIDE, NSA_KV_HEADS, NSA_HEAD_DIM)
        return u.transpose(0, 3, 1, 2, 4).reshape(bsz, NSA_KV_HEADS, nch, CMP_STRIDE * NSA_HEAD_DIM)

    kc, vc = compress(chunks(0), chunks(D_NSA_KV), p["pe_k"], p["pe_v"], p["wk1"], p["wk2"], p["wv1"], p["wv2"],
                      p["kc_gain"])
    o_cmp, sel = cmp_select(qn, kc, vc, bsz, seq)
    o_sel = masked_attention(qn, ksn, 0, kv6, 3 * NSA_KV_HEADS, bsz, seq, "sel", sel)
    o_win = masked_attention(qn, kwn, 0, kv6, 5 * NSA_KV_HEADS, bsz, seq, "win")
    y_nsa = nsa_combine(small, o_cmp, o_sel, o_win)
    w_out = p["w_out"].astype(BF16)
    return matmul([(y_ssm, w_out[:D_SSM]), (y_nsa, w_out[D_SSM:])], residual=h)


def rwkv_layer(h, p, bsz, seq):
    m, d = h.shape
    mixes = rwkv_mix(h, p["norm_mix"], p["mu"], seq)
    bf = lambda w: w.astype(BF16)

    def lora_pad(w_a, w_b):
        r = w_a.shape[1]
        rp = -(-r // LANES) * LANES
        return bf(jnp.pad(w_a, ((0, 0), (0, rp - r)))), bf(jnp.pad(w_b, ((0, rp - r), (0, 0))))

    perm = _wkv_column_order()
    r = matmul([(mixes[0], bf(p["w_r"][:, perm]))])
    k = matmul([(mixes[2], bf(p["w_k"][:, perm]))])
    v = matmul([(mixes[3], bf(p["w_v"][:, perm]))])
    w1, w2 = lora_pad(p["w1"], p["w2"][:, perm])
    w_raw = matmul([(matmul([(mixes[1], w1)], act="tanh", out_dtype=BF16), w2)], bias=p["w0"][perm])
    a1, a2 = lora_pad(p["a1"], p["a2"][:, perm])
    a_sig = matmul([(matmul([(mixes[4], a1)], out_dtype=BF16), a2)], bias=p["a0"][perm], act="sigmoid")
    g1, g2 = lora_pad(p["g1"], p["g2"][:, perm])
    g = matmul([(matmul([(mixes[5], g1)], act="sigmoid", out_dtype=BF16), g2)])
    o = wkv_fused(r, k, v, w_raw, a_sig, g, p["k_k"], p["k_a"], p["r_k"].reshape(-1), p["ln_w"], p["ln_b"], bsz, seq)
    return matmul([(o, bf(p["w_o"][perm, :]))], residual=h)


def kernel(x, mem, positions, norm_mix, norm_xattn, norm_mem, norm_ffn, xattn_wq, xattn_wkv, xattn_wo, xattn_q_gain, xattn_k_gain, ev_w_in, ev_conv_w, ev_conv_b, ev_dt_bias, ev_a_log, ev_d_skip, ev_ssm_norm, ev_q_gain, ev_kc_gain, ev_ks_gain, ev_kw_gain, ev_pe_k, ev_pe_v, ev_cmp_wk1, ev_cmp_wk2, ev_cmp_wv1, ev_cmp_wv2, ev_w_out, ev_ffn_w1, ev_ffn_w3, ev_ffn_w2, od_mu, od_w_r, od_w_k, od_w_v, od_w_o, od_w0, od_w1, od_w2, od_a0, od_a1, od_a2, od_g1, od_g2, od_k_k, od_k_a, od_r_k, od_ln_w, od_ln_b, od_router, od_moe_w1, od_moe_w3, od_moe_w2):
    bsz, seq, d = x.shape
    m = bsz * seq
    depth = norm_mix.shape[0]
    h = x.reshape(m, d)
    mem2d = mem.reshape(-1, d)
    posf = positions.astype(F32).reshape(m, 1)
    for layer in range(depth):
        i = layer // 2
        if layer % 2 == 0:
            p = dict(norm_mix=norm_mix[layer], w_in=ev_w_in[i], conv_w=ev_conv_w[i], conv_b=ev_conv_b[i],
                     dt_bias=ev_dt_bias[i], a_log=ev_a_log[i], d_skip=ev_d_skip[i], ssm_norm=ev_ssm_norm[i],
                     q_gain=ev_q_gain[i], kc_gain=ev_kc_gain[i], ks_gain=ev_ks_gain[i], kw_gain=ev_kw_gain[i],
                     pe_k=ev_pe_k[i], pe_v=ev_pe_v[i], wk1=ev_cmp_wk1[i], wk2=ev_cmp_wk2[i], wv1=ev_cmp_wv1[i],
                     wv2=ev_cmp_wv2[i], w_out=ev_w_out[i])
            h = even_mixer_layer(h, posf, p, bsz, seq)
        else:
            p = dict(norm_mix=norm_mix[layer], mu=od_mu[i], w_r=od_w_r[i], w_k=od_w_k[i], w_v=od_w_v[i], w_o=od_w_o[i],
                     w0=od_w0[i], w1=od_w1[i], w2=od_w2[i], a0=od_a0[i], a1=od_a1[i], a2=od_a2[i], g1=od_g1[i],
                     g2=od_g2[i], k_k=od_k_k[i], k_a=od_k_a[i], r_k=od_r_k[i], ln_w=od_ln_w[i], ln_b=od_ln_b[i])
            h = rwkv_layer(h, p, bsz, seq)
        h = memory_xattn(h, mem2d, norm_xattn[layer], norm_mem[layer], xattn_wq[layer], xattn_wkv[layer],
                         xattn_wo[layer], xattn_q_gain[layer], xattn_k_gain[layer], bsz, seq)
        if layer % 2 == 0:
            hn = rmsnorm(h, norm_ffn[layer], BF16)
            h = swiglu_ffn(hn, ev_ffn_w1[i].astype(BF16), ev_ffn_w3[i].astype(BF16), ev_ffn_w2[i].astype(BF16), h)
        else:
            h = moe_layer(h, norm_ffn[layer], od_router[i], od_moe_w1[i], od_moe_w3[i], od_moe_w2[i])
    return h.reshape(bsz, seq, d)
```

```python
import functools
import math

import numpy as np
import jax
import jax.numpy as jnp
from jax import lax
from jax.experimental import pallas as pl
from jax.experimental.pallas import tpu as pltpu

F32 = jnp.float32
BF16 = jnp.bfloat16
I32 = jnp.int32

D_MODEL = 2048
NORM_EPS = 1e-6
NEG_BIG = -1e30

SSM_HEADS = 32
SSM_HEAD_DIM = 64
D_SSM = SSM_HEADS * SSM_HEAD_DIM
SSM_GROUPS = 4
D_STATE = 128
CONV_WIDTH = 4
SSD_CHUNK = 128
D_CONV = D_SSM + 2 * SSM_GROUPS * D_STATE
D_SSM_GROUP = D_SSM // SSM_GROUPS

NSA_HEADS = 16
NSA_KV_HEADS = 4
NSA_HEAD_DIM = 128
NSA_Q_PER_KV = NSA_HEADS // NSA_KV_HEADS
D_NSA = NSA_HEADS * NSA_HEAD_DIM
D_NSA_KV = NSA_KV_HEADS * NSA_HEAD_DIM
CMP_BLOCK = 32
CMP_STRIDE = 16
CMP_HIDDEN = 256
SEL_BLOCK = 64
N_SELECT = 16
WINDOW = 512
ROPE_DIM = NSA_HEAD_DIM // 4
ROPE_THETA = 500000.0

RWKV_HEAD_DIM = 64
RWKV_HEADS = D_MODEL // RWKV_HEAD_DIM
RWKV_GN_EPS = 1e-5 * RWKV_HEAD_DIM

XATTN_HEADS = 4
XATTN_HEAD_DIM = 128
D_XATTN = XATTN_HEADS * XATTN_HEAD_DIM

N_EXPERTS = 8
TOP_K = 2

LANES = 128
SUBLANES = 8
VMEM_LIMIT_BYTES = 56 * 1024 * 1024

ROW_TILE = 512
MM_TM = 2048
MM_TN = 512
FFN_TM = 512
FFN_TF = 512
MOE_FFN_ROWS = 1024
MOE_TF = 256
COMBINE_TM = 256
MOE_TILE = 256
ATT_TQ = 256
ATT_TK = 512
LOG2E = 1.4426950408889634
CMP_TQ = 128
WKV_T = 32


def _params(*sem):
    return pltpu.CompilerParams(dimension_semantics=sem, vmem_limit_bytes=VMEM_LIMIT_BYTES)


def _sigmoid(x):
    return 1.0 / (1.0 + jnp.exp(-x))


def _silu(x):
    return x * _sigmoid(x)


def _softplus(x):
    return jnp.maximum(x, 0.0) + jnp.log(1.0 + jnp.exp(-jnp.abs(x)))


def _split3(x):
    h = x.astype(BF16)
    r = x - h.astype(F32)
    m = r.astype(BF16)
    l = (r - m.astype(F32)).astype(BF16)
    return h, m, l


def _dot(a, b):
    return jnp.dot(a, b, preferred_element_type=F32)


def _dot_nt(a, b):
    return lax.dot_general(a, b, (((1,), (1,)), ((), ())), preferred_element_type=F32)


def _dot_f32_sel(x, e):
    h, m, l = _split3(x)
    return _dot(h, e) + _dot(m, e) + _dot(l, e)


def _sel_dot_f32(e, x):
    h, m, l = _split3(x)
    return _dot(e, h) + _dot(e, m) + _dot(e, l)


def _rms(x, eps=NORM_EPS):
    return x * lax.rsqrt(jnp.mean(x * x, axis=-1, keepdims=True) + eps)


def _rmsnorm_kernel(x_ref, g_ref, o_ref):
    o_ref[...] = (_rms(x_ref[...]) * g_ref[...]).astype(o_ref.dtype)


def rmsnorm(x, gain, out_dtype):
    m, d = x.shape
    tm = min(ROW_TILE, m)
    return pl.pallas_call(
        _rmsnorm_kernel,
        grid=(m // tm,),
        in_specs=[pl.BlockSpec((tm, d), lambda i: (i, 0)), pl.BlockSpec((1, d), lambda i: (0, 0))],
        out_specs=pl.BlockSpec((tm, d), lambda i: (i, 0)),
        out_shape=jax.ShapeDtypeStruct((m, d), out_dtype),
        compiler_params=_params("parallel"),
        name="rmsnorm",
    )(x, gain.reshape(1, d))


def _mm_kernel(*refs, n_pairs, has_bias, has_res, act):
    o_ref = refs[-1]
    acc = None
    for p in range(n_pairs):
        a = refs[2 * p][...].astype(BF16)
        d = _dot(a, refs[2 * p + 1][...])
        acc = d if acc is None else acc + d
    k = 2 * n_pairs
    if has_bias:
        acc = acc + refs[k][...]
        k += 1
    if act == "tanh":
        acc = jnp.tanh(acc)
    elif act == "sigmoid":
        acc = _sigmoid(acc)
    if has_res:
        acc = acc + refs[k][...]
    o_ref[...] = acc.astype(o_ref.dtype)


def matmul(pairs, bias=None, residual=None, act=None, out_dtype=F32):
    m = pairs[0][0].shape[0]
    n = pairs[0][1].shape[1]
    k_total = sum(a.shape[1] for a, _ in pairs)
    tm = min(MM_TM if k_total <= D_MODEL else MM_TM // 2, m)
    tn = MM_TN if n % MM_TN == 0 else (256 if n % 256 == 0 else LANES)
    tn = min(tn, n)
    in_specs, args = [], []
    for a, w in pairs:
        kk = a.shape[1]
        in_specs += [pl.BlockSpec((tm, kk), lambda i, j: (i, 0)), pl.BlockSpec((kk, tn), lambda i, j: (0, j))]
        args += [a, w]
    if bias is not None:
        in_specs.append(pl.BlockSpec((1, tn), lambda i, j: (0, j)))
        args.append(bias.reshape(1, n).astype(F32))
    if residual is not None:
        in_specs.append(pl.BlockSpec((tm, tn), lambda i, j: (i, j)))
        args.append(residual)
    kern = functools.partial(_mm_kernel, n_pairs=len(pairs), has_bias=bias is not None,
                             has_res=residual is not None, act=act)
    return pl.pallas_call(
        kern,
        grid=(m // tm, n // tn),
        in_specs=in_specs,
        out_specs=pl.BlockSpec((tm, tn), lambda i, j: (i, j)),
        out_shape=jax.ShapeDtypeStruct((m, n), out_dtype),
        compiler_params=_params("parallel", "parallel"),
        name="matmul",
    )(*args)


def _ffn_kernel(x_ref, w1_ref, w3_ref, w2_ref, res_ref, o_ref):
    f = pl.program_id(1)

    @pl.when(f == 0)
    def _():
        o_ref[...] = res_ref[...]

    x = x_ref[...]
    h = _silu(_dot(x, w1_ref[...])) * _dot(x, w3_ref[...])
    o_ref[...] += _dot(h.astype(BF16), w2_ref[...])


def swiglu_ffn(x, w1, w3, w2, residual):
    m, d = x.shape
    ff = w1.shape[1]
    tm = min(FFN_TM, m)
    tf = FFN_TF
    return pl.pallas_call(
        _ffn_kernel,
        grid=(m // tm, ff // tf),
        in_specs=[pl.BlockSpec((tm, d), lambda i, f: (i, 0)),
                  pl.BlockSpec((d, tf), lambda i, f: (0, f)),
                  pl.BlockSpec((d, tf), lambda i, f: (0, f)),
                  pl.BlockSpec((tf, d), lambda i, f: (f, 0)),
                  pl.BlockSpec((tm, d), lambda i, f: (i, 0))],
        out_specs=pl.BlockSpec((tm, d), lambda i, f: (i, 0)),
        out_shape=jax.ShapeDtypeStruct((m, d), F32),
        compiler_params=_params("parallel", "arbitrary"),
        name="swiglu_ffn",
    )(x, w1, w3, w2, residual)


def _ssd_kernel(z_ref, xbc_ref, dt_ref, cw_ref, cb_ref, dtb_ref, alog_ref, dskip_ref, nw_ref, eh_ref,
                o_ref, ext_ref, st_ref):
    L = SSD_CHUNK
    G = SSM_GROUPS
    GW = D_SSM_GROUP
    c = pl.program_id(1)

    @pl.when(c == 0)
    def _():
        ext_ref[L:L + SUBLANES, :] = jnp.zeros((SUBLANES, D_CONV), F32)
        st_ref[...] = jnp.zeros_like(st_ref)

    ext_ref[0:SUBLANES, :] = ext_ref[L:L + SUBLANES, :]
    ext_ref[SUBLANES:L + SUBLANES, :] = xbc_ref[...]
    acc = jnp.zeros((L, D_CONV), F32) + cb_ref[...]
    for k in range(CONV_WIDTH):
        off = SUBLANES - (CONV_WIDTH - 1) + k
        acc = acc + cw_ref[k:k + 1, :] * ext_ref[off:off + L, :]
    xc = _silu(acc)

    dt = _softplus(dt_ref[...] + dtb_ref[...])
    a = -jnp.exp(alog_ref[...])
    adt = dt * a
    row = lax.broadcasted_iota(I32, (L, L), 0)
    col = lax.broadcasted_iota(I32, (L, L), 1)
    causal = row >= col
    tril = jnp.where(causal, 1.0, 0.0).astype(BF16)
    a_cum = _sel_dot_f32(tril, adt)
    a_cum_t = a_cum.T
    eh = eh_ref[...]
    acx = _dot_f32_sel(a_cum, eh)
    dtx = _dot_f32_sel(dt, eh)
    a_end = acx[L - 1:L, :]
    xs = xc[:, :D_SSM]
    xdt = xs * dtx
    xdte = (xdt * jnp.exp(a_end - acx)).astype(BF16)
    from_start = jnp.exp(acx)
    lane = lax.broadcasted_iota(I32, (L, LANES), 1)
    lo_half = lane < SSM_HEAD_DIM
    z = z_ref[...]

    for g in range(G):
        bm = xc[:, D_SSM + g * D_STATE:D_SSM + (g + 1) * D_STATE]
        cm = xc[:, D_SSM + G * D_STATE + g * D_STATE:D_SSM + G * D_STATE + (g + 1) * D_STATE]
        bm_b = bm.astype(BF16)
        cm_b = cm.astype(BF16)
        cb = _dot_nt(cm_b, bm_b)
        st = st_ref[g]
        y_off = _dot(cm_b, st.astype(BF16)) * from_start[:, g * GW:(g + 1) * GW]
        pieces = []
        for pp in range(GW // LANES):
            h0 = g * (GW // SSM_HEAD_DIM) + 2 * pp
            ms = []
            for hh in (h0, h0 + 1):
                diff = a_cum[:, hh:hh + 1] - a_cum_t[hh:hh + 1, :]
                dec = jnp.exp(jnp.where(causal, diff, NEG_BIG))
                ms.append((cb * dec).astype(BF16))
            lhs = jnp.concatenate(ms, axis=1)
            c0 = g * GW + pp * LANES
            xp = xdt[:, c0:c0 + LANES]
            rhs = jnp.concatenate([jnp.where(lo_half, xp, 0.0), jnp.where(lo_half, 0.0, xp)], axis=0)
            pieces.append(_dot(lhs, rhs.astype(BF16)))
        y = jnp.concatenate(pieces, axis=1) + y_off
        y = y + xs[:, g * GW:(g + 1) * GW] * dskip_ref[:, g * GW:(g + 1) * GW]
        y = y * _silu(z[:, g * GW:(g + 1) * GW])
        y = _rms(y) * nw_ref[:, g * GW:(g + 1) * GW]
        o_ref[:, g * GW:(g + 1) * GW] = y.astype(o_ref.dtype)
        s_new = _dot(bm.T.astype(BF16), xdte[:, g * GW:(g + 1) * GW])
        st_ref[g] = st * jnp.exp(a_end[:, g * GW:(g + 1) * GW]) + s_new


def ssd_group(z, xbc, small, conv_w, conv_b, dt_bias, a_log, d_skip, norm_w, bsz, seq):
    m = bsz * seq
    L = SSD_CHUNK
    nc = seq // L
    pad = LANES - SSM_HEADS
    eh = np.zeros((LANES, D_SSM), np.float32)
    for h in range(SSM_HEADS):
        eh[h, h * SSM_HEAD_DIM:(h + 1) * SSM_HEAD_DIM] = 1.0
    row_spec = lambda w: pl.BlockSpec((L, w), lambda b, c: (b * nc + c, 0))
    const = lambda r, w: pl.BlockSpec((r, w), lambda b, c: (0, 0))
    return pl.pallas_call(
        _ssd_kernel,
        grid=(bsz, nc),
        in_specs=[row_spec(D_SSM), row_spec(D_CONV), row_spec(LANES),
                  const(CONV_WIDTH, D_CONV), const(1, D_CONV), const(1, LANES), const(1, LANES),
                  const(1, D_SSM), const(1, D_SSM), const(LANES, D_SSM)],
        out_specs=row_spec(D_SSM),
        out_shape=jax.ShapeDtypeStruct((m, D_SSM), BF16),
        scratch_shapes=[pltpu.VMEM((L + SUBLANES, D_CONV), F32),
                        pltpu.VMEM((SSM_GROUPS, D_STATE, D_SSM_GROUP), F32)],
        compiler_params=_params("parallel", "arbitrary"),
        name="ssd_group",
    )(z, xbc, small, conv_w.T, conv_b.reshape(1, D_CONV),
      jnp.pad(dt_bias, (0, pad)).reshape(1, LANES), jnp.pad(a_log, (0, pad)).reshape(1, LANES),
      jnp.repeat(d_skip, SSM_HEAD_DIM).reshape(1, D_SSM), norm_w.reshape(1, D_SSM), jnp.asarray(eh, BF16))


def _rope_norm_kernel(pos_ref, invf_ref, q_ref, kv_ref, qg_ref, ksg_ref, kwg_ref, qo_ref, kso_ref, kwo_ref):
    ang = pos_ref[...] * invf_ref[...]
    cos = jnp.cos(ang)
    sin = jnp.sin(ang)
    lane = lax.broadcasted_iota(I32, ang.shape, 1)
    first = lane < ROPE_DIM // 2
    sin_signed = jnp.where(first, -sin, sin)

    def head(x, gain):
        y = _rms(x) * gain
        partner = jnp.where(first, pltpu.roll(y, LANES - ROPE_DIM // 2, axis=1), pltpu.roll(y, ROPE_DIM // 2, axis=1))
        return y * cos + partner * sin_signed

    hd = NSA_HEAD_DIM
    scale = hd ** -0.5 * LOG2E
    for h in range(NSA_HEADS):
        qo_ref[:, h * hd:(h + 1) * hd] = (head(q_ref[:, h * hd:(h + 1) * hd], qg_ref[...]) * scale).astype(qo_ref.dtype)
    for h in range(NSA_KV_HEADS):
        ks = kv_ref[:, 2 * D_NSA_KV + h * hd:2 * D_NSA_KV + (h + 1) * hd]
        kw = kv_ref[:, 4 * D_NSA_KV + h * hd:4 * D_NSA_KV + (h + 1) * hd]
        kso_ref[:, h * hd:(h + 1) * hd] = head(ks, ksg_ref[...]).astype(kso_ref.dtype)
        kwo_ref[:, h * hd:(h + 1) * hd] = head(kw, kwg_ref[...]).astype(kwo_ref.dtype)


def rope_norm(posf, q, kv6, q_gain, ks_gain, kw_gain):
    m = q.shape[0]
    tm = min(ROW_TILE, m)
    half = ROPE_DIM // 2
    inv = np.exp(-math.log(ROPE_THETA) * np.arange(0, ROPE_DIM, 2, dtype=np.float32) / ROPE_DIM).astype(np.float32)
    invf = np.zeros((1, LANES), np.float32)
    invf[0, :half] = inv
    invf[0, half:ROPE_DIM] = inv
    row = lambda w: pl.BlockSpec((tm, w), lambda i: (i, 0))
    const = pl.BlockSpec((1, LANES), lambda i: (0, 0))
    return pl.pallas_call(
        _rope_norm_kernel,
        grid=(m // tm,),
        in_specs=[row(1), const, row(D_NSA), row(6 * D_NSA_KV), const, const, const],
        out_specs=[row(D_NSA), row(D_NSA_KV), row(D_NSA_KV)],
        out_shape=[jax.ShapeDtypeStruct((m, D_NSA), BF16), jax.ShapeDtypeStruct((m, D_NSA_KV), BF16),
                   jax.ShapeDtypeStruct((m, D_NSA_KV), BF16)],
        compiler_params=_params("parallel"),
        name="rope_norm",
    )(posf, jnp.asarray(invf), q, kv6, q_gain.reshape(1, LANES), ks_gain.reshape(1, LANES), kw_gain.reshape(1, LANES))


def _compress_kernel(uk_ref, uv_ref, pek_ref, pev_ref, wk1_ref, wk2_ref, wv1_ref, wv2_ref, g_ref, kc_ref, vc_ref):
    def mlp(u, pe_ref, w1_ref, w2_ref):
        n = u.shape[0]
        h1 = _dot((u + pe_ref[0:1, :]).astype(BF16), w1_ref[0])
        h2 = _dot((u + pe_ref[1:2, :]).astype(BF16), w1_ref[1])
        pre = h1 + pltpu.roll(h2, n - 1, axis=0)
        return _dot(_silu(pre).astype(BF16), w2_ref[...])

    kc = mlp(uk_ref[0, 0], pek_ref, wk1_ref, wk2_ref)
    kc_ref[0, 0] = _rms(kc) * g_ref[...]
    vc_ref[0, 0] = mlp(uv_ref[0, 0], pev_ref, wv1_ref, wv2_ref)


def compress(uk, uv, pe_k, pe_v, wk1, wk2, wv1, wv2, kc_gain):
    bsz, nkv, nch, width = uk.shape
    half = CMP_STRIDE * NSA_HEAD_DIM
    u_spec = pl.BlockSpec((1, 1, nch, width), lambda b, g: (b, g, 0, 0))
    o_spec = pl.BlockSpec((1, 1, nch, NSA_HEAD_DIM), lambda b, g: (b, g, 0, 0))
    c2 = lambda s: pl.BlockSpec(s, lambda b, g: (0, 0))
    c3 = lambda s: pl.BlockSpec(s, lambda b, g: (0, 0, 0))
    o_shape = jax.ShapeDtypeStruct((bsz, nkv, nch, NSA_HEAD_DIM), F32)
    return pl.pallas_call(
        _compress_kernel,
        grid=(bsz, nkv),
        in_specs=[u_spec, u_spec, c2((2, half)), c2((2, half)),
                  c3((2, half, CMP_HIDDEN)), c2((CMP_HIDDEN, NSA_HEAD_DIM)),
                  c3((2, half, CMP_HIDDEN)), c2((CMP_HIDDEN, NSA_HEAD_DIM)), c2((1, NSA_HEAD_DIM))],
        out_specs=[o_spec, o_spec],
        out_shape=[o_shape, o_shape],
        compiler_params=_params("parallel", "parallel"),
        name="nsa_compress",
    )(uk, uv, pe_k.reshape(2, half), pe_v.reshape(2, half),
      wk1.reshape(2, half, CMP_HIDDEN).astype(BF16), wk2.astype(BF16),
      wv1.reshape(2, half, CMP_HIDDEN).astype(BF16), wv2.astype(BF16), kc_gain.reshape(1, NSA_HEAD_DIM))


def _stack_heads(q):
    return jnp.concatenate([q[:, r * NSA_HEAD_DIM:(r + 1) * NSA_HEAD_DIM] for r in range(NSA_Q_PER_KV)], axis=0)


def _unstack_heads(o, tq):
    return jnp.concatenate([o[r * tq:(r + 1) * tq, :] for r in range(NSA_Q_PER_KV)], axis=1)


def _cmp_select_kernel(q_ref, kc_ref, vc_ref, ovt_ref, o_ref, sel_ref, *, tq, n_blk):
    R = NSA_Q_PER_KV
    q0 = pl.program_id(2) * tq
    q4 = _stack_heads(q_ref[...])
    kc = kc_ref[0, 0].astype(BF16)
    ncmp = kc.shape[0]
    s = _dot_nt(q4, kc)
    t_row = q0 + lax.broadcasted_iota(I32, (tq, ncmp), 0)
    c_col = lax.broadcasted_iota(I32, (tq, ncmp), 1)
    visf = jnp.where((c_col * CMP_STRIDE + CMP_BLOCK - 1) <= t_row, 1.0, 0.0)
    vis = jnp.concatenate([visf] * R, axis=0) > 0.5
    s = jnp.where(vis, s, NEG_BIG)
    mx = jnp.max(s, axis=-1, keepdims=True)
    e = jnp.where(vis, jnp.exp2(s - mx), 0.0)
    den = jnp.sum(e, axis=-1, keepdims=True)
    p = e / jnp.maximum(den, 1e-30)
    o = _dot(p.astype(BF16), vc_ref[0, 0].astype(BF16))
    o_ref[...] = _unstack_heads(o, tq)

    psum = p[0:tq]
    for r in range(1, R):
        psum = psum + p[r * tq:(r + 1) * tq]
    h, m, l = _split3(psum)
    ovt = ovt_ref[...]
    imp = _dot_nt(ovt, h) + _dot_nt(ovt, m) + _dot_nt(ovt, l)
    jdx = lax.broadcasted_iota(I32, (n_blk, tq), 0)
    t_lane = q0 + lax.broadcasted_iota(I32, (n_blk, tq), 1)
    cur = t_lane // SEL_BLOCK
    causal = jdx <= cur
    forced = ((jdx == 0) | (jdx >= cur - 1)) & causal
    score = jnp.where(forced, jnp.inf, jnp.where(causal, imp, -jnp.inf))
    rank = jnp.zeros((n_blk, tq), F32)
    for jp in range(n_blk):
        rowv = score[jp:jp + 1, :]
        beats = (rowv > score) | ((rowv == score) & (jdx > jp))
        rank = rank + jnp.where(beats, 1.0, 0.0)
    sel = jnp.where((rank < float(min(N_SELECT, n_blk))) & causal, 1.0, 0.0)
    if n_blk < LANES:
        sel = jnp.concatenate([sel, jnp.zeros((LANES - n_blk, tq), F32)], axis=0)
    sel_ref[0, 0] = sel.astype(sel_ref.dtype)


def cmp_select(qn, kc, vc, bsz, seq):
    m = bsz * seq
    G = NSA_KV_HEADS
    tq = CMP_TQ
    nq = seq // tq
    ncmp = kc.shape[2]
    n_blk = seq // SEL_BLOCK
    assert n_blk <= LANES and tq == LANES
    c0 = np.arange(ncmp)[None, :] * CMP_STRIDE
    s0 = np.arange(n_blk)[:, None] * SEL_BLOCK
    ov = np.clip(np.minimum(c0 + CMP_BLOCK, s0 + SEL_BLOCK) - np.maximum(c0, s0), 0, None) / CMP_STRIDE
    n_cmp_valid = (seq - CMP_BLOCK) // CMP_STRIDE + 1
    ov[:, n_cmp_valid:] = 0.0
    kv_spec = pl.BlockSpec((1, 1, ncmp, NSA_HEAD_DIM), lambda b, g, i: (b, g, 0, 0))
    kern = functools.partial(_cmp_select_kernel, tq=tq, n_blk=n_blk)
    return pl.pallas_call(
        kern,
        grid=(bsz, G, nq),
        in_specs=[pl.BlockSpec((tq, NSA_Q_PER_KV * NSA_HEAD_DIM), lambda b, g, i: (b * nq + i, g)),
                  kv_spec, kv_spec, pl.BlockSpec((n_blk, ncmp), lambda b, g, i: (0, 0))],
        out_specs=[pl.BlockSpec((tq, NSA_Q_PER_KV * NSA_HEAD_DIM), lambda b, g, i: (b * nq + i, g)),
                   pl.BlockSpec((1, 1, LANES, tq), lambda b, g, i: (b, g, 0, i))],
        out_shape=[jax.ShapeDtypeStruct((m, D_NSA), F32), jax.ShapeDtypeStruct((bsz, G, LANES, seq), BF16)],
        compiler_params=_params("parallel", "parallel", "parallel"),
        name="nsa_cmp_select",
    )(qn, kc, vc, jnp.asarray(ov, BF16))


def _masked_attn_kernel(q_ref, k_ref, vt_ref, *rest, mode, tq, tk):
    if mode == "sel":
        selt_ref, o_ref = rest
    else:
        (o_ref,) = rest
    R = NSA_Q_PER_KV
    hd = NSA_HEAD_DIM
    q0 = pl.program_id(2) * tq
    key_row = lax.broadcasted_iota(I32, (tk, tq), 0)
    t_lane = q0 + lax.broadcasted_iota(I32, (tk, tq), 1)
    if mode == "sel":
        key_row_b = lax.broadcasted_iota(I32, (tk, LANES), 0)
        blk_lane = lax.broadcasted_iota(I32, (tk, LANES), 1)
        lo = 0
    else:
        lo = jnp.maximum(q0 - (WINDOW - 1), 0) // tk
    hi = (q0 + tq - 1) // tk + 1
    q = q_ref[...]

    def body(kt, carry):
        k0 = kt * tk
        k_t = k_ref[pl.ds(pl.multiple_of(k0, tk), tk), :]
        v_t = vt_ref[0, 0, kt]
        key = k0 + key_row
        if mode == "sel":
            expand = jnp.where((k0 + key_row_b) // SEL_BLOCK == blk_lane, 1.0, 0.0).astype(BF16)
            picked = _dot(expand, selt_ref[0, 0])
            ok = (picked > 0.5) & (key <= t_lane)
        else:
            ok = (key <= t_lane) & (key > t_lane - WINDOW)
        bias = jnp.where(ok, 0.0, NEG_BIG)
        ss = [_dot_nt(k_t, q[:, r * hd:(r + 1) * hd]) + bias for r in range(R)]
        stats = []
        for r in range(R):
            m_prev, l_prev = carry[3 * r], carry[3 * r + 1]
            m_new = jnp.maximum(m_prev, jnp.max(ss[r], axis=0, keepdims=True))
            p = jnp.exp2(ss[r] - m_new)
            alpha = jnp.exp2(m_prev - m_new)
            stats.append((m_new, alpha * l_prev + jnp.sum(p, axis=0, keepdims=True), alpha, p.astype(BF16)))
        new = []
        for r in range(R):
            m_new, l_new, alpha, p = stats[r]
            new += [m_new, l_new, alpha * carry[3 * r + 2] + _dot(v_t, p)]
        return tuple(new)

    init = (jnp.full((1, tq), NEG_BIG, F32), jnp.zeros((1, tq), F32), jnp.zeros((hd, tq), F32)) * R
    out = lax.fori_loop(lo, hi, body, init)
    o_ref[...] = jnp.concatenate([(out[3 * r + 2] / out[3 * r + 1]).T for r in range(R)], axis=1)


def masked_attention(qn, k, k_col0, vt, bsz, seq, mode, selt=None):
    m = bsz * seq
    G = NSA_KV_HEADS
    tq = min(ATT_TQ, seq)
    tk = vt.shape[-1]
    nq = seq // tq
    hd = NSA_HEAD_DIM
    q_spec = pl.BlockSpec((tq, NSA_Q_PER_KV * hd), lambda b, g, i: (b * nq + i, g))
    in_specs = [q_spec,
                pl.BlockSpec((seq, hd), lambda b, g, i: (b, k_col0 + g)),
                pl.BlockSpec((1, 1, seq // tk, hd, tk), lambda b, g, i: (b, g, 0, 0, 0))]
    args = [qn, k, vt]
    if mode == "sel":
        in_specs.append(pl.BlockSpec((1, 1, LANES, tq), lambda b, g, i: (b, g, 0, i)))
        args.append(selt)
    return pl.pallas_call(
        functools.partial(_masked_attn_kernel, mode=mode, tq=tq, tk=tk),
        grid=(bsz, G, nq),
        in_specs=in_specs,
        out_specs=q_spec,
        out_shape=jax.ShapeDtypeStruct((m, D_NSA), F32),
        compiler_params=_params("parallel", "parallel", "arbitrary"),
        name="nsa_attn_" + mode,
    )(*args)


def _nsa_combine_kernel(g_ref, e_ref, oc_ref, os_ref, ow_ref, o_ref):
    gate = _sigmoid(g_ref[...])
    gx = _dot_f32_sel(gate, e_ref[...])
    y = gx[:, :D_NSA] * oc_ref[...] + gx[:, D_NSA:2 * D_NSA] * os_ref[...] + gx[:, 2 * D_NSA:] * ow_ref[...]
    o_ref[...] = y.astype(o_ref.dtype)


def nsa_combine(small, o_cmp, o_sel, o_win):
    m = small.shape[0]
    tm = min(COMBINE_TM, m)
    e = np.zeros((LANES, 3 * D_NSA), np.float32)
    for h in range(NSA_HEADS):
        for br in range(3):
            e[SSM_HEADS + 3 * h + br, br * D_NSA + h * NSA_HEAD_DIM:br * D_NSA + (h + 1) * NSA_HEAD_DIM] = 1.0
    row = lambda w: pl.BlockSpec((tm, w), lambda i: (i, 0))
    return pl.pallas_call(
        _nsa_combine_kernel,
        grid=(m // tm,),
        in_specs=[row(LANES), pl.BlockSpec((LANES, 3 * D_NSA), lambda i: (0, 0)), row(D_NSA), row(D_NSA), row(D_NSA)],
        out_specs=row(D_NSA),
        out_shape=jax.ShapeDtypeStruct((m, D_NSA), BF16),
        compiler_params=_params("parallel"),
        name="nsa_combine",
    )(small, jnp.asarray(e, BF16), o_cmp, o_sel, o_win)


def _xattn_kernel(q_ref, kv_ref, qg_ref, kg_ref, o_ref):
    hd = XATTN_HEAD_DIM
    scale = hd ** -0.5
    outs = []
    for h in range(XATTN_HEADS):
        q = (_rms(q_ref[:, h * hd:(h + 1) * hd]) * qg_ref[...]).astype(BF16)
        k = (_rms(kv_ref[:, h * hd:(h + 1) * hd]) * kg_ref[...]).astype(BF16)
        v = kv_ref[:, D_XATTN + h * hd:D_XATTN + (h + 1) * hd].astype(BF16)
        s = _dot_nt(q, k) * scale
        e = jnp.exp(s - jnp.max(s, axis=-1, keepdims=True))
        p = e / jnp.sum(e, axis=-1, keepdims=True)
        outs.append(_dot(p.astype(BF16), v))
    o_ref[...] = jnp.concatenate(outs, axis=1).astype(o_ref.dtype)


def xattn_core(q, kv, q_gain, k_gain, bsz, seq):
    m = bsz * seq
    n_mem = kv.shape[0] // bsz
    tm = min(ROW_TILE, seq)
    nt = seq // tm
    return pl.pallas_call(
        _xattn_kernel,
        grid=(bsz, nt),
        in_specs=[pl.BlockSpec((tm, D_XATTN), lambda b, i: (b * nt + i, 0)),
                  pl.BlockSpec((n_mem, 2 * D_XATTN), lambda b, i: (b, 0)),
                  pl.BlockSpec((1, XATTN_HEAD_DIM), lambda b, i: (0, 0)),
                  pl.BlockSpec((1, XATTN_HEAD_DIM), lambda b, i: (0, 0))],
        out_specs=pl.BlockSpec((tm, D_XATTN), lambda b, i: (b * nt + i, 0)),
        out_shape=jax.ShapeDtypeStruct((m, D_XATTN), BF16),
        compiler_params=_params("parallel", "parallel"),
        name="xattn_core",
    )(q, kv, q_gain.reshape(1, XATTN_HEAD_DIM), k_gain.reshape(1, XATTN_HEAD_DIM))


def memory_xattn(h, mem2d, norm_x, norm_m, wq, wkv, wo, q_gain, k_gain, bsz, seq):
    hq = rmsnorm(h, norm_x, BF16)
    q = matmul([(hq, wq.astype(BF16))])
    mem_n = rmsnorm(mem2d, norm_m, BF16)
    kv = matmul([(mem_n, wkv.astype(BF16))])
    o = xattn_core(q, kv, q_gain, k_gain, bsz, seq)
    return matmul([(o, wo.astype(BF16))], residual=h)


def _mix_kernel(h_ref, hp_ref, g_ref, mu_ref, *rest, tm, tiles_per_seq):
    outs, ext_ref = rest[:-1], rest[-1]
    i = pl.program_id(0)
    hn = _rms(h_ref[...]) * g_ref[...]
    hp = _rms(hp_ref[...]) * g_ref[...]
    hp = jnp.where(i % tiles_per_seq == 0, 0.0, hp)
    ext_ref[0:SUBLANES, :] = hp
    ext_ref[SUBLANES:tm + SUBLANES, :] = hn
    xx = ext_ref[SUBLANES - 1:tm + SUBLANES - 1, :] - hn
    for j, o_ref in enumerate(outs):
        o_ref[...] = (hn + xx * mu_ref[j:j + 1, :]).astype(o_ref.dtype)


def rwkv_mix(h, gain, mu, seq):
    m, d = h.shape
    tm = min(ROW_TILE, seq)
    n_mix = mu.shape[0]
    kern = functools.partial(_mix_kernel, tm=tm, tiles_per_seq=seq // tm)
    row = pl.BlockSpec((tm, d), lambda i: (i, 0))
    return pl.pallas_call(
        kern,
        grid=(m // tm,),
        in_specs=[row, pl.BlockSpec((SUBLANES, d), lambda i: (jnp.maximum(i * (tm // SUBLANES) - 1, 0), 0)),
                  pl.BlockSpec((1, d), lambda i: (0, 0)), pl.BlockSpec((n_mix, d), lambda i: (0, 0))],
        out_specs=[row] * n_mix,
        out_shape=[jax.ShapeDtypeStruct((m, d), BF16)] * n_mix,
        scratch_shapes=[pltpu.VMEM((tm + SUBLANES, d), F32)],
        compiler_params=_params("parallel"),
        name="rwkv_mix",
    )(h, h, gain.reshape(1, d), mu)


WKV_LANE_GROUP = LANES // 4
WKV_ROWS_PER_BATCH = D_MODEL // LANES


def _wkv_fused_kernel(r_ref, k_ref, v_ref, w_ref, a_ref, g_ref, kkp_ref, kap_ref, rkp_ref, lnw_ref, lnb_ref,
                      o_ref, s_ref, wd_s, kh_s, kk_s, b_s, r_s, v_s, g_s, y_s, *, steps):
    n = RWKV_HEAD_DIM
    nb = LANES // WKV_LANE_GROUP
    lane_grp = lax.broadcasted_iota(I32, (steps, SUBLANES, LANES), 2) // WKV_LANE_GROUP

    @pl.when(pl.program_id(0) == 0)
    def _():
        s_ref[...] = jnp.zeros_like(s_ref)

    def to_scan(x_ref):
        pieces = []
        for n4 in range(nb):
            for v in range(WKV_ROWS_PER_BATCH // SUBLANES):
                acc = None
                for b in range(nb):
                    x = x_ref[b, :, SUBLANES * v:SUBLANES * (v + 1), :]
                    shift = (WKV_LANE_GROUP * (b - n4)) % LANES
                    if shift:
                        x = pltpu.roll(x, shift, axis=2)
                    acc = x if acc is None else jnp.where(lane_grp == b, x, acc)
                pieces.append(acc)
        return pieces

    rs, ks, vs, ws, as_, gs = (to_scan(x) for x in (r_ref, k_ref, v_ref, w_ref, a_ref, g_ref))
    kks = [ks[u] * kkp_ref[SUBLANES * u:SUBLANES * (u + 1), :] for u in range(len(ks))]
    ss = kks[0] * kks[0]
    for u in range(1, len(kks)):
        ss = ss + kks[u] * kks[u]
    inv_norm = 1.0 / jnp.maximum(jnp.sqrt(jnp.sum(ss, axis=1, keepdims=True)), 1e-12)
    for u in range(len(ks)):
        rows = slice(SUBLANES * u, SUBLANES * (u + 1))
        kk = kks[u] * inv_norm
        wd_s[:, rows, :] = jnp.exp(-jnp.exp(-_softplus(-ws[u]) - 0.5))
        kk_s[:, rows, :] = kk
        b_s[:, rows, :] = kk * as_[u]
        kh_s[:, rows, :] = ks[u] * (1.0 + (as_[u] - 1.0) * kap_ref[rows, :])
        r_s[:, rows, :] = rs[u]
        v_s[:, rows, :] = vs[u]
        g_s[:, rows, :] = gs[u]

    q0 = jnp.zeros((n, LANES), F32)
    for j in range(n):
        q0 = q0 + s_ref[j] * kk_s[0, j:j + 1, :]

    def step(t, q):
        tn = jnp.minimum(t + 1, steps - 1)
        v = v_s[t]
        y = jnp.zeros((n, LANES), F32)
        qn = jnp.zeros((n, LANES), F32)
        for j in range(n):
            s_new = s_ref[j] * wd_s[t, j:j + 1, :] - q * b_s[t, j:j + 1, :] + v * kh_s[t, j:j + 1, :]
            s_ref[j] = s_new
            y = y + s_new * r_s[t, j:j + 1, :]
            qn = qn + s_new * kk_s[tn, j:j + 1, :]
        y_s[t] = y
        return qn

    lax.fori_loop(0, steps, step, q0)

    y = y_s[...]
    inv_n = 1.0 / n
    mean = jnp.sum(y, axis=1, keepdims=True) * inv_n
    yc = y - mean
    var = jnp.sum(yc * yc, axis=1, keepdims=True) * inv_n
    bonus = jnp.sum(r_s[...] * kh_s[...] * rkp_ref[...], axis=1, keepdims=True) * v_s[...]
    o = (yc * lax.rsqrt(var + RWKV_GN_EPS) * lnw_ref[...] + lnb_ref[...] + bonus) * g_s[...]
    for b in range(nb):
        for v in range(WKV_ROWS_PER_BATCH // SUBLANES):
            acc = None
            for n4 in range(nb):
                u = n4 * (WKV_ROWS_PER_BATCH // SUBLANES) + v
                x = o[:, SUBLANES * u:SUBLANES * (u + 1), :]
                shift = (WKV_LANE_GROUP * (n4 - b)) % LANES
                if shift:
                    x = pltpu.roll(x, shift, axis=2)
                acc = x if acc is None else jnp.where(lane_grp == n4, x, acc)
            o_ref[b, :, SUBLANES * v:SUBLANES * (v + 1), :] = acc


def _wkv_column_order():
    c, n4, h = np.meshgrid(np.arange(WKV_ROWS_PER_BATCH), np.arange(LANES // WKV_LANE_GROUP), np.arange(RWKV_HEADS),
                           indexing="ij")
    return (h * RWKV_HEAD_DIM + n4 * WKV_ROWS_PER_BATCH + c).reshape(-1)


def _scan_tile(p, bsz):
    return jnp.tile(p.reshape(RWKV_HEADS, RWKV_HEAD_DIM).T, (1, bsz))


def wkv_fused(r, k, v, w_raw, a_sig, g, k_k, k_a, r_k, ln_w, ln_b, bsz, seq):
    assert bsz * RWKV_HEADS == LANES and RWKV_HEADS == WKV_LANE_GROUP
    steps = min(WKV_T, seq)
    view = lambda x: x.reshape(bsz, seq, WKV_ROWS_PER_BATCH, LANES)
    blk = pl.BlockSpec((bsz, steps, WKV_ROWS_PER_BATCH, LANES), lambda i: (0, i, 0, 0))
    par = pl.BlockSpec((RWKV_HEAD_DIM, LANES), lambda i: (0, 0))
    scr = pltpu.VMEM((steps, RWKV_HEAD_DIM, LANES), F32)
    out = pl.pallas_call(
        functools.partial(_wkv_fused_kernel, steps=steps),
        grid=(seq // steps,),
        in_specs=[blk] * 6 + [par] * 5,
        out_specs=blk,
        out_shape=jax.ShapeDtypeStruct((bsz, seq, WKV_ROWS_PER_BATCH, LANES), F32),
        scratch_shapes=[pltpu.VMEM((RWKV_HEAD_DIM, RWKV_HEAD_DIM, LANES), F32)] + [scr] * 8,
        compiler_params=_params("arbitrary"),
        name="wkv_fused",
    )(view(r), view(k), view(v), view(w_raw), view(a_sig), view(g),
      _scan_tile(k_k, bsz), _scan_tile(k_a, bsz), _scan_tile(r_k, bsz), _scan_tile(ln_w, bsz), _scan_tile(ln_b, bsz))
    return out.reshape(bsz * seq, D_MODEL)


def _router_kernel(h_ref, g_ref, wr_ref, info_ref, tile_cnt_ref, total_ref, carry_ref, *, tt):
    E = N_EXPERTS
    i = pl.program_id(0)

    @pl.when(i == 0)
    def _():
        carry_ref[...] = jnp.zeros_like(carry_ref)

    xh, xm, xl = _split3(_rms(h_ref[...]) * g_ref[...])
    wh, wm, wl = _split3(wr_ref[...])
    logits = (_dot_nt(wh, xh) + _dot_nt(wh, xm) + _dot_nt(wm, xh)
              + _dot_nt(wh, xl) + _dot_nt(wl, xh) + _dot_nt(wm, xm))
    eidx = lax.broadcasted_iota(I32, (E, tt), 0)
    v1 = jnp.max(logits, axis=0, keepdims=True)
    i1 = jnp.min(jnp.where(logits == v1, eidx, E), axis=0, keepdims=True)
    rest = jnp.where(eidx == i1, -jnp.inf, logits)
    v2 = jnp.max(rest, axis=0, keepdims=True)
    i2 = jnp.min(jnp.where(rest == v2, eidx, E), axis=0, keepdims=True)
    e2 = jnp.exp(v2 - v1)
    g1 = 1.0 / (1.0 + e2)
    g2 = e2 / (1.0 + e2)
    hit1 = eidx == i1
    hit2 = eidx == i2
    onehot = jnp.where(hit1 | hit2, 1.0, 0.0)
    rr = lax.broadcasted_iota(I32, (tt, tt), 0)
    cc = lax.broadcasted_iota(I32, (tt, tt), 1)
    before = jnp.where(rr < cc, 1.0, 0.0).astype(BF16)
    carry = carry_ref[...]
    carry_wide = jnp.concatenate([carry] * (tt // LANES), axis=1)
    cum = _dot(onehot.astype(BF16), before) + carry_wide
    rank1 = jnp.sum(jnp.where(hit1, cum, 0.0), axis=0, keepdims=True)
    rank2 = jnp.sum(jnp.where(hit2, cum, 0.0), axis=0, keepdims=True)
    zero = jnp.zeros((1, tt), F32)
    info_ref[...] = jnp.concatenate([i1.astype(F32), i2.astype(F32), rank1, rank2, g1, g2, zero, zero], axis=0)
    tile_cnt_ref[0] = carry
    carry = carry + _dot(onehot.astype(BF16), jnp.ones((tt, LANES), BF16))
    carry_ref[...] = carry
    total_ref[...] = carry


def moe_router(h, norm_gain, w_router):
    m, d = h.shape
    tt = min(MOE_TILE, m)
    nt = m // tt
    return pl.pallas_call(
        functools.partial(_router_kernel, tt=tt),
        grid=(nt,),
        in_specs=[pl.BlockSpec((tt, d), lambda i: (i, 0)), pl.BlockSpec((1, d), lambda i: (0, 0)),
                  pl.BlockSpec((N_EXPERTS, d), lambda i: (0, 0))],
        out_specs=[pl.BlockSpec((SUBLANES, tt), lambda i: (0, i)),
                   pl.BlockSpec((1, N_EXPERTS, LANES), lambda i: (i, 0, 0)),
                   pl.BlockSpec((N_EXPERTS, LANES), lambda i: (0, 0))],
        out_shape=[jax.ShapeDtypeStruct((SUBLANES, m), F32), jax.ShapeDtypeStruct((nt, N_EXPERTS, LANES), F32),
                   jax.ShapeDtypeStruct((N_EXPERTS, LANES), F32)],
        scratch_shapes=[pltpu.VMEM((N_EXPERTS, LANES), F32)],
        compiler_params=_params("arbitrary"),
        name="moe_router",
    )(h, norm_gain.reshape(1, d), w_router.T)


def _slots_kernel(start_ref, info_ref, o_ref):
    info = info_ref[...]
    e1, e2, r1, r2 = info[0:1], info[1:2], info[2:3], info[3:4]
    s1 = r1
    s2 = r2
    for e in range(N_EXPERTS):
        st = start_ref[e].astype(F32)
        s1 = s1 + jnp.where(e1 == float(e), st, 0.0)
        s2 = s2 + jnp.where(e2 == float(e), st, 0.0)
    o_ref[...] = jnp.concatenate([s1, s2, info[4:5], info[5:6], info[6:8], info[6:8]], axis=0)


def moe_slots(info, slot_start):
    m = info.shape[1]
    tt = min(2048, m)
    return pl.pallas_call(
        _slots_kernel,
        grid_spec=pltpu.PrefetchScalarGridSpec(
            num_scalar_prefetch=1, grid=(m // tt,),
            in_specs=[pl.BlockSpec((SUBLANES, tt), lambda i, st: (0, i))],
            out_specs=pl.BlockSpec((SUBLANES, tt), lambda i, st: (0, i))),
        out_shape=jax.ShapeDtypeStruct((SUBLANES, m), F32),
        compiler_params=_params("parallel"),
        name="moe_slots",
    )(slot_start, info)


def _dispatch_kernel(blk_ref, til_ref, flg_ref, slots_ref, x_ref, o_ref, acc_ref, *, bs):
    w = pl.program_id(0)
    flags = flg_ref[w]

    @pl.when((flags & 1) != 0)
    def _():
        acc_ref[...] = jnp.zeros_like(acc_ref)

    @pl.when((flags & 4) != 0)
    def _():
        sl = slots_ref[...]
        sid = (blk_ref[w] * bs + lax.broadcasted_iota(I32, (bs, sl.shape[1]), 0)).astype(F32)
        onehot = jnp.where((sid == sl[0:1, :]) | (sid == sl[1:2, :]), 1.0, 0.0).astype(BF16)
        acc_ref[...] += _dot(onehot, x_ref[...])

    @pl.when((flags & 2) != 0)
    def _():
        o_ref[...] = acc_ref[...].astype(o_ref.dtype)


def moe_dispatch(xn, slots, wl_blk, wl_til, wl_flg, n_slots):
    m, d = xn.shape
    bs = tt = MOE_TILE
    n_items = wl_blk.shape[0]
    return pl.pallas_call(
        functools.partial(_dispatch_kernel, bs=bs),
        grid_spec=pltpu.PrefetchScalarGridSpec(
            num_scalar_prefetch=3, grid=(n_items,),
            in_specs=[pl.BlockSpec((SUBLANES, tt), lambda w, b, t, f: (0, t[w])),
                      pl.BlockSpec((tt, d), lambda w, b, t, f: (t[w], 0))],
            out_specs=pl.BlockSpec((bs, d), lambda w, b, t, f: (b[w], 0)),
            scratch_shapes=[pltpu.VMEM((bs, d), F32)]),
        out_shape=jax.ShapeDtypeStruct((n_slots, d), BF16),
        compiler_params=_params("arbitrary"),
        name="moe_dispatch",
    )(wl_blk, wl_til, wl_flg, slots, xn)


def _moe_ffn_kernel(be_ref, nb_ref, x_ref, w1_ref, w3_ref, w2_ref, o_ref, acc_ref):
    b = pl.program_id(0)
    f = pl.program_id(1)

    @pl.when(f == 0)
    def _():
        acc_ref[...] = jnp.zeros_like(acc_ref)

    @pl.when(b < nb_ref[0])
    def _():
        x = x_ref[...]
        h = _silu(_dot(x, w1_ref[0].astype(BF16))) * _dot(x, w3_ref[0].astype(BF16))
        acc_ref[...] += _dot(h.astype(BF16), w2_ref[0].astype(BF16))

    @pl.when(f == pl.num_programs(1) - 1)
    def _():
        o_ref[...] = acc_ref[...].astype(o_ref.dtype)


def moe_ffn(xs, w1, w3, w2, block_expert, n_used):
    n_slots, d = xs.shape
    ff = w1.shape[2]
    bm = MOE_FFN_ROWS
    tf = MOE_TF
    nf = ff // tf

    def f_of(b, f, nb):
        return jnp.where(b < nb[0], f, nf - 1)

    return pl.pallas_call(
        _moe_ffn_kernel,
        grid_spec=pltpu.PrefetchScalarGridSpec(
            num_scalar_prefetch=2, grid=(n_slots // bm, nf),
            in_specs=[pl.BlockSpec((bm, d), lambda b, f, be, nb: (b, 0)),
                      pl.BlockSpec((1, d, tf), lambda b, f, be, nb: (be[b], 0, f_of(b, f, nb))),
                      pl.BlockSpec((1, d, tf), lambda b, f, be, nb: (be[b], 0, f_of(b, f, nb))),
                      pl.BlockSpec((1, tf, d), lambda b, f, be, nb: (be[b], f_of(b, f, nb), 0))],
            out_specs=pl.BlockSpec((bm, d), lambda b, f, be, nb: (b, 0)),
            scratch_shapes=[pltpu.VMEM((bm, d), F32)]),
        out_shape=jax.ShapeDtypeStruct((n_slots, d), BF16),
        compiler_params=_params("arbitrary", "arbitrary"),
        name="moe_ffn",
    )(block_expert, n_used, xs, w1, w3, w2)


def _combine_kernel(til_ref, blk_ref, flg_ref, tok_ref, y_ref, res_ref, o_ref, acc_ref, *, bs):
    w = pl.program_id(0)
    flags = flg_ref[w]

    @pl.when((flags & 1) != 0)
    def _():
        acc_ref[...] = res_ref[...]

    @pl.when((flags & 4) != 0)
    def _():
        tok = tok_ref[...]
        tt = tok.shape[0]
        sid = (blk_ref[w] * bs + lax.broadcasted_iota(I32, (tt, bs), 1)).astype(F32)
        y = y_ref[...]
        oh1 = jnp.where(sid == tok[:, 0:1], 1.0, 0.0).astype(BF16)
        oh2 = jnp.where(sid == tok[:, 1:2], 1.0, 0.0).astype(BF16)
        acc_ref[...] += tok[:, 2:3] * _dot(oh1, y) + tok[:, 3:4] * _dot(oh2, y)

    @pl.when((flags & 2) != 0)
    def _():
        o_ref[...] = acc_ref[...]


def moe_combine(y, tok_info, residual, wl_til, wl_blk, wl_flg):
    m, d = residual.shape
    bs = tt = MOE_TILE
    n_items = wl_til.shape[0]
    return pl.pallas_call(
        functools.partial(_combine_kernel, bs=bs),
        grid_spec=pltpu.PrefetchScalarGridSpec(
            num_scalar_prefetch=3, grid=(n_items,),
            in_specs=[pl.BlockSpec((tt, SUBLANES), lambda w, t, b, f: (t[w], 0)),
                      pl.BlockSpec((bs, d), lambda w, t, b, f: (b[w], 0)),
                      pl.BlockSpec((tt, d), lambda w, t, b, f: (t[w], 0))],
            out_specs=pl.BlockSpec((tt, d), lambda w, t, b, f: (t[w], 0)),
            scratch_shapes=[pltpu.VMEM((tt, d), F32)]),
        out_shape=jax.ShapeDtypeStruct((m, d), F32),
        compiler_params=_params("arbitrary"),
        name="moe_combine",
    )(wl_til, wl_blk, wl_flg, tok_info, y, residual)


def _work_list(overlap, n_items):
    n_major, n_minor = overlap.shape
    has = jnp.any(overlap, axis=1)
    marked = jnp.concatenate([overlap[:, :1] | ~has[:, None], overlap[:, 1:]], axis=1)
    maj, mino = jnp.nonzero(marked, size=n_items, fill_value=-1)
    valid = maj >= 0
    real = valid & overlap[jnp.maximum(maj, 0), jnp.maximum(mino, 0)]
    prev = jnp.concatenate([jnp.full((1,), -2, maj.dtype), maj[:-1]])
    nxt = jnp.concatenate([maj[1:], jnp.full((1,), -2, maj.dtype)])
    flags = (valid & (maj != prev)) * 1 + (valid & (maj != nxt)) * 2 + real * 4
    last_maj = jnp.max(jnp.where(valid, maj, 0))
    last_min = jnp.sum(jnp.where(valid & (maj == last_maj) & (maj != nxt), mino, 0))
    maj = jnp.where(valid, maj, last_maj)
    mino = jnp.where(valid, mino, last_min)
    return maj.astype(I32), mino.astype(I32), flags.astype(I32)


def moe_layer(h, norm_gain, w_router, w1, w3, w2):
    m, d = h.shape
    E = N_EXPERTS
    tile = min(MOE_TILE, m)
    n_tiles = m // tile
    ffn_rows = MOE_FFN_ROWS
    xn = rmsnorm(h, norm_gain, BF16)
    info, tile_cnt, total = moe_router(h, norm_gain, w_router)
    before = tile_cnt[:, :, 0]
    counts = total[:, 0].astype(I32)
    padded = (counts + ffn_rows - 1) // ffn_rows * ffn_rows
    ends = jnp.cumsum(padded)
    starts = ends - padded
    n_slots = (m * TOP_K // ffn_rows + E) * ffn_rows
    n_ffn_blocks = n_slots // ffn_rows
    n_dblocks = n_slots // tile
    slots = moe_slots(info, starts.astype(I32))
    db_start = jnp.arange(n_dblocks, dtype=I32) * tile
    db_e = jnp.minimum(jnp.sum(db_start[:, None] >= ends[None, :], axis=1), E - 1)
    r0 = db_start - starts[db_e]
    r1 = jnp.minimum(r0 + tile, counts[db_e])
    after = jnp.concatenate([before[1:], counts[None, :].astype(F32)], axis=0).astype(I32)
    bef = before.astype(I32)
    ov = (bef[:, db_e].T < r1[:, None]) & (after[:, db_e].T > r0[:, None]) & (db_start < ends[-1])[:, None]
    n_items = n_dblocks + E * n_tiles
    d_blk, d_til, d_flg = _work_list(ov, n_items)
    c_til, c_blk, c_flg = _work_list(ov.T, n_items)
    xs = moe_dispatch(xn, slots, d_blk, d_til, d_flg, n_slots)
    fb_start = jnp.arange(n_ffn_blocks, dtype=I32) * ffn_rows
    block_expert = jnp.minimum(jnp.sum(fb_start[:, None] >= ends[None, :], axis=1), E - 1).astype(I32)
    n_used = (ends[-1] // ffn_rows).astype(I32).reshape(1)
    ys = moe_ffn(xs, w1, w3, w2, block_expert, n_used)
    return moe_combine(ys, slots.T, h, c_til, c_blk, c_flg)


def even_mixer_layer(h, posf, p, bsz, seq):
    m = bsz * seq
    hn = rmsnorm(h, p["norm_mix"], BF16)
    w_in = p["w_in"]
    c = np.cumsum([0, D_SSM, D_CONV, SSM_HEADS, D_NSA] + [D_NSA_KV] * 6 + [3 * NSA_HEADS])
    z = matmul([(hn, w_in[:, c[0]:c[1]].astype(BF16))])
    xbc = matmul([(hn, w_in[:, c[1]:c[2]].astype(BF16))])
    q = matmul([(hn, w_in[:, c[3]:c[4]].astype(BF16))])
    kv6 = matmul([(hn, w_in[:, c[4]:c[10]].astype(BF16))])
    w_small = jnp.concatenate([w_in[:, c[2]:c[3]], w_in[:, c[10]:c[11]]], axis=1)
    w_small = jnp.pad(w_small, ((0, 0), (0, LANES - w_small.shape[1]))).astype(BF16)
    small = matmul([(hn, w_small)])
    y_ssm = ssd_group(z, xbc, small, p["conv_w"], p["conv_b"], p["dt_bias"], p["a_log"], p["d_skip"],
                      p["ssm_norm"], bsz, seq)
    qn, ksn, kwn = rope_norm(posf, q, kv6, p["q_gain"], p["ks_gain"], p["kw_gain"])
    nch = seq // CMP_STRIDE

    def chunks(col0):
        u = kv6[:, col0:col0 + D_NSA_KV].reshape(bsz, nch, CMP_STRIDE, NSA_KV_HEADS, NSA_HEAD_DIM)
        return u.transpose(0, 3, 1, 2, 4).reshape(bsz, NSA_KV_HEADS, nch, CMP_STRIDE * NSA_HEAD_DIM)

    kc, vc = compress(chunks(0), chunks(D_NSA_KV), p["pe_k"], p["pe_v"], p["wk1"], p["wk2"], p["wv1"], p["wv2"],
                      p["kc_gain"])
    o_cmp, selt = cmp_select(qn, kc, vc, bsz, seq)
    tk = min(ATT_TK, seq)

    def v_tiles(col0):
        u = kv6[:, col0:col0 + D_NSA_KV].astype(BF16).reshape(bsz, seq // tk, tk, NSA_KV_HEADS, NSA_HEAD_DIM)
        return u.transpose(0, 3, 1, 4, 2)

    o_sel = masked_attention(qn, ksn, 0, v_tiles(3 * D_NSA_KV), bsz, seq, "sel", selt)
    o_win = masked_attention(qn, kwn, 0, v_tiles(5 * D_NSA_KV), bsz, seq, "win")
    y_nsa = nsa_combine(small, o_cmp, o_sel, o_win)
    w_out = p["w_out"].astype(BF16)
    return matmul([(y_ssm, w_out[:D_SSM]), (y_nsa, w_out[D_SSM:])], residual=h)


def rwkv_layer(h, p, bsz, seq):
    m, d = h.shape
    mixes = rwkv_mix(h, p["norm_mix"], p["mu"], seq)
    bf = lambda w: w.astype(BF16)

    def lora_pad(w_a, w_b):
        r = w_a.shape[1]
        rp = -(-r // LANES) * LANES
        return bf(jnp.pad(w_a, ((0, 0), (0, rp - r)))), bf(jnp.pad(w_b, ((0, rp - r), (0, 0))))

    perm = _wkv_column_order()
    r = matmul([(mixes[0], bf(p["w_r"][:, perm]))])
    k = matmul([(mixes[2], bf(p["w_k"][:, perm]))])
    v = matmul([(mixes[3], bf(p["w_v"][:, perm]))])
    w1, w2 = lora_pad(p["w1"], p["w2"][:, perm])
    w_raw = matmul([(matmul([(mixes[1], w1)], act="tanh", out_dtype=BF16), w2)], bias=p["w0"][perm])
    a1, a2 = lora_pad(p["a1"], p["a2"][:, perm])
    a_sig = matmul([(matmul([(mixes[4], a1)], out_dtype=BF16), a2)], bias=p["a0"][perm], act="sigmoid")
    g1, g2 = lora_pad(p["g1"], p["g2"][:, perm])
    g = matmul([(matmul([(mixes[5], g1)], act="sigmoid", out_dtype=BF16), g2)])
    o = wkv_fused(r, k, v, w_raw, a_sig, g, p["k_k"], p["k_a"], p["r_k"].reshape(-1), p["ln_w"], p["ln_b"], bsz, seq)
    return matmul([(o, bf(p["w_o"][perm, :]))], residual=h)


def kernel(x, mem, positions, norm_mix, norm_xattn, norm_mem, norm_ffn, xattn_wq, xattn_wkv, xattn_wo, xattn_q_gain, xattn_k_gain, ev_w_in, ev_conv_w, ev_conv_b, ev_dt_bias, ev_a_log, ev_d_skip, ev_ssm_norm, ev_q_gain, ev_kc_gain, ev_ks_gain, ev_kw_gain, ev_pe_k, ev_pe_v, ev_cmp_wk1, ev_cmp_wk2, ev_cmp_wv1, ev_cmp_wv2, ev_w_out, ev_ffn_w1, ev_ffn_w3, ev_ffn_w2, od_mu, od_w_r, od_w_k, od_w_v, od_w_o, od_w0, od_w1, od_w2, od_a0, od_a1, od_a2, od_g1, od_g2, od_k_k, od_k_a, od_r_k, od_ln_w, od_ln_b, od_router, od_moe_w1, od_moe_w3, od_moe_w2):
    bsz, seq, d = x.shape
    m = bsz * seq
    depth = norm_mix.shape[0]
    h = x.reshape(m, d)
    mem2d = mem.reshape(-1, d)
    posf = positions.astype(F32).reshape(m, 1)
    for layer in range(depth):
        i = layer // 2
        if layer % 2 == 0:
            p = dict(norm_mix=norm_mix[layer], w_in=ev_w_in[i], conv_w=ev_conv_w[i], conv_b=ev_conv_b[i],
                     dt_bias=ev_dt_bias[i], a_log=ev_a_log[i], d_skip=ev_d_skip[i], ssm_norm=ev_ssm_norm[i],
                     q_gain=ev_q_gain[i], kc_gain=ev_kc_gain[i], ks_gain=ev_ks_gain[i], kw_gain=ev_kw_gain[i],
                     pe_k=ev_pe_k[i], pe_v=ev_pe_v[i], wk1=ev_cmp_wk1[i], wk2=ev_cmp_wk2[i], wv1=ev_cmp_wv1[i],
                     wv2=ev_cmp_wv2[i], w_out=ev_w_out[i])
            h = even_mixer_layer(h, posf, p, bsz, seq)
        else:
            p = dict(norm_mix=norm_mix[layer], mu=od_mu[i], w_r=od_w_r[i], w_k=od_w_k[i], w_v=od_w_v[i], w_o=od_w_o[i],
                     w0=od_w0[i], w1=od_w1[i], w2=od_w2[i], a0=od_a0[i], a1=od_a1[i], a2=od_a2[i], g1=od_g1[i],
                     g2=od_g2[i], k_k=od_k_k[i], k_a=od_k_a[i], r_k=od_r_k[i], ln_w=od_ln_w[i], ln_b=od_ln_b[i])
            h = rwkv_layer(h, p, bsz, seq)
        h = memory_xattn(h, mem2d, norm_xattn[layer], norm_mem[layer], xattn_wq[layer], xattn_wkv[layer],
                         xattn_wo[layer], xattn_q_gain[layer], xattn_k_gain[layer], bsz, seq)
        if layer % 2 == 0:
            hn = rmsnorm(h, norm_ffn[layer], BF16)
            h = swiglu_ffn(hn, ev_ffn_w1[i].astype(BF16), ev_ffn_w3[i].astype(BF16), ev_ffn_w2[i].astype(BF16), h)
        else:
            h = moe_layer(h, norm_ffn[layer], od_router[i], od_moe_w1[i], od_moe_w3[i], od_moe_w2[i])
    return h.reshape(bsz, seq, d)
```

```python
import functools
import math

import numpy as np
import jax
import jax.numpy as jnp
from jax import lax
from jax.experimental import pallas as pl
from jax.experimental.pallas import tpu as pltpu

F32 = jnp.float32
BF16 = jnp.bfloat16
I32 = jnp.int32

D_MODEL = 2048
NORM_EPS = 1e-6
NEG_BIG = -1e30

SSM_HEADS = 32
SSM_HEAD_DIM = 64
D_SSM = SSM_HEADS * SSM_HEAD_DIM
SSM_GROUPS = 4
D_STATE = 128
CONV_WIDTH = 4
SSD_CHUNK = 128
D_CONV = D_SSM + 2 * SSM_GROUPS * D_STATE
D_SSM_GROUP = D_SSM // SSM_GROUPS

NSA_HEADS = 16
NSA_KV_HEADS = 4
NSA_HEAD_DIM = 128
NSA_Q_PER_KV = NSA_HEADS // NSA_KV_HEADS
D_NSA = NSA_HEADS * NSA_HEAD_DIM
D_NSA_KV = NSA_KV_HEADS * NSA_HEAD_DIM
CMP_BLOCK = 32
CMP_STRIDE = 16
CMP_HIDDEN = 256
SEL_BLOCK = 64
N_SELECT = 16
WINDOW = 512
ROPE_DIM = NSA_HEAD_DIM // 4
ROPE_THETA = 500000.0

RWKV_HEAD_DIM = 64
RWKV_HEADS = D_MODEL // RWKV_HEAD_DIM
RWKV_GN_EPS = 1e-5 * RWKV_HEAD_DIM

XATTN_HEADS = 4
XATTN_HEAD_DIM = 128
D_XATTN = XATTN_HEADS * XATTN_HEAD_DIM

N_EXPERTS = 8
TOP_K = 2

LANES = 128
SUBLANES = 8
VMEM_LIMIT_BYTES = 56 * 1024 * 1024

ROW_TILE = 512
MM_TM = 2048
MM_TN = 512
FFN_TM = 512
FFN_TF = 512
MOE_FFN_ROWS = 1024
MOE_TF = 256
COMBINE_TM = 256
MOE_TILE = 256
ATT_TQ = 256
ATT_TK = 512
LOG2E = 1.4426950408889634
CMP_TQ = 128
WKV_T = 32


def _params(*sem):
    return pltpu.CompilerParams(dimension_semantics=sem, vmem_limit_bytes=VMEM_LIMIT_BYTES)


def _sigmoid(x):
    return 1.0 / (1.0 + jnp.exp(-x))


def _silu(x):
    return x * _sigmoid(x)


def _softplus(x):
    return jnp.maximum(x, 0.0) + jnp.log(1.0 + jnp.exp(-jnp.abs(x)))


def _split3(x):
    h = x.astype(BF16)
    r = x - h.astype(F32)
    m = r.astype(BF16)
    l = (r - m.astype(F32)).astype(BF16)
    return h, m, l


def _dot(a, b):
    return jnp.dot(a, b, preferred_element_type=F32)


def _dot_nt(a, b):
    return lax.dot_general(a, b, (((1,), (1,)), ((), ())), preferred_element_type=F32)


def _dot_f32_sel(x, e):
    h, m, l = _split3(x)
    return _dot(h, e) + _dot(m, e) + _dot(l, e)


def _sel_dot_f32(e, x):
    h, m, l = _split3(x)
    return _dot(e, h) + _dot(e, m) + _dot(e, l)


def _rms(x, eps=NORM_EPS):
    return x * lax.rsqrt(jnp.mean(x * x, axis=-1, keepdims=True) + eps)


def _rmsnorm_kernel(x_ref, g_ref, o_ref):
    o_ref[...] = (_rms(x_ref[...]) * g_ref[...]).astype(o_ref.dtype)


def rmsnorm(x, gain, out_dtype):
    m, d = x.shape
    tm = min(ROW_TILE, m)
    return pl.pallas_call(
        _rmsnorm_kernel,
        grid=(m // tm,),
        in_specs=[pl.BlockSpec((tm, d), lambda i: (i, 0)), pl.BlockSpec((1, d), lambda i: (0, 0))],
        out_specs=pl.BlockSpec((tm, d), lambda i: (i, 0)),
        out_shape=jax.ShapeDtypeStruct((m, d), out_dtype),
        compiler_params=_params("parallel"),
        name="rmsnorm",
    )(x, gain.reshape(1, d))


def _mm_kernel(*refs, n_pairs, has_bias, has_res, act):
    o_ref = refs[-1]
    acc = None
    for p in range(n_pairs):
        a = refs[2 * p][...].astype(BF16)
        d = _dot(a, refs[2 * p + 1][...])
        acc = d if acc is None else acc + d
    k = 2 * n_pairs
    if has_bias:
        acc = acc + refs[k][...]
        k += 1
    if act == "tanh":
        acc = jnp.tanh(acc)
    elif act == "sigmoid":
        acc = _sigmoid(acc)
    if has_res:
        acc = acc + refs[k][...]
    o_ref[...] = acc.astype(o_ref.dtype)


def matmul(pairs, bias=None, residual=None, act=None, out_dtype=F32):
    m = pairs[0][0].shape[0]
    n = pairs[0][1].shape[1]
    k_total = sum(a.shape[1] for a, _ in pairs)
    tm = min(MM_TM if k_total <= D_MODEL else MM_TM // 2, m)
    tn = MM_TN if n % MM_TN == 0 else (256 if n % 256 == 0 else LANES)
    tn = min(tn, n)
    in_specs, args = [], []
    for a, w in pairs:
        kk = a.shape[1]
        in_specs += [pl.BlockSpec((tm, kk), lambda i, j: (i, 0)), pl.BlockSpec((kk, tn), lambda i, j: (0, j))]
        args += [a, w]
    if bias is not None:
        in_specs.append(pl.BlockSpec((1, tn), lambda i, j: (0, j)))
        args.append(bias.reshape(1, n).astype(F32))
    if residual is not None:
        in_specs.append(pl.BlockSpec((tm, tn), lambda i, j: (i, j)))
        args.append(residual)
    kern = functools.partial(_mm_kernel, n_pairs=len(pairs), has_bias=bias is not None,
                             has_res=residual is not None, act=act)
    return pl.pallas_call(
        kern,
        grid=(m // tm, n // tn),
        in_specs=in_specs,
        out_specs=pl.BlockSpec((tm, tn), lambda i, j: (i, j)),
        out_shape=jax.ShapeDtypeStruct((m, n), out_dtype),
        compiler_params=_params("parallel", "parallel"),
        name="matmul",
    )(*args)


def _ffn_kernel(x_ref, w1_ref, w3_ref, w2_ref, res_ref, o_ref):
    f = pl.program_id(1)

    @pl.when(f == 0)
    def _():
        o_ref[...] = res_ref[...]

    x = x_ref[...]
    h = _silu(_dot(x, w1_ref[...])) * _dot(x, w3_ref[...])
    o_ref[...] += _dot(h.astype(BF16), w2_ref[...])


def swiglu_ffn(x, w1, w3, w2, residual):
    m, d = x.shape
    ff = w1.shape[1]
    tm = min(FFN_TM, m)
    tf = FFN_TF
    return pl.pallas_call(
        _ffn_kernel,
        grid=(m // tm, ff // tf),
        in_specs=[pl.BlockSpec((tm, d), lambda i, f: (i, 0)),
                  pl.BlockSpec((d, tf), lambda i, f: (0, f)),
                  pl.BlockSpec((d, tf), lambda i, f: (0, f)),
                  pl.BlockSpec((tf, d), lambda i, f: (f, 0)),
                  pl.BlockSpec((tm, d), lambda i, f: (i, 0))],
        out_specs=pl.BlockSpec((tm, d), lambda i, f: (i, 0)),
        out_shape=jax.ShapeDtypeStruct((m, d), F32),
        compiler_params=_params("parallel", "arbitrary"),
        name="swiglu_ffn",
    )(x, w1, w3, w2, residual)


def _ssd_kernel(z_ref, xbc_ref, dt_ref, cw_ref, cb_ref, dtb_ref, alog_ref, dskip_ref, nw_ref, eh_ref,
                o_ref, ext_ref, st_ref):
    L = SSD_CHUNK
    G = SSM_GROUPS
    GW = D_SSM_GROUP
    c = pl.program_id(1)

    @pl.when(c == 0)
    def _():
        ext_ref[L:L + SUBLANES, :] = jnp.zeros((SUBLANES, D_CONV), F32)
        st_ref[...] = jnp.zeros_like(st_ref)

    ext_ref[0:SUBLANES, :] = ext_ref[L:L + SUBLANES, :]
    ext_ref[SUBLANES:L + SUBLANES, :] = xbc_ref[...].astype(F32)
    acc = jnp.zeros((L, D_CONV), F32) + cb_ref[...]
    for k in range(CONV_WIDTH):
        off = SUBLANES - (CONV_WIDTH - 1) + k
        acc = acc + cw_ref[k:k + 1, :] * ext_ref[off:off + L, :]
    xc = _silu(acc)

    dt = _softplus(dt_ref[...] + dtb_ref[...])
    a = -jnp.exp(alog_ref[...])
    adt = dt * a
    row = lax.broadcasted_iota(I32, (L, L), 0)
    col = lax.broadcasted_iota(I32, (L, L), 1)
    causal = row >= col
    tril = jnp.where(causal, 1.0, 0.0).astype(BF16)
    a_cum = _sel_dot_f32(tril, adt)
    a_cum_t = a_cum.T
    eh = eh_ref[...]
    acx = _dot_f32_sel(a_cum, eh)
    dtx = _dot_f32_sel(dt, eh)
    a_end = acx[L - 1:L, :]
    xs = xc[:, :D_SSM]
    xdt = xs * dtx
    xdte = (xdt * jnp.exp(a_end - acx)).astype(BF16)
    from_start = jnp.exp(acx)
    lane = lax.broadcasted_iota(I32, (L, LANES), 1)
    lo_half = lane < SSM_HEAD_DIM
    z = z_ref[...].astype(F32)

    for g in range(G):
        bm = xc[:, D_SSM + g * D_STATE:D_SSM + (g + 1) * D_STATE]
        cm = xc[:, D_SSM + G * D_STATE + g * D_STATE:D_SSM + G * D_STATE + (g + 1) * D_STATE]
        bm_b = bm.astype(BF16)
        cm_b = cm.astype(BF16)
        cb = _dot_nt(cm_b, bm_b)
        st = st_ref[g]
        y_off = _dot(cm_b, st.astype(BF16)) * from_start[:, g * GW:(g + 1) * GW]
        pieces = []
        for pp in range(GW // LANES):
            h0 = g * (GW // SSM_HEAD_DIM) + 2 * pp
            ms = []
            for hh in (h0, h0 + 1):
                diff = a_cum[:, hh:hh + 1] - a_cum_t[hh:hh + 1, :]
                dec = jnp.exp(jnp.where(causal, diff, NEG_BIG))
                ms.append((cb * dec).astype(BF16))
            lhs = jnp.concatenate(ms, axis=1)
            c0 = g * GW + pp * LANES
            xp = xdt[:, c0:c0 + LANES]
            rhs = jnp.concatenate([jnp.where(lo_half, xp, 0.0), jnp.where(lo_half, 0.0, xp)], axis=0)
            pieces.append(_dot(lhs, rhs.astype(BF16)))
        y = jnp.concatenate(pieces, axis=1) + y_off
        y = y + xs[:, g * GW:(g + 1) * GW] * dskip_ref[:, g * GW:(g + 1) * GW]
        y = y * _silu(z[:, g * GW:(g + 1) * GW])
        y = _rms(y) * nw_ref[:, g * GW:(g + 1) * GW]
        o_ref[:, g * GW:(g + 1) * GW] = y.astype(o_ref.dtype)
        s_new = _dot(bm.T.astype(BF16), xdte[:, g * GW:(g + 1) * GW])
        st_ref[g] = st * jnp.exp(a_end[:, g * GW:(g + 1) * GW]) + s_new


def ssd_group(z, xbc, small, conv_w, conv_b, dt_bias, a_log, d_skip, norm_w, bsz, seq):
    m = bsz * seq
    L = SSD_CHUNK
    nc = seq // L
    pad = LANES - SSM_HEADS
    eh = np.zeros((LANES, D_SSM), np.float32)
    for h in range(SSM_HEADS):
        eh[h, h * SSM_HEAD_DIM:(h + 1) * SSM_HEAD_DIM] = 1.0
    row_spec = lambda w: pl.BlockSpec((L, w), lambda b, c: (b * nc + c, 0))
    const = lambda r, w: pl.BlockSpec((r, w), lambda b, c: (0, 0))
    return pl.pallas_call(
        _ssd_kernel,
        grid=(bsz, nc),
        in_specs=[row_spec(D_SSM), row_spec(D_CONV), row_spec(LANES),
                  const(CONV_WIDTH, D_CONV), const(1, D_CONV), const(1, LANES), const(1, LANES),
                  const(1, D_SSM), const(1, D_SSM), const(LANES, D_SSM)],
        out_specs=row_spec(D_SSM),
        out_shape=jax.ShapeDtypeStruct((m, D_SSM), BF16),
        scratch_shapes=[pltpu.VMEM((L + SUBLANES, D_CONV), F32),
                        pltpu.VMEM((SSM_GROUPS, D_STATE, D_SSM_GROUP), F32)],
        compiler_params=_params("parallel", "arbitrary"),
        name="ssd_group",
    )(z, xbc, small, conv_w.T, conv_b.reshape(1, D_CONV),
      jnp.pad(dt_bias, (0, pad)).reshape(1, LANES), jnp.pad(a_log, (0, pad)).reshape(1, LANES),
      jnp.repeat(d_skip, SSM_HEAD_DIM).reshape(1, D_SSM), norm_w.reshape(1, D_SSM), jnp.asarray(eh, BF16))


def _rope_norm_kernel(pos_ref, invf_ref, q_ref, kv_ref, qg_ref, ksg_ref, kwg_ref, qo_ref, kso_ref, kwo_ref):
    ang = pos_ref[...] * invf_ref[...]
    cos = jnp.cos(ang)
    sin = jnp.sin(ang)
    lane = lax.broadcasted_iota(I32, ang.shape, 1)
    first = lane < ROPE_DIM // 2
    sin_signed = jnp.where(first, -sin, sin)

    def head(x, gain):
        y = _rms(x.astype(F32)) * gain
        partner = jnp.where(first, pltpu.roll(y, LANES - ROPE_DIM // 2, axis=1), pltpu.roll(y, ROPE_DIM // 2, axis=1))
        return y * cos + partner * sin_signed

    hd = NSA_HEAD_DIM
    scale = hd ** -0.5 * LOG2E
    for h in range(NSA_HEADS):
        qo_ref[:, h * hd:(h + 1) * hd] = (head(q_ref[:, h * hd:(h + 1) * hd], qg_ref[...]) * scale).astype(qo_ref.dtype)
    for h in range(NSA_KV_HEADS):
        ks = kv_ref[:, 2 * D_NSA_KV + h * hd:2 * D_NSA_KV + (h + 1) * hd]
        kw = kv_ref[:, 4 * D_NSA_KV + h * hd:4 * D_NSA_KV + (h + 1) * hd]
        kso_ref[:, h * hd:(h + 1) * hd] = head(ks, ksg_ref[...]).astype(kso_ref.dtype)
        kwo_ref[:, h * hd:(h + 1) * hd] = head(kw, kwg_ref[...]).astype(kwo_ref.dtype)


def rope_norm(posf, q, kv6, q_gain, ks_gain, kw_gain):
    m = q.shape[0]
    tm = min(ROW_TILE, m)
    half = ROPE_DIM // 2
    inv = np.exp(-math.log(ROPE_THETA) * np.arange(0, ROPE_DIM, 2, dtype=np.float32) / ROPE_DIM).astype(np.float32)
    invf = np.zeros((1, LANES), np.float32)
    invf[0, :half] = inv
    invf[0, half:ROPE_DIM] = inv
    row = lambda w: pl.BlockSpec((tm, w), lambda i: (i, 0))
    const = pl.BlockSpec((1, LANES), lambda i: (0, 0))
    return pl.pallas_call(
        _rope_norm_kernel,
        grid=(m // tm,),
        in_specs=[row(1), const, row(D_NSA), row(6 * D_NSA_KV), const, const, const],
        out_specs=[row(D_NSA), row(D_NSA_KV), row(D_NSA_KV)],
        out_shape=[jax.ShapeDtypeStruct((m, D_NSA), BF16), jax.ShapeDtypeStruct((m, D_NSA_KV), BF16),
                   jax.ShapeDtypeStruct((m, D_NSA_KV), BF16)],
        compiler_params=_params("parallel"),
        name="rope_norm",
    )(posf, jnp.asarray(invf), q, kv6, q_gain.reshape(1, LANES), ks_gain.reshape(1, LANES), kw_gain.reshape(1, LANES))


def _compress_kernel(uk_ref, uv_ref, pek_ref, pev_ref, wk1_ref, wk2_ref, wv1_ref, wv2_ref, g_ref, kc_ref, vc_ref):
    def mlp(u, pe_ref, w1_ref, w2_ref):
        u = u.astype(F32)
        n = u.shape[0]
        h1 = _dot((u + pe_ref[0:1, :]).astype(BF16), w1_ref[0])
        h2 = _dot((u + pe_ref[1:2, :]).astype(BF16), w1_ref[1])
        pre = h1 + pltpu.roll(h2, n - 1, axis=0)
        return _dot(_silu(pre).astype(BF16), w2_ref[...])

    kc = mlp(uk_ref[0, 0], pek_ref, wk1_ref, wk2_ref)
    kc_ref[0, 0] = _rms(kc) * g_ref[...]
    vc_ref[0, 0] = mlp(uv_ref[0, 0], pev_ref, wv1_ref, wv2_ref)


def compress(uk, uv, pe_k, pe_v, wk1, wk2, wv1, wv2, kc_gain):
    bsz, nkv, nch, width = uk.shape
    half = CMP_STRIDE * NSA_HEAD_DIM
    u_spec = pl.BlockSpec((1, 1, nch, width), lambda b, g: (b, g, 0, 0))
    o_spec = pl.BlockSpec((1, 1, nch, NSA_HEAD_DIM), lambda b, g: (b, g, 0, 0))
    c2 = lambda s: pl.BlockSpec(s, lambda b, g: (0, 0))
    c3 = lambda s: pl.BlockSpec(s, lambda b, g: (0, 0, 0))
    o_shape = jax.ShapeDtypeStruct((bsz, nkv, nch, NSA_HEAD_DIM), F32)
    return pl.pallas_call(
        _compress_kernel,
        grid=(bsz, nkv),
        in_specs=[u_spec, u_spec, c2((2, half)), c2((2, half)),
                  c3((2, half, CMP_HIDDEN)), c2((CMP_HIDDEN, NSA_HEAD_DIM)),
                  c3((2, half, CMP_HIDDEN)), c2((CMP_HIDDEN, NSA_HEAD_DIM)), c2((1, NSA_HEAD_DIM))],
        out_specs=[o_spec, o_spec],
        out_shape=[o_shape, o_shape],
        compiler_params=_params("parallel", "parallel"),
        name="nsa_compress",
    )(uk, uv, pe_k.reshape(2, half), pe_v.reshape(2, half),
      wk1.reshape(2, half, CMP_HIDDEN).astype(BF16), wk2.astype(BF16),
      wv1.reshape(2, half, CMP_HIDDEN).astype(BF16), wv2.astype(BF16), kc_gain.reshape(1, NSA_HEAD_DIM))


def _stack_heads(q):
    return jnp.concatenate([q[:, r * NSA_HEAD_DIM:(r + 1) * NSA_HEAD_DIM] for r in range(NSA_Q_PER_KV)], axis=0)


def _unstack_heads(o, tq):
    return jnp.concatenate([o[r * tq:(r + 1) * tq, :] for r in range(NSA_Q_PER_KV)], axis=1)


def _cmp_select_kernel(q_ref, kc_ref, vc_ref, ovt_ref, o_ref, sel_ref, *, tq, n_blk):
    R = NSA_Q_PER_KV
    q0 = pl.program_id(2) * tq
    q4 = _stack_heads(q_ref[...])
    kc = kc_ref[0, 0].astype(BF16)
    ncmp = kc.shape[0]
    s = _dot_nt(q4, kc)
    t_row = q0 + lax.broadcasted_iota(I32, (tq, ncmp), 0)
    c_col = lax.broadcasted_iota(I32, (tq, ncmp), 1)
    visf = jnp.where((c_col * CMP_STRIDE + CMP_BLOCK - 1) <= t_row, 1.0, 0.0)
    vis = jnp.concatenate([visf] * R, axis=0) > 0.5
    s = jnp.where(vis, s, NEG_BIG)
    mx = jnp.max(s, axis=-1, keepdims=True)
    e = jnp.where(vis, jnp.exp2(s - mx), 0.0)
    den = jnp.sum(e, axis=-1, keepdims=True)
    p = e / jnp.maximum(den, 1e-30)
    o = _dot(p.astype(BF16), vc_ref[0, 0].astype(BF16))
    o_ref[...] = _unstack_heads(o, tq)

    psum = p[0:tq]
    for r in range(1, R):
        psum = psum + p[r * tq:(r + 1) * tq]
    h, m, l = _split3(psum)
    ovt = ovt_ref[...]
    imp = _dot_nt(ovt, h) + _dot_nt(ovt, m) + _dot_nt(ovt, l)
    jdx = lax.broadcasted_iota(I32, (n_blk, tq), 0)
    t_lane = q0 + lax.broadcasted_iota(I32, (n_blk, tq), 1)
    cur = t_lane // SEL_BLOCK
    causal = jdx <= cur
    forced = ((jdx == 0) | (jdx >= cur - 1)) & causal
    score = jnp.where(forced, jnp.inf, jnp.where(causal, imp, -jnp.inf))
    rank = jnp.zeros((n_blk, tq), F32)
    for jp in range(n_blk):
        rowv = score[jp:jp + 1, :]
        beats = (rowv > score) | ((rowv == score) & (jdx > jp))
        rank = rank + jnp.where(beats, 1.0, 0.0)
    sel = jnp.where((rank < float(min(N_SELECT, n_blk))) & causal, 1.0, 0.0)
    if n_blk < LANES:
        sel = jnp.concatenate([sel, jnp.zeros((LANES - n_blk, tq), F32)], axis=0)
    sel_ref[0, 0] = sel.astype(sel_ref.dtype)


def cmp_select(qn, kc, vc, bsz, seq):
    m = bsz * seq
    G = NSA_KV_HEADS
    tq = CMP_TQ
    nq = seq // tq
    ncmp = kc.shape[2]
    n_blk = seq // SEL_BLOCK
    assert n_blk <= LANES and tq == LANES
    c0 = np.arange(ncmp)[None, :] * CMP_STRIDE
    s0 = np.arange(n_blk)[:, None] * SEL_BLOCK
    ov = np.clip(np.minimum(c0 + CMP_BLOCK, s0 + SEL_BLOCK) - np.maximum(c0, s0), 0, None) / CMP_STRIDE
    n_cmp_valid = (seq - CMP_BLOCK) // CMP_STRIDE + 1
    ov[:, n_cmp_valid:] = 0.0
    kv_spec = pl.BlockSpec((1, 1, ncmp, NSA_HEAD_DIM), lambda b, g, i: (b, g, 0, 0))
    kern = functools.partial(_cmp_select_kernel, tq=tq, n_blk=n_blk)
    return pl.pallas_call(
        kern,
        grid=(bsz, G, nq),
        in_specs=[pl.BlockSpec((tq, NSA_Q_PER_KV * NSA_HEAD_DIM), lambda b, g, i: (b * nq + i, g)),
                  kv_spec, kv_spec, pl.BlockSpec((n_blk, ncmp), lambda b, g, i: (0, 0))],
        out_specs=[pl.BlockSpec((tq, NSA_Q_PER_KV * NSA_HEAD_DIM), lambda b, g, i: (b * nq + i, g)),
                   pl.BlockSpec((1, 1, LANES, tq), lambda b, g, i: (b, g, 0, i))],
        out_shape=[jax.ShapeDtypeStruct((m, D_NSA), F32), jax.ShapeDtypeStruct((bsz, G, LANES, seq), BF16)],
        compiler_params=_params("parallel", "parallel", "parallel"),
        name="nsa_cmp_select",
    )(qn, kc, vc, jnp.asarray(ov, BF16))


def _masked_attn_kernel(q_ref, k_ref, vt_ref, *rest, mode, tq, tk):
    if mode == "sel":
        selt_ref, o_ref = rest
    else:
        (o_ref,) = rest
    R = NSA_Q_PER_KV
    hd = NSA_HEAD_DIM
    q0 = pl.program_id(2) * tq
    key_row = lax.broadcasted_iota(I32, (tk, tq), 0)
    t_lane = q0 + lax.broadcasted_iota(I32, (tk, tq), 1)
    if mode == "sel":
        key_row_b = lax.broadcasted_iota(I32, (tk, LANES), 0)
        blk_lane = lax.broadcasted_iota(I32, (tk, LANES), 1)
        lo = 0
    else:
        lo = jnp.maximum(q0 - (WINDOW - 1), 0) // tk
    hi = (q0 + tq - 1) // tk + 1
    q = q_ref[...]

    def body(kt, carry):
        k0 = kt * tk
        k_t = k_ref[pl.ds(pl.multiple_of(k0, tk), tk), :]
        v_t = vt_ref[0, 0, kt]
        key = k0 + key_row
        if mode == "sel":
            expand = jnp.where((k0 + key_row_b) // SEL_BLOCK == blk_lane, 1.0, 0.0).astype(BF16)
            picked = _dot(expand, selt_ref[0, 0])
            ok = (picked > 0.5) & (key <= t_lane)
        else:
            ok = (key <= t_lane) & (key > t_lane - WINDOW)
        bias = jnp.where(ok, 0.0, NEG_BIG)
        ss = [_dot_nt(k_t, q[:, r * hd:(r + 1) * hd]) + bias for r in range(R)]
        stats = []
        for r in range(R):
            m_prev, l_prev = carry[3 * r], carry[3 * r + 1]
            m_new = jnp.maximum(m_prev, jnp.max(ss[r], axis=0, keepdims=True))
            p = jnp.exp2(ss[r] - m_new)
            alpha = jnp.exp2(m_prev - m_new)
            stats.append((m_new, alpha * l_prev + jnp.sum(p, axis=0, keepdims=True), alpha, p.astype(BF16)))
        new = []
        for r in range(R):
            m_new, l_new, alpha, p = stats[r]
            new += [m_new, l_new, alpha * carry[3 * r + 2] + _dot(v_t, p)]
        return tuple(new)

    init = (jnp.full((1, tq), NEG_BIG, F32), jnp.zeros((1, tq), F32), jnp.zeros((hd, tq), F32)) * R
    out = lax.fori_loop(lo, hi, body, init)
    o_ref[...] = jnp.concatenate([(out[3 * r + 2] / out[3 * r + 1]).T for r in range(R)], axis=1)


def masked_attention(qn, k, k_col0, vt, bsz, seq, mode, selt=None):
    m = bsz * seq
    G = NSA_KV_HEADS
    tq = min(ATT_TQ, seq)
    tk = vt.shape[-1]
    nq = seq // tq
    hd = NSA_HEAD_DIM
    q_spec = pl.BlockSpec((tq, NSA_Q_PER_KV * hd), lambda b, g, i: (b * nq + i, g))
    in_specs = [q_spec,
                pl.BlockSpec((seq, hd), lambda b, g, i: (b, k_col0 + g)),
                pl.BlockSpec((1, 1, seq // tk, hd, tk), lambda b, g, i: (b, g, 0, 0, 0))]
    args = [qn, k, vt]
    if mode == "sel":
        in_specs.append(pl.BlockSpec((1, 1, LANES, tq), lambda b, g, i: (b, g, 0, i)))
        args.append(selt)
    return pl.pallas_call(
        functools.partial(_masked_attn_kernel, mode=mode, tq=tq, tk=tk),
        grid=(bsz, G, nq),
        in_specs=in_specs,
        out_specs=q_spec,
        out_shape=jax.ShapeDtypeStruct((m, D_NSA), F32),
        compiler_params=_params("parallel", "parallel", "arbitrary"),
        name="nsa_attn_" + mode,
    )(*args)


def _nsa_combine_kernel(g_ref, e_ref, oc_ref, os_ref, ow_ref, o_ref):
    gate = _sigmoid(g_ref[...])
    gx = _dot_f32_sel(gate, e_ref[...])
    y = gx[:, :D_NSA] * oc_ref[...] + gx[:, D_NSA:2 * D_NSA] * os_ref[...] + gx[:, 2 * D_NSA:] * ow_ref[...]
    o_ref[...] = y.astype(o_ref.dtype)


def nsa_combine(small, o_cmp, o_sel, o_win):
    m = small.shape[0]
    tm = min(COMBINE_TM, m)
    e = np.zeros((LANES, 3 * D_NSA), np.float32)
    for h in range(NSA_HEADS):
        for br in range(3):
            e[SSM_HEADS + 3 * h + br, br * D_NSA + h * NSA_HEAD_DIM:br * D_NSA + (h + 1) * NSA_HEAD_DIM] = 1.0
    row = lambda w: pl.BlockSpec((tm, w), lambda i: (i, 0))
    return pl.pallas_call(
        _nsa_combine_kernel,
        grid=(m // tm,),
        in_specs=[row(LANES), pl.BlockSpec((LANES, 3 * D_NSA), lambda i: (0, 0)), row(D_NSA), row(D_NSA), row(D_NSA)],
        out_specs=row(D_NSA),
        out_shape=jax.ShapeDtypeStruct((m, D_NSA), BF16),
        compiler_params=_params("parallel"),
        name="nsa_combine",
    )(small, jnp.asarray(e, BF16), o_cmp, o_sel, o_win)


def _xattn_kernel(q_ref, kv_ref, qg_ref, kg_ref, o_ref):
    hd = XATTN_HEAD_DIM
    scale = hd ** -0.5
    outs = []
    for h in range(XATTN_HEADS):
        q = (_rms(q_ref[:, h * hd:(h + 1) * hd]) * qg_ref[...]).astype(BF16)
        k = (_rms(kv_ref[:, h * hd:(h + 1) * hd]) * kg_ref[...]).astype(BF16)
        v = kv_ref[:, D_XATTN + h * hd:D_XATTN + (h + 1) * hd].astype(BF16)
        s = _dot_nt(q, k) * scale
        e = jnp.exp(s - jnp.max(s, axis=-1, keepdims=True))
        p = e / jnp.sum(e, axis=-1, keepdims=True)
        outs.append(_dot(p.astype(BF16), v))
    o_ref[...] = jnp.concatenate(outs, axis=1).astype(o_ref.dtype)


def xattn_core(q, kv, q_gain, k_gain, bsz, seq):
    m = bsz * seq
    n_mem = kv.shape[0] // bsz
    tm = min(ROW_TILE, seq)
    nt = seq // tm
    return pl.pallas_call(
        _xattn_kernel,
        grid=(bsz, nt),
        in_specs=[pl.BlockSpec((tm, D_XATTN), lambda b, i: (b * nt + i, 0)),
                  pl.BlockSpec((n_mem, 2 * D_XATTN), lambda b, i: (b, 0)),
                  pl.BlockSpec((1, XATTN_HEAD_DIM), lambda b, i: (0, 0)),
                  pl.BlockSpec((1, XATTN_HEAD_DIM), lambda b, i: (0, 0))],
        out_specs=pl.BlockSpec((tm, D_XATTN), lambda b, i: (b * nt + i, 0)),
        out_shape=jax.ShapeDtypeStruct((m, D_XATTN), BF16),
        compiler_params=_params("parallel", "parallel"),
        name="xattn_core",
    )(q, kv, q_gain.reshape(1, XATTN_HEAD_DIM), k_gain.reshape(1, XATTN_HEAD_DIM))


def memory_xattn(h, mem2d, norm_x, norm_m, wq, wkv, wo, q_gain, k_gain, bsz, seq):
    hq = rmsnorm(h, norm_x, BF16)
    q = matmul([(hq, wq.astype(BF16))])
    mem_n = rmsnorm(mem2d, norm_m, BF16)
    kv = matmul([(mem_n, wkv.astype(BF16))])
    o = xattn_core(q, kv, q_gain, k_gain, bsz, seq)
    return matmul([(o, wo.astype(BF16))], residual=h)


def _mix_kernel(h_ref, hp_ref, g_ref, mu_ref, *rest, tm, tiles_per_seq):
    outs, ext_ref = rest[:-1], rest[-1]
    i = pl.program_id(0)
    hn = _rms(h_ref[...]) * g_ref[...]
    hp = _rms(hp_ref[...]) * g_ref[...]
    hp = jnp.where(i % tiles_per_seq == 0, 0.0, hp)
    ext_ref[0:SUBLANES, :] = hp
    ext_ref[SUBLANES:tm + SUBLANES, :] = hn
    xx = ext_ref[SUBLANES - 1:tm + SUBLANES - 1, :] - hn
    for j, o_ref in enumerate(outs):
        o_ref[...] = (hn + xx * mu_ref[j:j + 1, :]).astype(o_ref.dtype)


def rwkv_mix(h, gain, mu, seq):
    m, d = h.shape
    tm = min(ROW_TILE, seq)
    n_mix = mu.shape[0]
    kern = functools.partial(_mix_kernel, tm=tm, tiles_per_seq=seq // tm)
    row = pl.BlockSpec((tm, d), lambda i: (i, 0))
    return pl.pallas_call(
        kern,
        grid=(m // tm,),
        in_specs=[row, pl.BlockSpec((SUBLANES, d), lambda i: (jnp.maximum(i * (tm // SUBLANES) - 1, 0), 0)),
                  pl.BlockSpec((1, d), lambda i: (0, 0)), pl.BlockSpec((n_mix, d), lambda i: (0, 0))],
        out_specs=[row] * n_mix,
        out_shape=[jax.ShapeDtypeStruct((m, d), BF16)] * n_mix,
        scratch_shapes=[pltpu.VMEM((tm + SUBLANES, d), F32)],
        compiler_params=_params("parallel"),
        name="rwkv_mix",
    )(h, h, gain.reshape(1, d), mu)


WKV_LANE_GROUP = LANES // 4
WKV_ROWS_PER_BATCH = D_MODEL // LANES


def _wkv_fused_kernel(r_ref, k_ref, v_ref, w_ref, a_ref, g_ref, kkp_ref, kap_ref, rkp_ref, lnw_ref, lnb_ref,
                      o_ref, s_ref, wd_s, kh_s, kk_s, b_s, r_s, v_s, g_s, y_s, *, steps):
    n = RWKV_HEAD_DIM
    nb = LANES // WKV_LANE_GROUP
    lane_grp = lax.broadcasted_iota(I32, (steps, SUBLANES, LANES), 2) // WKV_LANE_GROUP

    @pl.when(pl.program_id(0) == 0)
    def _():
        s_ref[...] = jnp.zeros_like(s_ref)

    def to_scan(x_ref):
        pieces = []
        for n4 in range(nb):
            for v in range(WKV_ROWS_PER_BATCH // SUBLANES):
                acc = None
                for b in range(nb):
                    x = x_ref[b, :, SUBLANES * v:SUBLANES * (v + 1), :]
                    shift = (WKV_LANE_GROUP * (b - n4)) % LANES
                    if shift:
                        x = pltpu.roll(x, shift, axis=2)
                    acc = x if acc is None else jnp.where(lane_grp == b, x, acc)
                pieces.append(acc)
        return pieces

    rs, ks, vs, ws, as_, gs = (to_scan(x) for x in (r_ref, k_ref, v_ref, w_ref, a_ref, g_ref))
    kks = [ks[u] * kkp_ref[SUBLANES * u:SUBLANES * (u + 1), :] for u in range(len(ks))]
    ss = kks[0] * kks[0]
    for u in range(1, len(kks)):
        ss = ss + kks[u] * kks[u]
    inv_norm = 1.0 / jnp.maximum(jnp.sqrt(jnp.sum(ss, axis=1, keepdims=True)), 1e-12)
    for u in range(len(ks)):
        rows = slice(SUBLANES * u, SUBLANES * (u + 1))
        kk = kks[u] * inv_norm
        wd_s[:, rows, :] = jnp.exp(-jnp.exp(-_softplus(-ws[u]) - 0.5))
        kk_s[:, rows, :] = kk
        b_s[:, rows, :] = kk * as_[u]
        kh_s[:, rows, :] = ks[u] * (1.0 + (as_[u] - 1.0) * kap_ref[rows, :])
        r_s[:, rows, :] = rs[u]
        v_s[:, rows, :] = vs[u]
        g_s[:, rows, :] = gs[u]

    q0 = jnp.zeros((n, LANES), F32)
    for j in range(n):
        q0 = q0 + s_ref[j] * kk_s[0, j:j + 1, :]

    def step(t, q):
        tn = jnp.minimum(t + 1, steps - 1)
        v = v_s[t]
        y = jnp.zeros((n, LANES), F32)
        qn = jnp.zeros((n, LANES), F32)
        for j in range(n):
            s_new = s_ref[j] * wd_s[t, j:j + 1, :] - q * b_s[t, j:j + 1, :] + v * kh_s[t, j:j + 1, :]
            s_ref[j] = s_new
            y = y + s_new * r_s[t, j:j + 1, :]
            qn = qn + s_new * kk_s[tn, j:j + 1, :]
        y_s[t] = y
        return qn

    lax.fori_loop(0, steps, step, q0)

    y = y_s[...]
    inv_n = 1.0 / n
    mean = jnp.sum(y, axis=1, keepdims=True) * inv_n
    yc = y - mean
    var = jnp.sum(yc * yc, axis=1, keepdims=True) * inv_n
    bonus = jnp.sum(r_s[...] * kh_s[...] * rkp_ref[...], axis=1, keepdims=True) * v_s[...]
    o = (yc * lax.rsqrt(var + RWKV_GN_EPS) * lnw_ref[...] + lnb_ref[...] + bonus) * g_s[...]
    for b in range(nb):
        for v in range(WKV_ROWS_PER_BATCH // SUBLANES):
            acc = None
            for n4 in range(nb):
                u = n4 * (WKV_ROWS_PER_BATCH // SUBLANES) + v
                x = o[:, SUBLANES * u:SUBLANES * (u + 1), :]
                shift = (WKV_LANE_GROUP * (n4 - b)) % LANES
                if shift:
                    x = pltpu.roll(x, shift, axis=2)
                acc = x if acc is None else jnp.where(lane_grp == n4, x, acc)
            o_ref[b, :, SUBLANES * v:SUBLANES * (v + 1), :] = acc


def _wkv_column_order():
    c, n4, h = np.meshgrid(np.arange(WKV_ROWS_PER_BATCH), np.arange(LANES // WKV_LANE_GROUP), np.arange(RWKV_HEADS),
                           indexing="ij")
    return (h * RWKV_HEAD_DIM + n4 * WKV_ROWS_PER_BATCH + c).reshape(-1)


def _scan_tile(p, bsz):
    return jnp.tile(p.reshape(RWKV_HEADS, RWKV_HEAD_DIM).T, (1, bsz))


def wkv_fused(r, k, v, w_raw, a_sig, g, k_k, k_a, r_k, ln_w, ln_b, bsz, seq):
    assert bsz * RWKV_HEADS == LANES and RWKV_HEADS == WKV_LANE_GROUP
    steps = min(WKV_T, seq)
    view = lambda x: x.reshape(bsz, seq, WKV_ROWS_PER_BATCH, LANES)
    blk = pl.BlockSpec((bsz, steps, WKV_ROWS_PER_BATCH, LANES), lambda i: (0, i, 0, 0))
    par = pl.BlockSpec((RWKV_HEAD_DIM, LANES), lambda i: (0, 0))
    scr = pltpu.VMEM((steps, RWKV_HEAD_DIM, LANES), F32)
    out = pl.pallas_call(
        functools.partial(_wkv_fused_kernel, steps=steps),
        grid=(seq // steps,),
        in_specs=[blk] * 6 + [par] * 5,
        out_specs=blk,
        out_shape=jax.ShapeDtypeStruct((bsz, seq, WKV_ROWS_PER_BATCH, LANES), F32),
        scratch_shapes=[pltpu.VMEM((RWKV_HEAD_DIM, RWKV_HEAD_DIM, LANES), F32)] + [scr] * 8,
        compiler_params=_params("arbitrary"),
        name="wkv_fused",
    )(view(r), view(k), view(v), view(w_raw), view(a_sig), view(g),
      _scan_tile(k_k, bsz), _scan_tile(k_a, bsz), _scan_tile(r_k, bsz), _scan_tile(ln_w, bsz), _scan_tile(ln_b, bsz))
    return out.reshape(bsz * seq, D_MODEL)


def _router_kernel(h_ref, g_ref, wr_ref, info_ref, tile_cnt_ref, total_ref, carry_ref, *, tt):
    E = N_EXPERTS
    i = pl.program_id(0)

    @pl.when(i == 0)
    def _():
        carry_ref[...] = jnp.zeros_like(carry_ref)

    xh, xm, xl = _split3(_rms(h_ref[...]) * g_ref[...])
    wh, wm, wl = _split3(wr_ref[...])
    logits = (_dot_nt(wh, xh) + _dot_nt(wh, xm) + _dot_nt(wm, xh)
              + _dot_nt(wh, xl) + _dot_nt(wl, xh) + _dot_nt(wm, xm))
    eidx = lax.broadcasted_iota(I32, (E, tt), 0)
    v1 = jnp.max(logits, axis=0, keepdims=True)
    i1 = jnp.min(jnp.where(logits == v1, eidx, E), axis=0, keepdims=True)
    rest = jnp.where(eidx == i1, -jnp.inf, logits)
    v2 = jnp.max(rest, axis=0, keepdims=True)
    i2 = jnp.min(jnp.where(rest == v2, eidx, E), axis=0, keepdims=True)
    e2 = jnp.exp(v2 - v1)
    g1 = 1.0 / (1.0 + e2)
    g2 = e2 / (1.0 + e2)
    hit1 = eidx == i1
    hit2 = eidx == i2
    onehot = jnp.where(hit1 | hit2, 1.0, 0.0)
    rr = lax.broadcasted_iota(I32, (tt, tt), 0)
    cc = lax.broadcasted_iota(I32, (tt, tt), 1)
    before = jnp.where(rr < cc, 1.0, 0.0).astype(BF16)
    carry = carry_ref[...]
    carry_wide = jnp.concatenate([carry] * (tt // LANES), axis=1)
    cum = _dot(onehot.astype(BF16), before) + carry_wide
    rank1 = jnp.sum(jnp.where(hit1, cum, 0.0), axis=0, keepdims=True)
    rank2 = jnp.sum(jnp.where(hit2, cum, 0.0), axis=0, keepdims=True)
    zero = jnp.zeros((1, tt), F32)
    info_ref[...] = jnp.concatenate([i1.astype(F32), i2.astype(F32), rank1, rank2, g1, g2, zero, zero], axis=0)
    tile_cnt_ref[0] = carry
    carry = carry + _dot(onehot.astype(BF16), jnp.ones((tt, LANES), BF16))
    carry_ref[...] = carry
    total_ref[...] = carry


def moe_router(h, norm_gain, w_router):
    m, d = h.shape
    tt = min(MOE_TILE, m)
    nt = m // tt
    return pl.pallas_call(
        functools.partial(_router_kernel, tt=tt),
        grid=(nt,),
        in_specs=[pl.BlockSpec((tt, d), lambda i: (i, 0)), pl.BlockSpec((1, d), lambda i: (0, 0)),
                  pl.BlockSpec((N_EXPERTS, d), lambda i: (0, 0))],
        out_specs=[pl.BlockSpec((SUBLANES, tt), lambda i: (0, i)),
                   pl.BlockSpec((1, N_EXPERTS, LANES), lambda i: (i, 0, 0)),
                   pl.BlockSpec((N_EXPERTS, LANES), lambda i: (0, 0))],
        out_shape=[jax.ShapeDtypeStruct((SUBLANES, m), F32), jax.ShapeDtypeStruct((nt, N_EXPERTS, LANES), F32),
                   jax.ShapeDtypeStruct((N_EXPERTS, LANES), F32)],
        scratch_shapes=[pltpu.VMEM((N_EXPERTS, LANES), F32)],
        compiler_params=_params("arbitrary"),
        name="moe_router",
    )(h, norm_gain.reshape(1, d), w_router.T)


def _slots_kernel(start_ref, info_ref, o_ref):
    info = info_ref[...]
    e1, e2, r1, r2 = info[0:1], info[1:2], info[2:3], info[3:4]
    s1 = r1
    s2 = r2
    for e in range(N_EXPERTS):
        st = start_ref[e].astype(F32)
        s1 = s1 + jnp.where(e1 == float(e), st, 0.0)
        s2 = s2 + jnp.where(e2 == float(e), st, 0.0)
    o_ref[...] = jnp.concatenate([s1, s2, info[4:5], info[5:6], info[6:8], info[6:8]], axis=0)


def moe_slots(info, slot_start):
    m = info.shape[1]
    tt = min(2048, m)
    return pl.pallas_call(
        _slots_kernel,
        grid_spec=pltpu.PrefetchScalarGridSpec(
            num_scalar_prefetch=1, grid=(m // tt,),
            in_specs=[pl.BlockSpec((SUBLANES, tt), lambda i, st: (0, i))],
            out_specs=pl.BlockSpec((SUBLANES, tt), lambda i, st: (0, i))),
        out_shape=jax.ShapeDtypeStruct((SUBLANES, m), F32),
        compiler_params=_params("parallel"),
        name="moe_slots",
    )(slot_start, info)


def _dispatch_kernel(blk_ref, til_ref, flg_ref, slots_ref, x_ref, o_ref, *, bs):
    w = pl.program_id(0)
    flags = flg_ref[w]
    first = (flags & 1) != 0
    real = (flags & 4) != 0

    def gathered():
        sl = slots_ref[...]
        sid = (blk_ref[w] * bs + lax.broadcasted_iota(I32, (bs, sl.shape[1]), 0)).astype(F32)
        onehot = jnp.where((sid == sl[0:1, :]) | (sid == sl[1:2, :]), 1.0, 0.0).astype(BF16)
        return _dot(onehot, x_ref[...]).astype(o_ref.dtype)

    @pl.when(first & real)
    def _():
        o_ref[...] = gathered()

    @pl.when(first & jnp.logical_not(real))
    def _():
        o_ref[...] = jnp.zeros_like(o_ref)

    @pl.when(jnp.logical_not(first) & real)
    def _():
        o_ref[...] += gathered()


def moe_dispatch(xn, slots, wl_blk, wl_til, wl_flg, n_slots):
    m, d = xn.shape
    bs = tt = MOE_TILE
    n_items = wl_blk.shape[0]
    return pl.pallas_call(
        functools.partial(_dispatch_kernel, bs=bs),
        grid_spec=pltpu.PrefetchScalarGridSpec(
            num_scalar_prefetch=3, grid=(n_items,),
            in_specs=[pl.BlockSpec((SUBLANES, tt), lambda w, b, t, f: (0, t[w])),
                      pl.BlockSpec((tt, d), lambda w, b, t, f: (t[w], 0))],
            out_specs=pl.BlockSpec((bs, d), lambda w, b, t, f: (b[w], 0))),
        out_shape=jax.ShapeDtypeStruct((n_slots, d), BF16),
        compiler_params=_params("arbitrary"),
        name="moe_dispatch",
    )(wl_blk, wl_til, wl_flg, slots, xn)


def _moe_ffn_kernel(be_ref, rows_ref, x_ref, w1_ref, w3_ref, w2_ref, o_ref, acc_ref, *, sub):
    b = pl.program_id(0)
    f = pl.program_id(1)
    bm = x_ref.shape[0]

    @pl.when(f == 0)
    def _():
        acc_ref[...] = jnp.zeros_like(acc_ref)

    for rows in range(sub, bm + 1, sub):
        @pl.when(rows_ref[b] == rows)
        def _(rows=rows):
            x = x_ref[0:rows, :]
            h = _silu(_dot(x, w1_ref[0].astype(BF16))) * _dot(x, w3_ref[0].astype(BF16))
            acc_ref[0:rows, :] += _dot(h.astype(BF16), w2_ref[0].astype(BF16))

    @pl.when(f == pl.num_programs(1) - 1)
    def _():
        o_ref[...] = acc_ref[...].astype(o_ref.dtype)


def moe_ffn(xs, w1, w3, w2, block_expert, block_rows):
    n_slots, d = xs.shape
    ff = w1.shape[2]
    bm = MOE_FFN_ROWS
    tf = MOE_TF
    nf = ff // tf

    def f_of(b, f, rows):
        return jnp.where(rows[b] > 0, f, nf - 1)

    return pl.pallas_call(
        functools.partial(_moe_ffn_kernel, sub=MOE_TILE),
        grid_spec=pltpu.PrefetchScalarGridSpec(
            num_scalar_prefetch=2, grid=(n_slots // bm, nf),
            in_specs=[pl.BlockSpec((bm, d), lambda b, f, be, nb: (b, 0)),
                      pl.BlockSpec((1, d, tf), lambda b, f, be, nb: (be[b], 0, f_of(b, f, nb))),
                      pl.BlockSpec((1, d, tf), lambda b, f, be, nb: (be[b], 0, f_of(b, f, nb))),
                      pl.BlockSpec((1, tf, d), lambda b, f, be, nb: (be[b], f_of(b, f, nb), 0))],
            out_specs=pl.BlockSpec((bm, d), lambda b, f, be, nb: (b, 0)),
            scratch_shapes=[pltpu.VMEM((bm, d), F32)]),
        out_shape=jax.ShapeDtypeStruct((n_slots, d), BF16),
        compiler_params=_params("arbitrary", "arbitrary"),
        name="moe_ffn",
    )(block_expert, block_rows, xs, w1, w3, w2)


def _combine_kernel(til_ref, blk_ref, flg_ref, tok_ref, y_ref, res_ref, o_ref, *, bs):
    w = pl.program_id(0)
    flags = flg_ref[w]

    @pl.when((flags & 1) != 0)
    def _():
        o_ref[...] = res_ref[...]

    @pl.when((flags & 4) != 0)
    def _():
        tok = tok_ref[...]
        tt = tok.shape[0]
        sid = (blk_ref[w] * bs + lax.broadcasted_iota(I32, (tt, bs), 1)).astype(F32)
        y = y_ref[...]
        oh1 = jnp.where(sid == tok[:, 0:1], 1.0, 0.0).astype(BF16)
        oh2 = jnp.where(sid == tok[:, 1:2], 1.0, 0.0).astype(BF16)
        o_ref[...] += tok[:, 2:3] * _dot(oh1, y) + tok[:, 3:4] * _dot(oh2, y)


def moe_combine(y, tok_info, residual, wl_til, wl_blk, wl_flg):
    m, d = residual.shape
    bs = tt = MOE_TILE
    n_items = wl_til.shape[0]
    return pl.pallas_call(
        functools.partial(_combine_kernel, bs=bs),
        grid_spec=pltpu.PrefetchScalarGridSpec(
            num_scalar_prefetch=3, grid=(n_items,),
            in_specs=[pl.BlockSpec((tt, SUBLANES), lambda w, t, b, f: (t[w], 0)),
                      pl.BlockSpec((bs, d), lambda w, t, b, f: (b[w], 0)),
                      pl.BlockSpec((tt, d), lambda w, t, b, f: (t[w], 0))],
            out_specs=pl.BlockSpec((tt, d), lambda w, t, b, f: (t[w], 0))),
        out_shape=jax.ShapeDtypeStruct((m, d), F32),
        compiler_params=_params("arbitrary"),
        name="moe_combine",
    )(wl_til, wl_blk, wl_flg, tok_info, y, residual)


def _work_list(overlap, n_items):
    n_major, n_minor = overlap.shape
    has = jnp.any(overlap, axis=1)
    marked = jnp.concatenate([overlap[:, :1] | ~has[:, None], overlap[:, 1:]], axis=1)
    maj, mino = jnp.nonzero(marked, size=n_items, fill_value=-1)
    valid = maj >= 0
    real = valid & overlap[jnp.maximum(maj, 0), jnp.maximum(mino, 0)]
    prev = jnp.concatenate([jnp.full((1,), -2, maj.dtype), maj[:-1]])
    nxt = jnp.concatenate([maj[1:], jnp.full((1,), -2, maj.dtype)])
    flags = (valid & (maj != prev)) * 1 + (valid & (maj != nxt)) * 2 + real * 4
    last_maj = jnp.max(jnp.where(valid, maj, 0))
    last_min = jnp.sum(jnp.where(valid & (maj == last_maj) & (maj != nxt), mino, 0))
    maj = jnp.where(valid, maj, last_maj)
    mino = jnp.where(valid, mino, last_min)
    return maj.astype(I32), mino.astype(I32), flags.astype(I32)


def moe_layer(h, norm_gain, w_router, w1, w3, w2):
    m, d = h.shape
    E = N_EXPERTS
    tile = min(MOE_TILE, m)
    n_tiles = m // tile
    ffn_rows = MOE_FFN_ROWS
    xn = rmsnorm(h, norm_gain, BF16)
    info, tile_cnt, total = moe_router(h, norm_gain, w_router)
    before = tile_cnt[:, :, 0]
    counts = total[:, 0].astype(I32)
    padded = (counts + ffn_rows - 1) // ffn_rows * ffn_rows
    ends = jnp.cumsum(padded)
    starts = ends - padded
    n_slots = (m * TOP_K // ffn_rows + E) * ffn_rows
    n_ffn_blocks = n_slots // ffn_rows
    n_dblocks = n_slots // tile
    slots = moe_slots(info, starts.astype(I32))
    db_start = jnp.arange(n_dblocks, dtype=I32) * tile
    db_e = jnp.minimum(jnp.sum(db_start[:, None] >= ends[None, :], axis=1), E - 1)
    r0 = db_start - starts[db_e]
    r1 = jnp.minimum(r0 + tile, counts[db_e])
    after = jnp.concatenate([before[1:], counts[None, :].astype(F32)], axis=0).astype(I32)
    bef = before.astype(I32)
    ov = (bef[:, db_e].T < r1[:, None]) & (after[:, db_e].T > r0[:, None]) & (db_start < ends[-1])[:, None]
    n_items = n_dblocks + E * n_tiles
    d_blk, d_til, d_flg = _work_list(ov, n_items)
    c_til, c_blk, c_flg = _work_list(ov.T, n_items)
    xs = moe_dispatch(xn, slots, d_blk, d_til, d_flg, n_slots)
    fb_start = jnp.arange(n_ffn_blocks, dtype=I32) * ffn_rows
    block_expert = jnp.minimum(jnp.sum(fb_start[:, None] >= ends[None, :], axis=1), E - 1).astype(I32)
    filled = counts[block_expert] - (fb_start - starts[block_expert])
    block_rows = jnp.where(fb_start < ends[-1], (jnp.clip(filled, 0, ffn_rows) + tile - 1) // tile * tile, 0).astype(I32)
    ys = moe_ffn(xs, w1, w3, w2, block_expert, block_rows)
    return moe_combine(ys, slots.T, h, c_til, c_blk, c_flg)


def even_mixer_layer(h, posf, p, bsz, seq):
    m = bsz * seq
    hn = rmsnorm(h, p["norm_mix"], BF16)
    w_in = p["w_in"]
    c = np.cumsum([0, D_SSM, D_CONV, SSM_HEADS, D_NSA] + [D_NSA_KV] * 6 + [3 * NSA_HEADS])
    z = matmul([(hn, w_in[:, c[0]:c[1]].astype(BF16))], out_dtype=BF16)
    xbc = matmul([(hn, w_in[:, c[1]:c[2]].astype(BF16))], out_dtype=BF16)
    q = matmul([(hn, w_in[:, c[3]:c[4]].astype(BF16))], out_dtype=BF16)
    kv6 = matmul([(hn, w_in[:, c[4]:c[10]].astype(BF16))], out_dtype=BF16)
    w_small = jnp.concatenate([w_in[:, c[2]:c[3]], w_in[:, c[10]:c[11]]], axis=1)
    w_small = jnp.pad(w_small, ((0, 0), (0, LANES - w_small.shape[1]))).astype(BF16)
    small = matmul([(hn, w_small)])
    y_ssm = ssd_group(z, xbc, small, p["conv_w"], p["conv_b"], p["dt_bias"], p["a_log"], p["d_skip"],
                      p["ssm_norm"], bsz, seq)
    qn, ksn, kwn = rope_norm(posf, q, kv6, p["q_gain"], p["ks_gain"], p["kw_gain"])
    nch = seq // CMP_STRIDE

    def chunks(col0):
        u = kv6[:, col0:col0 + D_NSA_KV].reshape(bsz, nch, CMP_STRIDE, NSA_KV_HEADS, NSA_HEAD_DIM)
        return u.transpose(0, 3, 1, 2, 4).reshape(bsz, NSA_KV_HEADS, nch, CMP_STRIDE * NSA_HEAD_DIM)

    kc, vc = compress(chunks(0), chunks(D_NSA_KV), p["pe_k"], p["pe_v"], p["wk1"], p["wk2"], p["wv1"], p["wv2"],
                      p["kc_gain"])
    o_cmp, selt = cmp_select(qn, kc, vc, bsz, seq)
    tk = min(ATT_TK, seq)

    def v_tiles(col0):
        u = kv6[:, col0:col0 + D_NSA_KV].astype(BF16).reshape(bsz, seq // tk, tk, NSA_KV_HEADS, NSA_HEAD_DIM)
        return u.transpose(0, 3, 1, 4, 2)

    o_sel = masked_attention(qn, ksn, 0, v_tiles(3 * D_NSA_KV), bsz, seq, "sel", selt)
    o_win = masked_attention(qn, kwn, 0, v_tiles(5 * D_NSA_KV), bsz, seq, "win")
    y_nsa = nsa_combine(small, o_cmp, o_sel, o_win)
    w_out = p["w_out"].astype(BF16)
    return matmul([(y_ssm, w_out[:D_SSM]), (y_nsa, w_out[D_SSM:])], residual=h)


def rwkv_layer(h, p, bsz, seq):
    m, d = h.shape
    mixes = rwkv_mix(h, p["norm_mix"], p["mu"], seq)
    bf = lambda w: w.astype(BF16)

    def lora_pad(w_a, w_b):
        r = w_a.shape[1]
        rp = -(-r // LANES) * LANES
        return bf(jnp.pad(w_a, ((0, 0), (0, rp - r)))), bf(jnp.pad(w_b, ((0, rp - r), (0, 0))))

    perm = _wkv_column_order()
    r = matmul([(mixes[0], bf(p["w_r"][:, perm]))])
    k = matmul([(mixes[2], bf(p["w_k"][:, perm]))])
    v = matmul([(mixes[3], bf(p["w_v"][:, perm]))])
    w1, w2 = lora_pad(p["w1"], p["w2"][:, perm])
    w_raw = matmul([(matmul([(mixes[1], w1)], act="tanh", out_dtype=BF16), w2)], bias=p["w0"][perm])
    a1, a2 = lora_pad(p["a1"], p["a2"][:, perm])
    a_sig = matmul([(matmul([(mixes[4], a1)], out_dtype=BF16), a2)], bias=p["a0"][perm], act="sigmoid")
    g1, g2 = lora_pad(p["g1"], p["g2"][:, perm])
    g = matmul([(matmul([(mixes[5], g1)], act="sigmoid", out_dtype=BF16), g2)])
    o = wkv_fused(r, k, v, w_raw, a_sig, g, p["k_k"], p["k_a"], p["r_k"].reshape(-1), p["ln_w"], p["ln_b"], bsz, seq)
    return matmul([(o, bf(p["w_o"][perm, :]))], residual=h)


def kernel(x, mem, positions, norm_mix, norm_xattn, norm_mem, norm_ffn, xattn_wq, xattn_wkv, xattn_wo, xattn_q_gain, xattn_k_gain, ev_w_in, ev_conv_w, ev_conv_b, ev_dt_bias, ev_a_log, ev_d_skip, ev_ssm_norm, ev_q_gain, ev_kc_gain, ev_ks_gain, ev_kw_gain, ev_pe_k, ev_pe_v, ev_cmp_wk1, ev_cmp_wk2, ev_cmp_wv1, ev_cmp_wv2, ev_w_out, ev_ffn_w1, ev_ffn_w3, ev_ffn_w2, od_mu, od_w_r, od_w_k, od_w_v, od_w_o, od_w0, od_w1, od_w2, od_a0, od_a1, od_a2, od_g1, od_g2, od_k_k, od_k_a, od_r_k, od_ln_w, od_ln_b, od_router, od_moe_w1, od_moe_w3, od_moe_w2):
    bsz, seq, d = x.shape
    m = bsz * seq
    depth = norm_mix.shape[0]
    h = x.reshape(m, d)
    mem2d = mem.reshape(-1, d)
    posf = positions.astype(F32).reshape(m, 1)
    for layer in range(depth):
        i = layer // 2
        if layer % 2 == 0:
            p = dict(norm_mix=norm_mix[layer], w_in=ev_w_in[i], conv_w=ev_conv_w[i], conv_b=ev_conv_b[i],
                     dt_bias=ev_dt_bias[i], a_log=ev_a_log[i], d_skip=ev_d_skip[i], ssm_norm=ev_ssm_norm[i],
                     q_gain=ev_q_gain[i], kc_gain=ev_kc_gain[i], ks_gain=ev_ks_gain[i], kw_gain=ev_kw_gain[i],
                     pe_k=ev_pe_k[i], pe_v=ev_pe_v[i], wk1=ev_cmp_wk1[i], wk2=ev_cmp_wk2[i], wv1=ev_cmp_wv1[i],
                     wv2=ev_cmp_wv2[i], w_out=ev_w_out[i])
            h = even_mixer_layer(h, posf, p, bsz, seq)
        else:
            p = dict(norm_mix=norm_mix[layer], mu=od_mu[i], w_r=od_w_r[i], w_k=od_w_k[i], w_v=od_w_v[i], w_o=od_w_o[i],
                     w0=od_w0[i], w1=od_w1[i], w2=od_w2[i], a0=od_a0[i], a1=od_a1[i], a2=od_a2[i], g1=od_g1[i],
                     g2=od_g2[i], k_k=od_k_k[i], k_a=od_k_a[i], r_k=od_r_k[i], ln_w=od_ln_w[i], ln_b=od_ln_b[i])
            h = rwkv_layer(h, p, bsz, seq)
        h = memory_xattn(h, mem2d, norm_xattn[layer], norm_mem[layer], xattn_wq[layer], xattn_wkv[layer],
                         xattn_wo[layer], xattn_q_gain[layer], xattn_k_gain[layer], bsz, seq)
        if layer % 2 == 0:
            hn = rmsnorm(h, norm_ffn[layer], BF16)
            h = swiglu_ffn(hn, ev_ffn_w1[i].astype(BF16), ev_ffn_w3[i].astype(BF16), ev_ffn_w2[i].astype(BF16), h)
        else:
            h = moe_layer(h, norm_ffn[layer], od_router[i], od_moe_w1[i], od_moe_w3[i], od_moe_w2[i])
    return h.reshape(bsz, seq, d)
```

```python
import functools
import math

import numpy as np
import jax
import jax.numpy as jnp
from jax import lax
from jax.experimental import pallas as pl
from jax.experimental.pallas import tpu as pltpu

F32 = jnp.float32
BF16 = jnp.bfloat16
I32 = jnp.int32

D_MODEL = 2048
NORM_EPS = 1e-6
NEG_BIG = -1e30

SSM_HEADS = 32
SSM_HEAD_DIM = 64
D_SSM = SSM_HEADS * SSM_HEAD_DIM
SSM_GROUPS = 4
D_STATE = 128
CONV_WIDTH = 4
SSD_CHUNK = 128
D_CONV = D_SSM + 2 * SSM_GROUPS * D_STATE
D_SSM_GROUP = D_SSM // SSM_GROUPS

NSA_HEADS = 16
NSA_KV_HEADS = 4
NSA_HEAD_DIM = 128
NSA_Q_PER_KV = NSA_HEADS // NSA_KV_HEADS
D_NSA = NSA_HEADS * NSA_HEAD_DIM
D_NSA_KV = NSA_KV_HEADS * NSA_HEAD_DIM
CMP_BLOCK = 32
CMP_STRIDE = 16
CMP_HIDDEN = 256
SEL_BLOCK = 64
N_SELECT = 16
WINDOW = 512
ROPE_DIM = NSA_HEAD_DIM // 4
ROPE_THETA = 500000.0

RWKV_HEAD_DIM = 64
RWKV_HEADS = D_MODEL // RWKV_HEAD_DIM
RWKV_GN_EPS = 1e-5 * RWKV_HEAD_DIM

XATTN_HEADS = 4
XATTN_HEAD_DIM = 128
D_XATTN = XATTN_HEADS * XATTN_HEAD_DIM

N_EXPERTS = 8
TOP_K = 2

LANES = 128
SUBLANES = 8
VMEM_LIMIT_BYTES = 56 * 1024 * 1024

ROW_TILE = 512
MM_TM = 2048
MM_TN = 512
FFN_TM = 512
FFN_TF = 512
FFN_SPLIT = 256
MOE_FFN_ROWS = 1024
MOE_TF = 256
COMBINE_TM = 256
MOE_TILE = 256
ATT_TQ = 256
ATT_TK = 512
LOG2E = 1.4426950408889634
CMP_TQ = 128
WKV_T = 64
WKV_LOOKAHEAD = 8


def _params(*sem):
    return pltpu.CompilerParams(dimension_semantics=sem, vmem_limit_bytes=VMEM_LIMIT_BYTES)


def _sigmoid(x):
    return 1.0 / (1.0 + jnp.exp(-x))


def _silu(x):
    return x * _sigmoid(x)


def _softplus(x):
    return jnp.maximum(x, 0.0) + jnp.log(1.0 + jnp.exp(-jnp.abs(x)))


def _split3(x):
    h = x.astype(BF16)
    r = x - h.astype(F32)
    m = r.astype(BF16)
    l = (r - m.astype(F32)).astype(BF16)
    return h, m, l


def _dot(a, b):
    return jnp.dot(a, b, preferred_element_type=F32)


def _dot_nt(a, b):
    return lax.dot_general(a, b, (((1,), (1,)), ((), ())), preferred_element_type=F32)


def _dot_f32_sel(x, e):
    h, m, l = _split3(x)
    return _dot(h, e) + _dot(m, e) + _dot(l, e)


def _sel_dot_f32(e, x):
    h, m, l = _split3(x)
    return _dot(e, h) + _dot(e, m) + _dot(e, l)


def _rms(x, eps=NORM_EPS):
    return x * lax.rsqrt(jnp.mean(x * x, axis=-1, keepdims=True) + eps)


def _rmsnorm_kernel(x_ref, g_ref, o_ref):
    o_ref[...] = (_rms(x_ref[...]) * g_ref[...]).astype(o_ref.dtype)


def rmsnorm(x, gain, out_dtype):
    m, d = x.shape
    tm = min(ROW_TILE, m)
    return pl.pallas_call(
        _rmsnorm_kernel,
        grid=(m // tm,),
        in_specs=[pl.BlockSpec((tm, d), lambda i: (i, 0)), pl.BlockSpec((1, d), lambda i: (0, 0))],
        out_specs=pl.BlockSpec((tm, d), lambda i: (i, 0)),
        out_shape=jax.ShapeDtypeStruct((m, d), out_dtype),
        compiler_params=_params("parallel"),
        name="rmsnorm",
    )(x, gain.reshape(1, d))


def _mm_kernel(*refs, n_pairs, has_bias, has_res, act):
    o_ref = refs[-1]
    acc = None
    for p in range(n_pairs):
        a = refs[2 * p][...].astype(BF16)
        d = _dot(a, refs[2 * p + 1][...])
        acc = d if acc is None else acc + d
    k = 2 * n_pairs
    if has_bias:
        acc = acc + refs[k][...]
        k += 1
    if act == "tanh":
        acc = jnp.tanh(acc)
    elif act == "sigmoid":
        acc = _sigmoid(acc)
    if has_res:
        acc = acc + refs[k][...]
    o_ref[...] = acc.astype(o_ref.dtype)


def matmul(pairs, bias=None, residual=None, act=None, out_dtype=F32):
    m = pairs[0][0].shape[0]
    n = pairs[0][1].shape[1]
    k_total = sum(a.shape[1] for a, _ in pairs)
    tm = min(MM_TM if k_total <= D_MODEL else MM_TM // 2, m)
    tn = MM_TN if n % MM_TN == 0 else (256 if n % 256 == 0 else LANES)
    tn = min(tn, n)
    in_specs, args = [], []
    for a, w in pairs:
        kk = a.shape[1]
        in_specs += [pl.BlockSpec((tm, kk), lambda i, j: (i, 0)), pl.BlockSpec((kk, tn), lambda i, j: (0, j))]
        args += [a, w]
    if bias is not None:
        in_specs.append(pl.BlockSpec((1, tn), lambda i, j: (0, j)))
        args.append(bias.reshape(1, n).astype(F32))
    if residual is not None:
        in_specs.append(pl.BlockSpec((tm, tn), lambda i, j: (i, j)))
        args.append(residual)
    kern = functools.partial(_mm_kernel, n_pairs=len(pairs), has_bias=bias is not None,
                             has_res=residual is not None, act=act)
    return pl.pallas_call(
        kern,
        grid=(m // tm, n // tn),
        in_specs=in_specs,
        out_specs=pl.BlockSpec((tm, tn), lambda i, j: (i, j)),
        out_shape=jax.ShapeDtypeStruct((m, n), out_dtype),
        compiler_params=_params("parallel", "parallel"),
        name="matmul",
    )(*args)


def _ffn_kernel(x_ref, w1_ref, w3_ref, w2_ref, res_ref, o_ref):
    f = pl.program_id(1)

    @pl.when(f == 0)
    def _():
        o_ref[...] = res_ref[...]

    x = x_ref[...]
    tf = w1_ref.shape[1]
    cols = [slice(c, c + FFN_SPLIT) for c in range(0, tf, FFN_SPLIT)]
    gate_up = [(_dot(x, w1_ref[:, c]), _dot(x, w3_ref[:, c])) for c in cols]
    hs = [(_silu(g) * u).astype(BF16) for g, u in gate_up]
    acc = _dot(hs[0], w2_ref[cols[0], :])
    for h, c in zip(hs[1:], cols[1:]):
        acc = acc + _dot(h, w2_ref[c, :])
    o_ref[...] += acc


def swiglu_ffn(x, w1, w3, w2, residual):
    m, d = x.shape
    ff = w1.shape[1]
    tm = min(FFN_TM, m)
    tf = FFN_TF
    return pl.pallas_call(
        _ffn_kernel,
        grid=(m // tm, ff // tf),
        in_specs=[pl.BlockSpec((tm, d), lambda i, f: (i, 0)),
                  pl.BlockSpec((d, tf), lambda i, f: (0, f)),
                  pl.BlockSpec((d, tf), lambda i, f: (0, f)),
                  pl.BlockSpec((tf, d), lambda i, f: (f, 0)),
                  pl.BlockSpec((tm, d), lambda i, f: (i, 0))],
        out_specs=pl.BlockSpec((tm, d), lambda i, f: (i, 0)),
        out_shape=jax.ShapeDtypeStruct((m, d), F32),
        compiler_params=_params("parallel", "arbitrary"),
        name="swiglu_ffn",
    )(x, w1, w3, w2, residual)


def _ssd_kernel(z_ref, xbc_ref, dt_ref, cw_ref, cb_ref, dtb_ref, alog_ref, dskip_ref, nw_ref, eh_ref,
                o_ref, ext_ref, st_ref):
    L = SSD_CHUNK
    G = SSM_GROUPS
    GW = D_SSM_GROUP
    c = pl.program_id(1)

    @pl.when(c == 0)
    def _():
        ext_ref[L:L + SUBLANES, :] = jnp.zeros((SUBLANES, D_CONV), F32)
        st_ref[...] = jnp.zeros_like(st_ref)

    ext_ref[0:SUBLANES, :] = ext_ref[L:L + SUBLANES, :]
    ext_ref[SUBLANES:L + SUBLANES, :] = xbc_ref[...].astype(F32)
    acc = jnp.zeros((L, D_CONV), F32) + cb_ref[...]
    for k in range(CONV_WIDTH):
        off = SUBLANES - (CONV_WIDTH - 1) + k
        acc = acc + cw_ref[k:k + 1, :] * ext_ref[off:off + L, :]
    xc = _silu(acc)

    dt = _softplus(dt_ref[...] + dtb_ref[...])
    a = -jnp.exp(alog_ref[...])
    adt = dt * a
    row = lax.broadcasted_iota(I32, (L, L), 0)
    col = lax.broadcasted_iota(I32, (L, L), 1)
    causal = row >= col
    tril = jnp.where(causal, 1.0, 0.0).astype(BF16)
    a_cum = _sel_dot_f32(tril, adt)
    a_cum_t = a_cum.T
    eh = eh_ref[...]
    acx = _dot_f32_sel(a_cum, eh)
    dtx = _dot_f32_sel(dt, eh)
    a_end = acx[L - 1:L, :]
    xs = xc[:, :D_SSM]
    xdt = xs * dtx
    xdte = (xdt * jnp.exp(a_end - acx)).astype(BF16)
    from_start = jnp.exp(acx)
    lane = lax.broadcasted_iota(I32, (L, LANES), 1)
    lo_half = lane < SSM_HEAD_DIM
    z = z_ref[...].astype(F32)

    for g in range(G):
        bm = xc[:, D_SSM + g * D_STATE:D_SSM + (g + 1) * D_STATE]
        cm = xc[:, D_SSM + G * D_STATE + g * D_STATE:D_SSM + G * D_STATE + (g + 1) * D_STATE]
        bm_b = bm.astype(BF16)
        cm_b = cm.astype(BF16)
        cb = _dot_nt(cm_b, bm_b)
        st = st_ref[g]
        y_off = _dot(cm_b, st.astype(BF16)) * from_start[:, g * GW:(g + 1) * GW]
        pieces = []
        for pp in range(GW // LANES):
            h0 = g * (GW // SSM_HEAD_DIM) + 2 * pp
            ms = []
            for hh in (h0, h0 + 1):
                diff = a_cum[:, hh:hh + 1] - a_cum_t[hh:hh + 1, :]
                dec = jnp.exp(jnp.where(causal, diff, NEG_BIG))
                ms.append((cb * dec).astype(BF16))
            lhs = jnp.concatenate(ms, axis=1)
            c0 = g * GW + pp * LANES
            xp = xdt[:, c0:c0 + LANES]
            rhs = jnp.concatenate([jnp.where(lo_half, xp, 0.0), jnp.where(lo_half, 0.0, xp)], axis=0)
            pieces.append(_dot(lhs, rhs.astype(BF16)))
        y = jnp.concatenate(pieces, axis=1) + y_off
        y = y + xs[:, g * GW:(g + 1) * GW] * dskip_ref[:, g * GW:(g + 1) * GW]
        y = y * _silu(z[:, g * GW:(g + 1) * GW])
        y = _rms(y) * nw_ref[:, g * GW:(g + 1) * GW]
        o_ref[:, g * GW:(g + 1) * GW] = y.astype(o_ref.dtype)
        s_new = _dot(bm.T.astype(BF16), xdte[:, g * GW:(g + 1) * GW])
        st_ref[g] = st * jnp.exp(a_end[:, g * GW:(g + 1) * GW]) + s_new


def ssd_group(z, xbc, small, conv_w, conv_b, dt_bias, a_log, d_skip, norm_w, bsz, seq):
    m = bsz * seq
    L = SSD_CHUNK
    nc = seq // L
    pad = LANES - SSM_HEADS
    eh = np.zeros((LANES, D_SSM), np.float32)
    for h in range(SSM_HEADS):
        eh[h, h * SSM_HEAD_DIM:(h + 1) * SSM_HEAD_DIM] = 1.0
    row_spec = lambda w: pl.BlockSpec((L, w), lambda b, c: (b * nc + c, 0))
    const = lambda r, w: pl.BlockSpec((r, w), lambda b, c: (0, 0))
    return pl.pallas_call(
        _ssd_kernel,
        grid=(bsz, nc),
        in_specs=[row_spec(D_SSM), row_spec(D_CONV), row_spec(LANES),
                  const(CONV_WIDTH, D_CONV), const(1, D_CONV), const(1, LANES), const(1, LANES),
                  const(1, D_SSM), const(1, D_SSM), const(LANES, D_SSM)],
        out_specs=row_spec(D_SSM),
        out_shape=jax.ShapeDtypeStruct((m, D_SSM), BF16),
        scratch_shapes=[pltpu.VMEM((L + SUBLANES, D_CONV), F32),
                        pltpu.VMEM((SSM_GROUPS, D_STATE, D_SSM_GROUP), F32)],
        compiler_params=_params("parallel", "arbitrary"),
        name="ssd_group",
    )(z, xbc, small, conv_w.T, conv_b.reshape(1, D_CONV),
      jnp.pad(dt_bias, (0, pad)).reshape(1, LANES), jnp.pad(a_log, (0, pad)).reshape(1, LANES),
      jnp.repeat(d_skip, SSM_HEAD_DIM).reshape(1, D_SSM), norm_w.reshape(1, D_SSM), jnp.asarray(eh, BF16))


def _rope_norm_kernel(pos_ref, invf_ref, q_ref, kv_ref, qg_ref, ksg_ref, kwg_ref, qo_ref, kso_ref, kwo_ref):
    ang = pos_ref[...] * invf_ref[...]
    cos = jnp.cos(ang)
    sin = jnp.sin(ang)
    lane = lax.broadcasted_iota(I32, ang.shape, 1)
    first = lane < ROPE_DIM // 2
    sin_signed = jnp.where(first, -sin, sin)

    def head(x, gain):
        y = _rms(x.astype(F32)) * gain
        partner = jnp.where(first, pltpu.roll(y, LANES - ROPE_DIM // 2, axis=1), pltpu.roll(y, ROPE_DIM // 2, axis=1))
        return y * cos + partner * sin_signed

    hd = NSA_HEAD_DIM
    scale = hd ** -0.5 * LOG2E
    for h in range(NSA_HEADS):
        qo_ref[:, h * hd:(h + 1) * hd] = (head(q_ref[:, h * hd:(h + 1) * hd], qg_ref[...]) * scale).astype(qo_ref.dtype)
    for h in range(NSA_KV_HEADS):
        ks = kv_ref[:, 2 * D_NSA_KV + h * hd:2 * D_NSA_KV + (h + 1) * hd]
        kw = kv_ref[:, 4 * D_NSA_KV + h * hd:4 * D_NSA_KV + (h + 1) * hd]
        kso_ref[:, h * hd:(h + 1) * hd] = head(ks, ksg_ref[...]).astype(kso_ref.dtype)
        kwo_ref[:, h * hd:(h + 1) * hd] = head(kw, kwg_ref[...]).astype(kwo_ref.dtype)


def rope_norm(posf, q, kv6, q_gain, ks_gain, kw_gain):
    m = q.shape[0]
    tm = min(ROW_TILE, m)
    half = ROPE_DIM // 2
    inv = np.exp(-math.log(ROPE_THETA) * np.arange(0, ROPE_DIM, 2, dtype=np.float32) / ROPE_DIM).astype(np.float32)
    invf = np.zeros((1, LANES), np.float32)
    invf[0, :half] = inv
    invf[0, half:ROPE_DIM] = inv
    row = lambda w: pl.BlockSpec((tm, w), lambda i: (i, 0))
    const = pl.BlockSpec((1, LANES), lambda i: (0, 0))
    return pl.pallas_call(
        _rope_norm_kernel,
        grid=(m // tm,),
        in_specs=[row(1), const, row(D_NSA), row(6 * D_NSA_KV), const, const, const],
        out_specs=[row(D_NSA), row(D_NSA_KV), row(D_NSA_KV)],
        out_shape=[jax.ShapeDtypeStruct((m, D_NSA), BF16), jax.ShapeDtypeStruct((m, D_NSA_KV), BF16),
                   jax.ShapeDtypeStruct((m, D_NSA_KV), BF16)],
        compiler_params=_params("parallel"),
        name="rope_norm",
    )(posf, jnp.asarray(invf), q, kv6, q_gain.reshape(1, LANES), ks_gain.reshape(1, LANES), kw_gain.reshape(1, LANES))


def _compress_kernel(uk_ref, uv_ref, pek_ref, pev_ref, wk1_ref, wk2_ref, wv1_ref, wv2_ref, g_ref, kc_ref, vc_ref):
    def mlp(u, pe_ref, w1_ref, w2_ref):
        u = u.astype(F32)
        n = u.shape[0]
        h1 = _dot((u + pe_ref[0:1, :]).astype(BF16), w1_ref[0])
        h2 = _dot((u + pe_ref[1:2, :]).astype(BF16), w1_ref[1])
        pre = h1 + pltpu.roll(h2, n - 1, axis=0)
        return _dot(_silu(pre).astype(BF16), w2_ref[...])

    kc = mlp(uk_ref[0, 0], pek_ref, wk1_ref, wk2_ref)
    kc_ref[0, 0] = _rms(kc) * g_ref[...]
    vc_ref[0, 0] = mlp(uv_ref[0, 0], pev_ref, wv1_ref, wv2_ref)


def compress(uk, uv, pe_k, pe_v, wk1, wk2, wv1, wv2, kc_gain):
    bsz, nkv, nch, width = uk.shape
    half = CMP_STRIDE * NSA_HEAD_DIM
    u_spec = pl.BlockSpec((1, 1, nch, width), lambda b, g: (b, g, 0, 0))
    o_spec = pl.BlockSpec((1, 1, nch, NSA_HEAD_DIM), lambda b, g: (b, g, 0, 0))
    c2 = lambda s: pl.BlockSpec(s, lambda b, g: (0, 0))
    c3 = lambda s: pl.BlockSpec(s, lambda b, g: (0, 0, 0))
    o_shape = jax.ShapeDtypeStruct((bsz, nkv, nch, NSA_HEAD_DIM), F32)
    return pl.pallas_call(
        _compress_kernel,
        grid=(bsz, nkv),
        in_specs=[u_spec, u_spec, c2((2, half)), c2((2, half)),
                  c3((2, half, CMP_HIDDEN)), c2((CMP_HIDDEN, NSA_HEAD_DIM)),
                  c3((2, half, CMP_HIDDEN)), c2((CMP_HIDDEN, NSA_HEAD_DIM)), c2((1, NSA_HEAD_DIM))],
        out_specs=[o_spec, o_spec],
        out_shape=[o_shape, o_shape],
        compiler_params=_params("parallel", "parallel"),
        name="nsa_compress",
    )(uk, uv, pe_k.reshape(2, half), pe_v.reshape(2, half),
      wk1.reshape(2, half, CMP_HIDDEN).astype(BF16), wk2.astype(BF16),
      wv1.reshape(2, half, CMP_HIDDEN).astype(BF16), wv2.astype(BF16), kc_gain.reshape(1, NSA_HEAD_DIM))


def _stack_heads(q):
    return jnp.concatenate([q[:, r * NSA_HEAD_DIM:(r + 1) * NSA_HEAD_DIM] for r in range(NSA_Q_PER_KV)], axis=0)


def _unstack_heads(o, tq):
    return jnp.concatenate([o[r * tq:(r + 1) * tq, :] for r in range(NSA_Q_PER_KV)], axis=1)


def _cmp_select_kernel(q_ref, kc_ref, vc_ref, ovt_ref, o_ref, sel_ref, *, tq, n_blk):
    R = NSA_Q_PER_KV
    q0 = pl.program_id(2) * tq
    q4 = _stack_heads(q_ref[...])
    kc = kc_ref[0, 0].astype(BF16)
    ncmp = kc.shape[0]
    s = _dot_nt(q4, kc)
    t_row = q0 + lax.broadcasted_iota(I32, (tq, ncmp), 0)
    c_col = lax.broadcasted_iota(I32, (tq, ncmp), 1)
    visf = jnp.where((c_col * CMP_STRIDE + CMP_BLOCK - 1) <= t_row, 1.0, 0.0)
    vis = jnp.concatenate([visf] * R, axis=0) > 0.5
    s = jnp.where(vis, s, NEG_BIG)
    mx = jnp.max(s, axis=-1, keepdims=True)
    e = jnp.where(vis, jnp.exp2(s - mx), 0.0)
    den = jnp.sum(e, axis=-1, keepdims=True)
    p = e / jnp.maximum(den, 1e-30)
    o = _dot(p.astype(BF16), vc_ref[0, 0].astype(BF16))
    o_ref[...] = _unstack_heads(o, tq)

    psum = p[0:tq]
    for r in range(1, R):
        psum = psum + p[r * tq:(r + 1) * tq]
    h, m, l = _split3(psum)
    ovt = ovt_ref[...]
    imp = _dot_nt(ovt, h) + _dot_nt(ovt, m) + _dot_nt(ovt, l)
    jdx = lax.broadcasted_iota(I32, (n_blk, tq), 0)
    t_lane = q0 + lax.broadcasted_iota(I32, (n_blk, tq), 1)
    cur = t_lane // SEL_BLOCK
    causal = jdx <= cur
    forced = ((jdx == 0) | (jdx >= cur - 1)) & causal
    score = jnp.where(forced, jnp.inf, jnp.where(causal, imp, -jnp.inf))
    rank = jnp.zeros((n_blk, tq), F32)
    for jp in range(n_blk):
        rowv = score[jp:jp + 1, :]
        beats = (rowv > score) | ((rowv == score) & (jdx > jp))
        rank = rank + jnp.where(beats, 1.0, 0.0)
    sel = jnp.where((rank < float(min(N_SELECT, n_blk))) & causal, 1.0, 0.0)
    if n_blk < LANES:
        sel = jnp.concatenate([sel, jnp.zeros((LANES - n_blk, tq), F32)], axis=0)
    sel_ref[0, 0] = sel.astype(sel_ref.dtype)


def cmp_select(qn, kc, vc, bsz, seq):
    m = bsz * seq
    G = NSA_KV_HEADS
    tq = CMP_TQ
    nq = seq // tq
    ncmp = kc.shape[2]
    n_blk = seq // SEL_BLOCK
    assert n_blk <= LANES and tq == LANES
    c0 = np.arange(ncmp)[None, :] * CMP_STRIDE
    s0 = np.arange(n_blk)[:, None] * SEL_BLOCK
    ov = np.clip(np.minimum(c0 + CMP_BLOCK, s0 + SEL_BLOCK) - np.maximum(c0, s0), 0, None) / CMP_STRIDE
    n_cmp_valid = (seq - CMP_BLOCK) // CMP_STRIDE + 1
    ov[:, n_cmp_valid:] = 0.0
    kv_spec = pl.BlockSpec((1, 1, ncmp, NSA_HEAD_DIM), lambda b, g, i: (b, g, 0, 0))
    kern = functools.partial(_cmp_select_kernel, tq=tq, n_blk=n_blk)
    return pl.pallas_call(
        kern,
        grid=(bsz, G, nq),
        in_specs=[pl.BlockSpec((tq, NSA_Q_PER_KV * NSA_HEAD_DIM), lambda b, g, i: (b * nq + i, g)),
                  kv_spec, kv_spec, pl.BlockSpec((n_blk, ncmp), lambda b, g, i: (0, 0))],
        out_specs=[pl.BlockSpec((tq, NSA_Q_PER_KV * NSA_HEAD_DIM), lambda b, g, i: (b * nq + i, g)),
                   pl.BlockSpec((1, 1, LANES, tq), lambda b, g, i: (b, g, 0, i))],
        out_shape=[jax.ShapeDtypeStruct((m, D_NSA), F32), jax.ShapeDtypeStruct((bsz, G, LANES, seq), BF16)],
        compiler_params=_params("parallel", "parallel", "parallel"),
        name="nsa_cmp_select",
    )(qn, kc, vc, jnp.asarray(ov, BF16))


def _masked_attn_kernel(q_ref, k_ref, vt_ref, *rest, mode, tq, tk):
    if mode == "sel":
        selt_ref, o_ref = rest
    else:
        (o_ref,) = rest
    R = NSA_Q_PER_KV
    hd = NSA_HEAD_DIM
    q0 = pl.program_id(2) * tq
    key_row = lax.broadcasted_iota(I32, (tk, tq), 0)
    t_lane = q0 + lax.broadcasted_iota(I32, (tk, tq), 1)
    if mode == "sel":
        key_row_b = lax.broadcasted_iota(I32, (tk, LANES), 0)
        blk_lane = lax.broadcasted_iota(I32, (tk, LANES), 1)
        lo = 0
    else:
        lo = jnp.maximum(q0 - (WINDOW - 1), 0) // tk
    hi = (q0 + tq - 1) // tk + 1
    q = q_ref[...]

    def body(kt, carry):
        k0 = kt * tk
        k_t = k_ref[pl.ds(pl.multiple_of(k0, tk), tk), :]
        v_t = vt_ref[0, 0, kt]
        key = k0 + key_row
        if mode == "sel":
            expand = jnp.where((k0 + key_row_b) // SEL_BLOCK == blk_lane, 1.0, 0.0).astype(BF16)
            picked = _dot(expand, selt_ref[0, 0])
            ok = (picked > 0.5) & (key <= t_lane)
        else:
            ok = (key <= t_lane) & (key > t_lane - WINDOW)
        bias = jnp.where(ok, 0.0, NEG_BIG)
        ss = [_dot_nt(k_t, q[:, r * hd:(r + 1) * hd]) + bias for r in range(R)]
        stats = []
        for r in range(R):
            m_prev, l_prev = carry[3 * r], carry[3 * r + 1]
            m_new = jnp.maximum(m_prev, jnp.max(ss[r], axis=0, keepdims=True))
            p = jnp.exp2(ss[r] - m_new)
            alpha = jnp.exp2(m_prev - m_new)
            stats.append((m_new, alpha * l_prev + jnp.sum(p, axis=0, keepdims=True), alpha, p.astype(BF16)))
        new = []
        for r in range(R):
            m_new, l_new, alpha, p = stats[r]
            new += [m_new, l_new, alpha * carry[3 * r + 2] + _dot(v_t, p)]
        return tuple(new)

    init = (jnp.full((1, tq), NEG_BIG, F32), jnp.zeros((1, tq), F32), jnp.zeros((hd, tq), F32)) * R
    out = lax.fori_loop(lo, hi, body, init)
    o_ref[...] = jnp.concatenate([(out[3 * r + 2] / out[3 * r + 1]).T for r in range(R)], axis=1)


def masked_attention(qn, k, k_col0, vt, bsz, seq, mode, selt=None):
    m = bsz * seq
    G = NSA_KV_HEADS
    tq = min(ATT_TQ, seq)
    tk = vt.shape[-1]
    nq = seq // tq
    hd = NSA_HEAD_DIM
    q_spec = pl.BlockSpec((tq, NSA_Q_PER_KV * hd), lambda b, g, i: (b * nq + i, g))
    in_specs = [q_spec,
                pl.BlockSpec((seq, hd), lambda b, g, i: (b, k_col0 + g)),
                pl.BlockSpec((1, 1, seq // tk, hd, tk), lambda b, g, i: (b, g, 0, 0, 0))]
    args = [qn, k, vt]
    if mode == "sel":
        in_specs.append(pl.BlockSpec((1, 1, LANES, tq), lambda b, g, i: (b, g, 0, i)))
        args.append(selt)
    return pl.pallas_call(
        functools.partial(_masked_attn_kernel, mode=mode, tq=tq, tk=tk),
        grid=(bsz, G, nq),
        in_specs=in_specs,
        out_specs=q_spec,
        out_shape=jax.ShapeDtypeStruct((m, D_NSA), F32),
        compiler_params=_params("parallel", "parallel", "arbitrary"),
        name="nsa_attn_" + mode,
    )(*args)


def _nsa_combine_kernel(g_ref, e_ref, oc_ref, os_ref, ow_ref, o_ref):
    gate = _sigmoid(g_ref[...])
    gx = _dot_f32_sel(gate, e_ref[...])
    y = gx[:, :D_NSA] * oc_ref[...] + gx[:, D_NSA:2 * D_NSA] * os_ref[...] + gx[:, 2 * D_NSA:] * ow_ref[...]
    o_ref[...] = y.astype(o_ref.dtype)


def nsa_combine(small, o_cmp, o_sel, o_win):
    m = small.shape[0]
    tm = min(COMBINE_TM, m)
    e = np.zeros((LANES, 3 * D_NSA), np.float32)
    for h in range(NSA_HEADS):
        for br in range(3):
            e[SSM_HEADS + 3 * h + br, br * D_NSA + h * NSA_HEAD_DIM:br * D_NSA + (h + 1) * NSA_HEAD_DIM] = 1.0
    row = lambda w: pl.BlockSpec((tm, w), lambda i: (i, 0))
    return pl.pallas_call(
        _nsa_combine_kernel,
        grid=(m // tm,),
        in_specs=[row(LANES), pl.BlockSpec((LANES, 3 * D_NSA), lambda i: (0, 0)), row(D_NSA), row(D_NSA), row(D_NSA)],
        out_specs=row(D_NSA),
        out_shape=jax.ShapeDtypeStruct((m, D_NSA), BF16),
        compiler_params=_params("parallel"),
        name="nsa_combine",
    )(small, jnp.asarray(e, BF16), o_cmp, o_sel, o_win)


def _xattn_kernel(q_ref, kv_ref, qg_ref, kg_ref, o_ref):
    hd = XATTN_HEAD_DIM
    scale = hd ** -0.5
    outs = []
    for h in range(XATTN_HEADS):
        q = (_rms(q_ref[:, h * hd:(h + 1) * hd]) * qg_ref[...]).astype(BF16)
        k = (_rms(kv_ref[:, h * hd:(h + 1) * hd]) * kg_ref[...]).astype(BF16)
        v = kv_ref[:, D_XATTN + h * hd:D_XATTN + (h + 1) * hd].astype(BF16)
        s = _dot_nt(q, k) * scale
        e = jnp.exp(s - jnp.max(s, axis=-1, keepdims=True))
        p = e / jnp.sum(e, axis=-1, keepdims=True)
        outs.append(_dot(p.astype(BF16), v))
    o_ref[...] = jnp.concatenate(outs, axis=1).astype(o_ref.dtype)


def xattn_core(q, kv, q_gain, k_gain, bsz, seq):
    m = bsz * seq
    n_mem = kv.shape[0] // bsz
    tm = min(ROW_TILE, seq)
    nt = seq // tm
    return pl.pallas_call(
        _xattn_kernel,
        grid=(bsz, nt),
        in_specs=[pl.BlockSpec((tm, D_XATTN), lambda b, i: (b * nt + i, 0)),
                  pl.BlockSpec((n_mem, 2 * D_XATTN), lambda b, i: (b, 0)),
                  pl.BlockSpec((1, XATTN_HEAD_DIM), lambda b, i: (0, 0)),
                  pl.BlockSpec((1, XATTN_HEAD_DIM), lambda b, i: (0, 0))],
        out_specs=pl.BlockSpec((tm, D_XATTN), lambda b, i: (b * nt + i, 0)),
        out_shape=jax.ShapeDtypeStruct((m, D_XATTN), BF16),
        compiler_params=_params("parallel", "parallel"),
        name="xattn_core",
    )(q, kv, q_gain.reshape(1, XATTN_HEAD_DIM), k_gain.reshape(1, XATTN_HEAD_DIM))


def memory_xattn(h, mem2d, norm_x, norm_m, wq, wkv, wo, q_gain, k_gain, bsz, seq):
    hq = rmsnorm(h, norm_x, BF16)
    q = matmul([(hq, wq.astype(BF16))])
    mem_n = rmsnorm(mem2d, norm_m, BF16)
    kv = matmul([(mem_n, wkv.astype(BF16))])
    o = xattn_core(q, kv, q_gain, k_gain, bsz, seq)
    return matmul([(o, wo.astype(BF16))], residual=h)


def _mix_kernel(h_ref, hp_ref, g_ref, mu_ref, *rest, tm, tiles_per_seq):
    outs, ext_ref = rest[:-1], rest[-1]
    i = pl.program_id(0)
    hn = _rms(h_ref[...]) * g_ref[...]
    hp = _rms(hp_ref[...]) * g_ref[...]
    hp = jnp.where(i % tiles_per_seq == 0, 0.0, hp)
    ext_ref[0:SUBLANES, :] = hp
    ext_ref[SUBLANES:tm + SUBLANES, :] = hn
    xx = ext_ref[SUBLANES - 1:tm + SUBLANES - 1, :] - hn
    for j, o_ref in enumerate(outs):
        o_ref[...] = (hn + xx * mu_ref[j:j + 1, :]).astype(o_ref.dtype)


def rwkv_mix(h, gain, mu, seq):
    m, d = h.shape
    tm = min(ROW_TILE, seq)
    n_mix = mu.shape[0]
    kern = functools.partial(_mix_kernel, tm=tm, tiles_per_seq=seq // tm)
    row = pl.BlockSpec((tm, d), lambda i: (i, 0))
    return pl.pallas_call(
        kern,
        grid=(m // tm,),
        in_specs=[row, pl.BlockSpec((SUBLANES, d), lambda i: (jnp.maximum(i * (tm // SUBLANES) - 1, 0), 0)),
                  pl.BlockSpec((1, d), lambda i: (0, 0)), pl.BlockSpec((n_mix, d), lambda i: (0, 0))],
        out_specs=[row] * n_mix,
        out_shape=[jax.ShapeDtypeStruct((m, d), BF16)] * n_mix,
        scratch_shapes=[pltpu.VMEM((tm + SUBLANES, d), F32)],
        compiler_params=_params("parallel"),
        name="rwkv_mix",
    )(h, h, gain.reshape(1, d), mu)


WKV_LANE_GROUP = LANES // 4
WKV_ROWS_PER_BATCH = D_MODEL // LANES


def _wkv_fused_kernel(r_ref, k_ref, v_ref, w_ref, a_ref, g_ref, kkp_ref, kap_ref, rkp_ref, lnw_ref, lnb_ref,
                      o_ref, s_ref, wd_s, kh_s, kk_s, b_s, r_s, v_s, g_s, *, steps):
    n = RWKV_HEAD_DIM
    nb = LANES // WKV_LANE_GROUP
    halves = WKV_ROWS_PER_BATCH // SUBLANES
    ahead = min(WKV_LOOKAHEAD, steps)

    @pl.when(pl.program_id(0) == 0)
    def _():
        s_ref[...] = jnp.zeros_like(s_ref)

    def lane_group(cnt):
        return lax.broadcasted_iota(I32, (cnt, SUBLANES, LANES), 2) // WKV_LANE_GROUP

    def to_scan(x_ref, t0, cnt):
        grp = lane_group(cnt)
        rows = [x_ref[b, pl.ds(t0, cnt)].astype(F32) for b in range(nb)]
        pieces = []
        for n4 in range(nb):
            for v in range(halves):
                acc = None
                for b in range(nb):
                    x = rows[b][:, SUBLANES * v:SUBLANES * (v + 1), :]
                    shift = (WKV_LANE_GROUP * (b - n4)) % LANES
                    if shift:
                        x = pltpu.roll(x, shift, axis=2)
                    acc = x if acc is None else jnp.where(grp == b, x, acc)
                pieces.append(acc)
        return pieces

    def prepare(t0, cnt):
        rs, ks, vs, ws, as_, gs = (to_scan(x, t0, cnt) for x in (r_ref, k_ref, v_ref, w_ref, a_ref, g_ref))
        kks = [ks[u] * kkp_ref[SUBLANES * u:SUBLANES * (u + 1), :] for u in range(len(ks))]
        ss = kks[0] * kks[0]
        for u in range(1, len(kks)):
            ss = ss + kks[u] * kks[u]
        inv_norm = 1.0 / jnp.maximum(jnp.sqrt(jnp.sum(ss, axis=1, keepdims=True)), 1e-12)
        for u in range(len(ks)):
            rows = slice(SUBLANES * u, SUBLANES * (u + 1))
            at = (pl.ds(t0, cnt), rows, slice(None))
            kk = kks[u] * inv_norm
            wd_s[at] = jnp.exp(-jnp.exp(-_softplus(-ws[u]) - 0.5))
            kk_s[at] = kk
            b_s[at] = kk * as_[u]
            kh_s[at] = ks[u] * (1.0 + (as_[u] - 1.0) * kap_ref[rows, :])
            r_s[at] = rs[u]
            v_s[at] = vs[u]
            g_s[at] = gs[u]

    def finish(t, y):
        inv_n = 1.0 / n
        mean = jnp.sum(y, axis=0, keepdims=True) * inv_n
        yc = y - mean
        var = jnp.sum(yc * yc, axis=0, keepdims=True) * inv_n
        bonus = jnp.sum(r_s[t] * kh_s[t] * rkp_ref[...], axis=0, keepdims=True) * v_s[t]
        o = (yc * lax.rsqrt(var + RWKV_GN_EPS) * lnw_ref[...] + lnb_ref[...] + bonus) * g_s[t]
        grp = lane_group(1)[0]
        for b in range(nb):
            parts = []
            for v in range(halves):
                acc = None
                for n4 in range(nb):
                    u = n4 * halves + v
                    x = o[SUBLANES * u:SUBLANES * (u + 1), :]
                    shift = (WKV_LANE_GROUP * (n4 - b)) % LANES
                    if shift:
                        x = pltpu.roll(x, shift, axis=1)
                    acc = x if acc is None else jnp.where(grp == n4, x, acc)
                parts.append(acc)
            o_ref[b, pl.ds(t, 1)] = jnp.concatenate(parts, axis=0).astype(o_ref.dtype)[None]

    prepare(0, ahead)
    q0 = jnp.zeros((n, LANES), F32)
    for j in range(n):
        q0 = q0 + s_ref[j] * kk_s[0, j:j + 1, :]

    def step(t, q):
        tn = jnp.minimum(t + 1, steps - 1)
        v = v_s[t]
        y = jnp.zeros((n, LANES), F32)
        qn = jnp.zeros((n, LANES), F32)
        for j in range(n):
            s_new = s_ref[j] * wd_s[t, j:j + 1, :] - q * b_s[t, j:j + 1, :] + v * kh_s[t, j:j + 1, :]
            s_ref[j] = s_new
            y = y + s_new * r_s[t, j:j + 1, :]
            qn = qn + s_new * kk_s[tn, j:j + 1, :]
        finish(t, y)
        prepare(jnp.minimum(t + ahead, steps - 1), 1)
        return qn

    lax.fori_loop(0, steps, step, q0)


def _wkv_column_order():
    c, n4, h = np.meshgrid(np.arange(WKV_ROWS_PER_BATCH), np.arange(LANES // WKV_LANE_GROUP), np.arange(RWKV_HEADS),
                           indexing="ij")
    return (h * RWKV_HEAD_DIM + n4 * WKV_ROWS_PER_BATCH + c).reshape(-1)


def _scan_tile(p, bsz):
    return jnp.tile(p.reshape(RWKV_HEADS, RWKV_HEAD_DIM).T, (1, bsz))


def wkv_fused(r, k, v, w_raw, a_sig, g, k_k, k_a, r_k, ln_w, ln_b, bsz, seq):
    assert bsz * RWKV_HEADS == LANES and RWKV_HEADS == WKV_LANE_GROUP
    steps = min(WKV_T, seq)
    view = lambda x: x.reshape(bsz, seq, WKV_ROWS_PER_BATCH, LANES)
    blk = pl.BlockSpec((bsz, steps, WKV_ROWS_PER_BATCH, LANES), lambda i: (0, i, 0, 0))
    par = pl.BlockSpec((RWKV_HEAD_DIM, LANES), lambda i: (0, 0))
    scr = pltpu.VMEM((steps, RWKV_HEAD_DIM, LANES), F32)
    out = pl.pallas_call(
        functools.partial(_wkv_fused_kernel, steps=steps),
        grid=(seq // steps,),
        in_specs=[blk] * 6 + [par] * 5,
        out_specs=blk,
        out_shape=jax.ShapeDtypeStruct((bsz, seq, WKV_ROWS_PER_BATCH, LANES), BF16),
        scratch_shapes=[pltpu.VMEM((RWKV_HEAD_DIM, RWKV_HEAD_DIM, LANES), F32)] + [scr] * 7,
        compiler_params=_params("arbitrary"),
        name="wkv_fused",
    )(view(r), view(k), view(v), view(w_raw), view(a_sig), view(g),
      _scan_tile(k_k, bsz), _scan_tile(k_a, bsz), _scan_tile(r_k, bsz), _scan_tile(ln_w, bsz), _scan_tile(ln_b, bsz))
    return out.reshape(bsz * seq, D_MODEL)


def _router_kernel(h_ref, g_ref, wr_ref, info_ref, tile_cnt_ref, total_ref, carry_ref, *, tt):
    E = N_EXPERTS
    i = pl.program_id(0)

    @pl.when(i == 0)
    def _():
        carry_ref[...] = jnp.zeros_like(carry_ref)

    xh, xm, xl = _split3(_rms(h_ref[...]) * g_ref[...])
    wh, wm, wl = _split3(wr_ref[...])
    logits = (_dot_nt(wh, xh) + _dot_nt(wh, xm) + _dot_nt(wm, xh)
              + _dot_nt(wh, xl) + _dot_nt(wl, xh) + _dot_nt(wm, xm))
    eidx = lax.broadcasted_iota(I32, (E, tt), 0)
    v1 = jnp.max(logits, axis=0, keepdims=True)
    i1 = jnp.min(jnp.where(logits == v1, eidx, E), axis=0, keepdims=True)
    rest = jnp.where(eidx == i1, -jnp.inf, logits)
    v2 = jnp.max(rest, axis=0, keepdims=True)
    i2 = jnp.min(jnp.where(rest == v2, eidx, E), axis=0, keepdims=True)
    e2 = jnp.exp(v2 - v1)
    g1 = 1.0 / (1.0 + e2)
    g2 = e2 / (1.0 + e2)
    hit1 = eidx == i1
    hit2 = eidx == i2
    onehot = jnp.where(hit1 | hit2, 1.0, 0.0)
    rr = lax.broadcasted_iota(I32, (tt, tt), 0)
    cc = lax.broadcasted_iota(I32, (tt, tt), 1)
    before = jnp.where(rr < cc, 1.0, 0.0).astype(BF16)
    carry = carry_ref[...]
    carry_wide = jnp.concatenate([carry] * (tt // LANES), axis=1)
    cum = _dot(onehot.astype(BF16), before) + carry_wide
    rank1 = jnp.sum(jnp.where(hit1, cum, 0.0), axis=0, keepdims=True)
    rank2 = jnp.sum(jnp.where(hit2, cum, 0.0), axis=0, keepdims=True)
    zero = jnp.zeros((1, tt), F32)
    info_ref[...] = jnp.concatenate([i1.astype(F32), i2.astype(F32), rank1, rank2, g1, g2, zero, zero], axis=0)
    tile_cnt_ref[0] = carry
    carry = carry + _dot(onehot.astype(BF16), jnp.ones((tt, LANES), BF16))
    carry_ref[...] = carry
    total_ref[...] = carry


def moe_router(h, norm_gain, w_router):
    m, d = h.shape
    tt = min(MOE_TILE, m)
    nt = m // tt
    return pl.pallas_call(
        functools.partial(_router_kernel, tt=tt),
        grid=(nt,),
        in_specs=[pl.BlockSpec((tt, d), lambda i: (i, 0)), pl.BlockSpec((1, d), lambda i: (0, 0)),
                  pl.BlockSpec((N_EXPERTS, d), lambda i: (0, 0))],
        out_specs=[pl.BlockSpec((SUBLANES, tt), lambda i: (0, i)),
                   pl.BlockSpec((1, N_EXPERTS, LANES), lambda i: (i, 0, 0)),
                   pl.BlockSpec((N_EXPERTS, LANES), lambda i: (0, 0))],
        out_shape=[jax.ShapeDtypeStruct((SUBLANES, m), F32), jax.ShapeDtypeStruct((nt, N_EXPERTS, LANES), F32),
                   jax.ShapeDtypeStruct((N_EXPERTS, LANES), F32)],
        scratch_shapes=[pltpu.VMEM((N_EXPERTS, LANES), F32)],
        compiler_params=_params("arbitrary"),
        name="moe_router",
    )(h, norm_gain.reshape(1, d), w_router.T)


def _slots_kernel(start_ref, info_ref, o_ref):
    info = info_ref[...]
    e1, e2, r1, r2 = info[0:1], info[1:2], info[2:3], info[3:4]
    s1 = r1
    s2 = r2
    for e in range(N_EXPERTS):
        st = start_ref[e].astype(F32)
        s1 = s1 + jnp.where(e1 == float(e), st, 0.0)
        s2 = s2 + jnp.where(e2 == float(e), st, 0.0)
    o_ref[...] = jnp.concatenate([s1, s2, info[4:5], info[5:6], info[6:8], info[6:8]], axis=0)


def moe_slots(info, slot_start):
    m = info.shape[1]
    tt = min(2048, m)
    return pl.pallas_call(
        _slots_kernel,
        grid_spec=pltpu.PrefetchScalarGridSpec(
            num_scalar_prefetch=1, grid=(m // tt,),
            in_specs=[pl.BlockSpec((SUBLANES, tt), lambda i, st: (0, i))],
            out_specs=pl.BlockSpec((SUBLANES, tt), lambda i, st: (0, i))),
        out_shape=jax.ShapeDtypeStruct((SUBLANES, m), F32),
        compiler_params=_params("parallel"),
        name="moe_slots",
    )(slot_start, info)


def _dispatch_kernel(blk_ref, til_ref, flg_ref, slots_ref, x_ref, o_ref, *, bs):
    w = pl.program_id(0)
    flags = flg_ref[w]
    first = (flags & 1) != 0
    real = (flags & 4) != 0

    def gathered():
        sl = slots_ref[...]
        sid = (blk_ref[w] * bs + lax.broadcasted_iota(I32, (bs, sl.shape[1]), 0)).astype(F32)
        onehot = jnp.where((sid == sl[0:1, :]) | (sid == sl[1:2, :]), 1.0, 0.0).astype(BF16)
        return _dot(onehot, x_ref[...]).astype(o_ref.dtype)

    @pl.when(first & real)
    def _():
        o_ref[...] = gathered()

    @pl.when(first & jnp.logical_not(real))
    def _():
        o_ref[...] = jnp.zeros_like(o_ref)

    @pl.when(jnp.logical_not(first) & real)
    def _():
        o_ref[...] += gathered()


def moe_dispatch(xn, slots, wl_blk, wl_til, wl_flg, n_slots):
    m, d = xn.shape
    bs = tt = MOE_TILE
    n_items = wl_blk.shape[0]
    return pl.pallas_call(
        functools.partial(_dispatch_kernel, bs=bs),
        grid_spec=pltpu.PrefetchScalarGridSpec(
            num_scalar_prefetch=3, grid=(n_items,),
            in_specs=[pl.BlockSpec((SUBLANES, tt), lambda w, b, t, f: (0, t[w])),
                      pl.BlockSpec((tt, d), lambda w, b, t, f: (t[w], 0))],
            out_specs=pl.BlockSpec((bs, d), lambda w, b, t, f: (b[w], 0))),
        out_shape=jax.ShapeDtypeStruct((n_slots, d), BF16),
        compiler_params=_params("arbitrary"),
        name="moe_dispatch",
    )(wl_blk, wl_til, wl_flg, slots, xn)


def _moe_ffn_kernel(be_ref, rows_ref, x_ref, w1_ref, w3_ref, w2_ref, o_ref, acc_ref, *, sub):
    b = pl.program_id(0)
    f = pl.program_id(1)
    bm = x_ref.shape[0]

    @pl.when(f == 0)
    def _():
        acc_ref[...] = jnp.zeros_like(acc_ref)

    for rows in range(sub, bm + 1, sub):
        @pl.when(rows_ref[b] == rows)
        def _(rows=rows):
            x = x_ref[0:rows, :]
            h = _silu(_dot(x, w1_ref[0].astype(BF16))) * _dot(x, w3_ref[0].astype(BF16))
            acc_ref[0:rows, :] += _dot(h.astype(BF16), w2_ref[0].astype(BF16))

    @pl.when(f == pl.num_programs(1) - 1)
    def _():
        o_ref[...] = acc_ref[...].astype(o_ref.dtype)


def moe_ffn(xs, w1, w3, w2, block_expert, block_rows):
    n_slots, d = xs.shape
    ff = w1.shape[2]
    bm = MOE_FFN_ROWS
    tf = MOE_TF
    nf = ff // tf

    def f_of(b, f, rows):
        return jnp.where(rows[b] > 0, f, nf - 1)

    return pl.pallas_call(
        functools.partial(_moe_ffn_kernel, sub=MOE_TILE),
        grid_spec=pltpu.PrefetchScalarGridSpec(
            num_scalar_prefetch=2, grid=(n_slots // bm, nf),
            in_specs=[pl.BlockSpec((bm, d), lambda b, f, be, nb: (b, 0)),
                      pl.BlockSpec((1, d, tf), lambda b, f, be, nb: (be[b], 0, f_of(b, f, nb))),
                      pl.BlockSpec((1, d, tf), lambda b, f, be, nb: (be[b], 0, f_of(b, f, nb))),
                      pl.BlockSpec((1, tf, d), lambda b, f, be, nb: (be[b], f_of(b, f, nb), 0))],
            out_specs=pl.BlockSpec((bm, d), lambda b, f, be, nb: (b, 0)),
            scratch_shapes=[pltpu.VMEM((bm, d), F32)]),
        out_shape=jax.ShapeDtypeStruct((n_slots, d), BF16),
        compiler_params=_params("arbitrary", "arbitrary"),
        name="moe_ffn",
    )(block_expert, block_rows, xs, w1, w3, w2)


def _combine_kernel(til_ref, blk_ref, flg_ref, tok_ref, y_ref, res_ref, o_ref, *, bs):
    w = pl.program_id(0)
    flags = flg_ref[w]

    @pl.when((flags & 1) != 0)
    def _():
        o_ref[...] = res_ref[...]

    @pl.when((flags & 4) != 0)
    def _():
        tok = tok_ref[...]
        tt = tok.shape[0]
        sid = (blk_ref[w] * bs + lax.broadcasted_iota(I32, (tt, bs), 1)).astype(F32)
        y = y_ref[...]
        oh1 = jnp.where(sid == tok[:, 0:1], 1.0, 0.0).astype(BF16)
        oh2 = jnp.where(sid == tok[:, 1:2], 1.0, 0.0).astype(BF16)
        o_ref[...] += tok[:, 2:3] * _dot(oh1, y) + tok[:, 3:4] * _dot(oh2, y)


def moe_combine(y, tok_info, residual, wl_til, wl_blk, wl_flg):
    m, d = residual.shape
    bs = tt = MOE_TILE
    n_items = wl_til.shape[0]
    return pl.pallas_call(
        functools.partial(_combine_kernel, bs=bs),
        grid_spec=pltpu.PrefetchScalarGridSpec(
            num_scalar_prefetch=3, grid=(n_items,),
            in_specs=[pl.BlockSpec((tt, SUBLANES), lambda w, t, b, f: (t[w], 0)),
                      pl.BlockSpec((bs, d), lambda w, t, b, f: (b[w], 0)),
                      pl.BlockSpec((tt, d), lambda w, t, b, f: (t[w], 0))],
            out_specs=pl.BlockSpec((tt, d), lambda w, t, b, f: (t[w], 0))),
        out_shape=jax.ShapeDtypeStruct((m, d), F32),
        compiler_params=_params("arbitrary"),
        name="moe_combine",
    )(wl_til, wl_blk, wl_flg, tok_info, y, residual)


def _work_list(overlap, n_items):
    n_major, n_minor = overlap.shape
    has = jnp.any(overlap, axis=1)
    marked = jnp.concatenate([overlap[:, :1] | ~has[:, None], overlap[:, 1:]], axis=1)
    maj, mino = jnp.nonzero(marked, size=n_items, fill_value=-1)
    valid = maj >= 0
    real = valid & overlap[jnp.maximum(maj, 0), jnp.maximum(mino, 0)]
    prev = jnp.concatenate([jnp.full((1,), -2, maj.dtype), maj[:-1]])
    nxt = jnp.concatenate([maj[1:], jnp.full((1,), -2, maj.dtype)])
    flags = (valid & (maj != prev)) * 1 + (valid & (maj != nxt)) * 2 + real * 4
    last_maj = jnp.max(jnp.where(valid, maj, 0))
    last_min = jnp.sum(jnp.where(valid & (maj == last_maj) & (maj != nxt), mino, 0))
    maj = jnp.where(valid, maj, last_maj)
    mino = jnp.where(valid, mino, last_min)
    return maj.astype(I32), mino.astype(I32), flags.astype(I32)


def moe_layer(h, norm_gain, w_router, w1, w3, w2):
    m, d = h.shape
    E = N_EXPERTS
    tile = min(MOE_TILE, m)
    n_tiles = m // tile
    ffn_rows = MOE_FFN_ROWS
    xn = rmsnorm(h, norm_gain, BF16)
    info, tile_cnt, total = moe_router(h, norm_gain, w_router)
    before = tile_cnt[:, :, 0]
    counts = total[:, 0].astype(I32)
    padded = (counts + ffn_rows - 1) // ffn_rows * ffn_rows
    ends = jnp.cumsum(padded)
    starts = ends - padded
    n_slots = (m * TOP_K // ffn_rows + E) * ffn_rows
    n_ffn_blocks = n_slots // ffn_rows
    n_dblocks = n_slots // tile
    slots = moe_slots(info, starts.astype(I32))
    db_start = jnp.arange(n_dblocks, dtype=I32) * tile
    db_e = jnp.minimum(jnp.sum(db_start[:, None] >= ends[None, :], axis=1), E - 1)
    r0 = db_start - starts[db_e]
    r1 = jnp.minimum(r0 + tile, counts[db_e])
    after = jnp.concatenate([before[1:], counts[None, :].astype(F32)], axis=0).astype(I32)
    bef = before.astype(I32)
    ov = (bef[:, db_e].T < r1[:, None]) & (after[:, db_e].T > r0[:, None]) & (db_start < ends[-1])[:, None]
    n_items = n_dblocks + E * n_tiles
    d_blk, d_til, d_flg = _work_list(ov, n_items)
    c_til, c_blk, c_flg = _work_list(ov.T, n_items)
    xs = moe_dispatch(xn, slots, d_blk, d_til, d_flg, n_slots)
    fb_start = jnp.arange(n_ffn_blocks, dtype=I32) * ffn_rows
    block_expert = jnp.minimum(jnp.sum(fb_start[:, None] >= ends[None, :], axis=1), E - 1).astype(I32)
    filled = counts[block_expert] - (fb_start - starts[block_expert])
    block_rows = jnp.where(fb_start < ends[-1], (jnp.clip(filled, 0, ffn_rows) + tile - 1) // tile * tile, 0).astype(I32)
    ys = moe_ffn(xs, w1, w3, w2, block_expert, block_rows)
    return moe_combine(ys, slots.T, h, c_til, c_blk, c_flg)


def even_mixer_layer(h, posf, p, bsz, seq):
    m = bsz * seq
    hn = rmsnorm(h, p["norm_mix"], BF16)
    w_in = p["w_in"]
    c = np.cumsum([0, D_SSM, D_CONV, SSM_HEADS, D_NSA] + [D_NSA_KV] * 6 + [3 * NSA_HEADS])
    z = matmul([(hn, w_in[:, c[0]:c[1]].astype(BF16))], out_dtype=BF16)
    xbc = matmul([(hn, w_in[:, c[1]:c[2]].astype(BF16))], out_dtype=BF16)
    q = matmul([(hn, w_in[:, c[3]:c[4]].astype(BF16))], out_dtype=BF16)
    kv6 = matmul([(hn, w_in[:, c[4]:c[10]].astype(BF16))], out_dtype=BF16)
    w_small = jnp.concatenate([w_in[:, c[2]:c[3]], w_in[:, c[10]:c[11]]], axis=1)
    w_small = jnp.pad(w_small, ((0, 0), (0, LANES - w_small.shape[1]))).astype(BF16)
    small = matmul([(hn, w_small)])
    y_ssm = ssd_group(z, xbc, small, p["conv_w"], p["conv_b"], p["dt_bias"], p["a_log"], p["d_skip"],
                      p["ssm_norm"], bsz, seq)
    qn, ksn, kwn = rope_norm(posf, q, kv6, p["q_gain"], p["ks_gain"], p["kw_gain"])
    nch = seq // CMP_STRIDE

    def chunks(col0):
        u = kv6[:, col0:col0 + D_NSA_KV].reshape(bsz, nch, CMP_STRIDE, NSA_KV_HEADS, NSA_HEAD_DIM)
        return u.transpose(0, 3, 1, 2, 4).reshape(bsz, NSA_KV_HEADS, nch, CMP_STRIDE * NSA_HEAD_DIM)

    kc, vc = compress(chunks(0), chunks(D_NSA_KV), p["pe_k"], p["pe_v"], p["wk1"], p["wk2"], p["wv1"], p["wv2"],
                      p["kc_gain"])
    o_cmp, selt = cmp_select(qn, kc, vc, bsz, seq)
    tk = min(ATT_TK, seq)

    def v_tiles(col0):
        u = kv6[:, col0:col0 + D_NSA_KV].astype(BF16).reshape(bsz, seq // tk, tk, NSA_KV_HEADS, NSA_HEAD_DIM)
        return u.transpose(0, 3, 1, 4, 2)

    o_sel = masked_attention(qn, ksn, 0, v_tiles(3 * D_NSA_KV), bsz, seq, "sel", selt)
    o_win = masked_attention(qn, kwn, 0, v_tiles(5 * D_NSA_KV), bsz, seq, "win")
    y_nsa = nsa_combine(small, o_cmp, o_sel, o_win)
    w_out = p["w_out"].astype(BF16)
    return matmul([(y_ssm, w_out[:D_SSM]), (y_nsa, w_out[D_SSM:])], residual=h)


def rwkv_layer(h, p, bsz, seq):
    m, d = h.shape
    mixes = rwkv_mix(h, p["norm_mix"], p["mu"], seq)
    bf = lambda w: w.astype(BF16)

    def lora_pad(w_a, w_b):
        r = w_a.shape[1]
        rp = -(-r // LANES) * LANES
        return bf(jnp.pad(w_a, ((0, 0), (0, rp - r)))), bf(jnp.pad(w_b, ((0, rp - r), (0, 0))))

    perm = _wkv_column_order()
    r = matmul([(mixes[0], bf(p["w_r"][:, perm]))], out_dtype=BF16)
    k = matmul([(mixes[2], bf(p["w_k"][:, perm]))], out_dtype=BF16)
    v = matmul([(mixes[3], bf(p["w_v"][:, perm]))], out_dtype=BF16)
    w1, w2 = lora_pad(p["w1"], p["w2"][:, perm])
    w_raw = matmul([(matmul([(mixes[1], w1)], act="tanh", out_dtype=BF16), w2)], bias=p["w0"][perm])
    a1, a2 = lora_pad(p["a1"], p["a2"][:, perm])
    a_sig = matmul([(matmul([(mixes[4], a1)], out_dtype=BF16), a2)], bias=p["a0"][perm], act="sigmoid", out_dtype=BF16)
    g1, g2 = lora_pad(p["g1"], p["g2"][:, perm])
    g = matmul([(matmul([(mixes[5], g1)], act="sigmoid", out_dtype=BF16), g2)], out_dtype=BF16)
    o = wkv_fused(r, k, v, w_raw, a_sig, g, p["k_k"], p["k_a"], p["r_k"].reshape(-1), p["ln_w"], p["ln_b"], bsz, seq)
    return matmul([(o, bf(p["w_o"][perm, :]))], residual=h)


def kernel(x, mem, positions, norm_mix, norm_xattn, norm_mem, norm_ffn, xattn_wq, xattn_wkv, xattn_wo, xattn_q_gain, xattn_k_gain, ev_w_in, ev_conv_w, ev_conv_b, ev_dt_bias, ev_a_log, ev_d_skip, ev_ssm_norm, ev_q_gain, ev_kc_gain, ev_ks_gain, ev_kw_gain, ev_pe_k, ev_pe_v, ev_cmp_wk1, ev_cmp_wk2, ev_cmp_wv1, ev_cmp_wv2, ev_w_out, ev_ffn_w1, ev_ffn_w3, ev_ffn_w2, od_mu, od_w_r, od_w_k, od_w_v, od_w_o, od_w0, od_w1, od_w2, od_a0, od_a1, od_a2, od_g1, od_g2, od_k_k, od_k_a, od_r_k, od_ln_w, od_ln_b, od_router, od_moe_w1, od_moe_w3, od_moe_w2):
    bsz, seq, d = x.shape
    m = bsz * seq
    depth = norm_mix.shape[0]
    h = x.reshape(m, d)
    mem2d = mem.reshape(-1, d)
    posf = positions.astype(F32).reshape(m, 1)
    for layer in range(depth):
        i = layer // 2
        if layer % 2 == 0:
            p = dict(norm_mix=norm_mix[layer], w_in=ev_w_in[i], conv_w=ev_conv_w[i], conv_b=ev_conv_b[i],
                     dt_bias=ev_dt_bias[i], a_log=ev_a_log[i], d_skip=ev_d_skip[i], ssm_norm=ev_ssm_norm[i],
                     q_gain=ev_q_gain[i], kc_gain=ev_kc_gain[i], ks_gain=ev_ks_gain[i], kw_gain=ev_kw_gain[i],
                     pe_k=ev_pe_k[i], pe_v=ev_pe_v[i], wk1=ev_cmp_wk1[i], wk2=ev_cmp_wk2[i], wv1=ev_cmp_wv1[i],
                     wv2=ev_cmp_wv2[i], w_out=ev_w_out[i])
            h = even_mixer_layer(h, posf, p, bsz, seq)
        else:
            p = dict(norm_mix=norm_mix[layer], mu=od_mu[i], w_r=od_w_r[i], w_k=od_w_k[i], w_v=od_w_v[i], w_o=od_w_o[i],
                     w0=od_w0[i], w1=od_w1[i], w2=od_w2[i], a0=od_a0[i], a1=od_a1[i], a2=od_a2[i], g1=od_g1[i],
                     g2=od_g2[i], k_k=od_k_k[i], k_a=od_k_a[i], r_k=od_r_k[i], ln_w=od_ln_w[i], ln_b=od_ln_b[i])
            h = rwkv_layer(h, p, bsz, seq)
        h = memory_xattn(h, mem2d, norm_xattn[layer], norm_mem[layer], xattn_wq[layer], xattn_wkv[layer],
                         xattn_wo[layer], xattn_q_gain[layer], xattn_k_gain[layer], bsz, seq)
        if layer % 2 == 0:
            hn = rmsnorm(h, norm_ffn[layer], BF16)
            h = swiglu_ffn(hn, ev_ffn_w1[i].astype(BF16), ev_ffn_w3[i].astype(BF16), ev_ffn_w2[i].astype(BF16), h)
        else:
            h = moe_layer(h, norm_ffn[layer], od_router[i], od_moe_w1[i], od_moe_w3[i], od_moe_w2[i])
    return h.reshape(bsz, seq, d)
```

```python
import functools
import math

import numpy as np
import jax
import jax.numpy as jnp
from jax import lax
from jax.experimental import pallas as pl
from jax.experimental.pallas import tpu as pltpu

F32 = jnp.float32
BF16 = jnp.bfloat16
I32 = jnp.int32

D_MODEL = 2048
NORM_EPS = 1e-6
NEG_BIG = -1e30

SSM_HEADS = 32
SSM_HEAD_DIM = 64
D_SSM = SSM_HEADS * SSM_HEAD_DIM
SSM_GROUPS = 4
D_STATE = 128
CONV_WIDTH = 4
SSD_CHUNK = 128
D_CONV = D_SSM + 2 * SSM_GROUPS * D_STATE
D_SSM_GROUP = D_SSM // SSM_GROUPS

NSA_HEADS = 16
NSA_KV_HEADS = 4
NSA_HEAD_DIM = 128
NSA_Q_PER_KV = NSA_HEADS // NSA_KV_HEADS
D_NSA = NSA_HEADS * NSA_HEAD_DIM
D_NSA_KV = NSA_KV_HEADS * NSA_HEAD_DIM
CMP_BLOCK = 32
CMP_STRIDE = 16
CMP_HIDDEN = 256
SEL_BLOCK = 64
N_SELECT = 16
WINDOW = 512
ROPE_DIM = NSA_HEAD_DIM // 4
ROPE_THETA = 500000.0

RWKV_HEAD_DIM = 64
RWKV_HEADS = D_MODEL // RWKV_HEAD_DIM
RWKV_GN_EPS = 1e-5 * RWKV_HEAD_DIM

XATTN_HEADS = 4
XATTN_HEAD_DIM = 128
D_XATTN = XATTN_HEADS * XATTN_HEAD_DIM

N_EXPERTS = 8
TOP_K = 2

LANES = 128
SUBLANES = 8
VMEM_LIMIT_BYTES = 56 * 1024 * 1024

ROW_TILE = 512
MM_TM = 2048
MM_TN = 512
FFN_TM = 512
FFN_TF = 512
FFN_SPLIT = 256
MOE_FFN_ROWS = 1024
MOE_TF = 256
COMBINE_TM = 256
MOE_TILE = 512
ATT_TQ = 256
ATT_TK = 512
LOG2E = 1.4426950408889634
CMP_TQ = 128
WKV_T = 64
WKV_LOOKAHEAD = 8


def _params(*sem):
    return pltpu.CompilerParams(dimension_semantics=sem, vmem_limit_bytes=VMEM_LIMIT_BYTES)


def _sigmoid(x):
    return 1.0 / (1.0 + jnp.exp(-x))


def _silu(x):
    return x * _sigmoid(x)


def _softplus(x):
    return jnp.maximum(x, 0.0) + jnp.log(1.0 + jnp.exp(-jnp.abs(x)))


def _split3(x):
    h = x.astype(BF16)
    r = x - h.astype(F32)
    m = r.astype(BF16)
    l = (r - m.astype(F32)).astype(BF16)
    return h, m, l


def _dot(a, b):
    return jnp.dot(a, b, preferred_element_type=F32)


def _dot_nt(a, b):
    return lax.dot_general(a, b, (((1,), (1,)), ((), ())), preferred_element_type=F32)


def _dot_f32_sel(x, e):
    h, m, l = _split3(x)
    return _dot(h, e) + _dot(m, e) + _dot(l, e)


def _sel_dot_f32(e, x):
    h, m, l = _split3(x)
    return _dot(e, h) + _dot(e, m) + _dot(e, l)


def _rms(x, eps=NORM_EPS):
    return x * lax.rsqrt(jnp.mean(x * x, axis=-1, keepdims=True) + eps)


def _rmsnorm_kernel(x_ref, g_ref, o_ref):
    o_ref[...] = (_rms(x_ref[...]) * g_ref[...]).astype(o_ref.dtype)


def rmsnorm(x, gain, out_dtype):
    m, d = x.shape
    tm = min(ROW_TILE, m)
    return pl.pallas_call(
        _rmsnorm_kernel,
        grid=(m // tm,),
        in_specs=[pl.BlockSpec((tm, d), lambda i: (i, 0)), pl.BlockSpec((1, d), lambda i: (0, 0))],
        out_specs=pl.BlockSpec((tm, d), lambda i: (i, 0)),
        out_shape=jax.ShapeDtypeStruct((m, d), out_dtype),
        compiler_params=_params("parallel"),
        name="rmsnorm",
    )(x, gain.reshape(1, d))


def _mm_kernel(*refs, n_pairs, has_bias, has_res, act):
    o_ref = refs[-1]
    acc = None
    for p in range(n_pairs):
        a = refs[2 * p][...].astype(BF16)
        d = _dot(a, refs[2 * p + 1][...])
        acc = d if acc is None else acc + d
    k = 2 * n_pairs
    if has_bias:
        acc = acc + refs[k][...]
        k += 1
    if act == "tanh":
        acc = jnp.tanh(acc)
    elif act == "sigmoid":
        acc = _sigmoid(acc)
    if has_res:
        acc = acc + refs[k][...]
    o_ref[...] = acc.astype(o_ref.dtype)


def matmul(pairs, bias=None, residual=None, act=None, out_dtype=F32):
    m = pairs[0][0].shape[0]
    n = pairs[0][1].shape[1]
    k_total = sum(a.shape[1] for a, _ in pairs)
    tm = min(MM_TM if k_total <= D_MODEL else MM_TM // 2, m)
    tn = MM_TN if n % MM_TN == 0 else (256 if n % 256 == 0 else LANES)
    tn = min(tn, n)
    in_specs, args = [], []
    for a, w in pairs:
        kk = a.shape[1]
        in_specs += [pl.BlockSpec((tm, kk), lambda i, j: (i, 0)), pl.BlockSpec((kk, tn), lambda i, j: (0, j))]
        args += [a, w]
    if bias is not None:
        in_specs.append(pl.BlockSpec((1, tn), lambda i, j: (0, j)))
        args.append(bias.reshape(1, n).astype(F32))
    if residual is not None:
        in_specs.append(pl.BlockSpec((tm, tn), lambda i, j: (i, j)))
        args.append(residual)
    kern = functools.partial(_mm_kernel, n_pairs=len(pairs), has_bias=bias is not None,
                             has_res=residual is not None, act=act)
    return pl.pallas_call(
        kern,
        grid=(m // tm, n // tn),
        in_specs=in_specs,
        out_specs=pl.BlockSpec((tm, tn), lambda i, j: (i, j)),
        out_shape=jax.ShapeDtypeStruct((m, n), out_dtype),
        compiler_params=_params("parallel", "parallel"),
        name="matmul",
    )(*args)


def _ffn_kernel(x_ref, w1_ref, w3_ref, w2_ref, res_ref, o_ref):
    f = pl.program_id(1)

    @pl.when(f == 0)
    def _():
        o_ref[...] = res_ref[...]

    x = x_ref[...]
    tf = w1_ref.shape[1]
    cols = [slice(c, c + FFN_SPLIT) for c in range(0, tf, FFN_SPLIT)]
    gate_up = [(_dot(x, w1_ref[:, c]), _dot(x, w3_ref[:, c])) for c in cols]
    hs = [(_silu(g) * u).astype(BF16) for g, u in gate_up]
    acc = _dot(hs[0], w2_ref[cols[0], :])
    for h, c in zip(hs[1:], cols[1:]):
        acc = acc + _dot(h, w2_ref[c, :])
    o_ref[...] += acc


def swiglu_ffn(x, w1, w3, w2, residual):
    m, d = x.shape
    ff = w1.shape[1]
    tm = min(FFN_TM, m)
    tf = FFN_TF
    return pl.pallas_call(
        _ffn_kernel,
        grid=(m // tm, ff // tf),
        in_specs=[pl.BlockSpec((tm, d), lambda i, f: (i, 0)),
                  pl.BlockSpec((d, tf), lambda i, f: (0, f)),
                  pl.BlockSpec((d, tf), lambda i, f: (0, f)),
                  pl.BlockSpec((tf, d), lambda i, f: (f, 0)),
                  pl.BlockSpec((tm, d), lambda i, f: (i, 0))],
        out_specs=pl.BlockSpec((tm, d), lambda i, f: (i, 0)),
        out_shape=jax.ShapeDtypeStruct((m, d), F32),
        compiler_params=_params("parallel", "arbitrary"),
        name="swiglu_ffn",
    )(x, w1, w3, w2, residual)


def _ssd_kernel(z_ref, xbc_ref, dt_ref, cw_ref, cb_ref, dtb_ref, alog_ref, dskip_ref, nw_ref, eh_ref,
                o_ref, ext_ref, st_ref):
    L = SSD_CHUNK
    G = SSM_GROUPS
    GW = D_SSM_GROUP
    c = pl.program_id(1)

    @pl.when(c == 0)
    def _():
        ext_ref[L:L + SUBLANES, :] = jnp.zeros((SUBLANES, D_CONV), F32)
        st_ref[...] = jnp.zeros_like(st_ref)

    ext_ref[0:SUBLANES, :] = ext_ref[L:L + SUBLANES, :]
    ext_ref[SUBLANES:L + SUBLANES, :] = xbc_ref[...].astype(F32)
    acc = jnp.zeros((L, D_CONV), F32) + cb_ref[...]
    for k in range(CONV_WIDTH):
        off = SUBLANES - (CONV_WIDTH - 1) + k
        acc = acc + cw_ref[k:k + 1, :] * ext_ref[off:off + L, :]
    xc = _silu(acc)

    dt = _softplus(dt_ref[...] + dtb_ref[...])
    a = -jnp.exp(alog_ref[...])
    adt = dt * a
    row = lax.broadcasted_iota(I32, (L, L), 0)
    col = lax.broadcasted_iota(I32, (L, L), 1)
    causal = row >= col
    tril = jnp.where(causal, 1.0, 0.0).astype(BF16)
    a_cum = _sel_dot_f32(tril, adt)
    a_cum_t = a_cum.T
    eh = eh_ref[...]
    acx = _dot_f32_sel(a_cum, eh)
    dtx = _dot_f32_sel(dt, eh)
    a_end = acx[L - 1:L, :]
    xs = xc[:, :D_SSM]
    xdt = xs * dtx
    xdte = (xdt * jnp.exp(a_end - acx)).astype(BF16)
    from_start = jnp.exp(acx)
    lane = lax.broadcasted_iota(I32, (L, LANES), 1)
    lo_half = lane < SSM_HEAD_DIM
    z = z_ref[...].astype(F32)

    for g in range(G):
        bm = xc[:, D_SSM + g * D_STATE:D_SSM + (g + 1) * D_STATE]
        cm = xc[:, D_SSM + G * D_STATE + g * D_STATE:D_SSM + G * D_STATE + (g + 1) * D_STATE]
        bm_b = bm.astype(BF16)
        cm_b = cm.astype(BF16)
        cb = _dot_nt(cm_b, bm_b)
        st = st_ref[g]
        y_off = _dot(cm_b, st.astype(BF16)) * from_start[:, g * GW:(g + 1) * GW]
        pieces = []
        for pp in range(GW // LANES):
            h0 = g * (GW // SSM_HEAD_DIM) + 2 * pp
            ms = []
            for hh in (h0, h0 + 1):
                diff = a_cum[:, hh:hh + 1] - a_cum_t[hh:hh + 1, :]
                dec = jnp.exp(jnp.where(causal, diff, NEG_BIG))
                ms.append((cb * dec).astype(BF16))
            lhs = jnp.concatenate(ms, axis=1)
            c0 = g * GW + pp * LANES
            xp = xdt[:, c0:c0 + LANES]
            rhs = jnp.concatenate([jnp.where(lo_half, xp, 0.0), jnp.where(lo_half, 0.0, xp)], axis=0)
            pieces.append(_dot(lhs, rhs.astype(BF16)))
        y = jnp.concatenate(pieces, axis=1) + y_off
        y = y + xs[:, g * GW:(g + 1) * GW] * dskip_ref[:, g * GW:(g + 1) * GW]
        y = y * _silu(z[:, g * GW:(g + 1) * GW])
        y = _rms(y) * nw_ref[:, g * GW:(g + 1) * GW]
        o_ref[:, g * GW:(g + 1) * GW] = y.astype(o_ref.dtype)
        s_new = _dot(bm.T.astype(BF16), xdte[:, g * GW:(g + 1) * GW])
        st_ref[g] = st * jnp.exp(a_end[:, g * GW:(g + 1) * GW]) + s_new


def ssd_group(z, xbc, small, conv_w, conv_b, dt_bias, a_log, d_skip, norm_w, bsz, seq):
    m = bsz * seq
    L = SSD_CHUNK
    nc = seq // L
    pad = LANES - SSM_HEADS
    eh = np.zeros((LANES, D_SSM), np.float32)
    for h in range(SSM_HEADS):
        eh[h, h * SSM_HEAD_DIM:(h + 1) * SSM_HEAD_DIM] = 1.0
    row_spec = lambda w: pl.BlockSpec((L, w), lambda b, c: (b * nc + c, 0))
    const = lambda r, w: pl.BlockSpec((r, w), lambda b, c: (0, 0))
    return pl.pallas_call(
        _ssd_kernel,
        grid=(bsz, nc),
        in_specs=[row_spec(D_SSM), row_spec(D_CONV), row_spec(LANES),
                  const(CONV_WIDTH, D_CONV), const(1, D_CONV), const(1, LANES), const(1, LANES),
                  const(1, D_SSM), const(1, D_SSM), const(LANES, D_SSM)],
        out_specs=row_spec(D_SSM),
        out_shape=jax.ShapeDtypeStruct((m, D_SSM), BF16),
        scratch_shapes=[pltpu.VMEM((L + SUBLANES, D_CONV), F32),
                        pltpu.VMEM((SSM_GROUPS, D_STATE, D_SSM_GROUP), F32)],
        compiler_params=_params("parallel", "arbitrary"),
        name="ssd_group",
    )(z, xbc, small, conv_w.T, conv_b.reshape(1, D_CONV),
      jnp.pad(dt_bias, (0, pad)).reshape(1, LANES), jnp.pad(a_log, (0, pad)).reshape(1, LANES),
      jnp.repeat(d_skip, SSM_HEAD_DIM).reshape(1, D_SSM), norm_w.reshape(1, D_SSM), jnp.asarray(eh, BF16))


def _rope_norm_kernel(pos_ref, invf_ref, q_ref, kv_ref, qg_ref, ksg_ref, kwg_ref, qo_ref, kso_ref, kwo_ref):
    ang = pos_ref[...] * invf_ref[...]
    cos = jnp.cos(ang)
    sin = jnp.sin(ang)
    lane = lax.broadcasted_iota(I32, ang.shape, 1)
    first = lane < ROPE_DIM // 2
    sin_signed = jnp.where(first, -sin, sin)

    def head(x, gain):
        y = _rms(x.astype(F32)) * gain
        partner = jnp.where(first, pltpu.roll(y, LANES - ROPE_DIM // 2, axis=1), pltpu.roll(y, ROPE_DIM // 2, axis=1))
        return y * cos + partner * sin_signed

    hd = NSA_HEAD_DIM
    scale = hd ** -0.5 * LOG2E
    for h in range(NSA_HEADS):
        qo_ref[:, h * hd:(h + 1) * hd] = (head(q_ref[:, h * hd:(h + 1) * hd], qg_ref[...]) * scale).astype(qo_ref.dtype)
    for h in range(NSA_KV_HEADS):
        ks = kv_ref[:, 2 * D_NSA_KV + h * hd:2 * D_NSA_KV + (h + 1) * hd]
        kw = kv_ref[:, 4 * D_NSA_KV + h * hd:4 * D_NSA_KV + (h + 1) * hd]
        kso_ref[:, h * hd:(h + 1) * hd] = head(ks, ksg_ref[...]).astype(kso_ref.dtype)
        kwo_ref[:, h * hd:(h + 1) * hd] = head(kw, kwg_ref[...]).astype(kwo_ref.dtype)


def rope_norm(posf, q, kv6, q_gain, ks_gain, kw_gain):
    m = q.shape[0]
    tm = min(ROW_TILE, m)
    half = ROPE_DIM // 2
    inv = np.exp(-math.log(ROPE_THETA) * np.arange(0, ROPE_DIM, 2, dtype=np.float32) / ROPE_DIM).astype(np.float32)
    invf = np.zeros((1, LANES), np.float32)
    invf[0, :half] = inv
    invf[0, half:ROPE_DIM] = inv
    row = lambda w: pl.BlockSpec((tm, w), lambda i: (i, 0))
    const = pl.BlockSpec((1, LANES), lambda i: (0, 0))
    return pl.pallas_call(
        _rope_norm_kernel,
        grid=(m // tm,),
        in_specs=[row(1), const, row(D_NSA), row(6 * D_NSA_KV), const, const, const],
        out_specs=[row(D_NSA), row(D_NSA_KV), row(D_NSA_KV)],
        out_shape=[jax.ShapeDtypeStruct((m, D_NSA), BF16), jax.ShapeDtypeStruct((m, D_NSA_KV), BF16),
                   jax.ShapeDtypeStruct((m, D_NSA_KV), BF16)],
        compiler_params=_params("parallel"),
        name="rope_norm",
    )(posf, jnp.asarray(invf), q, kv6, q_gain.reshape(1, LANES), ks_gain.reshape(1, LANES), kw_gain.reshape(1, LANES))


def _compress_kernel(uk_ref, uv_ref, pek_ref, pev_ref, wk1_ref, wk2_ref, wv1_ref, wv2_ref, g_ref, kc_ref, vc_ref):
    def mlp(u, pe_ref, w1_ref, w2_ref):
        u = u.astype(F32)
        n = u.shape[0]
        h1 = _dot((u + pe_ref[0:1, :]).astype(BF16), w1_ref[0])
        h2 = _dot((u + pe_ref[1:2, :]).astype(BF16), w1_ref[1])
        pre = h1 + pltpu.roll(h2, n - 1, axis=0)
        return _dot(_silu(pre).astype(BF16), w2_ref[...])

    kc = mlp(uk_ref[0, 0], pek_ref, wk1_ref, wk2_ref)
    kc_ref[0, 0] = _rms(kc) * g_ref[...]
    vc_ref[0, 0] = mlp(uv_ref[0, 0], pev_ref, wv1_ref, wv2_ref)


def compress(uk, uv, pe_k, pe_v, wk1, wk2, wv1, wv2, kc_gain):
    bsz, nkv, nch, width = uk.shape
    half = CMP_STRIDE * NSA_HEAD_DIM
    u_spec = pl.BlockSpec((1, 1, nch, width), lambda b, g: (b, g, 0, 0))
    o_spec = pl.BlockSpec((1, 1, nch, NSA_HEAD_DIM), lambda b, g: (b, g, 0, 0))
    c2 = lambda s: pl.BlockSpec(s, lambda b, g: (0, 0))
    c3 = lambda s: pl.BlockSpec(s, lambda b, g: (0, 0, 0))
    o_shape = jax.ShapeDtypeStruct((bsz, nkv, nch, NSA_HEAD_DIM), F32)
    return pl.pallas_call(
        _compress_kernel,
        grid=(bsz, nkv),
        in_specs=[u_spec, u_spec, c2((2, half)), c2((2, half)),
                  c3((2, half, CMP_HIDDEN)), c2((CMP_HIDDEN, NSA_HEAD_DIM)),
                  c3((2, half, CMP_HIDDEN)), c2((CMP_HIDDEN, NSA_HEAD_DIM)), c2((1, NSA_HEAD_DIM))],
        out_specs=[o_spec, o_spec],
        out_shape=[o_shape, o_shape],
        compiler_params=_params("parallel", "parallel"),
        name="nsa_compress",
    )(uk, uv, pe_k.reshape(2, half), pe_v.reshape(2, half),
      wk1.reshape(2, half, CMP_HIDDEN).astype(BF16), wk2.astype(BF16),
      wv1.reshape(2, half, CMP_HIDDEN).astype(BF16), wv2.astype(BF16), kc_gain.reshape(1, NSA_HEAD_DIM))


def _stack_heads(q):
    return jnp.concatenate([q[:, r * NSA_HEAD_DIM:(r + 1) * NSA_HEAD_DIM] for r in range(NSA_Q_PER_KV)], axis=0)


def _unstack_heads(o, tq):
    return jnp.concatenate([o[r * tq:(r + 1) * tq, :] for r in range(NSA_Q_PER_KV)], axis=1)


def _cmp_select_kernel(q_ref, kc_ref, vc_ref, ovt_ref, o_ref, sel_ref, *, tq, n_blk):
    R = NSA_Q_PER_KV
    q0 = pl.program_id(2) * tq
    q4 = _stack_heads(q_ref[...])
    kc = kc_ref[0, 0].astype(BF16)
    ncmp = kc.shape[0]
    s = _dot_nt(q4, kc)
    t_row = q0 + lax.broadcasted_iota(I32, (tq, ncmp), 0)
    c_col = lax.broadcasted_iota(I32, (tq, ncmp), 1)
    visf = jnp.where((c_col * CMP_STRIDE + CMP_BLOCK - 1) <= t_row, 1.0, 0.0)
    vis = jnp.concatenate([visf] * R, axis=0) > 0.5
    s = jnp.where(vis, s, NEG_BIG)
    mx = jnp.max(s, axis=-1, keepdims=True)
    e = jnp.where(vis, jnp.exp2(s - mx), 0.0)
    den = jnp.sum(e, axis=-1, keepdims=True)
    p = e / jnp.maximum(den, 1e-30)
    o = _dot(p.astype(BF16), vc_ref[0, 0].astype(BF16))
    o_ref[...] = _unstack_heads(o, tq)

    psum = p[0:tq]
    for r in range(1, R):
        psum = psum + p[r * tq:(r + 1) * tq]
    h, m, l = _split3(psum)
    ovt = ovt_ref[...]
    imp = _dot_nt(ovt, h) + _dot_nt(ovt, m) + _dot_nt(ovt, l)
    jdx = lax.broadcasted_iota(I32, (n_blk, tq), 0)
    t_lane = q0 + lax.broadcasted_iota(I32, (n_blk, tq), 1)
    cur = t_lane // SEL_BLOCK
    causal = jdx <= cur
    forced = ((jdx == 0) | (jdx >= cur - 1)) & causal
    score = jnp.where(forced, jnp.inf, jnp.where(causal, imp, -jnp.inf))
    rank = jnp.zeros((n_blk, tq), F32)
    for jp in range(n_blk):
        rowv = score[jp:jp + 1, :]
        beats = (rowv > score) | ((rowv == score) & (jdx > jp))
        rank = rank + jnp.where(beats, 1.0, 0.0)
    sel = jnp.where((rank < float(min(N_SELECT, n_blk))) & causal, 1.0, 0.0)
    if n_blk < LANES:
        sel = jnp.concatenate([sel, jnp.zeros((LANES - n_blk, tq), F32)], axis=0)
    sel_ref[0, 0] = sel.astype(sel_ref.dtype)


def cmp_select(qn, kc, vc, bsz, seq):
    m = bsz * seq
    G = NSA_KV_HEADS
    tq = CMP_TQ
    nq = seq // tq
    ncmp = kc.shape[2]
    n_blk = seq // SEL_BLOCK
    assert n_blk <= LANES and tq == LANES
    c0 = np.arange(ncmp)[None, :] * CMP_STRIDE
    s0 = np.arange(n_blk)[:, None] * SEL_BLOCK
    ov = np.clip(np.minimum(c0 + CMP_BLOCK, s0 + SEL_BLOCK) - np.maximum(c0, s0), 0, None) / CMP_STRIDE
    n_cmp_valid = (seq - CMP_BLOCK) // CMP_STRIDE + 1
    ov[:, n_cmp_valid:] = 0.0
    kv_spec = pl.BlockSpec((1, 1, ncmp, NSA_HEAD_DIM), lambda b, g, i: (b, g, 0, 0))
    kern = functools.partial(_cmp_select_kernel, tq=tq, n_blk=n_blk)
    return pl.pallas_call(
        kern,
        grid=(bsz, G, nq),
        in_specs=[pl.BlockSpec((tq, NSA_Q_PER_KV * NSA_HEAD_DIM), lambda b, g, i: (b * nq + i, g)),
                  kv_spec, kv_spec, pl.BlockSpec((n_blk, ncmp), lambda b, g, i: (0, 0))],
        out_specs=[pl.BlockSpec((tq, NSA_Q_PER_KV * NSA_HEAD_DIM), lambda b, g, i: (b * nq + i, g)),
                   pl.BlockSpec((1, 1, LANES, tq), lambda b, g, i: (b, g, 0, i))],
        out_shape=[jax.ShapeDtypeStruct((m, D_NSA), F32), jax.ShapeDtypeStruct((bsz, G, LANES, seq), BF16)],
        compiler_params=_params("parallel", "parallel", "parallel"),
        name="nsa_cmp_select",
    )(qn, kc, vc, jnp.asarray(ov, BF16))


def _masked_attn_kernel(q_ref, k_ref, vt_ref, *rest, mode, tq, tk):
    if mode == "sel":
        selt_ref, o_ref = rest
    else:
        (o_ref,) = rest
    R = NSA_Q_PER_KV
    hd = NSA_HEAD_DIM
    q0 = pl.program_id(2) * tq
    key_row = lax.broadcasted_iota(I32, (tk, tq), 0)
    t_lane = q0 + lax.broadcasted_iota(I32, (tk, tq), 1)
    hi = (q0 + tq - 1) // tk + 1
    q = q_ref[...]
    if mode == "sel":
        key_row_b = lax.broadcasted_iota(I32, (tk, LANES), 0)
        blk_lane = lax.broadcasted_iota(I32, (tk, LANES), 1)
        penalty = ((1.0 - selt_ref[0, 0].astype(F32)) * NEG_BIG).astype(BF16)
        rhs = [jnp.concatenate([q[:, r * hd:(r + 1) * hd].astype(F32).T.astype(BF16), penalty], axis=0)
               for r in range(R)]
        lo = 0
    else:
        lo = jnp.maximum(q0 - (WINDOW - 1), 0) // tk

    def body(kt, carry, diagonal=True):
        k0 = kt * tk
        k_t = k_ref[pl.ds(pl.multiple_of(k0, tk), tk), :]
        v_t = vt_ref[0, 0, kt]
        key = k0 + key_row
        if mode == "sel":
            expand = jnp.where((k0 + key_row_b) // SEL_BLOCK == blk_lane, 1.0, 0.0).astype(BF16)
            lhs = jnp.concatenate([k_t, expand], axis=1)
            ss = [_dot(lhs, rhs[r]) for r in range(R)]
            if diagonal:
                ss = [jnp.where(key <= t_lane, s, NEG_BIG) for s in ss]
        else:
            bias = jnp.where((key <= t_lane) & (key > t_lane - WINDOW), 0.0, NEG_BIG)
            ss = [_dot_nt(k_t, q[:, r * hd:(r + 1) * hd]) + bias for r in range(R)]
        stats = []
        for r in range(R):
            m_prev, l_prev = carry[3 * r], carry[3 * r + 1]
            m_new = jnp.maximum(m_prev, jnp.max(ss[r], axis=0, keepdims=True))
            p = jnp.exp2(ss[r] - m_new)
            alpha = jnp.exp2(m_prev - m_new)
            stats.append((m_new, alpha * l_prev + jnp.sum(p, axis=0, keepdims=True), alpha, p.astype(BF16)))
        new = []
        for r in range(R):
            m_new, l_new, alpha, p = stats[r]
            new += [m_new, l_new, alpha * carry[3 * r + 2] + _dot(v_t, p)]
        return tuple(new)

    init = (jnp.full((1, tq), NEG_BIG, F32), jnp.zeros((1, tq), F32), jnp.zeros((hd, tq), F32)) * R
    if mode == "sel":
        out = lax.fori_loop(lo, hi - 1, functools.partial(body, diagonal=False), init)
        out = body(hi - 1, out)
    else:
        out = lax.fori_loop(lo, hi, body, init)
    o_ref[...] = jnp.concatenate([(out[3 * r + 2] / out[3 * r + 1]).T for r in range(R)], axis=1)


def masked_attention(qn, k, k_col0, vt, bsz, seq, mode, selt=None):
    m = bsz * seq
    G = NSA_KV_HEADS
    tq = min(ATT_TQ, seq)
    tk = vt.shape[-1]
    assert tk % tq == 0
    nq = seq // tq
    hd = NSA_HEAD_DIM
    q_spec = pl.BlockSpec((tq, NSA_Q_PER_KV * hd), lambda b, g, i: (b * nq + i, g))
    in_specs = [q_spec,
                pl.BlockSpec((seq, hd), lambda b, g, i: (b, k_col0 + g)),
                pl.BlockSpec((1, 1, seq // tk, hd, tk), lambda b, g, i: (b, g, 0, 0, 0))]
    args = [qn, k, vt]
    if mode == "sel":
        in_specs.append(pl.BlockSpec((1, 1, LANES, tq), lambda b, g, i: (b, g, 0, i)))
        args.append(selt)
    return pl.pallas_call(
        functools.partial(_masked_attn_kernel, mode=mode, tq=tq, tk=tk),
        grid=(bsz, G, nq),
        in_specs=in_specs,
        out_specs=q_spec,
        out_shape=jax.ShapeDtypeStruct((m, D_NSA), F32),
        compiler_params=_params("parallel", "parallel", "arbitrary"),
        name="nsa_attn_" + mode,
    )(*args)


def _nsa_combine_kernel(g_ref, e_ref, oc_ref, os_ref, ow_ref, o_ref):
    gate = _sigmoid(g_ref[...])
    gx = _dot_f32_sel(gate, e_ref[...])
    y = gx[:, :D_NSA] * oc_ref[...] + gx[:, D_NSA:2 * D_NSA] * os_ref[...] + gx[:, 2 * D_NSA:] * ow_ref[...]
    o_ref[...] = y.astype(o_ref.dtype)


def nsa_combine(small, o_cmp, o_sel, o_win):
    m = small.shape[0]
    tm = min(COMBINE_TM, m)
    e = np.zeros((LANES, 3 * D_NSA), np.float32)
    for h in range(NSA_HEADS):
        for br in range(3):
            e[SSM_HEADS + 3 * h + br, br * D_NSA + h * NSA_HEAD_DIM:br * D_NSA + (h + 1) * NSA_HEAD_DIM] = 1.0
    row = lambda w: pl.BlockSpec((tm, w), lambda i: (i, 0))
    return pl.pallas_call(
        _nsa_combine_kernel,
        grid=(m // tm,),
        in_specs=[row(LANES), pl.BlockSpec((LANES, 3 * D_NSA), lambda i: (0, 0)), row(D_NSA), row(D_NSA), row(D_NSA)],
        out_specs=row(D_NSA),
        out_shape=jax.ShapeDtypeStruct((m, D_NSA), BF16),
        compiler_params=_params("parallel"),
        name="nsa_combine",
    )(small, jnp.asarray(e, BF16), o_cmp, o_sel, o_win)


def _xattn_kernel(q_ref, kv_ref, qg_ref, kg_ref, o_ref):
    hd = XATTN_HEAD_DIM
    scale = hd ** -0.5
    outs = []
    for h in range(XATTN_HEADS):
        q = (_rms(q_ref[:, h * hd:(h + 1) * hd]) * qg_ref[...]).astype(BF16)
        k = (_rms(kv_ref[:, h * hd:(h + 1) * hd]) * kg_ref[...]).astype(BF16)
        v = kv_ref[:, D_XATTN + h * hd:D_XATTN + (h + 1) * hd].astype(BF16)
        s = _dot_nt(q, k) * scale
        e = jnp.exp(s - jnp.max(s, axis=-1, keepdims=True))
        p = e / jnp.sum(e, axis=-1, keepdims=True)
        outs.append(_dot(p.astype(BF16), v))
    o_ref[...] = jnp.concatenate(outs, axis=1).astype(o_ref.dtype)


def xattn_core(q, kv, q_gain, k_gain, bsz, seq):
    m = bsz * seq
    n_mem = kv.shape[0] // bsz
    tm = min(ROW_TILE, seq)
    nt = seq // tm
    return pl.pallas_call(
        _xattn_kernel,
        grid=(bsz, nt),
        in_specs=[pl.BlockSpec((tm, D_XATTN), lambda b, i: (b * nt + i, 0)),
                  pl.BlockSpec((n_mem, 2 * D_XATTN), lambda b, i: (b, 0)),
                  pl.BlockSpec((1, XATTN_HEAD_DIM), lambda b, i: (0, 0)),
                  pl.BlockSpec((1, XATTN_HEAD_DIM), lambda b, i: (0, 0))],
        out_specs=pl.BlockSpec((tm, D_XATTN), lambda b, i: (b * nt + i, 0)),
        out_shape=jax.ShapeDtypeStruct((m, D_XATTN), BF16),
        compiler_params=_params("parallel", "parallel"),
        name="xattn_core",
    )(q, kv, q_gain.reshape(1, XATTN_HEAD_DIM), k_gain.reshape(1, XATTN_HEAD_DIM))


def memory_xattn(h, mem2d, norm_x, norm_m, wq, wkv, wo, q_gain, k_gain, bsz, seq):
    hq = rmsnorm(h, norm_x, BF16)
    q = matmul([(hq, wq.astype(BF16))])
    mem_n = rmsnorm(mem2d, norm_m, BF16)
    kv = matmul([(mem_n, wkv.astype(BF16))])
    o = xattn_core(q, kv, q_gain, k_gain, bsz, seq)
    return matmul([(o, wo.astype(BF16))], residual=h)


def _mix_kernel(h_ref, hp_ref, g_ref, mu_ref, *rest, tm, tiles_per_seq):
    outs, ext_ref = rest[:-1], rest[-1]
    i = pl.program_id(0)
    hn = _rms(h_ref[...]) * g_ref[...]
    hp = _rms(hp_ref[...]) * g_ref[...]
    hp = jnp.where(i % tiles_per_seq == 0, 0.0, hp)
    ext_ref[0:SUBLANES, :] = hp
    ext_ref[SUBLANES:tm + SUBLANES, :] = hn
    xx = ext_ref[SUBLANES - 1:tm + SUBLANES - 1, :] - hn
    for j, o_ref in enumerate(outs):
        o_ref[...] = (hn + xx * mu_ref[j:j + 1, :]).astype(o_ref.dtype)


def rwkv_mix(h, gain, mu, seq):
    m, d = h.shape
    tm = min(ROW_TILE, seq)
    n_mix = mu.shape[0]
    kern = functools.partial(_mix_kernel, tm=tm, tiles_per_seq=seq // tm)
    row = pl.BlockSpec((tm, d), lambda i: (i, 0))
    return pl.pallas_call(
        kern,
        grid=(m // tm,),
        in_specs=[row, pl.BlockSpec((SUBLANES, d), lambda i: (jnp.maximum(i * (tm // SUBLANES) - 1, 0), 0)),
                  pl.BlockSpec((1, d), lambda i: (0, 0)), pl.BlockSpec((n_mix, d), lambda i: (0, 0))],
        out_specs=[row] * n_mix,
        out_shape=[jax.ShapeDtypeStruct((m, d), BF16)] * n_mix,
        scratch_shapes=[pltpu.VMEM((tm + SUBLANES, d), F32)],
        compiler_params=_params("parallel"),
        name="rwkv_mix",
    )(h, h, gain.reshape(1, d), mu)


WKV_LANE_GROUP = LANES // 4
WKV_ROWS_PER_BATCH = D_MODEL // LANES


def _wkv_fused_kernel(r_ref, k_ref, v_ref, w_ref, a_ref, g_ref, kkp_ref, kap_ref, rkp_ref, lnw_ref, lnb_ref,
                      o_ref, s_ref, wd_s, kh_s, kk_s, b_s, r_s, v_s, g_s, *, steps):
    n = RWKV_HEAD_DIM
    nb = LANES // WKV_LANE_GROUP
    halves = WKV_ROWS_PER_BATCH // SUBLANES
    ahead = min(WKV_LOOKAHEAD, steps)

    @pl.when(pl.program_id(0) == 0)
    def _():
        s_ref[...] = jnp.zeros_like(s_ref)

    def lane_group(cnt):
        return lax.broadcasted_iota(I32, (cnt, SUBLANES, LANES), 2) // WKV_LANE_GROUP

    def to_scan(x_ref, t0, cnt):
        grp = lane_group(cnt)
        rows = [x_ref[b, pl.ds(t0, cnt)].astype(F32) for b in range(nb)]
        pieces = []
        for n4 in range(nb):
            for v in range(halves):
                acc = None
                for b in range(nb):
                    x = rows[b][:, SUBLANES * v:SUBLANES * (v + 1), :]
                    shift = (WKV_LANE_GROUP * (b - n4)) % LANES
                    if shift:
                        x = pltpu.roll(x, shift, axis=2)
                    acc = x if acc is None else jnp.where(grp == b, x, acc)
                pieces.append(acc)
        return pieces

    def prepare(t0, cnt):
        rs, ks, vs, ws, as_, gs = (to_scan(x, t0, cnt) for x in (r_ref, k_ref, v_ref, w_ref, a_ref, g_ref))
        kks = [ks[u] * kkp_ref[SUBLANES * u:SUBLANES * (u + 1), :] for u in range(len(ks))]
        ss = kks[0] * kks[0]
        for u in range(1, len(kks)):
            ss = ss + kks[u] * kks[u]
        inv_norm = 1.0 / jnp.maximum(jnp.sqrt(jnp.sum(ss, axis=1, keepdims=True)), 1e-12)
        for u in range(len(ks)):
            rows = slice(SUBLANES * u, SUBLANES * (u + 1))
            at = (pl.ds(t0, cnt), rows, slice(None))
            kk = kks[u] * inv_norm
            wd_s[at] = jnp.exp(-jnp.exp(-_softplus(-ws[u]) - 0.5))
            kk_s[at] = kk
            b_s[at] = kk * as_[u]
            kh_s[at] = ks[u] * (1.0 + (as_[u] - 1.0) * kap_ref[rows, :])
            r_s[at] = rs[u]
            v_s[at] = vs[u]
            g_s[at] = gs[u]

    def finish(t, y):
        inv_n = 1.0 / n
        mean = jnp.sum(y, axis=0, keepdims=True) * inv_n
        yc = y - mean
        var = jnp.sum(yc * yc, axis=0, keepdims=True) * inv_n
        bonus = jnp.sum(r_s[t] * kh_s[t] * rkp_ref[...], axis=0, keepdims=True) * v_s[t]
        o = (yc * lax.rsqrt(var + RWKV_GN_EPS) * lnw_ref[...] + lnb_ref[...] + bonus) * g_s[t]
        grp = lane_group(1)[0]
        for b in range(nb):
            parts = []
            for v in range(halves):
                acc = None
                for n4 in range(nb):
                    u = n4 * halves + v
                    x = o[SUBLANES * u:SUBLANES * (u + 1), :]
                    shift = (WKV_LANE_GROUP * (n4 - b)) % LANES
                    if shift:
                        x = pltpu.roll(x, shift, axis=1)
                    acc = x if acc is None else jnp.where(grp == n4, x, acc)
                parts.append(acc)
            o_ref[b, pl.ds(t, 1)] = jnp.concatenate(parts, axis=0).astype(o_ref.dtype)[None]

    prepare(0, ahead)
    q0 = jnp.zeros((n, LANES), F32)
    for j in range(n):
        q0 = q0 + s_ref[j] * kk_s[0, j:j + 1, :]

    def step(t, q):
        tn = jnp.minimum(t + 1, steps - 1)
        v = v_s[t]
        y = jnp.zeros((n, LANES), F32)
        qn = jnp.zeros((n, LANES), F32)
        for j in range(n):
            s_new = s_ref[j] * wd_s[t, j:j + 1, :] - q * b_s[t, j:j + 1, :] + v * kh_s[t, j:j + 1, :]
            s_ref[j] = s_new
            y = y + s_new * r_s[t, j:j + 1, :]
            qn = qn + s_new * kk_s[tn, j:j + 1, :]
        finish(t, y)
        prepare(jnp.minimum(t + ahead, steps - 1), 1)
        return qn

    lax.fori_loop(0, steps, step, q0)


def _wkv_column_order():
    c, n4, h = np.meshgrid(np.arange(WKV_ROWS_PER_BATCH), np.arange(LANES // WKV_LANE_GROUP), np.arange(RWKV_HEADS),
                           indexing="ij")
    return (h * RWKV_HEAD_DIM + n4 * WKV_ROWS_PER_BATCH + c).reshape(-1)


def _scan_tile(p, bsz):
    return jnp.tile(p.reshape(RWKV_HEADS, RWKV_HEAD_DIM).T, (1, bsz))


def wkv_fused(r, k, v, w_raw, a_sig, g, k_k, k_a, r_k, ln_w, ln_b, bsz, seq):
    assert bsz * RWKV_HEADS == LANES and RWKV_HEADS == WKV_LANE_GROUP
    steps = min(WKV_T, seq)
    view = lambda x: x.reshape(bsz, seq, WKV_ROWS_PER_BATCH, LANES)
    blk = pl.BlockSpec((bsz, steps, WKV_ROWS_PER_BATCH, LANES), lambda i: (0, i, 0, 0))
    par = pl.BlockSpec((RWKV_HEAD_DIM, LANES), lambda i: (0, 0))
    scr = pltpu.VMEM((steps, RWKV_HEAD_DIM, LANES), F32)
    out = pl.pallas_call(
        functools.partial(_wkv_fused_kernel, steps=steps),
        grid=(seq // steps,),
        in_specs=[blk] * 6 + [par] * 5,
        out_specs=blk,
        out_shape=jax.ShapeDtypeStruct((bsz, seq, WKV_ROWS_PER_BATCH, LANES), BF16),
        scratch_shapes=[pltpu.VMEM((RWKV_HEAD_DIM, RWKV_HEAD_DIM, LANES), F32)] + [scr] * 7,
        compiler_params=_params("arbitrary"),
        name="wkv_fused",
    )(view(r), view(k), view(v), view(w_raw), view(a_sig), view(g),
      _scan_tile(k_k, bsz), _scan_tile(k_a, bsz), _scan_tile(r_k, bsz), _scan_tile(ln_w, bsz), _scan_tile(ln_b, bsz))
    return out.reshape(bsz * seq, D_MODEL)


def _router_kernel(h_ref, g_ref, wr_ref, info_ref, tile_cnt_ref, total_ref, carry_ref, *, tt):
    E = N_EXPERTS
    i = pl.program_id(0)

    @pl.when(i == 0)
    def _():
        carry_ref[...] = jnp.zeros_like(carry_ref)

    xh, xm, xl = _split3(_rms(h_ref[...]) * g_ref[...])
    wh, wm, wl = _split3(wr_ref[...])
    logits = (_dot_nt(wh, xh) + _dot_nt(wh, xm) + _dot_nt(wm, xh)
              + _dot_nt(wh, xl) + _dot_nt(wl, xh) + _dot_nt(wm, xm))
    eidx = lax.broadcasted_iota(I32, (E, tt), 0)
    v1 = jnp.max(logits, axis=0, keepdims=True)
    i1 = jnp.min(jnp.where(logits == v1, eidx, E), axis=0, keepdims=True)
    rest = jnp.where(eidx == i1, -jnp.inf, logits)
    v2 = jnp.max(rest, axis=0, keepdims=True)
    i2 = jnp.min(jnp.where(rest == v2, eidx, E), axis=0, keepdims=True)
    e2 = jnp.exp(v2 - v1)
    g1 = 1.0 / (1.0 + e2)
    g2 = e2 / (1.0 + e2)
    hit1 = eidx == i1
    hit2 = eidx == i2
    onehot = jnp.where(hit1 | hit2, 1.0, 0.0)
    rr = lax.broadcasted_iota(I32, (tt, tt), 0)
    cc = lax.broadcasted_iota(I32, (tt, tt), 1)
    before = jnp.where(rr < cc, 1.0, 0.0).astype(BF16)
    carry = carry_ref[...]
    carry_wide = jnp.concatenate([carry] * (tt // LANES), axis=1)
    cum = _dot(onehot.astype(BF16), before) + carry_wide
    rank1 = jnp.sum(jnp.where(hit1, cum, 0.0), axis=0, keepdims=True)
    rank2 = jnp.sum(jnp.where(hit2, cum, 0.0), axis=0, keepdims=True)
    zero = jnp.zeros((1, tt), F32)
    info_ref[...] = jnp.concatenate([i1.astype(F32), i2.astype(F32), rank1, rank2, g1, g2, zero, zero], axis=0)
    tile_cnt_ref[0] = carry
    carry = carry + _dot(onehot.astype(BF16), jnp.ones((tt, LANES), BF16))
    carry_ref[...] = carry
    total_ref[...] = carry


def moe_router(h, norm_gain, w_router):
    m, d = h.shape
    tt = min(MOE_TILE, m)
    nt = m // tt
    return pl.pallas_call(
        functools.partial(_router_kernel, tt=tt),
        grid=(nt,),
        in_specs=[pl.BlockSpec((tt, d), lambda i: (i, 0)), pl.BlockSpec((1, d), lambda i: (0, 0)),
                  pl.BlockSpec((N_EXPERTS, d), lambda i: (0, 0))],
        out_specs=[pl.BlockSpec((SUBLANES, tt), lambda i: (0, i)),
                   pl.BlockSpec((1, N_EXPERTS, LANES), lambda i: (i, 0, 0)),
                   pl.BlockSpec((N_EXPERTS, LANES), lambda i: (0, 0))],
        out_shape=[jax.ShapeDtypeStruct((SUBLANES, m), F32), jax.ShapeDtypeStruct((nt, N_EXPERTS, LANES), F32),
                   jax.ShapeDtypeStruct((N_EXPERTS, LANES), F32)],
        scratch_shapes=[pltpu.VMEM((N_EXPERTS, LANES), F32)],
        compiler_params=_params("arbitrary"),
        name="moe_router",
    )(h, norm_gain.reshape(1, d), w_router.T)


def _slots_kernel(start_ref, info_ref, o_ref):
    info = info_ref[...]
    e1, e2, r1, r2 = info[0:1], info[1:2], info[2:3], info[3:4]
    s1 = r1
    s2 = r2
    for e in range(N_EXPERTS):
        st = start_ref[e].astype(F32)
        s1 = s1 + jnp.where(e1 == float(e), st, 0.0)
        s2 = s2 + jnp.where(e2 == float(e), st, 0.0)
    o_ref[...] = jnp.concatenate([s1, s2, info[4:5], info[5:6], info[6:8], info[6:8]], axis=0)


def moe_slots(info, slot_start):
    m = info.shape[1]
    tt = min(2048, m)
    return pl.pallas_call(
        _slots_kernel,
        grid_spec=pltpu.PrefetchScalarGridSpec(
            num_scalar_prefetch=1, grid=(m // tt,),
            in_specs=[pl.BlockSpec((SUBLANES, tt), lambda i, st: (0, i))],
            out_specs=pl.BlockSpec((SUBLANES, tt), lambda i, st: (0, i))),
        out_shape=jax.ShapeDtypeStruct((SUBLANES, m), F32),
        compiler_params=_params("parallel"),
        name="moe_slots",
    )(slot_start, info)


def _dispatch_kernel(blk_ref, til_ref, flg_ref, slots_ref, x_ref, o_ref, go_ref, *, bs):
    w = pl.program_id(0)
    flags = flg_ref[w]
    first = (flags & 1) != 0
    real = (flags & 4) != 0

    def gathered():
        sl = slots_ref[...]
        sid = (blk_ref[w] * bs + lax.broadcasted_iota(I32, (bs, sl.shape[1]), 0)).astype(F32)
        hit1 = sid == sl[0:1, :]
        hit2 = sid == sl[1:2, :]
        onehot = jnp.where(hit1 | hit2, 1.0, 0.0).astype(BF16)
        gate = jnp.sum(jnp.where(hit1, sl[2:3, :], 0.0) + jnp.where(hit2, sl[3:4, :], 0.0), axis=1, keepdims=True)
        return _dot(onehot, x_ref[...]).astype(o_ref.dtype), jnp.broadcast_to(gate, go_ref.shape)

    @pl.when(first & real)
    def _():
        o_ref[...], go_ref[...] = gathered()

    @pl.when(first & jnp.logical_not(real))
    def _():
        o_ref[...] = jnp.zeros_like(o_ref)
        go_ref[...] = jnp.zeros_like(go_ref)

    @pl.when(jnp.logical_not(first) & real)
    def _():
        rows, gate = gathered()
        o_ref[...] += rows
        go_ref[...] += gate


def moe_dispatch(xn, slots, wl_blk, wl_til, wl_flg, n_slots):
    m, d = xn.shape
    bs = tt = MOE_TILE
    n_items = wl_blk.shape[0]
    return pl.pallas_call(
        functools.partial(_dispatch_kernel, bs=bs),
        grid_spec=pltpu.PrefetchScalarGridSpec(
            num_scalar_prefetch=3, grid=(n_items,),
            in_specs=[pl.BlockSpec((SUBLANES, tt), lambda w, b, t, f: (0, t[w])),
                      pl.BlockSpec((tt, d), lambda w, b, t, f: (t[w], 0))],
            out_specs=[pl.BlockSpec((bs, d), lambda w, b, t, f: (b[w], 0)),
                       pl.BlockSpec((bs, LANES), lambda w, b, t, f: (b[w], 0))]),
        out_shape=[jax.ShapeDtypeStruct((n_slots, d), BF16), jax.ShapeDtypeStruct((n_slots, LANES), F32)],
        compiler_params=_params("arbitrary"),
        name="moe_dispatch",
    )(wl_blk, wl_til, wl_flg, slots, xn)


def _moe_ffn_kernel(be_ref, rows_ref, x_ref, gate_ref, w1_ref, w3_ref, w2_ref, o_ref, acc_ref, *, sub):
    b = pl.program_id(0)
    f = pl.program_id(1)
    bm = x_ref.shape[0]

    @pl.when(f == 0)
    def _():
        acc_ref[...] = jnp.zeros_like(acc_ref)

    for rows in range(sub, bm + 1, sub):
        @pl.when(rows_ref[b] == rows)
        def _(rows=rows):
            x = x_ref[0:rows, :]
            h = _silu(_dot(x, w1_ref[0].astype(BF16))) * _dot(x, w3_ref[0].astype(BF16))
            acc_ref[0:rows, :] += _dot(h.astype(BF16), w2_ref[0].astype(BF16))

    @pl.when(f == pl.num_programs(1) - 1)
    def _():
        o_ref[...] = (acc_ref[...] * gate_ref[:, 0:1]).astype(o_ref.dtype)


def moe_ffn(xs, slot_gate, w1, w3, w2, block_expert, block_rows):
    n_slots, d = xs.shape
    ff = w1.shape[2]
    bm = MOE_FFN_ROWS
    tf = MOE_TF
    nf = ff // tf

    def f_of(b, f, rows):
        return jnp.where(rows[b] > 0, f, nf - 1)

    return pl.pallas_call(
        functools.partial(_moe_ffn_kernel, sub=MOE_TILE),
        grid_spec=pltpu.PrefetchScalarGridSpec(
            num_scalar_prefetch=2, grid=(n_slots // bm, nf),
            in_specs=[pl.BlockSpec((bm, d), lambda b, f, be, nb: (b, 0)),
                      pl.BlockSpec((bm, LANES), lambda b, f, be, nb: (b, 0)),
                      pl.BlockSpec((1, d, tf), lambda b, f, be, nb: (be[b], 0, f_of(b, f, nb))),
                      pl.BlockSpec((1, d, tf), lambda b, f, be, nb: (be[b], 0, f_of(b, f, nb))),
                      pl.BlockSpec((1, tf, d), lambda b, f, be, nb: (be[b], f_of(b, f, nb), 0))],
            out_specs=pl.BlockSpec((bm, d), lambda b, f, be, nb: (b, 0)),
            scratch_shapes=[pltpu.VMEM((bm, d), F32)]),
        out_shape=jax.ShapeDtypeStruct((n_slots, d), BF16),
        compiler_params=_params("arbitrary", "arbitrary"),
        name="moe_ffn",
    )(block_expert, block_rows, xs, slot_gate, w1, w3, w2)


def _combine_kernel(til_ref, blk_ref, flg_ref, tok_ref, y_ref, res_ref, o_ref, *, bs):
    w = pl.program_id(0)
    flags = flg_ref[w]

    @pl.when((flags & 1) != 0)
    def _():
        o_ref[...] = res_ref[...]

    @pl.when((flags & 4) != 0)
    def _():
        tok = tok_ref[...]
        tt = tok.shape[0]
        sid = (blk_ref[w] * bs + lax.broadcasted_iota(I32, (tt, bs), 1)).astype(F32)
        onehot = jnp.where((sid == tok[:, 0:1]) | (sid == tok[:, 1:2]), 1.0, 0.0).astype(BF16)
        o_ref[...] += _dot(onehot, y_ref[...])


def moe_combine(y, tok_info, residual, wl_til, wl_blk, wl_flg):
    m, d = residual.shape
    bs = tt = MOE_TILE
    n_items = wl_til.shape[0]
    return pl.pallas_call(
        functools.partial(_combine_kernel, bs=bs),
        grid_spec=pltpu.PrefetchScalarGridSpec(
            num_scalar_prefetch=3, grid=(n_items,),
            in_specs=[pl.BlockSpec((tt, SUBLANES), lambda w, t, b, f: (t[w], 0)),
                      pl.BlockSpec((bs, d), lambda w, t, b, f: (b[w], 0)),
                      pl.BlockSpec((tt, d), lambda w, t, b, f: (t[w], 0))],
            out_specs=pl.BlockSpec((tt, d), lambda w, t, b, f: (t[w], 0))),
        out_shape=jax.ShapeDtypeStruct((m, d), F32),
        compiler_params=_params("arbitrary"),
        name="moe_combine",
    )(wl_til, wl_blk, wl_flg, tok_info, y, residual)


def _work_list(overlap, n_items):
    n_major, n_minor = overlap.shape
    has = jnp.any(overlap, axis=1)
    marked = jnp.concatenate([overlap[:, :1] | ~has[:, None], overlap[:, 1:]], axis=1)
    maj, mino = jnp.nonzero(marked, size=n_items, fill_value=-1)
    valid = maj >= 0
    real = valid & overlap[jnp.maximum(maj, 0), jnp.maximum(mino, 0)]
    prev = jnp.concatenate([jnp.full((1,), -2, maj.dtype), maj[:-1]])
    nxt = jnp.concatenate([maj[1:], jnp.full((1,), -2, maj.dtype)])
    flags = (valid & (maj != prev)) * 1 + (valid & (maj != nxt)) * 2 + real * 4
    last_maj = jnp.max(jnp.where(valid, maj, 0))
    last_min = jnp.sum(jnp.where(valid & (maj == last_maj) & (maj != nxt), mino, 0))
    maj = jnp.where(valid, maj, last_maj)
    mino = jnp.where(valid, mino, last_min)
    return maj.astype(I32), mino.astype(I32), flags.astype(I32)


def moe_layer(h, norm_gain, w_router, w1, w3, w2):
    m, d = h.shape
    E = N_EXPERTS
    tile = min(MOE_TILE, m)
    n_tiles = m // tile
    ffn_rows = MOE_FFN_ROWS
    xn = rmsnorm(h, norm_gain, BF16)
    info, tile_cnt, total = moe_router(h, norm_gain, w_router)
    before = tile_cnt[:, :, 0]
    counts = total[:, 0].astype(I32)
    padded = (counts + ffn_rows - 1) // ffn_rows * ffn_rows
    ends = jnp.cumsum(padded)
    starts = ends - padded
    n_slots = (m * TOP_K // ffn_rows + E) * ffn_rows
    n_ffn_blocks = n_slots // ffn_rows
    n_dblocks = n_slots // tile
    slots = moe_slots(info, starts.astype(I32))
    db_start = jnp.arange(n_dblocks, dtype=I32) * tile
    db_e = jnp.minimum(jnp.sum(db_start[:, None] >= ends[None, :], axis=1), E - 1)
    r0 = db_start - starts[db_e]
    r1 = jnp.minimum(r0 + tile, counts[db_e])
    after = jnp.concatenate([before[1:], counts[None, :].astype(F32)], axis=0).astype(I32)
    bef = before.astype(I32)
    ov = (bef[:, db_e].T < r1[:, None]) & (after[:, db_e].T > r0[:, None]) & (db_start < ends[-1])[:, None]
    n_items = n_dblocks + E * n_tiles
    d_blk, d_til, d_flg = _work_list(ov, n_items)
    c_til, c_blk, c_flg = _work_list(ov.T, n_items)
    xs, slot_gate = moe_dispatch(xn, slots, d_blk, d_til, d_flg, n_slots)
    fb_start = jnp.arange(n_ffn_blocks, dtype=I32) * ffn_rows
    block_expert = jnp.minimum(jnp.sum(fb_start[:, None] >= ends[None, :], axis=1), E - 1).astype(I32)
    filled = counts[block_expert] - (fb_start - starts[block_expert])
    block_rows = jnp.where(fb_start < ends[-1], (jnp.clip(filled, 0, ffn_rows) + tile - 1) // tile * tile, 0).astype(I32)
    ys = moe_ffn(xs, slot_gate, w1, w3, w2, block_expert, block_rows)
    return moe_combine(ys, slots.T, h, c_til, c_blk, c_flg)


def even_mixer_layer(h, posf, p, bsz, seq):
    m = bsz * seq
    hn = rmsnorm(h, p["norm_mix"], BF16)
    w_in = p["w_in"]
    c = np.cumsum([0, D_SSM, D_CONV, SSM_HEADS, D_NSA] + [D_NSA_KV] * 6 + [3 * NSA_HEADS])
    z = matmul([(hn, w_in[:, c[0]:c[1]].astype(BF16))], out_dtype=BF16)
    xbc = matmul([(hn, w_in[:, c[1]:c[2]].astype(BF16))], out_dtype=BF16)
    q = matmul([(hn, w_in[:, c[3]:c[4]].astype(BF16))], out_dtype=BF16)
    kv6 = matmul([(hn, w_in[:, c[4]:c[10]].astype(BF16))], out_dtype=BF16)
    w_small = jnp.concatenate([w_in[:, c[2]:c[3]], w_in[:, c[10]:c[11]]], axis=1)
    w_small = jnp.pad(w_small, ((0, 0), (0, LANES - w_small.shape[1]))).astype(BF16)
    small = matmul([(hn, w_small)])
    y_ssm = ssd_group(z, xbc, small, p["conv_w"], p["conv_b"], p["dt_bias"], p["a_log"], p["d_skip"],
                      p["ssm_norm"], bsz, seq)
    qn, ksn, kwn = rope_norm(posf, q, kv6, p["q_gain"], p["ks_gain"], p["kw_gain"])
    nch = seq // CMP_STRIDE

    def chunks(col0):
        u = kv6[:, col0:col0 + D_NSA_KV].reshape(bsz, nch, CMP_STRIDE, NSA_KV_HEADS, NSA_HEAD_DIM)
        return u.transpose(0, 3, 1, 2, 4).reshape(bsz, NSA_KV_HEADS, nch, CMP_STRIDE * NSA_HEAD_DIM)

    kc, vc = compress(chunks(0), chunks(D_NSA_KV), p["pe_k"], p["pe_v"], p["wk1"], p["wk2"], p["wv1"], p["wv2"],
                      p["kc_gain"])
    o_cmp, selt = cmp_select(qn, kc, vc, bsz, seq)
    tk = min(ATT_TK, seq)

    def v_tiles(col0):
        u = kv6[:, col0:col0 + D_NSA_KV].astype(BF16).reshape(bsz, seq // tk, tk, NSA_KV_HEADS, NSA_HEAD_DIM)
        return u.transpose(0, 3, 1, 4, 2)

    o_sel = masked_attention(qn, ksn, 0, v_tiles(3 * D_NSA_KV), bsz, seq, "sel", selt)
    o_win = masked_attention(qn, kwn, 0, v_tiles(5 * D_NSA_KV), bsz, seq, "win")
    y_nsa = nsa_combine(small, o_cmp, o_sel, o_win)
    w_out = p["w_out"].astype(BF16)
    return matmul([(y_ssm, w_out[:D_SSM]), (y_nsa, w_out[D_SSM:])], residual=h)


def rwkv_layer(h, p, bsz, seq):
    m, d = h.shape
    mixes = rwkv_mix(h, p["norm_mix"], p["mu"], seq)
    bf = lambda w: w.astype(BF16)

    def lora_pad(w_a, w_b):
        r = w_a.shape[1]
        rp = -(-r // LANES) * LANES
        return bf(jnp.pad(w_a, ((0, 0), (0, rp - r)))), bf(jnp.pad(w_b, ((0, rp - r), (0, 0))))

    perm = _wkv_column_order()
    r = matmul([(mixes[0], bf(p["w_r"][:, perm]))], out_dtype=BF16)
    k = matmul([(mixes[2], bf(p["w_k"][:, perm]))], out_dtype=BF16)
    v = matmul([(mixes[3], bf(p["w_v"][:, perm]))], out_dtype=BF16)
    w1, w2 = lora_pad(p["w1"], p["w2"][:, perm])
    w_raw = matmul([(matmul([(mixes[1], w1)], act="tanh", out_dtype=BF16), w2)], bias=p["w0"][perm])
    a1, a2 = lora_pad(p["a1"], p["a2"][:, perm])
    a_sig = matmul([(matmul([(mixes[4], a1)], out_dtype=BF16), a2)], bias=p["a0"][perm], act="sigmoid", out_dtype=BF16)
    g1, g2 = lora_pad(p["g1"], p["g2"][:, perm])
    g = matmul([(matmul([(mixes[5], g1)], act="sigmoid", out_dtype=BF16), g2)], out_dtype=BF16)
    o = wkv_fused(r, k, v, w_raw, a_sig, g, p["k_k"], p["k_a"], p["r_k"].reshape(-1), p["ln_w"], p["ln_b"], bsz, seq)
    return matmul([(o, bf(p["w_o"][perm, :]))], residual=h)


def kernel(x, mem, positions, norm_mix, norm_xattn, norm_mem, norm_ffn, xattn_wq, xattn_wkv, xattn_wo, xattn_q_gain, xattn_k_gain, ev_w_in, ev_conv_w, ev_conv_b, ev_dt_bias, ev_a_log, ev_d_skip, ev_ssm_norm, ev_q_gain, ev_kc_gain, ev_ks_gain, ev_kw_gain, ev_pe_k, ev_pe_v, ev_cmp_wk1, ev_cmp_wk2, ev_cmp_wv1, ev_cmp_wv2, ev_w_out, ev_ffn_w1, ev_ffn_w3, ev_ffn_w2, od_mu, od_w_r, od_w_k, od_w_v, od_w_o, od_w0, od_w1, od_w2, od_a0, od_a1, od_a2, od_g1, od_g2, od_k_k, od_k_a, od_r_k, od_ln_w, od_ln_b, od_router, od_moe_w1, od_moe_w3, od_moe_w2):
    bsz, seq, d = x.shape
    m = bsz * seq
    depth = norm_mix.shape[0]
    h = x.reshape(m, d)
    mem2d = mem.reshape(-1, d)
    posf = positions.astype(F32).reshape(m, 1)
    for layer in range(depth):
        i = layer // 2
        if layer % 2 == 0:
            p = dict(norm_mix=norm_mix[layer], w_in=ev_w_in[i], conv_w=ev_conv_w[i], conv_b=ev_conv_b[i],
                     dt_bias=ev_dt_bias[i], a_log=ev_a_log[i], d_skip=ev_d_skip[i], ssm_norm=ev_ssm_norm[i],
                     q_gain=ev_q_gain[i], kc_gain=ev_kc_gain[i], ks_gain=ev_ks_gain[i], kw_gain=ev_kw_gain[i],
                     pe_k=ev_pe_k[i], pe_v=ev_pe_v[i], wk1=ev_cmp_wk1[i], wk2=ev_cmp_wk2[i], wv1=ev_cmp_wv1[i],
                     wv2=ev_cmp_wv2[i], w_out=ev_w_out[i])
            h = even_mixer_layer(h, posf, p, bsz, seq)
        else:
            p = dict(norm_mix=norm_mix[layer], mu=od_mu[i], w_r=od_w_r[i], w_k=od_w_k[i], w_v=od_w_v[i], w_o=od_w_o[i],
                     w0=od_w0[i], w1=od_w1[i], w2=od_w2[i], a0=od_a0[i], a1=od_a1[i], a2=od_a2[i], g1=od_g1[i],
                     g2=od_g2[i], k_k=od_k_k[i], k_a=od_k_a[i], r_k=od_r_k[i], ln_w=od_ln_w[i], ln_b=od_ln_b[i])
            h = rwkv_layer(h, p, bsz, seq)
        h = memory_xattn(h, mem2d, norm_xattn[layer], norm_mem[layer], xattn_wq[layer], xattn_wkv[layer],
                         xattn_wo[layer], xattn_q_gain[layer], xattn_k_gain[layer], bsz, seq)
        if layer % 2 == 0:
            hn = rmsnorm(h, norm_ffn[layer], BF16)
            h = swiglu_ffn(hn, ev_ffn_w1[i].astype(BF16), ev_ffn_w3[i].astype(BF16), ev_ffn_w2[i].astype(BF16), h)
        else:
            h = moe_layer(h, norm_ffn[layer], od_router[i], od_moe_w1[i], od_moe_w3[i], od_moe_w2[i])
    return h.reshape(bsz, seq, d)
```

```python
import functools
import math

import numpy as np
import jax
import jax.numpy as jnp
from jax import lax
from jax.experimental import pallas as pl
from jax.experimental.pallas import tpu as pltpu

F32 = jnp.float32
BF16 = jnp.bfloat16
I32 = jnp.int32

D_MODEL = 2048
NORM_EPS = 1e-6
NEG_BIG = -1e30

SSM_HEADS = 32
SSM_HEAD_DIM = 64
D_SSM = SSM_HEADS * SSM_HEAD_DIM
SSM_GROUPS = 4
D_STATE = 128
CONV_WIDTH = 4
SSD_CHUNK = 128
D_CONV = D_SSM + 2 * SSM_GROUPS * D_STATE
D_SSM_GROUP = D_SSM // SSM_GROUPS

NSA_HEADS = 16
NSA_KV_HEADS = 4
NSA_HEAD_DIM = 128
NSA_Q_PER_KV = NSA_HEADS // NSA_KV_HEADS
D_NSA = NSA_HEADS * NSA_HEAD_DIM
D_NSA_KV = NSA_KV_HEADS * NSA_HEAD_DIM
CMP_BLOCK = 32
CMP_STRIDE = 16
CMP_HIDDEN = 256
SEL_BLOCK = 64
N_SELECT = 16
WINDOW = 512
ROPE_DIM = NSA_HEAD_DIM // 4
ROPE_THETA = 500000.0

RWKV_HEAD_DIM = 64
RWKV_HEADS = D_MODEL // RWKV_HEAD_DIM
RWKV_GN_EPS = 1e-5 * RWKV_HEAD_DIM

XATTN_HEADS = 4
XATTN_HEAD_DIM = 128
D_XATTN = XATTN_HEADS * XATTN_HEAD_DIM

N_EXPERTS = 8
TOP_K = 2

LANES = 128
SUBLANES = 8
VMEM_LIMIT_BYTES = 56 * 1024 * 1024

ROW_TILE = 512
MM_TM = 2048
MM_TN = 512
MM_VMEM_BUDGET = 44 * 1024 * 1024
FFN_TM = 512
FFN_TF = 512
FFN_SPLIT = 256
MOE_FFN_ROWS = 1024
MOE_TF = 256
COMBINE_TM = 256
MOE_TILE = 512
ATT_TQ = 256
ATT_TK = 512
LOG2E = 1.4426950408889634
CMP_TQ = 256
WKV_T = 64
WKV_LOOKAHEAD = 8


def _params(*sem):
    return pltpu.CompilerParams(dimension_semantics=sem, vmem_limit_bytes=VMEM_LIMIT_BYTES)


def _sigmoid(x):
    return 1.0 / (1.0 + jnp.exp(-x))


def _silu(x):
    return x * _sigmoid(x)


def _softplus(x):
    return jnp.maximum(x, 0.0) + jnp.log(1.0 + jnp.exp(-jnp.abs(x)))


def _split3(x):
    h = x.astype(BF16)
    r = x - h.astype(F32)
    m = r.astype(BF16)
    l = (r - m.astype(F32)).astype(BF16)
    return h, m, l


def _dot(a, b):
    return jnp.dot(a, b, preferred_element_type=F32)


def _dot_nt(a, b):
    return lax.dot_general(a, b, (((1,), (1,)), ((), ())), preferred_element_type=F32)


def _dot_f32_sel(x, e):
    h, m, l = _split3(x)
    return _dot(h, e) + _dot(m, e) + _dot(l, e)


def _sel_dot_f32(e, x):
    h, m, l = _split3(x)
    return _dot(e, h) + _dot(e, m) + _dot(e, l)


def _rms(x, eps=NORM_EPS):
    return x * lax.rsqrt(jnp.mean(x * x, axis=-1, keepdims=True) + eps)


def _rmsnorm_kernel(x_ref, g_ref, o_ref):
    o_ref[...] = (_rms(x_ref[...]) * g_ref[...]).astype(o_ref.dtype)


def rmsnorm(x, gain, out_dtype):
    m, d = x.shape
    tm = min(ROW_TILE, m)
    return pl.pallas_call(
        _rmsnorm_kernel,
        grid=(m // tm,),
        in_specs=[pl.BlockSpec((tm, d), lambda i: (i, 0)), pl.BlockSpec((1, d), lambda i: (0, 0))],
        out_specs=pl.BlockSpec((tm, d), lambda i: (i, 0)),
        out_shape=jax.ShapeDtypeStruct((m, d), out_dtype),
        compiler_params=_params("parallel"),
        name="rmsnorm",
    )(x, gain.reshape(1, d))


def _mm_kernel(*refs, n_pairs, has_bias, has_res, act):
    o_ref = refs[-1]
    acc = None
    for p in range(n_pairs):
        a = refs[2 * p][...].astype(BF16)
        d = _dot(a, refs[2 * p + 1][...])
        acc = d if acc is None else acc + d
    k = 2 * n_pairs
    if has_bias:
        acc = acc + refs[k][...]
        k += 1
    if act == "tanh":
        acc = jnp.tanh(acc)
    elif act == "sigmoid":
        acc = _sigmoid(acc)
    if has_res:
        acc = acc + refs[k][...]
    o_ref[...] = acc.astype(o_ref.dtype)


def matmul(pairs, bias=None, residual=None, act=None, out_dtype=F32):
    m = pairs[0][0].shape[0]
    n = pairs[0][1].shape[1]
    k_total = sum(a.shape[1] for a, _ in pairs)
    tm = min(MM_TM if k_total <= D_MODEL else MM_TM // 2, m)
    out_bytes = jnp.dtype(out_dtype).itemsize + (4 if residual is not None else 0)

    def vmem_bytes(tn):
        return 2 * (tm * k_total * 2 + k_total * tn * 2 + tm * tn * out_bytes)

    tn = next((t for t in (2 * MM_TN, MM_TN, 256, LANES) if n % t == 0 and vmem_bytes(t) <= MM_VMEM_BUDGET), LANES)
    in_specs, args = [], []
    for a, w in pairs:
        kk = a.shape[1]
        in_specs += [pl.BlockSpec((tm, kk), lambda i, j: (i, 0)), pl.BlockSpec((kk, tn), lambda i, j: (0, j))]
        args += [a, w]
    if bias is not None:
        in_specs.append(pl.BlockSpec((1, tn), lambda i, j: (0, j)))
        args.append(bias.reshape(1, n).astype(F32))
    if residual is not None:
        in_specs.append(pl.BlockSpec((tm, tn), lambda i, j: (i, j)))
        args.append(residual)
    kern = functools.partial(_mm_kernel, n_pairs=len(pairs), has_bias=bias is not None,
                             has_res=residual is not None, act=act)
    return pl.pallas_call(
        kern,
        grid=(m // tm, n // tn),
        in_specs=in_specs,
        out_specs=pl.BlockSpec((tm, tn), lambda i, j: (i, j)),
        out_shape=jax.ShapeDtypeStruct((m, n), out_dtype),
        compiler_params=_params("parallel", "parallel"),
        name="matmul",
    )(*args)


def _ffn_kernel(x_ref, w1_ref, w3_ref, w2_ref, res_ref, o_ref):
    f = pl.program_id(1)

    @pl.when(f == 0)
    def _():
        o_ref[...] = res_ref[...]

    x = x_ref[...]
    tf = w1_ref.shape[1]
    cols = [slice(c, c + FFN_SPLIT) for c in range(0, tf, FFN_SPLIT)]
    gate_up = [(_dot(x, w1_ref[:, c]), _dot(x, w3_ref[:, c])) for c in cols]
    hs = [(_silu(g) * u).astype(BF16) for g, u in gate_up]
    acc = _dot(hs[0], w2_ref[cols[0], :])
    for h, c in zip(hs[1:], cols[1:]):
        acc = acc + _dot(h, w2_ref[c, :])
    o_ref[...] += acc


def swiglu_ffn(x, w1, w3, w2, residual):
    m, d = x.shape
    ff = w1.shape[1]
    tm = min(FFN_TM, m)
    tf = FFN_TF
    return pl.pallas_call(
        _ffn_kernel,
        grid=(m // tm, ff // tf),
        in_specs=[pl.BlockSpec((tm, d), lambda i, f: (i, 0)),
                  pl.BlockSpec((d, tf), lambda i, f: (0, f)),
                  pl.BlockSpec((d, tf), lambda i, f: (0, f)),
                  pl.BlockSpec((tf, d), lambda i, f: (f, 0)),
                  pl.BlockSpec((tm, d), lambda i, f: (i, 0))],
        out_specs=pl.BlockSpec((tm, d), lambda i, f: (i, 0)),
        out_shape=jax.ShapeDtypeStruct((m, d), F32),
        compiler_params=_params("parallel", "arbitrary"),
        name="swiglu_ffn",
    )(x, w1, w3, w2, residual)


def _ssd_kernel(z_ref, xbc_ref, dt_ref, cw_ref, cb_ref, dtb_ref, alog_ref, dskip_ref, nw_ref, eh_ref,
                o_ref, ext_ref, st_ref):
    L = SSD_CHUNK
    G = SSM_GROUPS
    GW = D_SSM_GROUP
    c = pl.program_id(1)

    @pl.when(c == 0)
    def _():
        ext_ref[L:L + SUBLANES, :] = jnp.zeros((SUBLANES, D_CONV), F32)
        st_ref[...] = jnp.zeros_like(st_ref)

    ext_ref[0:SUBLANES, :] = ext_ref[L:L + SUBLANES, :]
    ext_ref[SUBLANES:L + SUBLANES, :] = xbc_ref[...].astype(F32)
    acc = jnp.zeros((L, D_CONV), F32) + cb_ref[...]
    for k in range(CONV_WIDTH):
        off = SUBLANES - (CONV_WIDTH - 1) + k
        acc = acc + cw_ref[k:k + 1, :] * ext_ref[off:off + L, :]
    xc = _silu(acc)

    dt = _softplus(dt_ref[...] + dtb_ref[...])
    a = -jnp.exp(alog_ref[...])
    adt = dt * a
    row = lax.broadcasted_iota(I32, (L, L), 0)
    col = lax.broadcasted_iota(I32, (L, L), 1)
    causal = row >= col
    tril = jnp.where(causal, 1.0, 0.0).astype(BF16)
    a_cum = _sel_dot_f32(tril, adt)
    a_cum_t = a_cum.T
    eh = eh_ref[...]
    acx = _dot_f32_sel(a_cum, eh)
    dtx = _dot_f32_sel(dt, eh)
    a_end = acx[L - 1:L, :]
    xs = xc[:, :D_SSM]
    xdt = xs * dtx
    xdte = (xdt * jnp.exp(a_end - acx)).astype(BF16)
    from_start = jnp.exp(acx)
    lane = lax.broadcasted_iota(I32, (L, LANES), 1)
    lo_half = lane < SSM_HEAD_DIM
    z = z_ref[...].astype(F32)

    for g in range(G):
        bm = xc[:, D_SSM + g * D_STATE:D_SSM + (g + 1) * D_STATE]
        cm = xc[:, D_SSM + G * D_STATE + g * D_STATE:D_SSM + G * D_STATE + (g + 1) * D_STATE]
        bm_b = bm.astype(BF16)
        cm_b = cm.astype(BF16)
        cb = _dot_nt(cm_b, bm_b)
        st = st_ref[g]
        y_off = _dot(cm_b, st.astype(BF16)) * from_start[:, g * GW:(g + 1) * GW]
        pieces = []
        for pp in range(GW // LANES):
            h0 = g * (GW // SSM_HEAD_DIM) + 2 * pp
            ms = []
            for hh in (h0, h0 + 1):
                diff = a_cum[:, hh:hh + 1] - a_cum_t[hh:hh + 1, :]
                dec = jnp.exp(jnp.where(causal, diff, NEG_BIG))
                ms.append((cb * dec).astype(BF16))
            lhs = jnp.concatenate(ms, axis=1)
            c0 = g * GW + pp * LANES
            xp = xdt[:, c0:c0 + LANES]
            rhs = jnp.concatenate([jnp.where(lo_half, xp, 0.0), jnp.where(lo_half, 0.0, xp)], axis=0)
            pieces.append(_dot(lhs, rhs.astype(BF16)))
        y = jnp.concatenate(pieces, axis=1) + y_off
        y = y + xs[:, g * GW:(g + 1) * GW] * dskip_ref[:, g * GW:(g + 1) * GW]
        y = y * _silu(z[:, g * GW:(g + 1) * GW])
        y = _rms(y) * nw_ref[:, g * GW:(g + 1) * GW]
        o_ref[:, g * GW:(g + 1) * GW] = y.astype(o_ref.dtype)
        s_new = _dot(bm.T.astype(BF16), xdte[:, g * GW:(g + 1) * GW])
        st_ref[g] = st * jnp.exp(a_end[:, g * GW:(g + 1) * GW]) + s_new


def ssd_group(proj, z_col, xbc_col, small, conv_w, conv_b, dt_bias, a_log, d_skip, norm_w, bsz, seq):
    m = bsz * seq
    L = SSD_CHUNK
    nc = seq // L
    pad = LANES - SSM_HEADS
    eh = np.zeros((LANES, D_SSM), np.float32)
    for h in range(SSM_HEADS):
        eh[h, h * SSM_HEAD_DIM:(h + 1) * SSM_HEAD_DIM] = 1.0
    row_spec = lambda w, col=0: pl.BlockSpec((L, w), lambda b, c: (b * nc + c, col))
    const = lambda r, w: pl.BlockSpec((r, w), lambda b, c: (0, 0))
    return pl.pallas_call(
        _ssd_kernel,
        grid=(bsz, nc),
        in_specs=[row_spec(D_SSM, z_col), row_spec(D_CONV, xbc_col), row_spec(LANES),
                  const(CONV_WIDTH, D_CONV), const(1, D_CONV), const(1, LANES), const(1, LANES),
                  const(1, D_SSM), const(1, D_SSM), const(LANES, D_SSM)],
        out_specs=row_spec(D_SSM),
        out_shape=jax.ShapeDtypeStruct((m, D_SSM), BF16),
        scratch_shapes=[pltpu.VMEM((L + SUBLANES, D_CONV), F32),
                        pltpu.VMEM((SSM_GROUPS, D_STATE, D_SSM_GROUP), F32)],
        compiler_params=_params("parallel", "arbitrary"),
        name="ssd_group",
    )(proj, proj, small, conv_w.T, conv_b.reshape(1, D_CONV),
      jnp.pad(dt_bias, (0, pad)).reshape(1, LANES), jnp.pad(a_log, (0, pad)).reshape(1, LANES),
      jnp.repeat(d_skip, SSM_HEAD_DIM).reshape(1, D_SSM), norm_w.reshape(1, D_SSM), jnp.asarray(eh, BF16))


def _rope_norm_kernel(pos_ref, invf_ref, q_ref, kv_ref, qg_ref, ksg_ref, kwg_ref, qo_ref, kso_ref, kwo_ref):
    ang = pos_ref[...] * invf_ref[...]
    cos = jnp.cos(ang)
    sin = jnp.sin(ang)
    lane = lax.broadcasted_iota(I32, ang.shape, 1)
    first = lane < ROPE_DIM // 2
    sin_signed = jnp.where(first, -sin, sin)

    def head(x, gain):
        y = _rms(x.astype(F32)) * gain
        partner = jnp.where(first, pltpu.roll(y, LANES - ROPE_DIM // 2, axis=1), pltpu.roll(y, ROPE_DIM // 2, axis=1))
        return y * cos + partner * sin_signed

    hd = NSA_HEAD_DIM
    scale = hd ** -0.5 * LOG2E
    for h in range(NSA_HEADS):
        qo_ref[:, h * hd:(h + 1) * hd] = (head(q_ref[:, h * hd:(h + 1) * hd], qg_ref[...]) * scale).astype(qo_ref.dtype)
    for h in range(NSA_KV_HEADS):
        ks = kv_ref[:, 2 * D_NSA_KV + h * hd:2 * D_NSA_KV + (h + 1) * hd]
        kw = kv_ref[:, 4 * D_NSA_KV + h * hd:4 * D_NSA_KV + (h + 1) * hd]
        kso_ref[:, h * hd:(h + 1) * hd] = head(ks, ksg_ref[...]).astype(kso_ref.dtype)
        kwo_ref[:, h * hd:(h + 1) * hd] = head(kw, kwg_ref[...]).astype(kwo_ref.dtype)


def rope_norm(posf, proj, q_col, kv_col, q_gain, ks_gain, kw_gain):
    m = proj.shape[0]
    tm = min(ROW_TILE, m)
    half = ROPE_DIM // 2
    inv = np.exp(-math.log(ROPE_THETA) * np.arange(0, ROPE_DIM, 2, dtype=np.float32) / ROPE_DIM).astype(np.float32)
    invf = np.zeros((1, LANES), np.float32)
    invf[0, :half] = inv
    invf[0, half:ROPE_DIM] = inv
    row = lambda w, col=0: pl.BlockSpec((tm, w), lambda i: (i, col))
    const = pl.BlockSpec((1, LANES), lambda i: (0, 0))
    return pl.pallas_call(
        _rope_norm_kernel,
        grid=(m // tm,),
        in_specs=[row(1), const, row(D_NSA, q_col), row(6 * D_NSA_KV, kv_col), const, const, const],
        out_specs=[row(D_NSA), row(D_NSA_KV), row(D_NSA_KV)],
        out_shape=[jax.ShapeDtypeStruct((m, D_NSA), BF16), jax.ShapeDtypeStruct((m, D_NSA_KV), BF16),
                   jax.ShapeDtypeStruct((m, D_NSA_KV), BF16)],
        compiler_params=_params("parallel"),
        name="rope_norm",
    )(posf, jnp.asarray(invf), proj, proj, q_gain.reshape(1, LANES), ks_gain.reshape(1, LANES), kw_gain.reshape(1, LANES))


def _compress_kernel(uk_ref, uv_ref, pek_ref, pev_ref, wk1_ref, wk2_ref, wv1_ref, wv2_ref, g_ref, kc_ref, vc_ref):
    def mlp(u, pe_ref, w1_ref, w2_ref):
        u = u.astype(F32)
        n = u.shape[0]
        h1 = _dot((u + pe_ref[0:1, :]).astype(BF16), w1_ref[0])
        h2 = _dot((u + pe_ref[1:2, :]).astype(BF16), w1_ref[1])
        pre = h1 + pltpu.roll(h2, n - 1, axis=0)
        return _dot(_silu(pre).astype(BF16), w2_ref[...])

    kc = mlp(uk_ref[0, 0], pek_ref, wk1_ref, wk2_ref)
    kc_ref[0, 0] = _rms(kc) * g_ref[...]
    vc_ref[0, 0] = mlp(uv_ref[0, 0], pev_ref, wv1_ref, wv2_ref)


def compress(uk, uv, pe_k, pe_v, wk1, wk2, wv1, wv2, kc_gain):
    bsz, nkv, nch, width = uk.shape
    half = CMP_STRIDE * NSA_HEAD_DIM
    u_spec = pl.BlockSpec((1, 1, nch, width), lambda b, g: (b, g, 0, 0))
    o_spec = pl.BlockSpec((1, 1, nch, NSA_HEAD_DIM), lambda b, g: (b, g, 0, 0))
    c2 = lambda s: pl.BlockSpec(s, lambda b, g: (0, 0))
    c3 = lambda s: pl.BlockSpec(s, lambda b, g: (0, 0, 0))
    o_shape = jax.ShapeDtypeStruct((bsz, nkv, nch, NSA_HEAD_DIM), F32)
    return pl.pallas_call(
        _compress_kernel,
        grid=(bsz, nkv),
        in_specs=[u_spec, u_spec, c2((2, half)), c2((2, half)),
                  c3((2, half, CMP_HIDDEN)), c2((CMP_HIDDEN, NSA_HEAD_DIM)),
                  c3((2, half, CMP_HIDDEN)), c2((CMP_HIDDEN, NSA_HEAD_DIM)), c2((1, NSA_HEAD_DIM))],
        out_specs=[o_spec, o_spec],
        out_shape=[o_shape, o_shape],
        compiler_params=_params("parallel", "parallel"),
        name="nsa_compress",
    )(uk, uv, pe_k.reshape(2, half), pe_v.reshape(2, half),
      wk1.reshape(2, half, CMP_HIDDEN).astype(BF16), wk2.astype(BF16),
      wv1.reshape(2, half, CMP_HIDDEN).astype(BF16), wv2.astype(BF16), kc_gain.reshape(1, NSA_HEAD_DIM))


def _stack_heads(q):
    return jnp.concatenate([q[:, r * NSA_HEAD_DIM:(r + 1) * NSA_HEAD_DIM] for r in range(NSA_Q_PER_KV)], axis=0)


def _unstack_heads(o, tq):
    return jnp.concatenate([o[r * tq:(r + 1) * tq, :] for r in range(NSA_Q_PER_KV)], axis=1)


def _cmp_select_kernel(q_ref, kc_ref, vc_ref, ovt_ref, o_ref, sel_ref, *, tq, n_blk):
    R = NSA_Q_PER_KV
    q0 = pl.program_id(2) * tq
    q4 = _stack_heads(q_ref[...])
    kc = kc_ref[0, 0].astype(BF16)
    ncmp = kc.shape[0]
    s = _dot_nt(q4, kc)
    t_row = q0 + lax.broadcasted_iota(I32, (tq, ncmp), 0)
    c_col = lax.broadcasted_iota(I32, (tq, ncmp), 1)
    visf = jnp.where((c_col * CMP_STRIDE + CMP_BLOCK - 1) <= t_row, 1.0, 0.0)
    vis = jnp.concatenate([visf] * R, axis=0) > 0.5
    s = jnp.where(vis, s, NEG_BIG)
    mx = jnp.max(s, axis=-1, keepdims=True)
    e = jnp.where(vis, jnp.exp2(s - mx), 0.0)
    den = jnp.sum(e, axis=-1, keepdims=True)
    p = e / jnp.maximum(den, 1e-30)
    o = _dot(p.astype(BF16), vc_ref[0, 0].astype(BF16))
    o_ref[...] = _unstack_heads(o, tq)

    psum = p[0:tq]
    for r in range(1, R):
        psum = psum + p[r * tq:(r + 1) * tq]
    h, m, l = _split3(psum)
    ovt = ovt_ref[...]
    imp = _dot_nt(ovt, h) + _dot_nt(ovt, m) + _dot_nt(ovt, l)
    jdx = lax.broadcasted_iota(I32, (n_blk, tq), 0)
    t_lane = q0 + lax.broadcasted_iota(I32, (n_blk, tq), 1)
    cur = t_lane // SEL_BLOCK
    causal = jdx <= cur
    forced = ((jdx == 0) | (jdx >= cur - 1)) & causal
    score = jnp.where(forced, jnp.inf, jnp.where(causal, imp, -jnp.inf))
    rank = jnp.zeros((n_blk, tq), F32)
    for jp in range(n_blk):
        rowv = score[jp:jp + 1, :]
        beats = (rowv > score) | ((rowv == score) & (jdx > jp))
        rank = rank + jnp.where(beats, 1.0, 0.0)
    sel = jnp.where((rank < float(min(N_SELECT, n_blk))) & causal, 1.0, 0.0)
    if n_blk < LANES:
        sel = jnp.concatenate([sel, jnp.zeros((LANES - n_blk, tq), F32)], axis=0)
    sel_ref[0, 0] = sel.astype(sel_ref.dtype)


def cmp_select(qn, kc, vc, bsz, seq):
    m = bsz * seq
    G = NSA_KV_HEADS
    tq = CMP_TQ
    nq = seq // tq
    ncmp = kc.shape[2]
    n_blk = seq // SEL_BLOCK
    assert n_blk <= LANES and tq % LANES == 0
    c0 = np.arange(ncmp)[None, :] * CMP_STRIDE
    s0 = np.arange(n_blk)[:, None] * SEL_BLOCK
    ov = np.clip(np.minimum(c0 + CMP_BLOCK, s0 + SEL_BLOCK) - np.maximum(c0, s0), 0, None) / CMP_STRIDE
    n_cmp_valid = (seq - CMP_BLOCK) // CMP_STRIDE + 1
    ov[:, n_cmp_valid:] = 0.0
    kv_spec = pl.BlockSpec((1, 1, ncmp, NSA_HEAD_DIM), lambda b, g, i: (b, g, 0, 0))
    kern = functools.partial(_cmp_select_kernel, tq=tq, n_blk=n_blk)
    return pl.pallas_call(
        kern,
        grid=(bsz, G, nq),
        in_specs=[pl.BlockSpec((tq, NSA_Q_PER_KV * NSA_HEAD_DIM), lambda b, g, i: (b * nq + i, g)),
                  kv_spec, kv_spec, pl.BlockSpec((n_blk, ncmp), lambda b, g, i: (0, 0))],
        out_specs=[pl.BlockSpec((tq, NSA_Q_PER_KV * NSA_HEAD_DIM), lambda b, g, i: (b * nq + i, g)),
                   pl.BlockSpec((1, 1, LANES, tq), lambda b, g, i: (b, g, 0, i))],
        out_shape=[jax.ShapeDtypeStruct((m, D_NSA), F32), jax.ShapeDtypeStruct((bsz, G, LANES, seq), BF16)],
        compiler_params=_params("parallel", "parallel", "parallel"),
        name="nsa_cmp_select",
    )(qn, kc, vc, jnp.asarray(ov, BF16))


def _masked_attn_kernel(q_ref, k_ref, vt_ref, *rest, mode, tq, tk):
    if mode == "sel":
        selt_ref, o_ref = rest
    else:
        (o_ref,) = rest
    R = NSA_Q_PER_KV
    hd = NSA_HEAD_DIM
    q0 = pl.program_id(2) * tq
    key_row = lax.broadcasted_iota(I32, (tk, tq), 0)
    t_lane = q0 + lax.broadcasted_iota(I32, (tk, tq), 1)
    hi = (q0 + tq - 1) // tk + 1
    q = q_ref[...]
    if mode == "sel":
        key_row_b = lax.broadcasted_iota(I32, (tk, LANES), 0)
        blk_lane = lax.broadcasted_iota(I32, (tk, LANES), 1)
        penalty = ((1.0 - selt_ref[0, 0].astype(F32)) * NEG_BIG).astype(BF16)
        rhs = [jnp.concatenate([q[:, r * hd:(r + 1) * hd].astype(F32).T.astype(BF16), penalty], axis=0)
               for r in range(R)]
        lo = 0
    else:
        lo = jnp.maximum(q0 - (WINDOW - 1), 0) // tk

    def body(kt, carry, diagonal=True):
        k0 = kt * tk
        k_t = k_ref[pl.ds(pl.multiple_of(k0, tk), tk), :]
        v_t = vt_ref[0, 0, kt]
        key = k0 + key_row
        if mode == "sel":
            expand = jnp.where((k0 + key_row_b) // SEL_BLOCK == blk_lane, 1.0, 0.0).astype(BF16)
            lhs = jnp.concatenate([k_t, expand], axis=1)
            ss = [_dot(lhs, rhs[r]) for r in range(R)]
            if diagonal:
                ss = [jnp.where(key <= t_lane, s, NEG_BIG) for s in ss]
        else:
            bias = jnp.where((key <= t_lane) & (key > t_lane - WINDOW), 0.0, NEG_BIG)
            ss = [_dot_nt(k_t, q[:, r * hd:(r + 1) * hd]) + bias for r in range(R)]
        stats = []
        for r in range(R):
            m_prev, l_prev = carry[3 * r], carry[3 * r + 1]
            m_new = jnp.maximum(m_prev, jnp.max(ss[r], axis=0, keepdims=True))
            p = jnp.exp2(ss[r] - m_new)
            alpha = jnp.exp2(m_prev - m_new)
            stats.append((m_new, alpha * l_prev + jnp.sum(p, axis=0, keepdims=True), alpha, p.astype(BF16)))
        new = []
        for r in range(R):
            m_new, l_new, alpha, p = stats[r]
            new += [m_new, l_new, alpha * carry[3 * r + 2] + _dot(v_t, p)]
        return tuple(new)

    init = (jnp.full((1, tq), NEG_BIG, F32), jnp.zeros((1, tq), F32), jnp.zeros((hd, tq), F32)) * R
    if mode == "sel":
        out = lax.fori_loop(lo, hi - 1, functools.partial(body, diagonal=False), init)
        out = body(hi - 1, out)
    else:
        out = lax.fori_loop(lo, hi, body, init)
    o_ref[...] = jnp.concatenate([(out[3 * r + 2] / out[3 * r + 1]).T for r in range(R)], axis=1)


def masked_attention(qn, k, k_col0, vt, bsz, seq, mode, selt=None):
    m = bsz * seq
    G = NSA_KV_HEADS
    tq = min(ATT_TQ, seq)
    tk = vt.shape[-1]
    assert tk % tq == 0
    nq = seq // tq
    hd = NSA_HEAD_DIM
    q_spec = pl.BlockSpec((tq, NSA_Q_PER_KV * hd), lambda b, g, i: (b * nq + i, g))
    in_specs = [q_spec,
                pl.BlockSpec((seq, hd), lambda b, g, i: (b, k_col0 + g)),
                pl.BlockSpec((1, 1, seq // tk, hd, tk), lambda b, g, i: (b, g, 0, 0, 0))]
    args = [qn, k, vt]
    if mode == "sel":
        in_specs.append(pl.BlockSpec((1, 1, LANES, tq), lambda b, g, i: (b, g, 0, i)))
        args.append(selt)
    return pl.pallas_call(
        functools.partial(_masked_attn_kernel, mode=mode, tq=tq, tk=tk),
        grid=(bsz, G, nq),
        in_specs=in_specs,
        out_specs=q_spec,
        out_shape=jax.ShapeDtypeStruct((m, D_NSA), F32),
        compiler_params=_params("parallel", "parallel", "arbitrary"),
        name="nsa_attn_" + mode,
    )(*args)


def _nsa_combine_kernel(g_ref, e_ref, oc_ref, os_ref, ow_ref, o_ref):
    gate = _sigmoid(g_ref[...])
    gx = _dot_f32_sel(gate, e_ref[...])
    y = gx[:, :D_NSA] * oc_ref[...] + gx[:, D_NSA:2 * D_NSA] * os_ref[...] + gx[:, 2 * D_NSA:] * ow_ref[...]
    o_ref[...] = y.astype(o_ref.dtype)


def nsa_combine(small, o_cmp, o_sel, o_win):
    m = small.shape[0]
    tm = min(COMBINE_TM, m)
    e = np.zeros((LANES, 3 * D_NSA), np.float32)
    for h in range(NSA_HEADS):
        for br in range(3):
            e[SSM_HEADS + 3 * h + br, br * D_NSA + h * NSA_HEAD_DIM:br * D_NSA + (h + 1) * NSA_HEAD_DIM] = 1.0
    row = lambda w: pl.BlockSpec((tm, w), lambda i: (i, 0))
    return pl.pallas_call(
        _nsa_combine_kernel,
        grid=(m // tm,),
        in_specs=[row(LANES), pl.BlockSpec((LANES, 3 * D_NSA), lambda i: (0, 0)), row(D_NSA), row(D_NSA), row(D_NSA)],
        out_specs=row(D_NSA),
        out_shape=jax.ShapeDtypeStruct((m, D_NSA), BF16),
        compiler_params=_params("parallel"),
        name="nsa_combine",
    )(small, jnp.asarray(e, BF16), o_cmp, o_sel, o_win)


def _xattn_kernel(q_ref, kv_ref, qg_ref, kg_ref, o_ref):
    hd = XATTN_HEAD_DIM
    scale = hd ** -0.5
    outs = []
    for h in range(XATTN_HEADS):
        q = (_rms(q_ref[:, h * hd:(h + 1) * hd]) * qg_ref[...]).astype(BF16)
        k = (_rms(kv_ref[:, h * hd:(h + 1) * hd]) * kg_ref[...]).astype(BF16)
        v = kv_ref[:, D_XATTN + h * hd:D_XATTN + (h + 1) * hd].astype(BF16)
        s = _dot_nt(q, k) * scale
        e = jnp.exp(s - jnp.max(s, axis=-1, keepdims=True))
        p = e / jnp.sum(e, axis=-1, keepdims=True)
        outs.append(_dot(p.astype(BF16), v))
    o_ref[...] = jnp.concatenate(outs, axis=1).astype(o_ref.dtype)


def xattn_core(q, kv, q_gain, k_gain, bsz, seq):
    m = bsz * seq
    n_mem = kv.shape[0] // bsz
    tm = min(ROW_TILE, seq)
    nt = seq // tm
    return pl.pallas_call(
        _xattn_kernel,
        grid=(bsz, nt),
        in_specs=[pl.BlockSpec((tm, D_XATTN), lambda b, i: (b * nt + i, 0)),
                  pl.BlockSpec((n_mem, 2 * D_XATTN), lambda b, i: (b, 0)),
                  pl.BlockSpec((1, XATTN_HEAD_DIM), lambda b, i: (0, 0)),
                  pl.BlockSpec((1, XATTN_HEAD_DIM), lambda b, i: (0, 0))],
        out_specs=pl.BlockSpec((tm, D_XATTN), lambda b, i: (b * nt + i, 0)),
        out_shape=jax.ShapeDtypeStruct((m, D_XATTN), BF16),
        compiler_params=_params("parallel", "parallel"),
        name="xattn_core",
    )(q, kv, q_gain.reshape(1, XATTN_HEAD_DIM), k_gain.reshape(1, XATTN_HEAD_DIM))


def memory_xattn(h, mem2d, norm_x, norm_m, wq, wkv, wo, q_gain, k_gain, bsz, seq):
    hq = rmsnorm(h, norm_x, BF16)
    q = matmul([(hq, wq.astype(BF16))])
    mem_n = rmsnorm(mem2d, norm_m, BF16)
    kv = matmul([(mem_n, wkv.astype(BF16))])
    o = xattn_core(q, kv, q_gain, k_gain, bsz, seq)
    return matmul([(o, wo.astype(BF16))], residual=h)


def _mix_kernel(h_ref, hp_ref, g_ref, mu_ref, *rest, tm, tiles_per_seq):
    outs, ext_ref = rest[:-1], rest[-1]
    i = pl.program_id(0)
    hn = _rms(h_ref[...]) * g_ref[...]
    hp = _rms(hp_ref[...]) * g_ref[...]
    hp = jnp.where(i % tiles_per_seq == 0, 0.0, hp)
    ext_ref[0:SUBLANES, :] = hp
    ext_ref[SUBLANES:tm + SUBLANES, :] = hn
    xx = ext_ref[SUBLANES - 1:tm + SUBLANES - 1, :] - hn
    for j, o_ref in enumerate(outs):
        o_ref[...] = (hn + xx * mu_ref[j:j + 1, :]).astype(o_ref.dtype)


def rwkv_mix(h, gain, mu, seq):
    m, d = h.shape
    tm = min(ROW_TILE, seq)
    n_mix = mu.shape[0]
    kern = functools.partial(_mix_kernel, tm=tm, tiles_per_seq=seq // tm)
    row = pl.BlockSpec((tm, d), lambda i: (i, 0))
    return pl.pallas_call(
        kern,
        grid=(m // tm,),
        in_specs=[row, pl.BlockSpec((SUBLANES, d), lambda i: (jnp.maximum(i * (tm // SUBLANES) - 1, 0), 0)),
                  pl.BlockSpec((1, d), lambda i: (0, 0)), pl.BlockSpec((n_mix, d), lambda i: (0, 0))],
        out_specs=[row] * n_mix,
        out_shape=[jax.ShapeDtypeStruct((m, d), BF16)] * n_mix,
        scratch_shapes=[pltpu.VMEM((tm + SUBLANES, d), F32)],
        compiler_params=_params("parallel"),
        name="rwkv_mix",
    )(h, h, gain.reshape(1, d), mu)


WKV_LANE_GROUP = LANES // 4
WKV_ROWS_PER_BATCH = D_MODEL // LANES


def _wkv_fused_kernel(r_ref, k_ref, v_ref, w_ref, a_ref, g_ref, kkp_ref, kap_ref, rkp_ref, lnw_ref, lnb_ref,
                      o_ref, s_ref, wd_s, kh_s, kk_s, b_s, r_s, v_s, g_s, *, steps):
    n = RWKV_HEAD_DIM
    nb = LANES // WKV_LANE_GROUP
    halves = WKV_ROWS_PER_BATCH // SUBLANES
    ahead = min(WKV_LOOKAHEAD, steps)

    @pl.when(pl.program_id(0) == 0)
    def _():
        s_ref[...] = jnp.zeros_like(s_ref)

    def lane_group(cnt):
        return lax.broadcasted_iota(I32, (cnt, SUBLANES, LANES), 2) // WKV_LANE_GROUP

    def to_scan(x_ref, t0, cnt):
        grp = lane_group(cnt)
        rows = [x_ref[b, pl.ds(t0, cnt)].astype(F32) for b in range(nb)]
        pieces = []
        for n4 in range(nb):
            for v in range(halves):
                acc = None
                for b in range(nb):
                    x = rows[b][:, SUBLANES * v:SUBLANES * (v + 1), :]
                    shift = (WKV_LANE_GROUP * (b - n4)) % LANES
                    if shift:
                        x = pltpu.roll(x, shift, axis=2)
                    acc = x if acc is None else jnp.where(grp == b, x, acc)
                pieces.append(acc)
        return pieces

    def prepare(t0, cnt):
        rs, ks, vs, ws, as_, gs = (to_scan(x, t0, cnt) for x in (r_ref, k_ref, v_ref, w_ref, a_ref, g_ref))
        kks = [ks[u] * kkp_ref[SUBLANES * u:SUBLANES * (u + 1), :] for u in range(len(ks))]
        ss = kks[0] * kks[0]
        for u in range(1, len(kks)):
            ss = ss + kks[u] * kks[u]
        inv_norm = 1.0 / jnp.maximum(jnp.sqrt(jnp.sum(ss, axis=1, keepdims=True)), 1e-12)
        for u in range(len(ks)):
            rows = slice(SUBLANES * u, SUBLANES * (u + 1))
            at = (pl.ds(t0, cnt), rows, slice(None))
            kk = kks[u] * inv_norm
            wd_s[at] = jnp.exp(-jnp.exp(-_softplus(-ws[u]) - 0.5))
            kk_s[at] = kk
            b_s[at] = kk * as_[u]
            kh_s[at] = ks[u] * (1.0 + (as_[u] - 1.0) * kap_ref[rows, :])
            r_s[at] = rs[u]
            v_s[at] = vs[u]
            g_s[at] = gs[u]

    def finish(t, y):
        inv_n = 1.0 / n
        mean = jnp.sum(y, axis=0, keepdims=True) * inv_n
        yc = y - mean
        var = jnp.sum(yc * yc, axis=0, keepdims=True) * inv_n
        bonus = jnp.sum(r_s[t] * kh_s[t] * rkp_ref[...], axis=0, keepdims=True) * v_s[t]
        o = (yc * lax.rsqrt(var + RWKV_GN_EPS) * lnw_ref[...] + lnb_ref[...] + bonus) * g_s[t]
        grp = lane_group(1)[0]
        for b in range(nb):
            parts = []
            for v in range(halves):
                acc = None
                for n4 in range(nb):
                    u = n4 * halves + v
                    x = o[SUBLANES * u:SUBLANES * (u + 1), :]
                    shift = (WKV_LANE_GROUP * (n4 - b)) % LANES
                    if shift:
                        x = pltpu.roll(x, shift, axis=1)
                    acc = x if acc is None else jnp.where(grp == n4, x, acc)
                parts.append(acc)
            o_ref[b, pl.ds(t, 1)] = jnp.concatenate(parts, axis=0).astype(o_ref.dtype)[None]

    prepare(0, ahead)
    q0 = jnp.zeros((n, LANES), F32)
    for j in range(n):
        q0 = q0 + s_ref[j] * kk_s[0, j:j + 1, :]

    def step(t, q):
        tn = jnp.minimum(t + 1, steps - 1)
        v = v_s[t]
        y = jnp.zeros((n, LANES), F32)
        qn = jnp.zeros((n, LANES), F32)
        for j in range(n):
            s_new = s_ref[j] * wd_s[t, j:j + 1, :] - q * b_s[t, j:j + 1, :] + v * kh_s[t, j:j + 1, :]
            s_ref[j] = s_new
            y = y + s_new * r_s[t, j:j + 1, :]
            qn = qn + s_new * kk_s[tn, j:j + 1, :]
        finish(t, y)
        prepare(jnp.minimum(t + ahead, steps - 1), 1)
        return qn

    lax.fori_loop(0, steps, step, q0)


def _wkv_column_order():
    c, n4, h = np.meshgrid(np.arange(WKV_ROWS_PER_BATCH), np.arange(LANES // WKV_LANE_GROUP), np.arange(RWKV_HEADS),
                           indexing="ij")
    return (h * RWKV_HEAD_DIM + n4 * WKV_ROWS_PER_BATCH + c).reshape(-1)


def _scan_tile(p, bsz):
    return jnp.tile(p.reshape(RWKV_HEADS, RWKV_HEAD_DIM).T, (1, bsz))


def wkv_fused(r, k, v, w_raw, a_sig, g, k_k, k_a, r_k, ln_w, ln_b, bsz, seq):
    assert bsz * RWKV_HEADS == LANES and RWKV_HEADS == WKV_LANE_GROUP
    steps = min(WKV_T, seq)
    view = lambda x: x.reshape(bsz, seq, WKV_ROWS_PER_BATCH, LANES)
    blk = pl.BlockSpec((bsz, steps, WKV_ROWS_PER_BATCH, LANES), lambda i: (0, i, 0, 0))
    par = pl.BlockSpec((RWKV_HEAD_DIM, LANES), lambda i: (0, 0))
    scr = pltpu.VMEM((steps, RWKV_HEAD_DIM, LANES), F32)
    out = pl.pallas_call(
        functools.partial(_wkv_fused_kernel, steps=steps),
        grid=(seq // steps,),
        in_specs=[blk] * 6 + [par] * 5,
        out_specs=blk,
        out_shape=jax.ShapeDtypeStruct((bsz, seq, WKV_ROWS_PER_BATCH, LANES), BF16),
        scratch_shapes=[pltpu.VMEM((RWKV_HEAD_DIM, RWKV_HEAD_DIM, LANES), F32)] + [scr] * 7,
        compiler_params=_params("arbitrary"),
        name="wkv_fused",
    )(view(r), view(k), view(v), view(w_raw), view(a_sig), view(g),
      _scan_tile(k_k, bsz), _scan_tile(k_a, bsz), _scan_tile(r_k, bsz), _scan_tile(ln_w, bsz), _scan_tile(ln_b, bsz))
    return out.reshape(bsz * seq, D_MODEL)


def _router_kernel(h_ref, g_ref, wr_ref, info_ref, tile_cnt_ref, total_ref, carry_ref, *, tt):
    E = N_EXPERTS
    i = pl.program_id(0)

    @pl.when(i == 0)
    def _():
        carry_ref[...] = jnp.zeros_like(carry_ref)

    xh, xm, xl = _split3(_rms(h_ref[...]) * g_ref[...])
    wh, wm, wl = _split3(wr_ref[...])
    logits = (_dot_nt(wh, xh) + _dot_nt(wh, xm) + _dot_nt(wm, xh)
              + _dot_nt(wh, xl) + _dot_nt(wl, xh) + _dot_nt(wm, xm))
    eidx = lax.broadcasted_iota(I32, (E, tt), 0)
    v1 = jnp.max(logits, axis=0, keepdims=True)
    i1 = jnp.min(jnp.where(logits == v1, eidx, E), axis=0, keepdims=True)
    rest = jnp.where(eidx == i1, -jnp.inf, logits)
    v2 = jnp.max(rest, axis=0, keepdims=True)
    i2 = jnp.min(jnp.where(rest == v2, eidx, E), axis=0, keepdims=True)
    e2 = jnp.exp(v2 - v1)
    g1 = 1.0 / (1.0 + e2)
    g2 = e2 / (1.0 + e2)
    hit1 = eidx == i1
    hit2 = eidx == i2
    onehot = jnp.where(hit1 | hit2, 1.0, 0.0)
    rr = lax.broadcasted_iota(I32, (tt, tt), 0)
    cc = lax.broadcasted_iota(I32, (tt, tt), 1)
    before = jnp.where(rr < cc, 1.0, 0.0).astype(BF16)
    carry = carry_ref[...]
    carry_wide = jnp.concatenate([carry] * (tt // LANES), axis=1)
    cum = _dot(onehot.astype(BF16), before) + carry_wide
    rank1 = jnp.sum(jnp.where(hit1, cum, 0.0), axis=0, keepdims=True)
    rank2 = jnp.sum(jnp.where(hit2, cum, 0.0), axis=0, keepdims=True)
    zero = jnp.zeros((1, tt), F32)
    info_ref[...] = jnp.concatenate([i1.astype(F32), i2.astype(F32), rank1, rank2, g1, g2, zero, zero], axis=0)
    tile_cnt_ref[0] = carry
    carry = carry + _dot(onehot.astype(BF16), jnp.ones((tt, LANES), BF16))
    carry_ref[...] = carry
    total_ref[...] = carry


def moe_router(h, norm_gain, w_router):
    m, d = h.shape
    tt = min(MOE_TILE, m)
    nt = m // tt
    return pl.pallas_call(
        functools.partial(_router_kernel, tt=tt),
        grid=(nt,),
        in_specs=[pl.BlockSpec((tt, d), lambda i: (i, 0)), pl.BlockSpec((1, d), lambda i: (0, 0)),
                  pl.BlockSpec((N_EXPERTS, d), lambda i: (0, 0))],
        out_specs=[pl.BlockSpec((SUBLANES, tt), lambda i: (0, i)),
                   pl.BlockSpec((1, N_EXPERTS, LANES), lambda i: (i, 0, 0)),
                   pl.BlockSpec((N_EXPERTS, LANES), lambda i: (0, 0))],
        out_shape=[jax.ShapeDtypeStruct((SUBLANES, m), F32), jax.ShapeDtypeStruct((nt, N_EXPERTS, LANES), F32),
                   jax.ShapeDtypeStruct((N_EXPERTS, LANES), F32)],
        scratch_shapes=[pltpu.VMEM((N_EXPERTS, LANES), F32)],
        compiler_params=_params("arbitrary"),
        name="moe_router",
    )(h, norm_gain.reshape(1, d), w_router.T)


def _slots_kernel(start_ref, info_ref, o_ref):
    info = info_ref[...]
    e1, e2, r1, r2 = info[0:1], info[1:2], info[2:3], info[3:4]
    s1 = r1
    s2 = r2
    for e in range(N_EXPERTS):
        st = start_ref[e].astype(F32)
        s1 = s1 + jnp.where(e1 == float(e), st, 0.0)
        s2 = s2 + jnp.where(e2 == float(e), st, 0.0)
    o_ref[...] = jnp.concatenate([s1, s2, info[4:5], info[5:6], info[6:8], info[6:8]], axis=0)


def moe_slots(info, slot_start):
    m = info.shape[1]
    tt = min(2048, m)
    return pl.pallas_call(
        _slots_kernel,
        grid_spec=pltpu.PrefetchScalarGridSpec(
            num_scalar_prefetch=1, grid=(m // tt,),
            in_specs=[pl.BlockSpec((SUBLANES, tt), lambda i, st: (0, i))],
            out_specs=pl.BlockSpec((SUBLANES, tt), lambda i, st: (0, i))),
        out_shape=jax.ShapeDtypeStruct((SUBLANES, m), F32),
        compiler_params=_params("parallel"),
        name="moe_slots",
    )(slot_start, info)


def _dispatch_kernel(blk_ref, til_ref, flg_ref, slots_ref, x_ref, o_ref, go_ref, *, bs):
    w = pl.program_id(0)
    flags = flg_ref[w]
    first = (flags & 1) != 0
    real = (flags & 4) != 0

    def gathered():
        sl = slots_ref[...]
        sid = (blk_ref[w] * bs + lax.broadcasted_iota(I32, (bs, sl.shape[1]), 0)).astype(F32)
        hit1 = sid == sl[0:1, :]
        hit2 = sid == sl[1:2, :]
        onehot = jnp.where(hit1 | hit2, 1.0, 0.0).astype(BF16)
        gate = jnp.sum(jnp.where(hit1, sl[2:3, :], 0.0) + jnp.where(hit2, sl[3:4, :], 0.0), axis=1, keepdims=True)
        return _dot(onehot, x_ref[...]).astype(o_ref.dtype), jnp.broadcast_to(gate, go_ref.shape)

    @pl.when(first & real)
    def _():
        o_ref[...], go_ref[...] = gathered()

    @pl.when(first & jnp.logical_not(real))
    def _():
        o_ref[...] = jnp.zeros_like(o_ref)
        go_ref[...] = jnp.zeros_like(go_ref)

    @pl.when(jnp.logical_not(first) & real)
    def _():
        rows, gate = gathered()
        o_ref[...] += rows
        go_ref[...] += gate


def moe_dispatch(xn, slots, wl_blk, wl_til, wl_flg, n_slots):
    m, d = xn.shape
    bs = tt = MOE_TILE
    n_items = wl_blk.shape[0]
    return pl.pallas_call(
        functools.partial(_dispatch_kernel, bs=bs),
        grid_spec=pltpu.PrefetchScalarGridSpec(
            num_scalar_prefetch=3, grid=(n_items,),
            in_specs=[pl.BlockSpec((SUBLANES, tt), lambda w, b, t, f: (0, t[w])),
                      pl.BlockSpec((tt, d), lambda w, b, t, f: (t[w], 0))],
            out_specs=[pl.BlockSpec((bs, d), lambda w, b, t, f: (b[w], 0)),
                       pl.BlockSpec((bs, LANES), lambda w, b, t, f: (b[w], 0))]),
        out_shape=[jax.ShapeDtypeStruct((n_slots, d), BF16), jax.ShapeDtypeStruct((n_slots, LANES), F32)],
        compiler_params=_params("arbitrary"),
        name="moe_dispatch",
    )(wl_blk, wl_til, wl_flg, slots, xn)


def _moe_ffn_kernel(be_ref, rows_ref, x_ref, gate_ref, w1_ref, w3_ref, w2_ref, o_ref, acc_ref, *, sub):
    b = pl.program_id(0)
    f = pl.program_id(1)
    bm = x_ref.shape[0]

    @pl.when(f == 0)
    def _():
        acc_ref[...] = jnp.zeros_like(acc_ref)

    for rows in range(sub, bm + 1, sub):
        @pl.when(rows_ref[b] == rows)
        def _(rows=rows):
            x = x_ref[0:rows, :]
            h = _silu(_dot(x, w1_ref[0].astype(BF16))) * _dot(x, w3_ref[0].astype(BF16))
            acc_ref[0:rows, :] += _dot(h.astype(BF16), w2_ref[0].astype(BF16))

    @pl.when(f == pl.num_programs(1) - 1)
    def _():
        o_ref[...] = (acc_ref[...] * gate_ref[:, 0:1]).astype(o_ref.dtype)


def moe_ffn(xs, slot_gate, w1, w3, w2, block_expert, block_rows):
    n_slots, d = xs.shape
    ff = w1.shape[2]
    bm = MOE_FFN_ROWS
    tf = MOE_TF
    nf = ff // tf

    def f_of(b, f, rows):
        return jnp.where(rows[b] > 0, f, nf - 1)

    return pl.pallas_call(
        functools.partial(_moe_ffn_kernel, sub=MOE_TILE),
        grid_spec=pltpu.PrefetchScalarGridSpec(
            num_scalar_prefetch=2, grid=(n_slots // bm, nf),
            in_specs=[pl.BlockSpec((bm, d), lambda b, f, be, nb: (b, 0)),
                      pl.BlockSpec((bm, LANES), lambda b, f, be, nb: (b, 0)),
                      pl.BlockSpec((1, d, tf), lambda b, f, be, nb: (be[b], 0, f_of(b, f, nb))),
                      pl.BlockSpec((1, d, tf), lambda b, f, be, nb: (be[b], 0, f_of(b, f, nb))),
                      pl.BlockSpec((1, tf, d), lambda b, f, be, nb: (be[b], f_of(b, f, nb), 0))],
            out_specs=pl.BlockSpec((bm, d), lambda b, f, be, nb: (b, 0)),
            scratch_shapes=[pltpu.VMEM((bm, d), F32)]),
        out_shape=jax.ShapeDtypeStruct((n_slots, d), BF16),
        compiler_params=_params("arbitrary", "arbitrary"),
        name="moe_ffn",
    )(block_expert, block_rows, xs, slot_gate, w1, w3, w2)


def _combine_kernel(til_ref, blk_ref, flg_ref, tok_ref, y_ref, res_ref, o_ref, *, bs):
    w = pl.program_id(0)
    flags = flg_ref[w]

    @pl.when((flags & 1) != 0)
    def _():
        o_ref[...] = res_ref[...]

    @pl.when((flags & 4) != 0)
    def _():
        tok = tok_ref[...]
        tt = tok.shape[0]
        sid = (blk_ref[w] * bs + lax.broadcasted_iota(I32, (tt, bs), 1)).astype(F32)
        onehot = jnp.where((sid == tok[:, 0:1]) | (sid == tok[:, 1:2]), 1.0, 0.0).astype(BF16)
        o_ref[...] += _dot(onehot, y_ref[...])


def moe_combine(y, tok_info, residual, wl_til, wl_blk, wl_flg):
    m, d = residual.shape
    bs = tt = MOE_TILE
    n_items = wl_til.shape[0]
    return pl.pallas_call(
        functools.partial(_combine_kernel, bs=bs),
        grid_spec=pltpu.PrefetchScalarGridSpec(
            num_scalar_prefetch=3, grid=(n_items,),
            in_specs=[pl.BlockSpec((tt, SUBLANES), lambda w, t, b, f: (t[w], 0)),
                      pl.BlockSpec((bs, d), lambda w, t, b, f: (b[w], 0)),
                      pl.BlockSpec((tt, d), lambda w, t, b, f: (t[w], 0))],
            out_specs=pl.BlockSpec((tt, d), lambda w, t, b, f: (t[w], 0))),
        out_shape=jax.ShapeDtypeStruct((m, d), F32),
        compiler_params=_params("arbitrary"),
        name="moe_combine",
    )(wl_til, wl_blk, wl_flg, tok_info, y, residual)


def _work_list(overlap, n_items):
    n_major, n_minor = overlap.shape
    has = jnp.any(overlap, axis=1)
    marked = jnp.concatenate([overlap[:, :1] | ~has[:, None], overlap[:, 1:]], axis=1)
    maj, mino = jnp.nonzero(marked, size=n_items, fill_value=-1)
    valid = maj >= 0
    real = valid & overlap[jnp.maximum(maj, 0), jnp.maximum(mino, 0)]
    prev = jnp.concatenate([jnp.full((1,), -2, maj.dtype), maj[:-1]])
    nxt = jnp.concatenate([maj[1:], jnp.full((1,), -2, maj.dtype)])
    flags = (valid & (maj != prev)) * 1 + (valid & (maj != nxt)) * 2 + real * 4
    last_maj = jnp.max(jnp.where(valid, maj, 0))
    last_min = jnp.sum(jnp.where(valid & (maj == last_maj) & (maj != nxt), mino, 0))
    maj = jnp.where(valid, maj, last_maj)
    mino = jnp.where(valid, mino, last_min)
    return maj.astype(I32), mino.astype(I32), flags.astype(I32)


def moe_layer(h, norm_gain, w_router, w1, w3, w2):
    m, d = h.shape
    E = N_EXPERTS
    tile = min(MOE_TILE, m)
    n_tiles = m // tile
    ffn_rows = MOE_FFN_ROWS
    xn = rmsnorm(h, norm_gain, BF16)
    info, tile_cnt, total = moe_router(h, norm_gain, w_router)
    before = tile_cnt[:, :, 0]
    counts = total[:, 0].astype(I32)
    padded = (counts + ffn_rows - 1) // ffn_rows * ffn_rows
    ends = jnp.cumsum(padded)
    starts = ends - padded
    n_slots = (m * TOP_K // ffn_rows + E) * ffn_rows
    n_ffn_blocks = n_slots // ffn_rows
    n_dblocks = n_slots // tile
    slots = moe_slots(info, starts.astype(I32))
    db_start = jnp.arange(n_dblocks, dtype=I32) * tile
    db_e = jnp.minimum(jnp.sum(db_start[:, None] >= ends[None, :], axis=1), E - 1)
    r0 = db_start - starts[db_e]
    r1 = jnp.minimum(r0 + tile, counts[db_e])
    after = jnp.concatenate([before[1:], counts[None, :].astype(F32)], axis=0).astype(I32)
    bef = before.astype(I32)
    ov = (bef[:, db_e].T < r1[:, None]) & (after[:, db_e].T > r0[:, None]) & (db_start < ends[-1])[:, None]
    n_items = n_dblocks + E * n_tiles
    d_blk, d_til, d_flg = _work_list(ov, n_items)
    c_til, c_blk, c_flg = _work_list(ov.T, n_items)
    xs, slot_gate = moe_dispatch(xn, slots, d_blk, d_til, d_flg, n_slots)
    fb_start = jnp.arange(n_ffn_blocks, dtype=I32) * ffn_rows
    block_expert = jnp.minimum(jnp.sum(fb_start[:, None] >= ends[None, :], axis=1), E - 1).astype(I32)
    filled = counts[block_expert] - (fb_start - starts[block_expert])
    block_rows = jnp.where(fb_start < ends[-1], (jnp.clip(filled, 0, ffn_rows) + tile - 1) // tile * tile, 0).astype(I32)
    ys = moe_ffn(xs, slot_gate, w1, w3, w2, block_expert, block_rows)
    return moe_combine(ys, slots.T, h, c_til, c_blk, c_flg)


def even_mixer_layer(h, posf, p, bsz, seq):
    m = bsz * seq
    hn = rmsnorm(h, p["norm_mix"], BF16)
    w_in = p["w_in"]
    c = np.cumsum([0, D_SSM, D_CONV, SSM_HEADS, D_NSA] + [D_NSA_KV] * 6 + [3 * NSA_HEADS])
    w_proj = jnp.concatenate([w_in[:, c[1]:c[2]], w_in[:, c[4]:c[10]], w_in[:, c[0]:c[1]], w_in[:, c[3]:c[4]]], axis=1)
    proj = matmul([(hn, w_proj.astype(BF16))], out_dtype=BF16)
    kv0 = D_CONV
    w_small = jnp.concatenate([w_in[:, c[2]:c[3]], w_in[:, c[10]:c[11]]], axis=1)
    w_small = jnp.pad(w_small, ((0, 0), (0, LANES - w_small.shape[1]))).astype(BF16)
    small = matmul([(hn, w_small)])
    y_ssm = ssd_group(proj, (D_CONV + 6 * D_NSA_KV) // D_SSM, 0, small, p["conv_w"], p["conv_b"], p["dt_bias"],
                      p["a_log"], p["d_skip"], p["ssm_norm"], bsz, seq)
    qn, ksn, kwn = rope_norm(posf, proj, (D_CONV + 6 * D_NSA_KV + D_SSM) // D_NSA, kv0 // (6 * D_NSA_KV),
                             p["q_gain"], p["ks_gain"], p["kw_gain"])
    kv6 = proj[:, kv0:kv0 + 6 * D_NSA_KV]
    nch = seq // CMP_STRIDE

    def chunks(col0):
        u = kv6[:, col0:col0 + D_NSA_KV].reshape(bsz, nch, CMP_STRIDE, NSA_KV_HEADS, NSA_HEAD_DIM)
        return u.transpose(0, 3, 1, 2, 4).reshape(bsz, NSA_KV_HEADS, nch, CMP_STRIDE * NSA_HEAD_DIM)

    kc, vc = compress(chunks(0), chunks(D_NSA_KV), p["pe_k"], p["pe_v"], p["wk1"], p["wk2"], p["wv1"], p["wv2"],
                      p["kc_gain"])
    o_cmp, selt = cmp_select(qn, kc, vc, bsz, seq)
    tk = min(ATT_TK, seq)

    def v_tiles(col0):
        u = kv6[:, col0:col0 + D_NSA_KV].astype(BF16).reshape(bsz, seq // tk, tk, NSA_KV_HEADS, NSA_HEAD_DIM)
        return u.transpose(0, 3, 1, 4, 2)

    o_sel = masked_attention(qn, ksn, 0, v_tiles(3 * D_NSA_KV), bsz, seq, "sel", selt)
    o_win = masked_attention(qn, kwn, 0, v_tiles(5 * D_NSA_KV), bsz, seq, "win")
    y_nsa = nsa_combine(small, o_cmp, o_sel, o_win)
    w_out = p["w_out"].astype(BF16)
    return matmul([(y_ssm, w_out[:D_SSM]), (y_nsa, w_out[D_SSM:])], residual=h)


def rwkv_layer(h, p, bsz, seq):
    m, d = h.shape
    mixes = rwkv_mix(h, p["norm_mix"], p["mu"], seq)
    bf = lambda w: w.astype(BF16)

    def lora_pad(w_a, w_b):
        r = w_a.shape[1]
        rp = -(-r // LANES) * LANES
        return bf(jnp.pad(w_a, ((0, 0), (0, rp - r)))), bf(jnp.pad(w_b, ((0, rp - r), (0, 0))))

    perm = _wkv_column_order()
    r = matmul([(mixes[0], bf(p["w_r"][:, perm]))], out_dtype=BF16)
    k = matmul([(mixes[2], bf(p["w_k"][:, perm]))], out_dtype=BF16)
    v = matmul([(mixes[3], bf(p["w_v"][:, perm]))], out_dtype=BF16)
    w1, w2 = lora_pad(p["w1"], p["w2"][:, perm])
    w_raw = matmul([(matmul([(mixes[1], w1)], act="tanh", out_dtype=BF16), w2)], bias=p["w0"][perm])
    a1, a2 = lora_pad(p["a1"], p["a2"][:, perm])
    a_sig = matmul([(matmul([(mixes[4], a1)], out_dtype=BF16), a2)], bias=p["a0"][perm], act="sigmoid", out_dtype=BF16)
    g1, g2 = lora_pad(p["g1"], p["g2"][:, perm])
    g = matmul([(matmul([(mixes[5], g1)], act="sigmoid", out_dtype=BF16), g2)], out_dtype=BF16)
    o = wkv_fused(r, k, v, w_raw, a_sig, g, p["k_k"], p["k_a"], p["r_k"].reshape(-1), p["ln_w"], p["ln_b"], bsz, seq)
    return matmul([(o, bf(p["w_o"][perm, :]))], residual=h)


def kernel(x, mem, positions, norm_mix, norm_xattn, norm_mem, norm_ffn, xattn_wq, xattn_wkv, xattn_wo, xattn_q_gain, xattn_k_gain, ev_w_in, ev_conv_w, ev_conv_b, ev_dt_bias, ev_a_log, ev_d_skip, ev_ssm_norm, ev_q_gain, ev_kc_gain, ev_ks_gain, ev_kw_gain, ev_pe_k, ev_pe_v, ev_cmp_wk1, ev_cmp_wk2, ev_cmp_wv1, ev_cmp_wv2, ev_w_out, ev_ffn_w1, ev_ffn_w3, ev_ffn_w2, od_mu, od_w_r, od_w_k, od_w_v, od_w_o, od_w0, od_w1, od_w2, od_a0, od_a1, od_a2, od_g1, od_g2, od_k_k, od_k_a, od_r_k, od_ln_w, od_ln_b, od_router, od_moe_w1, od_moe_w3, od_moe_w2):
    bsz, seq, d = x.shape
    m = bsz * seq
    depth = norm_mix.shape[0]
    h = x.reshape(m, d)
    mem2d = mem.reshape(-1, d)
    posf = positions.astype(F32).reshape(m, 1)
    for layer in range(depth):
        i = layer // 2
        if layer % 2 == 0:
            p = dict(norm_mix=norm_mix[layer], w_in=ev_w_in[i], conv_w=ev_conv_w[i], conv_b=ev_conv_b[i],
                     dt_bias=ev_dt_bias[i], a_log=ev_a_log[i], d_skip=ev_d_skip[i], ssm_norm=ev_ssm_norm[i],
                     q_gain=ev_q_gain[i], kc_gain=ev_kc_gain[i], ks_gain=ev_ks_gain[i], kw_gain=ev_kw_gain[i],
                     pe_k=ev_pe_k[i], pe_v=ev_pe_v[i], wk1=ev_cmp_wk1[i], wk2=ev_cmp_wk2[i], wv1=ev_cmp_wv1[i],
                     wv2=ev_cmp_wv2[i], w_out=ev_w_out[i])
            h = even_mixer_layer(h, posf, p, bsz, seq)
        else:
            p = dict(norm_mix=norm_mix[layer], mu=od_mu[i], w_r=od_w_r[i], w_k=od_w_k[i], w_v=od_w_v[i], w_o=od_w_o[i],
                     w0=od_w0[i], w1=od_w1[i], w2=od_w2[i], a0=od_a0[i], a1=od_a1[i], a2=od_a2[i], g1=od_g1[i],
                     g2=od_g2[i], k_k=od_k_k[i], k_a=od_k_a[i], r_k=od_r_k[i], ln_w=od_ln_w[i], ln_b=od_ln_b[i])
            h = rwkv_layer(h, p, bsz, seq)
        h = memory_xattn(h, mem2d, norm_xattn[layer], norm_mem[layer], xattn_wq[layer], xattn_wkv[layer],
                         xattn_wo[layer], xattn_q_gain[layer], xattn_k_gain[layer], bsz, seq)
        if layer % 2 == 0:
            hn = rmsnorm(h, norm_ffn[layer], BF16)
            h = swiglu_ffn(hn, ev_ffn_w1[i].astype(BF16), ev_ffn_w3[i].astype(BF16), ev_ffn_w2[i].astype(BF16), h)
        else:
            h = moe_layer(h, norm_ffn[layer], od_router[i], od_moe_w1[i], od_moe_w3[i], od_moe_w2[i])
    return h.reshape(bsz, seq, d)
```

```python
import functools
import math

import numpy as np
import jax
import jax.numpy as jnp
from jax import lax
from jax.experimental import pallas as pl
from jax.experimental.pallas import tpu as pltpu

F32 = jnp.float32
BF16 = jnp.bfloat16
I32 = jnp.int32

D_MODEL = 2048
NORM_EPS = 1e-6
NEG_BIG = -1e30

SSM_HEADS = 32
SSM_HEAD_DIM = 64
D_SSM = SSM_HEADS * SSM_HEAD_DIM
SSM_GROUPS = 4
D_STATE = 128
CONV_WIDTH = 4
SSD_CHUNK = 128
D_CONV = D_SSM + 2 * SSM_GROUPS * D_STATE
D_SSM_GROUP = D_SSM // SSM_GROUPS
SSD_CONV_TAIL = 16
SSD_EXT_ROWS = 256

NSA_HEADS = 16
NSA_KV_HEADS = 4
NSA_HEAD_DIM = 128
NSA_Q_PER_KV = NSA_HEADS // NSA_KV_HEADS
D_NSA = NSA_HEADS * NSA_HEAD_DIM
D_NSA_KV = NSA_KV_HEADS * NSA_HEAD_DIM
CMP_BLOCK = 32
CMP_STRIDE = 16
CMP_HIDDEN = 256
SEL_BLOCK = 64
N_SELECT = 16
WINDOW = 512
ROPE_DIM = NSA_HEAD_DIM // 4
ROPE_THETA = 500000.0

RWKV_HEAD_DIM = 64
RWKV_HEADS = D_MODEL // RWKV_HEAD_DIM
RWKV_GN_EPS = 1e-5 * RWKV_HEAD_DIM

XATTN_HEADS = 4
XATTN_HEAD_DIM = 128
D_XATTN = XATTN_HEADS * XATTN_HEAD_DIM

N_EXPERTS = 8
TOP_K = 2

LANES = 128
SUBLANES = 8
VMEM_LIMIT_BYTES = 56 * 1024 * 1024

ROW_TILE = 512
MM_TM = 2048
MM_TN = 512
MM_VMEM_BUDGET = 44 * 1024 * 1024
FFN_TM = 512
FFN_TF = 512
FFN_SPLIT = 256
MOE_FFN_ROWS = 1024
MOE_TF = 256
COMBINE_TM = 256
MOE_TILE = 512
ATT_TQ = 256
ATT_TK = 512
LOG2E = 1.4426950408889634
CMP_TQ = 256
WKV_T = 64
WKV_LOOKAHEAD = 8


def _params(*sem):
    return pltpu.CompilerParams(dimension_semantics=sem, vmem_limit_bytes=VMEM_LIMIT_BYTES)


def _sigmoid(x):
    return 1.0 / (1.0 + jnp.exp(-x))


def _silu(x):
    return x * _sigmoid(x)


def _softplus(x):
    return jnp.maximum(x, 0.0) + jnp.log(1.0 + jnp.exp(-jnp.abs(x)))


def _split3(x):
    h = x.astype(BF16)
    r = x - h.astype(F32)
    m = r.astype(BF16)
    l = (r - m.astype(F32)).astype(BF16)
    return h, m, l


def _dot(a, b):
    return jnp.dot(a, b, preferred_element_type=F32)


def _dot_nt(a, b):
    return lax.dot_general(a, b, (((1,), (1,)), ((), ())), preferred_element_type=F32)


def _dot_f32_sel(x, e):
    h, m, l = _split3(x)
    return _dot(h, e) + _dot(m, e) + _dot(l, e)


def _sel_dot_f32(e, x):
    h, m, l = _split3(x)
    return _dot(e, h) + _dot(e, m) + _dot(e, l)


def _rms(x, eps=NORM_EPS):
    return x * lax.rsqrt(jnp.mean(x * x, axis=-1, keepdims=True) + eps)


def _rmsnorm_kernel(x_ref, g_ref, o_ref):
    o_ref[...] = (_rms(x_ref[...]) * g_ref[...]).astype(o_ref.dtype)


def rmsnorm(x, gain, out_dtype):
    m, d = x.shape
    tm = min(ROW_TILE, m)
    return pl.pallas_call(
        _rmsnorm_kernel,
        grid=(m // tm,),
        in_specs=[pl.BlockSpec((tm, d), lambda i: (i, 0)), pl.BlockSpec((1, d), lambda i: (0, 0))],
        out_specs=pl.BlockSpec((tm, d), lambda i: (i, 0)),
        out_shape=jax.ShapeDtypeStruct((m, d), out_dtype),
        compiler_params=_params("parallel"),
        name="rmsnorm",
    )(x, gain.reshape(1, d))


def _mm_kernel(*refs, n_pairs, has_res):
    o_ref = refs[-1]
    acc = None
    for p in range(n_pairs):
        d = _dot(refs[2 * p][...], refs[2 * p + 1][...])
        acc = d if acc is None else acc + d
    if has_res:
        acc = acc + refs[2 * n_pairs][...]
    o_ref[...] = acc.astype(o_ref.dtype)


def matmul(pairs, residual=None, out_dtype=F32):
    m = pairs[0][0].shape[0]
    n = pairs[0][1].shape[1]
    k_total = sum(a.shape[1] for a, _ in pairs)
    tm = min(MM_TM if k_total <= D_MODEL else MM_TM // 2, m)
    out_bytes = jnp.dtype(out_dtype).itemsize + (4 if residual is not None else 0)

    def vmem_bytes(tn):
        return 2 * (tm * k_total * 2 + k_total * tn * 2 + tm * tn * out_bytes)

    tn = next((t for t in (2 * MM_TN, MM_TN, 256, LANES) if n % t == 0 and vmem_bytes(t) <= MM_VMEM_BUDGET), LANES)
    in_specs, args = [], []
    for a, w in pairs:
        kk = a.shape[1]
        in_specs += [pl.BlockSpec((tm, kk), lambda i, j: (i, 0)), pl.BlockSpec((kk, tn), lambda i, j: (0, j))]
        args += [a, w]
    if residual is not None:
        in_specs.append(pl.BlockSpec((tm, tn), lambda i, j: (i, j)))
        args.append(residual)
    kern = functools.partial(_mm_kernel, n_pairs=len(pairs), has_res=residual is not None)
    return pl.pallas_call(
        kern,
        grid=(m // tm, n // tn),
        in_specs=in_specs,
        out_specs=pl.BlockSpec((tm, tn), lambda i, j: (i, j)),
        out_shape=jax.ShapeDtypeStruct((m, n), out_dtype),
        compiler_params=_params("parallel", "parallel"),
        name="matmul",
    )(*args)


def _act(x, act):
    if act == "tanh":
        return jnp.tanh(x)
    if act == "sigmoid":
        return _sigmoid(x)
    return x


def _lora_kernel(a_ref, wa_ref, wb_ref, bias_ref, o_ref, *, act_mid, act_out):
    mid = _act(_dot(a_ref[...], wa_ref[...]), act_mid).astype(BF16)
    o_ref[...] = _act(_dot(mid, wb_ref[...]) + bias_ref[...], act_out).astype(o_ref.dtype)


def lora(a, w_a, w_b, bias, act_mid, act_out, out_dtype):
    m, d = a.shape
    r = w_a.shape[1]
    n = w_b.shape[1]
    tm = min(ROW_TILE, m)
    return pl.pallas_call(
        functools.partial(_lora_kernel, act_mid=act_mid, act_out=act_out),
        grid=(m // tm,),
        in_specs=[pl.BlockSpec((tm, d), lambda i: (i, 0)), pl.BlockSpec((d, r), lambda i: (0, 0)),
                  pl.BlockSpec((r, n), lambda i: (0, 0)), pl.BlockSpec((1, n), lambda i: (0, 0))],
        out_specs=pl.BlockSpec((tm, n), lambda i: (i, 0)),
        out_shape=jax.ShapeDtypeStruct((m, n), out_dtype),
        compiler_params=_params("parallel"),
        name="lora",
    )(a, w_a, w_b, bias.reshape(1, n).astype(F32))


def _ffn_kernel(x_ref, w1_ref, w3_ref, w2_ref, res_ref, o_ref):
    f = pl.program_id(1)

    @pl.when(f == 0)
    def _():
        o_ref[...] = res_ref[...]

    x = x_ref[...]
    tf = w1_ref.shape[1]
    cols = [slice(c, c + FFN_SPLIT) for c in range(0, tf, FFN_SPLIT)]
    gate_up = [(_dot(x, w1_ref[:, c]), _dot(x, w3_ref[:, c])) for c in cols]
    hs = [(_silu(g) * u).astype(BF16) for g, u in gate_up]
    acc = _dot(hs[0], w2_ref[cols[0], :])
    for h, c in zip(hs[1:], cols[1:]):
        acc = acc + _dot(h, w2_ref[c, :])
    o_ref[...] += acc


def swiglu_ffn(x, w1, w3, w2, residual):
    m, d = x.shape
    ff = w1.shape[1]
    tm = min(FFN_TM, m)
    tf = FFN_TF
    return pl.pallas_call(
        _ffn_kernel,
        grid=(m // tm, ff // tf),
        in_specs=[pl.BlockSpec((tm, d), lambda i, f: (i, 0)),
                  pl.BlockSpec((d, tf), lambda i, f: (0, f)),
                  pl.BlockSpec((d, tf), lambda i, f: (0, f)),
                  pl.BlockSpec((tf, d), lambda i, f: (f, 0)),
                  pl.BlockSpec((tm, d), lambda i, f: (i, 0))],
        out_specs=pl.BlockSpec((tm, d), lambda i, f: (i, 0)),
        out_shape=jax.ShapeDtypeStruct((m, d), F32),
        compiler_params=_params("parallel", "arbitrary"),
        name="swiglu_ffn",
    )(x, w1, w3, w2, residual)


def _ssd_kernel(z_ref, xbc_ref, dt_ref, shift_ref, cw_ref, cb_ref, dtb_ref, alog_ref, dskip_ref, nw_ref, eh_ref,
                o_ref, ext_ref, st_ref):
    L = SSD_CHUNK
    G = SSM_GROUPS
    GW = D_SSM_GROUP
    T = SSD_CONV_TAIL
    c = pl.program_id(1)

    @pl.when(c == 0)
    def _():
        ext_ref[...] = jnp.zeros_like(ext_ref)
        st_ref[...] = jnp.zeros_like(st_ref)

    ext_ref[0:T, :] = ext_ref[L:L + T, :]
    ext_ref[T:L + T, :] = xbc_ref[...]
    shifted = _dot(shift_ref[...], ext_ref[...])
    acc = jnp.zeros((L, D_CONV), F32) + cb_ref[...]
    for k in range(CONV_WIDTH):
        acc = acc + cw_ref[k:k + 1, :] * shifted[k * L:(k + 1) * L, :]
    xc = _silu(acc)

    dt = _softplus(dt_ref[...] + dtb_ref[...])
    a = -jnp.exp(alog_ref[...])
    adt = dt * a
    row = lax.broadcasted_iota(I32, (L, L), 0)
    col = lax.broadcasted_iota(I32, (L, L), 1)
    causal = row >= col
    tril = jnp.where(causal, 1.0, 0.0).astype(BF16)
    a_cum = _sel_dot_f32(tril, adt)
    a_cum_t = a_cum.T
    eh = eh_ref[...]
    acx = _dot_f32_sel(a_cum, eh)
    dtx = _dot_f32_sel(dt, eh)
    a_end = acx[L - 1:L, :]
    xs = xc[:, :D_SSM]
    xdt = xs * dtx
    xdte = (xdt * jnp.exp(a_end - acx)).astype(BF16)
    from_start = jnp.exp(acx)
    lane = lax.broadcasted_iota(I32, (L, LANES), 1)
    lo_half = lane < SSM_HEAD_DIM
    z = z_ref[...].astype(F32)

    for g in range(G):
        bm = xc[:, D_SSM + g * D_STATE:D_SSM + (g + 1) * D_STATE]
        cm = xc[:, D_SSM + G * D_STATE + g * D_STATE:D_SSM + G * D_STATE + (g + 1) * D_STATE]
        bm_b = bm.astype(BF16)
        cm_b = cm.astype(BF16)
        cb = _dot_nt(cm_b, bm_b)
        st = st_ref[g]
        y_off = _dot(cm_b, st.astype(BF16)) * from_start[:, g * GW:(g + 1) * GW]
        pieces = []
        for pp in range(GW // LANES):
            h0 = g * (GW // SSM_HEAD_DIM) + 2 * pp
            ms = []
            for hh in (h0, h0 + 1):
                diff = a_cum[:, hh:hh + 1] - a_cum_t[hh:hh + 1, :]
                dec = jnp.exp(jnp.where(causal, diff, NEG_BIG))
                ms.append((cb * dec).astype(BF16))
            lhs = jnp.concatenate(ms, axis=1)
            c0 = g * GW + pp * LANES
            xp = xdt[:, c0:c0 + LANES]
            rhs = jnp.concatenate([jnp.where(lo_half, xp, 0.0), jnp.where(lo_half, 0.0, xp)], axis=0)
            pieces.append(_dot(lhs, rhs.astype(BF16)))
        y = jnp.concatenate(pieces, axis=1) + y_off
        y = y + xs[:, g * GW:(g + 1) * GW] * dskip_ref[:, g * GW:(g + 1) * GW]
        y = y * _silu(z[:, g * GW:(g + 1) * GW])
        y = _rms(y) * nw_ref[:, g * GW:(g + 1) * GW]
        o_ref[:, g * GW:(g + 1) * GW] = y.astype(o_ref.dtype)
        s_new = _dot(bm.T.astype(BF16), xdte[:, g * GW:(g + 1) * GW])
        st_ref[g] = st * jnp.exp(a_end[:, g * GW:(g + 1) * GW]) + s_new


def ssd_group(proj, z_col, xbc_col, small, conv_w, conv_b, dt_bias, a_log, d_skip, norm_w, bsz, seq):
    m = bsz * seq
    L = SSD_CHUNK
    nc = seq // L
    pad = LANES - SSM_HEADS
    eh = np.zeros((LANES, D_SSM), np.float32)
    for h in range(SSM_HEADS):
        eh[h, h * SSM_HEAD_DIM:(h + 1) * SSM_HEAD_DIM] = 1.0
    shift = np.zeros((CONV_WIDTH * L, SSD_EXT_ROWS), np.float32)
    for k in range(CONV_WIDTH):
        shift[k * L + np.arange(L), np.arange(L) + SSD_CONV_TAIL - (CONV_WIDTH - 1) + k] = 1.0
    row_spec = lambda w, col=0: pl.BlockSpec((L, w), lambda b, c: (b * nc + c, col))
    const = lambda r, w: pl.BlockSpec((r, w), lambda b, c: (0, 0))
    return pl.pallas_call(
        _ssd_kernel,
        grid=(bsz, nc),
        in_specs=[row_spec(D_SSM, z_col), row_spec(D_CONV, xbc_col), row_spec(LANES),
                  const(CONV_WIDTH * L, SSD_EXT_ROWS),
                  const(CONV_WIDTH, D_CONV), const(1, D_CONV), const(1, LANES), const(1, LANES),
                  const(1, D_SSM), const(1, D_SSM), const(LANES, D_SSM)],
        out_specs=row_spec(D_SSM),
        out_shape=jax.ShapeDtypeStruct((m, D_SSM), BF16),
        scratch_shapes=[pltpu.VMEM((SSD_EXT_ROWS, D_CONV), BF16),
                        pltpu.VMEM((SSM_GROUPS, D_STATE, D_SSM_GROUP), F32)],
        compiler_params=_params("parallel", "arbitrary"),
        name="ssd_group",
    )(proj, proj, small, jnp.asarray(shift, BF16), conv_w.T, conv_b.reshape(1, D_CONV),
      jnp.pad(dt_bias, (0, pad)).reshape(1, LANES), jnp.pad(a_log, (0, pad)).reshape(1, LANES),
      jnp.repeat(d_skip, SSM_HEAD_DIM).reshape(1, D_SSM), norm_w.reshape(1, D_SSM), jnp.asarray(eh, BF16))


def _rope_norm_kernel(pos_ref, invf_ref, q_ref, kv_ref, qg_ref, ksg_ref, kwg_ref, qo_ref, kso_ref, kwo_ref):
    ang = pos_ref[...] * invf_ref[...]
    cos = jnp.cos(ang)
    sin = jnp.sin(ang)
    lane = lax.broadcasted_iota(I32, ang.shape, 1)
    first = lane < ROPE_DIM // 2
    sin_signed = jnp.where(first, -sin, sin)

    def head(x, gain):
        y = _rms(x.astype(F32)) * gain
        partner = jnp.where(first, pltpu.roll(y, LANES - ROPE_DIM // 2, axis=1), pltpu.roll(y, ROPE_DIM // 2, axis=1))
        return y * cos + partner * sin_signed

    hd = NSA_HEAD_DIM
    scale = hd ** -0.5 * LOG2E
    for h in range(NSA_HEADS):
        qo_ref[:, h * hd:(h + 1) * hd] = (head(q_ref[:, h * hd:(h + 1) * hd], qg_ref[...]) * scale).astype(qo_ref.dtype)
    for h in range(NSA_KV_HEADS):
        ks = kv_ref[:, 2 * D_NSA_KV + h * hd:2 * D_NSA_KV + (h + 1) * hd]
        kw = kv_ref[:, 4 * D_NSA_KV + h * hd:4 * D_NSA_KV + (h + 1) * hd]
        kso_ref[:, h * hd:(h + 1) * hd] = head(ks, ksg_ref[...]).astype(kso_ref.dtype)
        kwo_ref[:, h * hd:(h + 1) * hd] = head(kw, kwg_ref[...]).astype(kwo_ref.dtype)


def rope_norm(posf, proj, q_col, kv_col, q_gain, ks_gain, kw_gain):
    m = proj.shape[0]
    tm = min(ROW_TILE, m)
    half = ROPE_DIM // 2
    inv = np.exp(-math.log(ROPE_THETA) * np.arange(0, ROPE_DIM, 2, dtype=np.float32) / ROPE_DIM).astype(np.float32)
    invf = np.zeros((1, LANES), np.float32)
    invf[0, :half] = inv
    invf[0, half:ROPE_DIM] = inv
    row = lambda w, col=0: pl.BlockSpec((tm, w), lambda i: (i, col))
    const = pl.BlockSpec((1, LANES), lambda i: (0, 0))
    return pl.pallas_call(
        _rope_norm_kernel,
        grid=(m // tm,),
        in_specs=[row(1), const, row(D_NSA, q_col), row(6 * D_NSA_KV, kv_col), const, const, const],
        out_specs=[row(D_NSA), row(D_NSA_KV), row(D_NSA_KV)],
        out_shape=[jax.ShapeDtypeStruct((m, D_NSA), BF16), jax.ShapeDtypeStruct((m, D_NSA_KV), BF16),
                   jax.ShapeDtypeStruct((m, D_NSA_KV), BF16)],
        compiler_params=_params("parallel"),
        name="rope_norm",
    )(posf, jnp.asarray(invf), proj, proj, q_gain.reshape(1, LANES), ks_gain.reshape(1, LANES), kw_gain.reshape(1, LANES))


def _compress_kernel(uk_ref, uv_ref, pek_ref, pev_ref, wk1_ref, wk2_ref, wv1_ref, wv2_ref, g_ref, kc_ref, vc_ref):
    def mlp(u, pe_ref, w1_ref, w2_ref):
        u = u.astype(F32)
        n = u.shape[0]
        h1 = _dot((u + pe_ref[0:1, :]).astype(BF16), w1_ref[0])
        h2 = _dot((u + pe_ref[1:2, :]).astype(BF16), w1_ref[1])
        pre = h1 + pltpu.roll(h2, n - 1, axis=0)
        return _dot(_silu(pre).astype(BF16), w2_ref[...])

    kc = mlp(uk_ref[0, 0], pek_ref, wk1_ref, wk2_ref)
    kc_ref[0, 0] = _rms(kc) * g_ref[...]
    vc_ref[0, 0] = mlp(uv_ref[0, 0], pev_ref, wv1_ref, wv2_ref)


def compress(uk, uv, pe_k, pe_v, wk1, wk2, wv1, wv2, kc_gain):
    bsz, nkv, nch, width = uk.shape
    half = CMP_STRIDE * NSA_HEAD_DIM
    u_spec = pl.BlockSpec((1, 1, nch, width), lambda b, g: (b, g, 0, 0))
    o_spec = pl.BlockSpec((1, 1, nch, NSA_HEAD_DIM), lambda b, g: (b, g, 0, 0))
    c2 = lambda s: pl.BlockSpec(s, lambda b, g: (0, 0))
    c3 = lambda s: pl.BlockSpec(s, lambda b, g: (0, 0, 0))
    o_shape = jax.ShapeDtypeStruct((bsz, nkv, nch, NSA_HEAD_DIM), F32)
    return pl.pallas_call(
        _compress_kernel,
        grid=(bsz, nkv),
        in_specs=[u_spec, u_spec, c2((2, half)), c2((2, half)),
                  c3((2, half, CMP_HIDDEN)), c2((CMP_HIDDEN, NSA_HEAD_DIM)),
                  c3((2, half, CMP_HIDDEN)), c2((CMP_HIDDEN, NSA_HEAD_DIM)), c2((1, NSA_HEAD_DIM))],
        out_specs=[o_spec, o_spec],
        out_shape=[o_shape, o_shape],
        compiler_params=_params("parallel", "parallel"),
        name="nsa_compress",
    )(uk, uv, pe_k.reshape(2, half), pe_v.reshape(2, half),
      wk1.reshape(2, half, CMP_HIDDEN).astype(BF16), wk2.astype(BF16),
      wv1.reshape(2, half, CMP_HIDDEN).astype(BF16), wv2.astype(BF16), kc_gain.reshape(1, NSA_HEAD_DIM))


def _stack_heads(q):
    return jnp.concatenate([q[:, r * NSA_HEAD_DIM:(r + 1) * NSA_HEAD_DIM] for r in range(NSA_Q_PER_KV)], axis=0)


def _unstack_heads(o, tq):
    return jnp.concatenate([o[r * tq:(r + 1) * tq, :] for r in range(NSA_Q_PER_KV)], axis=1)


def _cmp_select_kernel(q_ref, kc_ref, vc_ref, ovt_ref, o_ref, sel_ref, *, tq, n_blk):
    R = NSA_Q_PER_KV
    q0 = pl.program_id(2) * tq
    q4 = _stack_heads(q_ref[...])
    kc = kc_ref[0, 0].astype(BF16)
    ncmp = kc.shape[0]
    s = _dot_nt(q4, kc)
    t_row = q0 + lax.broadcasted_iota(I32, (tq, ncmp), 0)
    c_col = lax.broadcasted_iota(I32, (tq, ncmp), 1)
    visf = jnp.where((c_col * CMP_STRIDE + CMP_BLOCK - 1) <= t_row, 1.0, 0.0)
    vis = jnp.concatenate([visf] * R, axis=0) > 0.5
    s = jnp.where(vis, s, NEG_BIG)
    mx = jnp.max(s, axis=-1, keepdims=True)
    e = jnp.where(vis, jnp.exp2(s - mx), 0.0)
    den = jnp.sum(e, axis=-1, keepdims=True)
    p = e / jnp.maximum(den, 1e-30)
    o = _dot(p.astype(BF16), vc_ref[0, 0].astype(BF16))
    o_ref[...] = _unstack_heads(o, tq)

    psum = p[0:tq]
    for r in range(1, R):
        psum = psum + p[r * tq:(r + 1) * tq]
    h, m, l = _split3(psum)
    ovt = ovt_ref[...]
    imp = _dot_nt(ovt, h) + _dot_nt(ovt, m) + _dot_nt(ovt, l)
    jdx = lax.broadcasted_iota(I32, (n_blk, tq), 0)
    t_lane = q0 + lax.broadcasted_iota(I32, (n_blk, tq), 1)
    cur = t_lane // SEL_BLOCK
    causal = jdx <= cur
    forced = ((jdx == 0) | (jdx >= cur - 1)) & causal
    score = jnp.where(forced, jnp.inf, jnp.where(causal, imp, -jnp.inf))
    rank = jnp.zeros((n_blk, tq), F32)
    for jp in range(n_blk):
        rowv = score[jp:jp + 1, :]
        beats = (rowv > score) | ((rowv == score) & (jdx > jp))
        rank = rank + jnp.where(beats, 1.0, 0.0)
    sel = jnp.where((rank < float(min(N_SELECT, n_blk))) & causal, 1.0, 0.0)
    if n_blk < LANES:
        sel = jnp.concatenate([sel, jnp.zeros((LANES - n_blk, tq), F32)], axis=0)
    sel_ref[0, 0] = sel.astype(sel_ref.dtype)


def cmp_select(qn, kc, vc, bsz, seq):
    m = bsz * seq
    G = NSA_KV_HEADS
    tq = CMP_TQ
    nq = seq // tq
    ncmp = kc.shape[2]
    n_blk = seq // SEL_BLOCK
    assert n_blk <= LANES and tq % LANES == 0
    c0 = np.arange(ncmp)[None, :] * CMP_STRIDE
    s0 = np.arange(n_blk)[:, None] * SEL_BLOCK
    ov = np.clip(np.minimum(c0 + CMP_BLOCK, s0 + SEL_BLOCK) - np.maximum(c0, s0), 0, None) / CMP_STRIDE
    n_cmp_valid = (seq - CMP_BLOCK) // CMP_STRIDE + 1
    ov[:, n_cmp_valid:] = 0.0
    kv_spec = pl.BlockSpec((1, 1, ncmp, NSA_HEAD_DIM), lambda b, g, i: (b, g, 0, 0))
    kern = functools.partial(_cmp_select_kernel, tq=tq, n_blk=n_blk)
    return pl.pallas_call(
        kern,
        grid=(bsz, G, nq),
        in_specs=[pl.BlockSpec((tq, NSA_Q_PER_KV * NSA_HEAD_DIM), lambda b, g, i: (b * nq + i, g)),
                  kv_spec, kv_spec, pl.BlockSpec((n_blk, ncmp), lambda b, g, i: (0, 0))],
        out_specs=[pl.BlockSpec((tq, NSA_Q_PER_KV * NSA_HEAD_DIM), lambda b, g, i: (b * nq + i, g)),
                   pl.BlockSpec((1, 1, LANES, tq), lambda b, g, i: (b, g, 0, i))],
        out_shape=[jax.ShapeDtypeStruct((m, D_NSA), F32), jax.ShapeDtypeStruct((bsz, G, LANES, seq), BF16)],
        compiler_params=_params("parallel", "parallel", "parallel"),
        name="nsa_cmp_select",
    )(qn, kc, vc, jnp.asarray(ov, BF16))


def _masked_attn_kernel(q_ref, k_ref, vt_ref, *rest, mode, tq, tk):
    if mode == "sel":
        selt_ref, o_ref = rest
    else:
        (o_ref,) = rest
    R = NSA_Q_PER_KV
    hd = NSA_HEAD_DIM
    q0 = pl.program_id(2) * tq
    key_row = lax.broadcasted_iota(I32, (tk, tq), 0)
    t_lane = q0 + lax.broadcasted_iota(I32, (tk, tq), 1)
    hi = (q0 + tq - 1) // tk + 1
    q = q_ref[...]
    if mode == "sel":
        key_row_b = lax.broadcasted_iota(I32, (tk, LANES), 0)
        blk_lane = lax.broadcasted_iota(I32, (tk, LANES), 1)
        penalty = ((1.0 - selt_ref[0, 0].astype(F32)) * NEG_BIG).astype(BF16)
        rhs = [jnp.concatenate([q[:, r * hd:(r + 1) * hd].astype(F32).T.astype(BF16), penalty], axis=0)
               for r in range(R)]
        lo = 0
    else:
        lo = jnp.maximum(q0 - (WINDOW - 1), 0) // tk

    def body(kt, carry, diagonal=True):
        k0 = kt * tk
        k_t = k_ref[pl.ds(pl.multiple_of(k0, tk), tk), :]
        v_t = vt_ref[0, 0, kt]
        key = k0 + key_row
        if mode == "sel":
            expand = jnp.where((k0 + key_row_b) // SEL_BLOCK == blk_lane, 1.0, 0.0).astype(BF16)
            lhs = jnp.concatenate([k_t, expand], axis=1)
            ss = [_dot(lhs, rhs[r]) for r in range(R)]
            if diagonal:
                ss = [jnp.where(key <= t_lane, s, NEG_BIG) for s in ss]
        else:
            bias = jnp.where((key <= t_lane) & (key > t_lane - WINDOW), 0.0, NEG_BIG)
            ss = [_dot_nt(k_t, q[:, r * hd:(r + 1) * hd]) + bias for r in range(R)]
        stats = []
        for r in range(R):
            m_prev, l_prev = carry[3 * r], carry[3 * r + 1]
            m_new = jnp.maximum(m_prev, jnp.max(ss[r], axis=0, keepdims=True))
            p = jnp.exp2(ss[r] - m_new)
            alpha = jnp.exp2(m_prev - m_new)
            stats.append((m_new, alpha * l_prev + jnp.sum(p, axis=0, keepdims=True), alpha, p.astype(BF16)))
        new = []
        for r in range(R):
            m_new, l_new, alpha, p = stats[r]
            new += [m_new, l_new, alpha * carry[3 * r + 2] + _dot(v_t, p)]
        return tuple(new)

    init = (jnp.full((1, tq), NEG_BIG, F32), jnp.zeros((1, tq), F32), jnp.zeros((hd, tq), F32)) * R
    if mode == "sel":
        out = lax.fori_loop(lo, hi - 1, functools.partial(body, diagonal=False), init)
        out = body(hi - 1, out)
    else:
        out = lax.fori_loop(lo, hi, body, init)
    o_ref[...] = jnp.concatenate([(out[3 * r + 2] / out[3 * r + 1]).T for r in range(R)], axis=1)


def masked_attention(qn, k, k_col0, vt, bsz, seq, mode, selt=None):
    m = bsz * seq
    G = NSA_KV_HEADS
    tq = min(ATT_TQ, seq)
    tk = vt.shape[-1]
    assert tk % tq == 0
    nq = seq // tq
    hd = NSA_HEAD_DIM
    q_spec = pl.BlockSpec((tq, NSA_Q_PER_KV * hd), lambda b, g, i: (b * nq + i, g))
    in_specs = [q_spec,
                pl.BlockSpec((seq, hd), lambda b, g, i: (b, k_col0 + g)),
                pl.BlockSpec((1, 1, seq // tk, hd, tk), lambda b, g, i: (b, g, 0, 0, 0))]
    args = [qn, k, vt]
    if mode == "sel":
        in_specs.append(pl.BlockSpec((1, 1, LANES, tq), lambda b, g, i: (b, g, 0, i)))
        args.append(selt)
    return pl.pallas_call(
        functools.partial(_masked_attn_kernel, mode=mode, tq=tq, tk=tk),
        grid=(bsz, G, nq),
        in_specs=in_specs,
        out_specs=q_spec,
        out_shape=jax.ShapeDtypeStruct((m, D_NSA), F32),
        compiler_params=_params("parallel", "parallel", "arbitrary"),
        name="nsa_attn_" + mode,
    )(*args)


def _nsa_combine_kernel(g_ref, e_ref, oc_ref, os_ref, ow_ref, o_ref):
    gate = _sigmoid(g_ref[...])
    gx = _dot_f32_sel(gate, e_ref[...])
    y = gx[:, :D_NSA] * oc_ref[...] + gx[:, D_NSA:2 * D_NSA] * os_ref[...] + gx[:, 2 * D_NSA:] * ow_ref[...]
    o_ref[...] = y.astype(o_ref.dtype)


def nsa_combine(small, o_cmp, o_sel, o_win):
    m = small.shape[0]
    tm = min(COMBINE_TM, m)
    e = np.zeros((LANES, 3 * D_NSA), np.float32)
    for h in range(NSA_HEADS):
        for br in range(3):
            e[SSM_HEADS + 3 * h + br, br * D_NSA + h * NSA_HEAD_DIM:br * D_NSA + (h + 1) * NSA_HEAD_DIM] = 1.0
    row = lambda w: pl.BlockSpec((tm, w), lambda i: (i, 0))
    return pl.pallas_call(
        _nsa_combine_kernel,
        grid=(m // tm,),
        in_specs=[row(LANES), pl.BlockSpec((LANES, 3 * D_NSA), lambda i: (0, 0)), row(D_NSA), row(D_NSA), row(D_NSA)],
        out_specs=row(D_NSA),
        out_shape=jax.ShapeDtypeStruct((m, D_NSA), BF16),
        compiler_params=_params("parallel"),
        name="nsa_combine",
    )(small, jnp.asarray(e, BF16), o_cmp, o_sel, o_win)


def _xattn_kernel(q_ref, kv_ref, qg_ref, kg_ref, o_ref):
    hd = XATTN_HEAD_DIM
    scale = hd ** -0.5
    outs = []
    for h in range(XATTN_HEADS):
        q = (_rms(q_ref[:, h * hd:(h + 1) * hd]) * qg_ref[...]).astype(BF16)
        k = (_rms(kv_ref[:, h * hd:(h + 1) * hd]) * kg_ref[...]).astype(BF16)
        v = kv_ref[:, D_XATTN + h * hd:D_XATTN + (h + 1) * hd].astype(BF16)
        s = _dot_nt(q, k) * scale
        e = jnp.exp(s - jnp.max(s, axis=-1, keepdims=True))
        p = e / jnp.sum(e, axis=-1, keepdims=True)
        outs.append(_dot(p.astype(BF16), v))
    o_ref[...] = jnp.concatenate(outs, axis=1).astype(o_ref.dtype)


def xattn_core(q, kv, q_gain, k_gain, bsz, seq):
    m = bsz * seq
    n_mem = kv.shape[0] // bsz
    tm = min(ROW_TILE, seq)
    nt = seq // tm
    return pl.pallas_call(
        _xattn_kernel,
        grid=(bsz, nt),
        in_specs=[pl.BlockSpec((tm, D_XATTN), lambda b, i: (b * nt + i, 0)),
                  pl.BlockSpec((n_mem, 2 * D_XATTN), lambda b, i: (b, 0)),
                  pl.BlockSpec((1, XATTN_HEAD_DIM), lambda b, i: (0, 0)),
                  pl.BlockSpec((1, XATTN_HEAD_DIM), lambda b, i: (0, 0))],
        out_specs=pl.BlockSpec((tm, D_XATTN), lambda b, i: (b * nt + i, 0)),
        out_shape=jax.ShapeDtypeStruct((m, D_XATTN), BF16),
        compiler_params=_params("parallel", "parallel"),
        name="xattn_core",
    )(q, kv, q_gain.reshape(1, XATTN_HEAD_DIM), k_gain.reshape(1, XATTN_HEAD_DIM))


def memory_xattn(h, mem2d, norm_x, norm_m, wq, wkv, wo, q_gain, k_gain, bsz, seq):
    hq = rmsnorm(h, norm_x, BF16)
    q = matmul([(hq, wq.astype(BF16))])
    mem_n = rmsnorm(mem2d, norm_m, BF16)
    kv = matmul([(mem_n, wkv.astype(BF16))])
    o = xattn_core(q, kv, q_gain, k_gain, bsz, seq)
    return matmul([(o, wo.astype(BF16))], residual=h)


def _mix_kernel(h_ref, hp_ref, g_ref, mu_ref, *rest, tm, tiles_per_seq):
    outs, ext_ref = rest[:-1], rest[-1]
    i = pl.program_id(0)
    hn = _rms(h_ref[...]) * g_ref[...]
    hp = _rms(hp_ref[...]) * g_ref[...]
    hp = jnp.where(i % tiles_per_seq == 0, 0.0, hp)
    ext_ref[0:SUBLANES, :] = hp
    ext_ref[SUBLANES:tm + SUBLANES, :] = hn
    xx = ext_ref[SUBLANES - 1:tm + SUBLANES - 1, :] - hn
    for j, o_ref in enumerate(outs):
        o_ref[...] = (hn + xx * mu_ref[j:j + 1, :]).astype(o_ref.dtype)


def rwkv_mix(h, gain, mu, seq):
    m, d = h.shape
    tm = min(ROW_TILE, seq)
    n_mix = mu.shape[0]
    kern = functools.partial(_mix_kernel, tm=tm, tiles_per_seq=seq // tm)
    row = pl.BlockSpec((tm, d), lambda i: (i, 0))
    return pl.pallas_call(
        kern,
        grid=(m // tm,),
        in_specs=[row, pl.BlockSpec((SUBLANES, d), lambda i: (jnp.maximum(i * (tm // SUBLANES) - 1, 0), 0)),
                  pl.BlockSpec((1, d), lambda i: (0, 0)), pl.BlockSpec((n_mix, d), lambda i: (0, 0))],
        out_specs=[row] * n_mix,
        out_shape=[jax.ShapeDtypeStruct((m, d), BF16)] * n_mix,
        scratch_shapes=[pltpu.VMEM((tm + SUBLANES, d), F32)],
        compiler_params=_params("parallel"),
        name="rwkv_mix",
    )(h, h, gain.reshape(1, d), mu)


WKV_LANE_GROUP = LANES // 4
WKV_ROWS_PER_BATCH = D_MODEL // LANES


def _wkv_fused_kernel(r_ref, k_ref, v_ref, w_ref, a_ref, g_ref, kkp_ref, kap_ref, rkp_ref, lnw_ref, lnb_ref,
                      o_ref, s_ref, wd_s, kh_s, kk_s, b_s, r_s, v_s, g_s, *, steps):
    n = RWKV_HEAD_DIM
    nb = LANES // WKV_LANE_GROUP
    halves = WKV_ROWS_PER_BATCH // SUBLANES
    ahead = min(WKV_LOOKAHEAD, steps)

    @pl.when(pl.program_id(0) == 0)
    def _():
        s_ref[...] = jnp.zeros_like(s_ref)

    def lane_group(cnt):
        return lax.broadcasted_iota(I32, (cnt, SUBLANES, LANES), 2) // WKV_LANE_GROUP

    def to_scan(x_ref, t0, cnt):
        grp = lane_group(cnt)
        rows = [x_ref[b, pl.ds(t0, cnt)].astype(F32) for b in range(nb)]
        pieces = []
        for n4 in range(nb):
            for v in range(halves):
                acc = None
                for b in range(nb):
                    x = rows[b][:, SUBLANES * v:SUBLANES * (v + 1), :]
                    shift = (WKV_LANE_GROUP * (b - n4)) % LANES
                    if shift:
                        x = pltpu.roll(x, shift, axis=2)
                    acc = x if acc is None else jnp.where(grp == b, x, acc)
                pieces.append(acc)
        return pieces

    def prepare(t0, cnt):
        rs, ks, vs, ws, as_, gs = (to_scan(x, t0, cnt) for x in (r_ref, k_ref, v_ref, w_ref, a_ref, g_ref))
        kks = [ks[u] * kkp_ref[SUBLANES * u:SUBLANES * (u + 1), :] for u in range(len(ks))]
        ss = kks[0] * kks[0]
        for u in range(1, len(kks)):
            ss = ss + kks[u] * kks[u]
        inv_norm = 1.0 / jnp.maximum(jnp.sqrt(jnp.sum(ss, axis=1, keepdims=True)), 1e-12)
        for u in range(len(ks)):
            rows = slice(SUBLANES * u, SUBLANES * (u + 1))
            at = (pl.ds(t0, cnt), rows, slice(None))
            kk = kks[u] * inv_norm
            wd_s[at] = jnp.exp(-jnp.exp(-_softplus(-ws[u]) - 0.5))
            kk_s[at] = kk
            b_s[at] = kk * as_[u]
            kh_s[at] = ks[u] * (1.0 + (as_[u] - 1.0) * kap_ref[rows, :])
            r_s[at] = rs[u]
            v_s[at] = vs[u]
            g_s[at] = gs[u]

    def finish(t, y):
        inv_n = 1.0 / n
        mean = jnp.sum(y, axis=0, keepdims=True) * inv_n
        yc = y - mean
        var = jnp.sum(yc * yc, axis=0, keepdims=True) * inv_n
        bonus = jnp.sum(r_s[t] * kh_s[t] * rkp_ref[...], axis=0, keepdims=True) * v_s[t]
        o = (yc * lax.rsqrt(var + RWKV_GN_EPS) * lnw_ref[...] + lnb_ref[...] + bonus) * g_s[t]
        grp = lane_group(1)[0]
        for b in range(nb):
            parts = []
            for v in range(halves):
                acc = None
                for n4 in range(nb):
                    u = n4 * halves + v
                    x = o[SUBLANES * u:SUBLANES * (u + 1), :]
                    shift = (WKV_LANE_GROUP * (n4 - b)) % LANES
                    if shift:
                        x = pltpu.roll(x, shift, axis=1)
                    acc = x if acc is None else jnp.where(grp == n4, x, acc)
                parts.append(acc)
            o_ref[b, pl.ds(t, 1)] = jnp.concatenate(parts, axis=0).astype(o_ref.dtype)[None]

    prepare(0, ahead)
    q0 = jnp.zeros((n, LANES), F32)
    for j in range(n):
        q0 = q0 + s_ref[j] * kk_s[0, j:j + 1, :]

    def step(t, q):
        tn = jnp.minimum(t + 1, steps - 1)
        v = v_s[t]
        y = jnp.zeros((n, LANES), F32)
        qn = jnp.zeros((n, LANES), F32)
        for j in range(n):
            s_new = s_ref[j] * wd_s[t, j:j + 1, :] - q * b_s[t, j:j + 1, :] + v * kh_s[t, j:j + 1, :]
            s_ref[j] = s_new
            y = y + s_new * r_s[t, j:j + 1, :]
            qn = qn + s_new * kk_s[tn, j:j + 1, :]
        finish(t, y)
        prepare(jnp.minimum(t + ahead, steps - 1), 1)
        return qn

    lax.fori_loop(0, steps, step, q0)


def _wkv_column_order():
    c, n4, h = np.meshgrid(np.arange(WKV_ROWS_PER_BATCH), np.arange(LANES // WKV_LANE_GROUP), np.arange(RWKV_HEADS),
                           indexing="ij")
    return (h * RWKV_HEAD_DIM + n4 * WKV_ROWS_PER_BATCH + c).reshape(-1)


def _scan_tile(p, bsz):
    return jnp.tile(p.reshape(RWKV_HEADS, RWKV_HEAD_DIM).T, (1, bsz))


def wkv_fused(r, k, v, w_raw, a_sig, g, k_k, k_a, r_k, ln_w, ln_b, bsz, seq):
    assert bsz * RWKV_HEADS == LANES and RWKV_HEADS == WKV_LANE_GROUP
    steps = min(WKV_T, seq)
    view = lambda x: x.reshape(bsz, seq, WKV_ROWS_PER_BATCH, LANES)
    blk = pl.BlockSpec((bsz, steps, WKV_ROWS_PER_BATCH, LANES), lambda i: (0, i, 0, 0))
    par = pl.BlockSpec((RWKV_HEAD_DIM, LANES), lambda i: (0, 0))
    scr = pltpu.VMEM((steps, RWKV_HEAD_DIM, LANES), F32)
    out = pl.pallas_call(
        functools.partial(_wkv_fused_kernel, steps=steps),
        grid=(seq // steps,),
        in_specs=[blk] * 6 + [par] * 5,
        out_specs=blk,
        out_shape=jax.ShapeDtypeStruct((bsz, seq, WKV_ROWS_PER_BATCH, LANES), BF16),
        scratch_shapes=[pltpu.VMEM((RWKV_HEAD_DIM, RWKV_HEAD_DIM, LANES), F32)] + [scr] * 7,
        compiler_params=_params("arbitrary"),
        name="wkv_fused",
    )(view(r), view(k), view(v), view(w_raw), view(a_sig), view(g),
      _scan_tile(k_k, bsz), _scan_tile(k_a, bsz), _scan_tile(r_k, bsz), _scan_tile(ln_w, bsz), _scan_tile(ln_b, bsz))
    return out.reshape(bsz * seq, D_MODEL)


def _router_kernel(h_ref, g_ref, wr_ref, info_ref, tile_cnt_ref, total_ref, carry_ref, *, tt):
    E = N_EXPERTS
    i = pl.program_id(0)

    @pl.when(i == 0)
    def _():
        carry_ref[...] = jnp.zeros_like(carry_ref)

    xh, xm, xl = _split3(_rms(h_ref[...]) * g_ref[...])
    wh, wm, wl = _split3(wr_ref[...])
    logits = (_dot_nt(wh, xh) + _dot_nt(wh, xm) + _dot_nt(wm, xh)
              + _dot_nt(wh, xl) + _dot_nt(wl, xh) + _dot_nt(wm, xm))
    eidx = lax.broadcasted_iota(I32, (E, tt), 0)
    v1 = jnp.max(logits, axis=0, keepdims=True)
    i1 = jnp.min(jnp.where(logits == v1, eidx, E), axis=0, keepdims=True)
    rest = jnp.where(eidx == i1, -jnp.inf, logits)
    v2 = jnp.max(rest, axis=0, keepdims=True)
    i2 = jnp.min(jnp.where(rest == v2, eidx, E), axis=0, keepdims=True)
    e2 = jnp.exp(v2 - v1)
    g1 = 1.0 / (1.0 + e2)
    g2 = e2 / (1.0 + e2)
    hit1 = eidx == i1
    hit2 = eidx == i2
    onehot = jnp.where(hit1 | hit2, 1.0, 0.0)
    rr = lax.broadcasted_iota(I32, (tt, tt), 0)
    cc = lax.broadcasted_iota(I32, (tt, tt), 1)
    before = jnp.where(rr < cc, 1.0, 0.0).astype(BF16)
    carry = carry_ref[...]
    carry_wide = jnp.concatenate([carry] * (tt // LANES), axis=1)
    cum = _dot(onehot.astype(BF16), before) + carry_wide
    rank1 = jnp.sum(jnp.where(hit1, cum, 0.0), axis=0, keepdims=True)
    rank2 = jnp.sum(jnp.where(hit2, cum, 0.0), axis=0, keepdims=True)
    zero = jnp.zeros((1, tt), F32)
    info_ref[...] = jnp.concatenate([i1.astype(F32), i2.astype(F32), rank1, rank2, g1, g2, zero, zero], axis=0)
    tile_cnt_ref[0] = carry
    carry = carry + _dot(onehot.astype(BF16), jnp.ones((tt, LANES), BF16))
    carry_ref[...] = carry
    total_ref[...] = carry


def moe_router(h, norm_gain, w_router):
    m, d = h.shape
    tt = min(MOE_TILE, m)
    nt = m // tt
    return pl.pallas_call(
        functools.partial(_router_kernel, tt=tt),
        grid=(nt,),
        in_specs=[pl.BlockSpec((tt, d), lambda i: (i, 0)), pl.BlockSpec((1, d), lambda i: (0, 0)),
                  pl.BlockSpec((N_EXPERTS, d), lambda i: (0, 0))],
        out_specs=[pl.BlockSpec((SUBLANES, tt), lambda i: (0, i)),
                   pl.BlockSpec((1, N_EXPERTS, LANES), lambda i: (i, 0, 0)),
                   pl.BlockSpec((N_EXPERTS, LANES), lambda i: (0, 0))],
        out_shape=[jax.ShapeDtypeStruct((SUBLANES, m), F32), jax.ShapeDtypeStruct((nt, N_EXPERTS, LANES), F32),
                   jax.ShapeDtypeStruct((N_EXPERTS, LANES), F32)],
        scratch_shapes=[pltpu.VMEM((N_EXPERTS, LANES), F32)],
        compiler_params=_params("arbitrary"),
        name="moe_router",
    )(h, norm_gain.reshape(1, d), w_router.T)


def _slots_kernel(start_ref, info_ref, o_ref):
    info = info_ref[...]
    e1, e2, r1, r2 = info[0:1], info[1:2], info[2:3], info[3:4]
    s1 = r1
    s2 = r2
    for e in range(N_EXPERTS):
        st = start_ref[e].astype(F32)
        s1 = s1 + jnp.where(e1 == float(e), st, 0.0)
        s2 = s2 + jnp.where(e2 == float(e), st, 0.0)
    o_ref[...] = jnp.concatenate([s1, s2, info[4:5], info[5:6], info[6:8], info[6:8]], axis=0)


def moe_slots(info, slot_start):
    m = info.shape[1]
    tt = min(2048, m)
    return pl.pallas_call(
        _slots_kernel,
        grid_spec=pltpu.PrefetchScalarGridSpec(
            num_scalar_prefetch=1, grid=(m // tt,),
            in_specs=[pl.BlockSpec((SUBLANES, tt), lambda i, st: (0, i))],
            out_specs=pl.BlockSpec((SUBLANES, tt), lambda i, st: (0, i))),
        out_shape=jax.ShapeDtypeStruct((SUBLANES, m), F32),
        compiler_params=_params("parallel"),
        name="moe_slots",
    )(slot_start, info)


def _dispatch_kernel(blk_ref, til_ref, flg_ref, slots_ref, x_ref, o_ref, go_ref, *, bs):
    w = pl.program_id(0)
    flags = flg_ref[w]
    first = (flags & 1) != 0
    real = (flags & 4) != 0

    def gathered():
        sl = slots_ref[...]
        sid = (blk_ref[w] * bs + lax.broadcasted_iota(I32, (bs, sl.shape[1]), 0)).astype(F32)
        hit1 = sid == sl[0:1, :]
        hit2 = sid == sl[1:2, :]
        onehot = jnp.where(hit1 | hit2, 1.0, 0.0).astype(BF16)
        gate = jnp.sum(jnp.where(hit1, sl[2:3, :], 0.0) + jnp.where(hit2, sl[3:4, :], 0.0), axis=1, keepdims=True)
        return _dot(onehot, x_ref[...]).astype(o_ref.dtype), jnp.broadcast_to(gate, go_ref.shape)

    @pl.when(first & real)
    def _():
        o_ref[...], go_ref[...] = gathered()

    @pl.when(first & jnp.logical_not(real))
    def _():
        o_ref[...] = jnp.zeros_like(o_ref)
        go_ref[...] = jnp.zeros_like(go_ref)

    @pl.when(jnp.logical_not(first) & real)
    def _():
        rows, gate = gathered()
        o_ref[...] += rows
        go_ref[...] += gate


def moe_dispatch(xn, slots, wl_blk, wl_til, wl_flg, n_slots):
    m, d = xn.shape
    bs = tt = MOE_TILE
    n_items = wl_blk.shape[0]
    return pl.pallas_call(
        functools.partial(_dispatch_kernel, bs=bs),
        grid_spec=pltpu.PrefetchScalarGridSpec(
            num_scalar_prefetch=3, grid=(n_items,),
            in_specs=[pl.BlockSpec((SUBLANES, tt), lambda w, b, t, f: (0, t[w])),
                      pl.BlockSpec((tt, d), lambda w, b, t, f: (t[w], 0))],
            out_specs=[pl.BlockSpec((bs, d), lambda w, b, t, f: (b[w], 0)),
                       pl.BlockSpec((bs, LANES), lambda w, b, t, f: (b[w], 0))]),
        out_shape=[jax.ShapeDtypeStruct((n_slots, d), BF16), jax.ShapeDtypeStruct((n_slots, LANES), F32)],
        compiler_params=_params("arbitrary"),
        name="moe_dispatch",
    )(wl_blk, wl_til, wl_flg, slots, xn)


def _moe_ffn_kernel(be_ref, rows_ref, x_ref, gate_ref, w1_ref, w3_ref, w2_ref, o_ref, acc_ref, *, sub):
    b = pl.program_id(0)
    f = pl.program_id(1)
    bm = x_ref.shape[0]

    @pl.when(f == 0)
    def _():
        acc_ref[...] = jnp.zeros_like(acc_ref)

    for rows in range(sub, bm + 1, sub):
        @pl.when(rows_ref[b] == rows)
        def _(rows=rows):
            x = x_ref[0:rows, :]
            h = _silu(_dot(x, w1_ref[0].astype(BF16))) * _dot(x, w3_ref[0].astype(BF16))
            acc_ref[0:rows, :] += _dot(h.astype(BF16), w2_ref[0].astype(BF16))

    @pl.when(f == pl.num_programs(1) - 1)
    def _():
        o_ref[...] = (acc_ref[...] * gate_ref[:, 0:1]).astype(o_ref.dtype)


def moe_ffn(xs, slot_gate, w1, w3, w2, block_expert, block_rows):
    n_slots, d = xs.shape
    ff = w1.shape[2]
    bm = MOE_FFN_ROWS
    tf = MOE_TF
    nf = ff // tf

    def f_of(b, f, rows):
        return jnp.where(rows[b] > 0, f, nf - 1)

    return pl.pallas_call(
        functools.partial(_moe_ffn_kernel, sub=MOE_TILE),
        grid_spec=pltpu.PrefetchScalarGridSpec(
            num_scalar_prefetch=2, grid=(n_slots // bm, nf),
            in_specs=[pl.BlockSpec((bm, d), lambda b, f, be, nb: (b, 0)),
                      pl.BlockSpec((bm, LANES), lambda b, f, be, nb: (b, 0)),
                      pl.BlockSpec((1, d, tf), lambda b, f, be, nb: (be[b], 0, f_of(b, f, nb))),
                      pl.BlockSpec((1, d, tf), lambda b, f, be, nb: (be[b], 0, f_of(b, f, nb))),
                      pl.BlockSpec((1, tf, d), lambda b, f, be, nb: (be[b], f_of(b, f, nb), 0))],
            out_specs=pl.BlockSpec((bm, d), lambda b, f, be, nb: (b, 0)),
            scratch_shapes=[pltpu.VMEM((bm, d), F32)]),
        out_shape=jax.ShapeDtypeStruct((n_slots, d), BF16),
        compiler_params=_params("arbitrary", "arbitrary"),
        name="moe_ffn",
    )(block_expert, block_rows, xs, slot_gate, w1, w3, w2)


def _combine_kernel(til_ref, blk_ref, flg_ref, tok_ref, y_ref, res_ref, o_ref, *, bs):
    w = pl.program_id(0)
    flags = flg_ref[w]

    @pl.when((flags & 1) != 0)
    def _():
        o_ref[...] = res_ref[...]

    @pl.when((flags & 4) != 0)
    def _():
        tok = tok_ref[...]
        tt = tok.shape[0]
        sid = (blk_ref[w] * bs + lax.broadcasted_iota(I32, (tt, bs), 1)).astype(F32)
        onehot = jnp.where((sid == tok[:, 0:1]) | (sid == tok[:, 1:2]), 1.0, 0.0).astype(BF16)
        o_ref[...] += _dot(onehot, y_ref[...])


def moe_combine(y, tok_info, residual, wl_til, wl_blk, wl_flg):
    m, d = residual.shape
    bs = tt = MOE_TILE
    n_items = wl_til.shape[0]
    return pl.pallas_call(
        functools.partial(_combine_kernel, bs=bs),
        grid_spec=pltpu.PrefetchScalarGridSpec(
            num_scalar_prefetch=3, grid=(n_items,),
            in_specs=[pl.BlockSpec((tt, SUBLANES), lambda w, t, b, f: (t[w], 0)),
                      pl.BlockSpec((bs, d), lambda w, t, b, f: (b[w], 0)),
                      pl.BlockSpec((tt, d), lambda w, t, b, f: (t[w], 0))],
            out_specs=pl.BlockSpec((tt, d), lambda w, t, b, f: (t[w], 0))),
        out_shape=jax.ShapeDtypeStruct((m, d), F32),
        compiler_params=_params("arbitrary"),
        name="moe_combine",
    )(wl_til, wl_blk, wl_flg, tok_info, y, residual)


def _work_list(overlap, n_items):
    n_major, n_minor = overlap.shape
    has = jnp.any(overlap, axis=1)
    marked = jnp.concatenate([overlap[:, :1] | ~has[:, None], overlap[:, 1:]], axis=1)
    maj, mino = jnp.nonzero(marked, size=n_items, fill_value=-1)
    valid = maj >= 0
    real = valid & overlap[jnp.maximum(maj, 0), jnp.maximum(mino, 0)]
    prev = jnp.concatenate([jnp.full((1,), -2, maj.dtype), maj[:-1]])
    nxt = jnp.concatenate([maj[1:], jnp.full((1,), -2, maj.dtype)])
    flags = (valid & (maj != prev)) * 1 + (valid & (maj != nxt)) * 2 + real * 4
    last_maj = jnp.max(jnp.where(valid, maj, 0))
    last_min = jnp.sum(jnp.where(valid & (maj == last_maj) & (maj != nxt), mino, 0))
    maj = jnp.where(valid, maj, last_maj)
    mino = jnp.where(valid, mino, last_min)
    return maj.astype(I32), mino.astype(I32), flags.astype(I32)


def moe_layer(h, norm_gain, w_router, w1, w3, w2):
    m, d = h.shape
    E = N_EXPERTS
    tile = min(MOE_TILE, m)
    n_tiles = m // tile
    ffn_rows = MOE_FFN_ROWS
    xn = rmsnorm(h, norm_gain, BF16)
    info, tile_cnt, total = moe_router(h, norm_gain, w_router)
    before = tile_cnt[:, :, 0]
    counts = total[:, 0].astype(I32)
    padded = (counts + ffn_rows - 1) // ffn_rows * ffn_rows
    ends = jnp.cumsum(padded)
    starts = ends - padded
    n_slots = (m * TOP_K // ffn_rows + E) * ffn_rows
    n_ffn_blocks = n_slots // ffn_rows
    n_dblocks = n_slots // tile
    slots = moe_slots(info, starts.astype(I32))
    db_start = jnp.arange(n_dblocks, dtype=I32) * tile
    db_e = jnp.minimum(jnp.sum(db_start[:, None] >= ends[None, :], axis=1), E - 1)
    r0 = db_start - starts[db_e]
    r1 = jnp.minimum(r0 + tile, counts[db_e])
    after = jnp.concatenate([before[1:], counts[None, :].astype(F32)], axis=0).astype(I32)
    bef = before.astype(I32)
    ov = (bef[:, db_e].T < r1[:, None]) & (after[:, db_e].T > r0[:, None]) & (db_start < ends[-1])[:, None]
    n_items = n_dblocks + E * n_tiles
    d_blk, d_til, d_flg = _work_list(ov, n_items)
    c_til, c_blk, c_flg = _work_list(ov.T, n_items)
    xs, slot_gate = moe_dispatch(xn, slots, d_blk, d_til, d_flg, n_slots)
    fb_start = jnp.arange(n_ffn_blocks, dtype=I32) * ffn_rows
    block_expert = jnp.minimum(jnp.sum(fb_start[:, None] >= ends[None, :], axis=1), E - 1).astype(I32)
    filled = counts[block_expert] - (fb_start - starts[block_expert])
    block_rows = jnp.where(fb_start < ends[-1], (jnp.clip(filled, 0, ffn_rows) + tile - 1) // tile * tile, 0).astype(I32)
    ys = moe_ffn(xs, slot_gate, w1, w3, w2, block_expert, block_rows)
    return moe_combine(ys, slots.T, h, c_til, c_blk, c_flg)


def even_mixer_layer(h, posf, p, bsz, seq):
    m = bsz * seq
    hn = rmsnorm(h, p["norm_mix"], BF16)
    w_in = p["w_in"]
    c = np.cumsum([0, D_SSM, D_CONV, SSM_HEADS, D_NSA] + [D_NSA_KV] * 6 + [3 * NSA_HEADS])
    w_proj = jnp.concatenate([w_in[:, c[1]:c[2]], w_in[:, c[4]:c[10]], w_in[:, c[0]:c[1]], w_in[:, c[3]:c[4]]], axis=1)
    proj = matmul([(hn, w_proj.astype(BF16))], out_dtype=BF16)
    kv0 = D_CONV
    w_small = jnp.concatenate([w_in[:, c[2]:c[3]], w_in[:, c[10]:c[11]]], axis=1)
    w_small = jnp.pad(w_small, ((0, 0), (0, LANES - w_small.shape[1]))).astype(BF16)
    small = matmul([(hn, w_small)])
    y_ssm = ssd_group(proj, (D_CONV + 6 * D_NSA_KV) // D_SSM, 0, small, p["conv_w"], p["conv_b"], p["dt_bias"],
                      p["a_log"], p["d_skip"], p["ssm_norm"], bsz, seq)
    qn, ksn, kwn = rope_norm(posf, proj, (D_CONV + 6 * D_NSA_KV + D_SSM) // D_NSA, kv0 // (6 * D_NSA_KV),
                             p["q_gain"], p["ks_gain"], p["kw_gain"])
    kv6 = proj[:, kv0:kv0 + 6 * D_NSA_KV]
    nch = seq // CMP_STRIDE

    def chunks(col0):
        u = kv6[:, col0:col0 + D_NSA_KV].reshape(bsz, nch, CMP_STRIDE, NSA_KV_HEADS, NSA_HEAD_DIM)
        return u.transpose(0, 3, 1, 2, 4).reshape(bsz, NSA_KV_HEADS, nch, CMP_STRIDE * NSA_HEAD_DIM)

    kc, vc = compress(chunks(0), chunks(D_NSA_KV), p["pe_k"], p["pe_v"], p["wk1"], p["wk2"], p["wv1"], p["wv2"],
                      p["kc_gain"])
    o_cmp, selt = cmp_select(qn, kc, vc, bsz, seq)
    tk = min(ATT_TK, seq)

    def v_tiles(col0):
        u = kv6[:, col0:col0 + D_NSA_KV].astype(BF16).reshape(bsz, seq // tk, tk, NSA_KV_HEADS, NSA_HEAD_DIM)
        return u.transpose(0, 3, 1, 4, 2)

    o_sel = masked_attention(qn, ksn, 0, v_tiles(3 * D_NSA_KV), bsz, seq, "sel", selt)
    o_win = masked_attention(qn, kwn, 0, v_tiles(5 * D_NSA_KV), bsz, seq, "win")
    y_nsa = nsa_combine(small, o_cmp, o_sel, o_win)
    w_out = p["w_out"].astype(BF16)
    return matmul([(y_ssm, w_out[:D_SSM]), (y_nsa, w_out[D_SSM:])], residual=h)


def rwkv_layer(h, p, bsz, seq):
    m, d = h.shape
    mixes = rwkv_mix(h, p["norm_mix"], p["mu"], seq)
    bf = lambda w: w.astype(BF16)

    def lora_pad(w_a, w_b):
        r = w_a.shape[1]
        rp = -(-r // LANES) * LANES
        return bf(jnp.pad(w_a, ((0, 0), (0, rp - r)))), bf(jnp.pad(w_b, ((0, rp - r), (0, 0))))

    perm = _wkv_column_order()
    r = matmul([(mixes[0], bf(p["w_r"][:, perm]))], out_dtype=BF16)
    k = matmul([(mixes[2], bf(p["w_k"][:, perm]))], out_dtype=BF16)
    v = matmul([(mixes[3], bf(p["w_v"][:, perm]))], out_dtype=BF16)
    w1, w2 = lora_pad(p["w1"], p["w2"][:, perm])
    w_raw = lora(mixes[1], w1, w2, p["w0"][perm], "tanh", None, F32)
    a1, a2 = lora_pad(p["a1"], p["a2"][:, perm])
    a_sig = lora(mixes[4], a1, a2, p["a0"][perm], None, "sigmoid", BF16)
    g1, g2 = lora_pad(p["g1"], p["g2"][:, perm])
    g = lora(mixes[5], g1, g2, jnp.zeros((d,), F32), "sigmoid", None, BF16)
    o = wkv_fused(r, k, v, w_raw, a_sig, g, p["k_k"], p["k_a"], p["r_k"].reshape(-1), p["ln_w"], p["ln_b"], bsz, seq)
    return matmul([(o, bf(p["w_o"][perm, :]))], residual=h)


def kernel(x, mem, positions, norm_mix, norm_xattn, norm_mem, norm_ffn, xattn_wq, xattn_wkv, xattn_wo, xattn_q_gain, xattn_k_gain, ev_w_in, ev_conv_w, ev_conv_b, ev_dt_bias, ev_a_log, ev_d_skip, ev_ssm_norm, ev_q_gain, ev_kc_gain, ev_ks_gain, ev_kw_gain, ev_pe_k, ev_pe_v, ev_cmp_wk1, ev_cmp_wk2, ev_cmp_wv1, ev_cmp_wv2, ev_w_out, ev_ffn_w1, ev_ffn_w3, ev_ffn_w2, od_mu, od_w_r, od_w_k, od_w_v, od_w_o, od_w0, od_w1, od_w2, od_a0, od_a1, od_a2, od_g1, od_g2, od_k_k, od_k_a, od_r_k, od_ln_w, od_ln_b, od_router, od_moe_w1, od_moe_w3, od_moe_w2):
    bsz, seq, d = x.shape
    m = bsz * seq
    depth = norm_mix.shape[0]
    h = x.reshape(m, d)
    mem2d = mem.reshape(-1, d)
    posf = positions.astype(F32).reshape(m, 1)
    for layer in range(depth):
        i = layer // 2
        if layer % 2 == 0:
            p = dict(norm_mix=norm_mix[layer], w_in=ev_w_in[i], conv_w=ev_conv_w[i], conv_b=ev_conv_b[i],
                     dt_bias=ev_dt_bias[i], a_log=ev_a_log[i], d_skip=ev_d_skip[i], ssm_norm=ev_ssm_norm[i],
                     q_gain=ev_q_gain[i], kc_gain=ev_kc_gain[i], ks_gain=ev_ks_gain[i], kw_gain=ev_kw_gain[i],
                     pe_k=ev_pe_k[i], pe_v=ev_pe_v[i], wk1=ev_cmp_wk1[i], wk2=ev_cmp_wk2[i], wv1=ev_cmp_wv1[i],
                     wv2=ev_cmp_wv2[i], w_out=ev_w_out[i])
            h = even_mixer_layer(h, posf, p, bsz, seq)
        else:
            p = dict(norm_mix=norm_mix[layer], mu=od_mu[i], w_r=od_w_r[i], w_k=od_w_k[i], w_v=od_w_v[i], w_o=od_w_o[i],
                     w0=od_w0[i], w1=od_w1[i], w2=od_w2[i], a0=od_a0[i], a1=od_a1[i], a2=od_a2[i], g1=od_g1[i],
                     g2=od_g2[i], k_k=od_k_k[i], k_a=od_k_a[i], r_k=od_r_k[i], ln_w=od_ln_w[i], ln_b=od_ln_b[i])
            h = rwkv_layer(h, p, bsz, seq)
        h = memory_xattn(h, mem2d, norm_xattn[layer], norm_mem[layer], xattn_wq[layer], xattn_wkv[layer],
                         xattn_wo[layer], xattn_q_gain[layer], xattn_k_gain[layer], bsz, seq)
        if layer % 2 == 0:
            hn = rmsnorm(h, norm_ffn[layer], BF16)
            h = swiglu_ffn(hn, ev_ffn_w1[i].astype(BF16), ev_ffn_w3[i].astype(BF16), ev_ffn_w2[i].astype(BF16), h)
        else:
            h = moe_layer(h, norm_ffn[layer], od_router[i], od_moe_w1[i], od_moe_w3[i], od_moe_w2[i])
    return h.reshape(bsz, seq, d)
```

```python
import functools
import math

import numpy as np
import jax
import jax.numpy as jnp
from jax import lax
from jax.experimental import pallas as pl
from jax.experimental.pallas import tpu as pltpu

F32 = jnp.float32
BF16 = jnp.bfloat16
I32 = jnp.int32

D_MODEL = 2048
NORM_EPS = 1e-6
NEG_BIG = -1e30

SSM_HEADS = 32
SSM_HEAD_DIM = 64
D_SSM = SSM_HEADS * SSM_HEAD_DIM
SSM_GROUPS = 4
D_STATE = 128
CONV_WIDTH = 4
SSD_CHUNK = 128
D_CONV = D_SSM + 2 * SSM_GROUPS * D_STATE
D_SSM_GROUP = D_SSM // SSM_GROUPS
SSD_CONV_TAIL = 16
SSD_EXT_ROWS = 256

NSA_HEADS = 16
NSA_KV_HEADS = 4
NSA_HEAD_DIM = 128
NSA_Q_PER_KV = NSA_HEADS // NSA_KV_HEADS
D_NSA = NSA_HEADS * NSA_HEAD_DIM
D_NSA_KV = NSA_KV_HEADS * NSA_HEAD_DIM
CMP_BLOCK = 32
CMP_STRIDE = 16
CMP_HIDDEN = 256
SEL_BLOCK = 64
N_SELECT = 16
WINDOW = 512
ROPE_DIM = NSA_HEAD_DIM // 4
ROPE_THETA = 500000.0

RWKV_HEAD_DIM = 64
RWKV_HEADS = D_MODEL // RWKV_HEAD_DIM
RWKV_GN_EPS = 1e-5 * RWKV_HEAD_DIM

XATTN_HEADS = 4
XATTN_HEAD_DIM = 128
D_XATTN = XATTN_HEADS * XATTN_HEAD_DIM

N_EXPERTS = 8
TOP_K = 2

LANES = 128
SUBLANES = 8
VMEM_LIMIT_BYTES = 56 * 1024 * 1024

ROW_TILE = 512
MM_TM = 2048
MM_TN = 512
MM_VMEM_BUDGET = 44 * 1024 * 1024
FFN_TM = 512
FFN_TF = 512
FFN_SPLIT = 256
MOE_FFN_ROWS = 1024
MOE_TF = 256
MOE_ROW_GROUP = 256
COMBINE_TM = 256
MOE_TILE = 512
ATT_TQ = 256
ATT_TK = 512
WIN_TK = 256
LOG2E = 1.4426950408889634
CMP_TQ = 256
WKV_T = 64
WKV_LOOKAHEAD = 8


def _params(*sem):
    return pltpu.CompilerParams(dimension_semantics=sem, vmem_limit_bytes=VMEM_LIMIT_BYTES)


def _sigmoid(x):
    return 1.0 / (1.0 + jnp.exp(-x))


def _silu(x):
    return x * _sigmoid(x)


def _softplus(x):
    return jnp.maximum(x, 0.0) + jnp.log(1.0 + jnp.exp(-jnp.abs(x)))


def _split3(x):
    h = x.astype(BF16)
    r = x - h.astype(F32)
    m = r.astype(BF16)
    l = (r - m.astype(F32)).astype(BF16)
    return h, m, l


def _dot(a, b):
    return jnp.dot(a, b, preferred_element_type=F32)


def _dot_nt(a, b):
    return lax.dot_general(a, b, (((1,), (1,)), ((), ())), preferred_element_type=F32)


def _dot_f32_sel(x, e):
    h, m, l = _split3(x)
    return _dot(h, e) + _dot(m, e) + _dot(l, e)


def _sel_dot_f32(e, x):
    h, m, l = _split3(x)
    return _dot(e, h) + _dot(e, m) + _dot(e, l)


def _rms(x, eps=NORM_EPS):
    return x * lax.rsqrt(jnp.mean(x * x, axis=-1, keepdims=True) + eps)


def _rmsnorm_kernel(x_ref, g_ref, o_ref):
    o_ref[...] = (_rms(x_ref[...]) * g_ref[...]).astype(o_ref.dtype)


def rmsnorm(x, gain, out_dtype):
    m, d = x.shape
    tm = min(ROW_TILE, m)
    return pl.pallas_call(
        _rmsnorm_kernel,
        grid=(m // tm,),
        in_specs=[pl.BlockSpec((tm, d), lambda i: (i, 0)), pl.BlockSpec((1, d), lambda i: (0, 0))],
        out_specs=pl.BlockSpec((tm, d), lambda i: (i, 0)),
        out_shape=jax.ShapeDtypeStruct((m, d), out_dtype),
        compiler_params=_params("parallel"),
        name="rmsnorm",
    )(x, gain.reshape(1, d))


def _mm_kernel(*refs, n_pairs, has_res):
    o_ref = refs[-1]
    acc = None
    for p in range(n_pairs):
        d = _dot(refs[2 * p][...], refs[2 * p + 1][...])
        acc = d if acc is None else acc + d
    if has_res:
        acc = acc + refs[2 * n_pairs][...]
    o_ref[...] = acc.astype(o_ref.dtype)


def matmul(pairs, residual=None, out_dtype=F32):
    m = pairs[0][0].shape[0]
    n = pairs[0][1].shape[1]
    k_total = sum(a.shape[1] for a, _ in pairs)
    tm = min(MM_TM if k_total <= D_MODEL else MM_TM // 2, m)
    out_bytes = jnp.dtype(out_dtype).itemsize + (4 if residual is not None else 0)

    def vmem_bytes(tn):
        return 2 * (tm * k_total * 2 + k_total * tn * 2 + tm * tn * out_bytes)

    tn = next((t for t in (2 * MM_TN, MM_TN, 256, LANES) if n % t == 0 and vmem_bytes(t) <= MM_VMEM_BUDGET), LANES)
    in_specs, args = [], []
    for a, w in pairs:
        kk = a.shape[1]
        in_specs += [pl.BlockSpec((tm, kk), lambda i, j: (i, 0)), pl.BlockSpec((kk, tn), lambda i, j: (0, j))]
        args += [a, w]
    if residual is not None:
        in_specs.append(pl.BlockSpec((tm, tn), lambda i, j: (i, j)))
        args.append(residual)
    kern = functools.partial(_mm_kernel, n_pairs=len(pairs), has_res=residual is not None)
    return pl.pallas_call(
        kern,
        grid=(m // tm, n // tn),
        in_specs=in_specs,
        out_specs=pl.BlockSpec((tm, tn), lambda i, j: (i, j)),
        out_shape=jax.ShapeDtypeStruct((m, n), out_dtype),
        compiler_params=_params("parallel", "parallel"),
        name="matmul",
    )(*args)


def _act(x, act):
    if act == "tanh":
        return jnp.tanh(x)
    if act == "sigmoid":
        return _sigmoid(x)
    return x


def _lora_kernel(a_ref, wa_ref, wb_ref, bias_ref, o_ref, *, act_mid, act_out):
    mid = _act(_dot(a_ref[...], wa_ref[...]), act_mid).astype(BF16)
    o_ref[...] = _act(_dot(mid, wb_ref[...]) + bias_ref[...], act_out).astype(o_ref.dtype)


def lora(a, w_a, w_b, bias, act_mid, act_out, out_dtype):
    m, d = a.shape
    r = w_a.shape[1]
    n = w_b.shape[1]
    tm = min(ROW_TILE, m)
    return pl.pallas_call(
        functools.partial(_lora_kernel, act_mid=act_mid, act_out=act_out),
        grid=(m // tm,),
        in_specs=[pl.BlockSpec((tm, d), lambda i: (i, 0)), pl.BlockSpec((d, r), lambda i: (0, 0)),
                  pl.BlockSpec((r, n), lambda i: (0, 0)), pl.BlockSpec((1, n), lambda i: (0, 0))],
        out_specs=pl.BlockSpec((tm, n), lambda i: (i, 0)),
        out_shape=jax.ShapeDtypeStruct((m, n), out_dtype),
        compiler_params=_params("parallel"),
        name="lora",
    )(a, w_a, w_b, bias.reshape(1, n).astype(F32))


def _ffn_kernel(g_ref, w1_ref, w3_ref, w2_ref, res_ref, o_ref, x_ref):
    f = pl.program_id(1)

    @pl.when(f == 0)
    def _():
        res = res_ref[...]
        o_ref[...] = res
        x_ref[...] = (_rms(res) * g_ref[...]).astype(x_ref.dtype)

    x = x_ref[...]
    tf = w1_ref.shape[1]
    cols = [slice(c, c + FFN_SPLIT) for c in range(0, tf, FFN_SPLIT)]
    gate_up = [(_dot(x, w1_ref[:, c]), _dot(x, w3_ref[:, c])) for c in cols]
    hs = [(_silu(g) * u).astype(BF16) for g, u in gate_up]
    acc = _dot(hs[0], w2_ref[cols[0], :])
    for h, c in zip(hs[1:], cols[1:]):
        acc = acc + _dot(h, w2_ref[c, :])
    o_ref[...] += acc


def swiglu_ffn(residual, norm_gain, w1, w3, w2):
    m, d = residual.shape
    ff = w1.shape[1]
    tm = min(FFN_TM, m)
    tf = FFN_TF
    return pl.pallas_call(
        _ffn_kernel,
        grid=(m // tm, ff // tf),
        in_specs=[pl.BlockSpec((1, d), lambda i, f: (0, 0)),
                  pl.BlockSpec((d, tf), lambda i, f: (0, f)),
                  pl.BlockSpec((d, tf), lambda i, f: (0, f)),
                  pl.BlockSpec((tf, d), lambda i, f: (f, 0)),
                  pl.BlockSpec((tm, d), lambda i, f: (i, 0))],
        out_specs=pl.BlockSpec((tm, d), lambda i, f: (i, 0)),
        out_shape=jax.ShapeDtypeStruct((m, d), F32),
        scratch_shapes=[pltpu.VMEM((tm, d), BF16)],
        compiler_params=_params("parallel", "arbitrary"),
        name="swiglu_ffn",
    )(norm_gain.reshape(1, d), w1, w3, w2, residual)


def _ssd_kernel(z_ref, xbc_ref, dt_ref, shift_ref, cw_ref, cb_ref, dtb_ref, alog_ref, dskip_ref, nw_ref, eh_ref,
                o_ref, ext_ref, st_ref):
    L = SSD_CHUNK
    G = SSM_GROUPS
    GW = D_SSM_GROUP
    T = SSD_CONV_TAIL
    c = pl.program_id(1)

    @pl.when(c == 0)
    def _():
        ext_ref[...] = jnp.zeros_like(ext_ref)
        st_ref[...] = jnp.zeros_like(st_ref)

    ext_ref[0:T, :] = ext_ref[L:L + T, :]
    ext_ref[T:L + T, :] = xbc_ref[...]
    shifted = _dot(shift_ref[...], ext_ref[...])
    acc = jnp.zeros((L, D_CONV), F32) + cb_ref[...]
    for k in range(CONV_WIDTH):
        acc = acc + cw_ref[k:k + 1, :] * shifted[k * L:(k + 1) * L, :]
    xc = _silu(acc)

    dt = _softplus(dt_ref[...] + dtb_ref[...])
    a = -jnp.exp(alog_ref[...])
    adt = dt * a
    row = lax.broadcasted_iota(I32, (L, L), 0)
    col = lax.broadcasted_iota(I32, (L, L), 1)
    causal = row >= col
    tril = jnp.where(causal, 1.0, 0.0).astype(BF16)
    a_cum = _sel_dot_f32(tril, adt)
    a_cum_t = a_cum.T
    eh = eh_ref[...]
    acx = _dot_f32_sel(a_cum, eh)
    dtx = _dot_f32_sel(dt, eh)
    a_end = acx[L - 1:L, :]
    xs = xc[:, :D_SSM]
    xdt = xs * dtx
    xdte = (xdt * jnp.exp(a_end - acx)).astype(BF16)
    from_start = jnp.exp(acx)
    lane = lax.broadcasted_iota(I32, (L, LANES), 1)
    lo_half = lane < SSM_HEAD_DIM
    z = z_ref[...].astype(F32)

    for g in range(G):
        bm = xc[:, D_SSM + g * D_STATE:D_SSM + (g + 1) * D_STATE]
        cm = xc[:, D_SSM + G * D_STATE + g * D_STATE:D_SSM + G * D_STATE + (g + 1) * D_STATE]
        bm_b = bm.astype(BF16)
        cm_b = cm.astype(BF16)
        cb = _dot_nt(cm_b, bm_b)
        st = st_ref[g]
        y_off = _dot(cm_b, st.astype(BF16)) * from_start[:, g * GW:(g + 1) * GW]
        pieces = []
        for pp in range(GW // LANES):
            h0 = g * (GW // SSM_HEAD_DIM) + 2 * pp
            ms = []
            for hh in (h0, h0 + 1):
                diff = a_cum[:, hh:hh + 1] - a_cum_t[hh:hh + 1, :]
                dec = jnp.exp(jnp.where(causal, diff, NEG_BIG))
                ms.append((cb * dec).astype(BF16))
            lhs = jnp.concatenate(ms, axis=1)
            c0 = g * GW + pp * LANES
            xp = xdt[:, c0:c0 + LANES]
            rhs = jnp.concatenate([jnp.where(lo_half, xp, 0.0), jnp.where(lo_half, 0.0, xp)], axis=0)
            pieces.append(_dot(lhs, rhs.astype(BF16)))
        y = jnp.concatenate(pieces, axis=1) + y_off
        y = y + xs[:, g * GW:(g + 1) * GW] * dskip_ref[:, g * GW:(g + 1) * GW]
        y = y * _silu(z[:, g * GW:(g + 1) * GW])
        y = _rms(y) * nw_ref[:, g * GW:(g + 1) * GW]
        o_ref[:, g * GW:(g + 1) * GW] = y.astype(o_ref.dtype)
        s_new = _dot(bm.T.astype(BF16), xdte[:, g * GW:(g + 1) * GW])
        st_ref[g] = st * jnp.exp(a_end[:, g * GW:(g + 1) * GW]) + s_new


def ssd_group(proj, z_col, xbc_col, small, conv_w, conv_b, dt_bias, a_log, d_skip, norm_w, bsz, seq):
    m = bsz * seq
    L = SSD_CHUNK
    nc = seq // L
    pad = LANES - SSM_HEADS
    eh = np.zeros((LANES, D_SSM), np.float32)
    for h in range(SSM_HEADS):
        eh[h, h * SSM_HEAD_DIM:(h + 1) * SSM_HEAD_DIM] = 1.0
    shift = np.zeros((CONV_WIDTH * L, SSD_EXT_ROWS), np.float32)
    for k in range(CONV_WIDTH):
        shift[k * L + np.arange(L), np.arange(L) + SSD_CONV_TAIL - (CONV_WIDTH - 1) + k] = 1.0
    row_spec = lambda w, col=0: pl.BlockSpec((L, w), lambda b, c: (b * nc + c, col))
    const = lambda r, w: pl.BlockSpec((r, w), lambda b, c: (0, 0))
    return pl.pallas_call(
        _ssd_kernel,
        grid=(bsz, nc),
        in_specs=[row_spec(D_SSM, z_col), row_spec(D_CONV, xbc_col), row_spec(LANES),
                  const(CONV_WIDTH * L, SSD_EXT_ROWS),
                  const(CONV_WIDTH, D_CONV), const(1, D_CONV), const(1, LANES), const(1, LANES),
                  const(1, D_SSM), const(1, D_SSM), const(LANES, D_SSM)],
        out_specs=row_spec(D_SSM),
        out_shape=jax.ShapeDtypeStruct((m, D_SSM), BF16),
        scratch_shapes=[pltpu.VMEM((SSD_EXT_ROWS, D_CONV), BF16),
                        pltpu.VMEM((SSM_GROUPS, D_STATE, D_SSM_GROUP), F32)],
        compiler_params=_params("parallel", "arbitrary"),
        name="ssd_group",
    )(proj, proj, small, jnp.asarray(shift, BF16), conv_w.T, conv_b.reshape(1, D_CONV),
      jnp.pad(dt_bias, (0, pad)).reshape(1, LANES), jnp.pad(a_log, (0, pad)).reshape(1, LANES),
      jnp.repeat(d_skip, SSM_HEAD_DIM).reshape(1, D_SSM), norm_w.reshape(1, D_SSM), jnp.asarray(eh, BF16))


def _rope_norm_kernel(pos_ref, invf_ref, q_ref, kv_ref, qg_ref, ksg_ref, kwg_ref, qo_ref, kso_ref, kwo_ref):
    ang = pos_ref[...] * invf_ref[...]
    cos = jnp.cos(ang)
    sin = jnp.sin(ang)
    lane = lax.broadcasted_iota(I32, ang.shape, 1)
    first = lane < ROPE_DIM // 2
    sin_signed = jnp.where(first, -sin, sin)

    def head(x, gain):
        y = _rms(x.astype(F32)) * gain
        partner = jnp.where(first, pltpu.roll(y, LANES - ROPE_DIM // 2, axis=1), pltpu.roll(y, ROPE_DIM // 2, axis=1))
        return y * cos + partner * sin_signed

    hd = NSA_HEAD_DIM
    scale = hd ** -0.5 * LOG2E
    for h in range(NSA_HEADS):
        qo_ref[:, h * hd:(h + 1) * hd] = (head(q_ref[:, h * hd:(h + 1) * hd], qg_ref[...]) * scale).astype(qo_ref.dtype)
    for h in range(NSA_KV_HEADS):
        ks = kv_ref[:, 2 * D_NSA_KV + h * hd:2 * D_NSA_KV + (h + 1) * hd]
        kw = kv_ref[:, 4 * D_NSA_KV + h * hd:4 * D_NSA_KV + (h + 1) * hd]
        kso_ref[:, h * hd:(h + 1) * hd] = head(ks, ksg_ref[...]).astype(kso_ref.dtype)
        kwo_ref[:, h * hd:(h + 1) * hd] = head(kw, kwg_ref[...]).astype(kwo_ref.dtype)


def rope_norm(posf, proj, q_col, kv_col, q_gain, ks_gain, kw_gain):
    m = proj.shape[0]
    tm = min(ROW_TILE, m)
    half = ROPE_DIM // 2
    inv = np.exp(-math.log(ROPE_THETA) * np.arange(0, ROPE_DIM, 2, dtype=np.float32) / ROPE_DIM).astype(np.float32)
    invf = np.zeros((1, LANES), np.float32)
    invf[0, :half] = inv
    invf[0, half:ROPE_DIM] = inv
    row = lambda w, col=0: pl.BlockSpec((tm, w), lambda i: (i, col))
    const = pl.BlockSpec((1, LANES), lambda i: (0, 0))
    return pl.pallas_call(
        _rope_norm_kernel,
        grid=(m // tm,),
        in_specs=[row(1), const, row(D_NSA, q_col), row(6 * D_NSA_KV, kv_col), const, const, const],
        out_specs=[row(D_NSA), row(D_NSA_KV), row(D_NSA_KV)],
        out_shape=[jax.ShapeDtypeStruct((m, D_NSA), BF16), jax.ShapeDtypeStruct((m, D_NSA_KV), BF16),
                   jax.ShapeDtypeStruct((m, D_NSA_KV), BF16)],
        compiler_params=_params("parallel"),
        name="rope_norm",
    )(posf, jnp.asarray(invf), proj, proj, q_gain.reshape(1, LANES), ks_gain.reshape(1, LANES), kw_gain.reshape(1, LANES))


def _compress_kernel(uk_ref, uv_ref, pek_ref, pev_ref, wk1_ref, wk2_ref, wv1_ref, wv2_ref, g_ref, kc_ref, vc_ref):
    def mlp(u, pe_ref, w1_ref, w2_ref):
        u = u.astype(F32)
        n = u.shape[0]
        h1 = _dot((u + pe_ref[0:1, :]).astype(BF16), w1_ref[0])
        h2 = _dot((u + pe_ref[1:2, :]).astype(BF16), w1_ref[1])
        pre = h1 + pltpu.roll(h2, n - 1, axis=0)
        return _dot(_silu(pre).astype(BF16), w2_ref[...])

    kc = mlp(uk_ref[0, 0], pek_ref, wk1_ref, wk2_ref)
    kc_ref[0, 0] = _rms(kc) * g_ref[...]
    vc_ref[0, 0] = mlp(uv_ref[0, 0], pev_ref, wv1_ref, wv2_ref)


def compress(uk, uv, pe_k, pe_v, wk1, wk2, wv1, wv2, kc_gain):
    bsz, nkv, nch, width = uk.shape
    half = CMP_STRIDE * NSA_HEAD_DIM
    u_spec = pl.BlockSpec((1, 1, nch, width), lambda b, g: (b, g, 0, 0))
    o_spec = pl.BlockSpec((1, 1, nch, NSA_HEAD_DIM), lambda b, g: (b, g, 0, 0))
    c2 = lambda s: pl.BlockSpec(s, lambda b, g: (0, 0))
    c3 = lambda s: pl.BlockSpec(s, lambda b, g: (0, 0, 0))
    o_shape = jax.ShapeDtypeStruct((bsz, nkv, nch, NSA_HEAD_DIM), F32)
    return pl.pallas_call(
        _compress_kernel,
        grid=(bsz, nkv),
        in_specs=[u_spec, u_spec, c2((2, half)), c2((2, half)),
                  c3((2, half, CMP_HIDDEN)), c2((CMP_HIDDEN, NSA_HEAD_DIM)),
                  c3((2, half, CMP_HIDDEN)), c2((CMP_HIDDEN, NSA_HEAD_DIM)), c2((1, NSA_HEAD_DIM))],
        out_specs=[o_spec, o_spec],
        out_shape=[o_shape, o_shape],
        compiler_params=_params("parallel", "parallel"),
        name="nsa_compress",
    )(uk, uv, pe_k.reshape(2, half), pe_v.reshape(2, half),
      wk1.reshape(2, half, CMP_HIDDEN).astype(BF16), wk2.astype(BF16),
      wv1.reshape(2, half, CMP_HIDDEN).astype(BF16), wv2.astype(BF16), kc_gain.reshape(1, NSA_HEAD_DIM))


def _stack_heads(q):
    return jnp.concatenate([q[:, r * NSA_HEAD_DIM:(r + 1) * NSA_HEAD_DIM] for r in range(NSA_Q_PER_KV)], axis=0)


def _unstack_heads(o, tq):
    return jnp.concatenate([o[r * tq:(r + 1) * tq, :] for r in range(NSA_Q_PER_KV)], axis=1)


def _cmp_select_kernel(q_ref, kc_ref, vc_ref, ovt_ref, o_ref, sel_ref, *, tq, n_blk):
    R = NSA_Q_PER_KV
    q0 = pl.program_id(2) * tq
    q4 = _stack_heads(q_ref[...])
    kc = kc_ref[0, 0].astype(BF16)
    ncmp = kc.shape[0]
    s = _dot_nt(q4, kc)
    t_row = q0 + lax.broadcasted_iota(I32, (tq, ncmp), 0)
    c_col = lax.broadcasted_iota(I32, (tq, ncmp), 1)
    visf = jnp.where((c_col * CMP_STRIDE + CMP_BLOCK - 1) <= t_row, 1.0, 0.0)
    vis = jnp.concatenate([visf] * R, axis=0) > 0.5
    s = jnp.where(vis, s, NEG_BIG)
    mx = jnp.max(s, axis=-1, keepdims=True)
    e = jnp.where(vis, jnp.exp2(s - mx), 0.0)
    den = jnp.sum(e, axis=-1, keepdims=True)
    p = e / jnp.maximum(den, 1e-30)
    o = _dot(p.astype(BF16), vc_ref[0, 0].astype(BF16))
    o_ref[...] = _unstack_heads(o, tq)

    psum = p[0:tq]
    for r in range(1, R):
        psum = psum + p[r * tq:(r + 1) * tq]
    h, m, l = _split3(psum)
    ovt = ovt_ref[...]
    imp = _dot_nt(ovt, h) + _dot_nt(ovt, m) + _dot_nt(ovt, l)
    jdx = lax.broadcasted_iota(I32, (n_blk, tq), 0)
    t_lane = q0 + lax.broadcasted_iota(I32, (n_blk, tq), 1)
    cur = t_lane // SEL_BLOCK
    causal = jdx <= cur
    forced = ((jdx == 0) | (jdx >= cur - 1)) & causal
    score = jnp.where(forced, jnp.inf, jnp.where(causal, imp, -jnp.inf))
    rank = jnp.zeros((n_blk, tq), F32)
    for jp in range(n_blk):
        rowv = score[jp:jp + 1, :]
        beats = (rowv > score) | ((rowv == score) & (jdx > jp))
        rank = rank + jnp.where(beats, 1.0, 0.0)
    sel = jnp.where((rank < float(min(N_SELECT, n_blk))) & causal, 1.0, 0.0)
    if n_blk < LANES:
        sel = jnp.concatenate([sel, jnp.zeros((LANES - n_blk, tq), F32)], axis=0)
    sel_ref[0, 0] = sel.astype(sel_ref.dtype)


def cmp_select(qn, kc, vc, bsz, seq):
    m = bsz * seq
    G = NSA_KV_HEADS
    tq = CMP_TQ
    nq = seq // tq
    ncmp = kc.shape[2]
    n_blk = seq // SEL_BLOCK
    assert n_blk <= LANES and tq % LANES == 0
    c0 = np.arange(ncmp)[None, :] * CMP_STRIDE
    s0 = np.arange(n_blk)[:, None] * SEL_BLOCK
    ov = np.clip(np.minimum(c0 + CMP_BLOCK, s0 + SEL_BLOCK) - np.maximum(c0, s0), 0, None) / CMP_STRIDE
    n_cmp_valid = (seq - CMP_BLOCK) // CMP_STRIDE + 1
    ov[:, n_cmp_valid:] = 0.0
    kv_spec = pl.BlockSpec((1, 1, ncmp, NSA_HEAD_DIM), lambda b, g, i: (b, g, 0, 0))
    kern = functools.partial(_cmp_select_kernel, tq=tq, n_blk=n_blk)
    return pl.pallas_call(
        kern,
        grid=(bsz, G, nq),
        in_specs=[pl.BlockSpec((tq, NSA_Q_PER_KV * NSA_HEAD_DIM), lambda b, g, i: (b * nq + i, g)),
                  kv_spec, kv_spec, pl.BlockSpec((n_blk, ncmp), lambda b, g, i: (0, 0))],
        out_specs=[pl.BlockSpec((tq, NSA_Q_PER_KV * NSA_HEAD_DIM), lambda b, g, i: (b * nq + i, g)),
                   pl.BlockSpec((1, 1, LANES, tq), lambda b, g, i: (b, g, 0, i))],
        out_shape=[jax.ShapeDtypeStruct((m, D_NSA), F32), jax.ShapeDtypeStruct((bsz, G, LANES, seq), BF16)],
        compiler_params=_params("parallel", "parallel", "parallel"),
        name="nsa_cmp_select",
    )(qn, kc, vc, jnp.asarray(ov, BF16))


def _masked_attn_kernel(q_ref, k_ref, vt_ref, *rest, mode, tq, tk):
    if mode == "sel":
        selt_ref, o_ref = rest
    else:
        (o_ref,) = rest
    R = NSA_Q_PER_KV
    hd = NSA_HEAD_DIM
    q0 = pl.program_id(2) * tq
    key_row = lax.broadcasted_iota(I32, (tk, tq), 0)
    t_lane = q0 + lax.broadcasted_iota(I32, (tk, tq), 1)
    hi = (q0 + tq - 1) // tk + 1
    q = q_ref[...]
    if mode == "sel":
        key_row_b = lax.broadcasted_iota(I32, (tk, LANES), 0)
        blk_lane = lax.broadcasted_iota(I32, (tk, LANES), 1)
        penalty = ((1.0 - selt_ref[0, 0].astype(F32)) * NEG_BIG).astype(BF16)
        rhs = [jnp.concatenate([q[:, r * hd:(r + 1) * hd].astype(F32).T.astype(BF16), penalty], axis=0)
               for r in range(R)]
        lo = 0
    else:
        lo = jnp.maximum(q0 - (WINDOW - 1), 0) // tk

    def body(kt, carry, diagonal=True):
        k0 = kt * tk
        k_t = k_ref[pl.ds(pl.multiple_of(k0, tk), tk), :]
        v_t = vt_ref[0, 0, kt]
        key = k0 + key_row
        if mode == "sel":
            expand = jnp.where((k0 + key_row_b) // SEL_BLOCK == blk_lane, 1.0, 0.0).astype(BF16)
            lhs = jnp.concatenate([k_t, expand], axis=1)
            ss = [_dot(lhs, rhs[r]) for r in range(R)]
            if diagonal:
                ss = [jnp.where(key <= t_lane, s, NEG_BIG) for s in ss]
        else:
            bias = jnp.where((key <= t_lane) & (key > t_lane - WINDOW), 0.0, NEG_BIG)
            ss = [_dot_nt(k_t, q[:, r * hd:(r + 1) * hd]) + bias for r in range(R)]
        stats = []
        for r in range(R):
            m_prev, l_prev = carry[3 * r], carry[3 * r + 1]
            m_new = jnp.maximum(m_prev, jnp.max(ss[r], axis=0, keepdims=True))
            p = jnp.exp2(ss[r] - m_new)
            alpha = jnp.exp2(m_prev - m_new)
            stats.append((m_new, alpha * l_prev + jnp.sum(p, axis=0, keepdims=True), alpha, p.astype(BF16)))
        new = []
        for r in range(R):
            m_new, l_new, alpha, p = stats[r]
            new += [m_new, l_new, alpha * carry[3 * r + 2] + _dot(v_t, p)]
        return tuple(new)

    init = (jnp.full((1, tq), NEG_BIG, F32), jnp.zeros((1, tq), F32), jnp.zeros((hd, tq), F32)) * R
    if mode == "sel":
        out = lax.fori_loop(lo, hi - 1, functools.partial(body, diagonal=False), init)
        out = body(hi - 1, out)
    else:
        out = lax.fori_loop(lo, hi, body, init)
    o_ref[...] = jnp.concatenate([(out[3 * r + 2] / out[3 * r + 1]).T for r in range(R)], axis=1)


def masked_attention(qn, k, k_col0, vt, bsz, seq, mode, selt=None):
    m = bsz * seq
    G = NSA_KV_HEADS
    tq = min(ATT_TQ, seq)
    tk = vt.shape[-1]
    assert tk % tq == 0
    nq = seq // tq
    hd = NSA_HEAD_DIM
    q_spec = pl.BlockSpec((tq, NSA_Q_PER_KV * hd), lambda b, g, i: (b * nq + i, g))
    in_specs = [q_spec,
                pl.BlockSpec((seq, hd), lambda b, g, i: (b, k_col0 + g)),
                pl.BlockSpec((1, 1, seq // tk, hd, tk), lambda b, g, i: (b, g, 0, 0, 0))]
    args = [qn, k, vt]
    if mode == "sel":
        in_specs.append(pl.BlockSpec((1, 1, LANES, tq), lambda b, g, i: (b, g, 0, i)))
        args.append(selt)
    return pl.pallas_call(
        functools.partial(_masked_attn_kernel, mode=mode, tq=tq, tk=tk),
        grid=(bsz, G, nq),
        in_specs=in_specs,
        out_specs=q_spec,
        out_shape=jax.ShapeDtypeStruct((m, D_NSA), F32),
        compiler_params=_params("parallel", "parallel", "arbitrary"),
        name="nsa_attn_" + mode,
    )(*args)


def _nsa_combine_kernel(g_ref, e_ref, oc_ref, os_ref, ow_ref, o_ref):
    gate = _sigmoid(g_ref[...])
    gx = _dot_f32_sel(gate, e_ref[...])
    y = gx[:, :D_NSA] * oc_ref[...] + gx[:, D_NSA:2 * D_NSA] * os_ref[...] + gx[:, 2 * D_NSA:] * ow_ref[...]
    o_ref[...] = y.astype(o_ref.dtype)


def nsa_combine(small, o_cmp, o_sel, o_win):
    m = small.shape[0]
    tm = min(COMBINE_TM, m)
    e = np.zeros((LANES, 3 * D_NSA), np.float32)
    for h in range(NSA_HEADS):
        for br in range(3):
            e[SSM_HEADS + 3 * h + br, br * D_NSA + h * NSA_HEAD_DIM:br * D_NSA + (h + 1) * NSA_HEAD_DIM] = 1.0
    row = lambda w: pl.BlockSpec((tm, w), lambda i: (i, 0))
    return pl.pallas_call(
        _nsa_combine_kernel,
        grid=(m // tm,),
        in_specs=[row(LANES), pl.BlockSpec((LANES, 3 * D_NSA), lambda i: (0, 0)), row(D_NSA), row(D_NSA), row(D_NSA)],
        out_specs=row(D_NSA),
        out_shape=jax.ShapeDtypeStruct((m, D_NSA), BF16),
        compiler_params=_params("parallel"),
        name="nsa_combine",
    )(small, jnp.asarray(e, BF16), o_cmp, o_sel, o_win)


def _xattn_kernel(q_ref, kv_ref, qg_ref, kg_ref, o_ref):
    hd = XATTN_HEAD_DIM
    scale = hd ** -0.5
    outs = []
    for h in range(XATTN_HEADS):
        q = (_rms(q_ref[:, h * hd:(h + 1) * hd]) * qg_ref[...]).astype(BF16)
        k = (_rms(kv_ref[:, h * hd:(h + 1) * hd]) * kg_ref[...]).astype(BF16)
        v = kv_ref[:, D_XATTN + h * hd:D_XATTN + (h + 1) * hd].astype(BF16)
        s = _dot_nt(q, k) * scale
        e = jnp.exp(s - jnp.max(s, axis=-1, keepdims=True))
        p = e / jnp.sum(e, axis=-1, keepdims=True)
        outs.append(_dot(p.astype(BF16), v))
    o_ref[...] = jnp.concatenate(outs, axis=1).astype(o_ref.dtype)


def xattn_core(q, kv, q_gain, k_gain, bsz, seq):
    m = bsz * seq
    n_mem = kv.shape[0] // bsz
    tm = min(ROW_TILE, seq)
    nt = seq // tm
    return pl.pallas_call(
        _xattn_kernel,
        grid=(bsz, nt),
        in_specs=[pl.BlockSpec((tm, D_XATTN), lambda b, i: (b * nt + i, 0)),
                  pl.BlockSpec((n_mem, 2 * D_XATTN), lambda b, i: (b, 0)),
                  pl.BlockSpec((1, XATTN_HEAD_DIM), lambda b, i: (0, 0)),
                  pl.BlockSpec((1, XATTN_HEAD_DIM), lambda b, i: (0, 0))],
        out_specs=pl.BlockSpec((tm, D_XATTN), lambda b, i: (b * nt + i, 0)),
        out_shape=jax.ShapeDtypeStruct((m, D_XATTN), BF16),
        compiler_params=_params("parallel", "parallel"),
        name="xattn_core",
    )(q, kv, q_gain.reshape(1, XATTN_HEAD_DIM), k_gain.reshape(1, XATTN_HEAD_DIM))


def _norm_mm_kernel(h_ref, g_ref, w_ref, o_ref):
    x = (_rms(h_ref[...]) * g_ref[...]).astype(BF16)
    o_ref[...] = _dot(x, w_ref[...]).astype(o_ref.dtype)


def norm_matmul(h, gain, w):
    m, d = h.shape
    n = w.shape[1]
    tm = min(ROW_TILE, m)
    return pl.pallas_call(
        _norm_mm_kernel,
        grid=(m // tm,),
        in_specs=[pl.BlockSpec((tm, d), lambda i: (i, 0)), pl.BlockSpec((1, d), lambda i: (0, 0)),
                  pl.BlockSpec((d, n), lambda i: (0, 0))],
        out_specs=pl.BlockSpec((tm, n), lambda i: (i, 0)),
        out_shape=jax.ShapeDtypeStruct((m, n), F32),
        compiler_params=_params("parallel"),
        name="norm_matmul",
    )(h, gain.reshape(1, d), w)


def memory_xattn(h, mem2d, norm_x, norm_m, wq, wkv, wo, q_gain, k_gain, bsz, seq):
    q = norm_matmul(h, norm_x, wq.astype(BF16))
    kv = norm_matmul(mem2d, norm_m, wkv.astype(BF16))
    o = xattn_core(q, kv, q_gain, k_gain, bsz, seq)
    return matmul([(o, wo.astype(BF16))], residual=h)


def _mix_kernel(h_ref, hp_ref, g_ref, mu_ref, *rest, tm, tiles_per_seq):
    outs, ext_ref = rest[:-1], rest[-1]
    i = pl.program_id(0)
    hn = _rms(h_ref[...]) * g_ref[...]
    hp = _rms(hp_ref[...]) * g_ref[...]
    hp = jnp.where(i % tiles_per_seq == 0, 0.0, hp)
    ext_ref[0:SUBLANES, :] = hp
    ext_ref[SUBLANES:tm + SUBLANES, :] = hn
    xx = ext_ref[SUBLANES - 1:tm + SUBLANES - 1, :] - hn
    for j, o_ref in enumerate(outs):
        o_ref[...] = (hn + xx * mu_ref[j:j + 1, :]).astype(o_ref.dtype)


def rwkv_mix(h, gain, mu, seq):
    m, d = h.shape
    tm = min(ROW_TILE, seq)
    n_mix = mu.shape[0]
    kern = functools.partial(_mix_kernel, tm=tm, tiles_per_seq=seq // tm)
    row = pl.BlockSpec((tm, d), lambda i: (i, 0))
    return pl.pallas_call(
        kern,
        grid=(m // tm,),
        in_specs=[row, pl.BlockSpec((SUBLANES, d), lambda i: (jnp.maximum(i * (tm // SUBLANES) - 1, 0), 0)),
                  pl.BlockSpec((1, d), lambda i: (0, 0)), pl.BlockSpec((n_mix, d), lambda i: (0, 0))],
        out_specs=[row] * n_mix,
        out_shape=[jax.ShapeDtypeStruct((m, d), BF16)] * n_mix,
        scratch_shapes=[pltpu.VMEM((tm + SUBLANES, d), F32)],
        compiler_params=_params("parallel"),
        name="rwkv_mix",
    )(h, h, gain.reshape(1, d), mu)


WKV_LANE_GROUP = LANES // 4
WKV_ROWS_PER_BATCH = D_MODEL // LANES


def _wkv_fused_kernel(r_ref, k_ref, v_ref, w_ref, a_ref, g_ref, kkp_ref, kap_ref, rkp_ref, lnw_ref, lnb_ref,
                      o_ref, s_ref, wd_s, kh_s, kk_s, b_s, r_s, v_s, g_s, *, steps):
    n = RWKV_HEAD_DIM
    nb = LANES // WKV_LANE_GROUP
    halves = WKV_ROWS_PER_BATCH // SUBLANES
    ahead = min(WKV_LOOKAHEAD, steps)

    @pl.when(pl.program_id(0) == 0)
    def _():
        s_ref[...] = jnp.zeros_like(s_ref)

    def lane_group(cnt):
        return lax.broadcasted_iota(I32, (cnt, SUBLANES, LANES), 2) // WKV_LANE_GROUP

    def to_scan(x_ref, t0, cnt):
        grp = lane_group(cnt)
        rows = [x_ref[b, pl.ds(t0, cnt)].astype(F32) for b in range(nb)]
        pieces = []
        for n4 in range(nb):
            for v in range(halves):
                acc = None
                for b in range(nb):
                    x = rows[b][:, SUBLANES * v:SUBLANES * (v + 1), :]
                    shift = (WKV_LANE_GROUP * (b - n4)) % LANES
                    if shift:
                        x = pltpu.roll(x, shift, axis=2)
                    acc = x if acc is None else jnp.where(grp == b, x, acc)
                pieces.append(acc)
        return pieces

    def prepare(t0, cnt):
        rs, ks, vs, ws, as_, gs = (to_scan(x, t0, cnt) for x in (r_ref, k_ref, v_ref, w_ref, a_ref, g_ref))
        kks = [ks[u] * kkp_ref[SUBLANES * u:SUBLANES * (u + 1), :] for u in range(len(ks))]
        ss = kks[0] * kks[0]
        for u in range(1, len(kks)):
            ss = ss + kks[u] * kks[u]
        inv_norm = 1.0 / jnp.maximum(jnp.sqrt(jnp.sum(ss, axis=1, keepdims=True)), 1e-12)
        for u in range(len(ks)):
            rows = slice(SUBLANES * u, SUBLANES * (u + 1))
            at = (pl.ds(t0, cnt), rows, slice(None))
            kk = kks[u] * inv_norm
            wd_s[at] = jnp.exp(-jnp.exp(-_softplus(-ws[u]) - 0.5))
            kk_s[at] = kk
            b_s[at] = kk * as_[u]
            kh_s[at] = ks[u] * (1.0 + (as_[u] - 1.0) * kap_ref[rows, :])
            r_s[at] = rs[u]
            v_s[at] = vs[u]
            g_s[at] = gs[u]

    def finish(t, y):
        inv_n = 1.0 / n
        mean = jnp.sum(y, axis=0, keepdims=True) * inv_n
        yc = y - mean
        var = jnp.sum(yc * yc, axis=0, keepdims=True) * inv_n
        bonus = jnp.sum(r_s[t] * kh_s[t] * rkp_ref[...], axis=0, keepdims=True) * v_s[t]
        o = (yc * lax.rsqrt(var + RWKV_GN_EPS) * lnw_ref[...] + lnb_ref[...] + bonus) * g_s[t]
        grp = lane_group(1)[0]
        for b in range(nb):
            parts = []
            for v in range(halves):
                acc = None
                for n4 in range(nb):
                    u = n4 * halves + v
                    x = o[SUBLANES * u:SUBLANES * (u + 1), :]
                    shift = (WKV_LANE_GROUP * (n4 - b)) % LANES
                    if shift:
                        x = pltpu.roll(x, shift, axis=1)
                    acc = x if acc is None else jnp.where(grp == n4, x, acc)
                parts.append(acc)
            o_ref[b, pl.ds(t, 1)] = jnp.concatenate(parts, axis=0).astype(o_ref.dtype)[None]

    prepare(0, ahead)
    q0 = jnp.zeros((n, LANES), F32)
    for j in range(n):
        q0 = q0 + s_ref[j] * kk_s[0, j:j + 1, :]

    def step(t, q):
        tn = jnp.minimum(t + 1, steps - 1)
        v = v_s[t]
        y = jnp.zeros((n, LANES), F32)
        qn = jnp.zeros((n, LANES), F32)
        for j in range(n):
            s_new = s_ref[j] * wd_s[t, j:j + 1, :] - q * b_s[t, j:j + 1, :] + v * kh_s[t, j:j + 1, :]
            s_ref[j] = s_new
            y = y + s_new * r_s[t, j:j + 1, :]
            qn = qn + s_new * kk_s[tn, j:j + 1, :]
        finish(t, y)
        prepare(jnp.minimum(t + ahead, steps - 1), 1)
        return qn

    lax.fori_loop(0, steps, step, q0)


def _to_scan_order(x, axis):
    axis = axis % x.ndim
    shp = x.shape
    split = shp[:axis] + (RWKV_HEADS, LANES // WKV_LANE_GROUP, WKV_ROWS_PER_BATCH) + shp[axis + 1:]
    order = tuple(range(axis)) + (axis + 2, axis + 1, axis) + tuple(range(axis + 3, x.ndim + 2))
    return x.reshape(split).transpose(order).reshape(shp)


def _scan_tile(p, bsz):
    return jnp.tile(p.reshape(RWKV_HEADS, RWKV_HEAD_DIM).T, (1, bsz))


def wkv_fused(r, k, v, w_raw, a_sig, g, k_k, k_a, r_k, ln_w, ln_b, bsz, seq):
    assert bsz * RWKV_HEADS == LANES and RWKV_HEADS == WKV_LANE_GROUP
    steps = min(WKV_T, seq)
    view = lambda x: x.reshape(bsz, seq, WKV_ROWS_PER_BATCH, LANES)
    blk = pl.BlockSpec((bsz, steps, WKV_ROWS_PER_BATCH, LANES), lambda i: (0, i, 0, 0))
    par = pl.BlockSpec((RWKV_HEAD_DIM, LANES), lambda i: (0, 0))
    scr = pltpu.VMEM((steps, RWKV_HEAD_DIM, LANES), F32)
    out = pl.pallas_call(
        functools.partial(_wkv_fused_kernel, steps=steps),
        grid=(seq // steps,),
        in_specs=[blk] * 6 + [par] * 5,
        out_specs=blk,
        out_shape=jax.ShapeDtypeStruct((bsz, seq, WKV_ROWS_PER_BATCH, LANES), BF16),
        scratch_shapes=[pltpu.VMEM((RWKV_HEAD_DIM, RWKV_HEAD_DIM, LANES), F32)] + [scr] * 7,
        compiler_params=_params("arbitrary"),
        name="wkv_fused",
    )(view(r), view(k), view(v), view(w_raw), view(a_sig), view(g),
      _scan_tile(k_k, bsz), _scan_tile(k_a, bsz), _scan_tile(r_k, bsz), _scan_tile(ln_w, bsz), _scan_tile(ln_b, bsz))
    return out.reshape(bsz * seq, D_MODEL)


def _router_kernel(h_ref, g_ref, wr_ref, info_ref, tile_cnt_ref, total_ref, carry_ref, *, tt):
    E = N_EXPERTS
    i = pl.program_id(0)

    @pl.when(i == 0)
    def _():
        carry_ref[...] = jnp.zeros_like(carry_ref)

    xh, xm, xl = _split3(_rms(h_ref[...]) * g_ref[...])
    wh, wm, wl = _split3(wr_ref[...])
    logits = (_dot_nt(wh, xh) + _dot_nt(wh, xm) + _dot_nt(wm, xh)
              + _dot_nt(wh, xl) + _dot_nt(wl, xh) + _dot_nt(wm, xm))
    eidx = lax.broadcasted_iota(I32, (E, tt), 0)
    v1 = jnp.max(logits, axis=0, keepdims=True)
    i1 = jnp.min(jnp.where(logits == v1, eidx, E), axis=0, keepdims=True)
    rest = jnp.where(eidx == i1, -jnp.inf, logits)
    v2 = jnp.max(rest, axis=0, keepdims=True)
    i2 = jnp.min(jnp.where(rest == v2, eidx, E), axis=0, keepdims=True)
    e2 = jnp.exp(v2 - v1)
    g1 = 1.0 / (1.0 + e2)
    g2 = e2 / (1.0 + e2)
    hit1 = eidx == i1
    hit2 = eidx == i2
    onehot = jnp.where(hit1 | hit2, 1.0, 0.0)
    rr = lax.broadcasted_iota(I32, (tt, tt), 0)
    cc = lax.broadcasted_iota(I32, (tt, tt), 1)
    before = jnp.where(rr < cc, 1.0, 0.0).astype(BF16)
    carry = carry_ref[...]
    carry_wide = jnp.concatenate([carry] * (tt // LANES), axis=1)
    cum = _dot(onehot.astype(BF16), before) + carry_wide
    rank1 = jnp.sum(jnp.where(hit1, cum, 0.0), axis=0, keepdims=True)
    rank2 = jnp.sum(jnp.where(hit2, cum, 0.0), axis=0, keepdims=True)
    zero = jnp.zeros((1, tt), F32)
    info_ref[...] = jnp.concatenate([i1.astype(F32), i2.astype(F32), rank1, rank2, g1, g2, zero, zero], axis=0)
    tile_cnt_ref[0] = carry
    carry = carry + _dot(onehot.astype(BF16), jnp.ones((tt, LANES), BF16))
    carry_ref[...] = carry
    total_ref[...] = carry


def moe_router(h, norm_gain, w_router):
    m, d = h.shape
    tt = min(MOE_TILE, m)
    nt = m // tt
    return pl.pallas_call(
        functools.partial(_router_kernel, tt=tt),
        grid=(nt,),
        in_specs=[pl.BlockSpec((tt, d), lambda i: (i, 0)), pl.BlockSpec((1, d), lambda i: (0, 0)),
                  pl.BlockSpec((N_EXPERTS, d), lambda i: (0, 0))],
        out_specs=[pl.BlockSpec((SUBLANES, tt), lambda i: (0, i)),
                   pl.BlockSpec((1, N_EXPERTS, LANES), lambda i: (i, 0, 0)),
                   pl.BlockSpec((N_EXPERTS, LANES), lambda i: (0, 0))],
        out_shape=[jax.ShapeDtypeStruct((SUBLANES, m), F32), jax.ShapeDtypeStruct((nt, N_EXPERTS, LANES), F32),
                   jax.ShapeDtypeStruct((N_EXPERTS, LANES), F32)],
        scratch_shapes=[pltpu.VMEM((N_EXPERTS, LANES), F32)],
        compiler_params=_params("arbitrary"),
        name="moe_router",
    )(h, norm_gain.reshape(1, d), w_router.T)


def _slots_kernel(start_ref, info_ref, o_ref):
    info = info_ref[...]
    e1, e2, r1, r2 = info[0:1], info[1:2], info[2:3], info[3:4]
    s1 = r1
    s2 = r2
    for e in range(N_EXPERTS):
        st = start_ref[e].astype(F32)
        s1 = s1 + jnp.where(e1 == float(e), st, 0.0)
        s2 = s2 + jnp.where(e2 == float(e), st, 0.0)
    o_ref[...] = jnp.concatenate([s1, s2, info[4:5], info[5:6], info[6:8], info[6:8]], axis=0)


def moe_slots(info, slot_start):
    m = info.shape[1]
    tt = min(2048, m)
    return pl.pallas_call(
        _slots_kernel,
        grid_spec=pltpu.PrefetchScalarGridSpec(
            num_scalar_prefetch=1, grid=(m // tt,),
            in_specs=[pl.BlockSpec((SUBLANES, tt), lambda i, st: (0, i))],
            out_specs=pl.BlockSpec((SUBLANES, tt), lambda i, st: (0, i))),
        out_shape=jax.ShapeDtypeStruct((SUBLANES, m), F32),
        compiler_params=_params("parallel"),
        name="moe_slots",
    )(slot_start, info)


def _dispatch_kernel(blk_ref, til_ref, flg_ref, slots_ref, x_ref, o_ref, go_ref, *, bs):
    w = pl.program_id(0)
    flags = flg_ref[w]
    first = (flags & 1) != 0
    real = (flags & 4) != 0

    def gathered():
        sl = slots_ref[...]
        sid = (blk_ref[w] * bs + lax.broadcasted_iota(I32, (bs, sl.shape[1]), 0)).astype(F32)
        hit1 = sid == sl[0:1, :]
        hit2 = sid == sl[1:2, :]
        onehot = jnp.where(hit1 | hit2, 1.0, 0.0).astype(BF16)
        gate = jnp.sum(jnp.where(hit1, sl[2:3, :], 0.0) + jnp.where(hit2, sl[3:4, :], 0.0), axis=1, keepdims=True)
        return _dot(onehot, x_ref[...]).astype(o_ref.dtype), jnp.broadcast_to(gate, go_ref.shape)

    @pl.when(first & real)
    def _():
        o_ref[...], go_ref[...] = gathered()

    @pl.when(first & jnp.logical_not(real))
    def _():
        o_ref[...] = jnp.zeros_like(o_ref)
        go_ref[...] = jnp.zeros_like(go_ref)

    @pl.when(jnp.logical_not(first) & real)
    def _():
        rows, gate = gathered()
        o_ref[...] += rows
        go_ref[...] += gate


def moe_dispatch(xn, slots, wl_blk, wl_til, wl_flg, n_slots):
    m, d = xn.shape
    bs = tt = MOE_TILE
    n_items = wl_blk.shape[0]
    return pl.pallas_call(
        functools.partial(_dispatch_kernel, bs=bs),
        grid_spec=pltpu.PrefetchScalarGridSpec(
            num_scalar_prefetch=3, grid=(n_items,),
            in_specs=[pl.BlockSpec((SUBLANES, tt), lambda w, b, t, f: (0, t[w])),
                      pl.BlockSpec((tt, d), lambda w, b, t, f: (t[w], 0))],
            out_specs=[pl.BlockSpec((bs, d), lambda w, b, t, f: (b[w], 0)),
                       pl.BlockSpec((bs, LANES), lambda w, b, t, f: (b[w], 0))]),
        out_shape=[jax.ShapeDtypeStruct((n_slots, d), BF16), jax.ShapeDtypeStruct((n_slots, LANES), F32)],
        compiler_params=_params("arbitrary"),
        name="moe_dispatch",
    )(wl_blk, wl_til, wl_flg, slots, xn)


def _moe_ffn_kernel(be_ref, rows_ref, x_ref, gate_ref, w1_ref, w3_ref, w2_ref, o_ref, acc_ref, *, sub):
    b = pl.program_id(0)
    f = pl.program_id(1)
    bm = x_ref.shape[0]

    @pl.when(f == 0)
    def _():
        acc_ref[...] = jnp.zeros_like(acc_ref)

    for rows in range(sub, bm + 1, sub):
        @pl.when(rows_ref[b] == rows)
        def _(rows=rows):
            x = x_ref[0:rows, :]
            h = _silu(_dot(x, w1_ref[0].astype(BF16))) * _dot(x, w3_ref[0].astype(BF16))
            acc_ref[0:rows, :] += _dot(h.astype(BF16), w2_ref[0].astype(BF16))

    @pl.when(f == pl.num_programs(1) - 1)
    def _():
        o_ref[...] = (acc_ref[...] * gate_ref[:, 0:1]).astype(o_ref.dtype)


def moe_ffn(xs, slot_gate, w1, w3, w2, block_expert, block_rows):
    n_slots, d = xs.shape
    ff = w1.shape[2]
    bm = MOE_FFN_ROWS
    tf = MOE_TF
    nf = ff // tf

    def f_of(b, f, rows):
        return jnp.where(rows[b] > 0, f, nf - 1)

    return pl.pallas_call(
        functools.partial(_moe_ffn_kernel, sub=MOE_ROW_GROUP),
        grid_spec=pltpu.PrefetchScalarGridSpec(
            num_scalar_prefetch=2, grid=(n_slots // bm, nf),
            in_specs=[pl.BlockSpec((bm, d), lambda b, f, be, nb: (b, 0)),
                      pl.BlockSpec((bm, LANES), lambda b, f, be, nb: (b, 0)),
                      pl.BlockSpec((1, d, tf), lambda b, f, be, nb: (be[b], 0, f_of(b, f, nb))),
                      pl.BlockSpec((1, d, tf), lambda b, f, be, nb: (be[b], 0, f_of(b, f, nb))),
                      pl.BlockSpec((1, tf, d), lambda b, f, be, nb: (be[b], f_of(b, f, nb), 0))],
            out_specs=pl.BlockSpec((bm, d), lambda b, f, be, nb: (b, 0)),
            scratch_shapes=[pltpu.VMEM((bm, d), F32)]),
        out_shape=jax.ShapeDtypeStruct((n_slots, d), BF16),
        compiler_params=_params("arbitrary", "arbitrary"),
        name="moe_ffn",
    )(block_expert, block_rows, xs, slot_gate, w1, w3, w2)


def _combine_kernel(til_ref, blk_ref, flg_ref, tok_ref, y_ref, res_ref, o_ref, *, bs):
    w = pl.program_id(0)
    flags = flg_ref[w]

    @pl.when((flags & 1) != 0)
    def _():
        o_ref[...] = res_ref[...]

    @pl.when((flags & 4) != 0)
    def _():
        tok = tok_ref[...]
        tt = tok.shape[0]
        sid = (blk_ref[w] * bs + lax.broadcasted_iota(I32, (tt, bs), 1)).astype(F32)
        onehot = jnp.where((sid == tok[:, 0:1]) | (sid == tok[:, 1:2]), 1.0, 0.0).astype(BF16)
        o_ref[...] += _dot(onehot, y_ref[...])


def moe_combine(y, tok_info, residual, wl_til, wl_blk, wl_flg):
    m, d = residual.shape
    bs = tt = MOE_TILE
    n_items = wl_til.shape[0]
    return pl.pallas_call(
        functools.partial(_combine_kernel, bs=bs),
        grid_spec=pltpu.PrefetchScalarGridSpec(
            num_scalar_prefetch=3, grid=(n_items,),
            in_specs=[pl.BlockSpec((tt, SUBLANES), lambda w, t, b, f: (t[w], 0)),
                      pl.BlockSpec((bs, d), lambda w, t, b, f: (b[w], 0)),
                      pl.BlockSpec((tt, d), lambda w, t, b, f: (t[w], 0))],
            out_specs=pl.BlockSpec((tt, d), lambda w, t, b, f: (t[w], 0))),
        out_shape=jax.ShapeDtypeStruct((m, d), F32),
        compiler_params=_params("arbitrary"),
        name="moe_combine",
    )(wl_til, wl_blk, wl_flg, tok_info, y, residual)


def _work_list(overlap, n_items):
    n_major, n_minor = overlap.shape
    has = jnp.any(overlap, axis=1)
    marked = jnp.concatenate([overlap[:, :1] | ~has[:, None], overlap[:, 1:]], axis=1)
    maj, mino = jnp.nonzero(marked, size=n_items, fill_value=-1)
    valid = maj >= 0
    real = valid & overlap[jnp.maximum(maj, 0), jnp.maximum(mino, 0)]
    prev = jnp.concatenate([jnp.full((1,), -2, maj.dtype), maj[:-1]])
    nxt = jnp.concatenate([maj[1:], jnp.full((1,), -2, maj.dtype)])
    flags = (valid & (maj != prev)) * 1 + (valid & (maj != nxt)) * 2 + real * 4
    last_maj = jnp.max(jnp.where(valid, maj, 0))
    last_min = jnp.sum(jnp.where(valid & (maj == last_maj) & (maj != nxt), mino, 0))
    maj = jnp.where(valid, maj, last_maj)
    mino = jnp.where(valid, mino, last_min)
    return maj.astype(I32), mino.astype(I32), flags.astype(I32)


def moe_layer(h, norm_gain, w_router, w1, w3, w2):
    m, d = h.shape
    E = N_EXPERTS
    tile = min(MOE_TILE, m)
    n_tiles = m // tile
    ffn_rows = MOE_FFN_ROWS
    xn = rmsnorm(h, norm_gain, BF16)
    info, tile_cnt, total = moe_router(h, norm_gain, w_router)
    before = tile_cnt[:, :, 0]
    counts = total[:, 0].astype(I32)
    padded = (counts + ffn_rows - 1) // ffn_rows * ffn_rows
    ends = jnp.cumsum(padded)
    starts = ends - padded
    n_slots = (m * TOP_K // ffn_rows + E) * ffn_rows
    n_ffn_blocks = n_slots // ffn_rows
    n_dblocks = n_slots // tile
    slots = moe_slots(info, starts.astype(I32))
    db_start = jnp.arange(n_dblocks, dtype=I32) * tile
    db_e = jnp.minimum(jnp.sum(db_start[:, None] >= ends[None, :], axis=1), E - 1)
    r0 = db_start - starts[db_e]
    r1 = jnp.minimum(r0 + tile, counts[db_e])
    after = jnp.concatenate([before[1:], counts[None, :].astype(F32)], axis=0).astype(I32)
    bef = before.astype(I32)
    ov = (bef[:, db_e].T < r1[:, None]) & (after[:, db_e].T > r0[:, None]) & (db_start < ends[-1])[:, None]
    n_items = n_dblocks + E * n_tiles
    d_blk, d_til, d_flg = _work_list(ov, n_items)
    c_til, c_blk, c_flg = _work_list(ov.T, n_items)
    xs, slot_gate = moe_dispatch(xn, slots, d_blk, d_til, d_flg, n_slots)
    fb_start = jnp.arange(n_ffn_blocks, dtype=I32) * ffn_rows
    block_expert = jnp.minimum(jnp.sum(fb_start[:, None] >= ends[None, :], axis=1), E - 1).astype(I32)
    filled = counts[block_expert] - (fb_start - starts[block_expert])
    grp = MOE_ROW_GROUP
    block_rows = jnp.where(fb_start < ends[-1], (jnp.clip(filled, 0, ffn_rows) + grp - 1) // grp * grp, 0).astype(I32)
    ys = moe_ffn(xs, slot_gate, w1, w3, w2, block_expert, block_rows)
    return moe_combine(ys, slots.T, h, c_til, c_blk, c_flg)


def even_mixer_layer(h, posf, p, bsz, seq):
    m = bsz * seq
    hn = rmsnorm(h, p["norm_mix"], BF16)
    w_in = p["w_in"]
    c = np.cumsum([0, D_SSM, D_CONV, SSM_HEADS, D_NSA] + [D_NSA_KV] * 6 + [3 * NSA_HEADS])
    w_proj = jnp.concatenate([w_in[:, c[1]:c[2]], w_in[:, c[4]:c[10]], w_in[:, c[0]:c[1]], w_in[:, c[3]:c[4]]], axis=1)
    proj = matmul([(hn, w_proj.astype(BF16))], out_dtype=BF16)
    kv0 = D_CONV
    w_small = jnp.concatenate([w_in[:, c[2]:c[3]], w_in[:, c[10]:c[11]]], axis=1)
    w_small = jnp.pad(w_small, ((0, 0), (0, LANES - w_small.shape[1]))).astype(BF16)
    small = matmul([(hn, w_small)])
    y_ssm = ssd_group(proj, (D_CONV + 6 * D_NSA_KV) // D_SSM, 0, small, p["conv_w"], p["conv_b"], p["dt_bias"],
                      p["a_log"], p["d_skip"], p["ssm_norm"], bsz, seq)
    qn, ksn, kwn = rope_norm(posf, proj, (D_CONV + 6 * D_NSA_KV + D_SSM) // D_NSA, kv0 // (6 * D_NSA_KV),
                             p["q_gain"], p["ks_gain"], p["kw_gain"])
    kv6 = proj[:, kv0:kv0 + 6 * D_NSA_KV]
    nch = seq // CMP_STRIDE

    def chunks(col0):
        u = kv6[:, col0:col0 + D_NSA_KV].reshape(bsz, nch, CMP_STRIDE, NSA_KV_HEADS, NSA_HEAD_DIM)
        return u.transpose(0, 3, 1, 2, 4).reshape(bsz, NSA_KV_HEADS, nch, CMP_STRIDE * NSA_HEAD_DIM)

    kc, vc = compress(chunks(0), chunks(D_NSA_KV), p["pe_k"], p["pe_v"], p["wk1"], p["wk2"], p["wv1"], p["wv2"],
                      p["kc_gain"])
    o_cmp, selt = cmp_select(qn, kc, vc, bsz, seq)
    def v_tiles(col0, tk):
        tk = min(tk, seq)
        u = kv6[:, col0:col0 + D_NSA_KV].reshape(bsz, seq // tk, tk, NSA_KV_HEADS, NSA_HEAD_DIM)
        return u.transpose(0, 3, 1, 4, 2)

    o_sel = masked_attention(qn, ksn, 0, v_tiles(3 * D_NSA_KV, ATT_TK), bsz, seq, "sel", selt)
    o_win = masked_attention(qn, kwn, 0, v_tiles(5 * D_NSA_KV, WIN_TK), bsz, seq, "win")
    y_nsa = nsa_combine(small, o_cmp, o_sel, o_win)
    w_out = p["w_out"].astype(BF16)
    return matmul([(y_ssm, w_out[:D_SSM]), (y_nsa, w_out[D_SSM:])], residual=h)


def rwkv_layer(h, p, bsz, seq):
    m, d = h.shape
    mixes = rwkv_mix(h, p["norm_mix"], p["mu"], seq)
    bf = lambda w: w.astype(BF16)

    def lora_pad(w_a, w_b):
        r = w_a.shape[1]
        rp = -(-r // LANES) * LANES
        return bf(jnp.pad(w_a, ((0, 0), (0, rp - r)))), bf(jnp.pad(w_b, ((0, rp - r), (0, 0))))

    so = _to_scan_order
    r = matmul([(mixes[0], bf(so(p["w_r"], 1)))], out_dtype=BF16)
    k = matmul([(mixes[2], bf(so(p["w_k"], 1)))], out_dtype=BF16)
    v = matmul([(mixes[3], bf(so(p["w_v"], 1)))], out_dtype=BF16)
    w1, w2 = lora_pad(p["w1"], so(p["w2"], 1))
    w_raw = lora(mixes[1], w1, w2, so(p["w0"], 0), "tanh", None, F32)
    a1, a2 = lora_pad(p["a1"], so(p["a2"], 1))
    a_sig = lora(mixes[4], a1, a2, so(p["a0"], 0), None, "sigmoid", BF16)
    g1, g2 = lora_pad(p["g1"], so(p["g2"], 1))
    g = lora(mixes[5], g1, g2, jnp.zeros((d,), F32), "sigmoid", None, BF16)
    o = wkv_fused(r, k, v, w_raw, a_sig, g, p["k_k"], p["k_a"], p["r_k"].reshape(-1), p["ln_w"], p["ln_b"], bsz, seq)
    return matmul([(o, bf(so(p["w_o"], 0)))], residual=h)


def kernel(x, mem, positions, norm_mix, norm_xattn, norm_mem, norm_ffn, xattn_wq, xattn_wkv, xattn_wo, xattn_q_gain, xattn_k_gain, ev_w_in, ev_conv_w, ev_conv_b, ev_dt_bias, ev_a_log, ev_d_skip, ev_ssm_norm, ev_q_gain, ev_kc_gain, ev_ks_gain, ev_kw_gain, ev_pe_k, ev_pe_v, ev_cmp_wk1, ev_cmp_wk2, ev_cmp_wv1, ev_cmp_wv2, ev_w_out, ev_ffn_w1, ev_ffn_w3, ev_ffn_w2, od_mu, od_w_r, od_w_k, od_w_v, od_w_o, od_w0, od_w1, od_w2, od_a0, od_a1, od_a2, od_g1, od_g2, od_k_k, od_k_a, od_r_k, od_ln_w, od_ln_b, od_router, od_moe_w1, od_moe_w3, od_moe_w2):
    bsz, seq, d = x.shape
    m = bsz * seq
    depth = norm_mix.shape[0]
    h = x.reshape(m, d)
    mem2d = mem.reshape(-1, d)
    posf = positions.astype(F32).reshape(m, 1)
    for layer in range(depth):
        i = layer // 2
        if layer % 2 == 0:
            p = dict(norm_mix=norm_mix[layer], w_in=ev_w_in[i], conv_w=ev_conv_w[i], conv_b=ev_conv_b[i],
                     dt_bias=ev_dt_bias[i], a_log=ev_a_log[i], d_skip=ev_d_skip[i], ssm_norm=ev_ssm_norm[i],
                     q_gain=ev_q_gain[i], kc_gain=ev_kc_gain[i], ks_gain=ev_ks_gain[i], kw_gain=ev_kw_gain[i],
                     pe_k=ev_pe_k[i], pe_v=ev_pe_v[i], wk1=ev_cmp_wk1[i], wk2=ev_cmp_wk2[i], wv1=ev_cmp_wv1[i],
                     wv2=ev_cmp_wv2[i], w_out=ev_w_out[i])
            h = even_mixer_layer(h, posf, p, bsz, seq)
        else:
            p = dict(norm_mix=norm_mix[layer], mu=od_mu[i], w_r=od_w_r[i], w_k=od_w_k[i], w_v=od_w_v[i], w_o=od_w_o[i],
                     w0=od_w0[i], w1=od_w1[i], w2=od_w2[i], a0=od_a0[i], a1=od_a1[i], a2=od_a2[i], g1=od_g1[i],
                     g2=od_g2[i], k_k=od_k_k[i], k_a=od_k_a[i], r_k=od_r_k[i], ln_w=od_ln_w[i], ln_b=od_ln_b[i])
            h = rwkv_layer(h, p, bsz, seq)
        h = memory_xattn(h, mem2d, norm_xattn[layer], norm_mem[layer], xattn_wq[layer], xattn_wkv[layer],
                         xattn_wo[layer], xattn_q_gain[layer], xattn_k_gain[layer], bsz, seq)
        if layer % 2 == 0:
            h = swiglu_ffn(h, norm_ffn[layer], ev_ffn_w1[i].astype(BF16), ev_ffn_w3[i].astype(BF16),
                           ev_ffn_w2[i].astype(BF16))
        else:
            h = moe_layer(h, norm_ffn[layer], od_router[i], od_moe_w1[i], od_moe_w3[i], od_moe_w2[i])
    return h.reshape(bsz, seq, d)
```

```python
import functools
import math

import numpy as np
import jax
import jax.numpy as jnp
from jax import lax
from jax.experimental import pallas as pl
from jax.experimental.pallas import tpu as pltpu

F32 = jnp.float32
BF16 = jnp.bfloat16
I32 = jnp.int32

D_MODEL = 2048
NORM_EPS = 1e-6
NEG_BIG = -1e30

SSM_HEADS = 32
SSM_HEAD_DIM = 64
D_SSM = SSM_HEADS * SSM_HEAD_DIM
SSM_GROUPS = 4
D_STATE = 128
CONV_WIDTH = 4
SSD_CHUNK = 128
D_CONV = D_SSM + 2 * SSM_GROUPS * D_STATE
D_SSM_GROUP = D_SSM // SSM_GROUPS
SSD_CONV_TAIL = 16
SSD_EXT_ROWS = 256

NSA_HEADS = 16
NSA_KV_HEADS = 4
NSA_HEAD_DIM = 128
NSA_Q_PER_KV = NSA_HEADS // NSA_KV_HEADS
D_NSA = NSA_HEADS * NSA_HEAD_DIM
D_NSA_KV = NSA_KV_HEADS * NSA_HEAD_DIM
CMP_BLOCK = 32
CMP_STRIDE = 16
CMP_HIDDEN = 256
SEL_BLOCK = 64
N_SELECT = 16
WINDOW = 512
ROPE_DIM = NSA_HEAD_DIM // 4
ROPE_THETA = 500000.0

RWKV_HEAD_DIM = 64
RWKV_HEADS = D_MODEL // RWKV_HEAD_DIM
RWKV_GN_EPS = 1e-5 * RWKV_HEAD_DIM

XATTN_HEADS = 4
XATTN_HEAD_DIM = 128
D_XATTN = XATTN_HEADS * XATTN_HEAD_DIM

N_EXPERTS = 8
TOP_K = 2

LANES = 128
SUBLANES = 8
VMEM_LIMIT_BYTES = 56 * 1024 * 1024

ROW_TILE = 512
MM_TM = 2048
MM_TN = 512
MM_VMEM_BUDGET = 44 * 1024 * 1024
FFN_TM = 512
FFN_TF = 512
FFN_SPLIT = 256
MOE_FFN_ROWS = 1024
MOE_TF = 256
MOE_ROW_GROUP = 256
COMBINE_TM = 512
MOE_TILE = 512
ATT_TQ = 512
ATT_TK = 512
WIN_TK = 512
LOG2E = 1.4426950408889634
CMP_TQ = 256
WKV_T = 64
WKV_LOOKAHEAD = 8


def _params(*sem):
    return pltpu.CompilerParams(dimension_semantics=sem, vmem_limit_bytes=VMEM_LIMIT_BYTES)


def _sigmoid(x):
    return 1.0 / (1.0 + jnp.exp(-x))


def _silu(x):
    return x * _sigmoid(x)


def _softplus(x):
    return jnp.maximum(x, 0.0) + jnp.log(1.0 + jnp.exp(-jnp.abs(x)))


def _split3(x):
    h = x.astype(BF16)
    r = x - h.astype(F32)
    m = r.astype(BF16)
    l = (r - m.astype(F32)).astype(BF16)
    return h, m, l


def _dot(a, b):
    return jnp.dot(a, b, preferred_element_type=F32)


def _dot_nt(a, b):
    return lax.dot_general(a, b, (((1,), (1,)), ((), ())), preferred_element_type=F32)


def _dot_f32_sel(x, e):
    h, m, l = _split3(x)
    return _dot(h, e) + _dot(m, e) + _dot(l, e)


def _sel_dot_f32(e, x):
    h, m, l = _split3(x)
    return _dot(e, h) + _dot(e, m) + _dot(e, l)


def _rms(x, eps=NORM_EPS):
    return x * lax.rsqrt(jnp.mean(x * x, axis=-1, keepdims=True) + eps)


def _rmsnorm_kernel(x_ref, g_ref, o_ref):
    o_ref[...] = (_rms(x_ref[...]) * g_ref[...]).astype(o_ref.dtype)


def rmsnorm(x, gain, out_dtype):
    m, d = x.shape
    tm = min(ROW_TILE, m)
    return pl.pallas_call(
        _rmsnorm_kernel,
        grid=(m // tm,),
        in_specs=[pl.BlockSpec((tm, d), lambda i: (i, 0)), pl.BlockSpec((1, d), lambda i: (0, 0))],
        out_specs=pl.BlockSpec((tm, d), lambda i: (i, 0)),
        out_shape=jax.ShapeDtypeStruct((m, d), out_dtype),
        compiler_params=_params("parallel"),
        name="rmsnorm",
    )(x, gain.reshape(1, d))


def _mm_kernel(*refs, n_pairs, has_res):
    o_ref = refs[-1]
    acc = None
    for p in range(n_pairs):
        d = _dot(refs[2 * p][...], refs[2 * p + 1][...])
        acc = d if acc is None else acc + d
    if has_res:
        acc = acc + refs[2 * n_pairs][...]
    o_ref[...] = acc.astype(o_ref.dtype)


def matmul(pairs, residual=None, out_dtype=F32):
    m = pairs[0][0].shape[0]
    n = pairs[0][1].shape[1]
    k_total = sum(a.shape[1] for a, _ in pairs)
    tm = min(MM_TM if k_total <= D_MODEL else MM_TM // 2, m)
    out_bytes = jnp.dtype(out_dtype).itemsize + (4 if residual is not None else 0)

    def vmem_bytes(tn):
        return 2 * (tm * k_total * 2 + k_total * tn * 2 + tm * tn * out_bytes)

    tn = next((t for t in (2 * MM_TN, MM_TN, 256, LANES) if n % t == 0 and vmem_bytes(t) <= MM_VMEM_BUDGET), LANES)
    in_specs, args = [], []
    for a, w in pairs:
        kk = a.shape[1]
        in_specs += [pl.BlockSpec((tm, kk), lambda i, j: (i, 0)), pl.BlockSpec((kk, tn), lambda i, j: (0, j))]
        args += [a, w]
    if residual is not None:
        in_specs.append(pl.BlockSpec((tm, tn), lambda i, j: (i, j)))
        args.append(residual)
    kern = functools.partial(_mm_kernel, n_pairs=len(pairs), has_res=residual is not None)
    return pl.pallas_call(
        kern,
        grid=(m // tm, n // tn),
        in_specs=in_specs,
        out_specs=pl.BlockSpec((tm, tn), lambda i, j: (i, j)),
        out_shape=jax.ShapeDtypeStruct((m, n), out_dtype),
        compiler_params=_params("parallel", "parallel"),
        name="matmul",
    )(*args)


def _act(x, act):
    if act == "tanh":
        return jnp.tanh(x)
    if act == "sigmoid":
        return _sigmoid(x)
    return x


def _lora_kernel(a_ref, wa_ref, wb_ref, bias_ref, o_ref, *, act_mid, act_out):
    mid = _act(_dot(a_ref[...], wa_ref[...]), act_mid).astype(BF16)
    o_ref[...] = _act(_dot(mid, wb_ref[...]) + bias_ref[...], act_out).astype(o_ref.dtype)


def lora(a, w_a, w_b, bias, act_mid, act_out, out_dtype):
    m, d = a.shape
    r = w_a.shape[1]
    n = w_b.shape[1]
    tm = min(2 * ROW_TILE, m)
    return pl.pallas_call(
        functools.partial(_lora_kernel, act_mid=act_mid, act_out=act_out),
        grid=(m // tm,),
        in_specs=[pl.BlockSpec((tm, d), lambda i: (i, 0)), pl.BlockSpec((d, r), lambda i: (0, 0)),
                  pl.BlockSpec((r, n), lambda i: (0, 0)), pl.BlockSpec((1, n), lambda i: (0, 0))],
        out_specs=pl.BlockSpec((tm, n), lambda i: (i, 0)),
        out_shape=jax.ShapeDtypeStruct((m, n), out_dtype),
        compiler_params=_params("parallel"),
        name="lora",
    )(a, w_a, w_b, bias.reshape(1, n).astype(F32))


def _ffn_kernel(g_ref, w1_ref, w3_ref, w2_ref, res_ref, o_ref, x_ref):
    f = pl.program_id(1)

    @pl.when(f == 0)
    def _():
        res = res_ref[...]
        o_ref[...] = res
        x_ref[...] = (_rms(res) * g_ref[...]).astype(x_ref.dtype)

    x = x_ref[...]
    tf = w1_ref.shape[1]
    cols = [slice(c, c + FFN_SPLIT) for c in range(0, tf, FFN_SPLIT)]
    gate_up = [(_dot(x, w1_ref[:, c]), _dot(x, w3_ref[:, c])) for c in cols]
    hs = [(_silu(g) * u).astype(BF16) for g, u in gate_up]
    acc = _dot(hs[0], w2_ref[cols[0], :])
    for h, c in zip(hs[1:], cols[1:]):
        acc = acc + _dot(h, w2_ref[c, :])
    o_ref[...] += acc


def swiglu_ffn(residual, norm_gain, w1, w3, w2):
    m, d = residual.shape
    ff = w1.shape[1]
    tm = min(FFN_TM, m)
    tf = FFN_TF
    return pl.pallas_call(
        _ffn_kernel,
        grid=(m // tm, ff // tf),
        in_specs=[pl.BlockSpec((1, d), lambda i, f: (0, 0)),
                  pl.BlockSpec((d, tf), lambda i, f: (0, f)),
                  pl.BlockSpec((d, tf), lambda i, f: (0, f)),
                  pl.BlockSpec((tf, d), lambda i, f: (f, 0)),
                  pl.BlockSpec((tm, d), lambda i, f: (i, 0))],
        out_specs=pl.BlockSpec((tm, d), lambda i, f: (i, 0)),
        out_shape=jax.ShapeDtypeStruct((m, d), F32),
        scratch_shapes=[pltpu.VMEM((tm, d), BF16)],
        compiler_params=_params("parallel", "arbitrary"),
        name="swiglu_ffn",
    )(norm_gain.reshape(1, d), w1, w3, w2, residual)


def _ssd_kernel(z_ref, xbc_ref, dt_ref, shift_ref, cw_ref, cb_ref, dtb_ref, alog_ref, dskip_ref, nw_ref, eh_ref,
                o_ref, ext_ref, st_ref):
    L = SSD_CHUNK
    G = SSM_GROUPS
    GW = D_SSM_GROUP
    T = SSD_CONV_TAIL
    c = pl.program_id(1)

    @pl.when(c == 0)
    def _():
        ext_ref[...] = jnp.zeros_like(ext_ref)
        st_ref[...] = jnp.zeros_like(st_ref)

    ext_ref[0:T, :] = ext_ref[L:L + T, :]
    ext_ref[T:L + T, :] = xbc_ref[...]
    shifted = _dot(shift_ref[...], ext_ref[...])
    acc = jnp.zeros((L, D_CONV), F32) + cb_ref[...]
    for k in range(CONV_WIDTH):
        acc = acc + cw_ref[k:k + 1, :] * shifted[k * L:(k + 1) * L, :]
    xc = _silu(acc)

    dt = _softplus(dt_ref[...] + dtb_ref[...])
    a = -jnp.exp(alog_ref[...])
    adt = dt * a
    row = lax.broadcasted_iota(I32, (L, L), 0)
    col = lax.broadcasted_iota(I32, (L, L), 1)
    causal = row >= col
    tril = jnp.where(causal, 1.0, 0.0).astype(BF16)
    a_cum = _sel_dot_f32(tril, adt)
    a_cum_t = a_cum.T
    eh = eh_ref[...]
    acx = _dot_f32_sel(a_cum, eh)
    dtx = _dot_f32_sel(dt, eh)
    a_end = acx[L - 1:L, :]
    xs = xc[:, :D_SSM]
    xdt = xs * dtx
    xdte = (xdt * jnp.exp(a_end - acx)).astype(BF16)
    from_start = jnp.exp(acx)
    lane = lax.broadcasted_iota(I32, (L, LANES), 1)
    lo_half = lane < SSM_HEAD_DIM
    z = z_ref[...].astype(F32)

    for g in range(G):
        bm = xc[:, D_SSM + g * D_STATE:D_SSM + (g + 1) * D_STATE]
        cm = xc[:, D_SSM + G * D_STATE + g * D_STATE:D_SSM + G * D_STATE + (g + 1) * D_STATE]
        bm_b = bm.astype(BF16)
        cm_b = cm.astype(BF16)
        cb = _dot_nt(cm_b, bm_b)
        st = st_ref[g]
        y_off = _dot(cm_b, st.astype(BF16)) * from_start[:, g * GW:(g + 1) * GW]
        pieces = []
        for pp in range(GW // LANES):
            h0 = g * (GW // SSM_HEAD_DIM) + 2 * pp
            ms = []
            for hh in (h0, h0 + 1):
                diff = a_cum[:, hh:hh + 1] - a_cum_t[hh:hh + 1, :]
                dec = jnp.exp(jnp.where(causal, diff, NEG_BIG))
                ms.append((cb * dec).astype(BF16))
            lhs = jnp.concatenate(ms, axis=1)
            c0 = g * GW + pp * LANES
            xp = xdt[:, c0:c0 + LANES]
            rhs = jnp.concatenate([jnp.where(lo_half, xp, 0.0), jnp.where(lo_half, 0.0, xp)], axis=0)
            pieces.append(_dot(lhs, rhs.astype(BF16)))
        y = jnp.concatenate(pieces, axis=1) + y_off
        y = y + xs[:, g * GW:(g + 1) * GW] * dskip_ref[:, g * GW:(g + 1) * GW]
        y = y * _silu(z[:, g * GW:(g + 1) * GW])
        y = _rms(y) * nw_ref[:, g * GW:(g + 1) * GW]
        o_ref[:, g * GW:(g + 1) * GW] = y.astype(o_ref.dtype)
        s_new = _dot(bm.T.astype(BF16), xdte[:, g * GW:(g + 1) * GW])
        st_ref[g] = st * jnp.exp(a_end[:, g * GW:(g + 1) * GW]) + s_new


def ssd_group(proj, z_col, xbc_col, small, conv_w, conv_b, dt_bias, a_log, d_skip, norm_w, bsz, seq):
    m = bsz * seq
    L = SSD_CHUNK
    nc = seq // L
    pad = LANES - SSM_HEADS
    eh = np.zeros((LANES, D_SSM), np.float32)
    for h in range(SSM_HEADS):
        eh[h, h * SSM_HEAD_DIM:(h + 1) * SSM_HEAD_DIM] = 1.0
    shift = np.zeros((CONV_WIDTH * L, SSD_EXT_ROWS), np.float32)
    for k in range(CONV_WIDTH):
        shift[k * L + np.arange(L), np.arange(L) + SSD_CONV_TAIL - (CONV_WIDTH - 1) + k] = 1.0
    row_spec = lambda w, col=0: pl.BlockSpec((L, w), lambda b, c: (b * nc + c, col))
    const = lambda r, w: pl.BlockSpec((r, w), lambda b, c: (0, 0))
    return pl.pallas_call(
        _ssd_kernel,
        grid=(bsz, nc),
        in_specs=[row_spec(D_SSM, z_col), row_spec(D_CONV, xbc_col), row_spec(LANES),
                  const(CONV_WIDTH * L, SSD_EXT_ROWS),
                  const(CONV_WIDTH, D_CONV), const(1, D_CONV), const(1, LANES), const(1, LANES),
                  const(1, D_SSM), const(1, D_SSM), const(LANES, D_SSM)],
        out_specs=row_spec(D_SSM),
        out_shape=jax.ShapeDtypeStruct((m, D_SSM), BF16),
        scratch_shapes=[pltpu.VMEM((SSD_EXT_ROWS, D_CONV), BF16),
                        pltpu.VMEM((SSM_GROUPS, D_STATE, D_SSM_GROUP), F32)],
        compiler_params=_params("parallel", "arbitrary"),
        name="ssd_group",
    )(proj, proj, small, jnp.asarray(shift, BF16), conv_w.T, conv_b.reshape(1, D_CONV),
      jnp.pad(dt_bias, (0, pad)).reshape(1, LANES), jnp.pad(a_log, (0, pad)).reshape(1, LANES),
      jnp.repeat(d_skip, SSM_HEAD_DIM).reshape(1, D_SSM), norm_w.reshape(1, D_SSM), jnp.asarray(eh, BF16))


def _rope_norm_kernel(pos_ref, invf_ref, q_ref, kv_ref, qg_ref, ksg_ref, kwg_ref, qo_ref, kso_ref, kwo_ref):
    ang = pos_ref[...] * invf_ref[...]
    cos = jnp.cos(ang)
    sin = jnp.sin(ang)
    lane = lax.broadcasted_iota(I32, ang.shape, 1)
    first = lane < ROPE_DIM // 2
    sin_signed = jnp.where(first, -sin, sin)

    def head(x, gain):
        y = _rms(x.astype(F32)) * gain
        partner = jnp.where(first, pltpu.roll(y, LANES - ROPE_DIM // 2, axis=1), pltpu.roll(y, ROPE_DIM // 2, axis=1))
        return y * cos + partner * sin_signed

    hd = NSA_HEAD_DIM
    scale = hd ** -0.5 * LOG2E
    for h in range(NSA_HEADS):
        qo_ref[:, h * hd:(h + 1) * hd] = (head(q_ref[:, h * hd:(h + 1) * hd], qg_ref[...]) * scale).astype(qo_ref.dtype)
    for h in range(NSA_KV_HEADS):
        ks = kv_ref[:, 2 * D_NSA_KV + h * hd:2 * D_NSA_KV + (h + 1) * hd]
        kw = kv_ref[:, 4 * D_NSA_KV + h * hd:4 * D_NSA_KV + (h + 1) * hd]
        kso_ref[:, h * hd:(h + 1) * hd] = head(ks, ksg_ref[...]).astype(kso_ref.dtype)
        kwo_ref[:, h * hd:(h + 1) * hd] = head(kw, kwg_ref[...]).astype(kwo_ref.dtype)


def rope_norm(posf, proj, q_col, kv_col, q_gain, ks_gain, kw_gain):
    m = proj.shape[0]
    tm = min(ROW_TILE, m)
    half = ROPE_DIM // 2
    inv = np.exp(-math.log(ROPE_THETA) * np.arange(0, ROPE_DIM, 2, dtype=np.float32) / ROPE_DIM).astype(np.float32)
    invf = np.zeros((1, LANES), np.float32)
    invf[0, :half] = inv
    invf[0, half:ROPE_DIM] = inv
    row = lambda w, col=0: pl.BlockSpec((tm, w), lambda i: (i, col))
    const = pl.BlockSpec((1, LANES), lambda i: (0, 0))
    return pl.pallas_call(
        _rope_norm_kernel,
        grid=(m // tm,),
        in_specs=[row(1), const, row(D_NSA, q_col), row(6 * D_NSA_KV, kv_col), const, const, const],
        out_specs=[row(D_NSA), row(D_NSA_KV), row(D_NSA_KV)],
        out_shape=[jax.ShapeDtypeStruct((m, D_NSA), BF16), jax.ShapeDtypeStruct((m, D_NSA_KV), BF16),
                   jax.ShapeDtypeStruct((m, D_NSA_KV), BF16)],
        compiler_params=_params("parallel"),
        name="rope_norm",
    )(posf, jnp.asarray(invf), proj, proj, q_gain.reshape(1, LANES), ks_gain.reshape(1, LANES), kw_gain.reshape(1, LANES))


def _compress_kernel(uk_ref, uv_ref, pek_ref, pev_ref, wk1_ref, wk2_ref, wv1_ref, wv2_ref, g_ref, kc_ref, vc_ref):
    def mlp(u, pe_ref, w1_ref, w2_ref):
        u = u.astype(F32)
        n = u.shape[0]
        h1 = _dot((u + pe_ref[0:1, :]).astype(BF16), w1_ref[0])
        h2 = _dot((u + pe_ref[1:2, :]).astype(BF16), w1_ref[1])
        pre = h1 + pltpu.roll(h2, n - 1, axis=0)
        return _dot(_silu(pre).astype(BF16), w2_ref[...])

    kc = mlp(uk_ref[0, 0], pek_ref, wk1_ref, wk2_ref)
    kc_ref[0, 0] = _rms(kc) * g_ref[...]
    vc_ref[0, 0] = mlp(uv_ref[0, 0], pev_ref, wv1_ref, wv2_ref)


def compress(uk, uv, pe_k, pe_v, wk1, wk2, wv1, wv2, kc_gain):
    bsz, nkv, nch, width = uk.shape
    half = CMP_STRIDE * NSA_HEAD_DIM
    u_spec = pl.BlockSpec((1, 1, nch, width), lambda b, g: (b, g, 0, 0))
    o_spec = pl.BlockSpec((1, 1, nch, NSA_HEAD_DIM), lambda b, g: (b, g, 0, 0))
    c2 = lambda s: pl.BlockSpec(s, lambda b, g: (0, 0))
    c3 = lambda s: pl.BlockSpec(s, lambda b, g: (0, 0, 0))
    o_shape = jax.ShapeDtypeStruct((bsz, nkv, nch, NSA_HEAD_DIM), F32)
    return pl.pallas_call(
        _compress_kernel,
        grid=(bsz, nkv),
        in_specs=[u_spec, u_spec, c2((2, half)), c2((2, half)),
                  c3((2, half, CMP_HIDDEN)), c2((CMP_HIDDEN, NSA_HEAD_DIM)),
                  c3((2, half, CMP_HIDDEN)), c2((CMP_HIDDEN, NSA_HEAD_DIM)), c2((1, NSA_HEAD_DIM))],
        out_specs=[o_spec, o_spec],
        out_shape=[o_shape, o_shape],
        compiler_params=_params("parallel", "parallel"),
        name="nsa_compress",
    )(uk, uv, pe_k.reshape(2, half), pe_v.reshape(2, half),
      wk1.reshape(2, half, CMP_HIDDEN).astype(BF16), wk2.astype(BF16),
      wv1.reshape(2, half, CMP_HIDDEN).astype(BF16), wv2.astype(BF16), kc_gain.reshape(1, NSA_HEAD_DIM))


def _stack_heads(q):
    return jnp.concatenate([q[:, r * NSA_HEAD_DIM:(r + 1) * NSA_HEAD_DIM] for r in range(NSA_Q_PER_KV)], axis=0)


def _unstack_heads(o, tq):
    return jnp.concatenate([o[r * tq:(r + 1) * tq, :] for r in range(NSA_Q_PER_KV)], axis=1)


def _cmp_select_kernel(q_ref, kc_ref, vc_ref, ovt_ref, o_ref, sel_ref, *, tq, n_blk):
    R = NSA_Q_PER_KV
    q0 = pl.program_id(2) * tq
    q4 = _stack_heads(q_ref[...])
    kc = kc_ref[0, 0].astype(BF16)
    ncmp = kc.shape[0]
    s = _dot_nt(q4, kc)
    t_row = q0 + lax.broadcasted_iota(I32, (tq, ncmp), 0)
    c_col = lax.broadcasted_iota(I32, (tq, ncmp), 1)
    visf = jnp.where((c_col * CMP_STRIDE + CMP_BLOCK - 1) <= t_row, 1.0, 0.0)
    vis = jnp.concatenate([visf] * R, axis=0) > 0.5
    s = jnp.where(vis, s, NEG_BIG)
    mx = jnp.max(s, axis=-1, keepdims=True)
    e = jnp.where(vis, jnp.exp2(s - mx), 0.0)
    den = jnp.sum(e, axis=-1, keepdims=True)
    p = e / jnp.maximum(den, 1e-30)
    o = _dot(p.astype(BF16), vc_ref[0, 0].astype(BF16))
    o_ref[...] = _unstack_heads(o, tq).astype(o_ref.dtype)

    psum = p[0:tq]
    for r in range(1, R):
        psum = psum + p[r * tq:(r + 1) * tq]
    h, m, l = _split3(psum)
    ovt = ovt_ref[...]
    imp = _dot_nt(ovt, h) + _dot_nt(ovt, m) + _dot_nt(ovt, l)
    jdx = lax.broadcasted_iota(I32, (n_blk, tq), 0)
    t_lane = q0 + lax.broadcasted_iota(I32, (n_blk, tq), 1)
    cur = t_lane // SEL_BLOCK
    causal = jdx <= cur
    forced = ((jdx == 0) | (jdx >= cur - 1)) & causal
    score = jnp.where(forced, jnp.inf, jnp.where(causal, imp, -jnp.inf))
    rank = jnp.zeros((n_blk, tq), F32)
    for jp in range(n_blk):
        rowv = score[jp:jp + 1, :]
        beats = (rowv > score) | ((rowv == score) & (jdx > jp))
        rank = rank + jnp.where(beats, 1.0, 0.0)
    sel = jnp.where((rank < float(min(N_SELECT, n_blk))) & causal, 1.0, 0.0)
    if n_blk < LANES:
        sel = jnp.concatenate([sel, jnp.zeros((LANES - n_blk, tq), F32)], axis=0)
    sel_ref[0, 0] = sel.astype(sel_ref.dtype)


def cmp_select(qn, kc, vc, bsz, seq):
    m = bsz * seq
    G = NSA_KV_HEADS
    tq = CMP_TQ
    nq = seq // tq
    ncmp = kc.shape[2]
    n_blk = seq // SEL_BLOCK
    assert n_blk <= LANES and tq % LANES == 0
    c0 = np.arange(ncmp)[None, :] * CMP_STRIDE
    s0 = np.arange(n_blk)[:, None] * SEL_BLOCK
    ov = np.clip(np.minimum(c0 + CMP_BLOCK, s0 + SEL_BLOCK) - np.maximum(c0, s0), 0, None) / CMP_STRIDE
    n_cmp_valid = (seq - CMP_BLOCK) // CMP_STRIDE + 1
    ov[:, n_cmp_valid:] = 0.0
    kv_spec = pl.BlockSpec((1, 1, ncmp, NSA_HEAD_DIM), lambda b, g, i: (b, g, 0, 0))
    kern = functools.partial(_cmp_select_kernel, tq=tq, n_blk=n_blk)
    return pl.pallas_call(
        kern,
        grid=(bsz, G, nq),
        in_specs=[pl.BlockSpec((tq, NSA_Q_PER_KV * NSA_HEAD_DIM), lambda b, g, i: (b * nq + i, g)),
                  kv_spec, kv_spec, pl.BlockSpec((n_blk, ncmp), lambda b, g, i: (0, 0))],
        out_specs=[pl.BlockSpec((tq, NSA_Q_PER_KV * NSA_HEAD_DIM), lambda b, g, i: (b * nq + i, g)),
                   pl.BlockSpec((1, 1, LANES, tq), lambda b, g, i: (b, g, 0, i))],
        out_shape=[jax.ShapeDtypeStruct((m, D_NSA), BF16), jax.ShapeDtypeStruct((bsz, G, LANES, seq), BF16)],
        compiler_params=_params("parallel", "parallel", "parallel"),
        name="nsa_cmp_select",
    )(qn, kc, vc, jnp.asarray(ov, BF16))


def _masked_attn_kernel(q_ref, k_ref, vt_ref, *rest, mode, tq, tk):
    if mode == "sel":
        selt_ref, o_ref = rest
    else:
        (o_ref,) = rest
    R = NSA_Q_PER_KV
    hd = NSA_HEAD_DIM
    q0 = pl.program_id(2) * tq
    key_row = lax.broadcasted_iota(I32, (tk, tq), 0)
    t_lane = q0 + lax.broadcasted_iota(I32, (tk, tq), 1)
    hi = (q0 + tq - 1) // tk + 1
    q = q_ref[...]
    if mode == "sel":
        key_row_b = lax.broadcasted_iota(I32, (tk, LANES), 0)
        blk_lane = lax.broadcasted_iota(I32, (tk, LANES), 1)
        penalty = ((1.0 - selt_ref[0, 0].astype(F32)) * NEG_BIG).astype(BF16)
        rhs = [jnp.concatenate([q[:, r * hd:(r + 1) * hd].astype(F32).T.astype(BF16), penalty], axis=0)
               for r in range(R)]
        lo = 0
    else:
        lo = jnp.maximum(q0 - (WINDOW - 1), 0) // tk

    def body(kt, carry, diagonal=True):
        k0 = kt * tk
        k_t = k_ref[pl.ds(pl.multiple_of(k0, tk), tk), :]
        v_t = vt_ref[0, 0, kt]
        key = k0 + key_row
        if mode == "sel":
            expand = jnp.where((k0 + key_row_b) // SEL_BLOCK == blk_lane, 1.0, 0.0).astype(BF16)
            lhs = jnp.concatenate([k_t, expand], axis=1)
            ss = [_dot(lhs, rhs[r]) for r in range(R)]
            if diagonal:
                ss = [jnp.where(key <= t_lane, s, NEG_BIG) for s in ss]
        else:
            bias = jnp.where((key <= t_lane) & (key > t_lane - WINDOW), 0.0, NEG_BIG)
            ss = [_dot_nt(k_t, q[:, r * hd:(r + 1) * hd]) + bias for r in range(R)]
        stats = []
        for r in range(R):
            m_prev, l_prev = carry[3 * r], carry[3 * r + 1]
            m_new = jnp.maximum(m_prev, jnp.max(ss[r], axis=0, keepdims=True))
            p = jnp.exp2(ss[r] - m_new)
            alpha = jnp.exp2(m_prev - m_new)
            stats.append((m_new, alpha * l_prev + jnp.sum(p, axis=0, keepdims=True), alpha, p.astype(BF16)))
        new = []
        for r in range(R):
            m_new, l_new, alpha, p = stats[r]
            new += [m_new, l_new, alpha * carry[3 * r + 2] + _dot(v_t, p)]
        return tuple(new)

    init = (jnp.full((1, tq), NEG_BIG, F32), jnp.zeros((1, tq), F32), jnp.zeros((hd, tq), F32)) * R
    if mode == "sel":
        out = lax.fori_loop(lo, hi - 1, functools.partial(body, diagonal=False), init)
        out = body(hi - 1, out)
    else:
        out = lax.fori_loop(lo, hi, body, init)
    o = jnp.concatenate([(out[3 * r + 2] / out[3 * r + 1]).T for r in range(R)], axis=1)
    o_ref[...] = o.astype(o_ref.dtype)


def masked_attention(qn, k, k_col0, vt, bsz, seq, mode, selt=None):
    m = bsz * seq
    G = NSA_KV_HEADS
    tq = min(ATT_TQ, seq)
    tk = vt.shape[-1]
    assert tk % tq == 0
    nq = seq // tq
    hd = NSA_HEAD_DIM
    q_spec = pl.BlockSpec((tq, NSA_Q_PER_KV * hd), lambda b, g, i: (b * nq + i, g))
    in_specs = [q_spec,
                pl.BlockSpec((seq, hd), lambda b, g, i: (b, k_col0 + g)),
                pl.BlockSpec((1, 1, seq // tk, hd, tk), lambda b, g, i: (b, g, 0, 0, 0))]
    args = [qn, k, vt]
    if mode == "sel":
        in_specs.append(pl.BlockSpec((1, 1, LANES, tq), lambda b, g, i: (b, g, 0, i)))
        args.append(selt)
    return pl.pallas_call(
        functools.partial(_masked_attn_kernel, mode=mode, tq=tq, tk=tk),
        grid=(bsz, G, nq),
        in_specs=in_specs,
        out_specs=q_spec,
        out_shape=jax.ShapeDtypeStruct((m, D_NSA), BF16),
        compiler_params=_params("parallel", "parallel", "arbitrary"),
        name="nsa_attn_" + mode,
    )(*args)


def _nsa_combine_kernel(g_ref, e_ref, oc_ref, os_ref, ow_ref, o_ref):
    gate = _sigmoid(g_ref[...])
    gx = _dot_f32_sel(gate, e_ref[...])
    y = (gx[:, :D_NSA] * oc_ref[...].astype(F32) + gx[:, D_NSA:2 * D_NSA] * os_ref[...].astype(F32)
         + gx[:, 2 * D_NSA:] * ow_ref[...].astype(F32))
    o_ref[...] = y.astype(o_ref.dtype)


def nsa_combine(small, o_cmp, o_sel, o_win):
    m = small.shape[0]
    tm = min(COMBINE_TM, m)
    e = np.zeros((LANES, 3 * D_NSA), np.float32)
    for h in range(NSA_HEADS):
        for br in range(3):
            e[SSM_HEADS + 3 * h + br, br * D_NSA + h * NSA_HEAD_DIM:br * D_NSA + (h + 1) * NSA_HEAD_DIM] = 1.0
    row = lambda w: pl.BlockSpec((tm, w), lambda i: (i, 0))
    return pl.pallas_call(
        _nsa_combine_kernel,
        grid=(m // tm,),
        in_specs=[row(LANES), pl.BlockSpec((LANES, 3 * D_NSA), lambda i: (0, 0)), row(D_NSA), row(D_NSA), row(D_NSA)],
        out_specs=row(D_NSA),
        out_shape=jax.ShapeDtypeStruct((m, D_NSA), BF16),
        compiler_params=_params("parallel"),
        name="nsa_combine",
    )(small, jnp.asarray(e, BF16), o_cmp, o_sel, o_win)


def _xattn_kernel(q_ref, kv_ref, qg_ref, kg_ref, o_ref):
    hd = XATTN_HEAD_DIM
    scale = hd ** -0.5
    outs = []
    for h in range(XATTN_HEADS):
        q = (_rms(q_ref[:, h * hd:(h + 1) * hd]) * qg_ref[...]).astype(BF16)
        k = (_rms(kv_ref[:, h * hd:(h + 1) * hd]) * kg_ref[...]).astype(BF16)
        v = kv_ref[:, D_XATTN + h * hd:D_XATTN + (h + 1) * hd].astype(BF16)
        s = _dot_nt(q, k) * scale
        e = jnp.exp(s - jnp.max(s, axis=-1, keepdims=True))
        p = e / jnp.sum(e, axis=-1, keepdims=True)
        outs.append(_dot(p.astype(BF16), v))
    o_ref[...] = jnp.concatenate(outs, axis=1).astype(o_ref.dtype)


def xattn_core(q, kv, q_gain, k_gain, bsz, seq):
    m = bsz * seq
    n_mem = kv.shape[0] // bsz
    tm = min(ROW_TILE, seq)
    nt = seq // tm
    return pl.pallas_call(
        _xattn_kernel,
        grid=(bsz, nt),
        in_specs=[pl.BlockSpec((tm, D_XATTN), lambda b, i: (b * nt + i, 0)),
                  pl.BlockSpec((n_mem, 2 * D_XATTN), lambda b, i: (b, 0)),
                  pl.BlockSpec((1, XATTN_HEAD_DIM), lambda b, i: (0, 0)),
                  pl.BlockSpec((1, XATTN_HEAD_DIM), lambda b, i: (0, 0))],
        out_specs=pl.BlockSpec((tm, D_XATTN), lambda b, i: (b * nt + i, 0)),
        out_shape=jax.ShapeDtypeStruct((m, D_XATTN), BF16),
        compiler_params=_params("parallel", "parallel"),
        name="xattn_core",
    )(q, kv, q_gain.reshape(1, XATTN_HEAD_DIM), k_gain.reshape(1, XATTN_HEAD_DIM))


def _norm_mm_kernel(h_ref, g_ref, w_ref, o_ref):
    x = (_rms(h_ref[...]) * g_ref[...]).astype(BF16)
    o_ref[...] = _dot(x, w_ref[...]).astype(o_ref.dtype)


def norm_matmul(h, gain, w):
    m, d = h.shape
    n = w.shape[1]
    tm = min(ROW_TILE, m)
    return pl.pallas_call(
        _norm_mm_kernel,
        grid=(m // tm,),
        in_specs=[pl.BlockSpec((tm, d), lambda i: (i, 0)), pl.BlockSpec((1, d), lambda i: (0, 0)),
                  pl.BlockSpec((d, n), lambda i: (0, 0))],
        out_specs=pl.BlockSpec((tm, n), lambda i: (i, 0)),
        out_shape=jax.ShapeDtypeStruct((m, n), F32),
        compiler_params=_params("parallel"),
        name="norm_matmul",
    )(h, gain.reshape(1, d), w)


def memory_xattn(h, mem2d, norm_x, norm_m, wq, wkv, wo, q_gain, k_gain, bsz, seq):
    q = norm_matmul(h, norm_x, wq.astype(BF16))
    kv = norm_matmul(mem2d, norm_m, wkv.astype(BF16))
    o = xattn_core(q, kv, q_gain, k_gain, bsz, seq)
    return matmul([(o, wo.astype(BF16))], residual=h)


def _mix_kernel(h_ref, hp_ref, g_ref, mu_ref, *rest, tm, tiles_per_seq):
    outs, ext_ref = rest[:-1], rest[-1]
    i = pl.program_id(0)
    hn = _rms(h_ref[...]) * g_ref[...]
    hp = _rms(hp_ref[...]) * g_ref[...]
    hp = jnp.where(i % tiles_per_seq == 0, 0.0, hp)
    ext_ref[0:SUBLANES, :] = hp
    ext_ref[SUBLANES:tm + SUBLANES, :] = hn
    xx = ext_ref[SUBLANES - 1:tm + SUBLANES - 1, :] - hn
    for j, o_ref in enumerate(outs):
        o_ref[...] = (hn + xx * mu_ref[j:j + 1, :]).astype(o_ref.dtype)


def rwkv_mix(h, gain, mu, seq):
    m, d = h.shape
    tm = min(ROW_TILE, seq)
    n_mix = mu.shape[0]
    kern = functools.partial(_mix_kernel, tm=tm, tiles_per_seq=seq // tm)
    row = pl.BlockSpec((tm, d), lambda i: (i, 0))
    return pl.pallas_call(
        kern,
        grid=(m // tm,),
        in_specs=[row, pl.BlockSpec((SUBLANES, d), lambda i: (jnp.maximum(i * (tm // SUBLANES) - 1, 0), 0)),
                  pl.BlockSpec((1, d), lambda i: (0, 0)), pl.BlockSpec((n_mix, d), lambda i: (0, 0))],
        out_specs=[row] * n_mix,
        out_shape=[jax.ShapeDtypeStruct((m, d), BF16)] * n_mix,
        scratch_shapes=[pltpu.VMEM((tm + SUBLANES, d), F32)],
        compiler_params=_params("parallel"),
        name="rwkv_mix",
    )(h, h, gain.reshape(1, d), mu)


WKV_LANE_GROUP = LANES // 4
WKV_ROWS_PER_BATCH = D_MODEL // LANES


def _wkv_fused_kernel(r_ref, k_ref, v_ref, w_ref, a_ref, g_ref, kkp_ref, kap_ref, rkp_ref, lnw_ref, lnb_ref,
                      o_ref, s_ref, wd_s, kh_s, kk_s, b_s, r_s, v_s, g_s, *, steps):
    n = RWKV_HEAD_DIM
    nb = LANES // WKV_LANE_GROUP
    halves = WKV_ROWS_PER_BATCH // SUBLANES
    ahead = min(WKV_LOOKAHEAD, steps)

    @pl.when(pl.program_id(0) == 0)
    def _():
        s_ref[...] = jnp.zeros_like(s_ref)

    def lane_group(cnt):
        return lax.broadcasted_iota(I32, (cnt, SUBLANES, LANES), 2) // WKV_LANE_GROUP

    def to_scan(x_ref, t0, cnt):
        grp = lane_group(cnt)
        rows = [x_ref[b, pl.ds(t0, cnt)].astype(F32) for b in range(nb)]
        pieces = []
        for n4 in range(nb):
            for v in range(halves):
                acc = None
                for b in range(nb):
                    x = rows[b][:, SUBLANES * v:SUBLANES * (v + 1), :]
                    shift = (WKV_LANE_GROUP * (b - n4)) % LANES
                    if shift:
                        x = pltpu.roll(x, shift, axis=2)
                    acc = x if acc is None else jnp.where(grp == b, x, acc)
                pieces.append(acc)
        return pieces

    def prepare(t0, cnt):
        rs, ks, vs, ws, as_, gs = (to_scan(x, t0, cnt) for x in (r_ref, k_ref, v_ref, w_ref, a_ref, g_ref))
        kks = [ks[u] * kkp_ref[SUBLANES * u:SUBLANES * (u + 1), :] for u in range(len(ks))]
        ss = kks[0] * kks[0]
        for u in range(1, len(kks)):
            ss = ss + kks[u] * kks[u]
        inv_norm = 1.0 / jnp.maximum(jnp.sqrt(jnp.sum(ss, axis=1, keepdims=True)), 1e-12)
        for u in range(len(ks)):
            rows = slice(SUBLANES * u, SUBLANES * (u + 1))
            at = (pl.ds(t0, cnt), rows, slice(None))
            kk = kks[u] * inv_norm
            wd_s[at] = jnp.exp(-jnp.exp(-_softplus(-ws[u]) - 0.5))
            kk_s[at] = kk
            b_s[at] = kk * as_[u]
            kh_s[at] = ks[u] * (1.0 + (as_[u] - 1.0) * kap_ref[rows, :])
            r_s[at] = rs[u]
            v_s[at] = vs[u]
            g_s[at] = gs[u]

    def finish(t, y):
        inv_n = 1.0 / n
        mean = jnp.sum(y, axis=0, keepdims=True) * inv_n
        yc = y - mean
        var = jnp.sum(yc * yc, axis=0, keepdims=True) * inv_n
        bonus = jnp.sum(r_s[t] * kh_s[t] * rkp_ref[...], axis=0, keepdims=True) * v_s[t]
        o = (yc * lax.rsqrt(var + RWKV_GN_EPS) * lnw_ref[...] + lnb_ref[...] + bonus) * g_s[t]
        grp = lane_group(1)[0]
        for b in range(nb):
            parts = []
            for v in range(halves):
                acc = None
                for n4 in range(nb):
                    u = n4 * halves + v
                    x = o[SUBLANES * u:SUBLANES * (u + 1), :]
                    shift = (WKV_LANE_GROUP * (n4 - b)) % LANES
                    if shift:
                        x = pltpu.roll(x, shift, axis=1)
                    acc = x if acc is None else jnp.where(grp == n4, x, acc)
                parts.append(acc)
            o_ref[b, pl.ds(t, 1)] = jnp.concatenate(parts, axis=0).astype(o_ref.dtype)[None]

    prepare(0, ahead)
    q0 = jnp.zeros((n, LANES), F32)
    for j in range(n):
        q0 = q0 + s_ref[j] * kk_s[0, j:j + 1, :]

    def step(t, q):
        tn = jnp.minimum(t + 1, steps - 1)
        v = v_s[t]
        y = jnp.zeros((n, LANES), F32)
        qn = jnp.zeros((n, LANES), F32)
        for j in range(n):
            s_new = s_ref[j] * wd_s[t, j:j + 1, :] - q * b_s[t, j:j + 1, :] + v * kh_s[t, j:j + 1, :]
            s_ref[j] = s_new
            y = y + s_new * r_s[t, j:j + 1, :]
            qn = qn + s_new * kk_s[tn, j:j + 1, :]
        finish(t, y)
        prepare(jnp.minimum(t + ahead, steps - 1), 1)
        return qn

    lax.fori_loop(0, steps, step, q0)


def _to_scan_order(x, axis):
    axis = axis % x.ndim
    shp = x.shape
    split = shp[:axis] + (RWKV_HEADS, LANES // WKV_LANE_GROUP, WKV_ROWS_PER_BATCH) + shp[axis + 1:]
    order = tuple(range(axis)) + (axis + 2, axis + 1, axis) + tuple(range(axis + 3, x.ndim + 2))
    return x.reshape(split).transpose(order).reshape(shp)


def _scan_tile(p, bsz):
    return jnp.tile(p.reshape(RWKV_HEADS, RWKV_HEAD_DIM).T, (1, bsz))


def wkv_fused(r, k, v, w_raw, a_sig, g, k_k, k_a, r_k, ln_w, ln_b, bsz, seq):
    assert bsz * RWKV_HEADS == LANES and RWKV_HEADS == WKV_LANE_GROUP
    steps = min(WKV_T, seq)
    view = lambda x: x.reshape(bsz, seq, WKV_ROWS_PER_BATCH, LANES)
    blk = pl.BlockSpec((bsz, steps, WKV_ROWS_PER_BATCH, LANES), lambda i: (0, i, 0, 0))
    par = pl.BlockSpec((RWKV_HEAD_DIM, LANES), lambda i: (0, 0))
    scr = pltpu.VMEM((steps, RWKV_HEAD_DIM, LANES), F32)
    out = pl.pallas_call(
        functools.partial(_wkv_fused_kernel, steps=steps),
        grid=(seq // steps,),
        in_specs=[blk] * 6 + [par] * 5,
        out_specs=blk,
        out_shape=jax.ShapeDtypeStruct((bsz, seq, WKV_ROWS_PER_BATCH, LANES), BF16),
        scratch_shapes=[pltpu.VMEM((RWKV_HEAD_DIM, RWKV_HEAD_DIM, LANES), F32)] + [scr] * 7,
        compiler_params=_params("arbitrary"),
        name="wkv_fused",
    )(view(r), view(k), view(v), view(w_raw), view(a_sig), view(g),
      _scan_tile(k_k, bsz), _scan_tile(k_a, bsz), _scan_tile(r_k, bsz), _scan_tile(ln_w, bsz), _scan_tile(ln_b, bsz))
    return out.reshape(bsz * seq, D_MODEL)


def _router_kernel(h_ref, g_ref, wr_ref, info_ref, tile_cnt_ref, total_ref, carry_ref, *, tt):
    E = N_EXPERTS
    i = pl.program_id(0)

    @pl.when(i == 0)
    def _():
        carry_ref[...] = jnp.zeros_like(carry_ref)

    xh, xm, xl = _split3(_rms(h_ref[...]) * g_ref[...])
    wh, wm, wl = _split3(wr_ref[...])
    logits = (_dot_nt(wh, xh) + _dot_nt(wh, xm) + _dot_nt(wm, xh)
              + _dot_nt(wh, xl) + _dot_nt(wl, xh) + _dot_nt(wm, xm))
    eidx = lax.broadcasted_iota(I32, (E, tt), 0)
    v1 = jnp.max(logits, axis=0, keepdims=True)
    i1 = jnp.min(jnp.where(logits == v1, eidx, E), axis=0, keepdims=True)
    rest = jnp.where(eidx == i1, -jnp.inf, logits)
    v2 = jnp.max(rest, axis=0, keepdims=True)
    i2 = jnp.min(jnp.where(rest == v2, eidx, E), axis=0, keepdims=True)
    e2 = jnp.exp(v2 - v1)
    g1 = 1.0 / (1.0 + e2)
    g2 = e2 / (1.0 + e2)
    hit1 = eidx == i1
    hit2 = eidx == i2
    onehot = jnp.where(hit1 | hit2, 1.0, 0.0)
    rr = lax.broadcasted_iota(I32, (tt, tt), 0)
    cc = lax.broadcasted_iota(I32, (tt, tt), 1)
    before = jnp.where(rr < cc, 1.0, 0.0).astype(BF16)
    carry = carry_ref[...]
    carry_wide = jnp.concatenate([carry] * (tt // LANES), axis=1)
    cum = _dot(onehot.astype(BF16), before) + carry_wide
    rank1 = jnp.sum(jnp.where(hit1, cum, 0.0), axis=0, keepdims=True)
    rank2 = jnp.sum(jnp.where(hit2, cum, 0.0), axis=0, keepdims=True)
    zero = jnp.zeros((1, tt), F32)
    info_ref[...] = jnp.concatenate([i1.astype(F32), i2.astype(F32), rank1, rank2, g1, g2, zero, zero], axis=0)
    tile_cnt_ref[0] = carry
    carry = carry + _dot(onehot.astype(BF16), jnp.ones((tt, LANES), BF16))
    carry_ref[...] = carry
    total_ref[...] = carry


def moe_router(h, norm_gain, w_router):
    m, d = h.shape
    tt = min(MOE_TILE, m)
    nt = m // tt
    return pl.pallas_call(
        functools.partial(_router_kernel, tt=tt),
        grid=(nt,),
        in_specs=[pl.BlockSpec((tt, d), lambda i: (i, 0)), pl.BlockSpec((1, d), lambda i: (0, 0)),
                  pl.BlockSpec((N_EXPERTS, d), lambda i: (0, 0))],
        out_specs=[pl.BlockSpec((SUBLANES, tt), lambda i: (0, i)),
                   pl.BlockSpec((1, N_EXPERTS, LANES), lambda i: (i, 0, 0)),
                   pl.BlockSpec((N_EXPERTS, LANES), lambda i: (0, 0))],
        out_shape=[jax.ShapeDtypeStruct((SUBLANES, m), F32), jax.ShapeDtypeStruct((nt, N_EXPERTS, LANES), F32),
                   jax.ShapeDtypeStruct((N_EXPERTS, LANES), F32)],
        scratch_shapes=[pltpu.VMEM((N_EXPERTS, LANES), F32)],
        compiler_params=_params("arbitrary"),
        name="moe_router",
    )(h, norm_gain.reshape(1, d), w_router.T)


def _slots_kernel(start_ref, info_ref, o_ref):
    info = info_ref[...]
    e1, e2, r1, r2 = info[0:1], info[1:2], info[2:3], info[3:4]
    s1 = r1
    s2 = r2
    for e in range(N_EXPERTS):
        st = start_ref[e].astype(F32)
        s1 = s1 + jnp.where(e1 == float(e), st, 0.0)
        s2 = s2 + jnp.where(e2 == float(e), st, 0.0)
    o_ref[...] = jnp.concatenate([s1, s2, info[4:5], info[5:6], info[6:8], info[6:8]], axis=0)


def moe_slots(info, slot_start):
    m = info.shape[1]
    tt = min(2048, m)
    return pl.pallas_call(
        _slots_kernel,
        grid_spec=pltpu.PrefetchScalarGridSpec(
            num_scalar_prefetch=1, grid=(m // tt,),
            in_specs=[pl.BlockSpec((SUBLANES, tt), lambda i, st: (0, i))],
            out_specs=pl.BlockSpec((SUBLANES, tt), lambda i, st: (0, i))),
        out_shape=jax.ShapeDtypeStruct((SUBLANES, m), F32),
        compiler_params=_params("parallel"),
        name="moe_slots",
    )(slot_start, info)


def _dispatch_kernel(blk_ref, til_ref, flg_ref, slots_ref, x_ref, o_ref, go_ref, *, bs):
    w = pl.program_id(0)
    flags = flg_ref[w]
    first = (flags & 1) != 0
    real = (flags & 4) != 0

    def gathered():
        sl = slots_ref[...]
        sid = (blk_ref[w] * bs + lax.broadcasted_iota(I32, (bs, sl.shape[1]), 0)).astype(F32)
        hit1 = sid == sl[0:1, :]
        hit2 = sid == sl[1:2, :]
        onehot = jnp.where(hit1 | hit2, 1.0, 0.0).astype(BF16)
        gate = jnp.sum(jnp.where(hit1, sl[2:3, :], 0.0) + jnp.where(hit2, sl[3:4, :], 0.0), axis=1, keepdims=True)
        return _dot(onehot, x_ref[...]).astype(o_ref.dtype), jnp.broadcast_to(gate, go_ref.shape)

    @pl.when(first & real)
    def _():
        o_ref[...], go_ref[...] = gathered()

    @pl.when(first & jnp.logical_not(real))
    def _():
        o_ref[...] = jnp.zeros_like(o_ref)
        go_ref[...] = jnp.zeros_like(go_ref)

    @pl.when(jnp.logical_not(first) & real)
    def _():
        rows, gate = gathered()
        o_ref[...] += rows
        go_ref[...] += gate


def moe_dispatch(xn, slots, wl_blk, wl_til, wl_flg, n_slots):
    m, d = xn.shape
    bs = tt = MOE_TILE
    n_items = wl_blk.shape[0]
    return pl.pallas_call(
        functools.partial(_dispatch_kernel, bs=bs),
        grid_spec=pltpu.PrefetchScalarGridSpec(
            num_scalar_prefetch=3, grid=(n_items,),
            in_specs=[pl.BlockSpec((SUBLANES, tt), lambda w, b, t, f: (0, t[w])),
                      pl.BlockSpec((tt, d), lambda w, b, t, f: (t[w], 0))],
            out_specs=[pl.BlockSpec((bs, d), lambda w, b, t, f: (b[w], 0)),
                       pl.BlockSpec((bs, LANES), lambda w, b, t, f: (b[w], 0))]),
        out_shape=[jax.ShapeDtypeStruct((n_slots, d), BF16), jax.ShapeDtypeStruct((n_slots, LANES), F32)],
        compiler_params=_params("arbitrary"),
        name="moe_dispatch",
    )(wl_blk, wl_til, wl_flg, slots, xn)


def _moe_ffn_kernel(be_ref, rows_ref, x_ref, gate_ref, w1_ref, w3_ref, w2_ref, o_ref, acc_ref, *, sub):
    b = pl.program_id(0)
    f = pl.program_id(1)
    bm = x_ref.shape[0]

    @pl.when(f == 0)
    def _():
        acc_ref[...] = jnp.zeros_like(acc_ref)

    for rows in range(sub, bm + 1, sub):
        @pl.when(rows_ref[b] == rows)
        def _(rows=rows):
            x = x_ref[0:rows, :]
            h = _silu(_dot(x, w1_ref[0].astype(BF16))) * _dot(x, w3_ref[0].astype(BF16))
            acc_ref[0:rows, :] += _dot(h.astype(BF16), w2_ref[0].astype(BF16))

    @pl.when(f == pl.num_programs(1) - 1)
    def _():
        o_ref[...] = (acc_ref[...] * gate_ref[:, 0:1]).astype(o_ref.dtype)


def moe_ffn(xs, slot_gate, w1, w3, w2, block_expert, block_rows):
    n_slots, d = xs.shape
    ff = w1.shape[2]
    bm = MOE_FFN_ROWS
    tf = MOE_TF
    nf = ff // tf

    def f_of(b, f, rows):
        return jnp.where(rows[b] > 0, f, nf - 1)

    return pl.pallas_call(
        functools.partial(_moe_ffn_kernel, sub=MOE_ROW_GROUP),
        grid_spec=pltpu.PrefetchScalarGridSpec(
            num_scalar_prefetch=2, grid=(n_slots // bm, nf),
            in_specs=[pl.BlockSpec((bm, d), lambda b, f, be, nb: (b, 0)),
                      pl.BlockSpec((bm, LANES), lambda b, f, be, nb: (b, 0)),
                      pl.BlockSpec((1, d, tf), lambda b, f, be, nb: (be[b], 0, f_of(b, f, nb))),
                      pl.BlockSpec((1, d, tf), lambda b, f, be, nb: (be[b], 0, f_of(b, f, nb))),
                      pl.BlockSpec((1, tf, d), lambda b, f, be, nb: (be[b], f_of(b, f, nb), 0))],
            out_specs=pl.BlockSpec((bm, d), lambda b, f, be, nb: (b, 0)),
            scratch_shapes=[pltpu.VMEM((bm, d), F32)]),
        out_shape=jax.ShapeDtypeStruct((n_slots, d), BF16),
        compiler_params=_params("arbitrary", "arbitrary"),
        name="moe_ffn",
    )(block_expert, block_rows, xs, slot_gate, w1, w3, w2)


def _combine_kernel(til_ref, blk_ref, flg_ref, tok_ref, y_ref, res_ref, o_ref, *, bs):
    w = pl.program_id(0)
    flags = flg_ref[w]

    @pl.when((flags & 1) != 0)
    def _():
        o_ref[...] = res_ref[...]

    @pl.when((flags & 4) != 0)
    def _():
        tok = tok_ref[...]
        tt = tok.shape[0]
        sid = (blk_ref[w] * bs + lax.broadcasted_iota(I32, (tt, bs), 1)).astype(F32)
        onehot = jnp.where((sid == tok[:, 0:1]) | (sid == tok[:, 1:2]), 1.0, 0.0).astype(BF16)
        o_ref[...] += _dot(onehot, y_ref[...])


def moe_combine(y, tok_info, residual, wl_til, wl_blk, wl_flg):
    m, d = residual.shape
    bs = tt = MOE_TILE
    n_items = wl_til.shape[0]
    return pl.pallas_call(
        functools.partial(_combine_kernel, bs=bs),
        grid_spec=pltpu.PrefetchScalarGridSpec(
            num_scalar_prefetch=3, grid=(n_items,),
            in_specs=[pl.BlockSpec((tt, SUBLANES), lambda w, t, b, f: (t[w], 0)),
                      pl.BlockSpec((bs, d), lambda w, t, b, f: (b[w], 0)),
                      pl.BlockSpec((tt, d), lambda w, t, b, f: (t[w], 0))],
            out_specs=pl.BlockSpec((tt, d), lambda w, t, b, f: (t[w], 0))),
        out_shape=jax.ShapeDtypeStruct((m, d), F32),
        compiler_params=_params("arbitrary"),
        name="moe_combine",
    )(wl_til, wl_blk, wl_flg, tok_info, y, residual)


def _work_list(overlap, n_items):
    n_major, n_minor = overlap.shape
    has = jnp.any(overlap, axis=1)
    marked = jnp.concatenate([overlap[:, :1] | ~has[:, None], overlap[:, 1:]], axis=1)
    maj, mino = jnp.nonzero(marked, size=n_items, fill_value=-1)
    valid = maj >= 0
    real = valid & overlap[jnp.maximum(maj, 0), jnp.maximum(mino, 0)]
    prev = jnp.concatenate([jnp.full((1,), -2, maj.dtype), maj[:-1]])
    nxt = jnp.concatenate([maj[1:], jnp.full((1,), -2, maj.dtype)])
    flags = (valid & (maj != prev)) * 1 + (valid & (maj != nxt)) * 2 + real * 4
    last_maj = jnp.max(jnp.where(valid, maj, 0))
    last_min = jnp.sum(jnp.where(valid & (maj == last_maj) & (maj != nxt), mino, 0))
    maj = jnp.where(valid, maj, last_maj)
    mino = jnp.where(valid, mino, last_min)
    return maj.astype(I32), mino.astype(I32), flags.astype(I32)


def moe_layer(h, norm_gain, w_router, w1, w3, w2):
    m, d = h.shape
    E = N_EXPERTS
    tile = min(MOE_TILE, m)
    n_tiles = m // tile
    ffn_rows = MOE_FFN_ROWS
    xn = rmsnorm(h, norm_gain, BF16)
    info, tile_cnt, total = moe_router(h, norm_gain, w_router)
    before = tile_cnt[:, :, 0]
    counts = total[:, 0].astype(I32)
    padded = (counts + ffn_rows - 1) // ffn_rows * ffn_rows
    ends = jnp.cumsum(padded)
    starts = ends - padded
    n_slots = (m * TOP_K // ffn_rows + E) * ffn_rows
    n_ffn_blocks = n_slots // ffn_rows
    n_dblocks = n_slots // tile
    slots = moe_slots(info, starts.astype(I32))
    db_start = jnp.arange(n_dblocks, dtype=I32) * tile
    db_e = jnp.minimum(jnp.sum(db_start[:, None] >= ends[None, :], axis=1), E - 1)
    r0 = db_start - starts[db_e]
    r1 = jnp.minimum(r0 + tile, counts[db_e])
    after = jnp.concatenate([before[1:], counts[None, :].astype(F32)], axis=0).astype(I32)
    bef = before.astype(I32)
    ov = (bef[:, db_e].T < r1[:, None]) & (after[:, db_e].T > r0[:, None]) & (db_start < ends[-1])[:, None]
    n_items = n_dblocks + E * n_tiles
    d_blk, d_til, d_flg = _work_list(ov, n_items)
    c_til, c_blk, c_flg = _work_list(ov.T, n_items)
    xs, slot_gate = moe_dispatch(xn, slots, d_blk, d_til, d_flg, n_slots)
    fb_start = jnp.arange(n_ffn_blocks, dtype=I32) * ffn_rows
    block_expert = jnp.minimum(jnp.sum(fb_start[:, None] >= ends[None, :], axis=1), E - 1).astype(I32)
    filled = counts[block_expert] - (fb_start - starts[block_expert])
    grp = MOE_ROW_GROUP
    block_rows = jnp.where(fb_start < ends[-1], (jnp.clip(filled, 0, ffn_rows) + grp - 1) // grp * grp, 0).astype(I32)
    ys = moe_ffn(xs, slot_gate, w1, w3, w2, block_expert, block_rows)
    return moe_combine(ys, slots.T, h, c_til, c_blk, c_flg)


def even_mixer_layer(h, posf, p, bsz, seq):
    m = bsz * seq
    hn = rmsnorm(h, p["norm_mix"], BF16)
    w_in = p["w_in"]
    c = np.cumsum([0, D_SSM, D_CONV, SSM_HEADS, D_NSA] + [D_NSA_KV] * 6 + [3 * NSA_HEADS])
    w_proj = jnp.concatenate([w_in[:, c[1]:c[2]], w_in[:, c[4]:c[10]], w_in[:, c[0]:c[1]], w_in[:, c[3]:c[4]]], axis=1)
    proj = matmul([(hn, w_proj.astype(BF16))], out_dtype=BF16)
    kv0 = D_CONV
    w_small = jnp.concatenate([w_in[:, c[2]:c[3]], w_in[:, c[10]:c[11]]], axis=1)
    w_small = jnp.pad(w_small, ((0, 0), (0, LANES - w_small.shape[1]))).astype(BF16)
    small = matmul([(hn, w_small)])
    y_ssm = ssd_group(proj, (D_CONV + 6 * D_NSA_KV) // D_SSM, 0, small, p["conv_w"], p["conv_b"], p["dt_bias"],
                      p["a_log"], p["d_skip"], p["ssm_norm"], bsz, seq)
    qn, ksn, kwn = rope_norm(posf, proj, (D_CONV + 6 * D_NSA_KV + D_SSM) // D_NSA, kv0 // (6 * D_NSA_KV),
                             p["q_gain"], p["ks_gain"], p["kw_gain"])
    kv6 = proj[:, kv0:kv0 + 6 * D_NSA_KV]
    nch = seq // CMP_STRIDE

    def chunks(col0):
        u = kv6[:, col0:col0 + D_NSA_KV].reshape(bsz, nch, CMP_STRIDE, NSA_KV_HEADS, NSA_HEAD_DIM)
        return u.transpose(0, 3, 1, 2, 4).reshape(bsz, NSA_KV_HEADS, nch, CMP_STRIDE * NSA_HEAD_DIM)

    kc, vc = compress(chunks(0), chunks(D_NSA_KV), p["pe_k"], p["pe_v"], p["wk1"], p["wk2"], p["wv1"], p["wv2"],
                      p["kc_gain"])
    o_cmp, selt = cmp_select(qn, kc, vc, bsz, seq)
    def v_tiles(col0, tk):
        tk = min(tk, seq)
        u = kv6[:, col0:col0 + D_NSA_KV].reshape(bsz, seq // tk, tk, NSA_KV_HEADS, NSA_HEAD_DIM)
        return u.transpose(0, 3, 1, 4, 2)

    o_sel = masked_attention(qn, ksn, 0, v_tiles(3 * D_NSA_KV, ATT_TK), bsz, seq, "sel", selt)
    o_win = masked_attention(qn, kwn, 0, v_tiles(5 * D_NSA_KV, WIN_TK), bsz, seq, "win")
    y_nsa = nsa_combine(small, o_cmp, o_sel, o_win)
    w_out = p["w_out"].astype(BF16)
    return matmul([(y_ssm, w_out[:D_SSM]), (y_nsa, w_out[D_SSM:])], residual=h)


def rwkv_layer(h, p, bsz, seq):
    m, d = h.shape
    mixes = rwkv_mix(h, p["norm_mix"], p["mu"], seq)
    bf = lambda w: w.astype(BF16)

    def lora_pad(w_a, w_b):
        r = w_a.shape[1]
        rp = -(-r // LANES) * LANES
        return bf(jnp.pad(w_a, ((0, 0), (0, rp - r)))), bf(jnp.pad(w_b, ((0, rp - r), (0, 0))))

    so = _to_scan_order
    r = matmul([(mixes[0], bf(so(p["w_r"], 1)))], out_dtype=BF16)
    k = matmul([(mixes[2], bf(so(p["w_k"], 1)))], out_dtype=BF16)
    v = matmul([(mixes[3], bf(so(p["w_v"], 1)))], out_dtype=BF16)
    w1, w2 = lora_pad(p["w1"], so(p["w2"], 1))
    w_raw = lora(mixes[1], w1, w2, so(p["w0"], 0), "tanh", None, F32)
    a1, a2 = lora_pad(p["a1"], so(p["a2"], 1))
    a_sig = lora(mixes[4], a1, a2, so(p["a0"], 0), None, "sigmoid", BF16)
    g1, g2 = lora_pad(p["g1"], so(p["g2"], 1))
    g = lora(mixes[5], g1, g2, jnp.zeros((d,), F32), "sigmoid", None, BF16)
    o = wkv_fused(r, k, v, w_raw, a_sig, g, p["k_k"], p["k_a"], p["r_k"].reshape(-1), p["ln_w"], p["ln_b"], bsz, seq)
    return matmul([(o, bf(so(p["w_o"], 0)))], residual=h)


def kernel(x, mem, positions, norm_mix, norm_xattn, norm_mem, norm_ffn, xattn_wq, xattn_wkv, xattn_wo, xattn_q_gain, xattn_k_gain, ev_w_in, ev_conv_w, ev_conv_b, ev_dt_bias, ev_a_log, ev_d_skip, ev_ssm_norm, ev_q_gain, ev_kc_gain, ev_ks_gain, ev_kw_gain, ev_pe_k, ev_pe_v, ev_cmp_wk1, ev_cmp_wk2, ev_cmp_wv1, ev_cmp_wv2, ev_w_out, ev_ffn_w1, ev_ffn_w3, ev_ffn_w2, od_mu, od_w_r, od_w_k, od_w_v, od_w_o, od_w0, od_w1, od_w2, od_a0, od_a1, od_a2, od_g1, od_g2, od_k_k, od_k_a, od_r_k, od_ln_w, od_ln_b, od_router, od_moe_w1, od_moe_w3, od_moe_w2):
    bsz, seq, d = x.shape
    m = bsz * seq
    depth = norm_mix.shape[0]
    h = x.reshape(m, d)
    mem2d = mem.reshape(-1, d)
    posf = positions.astype(F32).reshape(m, 1)
    for layer in range(depth):
        i = layer // 2
        if layer % 2 == 0:
            p = dict(norm_mix=norm_mix[layer], w_in=ev_w_in[i], conv_w=ev_conv_w[i], conv_b=ev_conv_b[i],
                     dt_bias=ev_dt_bias[i], a_log=ev_a_log[i], d_skip=ev_d_skip[i], ssm_norm=ev_ssm_norm[i],
                     q_gain=ev_q_gain[i], kc_gain=ev_kc_gain[i], ks_gain=ev_ks_gain[i], kw_gain=ev_kw_gain[i],
                     pe_k=ev_pe_k[i], pe_v=ev_pe_v[i], wk1=ev_cmp_wk1[i], wk2=ev_cmp_wk2[i], wv1=ev_cmp_wv1[i],
                     wv2=ev_cmp_wv2[i], w_out=ev_w_out[i])
            h = even_mixer_layer(h, posf, p, bsz, seq)
        else:
            p = dict(norm_mix=norm_mix[layer], mu=od_mu[i], w_r=od_w_r[i], w_k=od_w_k[i], w_v=od_w_v[i], w_o=od_w_o[i],
                     w0=od_w0[i], w1=od_w1[i], w2=od_w2[i], a0=od_a0[i], a1=od_a1[i], a2=od_a2[i], g1=od_g1[i],
                     g2=od_g2[i], k_k=od_k_k[i], k_a=od_k_a[i], r_k=od_r_k[i], ln_w=od_ln_w[i], ln_b=od_ln_b[i])
            h = rwkv_layer(h, p, bsz, seq)
        h = memory_xattn(h, mem2d, norm_xattn[layer], norm_mem[layer], xattn_wq[layer], xattn_wkv[layer],
                         xattn_wo[layer], xattn_q_gain[layer], xattn_k_gain[layer], bsz, seq)
        if layer % 2 == 0:
            h = swiglu_ffn(h, norm_ffn[layer], ev_ffn_w1[i].astype(BF16), ev_ffn_w3[i].astype(BF16),
                           ev_ffn_w2[i].astype(BF16))
        else:
            h = moe_layer(h, norm_ffn[layer], od_router[i], od_moe_w1[i], od_moe_w3[i], od_moe_w2[i])
    return h.reshape(bsz, seq, d)
```

```python
import functools
import math

import numpy as np
import jax
import jax.numpy as jnp
from jax import lax
from jax.experimental import pallas as pl
from jax.experimental.pallas import tpu as pltpu

F32 = jnp.float32
BF16 = jnp.bfloat16
I32 = jnp.int32

D_MODEL = 2048
NORM_EPS = 1e-6
NEG_BIG = -1e30

SSM_HEADS = 32
SSM_HEAD_DIM = 64
D_SSM = SSM_HEADS * SSM_HEAD_DIM
SSM_GROUPS = 4
D_STATE = 128
CONV_WIDTH = 4
SSD_CHUNK = 128
D_CONV = D_SSM + 2 * SSM_GROUPS * D_STATE
D_SSM_GROUP = D_SSM // SSM_GROUPS
SSD_CONV_TAIL = 16
SSD_EXT_ROWS = 256

NSA_HEADS = 16
NSA_KV_HEADS = 4
NSA_HEAD_DIM = 128
NSA_Q_PER_KV = NSA_HEADS // NSA_KV_HEADS
D_NSA = NSA_HEADS * NSA_HEAD_DIM
D_NSA_KV = NSA_KV_HEADS * NSA_HEAD_DIM
CMP_BLOCK = 32
CMP_STRIDE = 16
CMP_HIDDEN = 256
SEL_BLOCK = 64
N_SELECT = 16
WINDOW = 512
ROPE_DIM = NSA_HEAD_DIM // 4
ROPE_THETA = 500000.0

RWKV_HEAD_DIM = 64
RWKV_HEADS = D_MODEL // RWKV_HEAD_DIM
RWKV_GN_EPS = 1e-5 * RWKV_HEAD_DIM

XATTN_HEADS = 4
XATTN_HEAD_DIM = 128
D_XATTN = XATTN_HEADS * XATTN_HEAD_DIM

N_EXPERTS = 8
TOP_K = 2

LANES = 128
SUBLANES = 8
VMEM_LIMIT_BYTES = 56 * 1024 * 1024

ROW_TILE = 512
MM_TM = 2048
MM_TN = 512
MM_VMEM_BUDGET = 44 * 1024 * 1024
FFN_TM = 512
FFN_TF = 512
FFN_SPLIT = 256
MOE_FFN_ROWS = 1024
MOE_TF = 256
MOE_ROW_GROUP = 256
COMBINE_TM = 512
MOE_TILE = 512
ATT_TQ = 512
ATT_TK = 512
WIN_TK = 512
LOG2E = 1.4426950408889634
CMP_TQ = 512
WKV_T = 64
WKV_LOOKAHEAD = 8


def _params(*sem):
    return pltpu.CompilerParams(dimension_semantics=sem, vmem_limit_bytes=VMEM_LIMIT_BYTES)


def _sigmoid(x):
    return 1.0 / (1.0 + jnp.exp(-x))


def _silu(x):
    return x * _sigmoid(x)


def _softplus(x):
    return jnp.maximum(x, 0.0) + jnp.log(1.0 + jnp.exp(-jnp.abs(x)))


def _split3(x):
    h = x.astype(BF16)
    r = x - h.astype(F32)
    m = r.astype(BF16)
    l = (r - m.astype(F32)).astype(BF16)
    return h, m, l


def _dot(a, b):
    return jnp.dot(a, b, preferred_element_type=F32)


def _dot_nt(a, b):
    return lax.dot_general(a, b, (((1,), (1,)), ((), ())), preferred_element_type=F32)


def _dot_f32_sel(x, e):
    h, m, l = _split3(x)
    return _dot(h, e) + _dot(m, e) + _dot(l, e)


def _sel_dot_f32(e, x):
    h, m, l = _split3(x)
    return _dot(e, h) + _dot(e, m) + _dot(e, l)


def _rms(x, eps=NORM_EPS):
    return x * lax.rsqrt(jnp.mean(x * x, axis=-1, keepdims=True) + eps)


def _rmsnorm_kernel(x_ref, g_ref, o_ref):
    o_ref[...] = (_rms(x_ref[...]) * g_ref[...]).astype(o_ref.dtype)


def rmsnorm(x, gain, out_dtype):
    m, d = x.shape
    tm = min(ROW_TILE, m)
    return pl.pallas_call(
        _rmsnorm_kernel,
        grid=(m // tm,),
        in_specs=[pl.BlockSpec((tm, d), lambda i: (i, 0)), pl.BlockSpec((1, d), lambda i: (0, 0))],
        out_specs=pl.BlockSpec((tm, d), lambda i: (i, 0)),
        out_shape=jax.ShapeDtypeStruct((m, d), out_dtype),
        compiler_params=_params("parallel"),
        name="rmsnorm",
    )(x, gain.reshape(1, d))


def _mm_kernel(*refs, n_pairs, has_res):
    o_ref = refs[-1]
    acc = None
    for p in range(n_pairs):
        d = _dot(refs[2 * p][...], refs[2 * p + 1][...])
        acc = d if acc is None else acc + d
    if has_res:
        acc = acc + refs[2 * n_pairs][...]
    o_ref[...] = acc.astype(o_ref.dtype)


def matmul(pairs, residual=None, out_dtype=F32):
    m = pairs[0][0].shape[0]
    n = pairs[0][1].shape[1]
    k_total = sum(a.shape[1] for a, _ in pairs)
    tm = min(MM_TM if k_total <= D_MODEL else MM_TM // 2, m)
    out_bytes = jnp.dtype(out_dtype).itemsize + (4 if residual is not None else 0)

    def vmem_bytes(tn):
        return 2 * (tm * k_total * 2 + k_total * tn * 2 + tm * tn * out_bytes)

    tn = next((t for t in (2 * MM_TN, MM_TN, 256, LANES) if n % t == 0 and vmem_bytes(t) <= MM_VMEM_BUDGET), LANES)
    in_specs, args = [], []
    for a, w in pairs:
        kk = a.shape[1]
        in_specs += [pl.BlockSpec((tm, kk), lambda i, j: (i, 0)), pl.BlockSpec((kk, tn), lambda i, j: (0, j))]
        args += [a, w]
    if residual is not None:
        in_specs.append(pl.BlockSpec((tm, tn), lambda i, j: (i, j)))
        args.append(residual)
    kern = functools.partial(_mm_kernel, n_pairs=len(pairs), has_res=residual is not None)
    return pl.pallas_call(
        kern,
        grid=(m // tm, n // tn),
        in_specs=in_specs,
        out_specs=pl.BlockSpec((tm, tn), lambda i, j: (i, j)),
        out_shape=jax.ShapeDtypeStruct((m, n), out_dtype),
        compiler_params=_params("parallel", "parallel"),
        name="matmul",
    )(*args)


def _act(x, act):
    if act == "tanh":
        return jnp.tanh(x)
    if act == "sigmoid":
        return _sigmoid(x)
    return x


def _lora_kernel(a_ref, wa_ref, wb_ref, bias_ref, o_ref, *, act_mid, act_out):
    mid = _act(_dot(a_ref[...], wa_ref[...]), act_mid).astype(BF16)
    o_ref[...] = _act(_dot(mid, wb_ref[...]) + bias_ref[...], act_out).astype(o_ref.dtype)


def lora(a, w_a, w_b, bias, act_mid, act_out, out_dtype):
    m, d = a.shape
    r = w_a.shape[1]
    n = w_b.shape[1]
    tm = min(2 * ROW_TILE, m)
    return pl.pallas_call(
        functools.partial(_lora_kernel, act_mid=act_mid, act_out=act_out),
        grid=(m // tm,),
        in_specs=[pl.BlockSpec((tm, d), lambda i: (i, 0)), pl.BlockSpec((d, r), lambda i: (0, 0)),
                  pl.BlockSpec((r, n), lambda i: (0, 0)), pl.BlockSpec((1, n), lambda i: (0, 0))],
        out_specs=pl.BlockSpec((tm, n), lambda i: (i, 0)),
        out_shape=jax.ShapeDtypeStruct((m, n), out_dtype),
        compiler_params=_params("parallel"),
        name="lora",
    )(a, w_a, w_b, bias.reshape(1, n).astype(F32))


def _ffn_kernel(g_ref, w1_ref, w3_ref, w2_ref, res_ref, o_ref, x_ref):
    f = pl.program_id(1)

    @pl.when(f == 0)
    def _():
        res = res_ref[...]
        o_ref[...] = res
        x_ref[...] = (_rms(res) * g_ref[...]).astype(x_ref.dtype)

    x = x_ref[...]
    tf = w1_ref.shape[1]
    cols = [slice(c, c + FFN_SPLIT) for c in range(0, tf, FFN_SPLIT)]
    gate_up = [(_dot(x, w1_ref[:, c]), _dot(x, w3_ref[:, c])) for c in cols]
    hs = [(_silu(g) * u).astype(BF16) for g, u in gate_up]
    acc = _dot(hs[0], w2_ref[cols[0], :])
    for h, c in zip(hs[1:], cols[1:]):
        acc = acc + _dot(h, w2_ref[c, :])
    o_ref[...] += acc


def swiglu_ffn(residual, norm_gain, w1, w3, w2):
    m, d = residual.shape
    ff = w1.shape[1]
    tm = min(FFN_TM, m)
    tf = FFN_TF
    return pl.pallas_call(
        _ffn_kernel,
        grid=(m // tm, ff // tf),
        in_specs=[pl.BlockSpec((1, d), lambda i, f: (0, 0)),
                  pl.BlockSpec((d, tf), lambda i, f: (0, f)),
                  pl.BlockSpec((d, tf), lambda i, f: (0, f)),
                  pl.BlockSpec((tf, d), lambda i, f: (f, 0)),
                  pl.BlockSpec((tm, d), lambda i, f: (i, 0))],
        out_specs=pl.BlockSpec((tm, d), lambda i, f: (i, 0)),
        out_shape=jax.ShapeDtypeStruct((m, d), F32),
        scratch_shapes=[pltpu.VMEM((tm, d), BF16)],
        compiler_params=_params("parallel", "arbitrary"),
        name="swiglu_ffn",
    )(norm_gain.reshape(1, d), w1, w3, w2, residual)


def _ssd_kernel(z_ref, xbc_ref, dt_ref, shift_ref, cw_ref, cb_ref, dtb_ref, alog_ref, dskip_ref, nw_ref, eh_ref,
                o_ref, ext_ref, st_ref):
    L = SSD_CHUNK
    G = SSM_GROUPS
    GW = D_SSM_GROUP
    T = SSD_CONV_TAIL
    c = pl.program_id(1)

    @pl.when(c == 0)
    def _():
        ext_ref[...] = jnp.zeros_like(ext_ref)
        st_ref[...] = jnp.zeros_like(st_ref)

    ext_ref[0:T, :] = ext_ref[L:L + T, :]
    ext_ref[T:L + T, :] = xbc_ref[...]
    shifted = _dot(shift_ref[...], ext_ref[...])
    acc = jnp.zeros((L, D_CONV), F32) + cb_ref[...]
    for k in range(CONV_WIDTH):
        acc = acc + cw_ref[k:k + 1, :] * shifted[k * L:(k + 1) * L, :]
    xc = _silu(acc)

    dt = _softplus(dt_ref[...] + dtb_ref[...])
    a = -jnp.exp(alog_ref[...])
    adt = dt * a
    row = lax.broadcasted_iota(I32, (L, L), 0)
    col = lax.broadcasted_iota(I32, (L, L), 1)
    causal = row >= col
    tril = jnp.where(causal, 1.0, 0.0).astype(BF16)
    a_cum = _sel_dot_f32(tril, adt)
    a_cum_t = a_cum.T
    eh = eh_ref[...]
    acx = _dot_f32_sel(a_cum, eh)
    dtx = _dot_f32_sel(dt, eh)
    a_end = acx[L - 1:L, :]
    xs = xc[:, :D_SSM]
    xdt = xs * dtx
    xdte = (xdt * jnp.exp(a_end - acx)).astype(BF16)
    from_start = jnp.exp(acx)
    lane = lax.broadcasted_iota(I32, (L, LANES), 1)
    lo_half = lane < SSM_HEAD_DIM
    z = z_ref[...].astype(F32)

    for g in range(G):
        bm = xc[:, D_SSM + g * D_STATE:D_SSM + (g + 1) * D_STATE]
        cm = xc[:, D_SSM + G * D_STATE + g * D_STATE:D_SSM + G * D_STATE + (g + 1) * D_STATE]
        bm_b = bm.astype(BF16)
        cm_b = cm.astype(BF16)
        cb = _dot_nt(cm_b, bm_b)
        st = st_ref[g]
        y_off = _dot(cm_b, st.astype(BF16)) * from_start[:, g * GW:(g + 1) * GW]
        pieces = []
        for pp in range(GW // LANES):
            h0 = g * (GW // SSM_HEAD_DIM) + 2 * pp
            ms = []
            for hh in (h0, h0 + 1):
                diff = a_cum[:, hh:hh + 1] - a_cum_t[hh:hh + 1, :]
                dec = jnp.exp(jnp.where(causal, diff, NEG_BIG))
                ms.append((cb * dec).astype(BF16))
            lhs = jnp.concatenate(ms, axis=1)
            c0 = g * GW + pp * LANES
            xp = xdt[:, c0:c0 + LANES]
            rhs = jnp.concatenate([jnp.where(lo_half, xp, 0.0), jnp.where(lo_half, 0.0, xp)], axis=0)
            pieces.append(_dot(lhs, rhs.astype(BF16)))
        y = jnp.concatenate(pieces, axis=1) + y_off
        y = y + xs[:, g * GW:(g + 1) * GW] * dskip_ref[:, g * GW:(g + 1) * GW]
        y = y * _silu(z[:, g * GW:(g + 1) * GW])
        y = _rms(y) * nw_ref[:, g * GW:(g + 1) * GW]
        o_ref[:, g * GW:(g + 1) * GW] = y.astype(o_ref.dtype)
        s_new = _dot(bm.T.astype(BF16), xdte[:, g * GW:(g + 1) * GW])
        st_ref[g] = st * jnp.exp(a_end[:, g * GW:(g + 1) * GW]) + s_new


def ssd_group(proj, z_col, xbc_col, small, conv_w, conv_b, dt_bias, a_log, d_skip, norm_w, bsz, seq):
    m = bsz * seq
    L = SSD_CHUNK
    nc = seq // L
    pad = LANES - SSM_HEADS
    eh = np.zeros((LANES, D_SSM), np.float32)
    for h in range(SSM_HEADS):
        eh[h, h * SSM_HEAD_DIM:(h + 1) * SSM_HEAD_DIM] = 1.0
    shift = np.zeros((CONV_WIDTH * L, SSD_EXT_ROWS), np.float32)
    for k in range(CONV_WIDTH):
        shift[k * L + np.arange(L), np.arange(L) + SSD_CONV_TAIL - (CONV_WIDTH - 1) + k] = 1.0
    row_spec = lambda w, col=0: pl.BlockSpec((L, w), lambda b, c: (b * nc + c, col))
    const = lambda r, w: pl.BlockSpec((r, w), lambda b, c: (0, 0))
    return pl.pallas_call(
        _ssd_kernel,
        grid=(bsz, nc),
        in_specs=[row_spec(D_SSM, z_col), row_spec(D_CONV, xbc_col), row_spec(LANES),
                  const(CONV_WIDTH * L, SSD_EXT_ROWS),
                  const(CONV_WIDTH, D_CONV), const(1, D_CONV), const(1, LANES), const(1, LANES),
                  const(1, D_SSM), const(1, D_SSM), const(LANES, D_SSM)],
        out_specs=row_spec(D_SSM),
        out_shape=jax.ShapeDtypeStruct((m, D_SSM), BF16),
        scratch_shapes=[pltpu.VMEM((SSD_EXT_ROWS, D_CONV), BF16),
                        pltpu.VMEM((SSM_GROUPS, D_STATE, D_SSM_GROUP), F32)],
        compiler_params=_params("parallel", "arbitrary"),
        name="ssd_group",
    )(proj, proj, small, jnp.asarray(shift, BF16), conv_w.T, conv_b.reshape(1, D_CONV),
      jnp.pad(dt_bias, (0, pad)).reshape(1, LANES), jnp.pad(a_log, (0, pad)).reshape(1, LANES),
      jnp.repeat(d_skip, SSM_HEAD_DIM).reshape(1, D_SSM), norm_w.reshape(1, D_SSM), jnp.asarray(eh, BF16))


def _rope_norm_kernel(pos_ref, invf_ref, q_ref, kv_ref, qg_ref, ksg_ref, kwg_ref, qo_ref, kso_ref, kwo_ref):
    ang = pos_ref[...] * invf_ref[...]
    cos = jnp.cos(ang)
    sin = jnp.sin(ang)
    lane = lax.broadcasted_iota(I32, ang.shape, 1)
    first = lane < ROPE_DIM // 2
    sin_signed = jnp.where(first, -sin, sin)

    def head(x, gain):
        y = _rms(x.astype(F32)) * gain
        partner = jnp.where(first, pltpu.roll(y, LANES - ROPE_DIM // 2, axis=1), pltpu.roll(y, ROPE_DIM // 2, axis=1))
        return y * cos + partner * sin_signed

    hd = NSA_HEAD_DIM
    scale = hd ** -0.5 * LOG2E
    for h in range(NSA_HEADS):
        qo_ref[:, h * hd:(h + 1) * hd] = (head(q_ref[:, h * hd:(h + 1) * hd], qg_ref[...]) * scale).astype(qo_ref.dtype)
    for h in range(NSA_KV_HEADS):
        ks = kv_ref[:, 2 * D_NSA_KV + h * hd:2 * D_NSA_KV + (h + 1) * hd]
        kw = kv_ref[:, 4 * D_NSA_KV + h * hd:4 * D_NSA_KV + (h + 1) * hd]
        kso_ref[:, h * hd:(h + 1) * hd] = head(ks, ksg_ref[...]).astype(kso_ref.dtype)
        kwo_ref[:, h * hd:(h + 1) * hd] = head(kw, kwg_ref[...]).astype(kwo_ref.dtype)


def rope_norm(posf, proj, q_col, kv_col, q_gain, ks_gain, kw_gain):
    m = proj.shape[0]
    tm = min(ROW_TILE, m)
    half = ROPE_DIM // 2
    inv = np.exp(-math.log(ROPE_THETA) * np.arange(0, ROPE_DIM, 2, dtype=np.float32) / ROPE_DIM).astype(np.float32)
    invf = np.zeros((1, LANES), np.float32)
    invf[0, :half] = inv
    invf[0, half:ROPE_DIM] = inv
    row = lambda w, col=0: pl.BlockSpec((tm, w), lambda i: (i, col))
    const = pl.BlockSpec((1, LANES), lambda i: (0, 0))
    return pl.pallas_call(
        _rope_norm_kernel,
        grid=(m // tm,),
        in_specs=[row(1), const, row(D_NSA, q_col), row(6 * D_NSA_KV, kv_col), const, const, const],
        out_specs=[row(D_NSA), row(D_NSA_KV), row(D_NSA_KV)],
        out_shape=[jax.ShapeDtypeStruct((m, D_NSA), BF16), jax.ShapeDtypeStruct((m, D_NSA_KV), BF16),
                   jax.ShapeDtypeStruct((m, D_NSA_KV), BF16)],
        compiler_params=_params("parallel"),
        name="rope_norm",
    )(posf, jnp.asarray(invf), proj, proj, q_gain.reshape(1, LANES), ks_gain.reshape(1, LANES), kw_gain.reshape(1, LANES))


def _compress_kernel(uk_ref, uv_ref, pek_ref, pev_ref, wk1_ref, wk2_ref, wv1_ref, wv2_ref, g_ref, kc_ref, vc_ref):
    def mlp(u, pe_ref, w1_ref, w2_ref):
        u = u.astype(F32)
        n = u.shape[0]
        h1 = _dot((u + pe_ref[0:1, :]).astype(BF16), w1_ref[0])
        h2 = _dot((u + pe_ref[1:2, :]).astype(BF16), w1_ref[1])
        pre = h1 + pltpu.roll(h2, n - 1, axis=0)
        return _dot(_silu(pre).astype(BF16), w2_ref[...])

    kc = mlp(uk_ref[0, 0], pek_ref, wk1_ref, wk2_ref)
    kc_ref[0, 0] = _rms(kc) * g_ref[...]
    vc_ref[0, 0] = mlp(uv_ref[0, 0], pev_ref, wv1_ref, wv2_ref)


def compress(uk, uv, pe_k, pe_v, wk1, wk2, wv1, wv2, kc_gain):
    bsz, nkv, nch, width = uk.shape
    half = CMP_STRIDE * NSA_HEAD_DIM
    u_spec = pl.BlockSpec((1, 1, nch, width), lambda b, g: (b, g, 0, 0))
    o_spec = pl.BlockSpec((1, 1, nch, NSA_HEAD_DIM), lambda b, g: (b, g, 0, 0))
    c2 = lambda s: pl.BlockSpec(s, lambda b, g: (0, 0))
    c3 = lambda s: pl.BlockSpec(s, lambda b, g: (0, 0, 0))
    o_shape = jax.ShapeDtypeStruct((bsz, nkv, nch, NSA_HEAD_DIM), F32)
    return pl.pallas_call(
        _compress_kernel,
        grid=(bsz, nkv),
        in_specs=[u_spec, u_spec, c2((2, half)), c2((2, half)),
                  c3((2, half, CMP_HIDDEN)), c2((CMP_HIDDEN, NSA_HEAD_DIM)),
                  c3((2, half, CMP_HIDDEN)), c2((CMP_HIDDEN, NSA_HEAD_DIM)), c2((1, NSA_HEAD_DIM))],
        out_specs=[o_spec, o_spec],
        out_shape=[o_shape, o_shape],
        compiler_params=_params("parallel", "parallel"),
        name="nsa_compress",
    )(uk, uv, pe_k.reshape(2, half), pe_v.reshape(2, half),
      wk1.reshape(2, half, CMP_HIDDEN).astype(BF16), wk2.astype(BF16),
      wv1.reshape(2, half, CMP_HIDDEN).astype(BF16), wv2.astype(BF16), kc_gain.reshape(1, NSA_HEAD_DIM))


def _stack_heads(q):
    return jnp.concatenate([q[:, r * NSA_HEAD_DIM:(r + 1) * NSA_HEAD_DIM] for r in range(NSA_Q_PER_KV)], axis=0)


def _unstack_heads(o, tq):
    return jnp.concatenate([o[r * tq:(r + 1) * tq, :] for r in range(NSA_Q_PER_KV)], axis=1)


def _cmp_select_kernel(q_ref, kc_ref, vc_ref, ovt_ref, o_ref, sel_ref, *, tq, n_blk):
    R = NSA_Q_PER_KV
    q0 = pl.program_id(2) * tq
    q4 = _stack_heads(q_ref[...])
    kc = kc_ref[0, 0].astype(BF16)
    ncmp = kc.shape[0]
    s = _dot_nt(q4, kc)
    t_row = q0 + lax.broadcasted_iota(I32, (tq, ncmp), 0)
    c_col = lax.broadcasted_iota(I32, (tq, ncmp), 1)
    visf = jnp.where((c_col * CMP_STRIDE + CMP_BLOCK - 1) <= t_row, 1.0, 0.0)
    vis = jnp.concatenate([visf] * R, axis=0) > 0.5
    s = jnp.where(vis, s, NEG_BIG)
    mx = jnp.max(s, axis=-1, keepdims=True)
    e = jnp.where(vis, jnp.exp2(s - mx), 0.0)
    den = jnp.sum(e, axis=-1, keepdims=True)
    p = e / jnp.maximum(den, 1e-30)
    o = _dot(p.astype(BF16), vc_ref[0, 0].astype(BF16))
    o_ref[...] = _unstack_heads(o, tq).astype(o_ref.dtype)

    psum = p[0:tq]
    for r in range(1, R):
        psum = psum + p[r * tq:(r + 1) * tq]
    h, m, l = _split3(psum)
    ovt = ovt_ref[...]
    imp = _dot_nt(ovt, h) + _dot_nt(ovt, m) + _dot_nt(ovt, l)
    jdx = lax.broadcasted_iota(I32, (n_blk, tq), 0)
    t_lane = q0 + lax.broadcasted_iota(I32, (n_blk, tq), 1)
    cur = t_lane // SEL_BLOCK
    causal = jdx <= cur
    forced = ((jdx == 0) | (jdx >= cur - 1)) & causal
    score = jnp.where(forced, jnp.inf, jnp.where(causal, imp, -jnp.inf))
    rank = jnp.zeros((n_blk, tq), F32)
    for jp in range(n_blk):
        rowv = score[jp:jp + 1, :]
        beats = (rowv > score) | ((rowv == score) & (jdx > jp))
        rank = rank + jnp.where(beats, 1.0, 0.0)
    sel = jnp.where((rank < float(min(N_SELECT, n_blk))) & causal, 1.0, 0.0)
    if n_blk < LANES:
        sel = jnp.concatenate([sel, jnp.zeros((LANES - n_blk, tq), F32)], axis=0)
    sel_ref[0, 0] = sel.astype(sel_ref.dtype)


def cmp_select(qn, kc, vc, bsz, seq):
    m = bsz * seq
    G = NSA_KV_HEADS
    tq = CMP_TQ
    nq = seq // tq
    ncmp = kc.shape[2]
    n_blk = seq // SEL_BLOCK
    assert n_blk <= LANES and tq % LANES == 0
    c0 = np.arange(ncmp)[None, :] * CMP_STRIDE
    s0 = np.arange(n_blk)[:, None] * SEL_BLOCK
    ov = np.clip(np.minimum(c0 + CMP_BLOCK, s0 + SEL_BLOCK) - np.maximum(c0, s0), 0, None) / CMP_STRIDE
    n_cmp_valid = (seq - CMP_BLOCK) // CMP_STRIDE + 1
    ov[:, n_cmp_valid:] = 0.0
    kv_spec = pl.BlockSpec((1, 1, ncmp, NSA_HEAD_DIM), lambda b, g, i: (b, g, 0, 0))
    kern = functools.partial(_cmp_select_kernel, tq=tq, n_blk=n_blk)
    return pl.pallas_call(
        kern,
        grid=(bsz, G, nq),
        in_specs=[pl.BlockSpec((tq, NSA_Q_PER_KV * NSA_HEAD_DIM), lambda b, g, i: (b * nq + i, g)),
                  kv_spec, kv_spec, pl.BlockSpec((n_blk, ncmp), lambda b, g, i: (0, 0))],
        out_specs=[pl.BlockSpec((tq, NSA_Q_PER_KV * NSA_HEAD_DIM), lambda b, g, i: (b * nq + i, g)),
                   pl.BlockSpec((1, 1, LANES, tq), lambda b, g, i: (b, g, 0, i))],
        out_shape=[jax.ShapeDtypeStruct((m, D_NSA), BF16), jax.ShapeDtypeStruct((bsz, G, LANES, seq), BF16)],
        compiler_params=_params("parallel", "parallel", "parallel"),
        name="nsa_cmp_select",
    )(qn, kc, vc, jnp.asarray(ov, BF16))


def _masked_attn_kernel(q_ref, k_ref, vt_ref, *rest, mode, tq, tk):
    if mode == "sel":
        selt_ref, o_ref = rest
    else:
        (o_ref,) = rest
    R = NSA_Q_PER_KV
    hd = NSA_HEAD_DIM
    q0 = pl.program_id(2) * tq
    key_row = lax.broadcasted_iota(I32, (tk, tq), 0)
    t_lane = q0 + lax.broadcasted_iota(I32, (tk, tq), 1)
    hi = (q0 + tq - 1) // tk + 1
    q = q_ref[...]
    if mode == "sel":
        key_row_b = lax.broadcasted_iota(I32, (tk, LANES), 0)
        blk_lane = lax.broadcasted_iota(I32, (tk, LANES), 1)
        penalty = ((1.0 - selt_ref[0, 0].astype(F32)) * NEG_BIG).astype(BF16)
        rhs = [jnp.concatenate([q[:, r * hd:(r + 1) * hd].astype(F32).T.astype(BF16), penalty], axis=0)
               for r in range(R)]
        lo = 0
    else:
        lo = jnp.maximum(q0 - (WINDOW - 1), 0) // tk

    def body(kt, carry, diagonal=True):
        k0 = kt * tk
        k_t = k_ref[pl.ds(pl.multiple_of(k0, tk), tk), :]
        v_t = vt_ref[0, 0, kt]
        key = k0 + key_row
        if mode == "sel":
            expand = jnp.where((k0 + key_row_b) // SEL_BLOCK == blk_lane, 1.0, 0.0).astype(BF16)
            lhs = jnp.concatenate([k_t, expand], axis=1)
            ss = [_dot(lhs, rhs[r]) for r in range(R)]
            if diagonal:
                ss = [jnp.where(key <= t_lane, s, NEG_BIG) for s in ss]
        else:
            bias = jnp.where((key <= t_lane) & (key > t_lane - WINDOW), 0.0, NEG_BIG)
            ss = [_dot_nt(k_t, q[:, r * hd:(r + 1) * hd]) + bias for r in range(R)]
        stats = []
        for r in range(R):
            m_prev, l_prev = carry[3 * r], carry[3 * r + 1]
            m_new = jnp.maximum(m_prev, jnp.max(ss[r], axis=0, keepdims=True))
            p = jnp.exp2(ss[r] - m_new)
            alpha = jnp.exp2(m_prev - m_new)
            stats.append((m_new, alpha * l_prev + jnp.sum(p, axis=0, keepdims=True), alpha, p.astype(BF16)))
        new = []
        for r in range(R):
            m_new, l_new, alpha, p = stats[r]
            new += [m_new, l_new, alpha * carry[3 * r + 2] + _dot(v_t, p)]
        return tuple(new)

    init = (jnp.full((1, tq), NEG_BIG, F32), jnp.zeros((1, tq), F32), jnp.zeros((hd, tq), F32)) * R
    if mode == "sel":
        out = lax.fori_loop(lo, hi - 1, functools.partial(body, diagonal=False), init)
        out = body(hi - 1, out)
    else:
        out = lax.fori_loop(lo, hi, body, init)
    o = jnp.concatenate([(out[3 * r + 2] / out[3 * r + 1]).T for r in range(R)], axis=1)
    o_ref[...] = o.astype(o_ref.dtype)


def masked_attention(qn, k, k_col0, vt, bsz, seq, mode, selt=None):
    m = bsz * seq
    G = NSA_KV_HEADS
    tq = min(ATT_TQ, seq)
    tk = vt.shape[-1]
    assert tk % tq == 0
    nq = seq // tq
    hd = NSA_HEAD_DIM
    q_spec = pl.BlockSpec((tq, NSA_Q_PER_KV * hd), lambda b, g, i: (b * nq + i, g))
    in_specs = [q_spec,
                pl.BlockSpec((seq, hd), lambda b, g, i: (b, k_col0 + g)),
                pl.BlockSpec((1, 1, seq // tk, hd, tk), lambda b, g, i: (b, g, 0, 0, 0))]
    args = [qn, k, vt]
    if mode == "sel":
        in_specs.append(pl.BlockSpec((1, 1, LANES, tq), lambda b, g, i: (b, g, 0, i)))
        args.append(selt)
    return pl.pallas_call(
        functools.partial(_masked_attn_kernel, mode=mode, tq=tq, tk=tk),
        grid=(bsz, G, nq),
        in_specs=in_specs,
        out_specs=q_spec,
        out_shape=jax.ShapeDtypeStruct((m, D_NSA), BF16),
        compiler_params=_params("parallel", "parallel", "arbitrary"),
        name="nsa_attn_" + mode,
    )(*args)


def _nsa_combine_kernel(g_ref, e_ref, oc_ref, os_ref, ow_ref, o_ref):
    gate = _sigmoid(g_ref[...])
    gx = _dot_f32_sel(gate, e_ref[...])
    y = (gx[:, :D_NSA] * oc_ref[...].astype(F32) + gx[:, D_NSA:2 * D_NSA] * os_ref[...].astype(F32)
         + gx[:, 2 * D_NSA:] * ow_ref[...].astype(F32))
    o_ref[...] = y.astype(o_ref.dtype)


def nsa_combine(small, o_cmp, o_sel, o_win):
    m = small.shape[0]
    tm = min(COMBINE_TM, m)
    e = np.zeros((LANES, 3 * D_NSA), np.float32)
    for h in range(NSA_HEADS):
        for br in range(3):
            e[SSM_HEADS + 3 * h + br, br * D_NSA + h * NSA_HEAD_DIM:br * D_NSA + (h + 1) * NSA_HEAD_DIM] = 1.0
    row = lambda w: pl.BlockSpec((tm, w), lambda i: (i, 0))
    return pl.pallas_call(
        _nsa_combine_kernel,
        grid=(m // tm,),
        in_specs=[row(LANES), pl.BlockSpec((LANES, 3 * D_NSA), lambda i: (0, 0)), row(D_NSA), row(D_NSA), row(D_NSA)],
        out_specs=row(D_NSA),
        out_shape=jax.ShapeDtypeStruct((m, D_NSA), BF16),
        compiler_params=_params("parallel"),
        name="nsa_combine",
    )(small, jnp.asarray(e, BF16), o_cmp, o_sel, o_win)


def _xattn_kernel(q_ref, kv_ref, qg_ref, kg_ref, o_ref):
    hd = XATTN_HEAD_DIM
    scale = hd ** -0.5
    outs = []
    for h in range(XATTN_HEADS):
        q = (_rms(q_ref[:, h * hd:(h + 1) * hd]) * qg_ref[...]).astype(BF16)
        k = (_rms(kv_ref[:, h * hd:(h + 1) * hd]) * kg_ref[...]).astype(BF16)
        v = kv_ref[:, D_XATTN + h * hd:D_XATTN + (h + 1) * hd].astype(BF16)
        s = _dot_nt(q, k) * scale
        e = jnp.exp(s - jnp.max(s, axis=-1, keepdims=True))
        p = e / jnp.sum(e, axis=-1, keepdims=True)
        outs.append(_dot(p.astype(BF16), v))
    o_ref[...] = jnp.concatenate(outs, axis=1).astype(o_ref.dtype)


def xattn_core(q, kv, q_gain, k_gain, bsz, seq):
    m = bsz * seq
    n_mem = kv.shape[0] // bsz
    tm = min(ROW_TILE, seq)
    nt = seq // tm
    return pl.pallas_call(
        _xattn_kernel,
        grid=(bsz, nt),
        in_specs=[pl.BlockSpec((tm, D_XATTN), lambda b, i: (b * nt + i, 0)),
                  pl.BlockSpec((n_mem, 2 * D_XATTN), lambda b, i: (b, 0)),
                  pl.BlockSpec((1, XATTN_HEAD_DIM), lambda b, i: (0, 0)),
                  pl.BlockSpec((1, XATTN_HEAD_DIM), lambda b, i: (0, 0))],
        out_specs=pl.BlockSpec((tm, D_XATTN), lambda b, i: (b * nt + i, 0)),
        out_shape=jax.ShapeDtypeStruct((m, D_XATTN), BF16),
        compiler_params=_params("parallel", "parallel"),
        name="xattn_core",
    )(q, kv, q_gain.reshape(1, XATTN_HEAD_DIM), k_gain.reshape(1, XATTN_HEAD_DIM))


def _norm_mm_kernel(h_ref, g_ref, w_ref, o_ref):
    x = (_rms(h_ref[...]) * g_ref[...]).astype(BF16)
    o_ref[...] = _dot(x, w_ref[...]).astype(o_ref.dtype)


def norm_matmul(h, gain, w):
    m, d = h.shape
    n = w.shape[1]
    tm = min(ROW_TILE, m)
    return pl.pallas_call(
        _norm_mm_kernel,
        grid=(m // tm,),
        in_specs=[pl.BlockSpec((tm, d), lambda i: (i, 0)), pl.BlockSpec((1, d), lambda i: (0, 0)),
                  pl.BlockSpec((d, n), lambda i: (0, 0))],
        out_specs=pl.BlockSpec((tm, n), lambda i: (i, 0)),
        out_shape=jax.ShapeDtypeStruct((m, n), F32),
        compiler_params=_params("parallel"),
        name="norm_matmul",
    )(h, gain.reshape(1, d), w)


def memory_xattn(h, mem2d, norm_x, norm_m, wq, wkv, wo, q_gain, k_gain, bsz, seq):
    q = norm_matmul(h, norm_x, wq.astype(BF16))
    kv = norm_matmul(mem2d, norm_m, wkv.astype(BF16))
    o = xattn_core(q, kv, q_gain, k_gain, bsz, seq)
    return matmul([(o, wo.astype(BF16))], residual=h)


def _mix_kernel(h_ref, hp_ref, g_ref, mu_ref, *rest, tm, tiles_per_seq):
    outs, ext_ref = rest[:-1], rest[-1]
    i = pl.program_id(0)
    hn = _rms(h_ref[...]) * g_ref[...]
    hp = _rms(hp_ref[...]) * g_ref[...]
    hp = jnp.where(i % tiles_per_seq == 0, 0.0, hp)
    ext_ref[0:SUBLANES, :] = hp
    ext_ref[SUBLANES:tm + SUBLANES, :] = hn
    xx = ext_ref[SUBLANES - 1:tm + SUBLANES - 1, :] - hn
    for j, o_ref in enumerate(outs):
        o_ref[...] = (hn + xx * mu_ref[j:j + 1, :]).astype(o_ref.dtype)


def rwkv_mix(h, gain, mu, seq):
    m, d = h.shape
    tm = min(ROW_TILE, seq)
    n_mix = mu.shape[0]
    kern = functools.partial(_mix_kernel, tm=tm, tiles_per_seq=seq // tm)
    row = pl.BlockSpec((tm, d), lambda i: (i, 0))
    return pl.pallas_call(
        kern,
        grid=(m // tm,),
        in_specs=[row, pl.BlockSpec((SUBLANES, d), lambda i: (jnp.maximum(i * (tm // SUBLANES) - 1, 0), 0)),
                  pl.BlockSpec((1, d), lambda i: (0, 0)), pl.BlockSpec((n_mix, d), lambda i: (0, 0))],
        out_specs=[row] * n_mix,
        out_shape=[jax.ShapeDtypeStruct((m, d), BF16)] * n_mix,
        scratch_shapes=[pltpu.VMEM((tm + SUBLANES, d), F32)],
        compiler_params=_params("parallel"),
        name="rwkv_mix",
    )(h, h, gain.reshape(1, d), mu)


WKV_LANE_GROUP = LANES // 4
WKV_ROWS_PER_BATCH = D_MODEL // LANES


def _wkv_fused_kernel(r_ref, k_ref, v_ref, w_ref, a_ref, g_ref, kkp_ref, kap_ref, rkp_ref, lnw_ref, lnb_ref,
                      o_ref, s_ref, wd_s, kh_s, kk_s, b_s, r_s, v_s, g_s, *, steps):
    n = RWKV_HEAD_DIM
    nb = LANES // WKV_LANE_GROUP
    halves = WKV_ROWS_PER_BATCH // SUBLANES
    ahead = min(WKV_LOOKAHEAD, steps)

    @pl.when(pl.program_id(0) == 0)
    def _():
        s_ref[...] = jnp.zeros_like(s_ref)

    def lane_group(cnt):
        return lax.broadcasted_iota(I32, (cnt, SUBLANES, LANES), 2) // WKV_LANE_GROUP

    def to_scan(x_ref, t0, cnt):
        grp = lane_group(cnt)
        rows = [x_ref[b, pl.ds(t0, cnt)].astype(F32) for b in range(nb)]
        pieces = []
        for n4 in range(nb):
            for v in range(halves):
                acc = None
                for b in range(nb):
                    x = rows[b][:, SUBLANES * v:SUBLANES * (v + 1), :]
                    shift = (WKV_LANE_GROUP * (b - n4)) % LANES
                    if shift:
                        x = pltpu.roll(x, shift, axis=2)
                    acc = x if acc is None else jnp.where(grp == b, x, acc)
                pieces.append(acc)
        return pieces

    def prepare(t0, cnt):
        rs, ks, vs, ws, as_, gs = (to_scan(x, t0, cnt) for x in (r_ref, k_ref, v_ref, w_ref, a_ref, g_ref))
        kks = [ks[u] * kkp_ref[SUBLANES * u:SUBLANES * (u + 1), :] for u in range(len(ks))]
        ss = kks[0] * kks[0]
        for u in range(1, len(kks)):
            ss = ss + kks[u] * kks[u]
        inv_norm = 1.0 / jnp.maximum(jnp.sqrt(jnp.sum(ss, axis=1, keepdims=True)), 1e-12)
        for u in range(len(ks)):
            rows = slice(SUBLANES * u, SUBLANES * (u + 1))
            at = (pl.ds(t0, cnt), rows, slice(None))
            kk = kks[u] * inv_norm
            wd_s[at] = jnp.exp(-jnp.exp(-_softplus(-ws[u]) - 0.5))
            kk_s[at] = kk
            b_s[at] = kk * as_[u]
            kh_s[at] = ks[u] * (1.0 + (as_[u] - 1.0) * kap_ref[rows, :])
            r_s[at] = rs[u]
            v_s[at] = vs[u]
            g_s[at] = gs[u]

    def finish(t, y):
        inv_n = 1.0 / n
        mean = jnp.sum(y, axis=0, keepdims=True) * inv_n
        yc = y - mean
        var = jnp.sum(yc * yc, axis=0, keepdims=True) * inv_n
        bonus = jnp.sum(r_s[t] * kh_s[t] * rkp_ref[...], axis=0, keepdims=True) * v_s[t]
        o = (yc * lax.rsqrt(var + RWKV_GN_EPS) * lnw_ref[...] + lnb_ref[...] + bonus) * g_s[t]
        grp = lane_group(1)[0]
        for b in range(nb):
            parts = []
            for v in range(halves):
                acc = None
                for n4 in range(nb):
                    u = n4 * halves + v
                    x = o[SUBLANES * u:SUBLANES * (u + 1), :]
                    shift = (WKV_LANE_GROUP * (n4 - b)) % LANES
                    if shift:
                        x = pltpu.roll(x, shift, axis=1)
                    acc = x if acc is None else jnp.where(grp == n4, x, acc)
                parts.append(acc)
            o_ref[b, pl.ds(t, 1)] = jnp.concatenate(parts, axis=0).astype(o_ref.dtype)[None]

    prepare(0, ahead)
    q0 = jnp.zeros((n, LANES), F32)
    for j in range(n):
        q0 = q0 + s_ref[j] * kk_s[0, j:j + 1, :]

    def step(t, q):
        tn = jnp.minimum(t + 1, steps - 1)
        v = v_s[t]
        y = jnp.zeros((n, LANES), F32)
        qn = jnp.zeros((n, LANES), F32)
        for j in range(n):
            s_new = s_ref[j] * wd_s[t, j:j + 1, :] - q * b_s[t, j:j + 1, :] + v * kh_s[t, j:j + 1, :]
            s_ref[j] = s_new
            y = y + s_new * r_s[t, j:j + 1, :]
            qn = qn + s_new * kk_s[tn, j:j + 1, :]
        finish(t, y)
        prepare(jnp.minimum(t + ahead, steps - 1), 1)
        return qn

    lax.fori_loop(0, steps, step, q0)


def _to_scan_order(x, axis):
    axis = axis % x.ndim
    shp = x.shape
    split = shp[:axis] + (RWKV_HEADS, LANES // WKV_LANE_GROUP, WKV_ROWS_PER_BATCH) + shp[axis + 1:]
    order = tuple(range(axis)) + (axis + 2, axis + 1, axis) + tuple(range(axis + 3, x.ndim + 2))
    return x.reshape(split).transpose(order).reshape(shp)


def _scan_tile(p, bsz):
    return jnp.tile(p.reshape(RWKV_HEADS, RWKV_HEAD_DIM).T, (1, bsz))


def wkv_fused(r, k, v, w_raw, a_sig, g, k_k, k_a, r_k, ln_w, ln_b, bsz, seq):
    assert bsz * RWKV_HEADS == LANES and RWKV_HEADS == WKV_LANE_GROUP
    steps = min(WKV_T, seq)
    view = lambda x: x.reshape(bsz, seq, WKV_ROWS_PER_BATCH, LANES)
    blk = pl.BlockSpec((bsz, steps, WKV_ROWS_PER_BATCH, LANES), lambda i: (0, i, 0, 0))
    par = pl.BlockSpec((RWKV_HEAD_DIM, LANES), lambda i: (0, 0))
    scr = pltpu.VMEM((steps, RWKV_HEAD_DIM, LANES), F32)
    out = pl.pallas_call(
        functools.partial(_wkv_fused_kernel, steps=steps),
        grid=(seq // steps,),
        in_specs=[blk] * 6 + [par] * 5,
        out_specs=blk,
        out_shape=jax.ShapeDtypeStruct((bsz, seq, WKV_ROWS_PER_BATCH, LANES), BF16),
        scratch_shapes=[pltpu.VMEM((RWKV_HEAD_DIM, RWKV_HEAD_DIM, LANES), F32)] + [scr] * 7,
        compiler_params=_params("arbitrary"),
        name="wkv_fused",
    )(view(r), view(k), view(v), view(w_raw), view(a_sig), view(g),
      _scan_tile(k_k, bsz), _scan_tile(k_a, bsz), _scan_tile(r_k, bsz), _scan_tile(ln_w, bsz), _scan_tile(ln_b, bsz))
    return out.reshape(bsz * seq, D_MODEL)


def _router_kernel(h_ref, g_ref, wr_ref, info_ref, tile_cnt_ref, total_ref, carry_ref, *, tt):
    E = N_EXPERTS
    i = pl.program_id(0)

    @pl.when(i == 0)
    def _():
        carry_ref[...] = jnp.zeros_like(carry_ref)

    xh, xm, xl = _split3(_rms(h_ref[...]) * g_ref[...])
    wh, wm, wl = _split3(wr_ref[...])
    logits = (_dot_nt(wh, xh) + _dot_nt(wh, xm) + _dot_nt(wm, xh)
              + _dot_nt(wh, xl) + _dot_nt(wl, xh) + _dot_nt(wm, xm))
    eidx = lax.broadcasted_iota(I32, (E, tt), 0)
    v1 = jnp.max(logits, axis=0, keepdims=True)
    i1 = jnp.min(jnp.where(logits == v1, eidx, E), axis=0, keepdims=True)
    rest = jnp.where(eidx == i1, -jnp.inf, logits)
    v2 = jnp.max(rest, axis=0, keepdims=True)
    i2 = jnp.min(jnp.where(rest == v2, eidx, E), axis=0, keepdims=True)
    e2 = jnp.exp(v2 - v1)
    g1 = 1.0 / (1.0 + e2)
    g2 = e2 / (1.0 + e2)
    hit1 = eidx == i1
    hit2 = eidx == i2
    onehot = jnp.where(hit1 | hit2, 1.0, 0.0)
    rr = lax.broadcasted_iota(I32, (tt, tt), 0)
    cc = lax.broadcasted_iota(I32, (tt, tt), 1)
    before = jnp.where(rr < cc, 1.0, 0.0).astype(BF16)
    carry = carry_ref[...]
    carry_wide = jnp.concatenate([carry] * (tt // LANES), axis=1)
    cum = _dot(onehot.astype(BF16), before) + carry_wide
    rank1 = jnp.sum(jnp.where(hit1, cum, 0.0), axis=0, keepdims=True)
    rank2 = jnp.sum(jnp.where(hit2, cum, 0.0), axis=0, keepdims=True)
    zero = jnp.zeros((1, tt), F32)
    info_ref[...] = jnp.concatenate([i1.astype(F32), i2.astype(F32), rank1, rank2, g1, g2, zero, zero], axis=0)
    tile_cnt_ref[0] = carry
    carry = carry + _dot(onehot.astype(BF16), jnp.ones((tt, LANES), BF16))
    carry_ref[...] = carry
    total_ref[...] = carry


def moe_router(h, norm_gain, w_router):
    m, d = h.shape
    tt = min(MOE_TILE, m)
    nt = m // tt
    return pl.pallas_call(
        functools.partial(_router_kernel, tt=tt),
        grid=(nt,),
        in_specs=[pl.BlockSpec((tt, d), lambda i: (i, 0)), pl.BlockSpec((1, d), lambda i: (0, 0)),
                  pl.BlockSpec((N_EXPERTS, d), lambda i: (0, 0))],
        out_specs=[pl.BlockSpec((SUBLANES, tt), lambda i: (0, i)),
                   pl.BlockSpec((1, N_EXPERTS, LANES), lambda i: (i, 0, 0)),
                   pl.BlockSpec((N_EXPERTS, LANES), lambda i: (0, 0))],
        out_shape=[jax.ShapeDtypeStruct((SUBLANES, m), F32), jax.ShapeDtypeStruct((nt, N_EXPERTS, LANES), F32),
                   jax.ShapeDtypeStruct((N_EXPERTS, LANES), F32)],
        scratch_shapes=[pltpu.VMEM((N_EXPERTS, LANES), F32)],
        compiler_params=_params("arbitrary"),
        name="moe_router",
    )(h, norm_gain.reshape(1, d), w_router.T)


def _slots_kernel(start_ref, info_ref, o_ref):
    info = info_ref[...]
    e1, e2, r1, r2 = info[0:1], info[1:2], info[2:3], info[3:4]
    s1 = r1
    s2 = r2
    for e in range(N_EXPERTS):
        st = start_ref[e].astype(F32)
        s1 = s1 + jnp.where(e1 == float(e), st, 0.0)
        s2 = s2 + jnp.where(e2 == float(e), st, 0.0)
    o_ref[...] = jnp.concatenate([s1, s2, info[4:5], info[5:6], info[6:8], info[6:8]], axis=0)


def moe_slots(info, slot_start):
    m = info.shape[1]
    tt = min(2048, m)
    return pl.pallas_call(
        _slots_kernel,
        grid_spec=pltpu.PrefetchScalarGridSpec(
            num_scalar_prefetch=1, grid=(m // tt,),
            in_specs=[pl.BlockSpec((SUBLANES, tt), lambda i, st: (0, i))],
            out_specs=pl.BlockSpec((SUBLANES, tt), lambda i, st: (0, i))),
        out_shape=jax.ShapeDtypeStruct((SUBLANES, m), F32),
        compiler_params=_params("parallel"),
        name="moe_slots",
    )(slot_start, info)


def _dispatch_kernel(blk_ref, til_ref, flg_ref, slots_ref, x_ref, o_ref, go_ref, *, bs):
    w = pl.program_id(0)
    flags = flg_ref[w]
    first = (flags & 1) != 0
    real = (flags & 4) != 0

    def gathered():
        sl = slots_ref[...]
        sid = (blk_ref[w] * bs + lax.broadcasted_iota(I32, (bs, sl.shape[1]), 0)).astype(F32)
        hit1 = sid == sl[0:1, :]
        hit2 = sid == sl[1:2, :]
        onehot = jnp.where(hit1 | hit2, 1.0, 0.0).astype(BF16)
        gate = jnp.sum(jnp.where(hit1, sl[2:3, :], 0.0) + jnp.where(hit2, sl[3:4, :], 0.0), axis=1, keepdims=True)
        return _dot(onehot, x_ref[...]).astype(o_ref.dtype), jnp.broadcast_to(gate, go_ref.shape)

    @pl.when(first & real)
    def _():
        o_ref[...], go_ref[...] = gathered()

    @pl.when(first & jnp.logical_not(real))
    def _():
        o_ref[...] = jnp.zeros_like(o_ref)
        go_ref[...] = jnp.zeros_like(go_ref)

    @pl.when(jnp.logical_not(first) & real)
    def _():
        rows, gate = gathered()
        o_ref[...] += rows
        go_ref[...] += gate


def moe_dispatch(xn, slots, wl_blk, wl_til, wl_flg, n_slots):
    m, d = xn.shape
    bs = tt = MOE_TILE
    n_items = wl_blk.shape[0]
    return pl.pallas_call(
        functools.partial(_dispatch_kernel, bs=bs),
        grid_spec=pltpu.PrefetchScalarGridSpec(
            num_scalar_prefetch=3, grid=(n_items,),
            in_specs=[pl.BlockSpec((SUBLANES, tt), lambda w, b, t, f: (0, t[w])),
                      pl.BlockSpec((tt, d), lambda w, b, t, f: (t[w], 0))],
            out_specs=[pl.BlockSpec((bs, d), lambda w, b, t, f: (b[w], 0)),
                       pl.BlockSpec((bs, LANES), lambda w, b, t, f: (b[w], 0))]),
        out_shape=[jax.ShapeDtypeStruct((n_slots, d), BF16), jax.ShapeDtypeStruct((n_slots, LANES), F32)],
        compiler_params=_params("arbitrary"),
        name="moe_dispatch",
    )(wl_blk, wl_til, wl_flg, slots, xn)


def _moe_ffn_kernel(be_ref, rows_ref, x_ref, gate_ref, w1_ref, w3_ref, w2_ref, o_ref, acc_ref, *, sub):
    b = pl.program_id(0)
    f = pl.program_id(1)
    bm = x_ref.shape[0]

    @pl.when(f == 0)
    def _():
        acc_ref[...] = jnp.zeros_like(acc_ref)

    for rows in range(sub, bm + 1, sub):
        @pl.when(rows_ref[b] == rows)
        def _(rows=rows):
            x = x_ref[0:rows, :]
            h = _silu(_dot(x, w1_ref[0].astype(BF16))) * _dot(x, w3_ref[0].astype(BF16))
            acc_ref[0:rows, :] += _dot(h.astype(BF16), w2_ref[0].astype(BF16))

    @pl.when(f == pl.num_programs(1) - 1)
    def _():
        o_ref[...] = (acc_ref[...] * gate_ref[:, 0:1]).astype(o_ref.dtype)


def moe_ffn(xs, slot_gate, w1, w3, w2, block_expert, block_rows):
    n_slots, d = xs.shape
    ff = w1.shape[2]
    bm = MOE_FFN_ROWS
    tf = MOE_TF
    nf = ff // tf

    def f_of(b, f, rows):
        return jnp.where(rows[b] > 0, f, nf - 1)

    return pl.pallas_call(
        functools.partial(_moe_ffn_kernel, sub=MOE_ROW_GROUP),
        grid_spec=pltpu.PrefetchScalarGridSpec(
            num_scalar_prefetch=2, grid=(n_slots // bm, nf),
            in_specs=[pl.BlockSpec((bm, d), lambda b, f, be, nb: (b, 0)),
                      pl.BlockSpec((bm, LANES), lambda b, f, be, nb: (b, 0)),
                      pl.BlockSpec((1, d, tf), lambda b, f, be, nb: (be[b], 0, f_of(b, f, nb))),
                      pl.BlockSpec((1, d, tf), lambda b, f, be, nb: (be[b], 0, f_of(b, f, nb))),
                      pl.BlockSpec((1, tf, d), lambda b, f, be, nb: (be[b], f_of(b, f, nb), 0))],
            out_specs=pl.BlockSpec((bm, d), lambda b, f, be, nb: (b, 0)),
            scratch_shapes=[pltpu.VMEM((bm, d), F32)]),
        out_shape=jax.ShapeDtypeStruct((n_slots, d), BF16),
        compiler_params=_params("arbitrary", "arbitrary"),
        name="moe_ffn",
    )(block_expert, block_rows, xs, slot_gate, w1, w3, w2)


def _combine_kernel(til_ref, blk_ref, flg_ref, tok_ref, y_ref, res_ref, o_ref, *, bs):
    w = pl.program_id(0)
    flags = flg_ref[w]

    @pl.when((flags & 1) != 0)
    def _():
        o_ref[...] = res_ref[...]

    @pl.when((flags & 4) != 0)
    def _():
        tok = tok_ref[...]
        tt = tok.shape[0]
        sid = (blk_ref[w] * bs + lax.broadcasted_iota(I32, (tt, bs), 1)).astype(F32)
        onehot = jnp.where((sid == tok[:, 0:1]) | (sid == tok[:, 1:2]), 1.0, 0.0).astype(BF16)
        o_ref[...] += _dot(onehot, y_ref[...])


def moe_combine(y, tok_info, residual, wl_til, wl_blk, wl_flg):
    m, d = residual.shape
    bs = tt = MOE_TILE
    n_items = wl_til.shape[0]
    return pl.pallas_call(
        functools.partial(_combine_kernel, bs=bs),
        grid_spec=pltpu.PrefetchScalarGridSpec(
            num_scalar_prefetch=3, grid=(n_items,),
            in_specs=[pl.BlockSpec((tt, SUBLANES), lambda w, t, b, f: (t[w], 0)),
                      pl.BlockSpec((bs, d), lambda w, t, b, f: (b[w], 0)),
                      pl.BlockSpec((tt, d), lambda w, t, b, f: (t[w], 0))],
            out_specs=pl.BlockSpec((tt, d), lambda w, t, b, f: (t[w], 0))),
        out_shape=jax.ShapeDtypeStruct((m, d), F32),
        compiler_params=_params("arbitrary"),
        name="moe_combine",
    )(wl_til, wl_blk, wl_flg, tok_info, y, residual)


def _work_list(overlap, n_items):
    n_major, n_minor = overlap.shape
    has = jnp.any(overlap, axis=1)
    marked = jnp.concatenate([overlap[:, :1] | ~has[:, None], overlap[:, 1:]], axis=1)
    maj, mino = jnp.nonzero(marked, size=n_items, fill_value=-1)
    valid = maj >= 0
    real = valid & overlap[jnp.maximum(maj, 0), jnp.maximum(mino, 0)]
    prev = jnp.concatenate([jnp.full((1,), -2, maj.dtype), maj[:-1]])
    nxt = jnp.concatenate([maj[1:], jnp.full((1,), -2, maj.dtype)])
    flags = (valid & (maj != prev)) * 1 + (valid & (maj != nxt)) * 2 + real * 4
    last_maj = jnp.max(jnp.where(valid, maj, 0))
    last_min = jnp.sum(jnp.where(valid & (maj == last_maj) & (maj != nxt), mino, 0))
    maj = jnp.where(valid, maj, last_maj)
    mino = jnp.where(valid, mino, last_min)
    return maj.astype(I32), mino.astype(I32), flags.astype(I32)


def moe_layer(h, norm_gain, w_router, w1, w3, w2):
    m, d = h.shape
    E = N_EXPERTS
    tile = min(MOE_TILE, m)
    n_tiles = m // tile
    ffn_rows = MOE_FFN_ROWS
    xn = rmsnorm(h, norm_gain, BF16)
    info, tile_cnt, total = moe_router(h, norm_gain, w_router)
    before = tile_cnt[:, :, 0]
    counts = total[:, 0].astype(I32)
    padded = (counts + ffn_rows - 1) // ffn_rows * ffn_rows
    ends = jnp.cumsum(padded)
    starts = ends - padded
    n_slots = (m * TOP_K // ffn_rows + E) * ffn_rows
    n_ffn_blocks = n_slots // ffn_rows
    n_dblocks = n_slots // tile
    slots = moe_slots(info, starts.astype(I32))
    db_start = jnp.arange(n_dblocks, dtype=I32) * tile
    db_e = jnp.minimum(jnp.sum(db_start[:, None] >= ends[None, :], axis=1), E - 1)
    r0 = db_start - starts[db_e]
    r1 = jnp.minimum(r0 + tile, counts[db_e])
    after = jnp.concatenate([before[1:], counts[None, :].astype(F32)], axis=0).astype(I32)
    bef = before.astype(I32)
    ov = (bef[:, db_e].T < r1[:, None]) & (after[:, db_e].T > r0[:, None]) & (db_start < ends[-1])[:, None]
    n_items = n_dblocks + E * n_tiles
    d_blk, d_til, d_flg = _work_list(ov, n_items)
    c_til, c_blk, c_flg = _work_list(ov.T, n_items)
    xs, slot_gate = moe_dispatch(xn, slots, d_blk, d_til, d_flg, n_slots)
    fb_start = jnp.arange(n_ffn_blocks, dtype=I32) * ffn_rows
    block_expert = jnp.minimum(jnp.sum(fb_start[:, None] >= ends[None, :], axis=1), E - 1).astype(I32)
    filled = counts[block_expert] - (fb_start - starts[block_expert])
    grp = MOE_ROW_GROUP
    block_rows = jnp.where(fb_start < ends[-1], (jnp.clip(filled, 0, ffn_rows) + grp - 1) // grp * grp, 0).astype(I32)
    ys = moe_ffn(xs, slot_gate, w1, w3, w2, block_expert, block_rows)
    return moe_combine(ys, slots.T, h, c_til, c_blk, c_flg)


def even_mixer_layer(h, posf, p, bsz, seq):
    m = bsz * seq
    hn = rmsnorm(h, p["norm_mix"], BF16)
    w_in = p["w_in"]
    c = np.cumsum([0, D_SSM, D_CONV, SSM_HEADS, D_NSA] + [D_NSA_KV] * 6 + [3 * NSA_HEADS])
    w_proj = jnp.concatenate([w_in[:, c[1]:c[2]], w_in[:, c[4]:c[10]], w_in[:, c[0]:c[1]], w_in[:, c[3]:c[4]]], axis=1)
    proj = matmul([(hn, w_proj.astype(BF16))], out_dtype=BF16)
    kv0 = D_CONV
    w_small = jnp.concatenate([w_in[:, c[2]:c[3]], w_in[:, c[10]:c[11]]], axis=1)
    w_small = jnp.pad(w_small, ((0, 0), (0, LANES - w_small.shape[1]))).astype(BF16)
    small = matmul([(hn, w_small)])
    y_ssm = ssd_group(proj, (D_CONV + 6 * D_NSA_KV) // D_SSM, 0, small, p["conv_w"], p["conv_b"], p["dt_bias"],
                      p["a_log"], p["d_skip"], p["ssm_norm"], bsz, seq)
    qn, ksn, kwn = rope_norm(posf, proj, (D_CONV + 6 * D_NSA_KV + D_SSM) // D_NSA, kv0 // (6 * D_NSA_KV),
                             p["q_gain"], p["ks_gain"], p["kw_gain"])
    kv6 = proj[:, kv0:kv0 + 6 * D_NSA_KV]
    nch = seq // CMP_STRIDE

    def chunks(col0):
        u = kv6[:, col0:col0 + D_NSA_KV].reshape(bsz, nch, CMP_STRIDE, NSA_KV_HEADS, NSA_HEAD_DIM)
        return u.transpose(0, 3, 1, 2, 4).reshape(bsz, NSA_KV_HEADS, nch, CMP_STRIDE * NSA_HEAD_DIM)

    kc, vc = compress(chunks(0), chunks(D_NSA_KV), p["pe_k"], p["pe_v"], p["wk1"], p["wk2"], p["wv1"], p["wv2"],
                      p["kc_gain"])
    o_cmp, selt = cmp_select(qn, kc, vc, bsz, seq)
    def v_tiles(col0, tk):
        tk = min(tk, seq)
        u = kv6[:, col0:col0 + D_NSA_KV].reshape(bsz, seq // tk, tk, NSA_KV_HEADS, NSA_HEAD_DIM)
        return u.transpose(0, 3, 1, 4, 2)

    o_sel = masked_attention(qn, ksn, 0, v_tiles(3 * D_NSA_KV, ATT_TK), bsz, seq, "sel", selt)
    o_win = masked_attention(qn, kwn, 0, v_tiles(5 * D_NSA_KV, WIN_TK), bsz, seq, "win")
    y_nsa = nsa_combine(small, o_cmp, o_sel, o_win)
    w_out = p["w_out"].astype(BF16)
    return matmul([(y_ssm, w_out[:D_SSM]), (y_nsa, w_out[D_SSM:])], residual=h)


def rwkv_layer(h, p, bsz, seq):
    m, d = h.shape
    mixes = rwkv_mix(h, p["norm_mix"], p["mu"], seq)
    bf = lambda w: w.astype(BF16)

    def lora_pad(w_a, w_b):
        r = w_a.shape[1]
        rp = -(-r // LANES) * LANES
        return bf(jnp.pad(w_a, ((0, 0), (0, rp - r)))), bf(jnp.pad(w_b, ((0, rp - r), (0, 0))))

    so = _to_scan_order
    r = matmul([(mixes[0], bf(so(p["w_r"], 1)))], out_dtype=BF16)
    k = matmul([(mixes[2], bf(so(p["w_k"], 1)))], out_dtype=BF16)
    v = matmul([(mixes[3], bf(so(p["w_v"], 1)))], out_dtype=BF16)
    w1, w2 = lora_pad(p["w1"], so(p["w2"], 1))
    w_raw = lora(mixes[1], w1, w2, so(p["w0"], 0), "tanh", None, F32)
    a1, a2 = lora_pad(p["a1"], so(p["a2"], 1))
    a_sig = lora(mixes[4], a1, a2, so(p["a0"], 0), None, "sigmoid", BF16)
    g1, g2 = lora_pad(p["g1"], so(p["g2"], 1))
    g = lora(mixes[5], g1, g2, jnp.zeros((d,), F32), "sigmoid", None, BF16)
    o = wkv_fused(r, k, v, w_raw, a_sig, g, p["k_k"], p["k_a"], p["r_k"].reshape(-1), p["ln_w"], p["ln_b"], bsz, seq)
    return matmul([(o, bf(so(p["w_o"], 0)))], residual=h)


def kernel(x, mem, positions, norm_mix, norm_xattn, norm_mem, norm_ffn, xattn_wq, xattn_wkv, xattn_wo, xattn_q_gain, xattn_k_gain, ev_w_in, ev_conv_w, ev_conv_b, ev_dt_bias, ev_a_log, ev_d_skip, ev_ssm_norm, ev_q_gain, ev_kc_gain, ev_ks_gain, ev_kw_gain, ev_pe_k, ev_pe_v, ev_cmp_wk1, ev_cmp_wk2, ev_cmp_wv1, ev_cmp_wv2, ev_w_out, ev_ffn_w1, ev_ffn_w3, ev_ffn_w2, od_mu, od_w_r, od_w_k, od_w_v, od_w_o, od_w0, od_w1, od_w2, od_a0, od_a1, od_a2, od_g1, od_g2, od_k_k, od_k_a, od_r_k, od_ln_w, od_ln_b, od_router, od_moe_w1, od_moe_w3, od_moe_w2):
    bsz, seq, d = x.shape
    m = bsz * seq
    depth = norm_mix.shape[0]
    h = x.reshape(m, d)
    mem2d = mem.reshape(-1, d)
    posf = positions.astype(F32).reshape(m, 1)
    for layer in range(depth):
        i = layer // 2
        if layer % 2 == 0:
            p = dict(norm_mix=norm_mix[layer], w_in=ev_w_in[i], conv_w=ev_conv_w[i], conv_b=ev_conv_b[i],
                     dt_bias=ev_dt_bias[i], a_log=ev_a_log[i], d_skip=ev_d_skip[i], ssm_norm=ev_ssm_norm[i],
                     q_gain=ev_q_gain[i], kc_gain=ev_kc_gain[i], ks_gain=ev_ks_gain[i], kw_gain=ev_kw_gain[i],
                     pe_k=ev_pe_k[i], pe_v=ev_pe_v[i], wk1=ev_cmp_wk1[i], wk2=ev_cmp_wk2[i], wv1=ev_cmp_wv1[i],
                     wv2=ev_cmp_wv2[i], w_out=ev_w_out[i])
            h = even_mixer_layer(h, posf, p, bsz, seq)
        else:
            p = dict(norm_mix=norm_mix[layer], mu=od_mu[i], w_r=od_w_r[i], w_k=od_w_k[i], w_v=od_w_v[i], w_o=od_w_o[i],
                     w0=od_w0[i], w1=od_w1[i], w2=od_w2[i], a0=od_a0[i], a1=od_a1[i], a2=od_a2[i], g1=od_g1[i],
                     g2=od_g2[i], k_k=od_k_k[i], k_a=od_k_a[i], r_k=od_r_k[i], ln_w=od_ln_w[i], ln_b=od_ln_b[i])
            h = rwkv_layer(h, p, bsz, seq)
        h = memory_xattn(h, mem2d, norm_xattn[layer], norm_mem[layer], xattn_wq[layer], xattn_wkv[layer],
                         xattn_wo[layer], xattn_q_gain[layer], xattn_k_gain[layer], bsz, seq)
        if layer % 2 == 0:
            h = swiglu_ffn(h, norm_ffn[layer], ev_ffn_w1[i].astype(BF16), ev_ffn_w3[i].astype(BF16),
                           ev_ffn_w2[i].astype(BF16))
        else:
            h = moe_layer(h, norm_ffn[layer], od_router[i], od_moe_w1[i], od_moe_w3[i], od_moe_w2[i])
    return h.reshape(bsz, seq, d)
```

```python
import functools
import math

import numpy as np
import jax
import jax.numpy as jnp
from jax import lax
from jax.experimental import pallas as pl
from jax.experimental.pallas import tpu as pltpu

F32 = jnp.float32
BF16 = jnp.bfloat16
I32 = jnp.int32

D_MODEL = 2048
NORM_EPS = 1e-6
NEG_BIG = -1e30

SSM_HEADS = 32
SSM_HEAD_DIM = 64
D_SSM = SSM_HEADS * SSM_HEAD_DIM
SSM_GROUPS = 4
D_STATE = 128
CONV_WIDTH = 4
SSD_CHUNK = 128
D_CONV = D_SSM + 2 * SSM_GROUPS * D_STATE
D_SSM_GROUP = D_SSM // SSM_GROUPS
SSD_CONV_TAIL = 16
SSD_EXT_ROWS = 256

NSA_HEADS = 16
NSA_KV_HEADS = 4
NSA_HEAD_DIM = 128
NSA_Q_PER_KV = NSA_HEADS // NSA_KV_HEADS
D_NSA = NSA_HEADS * NSA_HEAD_DIM
D_NSA_KV = NSA_KV_HEADS * NSA_HEAD_DIM
CMP_BLOCK = 32
CMP_STRIDE = 16
CMP_HIDDEN = 256
SEL_BLOCK = 64
N_SELECT = 16
WINDOW = 512
ROPE_DIM = NSA_HEAD_DIM // 4
ROPE_THETA = 500000.0

RWKV_HEAD_DIM = 64
RWKV_HEADS = D_MODEL // RWKV_HEAD_DIM
RWKV_GN_EPS = 1e-5 * RWKV_HEAD_DIM

XATTN_HEADS = 4
XATTN_HEAD_DIM = 128
D_XATTN = XATTN_HEADS * XATTN_HEAD_DIM

N_EXPERTS = 8
TOP_K = 2

LANES = 128
SUBLANES = 8
VMEM_LIMIT_BYTES = 56 * 1024 * 1024

ROW_TILE = 512
MM_TM = 2048
MM_TN = 512
MM_VMEM_BUDGET = 44 * 1024 * 1024
FFN_TM = 512
FFN_TF = 512
FFN_SPLIT = 256
MOE_FFN_ROWS = 1024
MOE_TF = 256
MOE_ROW_GROUP = 256
COMBINE_TM = 512
MOE_TILE = 512
ATT_TQ = 512
ATT_TK = 512
WIN_TK = 512
LOG2E = 1.4426950408889634
CMP_TQ = 512
WKV_T = 64
WKV_LOOKAHEAD = 8


def _params(*sem):
    return pltpu.CompilerParams(dimension_semantics=sem, vmem_limit_bytes=VMEM_LIMIT_BYTES)


def _sigmoid(x):
    return 1.0 / (1.0 + jnp.exp(-x))


def _silu(x):
    return x * _sigmoid(x)


def _softplus(x):
    return jnp.maximum(x, 0.0) + jnp.log(1.0 + jnp.exp(-jnp.abs(x)))


def _split3(x):
    h = x.astype(BF16)
    r = x - h.astype(F32)
    m = r.astype(BF16)
    l = (r - m.astype(F32)).astype(BF16)
    return h, m, l


def _dot(a, b):
    return jnp.dot(a, b, preferred_element_type=F32)


def _dot_nt(a, b):
    return lax.dot_general(a, b, (((1,), (1,)), ((), ())), preferred_element_type=F32)


def _dot_f32_sel(x, e):
    h, m, l = _split3(x)
    return _dot(h, e) + _dot(m, e) + _dot(l, e)


def _sel_dot_f32(e, x):
    h, m, l = _split3(x)
    return _dot(e, h) + _dot(e, m) + _dot(e, l)


def _rms(x, eps=NORM_EPS):
    return x * lax.rsqrt(jnp.mean(x * x, axis=-1, keepdims=True) + eps)


def _rmsnorm_kernel(x_ref, g_ref, o_ref):
    o_ref[...] = (_rms(x_ref[...]) * g_ref[...]).astype(o_ref.dtype)


def rmsnorm(x, gain, out_dtype):
    m, d = x.shape
    tm = min(ROW_TILE, m)
    return pl.pallas_call(
        _rmsnorm_kernel,
        grid=(m // tm,),
        in_specs=[pl.BlockSpec((tm, d), lambda i: (i, 0)), pl.BlockSpec((1, d), lambda i: (0, 0))],
        out_specs=pl.BlockSpec((tm, d), lambda i: (i, 0)),
        out_shape=jax.ShapeDtypeStruct((m, d), out_dtype),
        compiler_params=_params("parallel"),
        name="rmsnorm",
    )(x, gain.reshape(1, d))


def _mm_kernel(*refs, n_pairs, has_res):
    o_ref = refs[-1]
    acc = None
    for p in range(n_pairs):
        d = _dot(refs[2 * p][...], refs[2 * p + 1][...])
        acc = d if acc is None else acc + d
    if has_res:
        acc = acc + refs[2 * n_pairs][...]
    o_ref[...] = acc.astype(o_ref.dtype)


def matmul(pairs, residual=None, out_dtype=F32):
    m = pairs[0][0].shape[0]
    n = pairs[0][1].shape[1]
    k_total = sum(a.shape[1] for a, _ in pairs)
    tm = min(MM_TM if k_total <= D_MODEL else MM_TM // 2, m)
    out_bytes = jnp.dtype(out_dtype).itemsize + (4 if residual is not None else 0)

    def vmem_bytes(tn):
        return 2 * (tm * k_total * 2 + k_total * tn * 2 + tm * tn * out_bytes)

    tn = next((t for t in (2 * MM_TN, MM_TN, 256, LANES) if n % t == 0 and vmem_bytes(t) <= MM_VMEM_BUDGET), LANES)
    in_specs, args = [], []
    for a, w in pairs:
        kk = a.shape[1]
        in_specs += [pl.BlockSpec((tm, kk), lambda i, j: (i, 0)), pl.BlockSpec((kk, tn), lambda i, j: (0, j))]
        args += [a, w]
    if residual is not None:
        in_specs.append(pl.BlockSpec((tm, tn), lambda i, j: (i, j)))
        args.append(residual)
    kern = functools.partial(_mm_kernel, n_pairs=len(pairs), has_res=residual is not None)
    return pl.pallas_call(
        kern,
        grid=(m // tm, n // tn),
        in_specs=in_specs,
        out_specs=pl.BlockSpec((tm, tn), lambda i, j: (i, j)),
        out_shape=jax.ShapeDtypeStruct((m, n), out_dtype),
        compiler_params=_params("parallel", "parallel"),
        name="matmul",
    )(*args)


def _act(x, act):
    if act == "tanh":
        return jnp.tanh(x)
    if act == "sigmoid":
        return _sigmoid(x)
    return x


def _lora_kernel(a_ref, wa_ref, wb_ref, bias_ref, o_ref, *, act_mid, act_out):
    mid = _act(_dot(a_ref[...], wa_ref[...]), act_mid).astype(BF16)
    o_ref[...] = _act(_dot(mid, wb_ref[...]) + bias_ref[...], act_out).astype(o_ref.dtype)


def lora(a, w_a, w_b, bias, act_mid, act_out, out_dtype):
    m, d = a.shape
    r = w_a.shape[1]
    n = w_b.shape[1]
    tm = min(2 * ROW_TILE, m)
    return pl.pallas_call(
        functools.partial(_lora_kernel, act_mid=act_mid, act_out=act_out),
        grid=(m // tm,),
        in_specs=[pl.BlockSpec((tm, d), lambda i: (i, 0)), pl.BlockSpec((d, r), lambda i: (0, 0)),
                  pl.BlockSpec((r, n), lambda i: (0, 0)), pl.BlockSpec((1, n), lambda i: (0, 0))],
        out_specs=pl.BlockSpec((tm, n), lambda i: (i, 0)),
        out_shape=jax.ShapeDtypeStruct((m, n), out_dtype),
        compiler_params=_params("parallel"),
        name="lora",
    )(a, w_a, w_b, bias.reshape(1, n).astype(F32))


def _ffn_kernel(g_ref, w1_ref, w3_ref, w2_ref, res_ref, o_ref, x_ref):
    f = pl.program_id(1)

    @pl.when(f == 0)
    def _():
        res = res_ref[...]
        o_ref[...] = res
        x_ref[...] = (_rms(res) * g_ref[...]).astype(x_ref.dtype)

    x = x_ref[...]
    tf = w1_ref.shape[1]
    cols = [slice(c, c + FFN_SPLIT) for c in range(0, tf, FFN_SPLIT)]
    gate_up = [(_dot(x, w1_ref[:, c]), _dot(x, w3_ref[:, c])) for c in cols]
    hs = [(_silu(g) * u).astype(BF16) for g, u in gate_up]
    acc = _dot(hs[0], w2_ref[cols[0], :])
    for h, c in zip(hs[1:], cols[1:]):
        acc = acc + _dot(h, w2_ref[c, :])
    o_ref[...] += acc


def swiglu_ffn(residual, norm_gain, w1, w3, w2):
    m, d = residual.shape
    ff = w1.shape[1]
    tm = min(FFN_TM, m)
    tf = FFN_TF
    return pl.pallas_call(
        _ffn_kernel,
        grid=(m // tm, ff // tf),
        in_specs=[pl.BlockSpec((1, d), lambda i, f: (0, 0)),
                  pl.BlockSpec((d, tf), lambda i, f: (0, f)),
                  pl.BlockSpec((d, tf), lambda i, f: (0, f)),
                  pl.BlockSpec((tf, d), lambda i, f: (f, 0)),
                  pl.BlockSpec((tm, d), lambda i, f: (i, 0))],
        out_specs=pl.BlockSpec((tm, d), lambda i, f: (i, 0)),
        out_shape=jax.ShapeDtypeStruct((m, d), F32),
        scratch_shapes=[pltpu.VMEM((tm, d), BF16)],
        compiler_params=_params("parallel", "arbitrary"),
        name="swiglu_ffn",
    )(norm_gain.reshape(1, d), w1, w3, w2, residual)


def _ssd_kernel(z_ref, xbc_ref, dt_ref, shift_ref, cw_ref, cb_ref, dtb_ref, alog_ref, dskip_ref, nw_ref, eh_ref,
                o_ref, ext_ref, st_ref):
    L = SSD_CHUNK
    G = SSM_GROUPS
    GW = D_SSM_GROUP
    T = SSD_CONV_TAIL
    c = pl.program_id(1)

    @pl.when(c == 0)
    def _():
        ext_ref[...] = jnp.zeros_like(ext_ref)
        st_ref[...] = jnp.zeros_like(st_ref)

    ext_ref[0:T, :] = ext_ref[L:L + T, :]
    ext_ref[T:L + T, :] = xbc_ref[...]
    shifted = _dot(shift_ref[...], ext_ref[...])
    acc = jnp.zeros((L, D_CONV), F32) + cb_ref[...]
    for k in range(CONV_WIDTH):
        acc = acc + cw_ref[k:k + 1, :] * shifted[k * L:(k + 1) * L, :]
    xc = _silu(acc)

    dt = _softplus(dt_ref[...] + dtb_ref[...])
    a = -jnp.exp(alog_ref[...])
    adt = dt * a
    row = lax.broadcasted_iota(I32, (L, L), 0)
    col = lax.broadcasted_iota(I32, (L, L), 1)
    causal = row >= col
    tril = jnp.where(causal, 1.0, 0.0).astype(BF16)
    a_cum = _sel_dot_f32(tril, adt)
    a_cum_t = a_cum.T
    eh = eh_ref[...]
    acx = _dot_f32_sel(a_cum, eh)
    dtx = _dot_f32_sel(dt, eh)
    a_end = acx[L - 1:L, :]
    xs = xc[:, :D_SSM]
    xdt = xs * dtx
    xdte = (xdt * jnp.exp(a_end - acx)).astype(BF16)
    from_start = jnp.exp(acx)
    lane = lax.broadcasted_iota(I32, (L, LANES), 1)
    lo_half = lane < SSM_HEAD_DIM
    z = z_ref[...].astype(F32)

    for g in range(G):
        bm = xc[:, D_SSM + g * D_STATE:D_SSM + (g + 1) * D_STATE]
        cm = xc[:, D_SSM + G * D_STATE + g * D_STATE:D_SSM + G * D_STATE + (g + 1) * D_STATE]
        bm_b = bm.astype(BF16)
        cm_b = cm.astype(BF16)
        cb = _dot_nt(cm_b, bm_b)
        st = st_ref[g]
        y_off = _dot(cm_b, st.astype(BF16)) * from_start[:, g * GW:(g + 1) * GW]
        pieces = []
        for pp in range(GW // LANES):
            h0 = g * (GW // SSM_HEAD_DIM) + 2 * pp
            ms = []
            for hh in (h0, h0 + 1):
                diff = a_cum[:, hh:hh + 1] - a_cum_t[hh:hh + 1, :]
                dec = jnp.exp(jnp.where(causal, diff, NEG_BIG))
                ms.append((cb * dec).astype(BF16))
            lhs = jnp.concatenate(ms, axis=1)
            c0 = g * GW + pp * LANES
            xp = xdt[:, c0:c0 + LANES]
            rhs = jnp.concatenate([jnp.where(lo_half, xp, 0.0), jnp.where(lo_half, 0.0, xp)], axis=0)
            pieces.append(_dot(lhs, rhs.astype(BF16)))
        y = jnp.concatenate(pieces, axis=1) + y_off
        y = y + xs[:, g * GW:(g + 1) * GW] * dskip_ref[:, g * GW:(g + 1) * GW]
        y = y * _silu(z[:, g * GW:(g + 1) * GW])
        y = _rms(y) * nw_ref[:, g * GW:(g + 1) * GW]
        o_ref[:, g * GW:(g + 1) * GW] = y.astype(o_ref.dtype)
        s_new = _dot(bm.T.astype(BF16), xdte[:, g * GW:(g + 1) * GW])
        st_ref[g] = st * jnp.exp(a_end[:, g * GW:(g + 1) * GW]) + s_new


def ssd_group(proj, z_col, xbc_col, small, conv_w, conv_b, dt_bias, a_log, d_skip, norm_w, bsz, seq):
    m = bsz * seq
    L = SSD_CHUNK
    nc = seq // L
    pad = LANES - SSM_HEADS
    eh = np.zeros((LANES, D_SSM), np.float32)
    for h in range(SSM_HEADS):
        eh[h, h * SSM_HEAD_DIM:(h + 1) * SSM_HEAD_DIM] = 1.0
    shift = np.zeros((CONV_WIDTH * L, SSD_EXT_ROWS), np.float32)
    for k in range(CONV_WIDTH):
        shift[k * L + np.arange(L), np.arange(L) + SSD_CONV_TAIL - (CONV_WIDTH - 1) + k] = 1.0
    row_spec = lambda w, col=0: pl.BlockSpec((L, w), lambda b, c: (b * nc + c, col))
    const = lambda r, w: pl.BlockSpec((r, w), lambda b, c: (0, 0))
    return pl.pallas_call(
        _ssd_kernel,
        grid=(bsz, nc),
        in_specs=[row_spec(D_SSM, z_col), row_spec(D_CONV, xbc_col), row_spec(LANES),
                  const(CONV_WIDTH * L, SSD_EXT_ROWS),
                  const(CONV_WIDTH, D_CONV), const(1, D_CONV), const(1, LANES), const(1, LANES),
                  const(1, D_SSM), const(1, D_SSM), const(LANES, D_SSM)],
        out_specs=row_spec(D_SSM),
        out_shape=jax.ShapeDtypeStruct((m, D_SSM), BF16),
        scratch_shapes=[pltpu.VMEM((SSD_EXT_ROWS, D_CONV), BF16),
                        pltpu.VMEM((SSM_GROUPS, D_STATE, D_SSM_GROUP), F32)],
        compiler_params=_params("parallel", "arbitrary"),
        name="ssd_group",
    )(proj, proj, small, jnp.asarray(shift, BF16), conv_w.T, conv_b.reshape(1, D_CONV),
      jnp.pad(dt_bias, (0, pad)).reshape(1, LANES), jnp.pad(a_log, (0, pad)).reshape(1, LANES),
      jnp.repeat(d_skip, SSM_HEAD_DIM).reshape(1, D_SSM), norm_w.reshape(1, D_SSM), jnp.asarray(eh, BF16))


def _rope_norm_kernel(pos_ref, invf_ref, q_ref, kv_ref, qg_ref, ksg_ref, kwg_ref, qo_ref, kso_ref, kwo_ref):
    ang = pos_ref[...] * invf_ref[...]
    cos = jnp.cos(ang)
    sin = jnp.sin(ang)
    lane = lax.broadcasted_iota(I32, ang.shape, 1)
    first = lane < ROPE_DIM // 2
    sin_signed = jnp.where(first, -sin, sin)

    def head(x, gain):
        y = _rms(x.astype(F32)) * gain
        partner = jnp.where(first, pltpu.roll(y, LANES - ROPE_DIM // 2, axis=1), pltpu.roll(y, ROPE_DIM // 2, axis=1))
        return y * cos + partner * sin_signed

    hd = NSA_HEAD_DIM
    scale = hd ** -0.5 * LOG2E
    for h in range(NSA_HEADS):
        qo_ref[:, h * hd:(h + 1) * hd] = (head(q_ref[:, h * hd:(h + 1) * hd], qg_ref[...]) * scale).astype(qo_ref.dtype)
    for h in range(NSA_KV_HEADS):
        ks = kv_ref[:, 2 * D_NSA_KV + h * hd:2 * D_NSA_KV + (h + 1) * hd]
        kw = kv_ref[:, 4 * D_NSA_KV + h * hd:4 * D_NSA_KV + (h + 1) * hd]
        kso_ref[:, h * hd:(h + 1) * hd] = head(ks, ksg_ref[...]).astype(kso_ref.dtype)
        kwo_ref[:, h * hd:(h + 1) * hd] = head(kw, kwg_ref[...]).astype(kwo_ref.dtype)


def rope_norm(posf, proj, q_col, kv_col, q_gain, ks_gain, kw_gain):
    m = proj.shape[0]
    tm = min(ROW_TILE, m)
    half = ROPE_DIM // 2
    inv = np.exp(-math.log(ROPE_THETA) * np.arange(0, ROPE_DIM, 2, dtype=np.float32) / ROPE_DIM).astype(np.float32)
    invf = np.zeros((1, LANES), np.float32)
    invf[0, :half] = inv
    invf[0, half:ROPE_DIM] = inv
    row = lambda w, col=0: pl.BlockSpec((tm, w), lambda i: (i, col))
    const = pl.BlockSpec((1, LANES), lambda i: (0, 0))
    return pl.pallas_call(
        _rope_norm_kernel,
        grid=(m // tm,),
        in_specs=[row(1), const, row(D_NSA, q_col), row(6 * D_NSA_KV, kv_col), const, const, const],
        out_specs=[row(D_NSA), row(D_NSA_KV), row(D_NSA_KV)],
        out_shape=[jax.ShapeDtypeStruct((m, D_NSA), BF16), jax.ShapeDtypeStruct((m, D_NSA_KV), BF16),
                   jax.ShapeDtypeStruct((m, D_NSA_KV), BF16)],
        compiler_params=_params("parallel"),
        name="rope_norm",
    )(posf, jnp.asarray(invf), proj, proj, q_gain.reshape(1, LANES), ks_gain.reshape(1, LANES), kw_gain.reshape(1, LANES))


def _compress_kernel(uk_ref, uv_ref, pek_ref, pev_ref, wk1_ref, wk2_ref, wv1_ref, wv2_ref, g_ref, kc_ref, vc_ref):
    def mlp(u, pe_ref, w1_ref, w2_ref):
        u = u.astype(F32)
        n = u.shape[0]
        h1 = _dot((u + pe_ref[0:1, :]).astype(BF16), w1_ref[0])
        h2 = _dot((u + pe_ref[1:2, :]).astype(BF16), w1_ref[1])
        pre = h1 + pltpu.roll(h2, n - 1, axis=0)
        return _dot(_silu(pre).astype(BF16), w2_ref[...])

    kc = mlp(uk_ref[0, 0], pek_ref, wk1_ref, wk2_ref)
    kc_ref[0, 0] = _rms(kc) * g_ref[...]
    vc_ref[0, 0] = mlp(uv_ref[0, 0], pev_ref, wv1_ref, wv2_ref)


def compress(uk, uv, pe_k, pe_v, wk1, wk2, wv1, wv2, kc_gain):
    bsz, nkv, nch, width = uk.shape
    half = CMP_STRIDE * NSA_HEAD_DIM
    u_spec = pl.BlockSpec((1, 1, nch, width), lambda b, g: (b, g, 0, 0))
    o_spec = pl.BlockSpec((1, 1, nch, NSA_HEAD_DIM), lambda b, g: (b, g, 0, 0))
    c2 = lambda s: pl.BlockSpec(s, lambda b, g: (0, 0))
    c3 = lambda s: pl.BlockSpec(s, lambda b, g: (0, 0, 0))
    o_shape = jax.ShapeDtypeStruct((bsz, nkv, nch, NSA_HEAD_DIM), F32)
    return pl.pallas_call(
        _compress_kernel,
        grid=(bsz, nkv),
        in_specs=[u_spec, u_spec, c2((2, half)), c2((2, half)),
                  c3((2, half, CMP_HIDDEN)), c2((CMP_HIDDEN, NSA_HEAD_DIM)),
                  c3((2, half, CMP_HIDDEN)), c2((CMP_HIDDEN, NSA_HEAD_DIM)), c2((1, NSA_HEAD_DIM))],
        out_specs=[o_spec, o_spec],
        out_shape=[o_shape, o_shape],
        compiler_params=_params("parallel", "parallel"),
        name="nsa_compress",
    )(uk, uv, pe_k.reshape(2, half), pe_v.reshape(2, half),
      wk1.reshape(2, half, CMP_HIDDEN).astype(BF16), wk2.astype(BF16),
      wv1.reshape(2, half, CMP_HIDDEN).astype(BF16), wv2.astype(BF16), kc_gain.reshape(1, NSA_HEAD_DIM))


def _stack_heads(q):
    return jnp.concatenate([q[:, r * NSA_HEAD_DIM:(r + 1) * NSA_HEAD_DIM] for r in range(NSA_Q_PER_KV)], axis=0)


def _unstack_heads(o, tq):
    return jnp.concatenate([o[r * tq:(r + 1) * tq, :] for r in range(NSA_Q_PER_KV)], axis=1)


def _cmp_select_kernel(q_ref, kc_ref, vc_ref, ovt_ref, o_ref, sel_ref, *, tq, n_blk):
    R = NSA_Q_PER_KV
    q0 = pl.program_id(2) * tq
    q4 = _stack_heads(q_ref[...])
    kc = kc_ref[0, 0].astype(BF16)
    ncmp = kc.shape[0]
    s = _dot_nt(q4, kc)
    t_row = q0 + lax.broadcasted_iota(I32, (tq, ncmp), 0)
    c_col = lax.broadcasted_iota(I32, (tq, ncmp), 1)
    visf = jnp.where((c_col * CMP_STRIDE + CMP_BLOCK - 1) <= t_row, 1.0, 0.0)
    vis = jnp.concatenate([visf] * R, axis=0) > 0.5
    s = jnp.where(vis, s, NEG_BIG)
    mx = jnp.max(s, axis=-1, keepdims=True)
    e = jnp.where(vis, jnp.exp2(s - mx), 0.0)
    den = jnp.sum(e, axis=-1, keepdims=True)
    p = e / jnp.maximum(den, 1e-30)
    o = _dot(p.astype(BF16), vc_ref[0, 0].astype(BF16))
    o_ref[...] = _unstack_heads(o, tq).astype(o_ref.dtype)

    psum = p[0:tq]
    for r in range(1, R):
        psum = psum + p[r * tq:(r + 1) * tq]
    h, m, l = _split3(psum)
    ovt = ovt_ref[...]
    imp = _dot_nt(ovt, h) + _dot_nt(ovt, m) + _dot_nt(ovt, l)
    jdx = lax.broadcasted_iota(I32, (n_blk, tq), 0)
    t_lane = q0 + lax.broadcasted_iota(I32, (n_blk, tq), 1)
    cur = t_lane // SEL_BLOCK
    causal = jdx <= cur
    forced = ((jdx == 0) | (jdx >= cur - 1)) & causal
    score = jnp.where(forced, jnp.inf, jnp.where(causal, imp, -jnp.inf))
    rank = jnp.zeros((n_blk, tq), F32)
    for jp in range(n_blk):
        rowv = score[jp:jp + 1, :]
        beats = (rowv > score) | ((rowv == score) & (jdx > jp))
        rank = rank + jnp.where(beats, 1.0, 0.0)
    sel = jnp.where((rank < float(min(N_SELECT, n_blk))) & causal, 1.0, 0.0)
    if n_blk < LANES:
        sel = jnp.concatenate([sel, jnp.zeros((LANES - n_blk, tq), F32)], axis=0)
    sel_ref[0, 0] = sel.astype(sel_ref.dtype)


def cmp_select(qn, kc, vc, bsz, seq):
    m = bsz * seq
    G = NSA_KV_HEADS
    tq = CMP_TQ
    nq = seq // tq
    ncmp = kc.shape[2]
    n_blk = seq // SEL_BLOCK
    assert n_blk <= LANES and tq % LANES == 0
    c0 = np.arange(ncmp)[None, :] * CMP_STRIDE
    s0 = np.arange(n_blk)[:, None] * SEL_BLOCK
    ov = np.clip(np.minimum(c0 + CMP_BLOCK, s0 + SEL_BLOCK) - np.maximum(c0, s0), 0, None) / CMP_STRIDE
    n_cmp_valid = (seq - CMP_BLOCK) // CMP_STRIDE + 1
    ov[:, n_cmp_valid:] = 0.0
    kv_spec = pl.BlockSpec((1, 1, ncmp, NSA_HEAD_DIM), lambda b, g, i: (b, g, 0, 0))
    kern = functools.partial(_cmp_select_kernel, tq=tq, n_blk=n_blk)
    return pl.pallas_call(
        kern,
        grid=(bsz, G, nq),
        in_specs=[pl.BlockSpec((tq, NSA_Q_PER_KV * NSA_HEAD_DIM), lambda b, g, i: (b * nq + i, g)),
                  kv_spec, kv_spec, pl.BlockSpec((n_blk, ncmp), lambda b, g, i: (0, 0))],
        out_specs=[pl.BlockSpec((tq, NSA_Q_PER_KV * NSA_HEAD_DIM), lambda b, g, i: (b * nq + i, g)),
                   pl.BlockSpec((1, 1, LANES, tq), lambda b, g, i: (b, g, 0, i))],
        out_shape=[jax.ShapeDtypeStruct((m, D_NSA), BF16), jax.ShapeDtypeStruct((bsz, G, LANES, seq), BF16)],
        compiler_params=_params("parallel", "parallel", "parallel"),
        name="nsa_cmp_select",
    )(qn, kc, vc, jnp.asarray(ov, BF16))


def _masked_attn_kernel(q_ref, k_ref, vt_ref, *rest, mode, tq, tk):
    if mode == "sel":
        selt_ref, o_ref = rest
    else:
        (o_ref,) = rest
    R = NSA_Q_PER_KV
    hd = NSA_HEAD_DIM
    q0 = pl.program_id(2) * tq
    key_row = lax.broadcasted_iota(I32, (tk, tq), 0)
    t_lane = q0 + lax.broadcasted_iota(I32, (tk, tq), 1)
    hi = (q0 + tq - 1) // tk + 1
    q = q_ref[...]
    if mode == "sel":
        key_row_b = lax.broadcasted_iota(I32, (tk, LANES), 0)
        blk_lane = lax.broadcasted_iota(I32, (tk, LANES), 1)
        penalty = ((1.0 - selt_ref[0, 0].astype(F32)) * NEG_BIG).astype(BF16)
        rhs = [jnp.concatenate([q[:, r * hd:(r + 1) * hd].astype(F32).T.astype(BF16), penalty], axis=0)
               for r in range(R)]
        lo = 0
    else:
        lo = jnp.maximum(q0 - (WINDOW - 1), 0) // tk

    def body(kt, carry, diagonal=True):
        k0 = kt * tk
        k_t = k_ref[pl.ds(pl.multiple_of(k0, tk), tk), :]
        v_t = vt_ref[0, 0, kt]
        key = k0 + key_row
        if mode == "sel":
            expand = jnp.where((k0 + key_row_b) // SEL_BLOCK == blk_lane, 1.0, 0.0).astype(BF16)
            lhs = jnp.concatenate([k_t, expand], axis=1)
            ss = [_dot(lhs, rhs[r]) for r in range(R)]
            if diagonal:
                ss = [jnp.where(key <= t_lane, s, NEG_BIG) for s in ss]
        else:
            bias = jnp.where((key <= t_lane) & (key > t_lane - WINDOW), 0.0, NEG_BIG)
            ss = [_dot_nt(k_t, q[:, r * hd:(r + 1) * hd]) + bias for r in range(R)]
        stats = []
        for r in range(R):
            m_prev, l_prev = carry[3 * r], carry[3 * r + 1]
            m_new = jnp.maximum(m_prev, jnp.max(ss[r], axis=0, keepdims=True))
            p = jnp.exp2(ss[r] - m_new)
            alpha = jnp.exp2(m_prev - m_new)
            stats.append((m_new, alpha * l_prev + jnp.sum(p, axis=0, keepdims=True), alpha, p.astype(BF16)))
        new = []
        for r in range(R):
            m_new, l_new, alpha, p = stats[r]
            new += [m_new, l_new, alpha * carry[3 * r + 2] + _dot(v_t, p)]
        return tuple(new)

    init = (jnp.full((1, tq), NEG_BIG, F32), jnp.zeros((1, tq), F32), jnp.zeros((hd, tq), F32)) * R
    if mode == "sel":
        out = lax.fori_loop(lo, hi - 1, functools.partial(body, diagonal=False), init)
        out = body(hi - 1, out)
    else:
        out = lax.fori_loop(lo, hi, body, init)
    o = jnp.concatenate([(out[3 * r + 2] / out[3 * r + 1]).T for r in range(R)], axis=1)
    o_ref[...] = o.astype(o_ref.dtype)


def masked_attention(qn, k, k_col0, vt, bsz, seq, mode, selt=None):
    m = bsz * seq
    G = NSA_KV_HEADS
    tq = min(ATT_TQ, seq)
    tk = vt.shape[-1]
    assert tk % tq == 0
    nq = seq // tq
    hd = NSA_HEAD_DIM
    q_spec = pl.BlockSpec((tq, NSA_Q_PER_KV * hd), lambda b, g, i: (b * nq + i, g))
    in_specs = [q_spec,
                pl.BlockSpec((seq, hd), lambda b, g, i: (b, k_col0 + g)),
                pl.BlockSpec((1, 1, seq // tk, hd, tk), lambda b, g, i: (b, g, 0, 0, 0))]
    args = [qn, k, vt]
    if mode == "sel":
        in_specs.append(pl.BlockSpec((1, 1, LANES, tq), lambda b, g, i: (b, g, 0, i)))
        args.append(selt)
    return pl.pallas_call(
        functools.partial(_masked_attn_kernel, mode=mode, tq=tq, tk=tk),
        grid=(bsz, G, nq),
        in_specs=in_specs,
        out_specs=q_spec,
        out_shape=jax.ShapeDtypeStruct((m, D_NSA), BF16),
        compiler_params=_params("parallel", "parallel", "arbitrary"),
        name="nsa_attn_" + mode,
    )(*args)


def _nsa_combine_kernel(g_ref, e_ref, oc_ref, os_ref, ow_ref, o_ref):
    gate = _sigmoid(g_ref[...])
    gx = _dot_f32_sel(gate, e_ref[...])
    y = (gx[:, :D_NSA] * oc_ref[...].astype(F32) + gx[:, D_NSA:2 * D_NSA] * os_ref[...].astype(F32)
         + gx[:, 2 * D_NSA:] * ow_ref[...].astype(F32))
    o_ref[...] = y.astype(o_ref.dtype)


def nsa_combine(small, o_cmp, o_sel, o_win):
    m = small.shape[0]
    tm = min(COMBINE_TM, m)
    e = np.zeros((LANES, 3 * D_NSA), np.float32)
    for h in range(NSA_HEADS):
        for br in range(3):
            e[SSM_HEADS + 3 * h + br, br * D_NSA + h * NSA_HEAD_DIM:br * D_NSA + (h + 1) * NSA_HEAD_DIM] = 1.0
    row = lambda w: pl.BlockSpec((tm, w), lambda i: (i, 0))
    return pl.pallas_call(
        _nsa_combine_kernel,
        grid=(m // tm,),
        in_specs=[row(LANES), pl.BlockSpec((LANES, 3 * D_NSA), lambda i: (0, 0)), row(D_NSA), row(D_NSA), row(D_NSA)],
        out_specs=row(D_NSA),
        out_shape=jax.ShapeDtypeStruct((m, D_NSA), BF16),
        compiler_params=_params("parallel"),
        name="nsa_combine",
    )(small, jnp.asarray(e, BF16), o_cmp, o_sel, o_win)


def _xattn_kernel(q_ref, kv_ref, qg_ref, kg_ref, o_ref):
    hd = XATTN_HEAD_DIM
    scale = hd ** -0.5
    outs = []
    for h in range(XATTN_HEADS):
        q = (_rms(q_ref[:, h * hd:(h + 1) * hd]) * qg_ref[...]).astype(BF16)
        k = (_rms(kv_ref[:, h * hd:(h + 1) * hd]) * kg_ref[...]).astype(BF16)
        v = kv_ref[:, D_XATTN + h * hd:D_XATTN + (h + 1) * hd].astype(BF16)
        s = _dot_nt(q, k) * scale
        e = jnp.exp(s - jnp.max(s, axis=-1, keepdims=True))
        p = e / jnp.sum(e, axis=-1, keepdims=True)
        outs.append(_dot(p.astype(BF16), v))
    o_ref[...] = jnp.concatenate(outs, axis=1).astype(o_ref.dtype)


def xattn_core(q, kv, q_gain, k_gain, bsz, seq):
    m = bsz * seq
    n_mem = kv.shape[0] // bsz
    tm = min(ROW_TILE, seq)
    nt = seq // tm
    return pl.pallas_call(
        _xattn_kernel,
        grid=(bsz, nt),
        in_specs=[pl.BlockSpec((tm, D_XATTN), lambda b, i: (b * nt + i, 0)),
                  pl.BlockSpec((n_mem, 2 * D_XATTN), lambda b, i: (b, 0)),
                  pl.BlockSpec((1, XATTN_HEAD_DIM), lambda b, i: (0, 0)),
                  pl.BlockSpec((1, XATTN_HEAD_DIM), lambda b, i: (0, 0))],
        out_specs=pl.BlockSpec((tm, D_XATTN), lambda b, i: (b * nt + i, 0)),
        out_shape=jax.ShapeDtypeStruct((m, D_XATTN), BF16),
        compiler_params=_params("parallel", "parallel"),
        name="xattn_core",
    )(q, kv, q_gain.reshape(1, XATTN_HEAD_DIM), k_gain.reshape(1, XATTN_HEAD_DIM))


def _norm_mm_kernel(h_ref, g_ref, w_ref, o_ref):
    x = (_rms(h_ref[...]) * g_ref[...]).astype(BF16)
    o_ref[...] = _dot(x, w_ref[...]).astype(o_ref.dtype)


def norm_matmul(h, gain, w):
    m, d = h.shape
    n = w.shape[1]
    tm = min(ROW_TILE, m)
    return pl.pallas_call(
        _norm_mm_kernel,
        grid=(m // tm,),
        in_specs=[pl.BlockSpec((tm, d), lambda i: (i, 0)), pl.BlockSpec((1, d), lambda i: (0, 0)),
                  pl.BlockSpec((d, n), lambda i: (0, 0))],
        out_specs=pl.BlockSpec((tm, n), lambda i: (i, 0)),
        out_shape=jax.ShapeDtypeStruct((m, n), F32),
        compiler_params=_params("parallel"),
        name="norm_matmul",
    )(h, gain.reshape(1, d), w)


def memory_xattn(h, mem2d, norm_x, norm_m, wq, wkv, wo, q_gain, k_gain, bsz, seq):
    q = norm_matmul(h, norm_x, wq.astype(BF16))
    kv = norm_matmul(mem2d, norm_m, wkv.astype(BF16))
    o = xattn_core(q, kv, q_gain, k_gain, bsz, seq)
    return matmul([(o, wo.astype(BF16))], residual=h)


def _mix_kernel(h_ref, hp_ref, g_ref, mu_ref, *rest, tm, tiles_per_seq):
    outs, ext_ref = rest[:-1], rest[-1]
    i = pl.program_id(0)
    hn = _rms(h_ref[...]) * g_ref[...]
    hp = _rms(hp_ref[...]) * g_ref[...]
    hp = jnp.where(i % tiles_per_seq == 0, 0.0, hp)
    ext_ref[0:SUBLANES, :] = hp
    ext_ref[SUBLANES:tm + SUBLANES, :] = hn
    xx = ext_ref[SUBLANES - 1:tm + SUBLANES - 1, :] - hn
    for j, o_ref in enumerate(outs):
        o_ref[...] = (hn + xx * mu_ref[j:j + 1, :]).astype(o_ref.dtype)


def rwkv_mix(h, gain, mu, seq):
    m, d = h.shape
    tm = min(ROW_TILE, seq)
    n_mix = mu.shape[0]
    kern = functools.partial(_mix_kernel, tm=tm, tiles_per_seq=seq // tm)
    row = pl.BlockSpec((tm, d), lambda i: (i, 0))
    return pl.pallas_call(
        kern,
        grid=(m // tm,),
        in_specs=[row, pl.BlockSpec((SUBLANES, d), lambda i: (jnp.maximum(i * (tm // SUBLANES) - 1, 0), 0)),
                  pl.BlockSpec((1, d), lambda i: (0, 0)), pl.BlockSpec((n_mix, d), lambda i: (0, 0))],
        out_specs=[row] * n_mix,
        out_shape=[jax.ShapeDtypeStruct((m, d), BF16)] * n_mix,
        scratch_shapes=[pltpu.VMEM((tm + SUBLANES, d), F32)],
        compiler_params=_params("parallel"),
        name="rwkv_mix",
    )(h, h, gain.reshape(1, d), mu)


WKV_LANE_GROUP = LANES // 4
WKV_ROWS_PER_BATCH = D_MODEL // LANES


def _wkv_fused_kernel(r_ref, k_ref, v_ref, w_ref, a_ref, g_ref, kkp_ref, kap_ref, rkp_ref, lnw_ref, lnb_ref,
                      o_ref, s_ref, wd_s, kh_s, kk_s, b_s, r_s, v_s, g_s, *, steps):
    n = RWKV_HEAD_DIM
    nb = LANES // WKV_LANE_GROUP
    halves = WKV_ROWS_PER_BATCH // SUBLANES
    ahead = min(WKV_LOOKAHEAD, steps)

    @pl.when(pl.program_id(0) == 0)
    def _():
        s_ref[...] = jnp.zeros_like(s_ref)

    def lane_group(cnt):
        return lax.broadcasted_iota(I32, (cnt, SUBLANES, LANES), 2) // WKV_LANE_GROUP

    def to_scan(x_ref, t0, cnt):
        grp = lane_group(cnt)
        rows = [x_ref[b, pl.ds(t0, cnt)].astype(F32) for b in range(nb)]
        pieces = []
        for n4 in range(nb):
            for v in range(halves):
                acc = None
                for b in range(nb):
                    x = rows[b][:, SUBLANES * v:SUBLANES * (v + 1), :]
                    shift = (WKV_LANE_GROUP * (b - n4)) % LANES
                    if shift:
                        x = pltpu.roll(x, shift, axis=2)
                    acc = x if acc is None else jnp.where(grp == b, x, acc)
                pieces.append(acc)
        return pieces

    def prepare(t0, cnt):
        rs, ks, vs, ws, as_, gs = (to_scan(x, t0, cnt) for x in (r_ref, k_ref, v_ref, w_ref, a_ref, g_ref))
        kks = [ks[u] * kkp_ref[SUBLANES * u:SUBLANES * (u + 1), :] for u in range(len(ks))]
        ss = kks[0] * kks[0]
        for u in range(1, len(kks)):
            ss = ss + kks[u] * kks[u]
        inv_norm = 1.0 / jnp.maximum(jnp.sqrt(jnp.sum(ss, axis=1, keepdims=True)), 1e-12)
        for u in range(len(ks)):
            rows = slice(SUBLANES * u, SUBLANES * (u + 1))
            at = (pl.ds(t0, cnt), rows, slice(None))
            kk = kks[u] * inv_norm
            wd_s[at] = jnp.exp(-jnp.exp(-_softplus(-ws[u]) - 0.5))
            kk_s[at] = kk
            b_s[at] = kk * as_[u]
            kh_s[at] = ks[u] * (1.0 + (as_[u] - 1.0) * kap_ref[rows, :])
            r_s[at] = rs[u]
            v_s[at] = vs[u]
            g_s[at] = gs[u]

    def finish(t, y):
        inv_n = 1.0 / n
        mean = jnp.sum(y, axis=0, keepdims=True) * inv_n
        yc = y - mean
        var = jnp.sum(yc * yc, axis=0, keepdims=True) * inv_n
        bonus = jnp.sum(r_s[t] * kh_s[t] * rkp_ref[...], axis=0, keepdims=True) * v_s[t]
        o = (yc * lax.rsqrt(var + RWKV_GN_EPS) * lnw_ref[...] + lnb_ref[...] + bonus) * g_s[t]
        grp = lane_group(1)[0]
        for b in range(nb):
            parts = []
            for v in range(halves):
                acc = None
                for n4 in range(nb):
                    u = n4 * halves + v
                    x = o[SUBLANES * u:SUBLANES * (u + 1), :]
                    shift = (WKV_LANE_GROUP * (n4 - b)) % LANES
                    if shift:
                        x = pltpu.roll(x, shift, axis=1)
                    acc = x if acc is None else jnp.where(grp == n4, x, acc)
                parts.append(acc)
            o_ref[b, pl.ds(t, 1)] = jnp.concatenate(parts, axis=0).astype(o_ref.dtype)[None]

    prepare(0, ahead)
    q0 = jnp.zeros((n, LANES), F32)
    for j in range(n):
        q0 = q0 + s_ref[j] * kk_s[0, j:j + 1, :]

    def step(t, q):
        tn = jnp.minimum(t + 1, steps - 1)
        v = v_s[t]
        y = jnp.zeros((n, LANES), F32)
        qn = jnp.zeros((n, LANES), F32)
        for j in range(n):
            s_new = s_ref[j] * wd_s[t, j:j + 1, :] - q * b_s[t, j:j + 1, :] + v * kh_s[t, j:j + 1, :]
            s_ref[j] = s_new
            y = y + s_new * r_s[t, j:j + 1, :]
            qn = qn + s_new * kk_s[tn, j:j + 1, :]
        finish(t, y)
        prepare(jnp.minimum(t + ahead, steps - 1), 1)
        return qn

    lax.fori_loop(0, steps, step, q0)


def _to_scan_order(x, axis):
    axis = axis % x.ndim
    shp = x.shape
    split = shp[:axis] + (RWKV_HEADS, LANES // WKV_LANE_GROUP, WKV_ROWS_PER_BATCH) + shp[axis + 1:]
    order = tuple(range(axis)) + (axis + 2, axis + 1, axis) + tuple(range(axis + 3, x.ndim + 2))
    return x.reshape(split).transpose(order).reshape(shp)


def _scan_tile(p, bsz):
    return jnp.tile(p.reshape(RWKV_HEADS, RWKV_HEAD_DIM).T, (1, bsz))


def wkv_fused(r, k, v, w_raw, a_sig, g, k_k, k_a, r_k, ln_w, ln_b, bsz, seq):
    assert bsz * RWKV_HEADS == LANES and RWKV_HEADS == WKV_LANE_GROUP
    steps = min(WKV_T, seq)
    view = lambda x: x.reshape(bsz, seq, WKV_ROWS_PER_BATCH, LANES)
    blk = pl.BlockSpec((bsz, steps, WKV_ROWS_PER_BATCH, LANES), lambda i: (0, i, 0, 0))
    par = pl.BlockSpec((RWKV_HEAD_DIM, LANES), lambda i: (0, 0))
    scr = pltpu.VMEM((steps, RWKV_HEAD_DIM, LANES), F32)
    out = pl.pallas_call(
        functools.partial(_wkv_fused_kernel, steps=steps),
        grid=(seq // steps,),
        in_specs=[blk] * 6 + [par] * 5,
        out_specs=blk,
        out_shape=jax.ShapeDtypeStruct((bsz, seq, WKV_ROWS_PER_BATCH, LANES), BF16),
        scratch_shapes=[pltpu.VMEM((RWKV_HEAD_DIM, RWKV_HEAD_DIM, LANES), F32)] + [scr] * 7,
        compiler_params=_params("arbitrary"),
        name="wkv_fused",
    )(view(r), view(k), view(v), view(w_raw), view(a_sig), view(g),
      _scan_tile(k_k, bsz), _scan_tile(k_a, bsz), _scan_tile(r_k, bsz), _scan_tile(ln_w, bsz), _scan_tile(ln_b, bsz))
    return out.reshape(bsz * seq, D_MODEL)


def _router_kernel(h_ref, g_ref, wr_ref, info_ref, tile_cnt_ref, total_ref, carry_ref, *, tt):
    E = N_EXPERTS
    i = pl.program_id(0)

    @pl.when(i == 0)
    def _():
        carry_ref[...] = jnp.zeros_like(carry_ref)

    xh, xm, xl = _split3(_rms(h_ref[...]) * g_ref[...])
    wh, wm, wl = _split3(wr_ref[...])
    logits = (_dot_nt(wh, xh) + _dot_nt(wh, xm) + _dot_nt(wm, xh)
              + _dot_nt(wh, xl) + _dot_nt(wl, xh) + _dot_nt(wm, xm))
    eidx = lax.broadcasted_iota(I32, (E, tt), 0)
    v1 = jnp.max(logits, axis=0, keepdims=True)
    i1 = jnp.min(jnp.where(logits == v1, eidx, E), axis=0, keepdims=True)
    rest = jnp.where(eidx == i1, -jnp.inf, logits)
    v2 = jnp.max(rest, axis=0, keepdims=True)
    i2 = jnp.min(jnp.where(rest == v2, eidx, E), axis=0, keepdims=True)
    e2 = jnp.exp(v2 - v1)
    g1 = 1.0 / (1.0 + e2)
    g2 = e2 / (1.0 + e2)
    hit1 = eidx == i1
    hit2 = eidx == i2
    onehot = jnp.where(hit1 | hit2, 1.0, 0.0)
    rr = lax.broadcasted_iota(I32, (tt, tt), 0)
    cc = lax.broadcasted_iota(I32, (tt, tt), 1)
    before = jnp.where(rr < cc, 1.0, 0.0).astype(BF16)
    carry = carry_ref[...]
    carry_wide = jnp.concatenate([carry] * (tt // LANES), axis=1)
    cum = _dot(onehot.astype(BF16), before) + carry_wide
    rank1 = jnp.sum(jnp.where(hit1, cum, 0.0), axis=0, keepdims=True)
    rank2 = jnp.sum(jnp.where(hit2, cum, 0.0), axis=0, keepdims=True)
    zero = jnp.zeros((1, tt), F32)
    info_ref[...] = jnp.concatenate([i1.astype(F32), i2.astype(F32), rank1, rank2, g1, g2, zero, zero], axis=0)
    tile_cnt_ref[0] = carry
    carry = carry + _dot(onehot.astype(BF16), jnp.ones((tt, LANES), BF16))
    carry_ref[...] = carry
    total_ref[...] = carry


def moe_router(h, norm_gain, w_router):
    m, d = h.shape
    tt = min(MOE_TILE, m)
    nt = m // tt
    return pl.pallas_call(
        functools.partial(_router_kernel, tt=tt),
        grid=(nt,),
        in_specs=[pl.BlockSpec((tt, d), lambda i: (i, 0)), pl.BlockSpec((1, d), lambda i: (0, 0)),
                  pl.BlockSpec((N_EXPERTS, d), lambda i: (0, 0))],
        out_specs=[pl.BlockSpec((SUBLANES, tt), lambda i: (0, i)),
                   pl.BlockSpec((1, N_EXPERTS, LANES), lambda i: (i, 0, 0)),
                   pl.BlockSpec((N_EXPERTS, LANES), lambda i: (0, 0))],
        out_shape=[jax.ShapeDtypeStruct((SUBLANES, m), F32), jax.ShapeDtypeStruct((nt, N_EXPERTS, LANES), F32),
                   jax.ShapeDtypeStruct((N_EXPERTS, LANES), F32)],
        scratch_shapes=[pltpu.VMEM((N_EXPERTS, LANES), F32)],
        compiler_params=_params("arbitrary"),
        name="moe_router",
    )(h, norm_gain.reshape(1, d), w_router.T)


def _slots_kernel(start_ref, info_ref, o_ref):
    info = info_ref[...]
    e1, e2, r1, r2 = info[0:1], info[1:2], info[2:3], info[3:4]
    s1 = r1
    s2 = r2
    for e in range(N_EXPERTS):
        st = start_ref[e].astype(F32)
        s1 = s1 + jnp.where(e1 == float(e), st, 0.0)
        s2 = s2 + jnp.where(e2 == float(e), st, 0.0)
    o_ref[...] = jnp.concatenate([s1, s2, info[4:5], info[5:6], info[6:8], info[6:8]], axis=0)


def moe_slots(info, slot_start):
    m = info.shape[1]
    tt = min(2048, m)
    return pl.pallas_call(
        _slots_kernel,
        grid_spec=pltpu.PrefetchScalarGridSpec(
            num_scalar_prefetch=1, grid=(m // tt,),
            in_specs=[pl.BlockSpec((SUBLANES, tt), lambda i, st: (0, i))],
            out_specs=pl.BlockSpec((SUBLANES, tt), lambda i, st: (0, i))),
        out_shape=jax.ShapeDtypeStruct((SUBLANES, m), F32),
        compiler_params=_params("parallel"),
        name="moe_slots",
    )(slot_start, info)


def _dispatch_kernel(blk_ref, til_ref, flg_ref, slots_ref, x_ref, o_ref, go_ref, *, bs):
    w = pl.program_id(0)
    flags = flg_ref[w]
    first = (flags & 1) != 0
    real = (flags & 4) != 0

    def gathered():
        sl = slots_ref[...]
        sid = (blk_ref[w] * bs + lax.broadcasted_iota(I32, (bs, sl.shape[1]), 0)).astype(F32)
        hit1 = sid == sl[0:1, :]
        hit2 = sid == sl[1:2, :]
        onehot = jnp.where(hit1 | hit2, 1.0, 0.0).astype(BF16)
        gate = jnp.sum(jnp.where(hit1, sl[2:3, :], 0.0) + jnp.where(hit2, sl[3:4, :], 0.0), axis=1, keepdims=True)
        return _dot(onehot, x_ref[...]).astype(o_ref.dtype), jnp.broadcast_to(gate, go_ref.shape)

    @pl.when(first & real)
    def _():
        o_ref[...], go_ref[...] = gathered()

    @pl.when(first & jnp.logical_not(real))
    def _():
        o_ref[...] = jnp.zeros_like(o_ref)
        go_ref[...] = jnp.zeros_like(go_ref)

    @pl.when(jnp.logical_not(first) & real)
    def _():
        rows, gate = gathered()
        o_ref[...] += rows
        go_ref[...] += gate


def moe_dispatch(xn, slots, wl_blk, wl_til, wl_flg, n_slots):
    m, d = xn.shape
    bs = tt = MOE_TILE
    n_items = wl_blk.shape[0]
    return pl.pallas_call(
        functools.partial(_dispatch_kernel, bs=bs),
        grid_spec=pltpu.PrefetchScalarGridSpec(
            num_scalar_prefetch=3, grid=(n_items,),
            in_specs=[pl.BlockSpec((SUBLANES, tt), lambda w, b, t, f: (0, t[w])),
                      pl.BlockSpec((tt, d), lambda w, b, t, f: (t[w], 0))],
            out_specs=[pl.BlockSpec((bs, d), lambda w, b, t, f: (b[w], 0)),
                       pl.BlockSpec((bs, LANES), lambda w, b, t, f: (b[w], 0))]),
        out_shape=[jax.ShapeDtypeStruct((n_slots, d), BF16), jax.ShapeDtypeStruct((n_slots, LANES), F32)],
        compiler_params=_params("arbitrary"),
        name="moe_dispatch",
    )(wl_blk, wl_til, wl_flg, slots, xn)


def _moe_ffn_kernel(be_ref, rows_ref, x_ref, gate_ref, w1_ref, w3_ref, w2_ref, o_ref, acc_ref, *, sub):
    b = pl.program_id(0)
    f = pl.program_id(1)
    bm = x_ref.shape[0]

    @pl.when(f == 0)
    def _():
        acc_ref[...] = jnp.zeros_like(acc_ref)

    for rows in range(sub, bm + 1, sub):
        @pl.when(rows_ref[b] == rows)
        def _(rows=rows):
            x = x_ref[0:rows, :]
            h = _silu(_dot(x, w1_ref[0].astype(BF16))) * _dot(x, w3_ref[0].astype(BF16))
            acc_ref[0:rows, :] += _dot(h.astype(BF16), w2_ref[0].astype(BF16))

    @pl.when(f == pl.num_programs(1) - 1)
    def _():
        o_ref[...] = (acc_ref[...] * gate_ref[:, 0:1]).astype(o_ref.dtype)


def moe_ffn(xs, slot_gate, w1, w3, w2, block_expert, block_rows):
    n_slots, d = xs.shape
    ff = w1.shape[2]
    bm = MOE_FFN_ROWS
    tf = MOE_TF
    nf = ff // tf

    def f_of(b, f, rows):
        return jnp.where(rows[b] > 0, f, nf - 1)

    return pl.pallas_call(
        functools.partial(_moe_ffn_kernel, sub=MOE_ROW_GROUP),
        grid_spec=pltpu.PrefetchScalarGridSpec(
            num_scalar_prefetch=2, grid=(n_slots // bm, nf),
            in_specs=[pl.BlockSpec((bm, d), lambda b, f, be, nb: (b, 0)),
                      pl.BlockSpec((bm, LANES), lambda b, f, be, nb: (b, 0)),
                      pl.BlockSpec((1, d, tf), lambda b, f, be, nb: (be[b], 0, f_of(b, f, nb))),
                      pl.BlockSpec((1, d, tf), lambda b, f, be, nb: (be[b], 0, f_of(b, f, nb))),
                      pl.BlockSpec((1, tf, d), lambda b, f, be, nb: (be[b], f_of(b, f, nb), 0))],
            out_specs=pl.BlockSpec((bm, d), lambda b, f, be, nb: (b, 0)),
            scratch_shapes=[pltpu.VMEM((bm, d), F32)]),
        out_shape=jax.ShapeDtypeStruct((n_slots, d), BF16),
        compiler_params=_params("arbitrary", "arbitrary"),
        name="moe_ffn",
    )(block_expert, block_rows, xs, slot_gate, w1, w3, w2)


def _combine_kernel(til_ref, blk_ref, flg_ref, tok_ref, y_ref, res_ref, o_ref, *, bs):
    w = pl.program_id(0)
    flags = flg_ref[w]

    @pl.when((flags & 1) != 0)
    def _():
        o_ref[...] = res_ref[...]

    @pl.when((flags & 4) != 0)
    def _():
        tok = tok_ref[...]
        tt = tok.shape[0]
        sid = (blk_ref[w] * bs + lax.broadcasted_iota(I32, (tt, bs), 1)).astype(F32)
        onehot = jnp.where((sid == tok[:, 0:1]) | (sid == tok[:, 1:2]), 1.0, 0.0).astype(BF16)
        o_ref[...] += _dot(onehot, y_ref[...])


def moe_combine(y, tok_info, residual, wl_til, wl_blk, wl_flg):
    m, d = residual.shape
    bs = tt = MOE_TILE
    n_items = wl_til.shape[0]
    return pl.pallas_call(
        functools.partial(_combine_kernel, bs=bs),
        grid_spec=pltpu.PrefetchScalarGridSpec(
            num_scalar_prefetch=3, grid=(n_items,),
            in_specs=[pl.BlockSpec((tt, SUBLANES), lambda w, t, b, f: (t[w], 0)),
                      pl.BlockSpec((bs, d), lambda w, t, b, f: (b[w], 0)),
                      pl.BlockSpec((tt, d), lambda w, t, b, f: (t[w], 0))],
            out_specs=pl.BlockSpec((tt, d), lambda w, t, b, f: (t[w], 0))),
        out_shape=jax.ShapeDtypeStruct((m, d), F32),
        compiler_params=_params("arbitrary"),
        name="moe_combine",
    )(wl_til, wl_blk, wl_flg, tok_info, y, residual)


def _work_list(overlap, n_items):
    n_major, n_minor = overlap.shape
    has = jnp.any(overlap, axis=1)
    marked = jnp.concatenate([overlap[:, :1] | ~has[:, None], overlap[:, 1:]], axis=1)
    maj, mino = jnp.nonzero(marked, size=n_items, fill_value=-1)
    valid = maj >= 0
    real = valid & overlap[jnp.maximum(maj, 0), jnp.maximum(mino, 0)]
    prev = jnp.concatenate([jnp.full((1,), -2, maj.dtype), maj[:-1]])
    nxt = jnp.concatenate([maj[1:], jnp.full((1,), -2, maj.dtype)])
    flags = (valid & (maj != prev)) * 1 + (valid & (maj != nxt)) * 2 + real * 4
    last_maj = jnp.max(jnp.where(valid, maj, 0))
    last_min = jnp.sum(jnp.where(valid & (maj == last_maj) & (maj != nxt), mino, 0))
    maj = jnp.where(valid, maj, last_maj)
    mino = jnp.where(valid, mino, last_min)
    return maj.astype(I32), mino.astype(I32), flags.astype(I32)


def moe_layer(h, norm_gain, w_router, w1, w3, w2):
    m, d = h.shape
    E = N_EXPERTS
    tile = min(MOE_TILE, m)
    n_tiles = m // tile
    ffn_rows = MOE_FFN_ROWS
    xn = rmsnorm(h, norm_gain, BF16)
    info, tile_cnt, total = moe_router(h, norm_gain, w_router)
    before = tile_cnt[:, :, 0]
    counts = total[:, 0].astype(I32)
    padded = (counts + ffn_rows - 1) // ffn_rows * ffn_rows
    ends = jnp.cumsum(padded)
    starts = ends - padded
    n_slots = (m * TOP_K // ffn_rows + E) * ffn_rows
    n_ffn_blocks = n_slots // ffn_rows
    n_dblocks = n_slots // tile
    slots = moe_slots(info, starts.astype(I32))
    db_start = jnp.arange(n_dblocks, dtype=I32) * tile
    db_e = jnp.minimum(jnp.sum(db_start[:, None] >= ends[None, :], axis=1), E - 1)
    r0 = db_start - starts[db_e]
    r1 = jnp.minimum(r0 + tile, counts[db_e])
    after = jnp.concatenate([before[1:], counts[None, :].astype(F32)], axis=0).astype(I32)
    bef = before.astype(I32)
    ov = (bef[:, db_e].T < r1[:, None]) & (after[:, db_e].T > r0[:, None]) & (db_start < ends[-1])[:, None]
    n_items = n_dblocks + E * n_tiles
    d_blk, d_til, d_flg = _work_list(ov, n_items)
    c_til, c_blk, c_flg = _work_list(ov.T, n_items)
    xs, slot_gate = moe_dispatch(xn, slots, d_blk, d_til, d_flg, n_slots)
    fb_start = jnp.arange(n_ffn_blocks, dtype=I32) * ffn_rows
    block_expert = jnp.minimum(jnp.sum(fb_start[:, None] >= ends[None, :], axis=1), E - 1).astype(I32)
    filled = counts[block_expert] - (fb_start - starts[block_expert])
    grp = MOE_ROW_GROUP
    block_rows = jnp.where(fb_start < ends[-1], (jnp.clip(filled, 0, ffn_rows) + grp - 1) // grp * grp, 0).astype(I32)
    ys = moe_ffn(xs, slot_gate, w1, w3, w2, block_expert, block_rows)
    return moe_combine(ys, slots.T, h, c_til, c_blk, c_flg)


def even_mixer_layer(h, posf, p, bsz, seq):
    m = bsz * seq
    hn = rmsnorm(h, p["norm_mix"], BF16)
    w_in = p["w_in"]
    c = np.cumsum([0, D_SSM, D_CONV, SSM_HEADS, D_NSA] + [D_NSA_KV] * 6 + [3 * NSA_HEADS])
    w_proj = jnp.concatenate([w_in[:, c[1]:c[2]], w_in[:, c[4]:c[10]], w_in[:, c[0]:c[1]], w_in[:, c[3]:c[4]]], axis=1)
    proj = matmul([(hn, w_proj.astype(BF16))], out_dtype=BF16)
    kv0 = D_CONV
    w_small = jnp.concatenate([w_in[:, c[2]:c[3]], w_in[:, c[10]:c[11]]], axis=1)
    w_small = jnp.pad(w_small, ((0, 0), (0, LANES - w_small.shape[1]))).astype(BF16)
    small = matmul([(hn, w_small)])
    y_ssm = ssd_group(proj, (D_CONV + 6 * D_NSA_KV) // D_SSM, 0, small, p["conv_w"], p["conv_b"], p["dt_bias"],
                      p["a_log"], p["d_skip"], p["ssm_norm"], bsz, seq)
    qn, ksn, kwn = rope_norm(posf, proj, (D_CONV + 6 * D_NSA_KV + D_SSM) // D_NSA, kv0 // (6 * D_NSA_KV),
                             p["q_gain"], p["ks_gain"], p["kw_gain"])
    nch = seq // CMP_STRIDE

    def chunks(col0):
        u = proj[:, kv0 + col0:kv0 + col0 + D_NSA_KV].reshape(bsz, nch, CMP_STRIDE, NSA_KV_HEADS, NSA_HEAD_DIM)
        return u.transpose(0, 3, 1, 2, 4).reshape(bsz, NSA_KV_HEADS, nch, CMP_STRIDE * NSA_HEAD_DIM)

    kc, vc = compress(chunks(0), chunks(D_NSA_KV), p["pe_k"], p["pe_v"], p["wk1"], p["wk2"], p["wv1"], p["wv2"],
                      p["kc_gain"])
    o_cmp, selt = cmp_select(qn, kc, vc, bsz, seq)
    def v_tiles(col0, tk):
        tk = min(tk, seq)
        u = proj[:, kv0 + col0:kv0 + col0 + D_NSA_KV].reshape(bsz, seq // tk, tk, NSA_KV_HEADS, NSA_HEAD_DIM)
        return u.transpose(0, 3, 1, 4, 2)

    o_sel = masked_attention(qn, ksn, 0, v_tiles(3 * D_NSA_KV, ATT_TK), bsz, seq, "sel", selt)
    o_win = masked_attention(qn, kwn, 0, v_tiles(5 * D_NSA_KV, WIN_TK), bsz, seq, "win")
    y_nsa = nsa_combine(small, o_cmp, o_sel, o_win)
    w_out = p["w_out"].astype(BF16)
    return matmul([(y_ssm, w_out[:D_SSM]), (y_nsa, w_out[D_SSM:])], residual=h)


def rwkv_layer(h, p, bsz, seq):
    m, d = h.shape
    mixes = rwkv_mix(h, p["norm_mix"], p["mu"], seq)
    bf = lambda w: w.astype(BF16)

    def lora_pad(w_a, w_b):
        r = w_a.shape[1]
        rp = -(-r // LANES) * LANES
        return bf(jnp.pad(w_a, ((0, 0), (0, rp - r)))), bf(jnp.pad(w_b, ((0, rp - r), (0, 0))))

    so = _to_scan_order
    r = matmul([(mixes[0], bf(so(p["w_r"], 1)))], out_dtype=BF16)
    k = matmul([(mixes[2], bf(so(p["w_k"], 1)))], out_dtype=BF16)
    v = matmul([(mixes[3], bf(so(p["w_v"], 1)))], out_dtype=BF16)
    w1, w2 = lora_pad(p["w1"], so(p["w2"], 1))
    w_raw = lora(mixes[1], w1, w2, so(p["w0"], 0), "tanh", None, F32)
    a1, a2 = lora_pad(p["a1"], so(p["a2"], 1))
    a_sig = lora(mixes[4], a1, a2, so(p["a0"], 0), None, "sigmoid", BF16)
    g1, g2 = lora_pad(p["g1"], so(p["g2"], 1))
    g = lora(mixes[5], g1, g2, jnp.zeros((d,), F32), "sigmoid", None, BF16)
    o = wkv_fused(r, k, v, w_raw, a_sig, g, p["k_k"], p["k_a"], p["r_k"].reshape(-1), p["ln_w"], p["ln_b"], bsz, seq)
    return matmul([(o, bf(so(p["w_o"], 0)))], residual=h)


def kernel(x, mem, positions, norm_mix, norm_xattn, norm_mem, norm_ffn, xattn_wq, xattn_wkv, xattn_wo, xattn_q_gain, xattn_k_gain, ev_w_in, ev_conv_w, ev_conv_b, ev_dt_bias, ev_a_log, ev_d_skip, ev_ssm_norm, ev_q_gain, ev_kc_gain, ev_ks_gain, ev_kw_gain, ev_pe_k, ev_pe_v, ev_cmp_wk1, ev_cmp_wk2, ev_cmp_wv1, ev_cmp_wv2, ev_w_out, ev_ffn_w1, ev_ffn_w3, ev_ffn_w2, od_mu, od_w_r, od_w_k, od_w_v, od_w_o, od_w0, od_w1, od_w2, od_a0, od_a1, od_a2, od_g1, od_g2, od_k_k, od_k_a, od_r_k, od_ln_w, od_ln_b, od_router, od_moe_w1, od_moe_w3, od_moe_w2):
    bsz, seq, d = x.shape
    m = bsz * seq
    depth = norm_mix.shape[0]
    h = x.reshape(m, d)
    mem2d = mem.reshape(-1, d)
    posf = positions.astype(F32).reshape(m, 1)
    for layer in range(depth):
        i = layer // 2
        if layer % 2 == 0:
            p = dict(norm_mix=norm_mix[layer], w_in=ev_w_in[i], conv_w=ev_conv_w[i], conv_b=ev_conv_b[i],
                     dt_bias=ev_dt_bias[i], a_log=ev_a_log[i], d_skip=ev_d_skip[i], ssm_norm=ev_ssm_norm[i],
                     q_gain=ev_q_gain[i], kc_gain=ev_kc_gain[i], ks_gain=ev_ks_gain[i], kw_gain=ev_kw_gain[i],
                     pe_k=ev_pe_k[i], pe_v=ev_pe_v[i], wk1=ev_cmp_wk1[i], wk2=ev_cmp_wk2[i], wv1=ev_cmp_wv1[i],
                     wv2=ev_cmp_wv2[i], w_out=ev_w_out[i])
            h = even_mixer_layer(h, posf, p, bsz, seq)
        else:
            p = dict(norm_mix=norm_mix[layer], mu=od_mu[i], w_r=od_w_r[i], w_k=od_w_k[i], w_v=od_w_v[i], w_o=od_w_o[i],
                     w0=od_w0[i], w1=od_w1[i], w2=od_w2[i], a0=od_a0[i], a1=od_a1[i], a2=od_a2[i], g1=od_g1[i],
                     g2=od_g2[i], k_k=od_k_k[i], k_a=od_k_a[i], r_k=od_r_k[i], ln_w=od_ln_w[i], ln_b=od_ln_b[i])
            h = rwkv_layer(h, p, bsz, seq)
        h = memory_xattn(h, mem2d, norm_xattn[layer], norm_mem[layer], xattn_wq[layer], xattn_wkv[layer],
                         xattn_wo[layer], xattn_q_gain[layer], xattn_k_gain[layer], bsz, seq)
        if layer % 2 == 0:
            h = swiglu_ffn(h, norm_ffn[layer], ev_ffn_w1[i].astype(BF16), ev_ffn_w3[i].astype(BF16),
                           ev_ffn_w2[i].astype(BF16))
        else:
            h = moe_layer(h, norm_ffn[layer], od_router[i], od_moe_w1[i], od_moe_w3[i], od_moe_w2[i])
    return h.reshape(bsz, seq, d)
```

```python
import functools
import math

import numpy as np
import jax
import jax.numpy as jnp
from jax import lax
from jax.experimental import pallas as pl
from jax.experimental.pallas import tpu as pltpu

F32 = jnp.float32
BF16 = jnp.bfloat16
I32 = jnp.int32

D_MODEL = 2048
NORM_EPS = 1e-6
NEG_BIG = -1e30

SSM_HEADS = 32
SSM_HEAD_DIM = 64
D_SSM = SSM_HEADS * SSM_HEAD_DIM
SSM_GROUPS = 4
D_STATE = 128
CONV_WIDTH = 4
SSD_CHUNK = 128
D_CONV = D_SSM + 2 * SSM_GROUPS * D_STATE
D_SSM_GROUP = D_SSM // SSM_GROUPS
SSD_CONV_TAIL = 16
SSD_EXT_ROWS = 256

NSA_HEADS = 16
NSA_KV_HEADS = 4
NSA_HEAD_DIM = 128
NSA_Q_PER_KV = NSA_HEADS // NSA_KV_HEADS
D_NSA = NSA_HEADS * NSA_HEAD_DIM
D_NSA_KV = NSA_KV_HEADS * NSA_HEAD_DIM
CMP_BLOCK = 32
CMP_STRIDE = 16
CMP_HIDDEN = 256
SEL_BLOCK = 64
N_SELECT = 16
WINDOW = 512
ROPE_DIM = NSA_HEAD_DIM // 4
ROPE_THETA = 500000.0

RWKV_HEAD_DIM = 64
RWKV_HEADS = D_MODEL // RWKV_HEAD_DIM
RWKV_GN_EPS = 1e-5 * RWKV_HEAD_DIM

XATTN_HEADS = 4
XATTN_HEAD_DIM = 128
D_XATTN = XATTN_HEADS * XATTN_HEAD_DIM

N_EXPERTS = 8
TOP_K = 2

LANES = 128
SUBLANES = 8
VMEM_LIMIT_BYTES = 56 * 1024 * 1024

ROW_TILE = 512
MM_TM = 2048
MM_TN = 512
MM_VMEM_BUDGET = 44 * 1024 * 1024
FFN_TM = 512
FFN_TF = 512
FFN_SPLIT = 256
MOE_FFN_ROWS = 1024
MOE_TF = 256
MOE_ROW_GROUP = 256
COMBINE_TM = 512
MOE_TILE = 512
ATT_TQ = 512
ATT_TK = 512
WIN_TK = 512
LOG2E = 1.4426950408889634
CMP_TQ = 512
WKV_T = 64
WKV_LOOKAHEAD = 8


def _params(*sem):
    return pltpu.CompilerParams(dimension_semantics=sem, vmem_limit_bytes=VMEM_LIMIT_BYTES)


def _sigmoid(x):
    return 1.0 / (1.0 + jnp.exp(-x))


def _silu(x):
    return x * _sigmoid(x)


def _softplus(x):
    return jnp.maximum(x, 0.0) + jnp.log(1.0 + jnp.exp(-jnp.abs(x)))


def _split3(x):
    h = x.astype(BF16)
    r = x - h.astype(F32)
    m = r.astype(BF16)
    l = (r - m.astype(F32)).astype(BF16)
    return h, m, l


def _dot(a, b):
    return jnp.dot(a, b, preferred_element_type=F32)


def _dot_nt(a, b):
    return lax.dot_general(a, b, (((1,), (1,)), ((), ())), preferred_element_type=F32)


def _dot_f32_sel(x, e):
    h, m, l = _split3(x)
    return _dot(h, e) + _dot(m, e) + _dot(l, e)


def _sel_dot_f32(e, x):
    h, m, l = _split3(x)
    return _dot(e, h) + _dot(e, m) + _dot(e, l)


def _rms(x, eps=NORM_EPS):
    return x * lax.rsqrt(jnp.mean(x * x, axis=-1, keepdims=True) + eps)


def _rmsnorm_kernel(x_ref, g_ref, o_ref):
    o_ref[...] = (_rms(x_ref[...]) * g_ref[...]).astype(o_ref.dtype)


def rmsnorm(x, gain, out_dtype):
    m, d = x.shape
    tm = min(ROW_TILE, m)
    return pl.pallas_call(
        _rmsnorm_kernel,
        grid=(m // tm,),
        in_specs=[pl.BlockSpec((tm, d), lambda i: (i, 0)), pl.BlockSpec((1, d), lambda i: (0, 0))],
        out_specs=pl.BlockSpec((tm, d), lambda i: (i, 0)),
        out_shape=jax.ShapeDtypeStruct((m, d), out_dtype),
        compiler_params=_params("parallel"),
        name="rmsnorm",
    )(x, gain.reshape(1, d))


def _mm_kernel(*refs, n_pairs, has_res):
    o_ref = refs[-1]
    acc = None
    for p in range(n_pairs):
        d = _dot(refs[2 * p][...], refs[2 * p + 1][...])
        acc = d if acc is None else acc + d
    if has_res:
        acc = acc + refs[2 * n_pairs][...]
    o_ref[...] = acc.astype(o_ref.dtype)


def matmul(pairs, residual=None, out_dtype=F32):
    m = pairs[0][0].shape[0]
    n = pairs[0][1].shape[1]
    k_total = sum(a.shape[1] for a, _ in pairs)
    tm = min(MM_TM if k_total <= D_MODEL else MM_TM // 2, m)
    out_bytes = jnp.dtype(out_dtype).itemsize + (4 if residual is not None else 0)

    def vmem_bytes(tn):
        return 2 * (tm * k_total * 2 + k_total * tn * 2 + tm * tn * out_bytes)

    tn = next((t for t in (2 * MM_TN, MM_TN, 256, LANES) if n % t == 0 and vmem_bytes(t) <= MM_VMEM_BUDGET), LANES)
    in_specs, args = [], []
    for a, w in pairs:
        kk = a.shape[1]
        in_specs += [pl.BlockSpec((tm, kk), lambda i, j: (i, 0)), pl.BlockSpec((kk, tn), lambda i, j: (0, j))]
        args += [a, w]
    if residual is not None:
        in_specs.append(pl.BlockSpec((tm, tn), lambda i, j: (i, j)))
        args.append(residual)
    kern = functools.partial(_mm_kernel, n_pairs=len(pairs), has_res=residual is not None)
    return pl.pallas_call(
        kern,
        grid=(m // tm, n // tn),
        in_specs=in_specs,
        out_specs=pl.BlockSpec((tm, tn), lambda i, j: (i, j)),
        out_shape=jax.ShapeDtypeStruct((m, n), out_dtype),
        compiler_params=_params("parallel", "parallel"),
        name="matmul",
    )(*args)


def _act(x, act):
    if act == "tanh":
        return jnp.tanh(x)
    if act == "sigmoid":
        return _sigmoid(x)
    return x


def _lora_kernel(a_ref, wa_ref, wb_ref, bias_ref, o_ref, *, act_mid, act_out):
    mid = _act(_dot(a_ref[...], wa_ref[...]), act_mid).astype(BF16)
    o_ref[...] = _act(_dot(mid, wb_ref[...]) + bias_ref[...], act_out).astype(o_ref.dtype)


def lora(a, w_a, w_b, bias, act_mid, act_out, out_dtype):
    m, d = a.shape
    r = w_a.shape[1]
    n = w_b.shape[1]
    tm = min(2 * ROW_TILE, m)
    return pl.pallas_call(
        functools.partial(_lora_kernel, act_mid=act_mid, act_out=act_out),
        grid=(m // tm,),
        in_specs=[pl.BlockSpec((tm, d), lambda i: (i, 0)), pl.BlockSpec((d, r), lambda i: (0, 0)),
                  pl.BlockSpec((r, n), lambda i: (0, 0)), pl.BlockSpec((1, n), lambda i: (0, 0))],
        out_specs=pl.BlockSpec((tm, n), lambda i: (i, 0)),
        out_shape=jax.ShapeDtypeStruct((m, n), out_dtype),
        compiler_params=_params("parallel"),
        name="lora",
    )(a, w_a, w_b, bias.reshape(1, n).astype(F32))


def _ffn_kernel(g_ref, w1_ref, w3_ref, w2_ref, res_ref, o_ref, x_ref):
    f = pl.program_id(1)

    @pl.when(f == 0)
    def _():
        res = res_ref[...]
        o_ref[...] = res
        x_ref[...] = (_rms(res) * g_ref[...]).astype(x_ref.dtype)

    x = x_ref[...]
    tf = w1_ref.shape[1]
    cols = [slice(c, c + FFN_SPLIT) for c in range(0, tf, FFN_SPLIT)]
    gate_up = [(_dot(x, w1_ref[:, c]), _dot(x, w3_ref[:, c])) for c in cols]
    hs = [(_silu(g) * u).astype(BF16) for g, u in gate_up]
    acc = _dot(hs[0], w2_ref[cols[0], :])
    for h, c in zip(hs[1:], cols[1:]):
        acc = acc + _dot(h, w2_ref[c, :])
    o_ref[...] += acc


def swiglu_ffn(residual, norm_gain, w1, w3, w2):
    m, d = residual.shape
    ff = w1.shape[1]
    tm = min(FFN_TM, m)
    tf = FFN_TF
    return pl.pallas_call(
        _ffn_kernel,
        grid=(m // tm, ff // tf),
        in_specs=[pl.BlockSpec((1, d), lambda i, f: (0, 0)),
                  pl.BlockSpec((d, tf), lambda i, f: (0, f)),
                  pl.BlockSpec((d, tf), lambda i, f: (0, f)),
                  pl.BlockSpec((tf, d), lambda i, f: (f, 0)),
                  pl.BlockSpec((tm, d), lambda i, f: (i, 0))],
        out_specs=pl.BlockSpec((tm, d), lambda i, f: (i, 0)),
        out_shape=jax.ShapeDtypeStruct((m, d), F32),
        scratch_shapes=[pltpu.VMEM((tm, d), BF16)],
        compiler_params=_params("parallel", "arbitrary"),
        name="swiglu_ffn",
    )(norm_gain.reshape(1, d), w1, w3, w2, residual)


def _ssd_kernel(z_ref, xbc_ref, dt_ref, shift_ref, cw_ref, cb_ref, dtb_ref, alog_ref, dskip_ref, nw_ref, eh_ref,
                o_ref, ext_ref, st_ref):
    L = SSD_CHUNK
    G = SSM_GROUPS
    GW = D_SSM_GROUP
    T = SSD_CONV_TAIL
    c = pl.program_id(1)

    @pl.when(c == 0)
    def _():
        ext_ref[...] = jnp.zeros_like(ext_ref)
        st_ref[...] = jnp.zeros_like(st_ref)

    ext_ref[0:T, :] = ext_ref[L:L + T, :]
    ext_ref[T:L + T, :] = xbc_ref[...]
    shifted = _dot(shift_ref[...], ext_ref[...])
    acc = jnp.zeros((L, D_CONV), F32) + cb_ref[...]
    for k in range(CONV_WIDTH):
        acc = acc + cw_ref[k:k + 1, :] * shifted[k * L:(k + 1) * L, :]
    xc = _silu(acc)

    dt = _softplus(dt_ref[...] + dtb_ref[...])
    a = -jnp.exp(alog_ref[...])
    adt = dt * a
    row = lax.broadcasted_iota(I32, (L, L), 0)
    col = lax.broadcasted_iota(I32, (L, L), 1)
    causal = row >= col
    tril = jnp.where(causal, 1.0, 0.0).astype(BF16)
    a_cum = _sel_dot_f32(tril, adt)
    a_cum_t = a_cum.T
    eh = eh_ref[...]
    acx = _dot_f32_sel(a_cum, eh)
    dtx = _dot_f32_sel(dt, eh)
    a_end = acx[L - 1:L, :]
    xs = xc[:, :D_SSM]
    xdt = xs * dtx
    xdte = (xdt * jnp.exp(a_end - acx)).astype(BF16)
    from_start = jnp.exp(acx)
    lane = lax.broadcasted_iota(I32, (L, LANES), 1)
    lo_half = lane < SSM_HEAD_DIM
    z = z_ref[...].astype(F32)

    for g in range(G):
        bm = xc[:, D_SSM + g * D_STATE:D_SSM + (g + 1) * D_STATE]
        cm = xc[:, D_SSM + G * D_STATE + g * D_STATE:D_SSM + G * D_STATE + (g + 1) * D_STATE]
        bm_b = bm.astype(BF16)
        cm_b = cm.astype(BF16)
        cb = _dot_nt(cm_b, bm_b)
        st = st_ref[g]
        y_off = _dot(cm_b, st.astype(BF16)) * from_start[:, g * GW:(g + 1) * GW]
        pieces = []
        for pp in range(GW // LANES):
            h0 = g * (GW // SSM_HEAD_DIM) + 2 * pp
            ms = []
            for hh in (h0, h0 + 1):
                diff = a_cum[:, hh:hh + 1] - a_cum_t[hh:hh + 1, :]
                dec = jnp.exp(jnp.where(causal, diff, NEG_BIG))
                ms.append((cb * dec).astype(BF16))
            lhs = jnp.concatenate(ms, axis=1)
            c0 = g * GW + pp * LANES
            xp = xdt[:, c0:c0 + LANES]
            rhs = jnp.concatenate([jnp.where(lo_half, xp, 0.0), jnp.where(lo_half, 0.0, xp)], axis=0)
            pieces.append(_dot(lhs, rhs.astype(BF16)))
        y = jnp.concatenate(pieces, axis=1) + y_off
        y = y + xs[:, g * GW:(g + 1) * GW] * dskip_ref[:, g * GW:(g + 1) * GW]
        y = y * _silu(z[:, g * GW:(g + 1) * GW])
        y = _rms(y) * nw_ref[:, g * GW:(g + 1) * GW]
        o_ref[:, g * GW:(g + 1) * GW] = y.astype(o_ref.dtype)
        s_new = _dot(bm.T.astype(BF16), xdte[:, g * GW:(g + 1) * GW])
        st_ref[g] = st * jnp.exp(a_end[:, g * GW:(g + 1) * GW]) + s_new


def ssd_group(proj, z_col, xbc_col, small, conv_w, conv_b, dt_bias, a_log, d_skip, norm_w, bsz, seq):
    m = bsz * seq
    L = SSD_CHUNK
    nc = seq // L
    pad = LANES - SSM_HEADS
    eh = np.zeros((LANES, D_SSM), np.float32)
    for h in range(SSM_HEADS):
        eh[h, h * SSM_HEAD_DIM:(h + 1) * SSM_HEAD_DIM] = 1.0
    shift = np.zeros((CONV_WIDTH * L, SSD_EXT_ROWS), np.float32)
    for k in range(CONV_WIDTH):
        shift[k * L + np.arange(L), np.arange(L) + SSD_CONV_TAIL - (CONV_WIDTH - 1) + k] = 1.0
    row_spec = lambda w, col=0: pl.BlockSpec((L, w), lambda b, c: (b * nc + c, col))
    const = lambda r, w: pl.BlockSpec((r, w), lambda b, c: (0, 0))
    return pl.pallas_call(
        _ssd_kernel,
        grid=(bsz, nc),
        in_specs=[row_spec(D_SSM, z_col), row_spec(D_CONV, xbc_col), row_spec(LANES),
                  const(CONV_WIDTH * L, SSD_EXT_ROWS),
                  const(CONV_WIDTH, D_CONV), const(1, D_CONV), const(1, LANES), const(1, LANES),
                  const(1, D_SSM), const(1, D_SSM), const(LANES, D_SSM)],
        out_specs=row_spec(D_SSM),
        out_shape=jax.ShapeDtypeStruct((m, D_SSM), BF16),
        scratch_shapes=[pltpu.VMEM((SSD_EXT_ROWS, D_CONV), BF16),
                        pltpu.VMEM((SSM_GROUPS, D_STATE, D_SSM_GROUP), F32)],
        compiler_params=_params("parallel", "arbitrary"),
        name="ssd_group",
    )(proj, proj, small, jnp.asarray(shift, BF16), conv_w.T, conv_b.reshape(1, D_CONV),
      jnp.pad(dt_bias, (0, pad)).reshape(1, LANES), jnp.pad(a_log, (0, pad)).reshape(1, LANES),
      jnp.repeat(d_skip, SSM_HEAD_DIM).reshape(1, D_SSM), norm_w.reshape(1, D_SSM), jnp.asarray(eh, BF16))


def _rope_norm_kernel(pos_ref, invf_ref, q_ref, kv_ref, qg_ref, ksg_ref, kwg_ref, qo_ref, kso_ref, kwo_ref):
    ang = pos_ref[...] * invf_ref[...]
    cos = jnp.cos(ang)
    sin = jnp.sin(ang)
    lane = lax.broadcasted_iota(I32, ang.shape, 1)
    first = lane < ROPE_DIM // 2
    sin_signed = jnp.where(first, -sin, sin)

    def head(x, gain):
        y = _rms(x.astype(F32)) * gain
        partner = jnp.where(first, pltpu.roll(y, LANES - ROPE_DIM // 2, axis=1), pltpu.roll(y, ROPE_DIM // 2, axis=1))
        return y * cos + partner * sin_signed

    hd = NSA_HEAD_DIM
    scale = hd ** -0.5 * LOG2E
    for h in range(NSA_HEADS):
        qo_ref[:, h * hd:(h + 1) * hd] = (head(q_ref[:, h * hd:(h + 1) * hd], qg_ref[...]) * scale).astype(qo_ref.dtype)
    for h in range(NSA_KV_HEADS):
        ks = kv_ref[:, 2 * D_NSA_KV + h * hd:2 * D_NSA_KV + (h + 1) * hd]
        kw = kv_ref[:, 4 * D_NSA_KV + h * hd:4 * D_NSA_KV + (h + 1) * hd]
        kso_ref[:, h * hd:(h + 1) * hd] = head(ks, ksg_ref[...]).astype(kso_ref.dtype)
        kwo_ref[:, h * hd:(h + 1) * hd] = head(kw, kwg_ref[...]).astype(kwo_ref.dtype)


def rope_norm(posf, proj, q_col, kv_col, q_gain, ks_gain, kw_gain):
    m = proj.shape[0]
    tm = min(ROW_TILE, m)
    half = ROPE_DIM // 2
    inv = np.exp(-math.log(ROPE_THETA) * np.arange(0, ROPE_DIM, 2, dtype=np.float32) / ROPE_DIM).astype(np.float32)
    invf = np.zeros((1, LANES), np.float32)
    invf[0, :half] = inv
    invf[0, half:ROPE_DIM] = inv
    row = lambda w, col=0: pl.BlockSpec((tm, w), lambda i: (i, col))
    const = pl.BlockSpec((1, LANES), lambda i: (0, 0))
    return pl.pallas_call(
        _rope_norm_kernel,
        grid=(m // tm,),
        in_specs=[row(1), const, row(D_NSA, q_col), row(6 * D_NSA_KV, kv_col), const, const, const],
        out_specs=[row(D_NSA), row(D_NSA_KV), row(D_NSA_KV)],
        out_shape=[jax.ShapeDtypeStruct((m, D_NSA), BF16), jax.ShapeDtypeStruct((m, D_NSA_KV), BF16),
                   jax.ShapeDtypeStruct((m, D_NSA_KV), BF16)],
        compiler_params=_params("parallel"),
        name="rope_norm",
    )(posf, jnp.asarray(invf), proj, proj, q_gain.reshape(1, LANES), ks_gain.reshape(1, LANES), kw_gain.reshape(1, LANES))


def _compress_kernel(uk_ref, uv_ref, pek_ref, pev_ref, wk1_ref, wk2_ref, wv1_ref, wv2_ref, g_ref, kc_ref, vc_ref):
    def mlp(u, pe_ref, w1_ref, w2_ref):
        u = u.astype(F32)
        n = u.shape[0]
        h1 = _dot((u + pe_ref[0:1, :]).astype(BF16), w1_ref[0])
        h2 = _dot((u + pe_ref[1:2, :]).astype(BF16), w1_ref[1])
        pre = h1 + pltpu.roll(h2, n - 1, axis=0)
        return _dot(_silu(pre).astype(BF16), w2_ref[...])

    kc = mlp(uk_ref[0, 0], pek_ref, wk1_ref, wk2_ref)
    kc_ref[0, 0] = _rms(kc) * g_ref[...]
    vc_ref[0, 0] = mlp(uv_ref[0, 0], pev_ref, wv1_ref, wv2_ref)


def compress(uk, uv, pe_k, pe_v, wk1, wk2, wv1, wv2, kc_gain):
    bsz, nkv, nch, width = uk.shape
    half = CMP_STRIDE * NSA_HEAD_DIM
    u_spec = pl.BlockSpec((1, 1, nch, width), lambda b, g: (b, g, 0, 0))
    o_spec = pl.BlockSpec((1, 1, nch, NSA_HEAD_DIM), lambda b, g: (b, g, 0, 0))
    c2 = lambda s: pl.BlockSpec(s, lambda b, g: (0, 0))
    c3 = lambda s: pl.BlockSpec(s, lambda b, g: (0, 0, 0))
    o_shape = jax.ShapeDtypeStruct((bsz, nkv, nch, NSA_HEAD_DIM), F32)
    return pl.pallas_call(
        _compress_kernel,
        grid=(bsz, nkv),
        in_specs=[u_spec, u_spec, c2((2, half)), c2((2, half)),
                  c3((2, half, CMP_HIDDEN)), c2((CMP_HIDDEN, NSA_HEAD_DIM)),
                  c3((2, half, CMP_HIDDEN)), c2((CMP_HIDDEN, NSA_HEAD_DIM)), c2((1, NSA_HEAD_DIM))],
        out_specs=[o_spec, o_spec],
        out_shape=[o_shape, o_shape],
        compiler_params=_params("parallel", "parallel"),
        name="nsa_compress",
    )(uk, uv, pe_k.reshape(2, half), pe_v.reshape(2, half),
      wk1.reshape(2, half, CMP_HIDDEN).astype(BF16), wk2.astype(BF16),
      wv1.reshape(2, half, CMP_HIDDEN).astype(BF16), wv2.astype(BF16), kc_gain.reshape(1, NSA_HEAD_DIM))


def _stack_heads(q):
    return jnp.concatenate([q[:, r * NSA_HEAD_DIM:(r + 1) * NSA_HEAD_DIM] for r in range(NSA_Q_PER_KV)], axis=0)


def _unstack_heads(o, tq):
    return jnp.concatenate([o[r * tq:(r + 1) * tq, :] for r in range(NSA_Q_PER_KV)], axis=1)


def _cmp_select_kernel(q_ref, kc_ref, vc_ref, ovt_ref, o_ref, sel_ref, *, tq, n_blk):
    R = NSA_Q_PER_KV
    q0 = pl.program_id(2) * tq
    q4 = _stack_heads(q_ref[...])
    kc = kc_ref[0, 0].astype(BF16)
    ncmp = kc.shape[0]
    s = _dot_nt(q4, kc)
    t_row = q0 + lax.broadcasted_iota(I32, (tq, ncmp), 0)
    c_col = lax.broadcasted_iota(I32, (tq, ncmp), 1)
    visf = jnp.where((c_col * CMP_STRIDE + CMP_BLOCK - 1) <= t_row, 1.0, 0.0)
    vis = jnp.concatenate([visf] * R, axis=0) > 0.5
    s = jnp.where(vis, s, NEG_BIG)
    mx = jnp.max(s, axis=-1, keepdims=True)
    e = jnp.where(vis, jnp.exp2(s - mx), 0.0)
    den = jnp.sum(e, axis=-1, keepdims=True)
    p = e / jnp.maximum(den, 1e-30)
    o = _dot(p.astype(BF16), vc_ref[0, 0].astype(BF16))
    o_ref[...] = _unstack_heads(o, tq).astype(o_ref.dtype)

    psum = p[0:tq]
    for r in range(1, R):
        psum = psum + p[r * tq:(r + 1) * tq]
    h, m, l = _split3(psum)
    ovt = ovt_ref[...]
    imp = _dot_nt(ovt, h) + _dot_nt(ovt, m) + _dot_nt(ovt, l)
    jdx = lax.broadcasted_iota(I32, (n_blk, tq), 0)
    t_lane = q0 + lax.broadcasted_iota(I32, (n_blk, tq), 1)
    cur = t_lane // SEL_BLOCK
    causal = jdx <= cur
    forced = ((jdx == 0) | (jdx >= cur - 1)) & causal
    score = jnp.where(forced, jnp.inf, jnp.where(causal, imp, -jnp.inf))
    jf = jdx.astype(F32)
    picked = jnp.zeros((n_blk, tq), F32)
    for _ in range(min(N_SELECT, n_blk)):
        best = jnp.max(score, axis=0, keepdims=True)
        first = jnp.min(jnp.where(score == best, jf, float(n_blk)), axis=0, keepdims=True)
        hit = jf == first
        picked = jnp.where(hit, 1.0, picked)
        score = jnp.where(hit, -jnp.inf, score)
    sel = jnp.where(causal, picked, 0.0)
    if n_blk < LANES:
        sel = jnp.concatenate([sel, jnp.zeros((LANES - n_blk, tq), F32)], axis=0)
    sel_ref[0, 0] = sel.astype(sel_ref.dtype)


def cmp_select(qn, kc, vc, bsz, seq):
    m = bsz * seq
    G = NSA_KV_HEADS
    tq = CMP_TQ
    nq = seq // tq
    ncmp = kc.shape[2]
    n_blk = seq // SEL_BLOCK
    assert n_blk <= LANES and tq % LANES == 0
    c0 = np.arange(ncmp)[None, :] * CMP_STRIDE
    s0 = np.arange(n_blk)[:, None] * SEL_BLOCK
    ov = np.clip(np.minimum(c0 + CMP_BLOCK, s0 + SEL_BLOCK) - np.maximum(c0, s0), 0, None) / CMP_STRIDE
    n_cmp_valid = (seq - CMP_BLOCK) // CMP_STRIDE + 1
    ov[:, n_cmp_valid:] = 0.0
    kv_spec = pl.BlockSpec((1, 1, ncmp, NSA_HEAD_DIM), lambda b, g, i: (b, g, 0, 0))
    kern = functools.partial(_cmp_select_kernel, tq=tq, n_blk=n_blk)
    return pl.pallas_call(
        kern,
        grid=(bsz, G, nq),
        in_specs=[pl.BlockSpec((tq, NSA_Q_PER_KV * NSA_HEAD_DIM), lambda b, g, i: (b * nq + i, g)),
                  kv_spec, kv_spec, pl.BlockSpec((n_blk, ncmp), lambda b, g, i: (0, 0))],
        out_specs=[pl.BlockSpec((tq, NSA_Q_PER_KV * NSA_HEAD_DIM), lambda b, g, i: (b * nq + i, g)),
                   pl.BlockSpec((1, 1, LANES, tq), lambda b, g, i: (b, g, 0, i))],
        out_shape=[jax.ShapeDtypeStruct((m, D_NSA), BF16), jax.ShapeDtypeStruct((bsz, G, LANES, seq), BF16)],
        compiler_params=_params("parallel", "parallel", "parallel"),
        name="nsa_cmp_select",
    )(qn, kc, vc, jnp.asarray(ov, BF16))


def _masked_attn_kernel(q_ref, k_ref, vt_ref, *rest, mode, tq, tk):
    if mode == "sel":
        selt_ref, o_ref = rest
    else:
        (o_ref,) = rest
    R = NSA_Q_PER_KV
    hd = NSA_HEAD_DIM
    q0 = pl.program_id(2) * tq
    key_row = lax.broadcasted_iota(I32, (tk, tq), 0)
    t_lane = q0 + lax.broadcasted_iota(I32, (tk, tq), 1)
    hi = (q0 + tq - 1) // tk + 1
    q = q_ref[...]
    if mode == "sel":
        key_row_b = lax.broadcasted_iota(I32, (tk, LANES), 0)
        blk_lane = lax.broadcasted_iota(I32, (tk, LANES), 1)
        penalty = ((1.0 - selt_ref[0, 0].astype(F32)) * NEG_BIG).astype(BF16)
        rhs = [jnp.concatenate([q[:, r * hd:(r + 1) * hd].astype(F32).T.astype(BF16), penalty], axis=0)
               for r in range(R)]
        lo = 0
    else:
        lo = jnp.maximum(q0 - (WINDOW - 1), 0) // tk

    def body(kt, carry, diagonal=True):
        k0 = kt * tk
        k_t = k_ref[pl.ds(pl.multiple_of(k0, tk), tk), :]
        v_t = vt_ref[0, 0, kt]
        key = k0 + key_row
        if mode == "sel":
            expand = jnp.where((k0 + key_row_b) // SEL_BLOCK == blk_lane, 1.0, 0.0).astype(BF16)
            lhs = jnp.concatenate([k_t, expand], axis=1)
            ss = [_dot(lhs, rhs[r]) for r in range(R)]
            if diagonal:
                ss = [jnp.where(key <= t_lane, s, NEG_BIG) for s in ss]
        else:
            bias = jnp.where((key <= t_lane) & (key > t_lane - WINDOW), 0.0, NEG_BIG)
            ss = [_dot_nt(k_t, q[:, r * hd:(r + 1) * hd]) + bias for r in range(R)]
        stats = []
        for r in range(R):
            m_prev, l_prev = carry[3 * r], carry[3 * r + 1]
            m_new = jnp.maximum(m_prev, jnp.max(ss[r], axis=0, keepdims=True))
            p = jnp.exp2(ss[r] - m_new)
            alpha = jnp.exp2(m_prev - m_new)
            stats.append((m_new, alpha * l_prev + jnp.sum(p, axis=0, keepdims=True), alpha, p.astype(BF16)))
        new = []
        for r in range(R):
            m_new, l_new, alpha, p = stats[r]
            new += [m_new, l_new, alpha * carry[3 * r + 2] + _dot(v_t, p)]
        return tuple(new)

    init = (jnp.full((1, tq), NEG_BIG, F32), jnp.zeros((1, tq), F32), jnp.zeros((hd, tq), F32)) * R
    if mode == "sel":
        out = lax.fori_loop(lo, hi - 1, functools.partial(body, diagonal=False), init)
        out = body(hi - 1, out)
    else:
        out = lax.fori_loop(lo, hi, body, init)
    o = jnp.concatenate([(out[3 * r + 2] / out[3 * r + 1]).T for r in range(R)], axis=1)
    o_ref[...] = o.astype(o_ref.dtype)


def masked_attention(qn, k, k_col0, vt, bsz, seq, mode, selt=None):
    m = bsz * seq
    G = NSA_KV_HEADS
    tq = min(ATT_TQ, seq)
    tk = vt.shape[-1]
    assert tk % tq == 0
    nq = seq // tq
    hd = NSA_HEAD_DIM
    q_spec = pl.BlockSpec((tq, NSA_Q_PER_KV * hd), lambda b, g, i: (b * nq + i, g))
    in_specs = [q_spec,
                pl.BlockSpec((seq, hd), lambda b, g, i: (b, k_col0 + g)),
                pl.BlockSpec((1, 1, seq // tk, hd, tk), lambda b, g, i: (b, g, 0, 0, 0))]
    args = [qn, k, vt]
    if mode == "sel":
        in_specs.append(pl.BlockSpec((1, 1, LANES, tq), lambda b, g, i: (b, g, 0, i)))
        args.append(selt)
    return pl.pallas_call(
        functools.partial(_masked_attn_kernel, mode=mode, tq=tq, tk=tk),
        grid=(bsz, G, nq),
        in_specs=in_specs,
        out_specs=q_spec,
        out_shape=jax.ShapeDtypeStruct((m, D_NSA), BF16),
        compiler_params=_params("parallel", "parallel", "arbitrary"),
        name="nsa_attn_" + mode,
    )(*args)


def _nsa_combine_kernel(g_ref, e_ref, oc_ref, os_ref, ow_ref, o_ref):
    gate = _sigmoid(g_ref[...])
    gx = _dot_f32_sel(gate, e_ref[...])
    y = (gx[:, :D_NSA] * oc_ref[...].astype(F32) + gx[:, D_NSA:2 * D_NSA] * os_ref[...].astype(F32)
         + gx[:, 2 * D_NSA:] * ow_ref[...].astype(F32))
    o_ref[...] = y.astype(o_ref.dtype)


def nsa_combine(small, o_cmp, o_sel, o_win):
    m = small.shape[0]
    tm = min(COMBINE_TM, m)
    e = np.zeros((LANES, 3 * D_NSA), np.float32)
    for h in range(NSA_HEADS):
        for br in range(3):
            e[SSM_HEADS + 3 * h + br, br * D_NSA + h * NSA_HEAD_DIM:br * D_NSA + (h + 1) * NSA_HEAD_DIM] = 1.0
    row = lambda w: pl.BlockSpec((tm, w), lambda i: (i, 0))
    return pl.pallas_call(
        _nsa_combine_kernel,
        grid=(m // tm,),
        in_specs=[row(LANES), pl.BlockSpec((LANES, 3 * D_NSA), lambda i: (0, 0)), row(D_NSA), row(D_NSA), row(D_NSA)],
        out_specs=row(D_NSA),
        out_shape=jax.ShapeDtypeStruct((m, D_NSA), BF16),
        compiler_params=_params("parallel"),
        name="nsa_combine",
    )(small, jnp.asarray(e, BF16), o_cmp, o_sel, o_win)


def _xattn_kernel(q_ref, kv_ref, qg_ref, kg_ref, o_ref):
    hd = XATTN_HEAD_DIM
    scale = hd ** -0.5
    outs = []
    for h in range(XATTN_HEADS):
        q = (_rms(q_ref[:, h * hd:(h + 1) * hd]) * qg_ref[...]).astype(BF16)
        k = (_rms(kv_ref[:, h * hd:(h + 1) * hd]) * kg_ref[...]).astype(BF16)
        v = kv_ref[:, D_XATTN + h * hd:D_XATTN + (h + 1) * hd].astype(BF16)
        s = _dot_nt(q, k) * scale
        e = jnp.exp(s - jnp.max(s, axis=-1, keepdims=True))
        p = e / jnp.sum(e, axis=-1, keepdims=True)
        outs.append(_dot(p.astype(BF16), v))
    o_ref[...] = jnp.concatenate(outs, axis=1).astype(o_ref.dtype)


def xattn_core(q, kv, q_gain, k_gain, bsz, seq):
    m = bsz * seq
    n_mem = kv.shape[0] // bsz
    tm = min(ROW_TILE, seq)
    nt = seq // tm
    return pl.pallas_call(
        _xattn_kernel,
        grid=(bsz, nt),
        in_specs=[pl.BlockSpec((tm, D_XATTN), lambda b, i: (b * nt + i, 0)),
                  pl.BlockSpec((n_mem, 2 * D_XATTN), lambda b, i: (b, 0)),
                  pl.BlockSpec((1, XATTN_HEAD_DIM), lambda b, i: (0, 0)),
                  pl.BlockSpec((1, XATTN_HEAD_DIM), lambda b, i: (0, 0))],
        out_specs=pl.BlockSpec((tm, D_XATTN), lambda b, i: (b * nt + i, 0)),
        out_shape=jax.ShapeDtypeStruct((m, D_XATTN), BF16),
        compiler_params=_params("parallel", "parallel"),
        name="xattn_core",
    )(q, kv, q_gain.reshape(1, XATTN_HEAD_DIM), k_gain.reshape(1, XATTN_HEAD_DIM))


def _norm_mm_kernel(h_ref, g_ref, w_ref, o_ref):
    x = (_rms(h_ref[...]) * g_ref[...]).astype(BF16)
    o_ref[...] = _dot(x, w_ref[...]).astype(o_ref.dtype)


def norm_matmul(h, gain, w):
    m, d = h.shape
    n = w.shape[1]
    tm = min(ROW_TILE, m)
    return pl.pallas_call(
        _norm_mm_kernel,
        grid=(m // tm,),
        in_specs=[pl.BlockSpec((tm, d), lambda i: (i, 0)), pl.BlockSpec((1, d), lambda i: (0, 0)),
                  pl.BlockSpec((d, n), lambda i: (0, 0))],
        out_specs=pl.BlockSpec((tm, n), lambda i: (i, 0)),
        out_shape=jax.ShapeDtypeStruct((m, n), F32),
        compiler_params=_params("parallel"),
        name="norm_matmul",
    )(h, gain.reshape(1, d), w)


def memory_xattn(h, mem2d, norm_x, norm_m, wq, wkv, wo, q_gain, k_gain, bsz, seq):
    q = norm_matmul(h, norm_x, wq.astype(BF16))
    kv = norm_matmul(mem2d, norm_m, wkv.astype(BF16))
    o = xattn_core(q, kv, q_gain, k_gain, bsz, seq)
    return matmul([(o, wo.astype(BF16))], residual=h)


def _mix_kernel(h_ref, hp_ref, g_ref, mu_ref, *rest, tm, tiles_per_seq):
    outs, ext_ref = rest[:-1], rest[-1]
    i = pl.program_id(0)
    hn = _rms(h_ref[...]) * g_ref[...]
    hp = _rms(hp_ref[...]) * g_ref[...]
    hp = jnp.where(i % tiles_per_seq == 0, 0.0, hp)
    ext_ref[0:SUBLANES, :] = hp
    ext_ref[SUBLANES:tm + SUBLANES, :] = hn
    xx = ext_ref[SUBLANES - 1:tm + SUBLANES - 1, :] - hn
    for j, o_ref in enumerate(outs):
        o_ref[...] = (hn + xx * mu_ref[j:j + 1, :]).astype(o_ref.dtype)


def rwkv_mix(h, gain, mu, seq):
    m, d = h.shape
    tm = min(ROW_TILE, seq)
    n_mix = mu.shape[0]
    kern = functools.partial(_mix_kernel, tm=tm, tiles_per_seq=seq // tm)
    row = pl.BlockSpec((tm, d), lambda i: (i, 0))
    return pl.pallas_call(
        kern,
        grid=(m // tm,),
        in_specs=[row, pl.BlockSpec((SUBLANES, d), lambda i: (jnp.maximum(i * (tm // SUBLANES) - 1, 0), 0)),
                  pl.BlockSpec((1, d), lambda i: (0, 0)), pl.BlockSpec((n_mix, d), lambda i: (0, 0))],
        out_specs=[row] * n_mix,
        out_shape=[jax.ShapeDtypeStruct((m, d), BF16)] * n_mix,
        scratch_shapes=[pltpu.VMEM((tm + SUBLANES, d), F32)],
        compiler_params=_params("parallel"),
        name="rwkv_mix",
    )(h, h, gain.reshape(1, d), mu)


WKV_LANE_GROUP = LANES // 4
WKV_ROWS_PER_BATCH = D_MODEL // LANES


def _wkv_fused_kernel(r_ref, k_ref, v_ref, w_ref, a_ref, g_ref, kkp_ref, kap_ref, rkp_ref, lnw_ref, lnb_ref,
                      o_ref, s_ref, wd_s, kh_s, kk_s, b_s, r_s, v_s, g_s, *, steps):
    n = RWKV_HEAD_DIM
    nb = LANES // WKV_LANE_GROUP
    halves = WKV_ROWS_PER_BATCH // SUBLANES
    ahead = min(WKV_LOOKAHEAD, steps)

    @pl.when(pl.program_id(0) == 0)
    def _():
        s_ref[...] = jnp.zeros_like(s_ref)

    def lane_group(cnt):
        return lax.broadcasted_iota(I32, (cnt, SUBLANES, LANES), 2) // WKV_LANE_GROUP

    def to_scan(x_ref, t0, cnt):
        grp = lane_group(cnt)
        rows = [x_ref[b, pl.ds(t0, cnt)].astype(F32) for b in range(nb)]
        pieces = []
        for n4 in range(nb):
            for v in range(halves):
                acc = None
                for b in range(nb):
                    x = rows[b][:, SUBLANES * v:SUBLANES * (v + 1), :]
                    shift = (WKV_LANE_GROUP * (b - n4)) % LANES
                    if shift:
                        x = pltpu.roll(x, shift, axis=2)
                    acc = x if acc is None else jnp.where(grp == b, x, acc)
                pieces.append(acc)
        return pieces

    def prepare(t0, cnt):
        rs, ks, vs, ws, as_, gs = (to_scan(x, t0, cnt) for x in (r_ref, k_ref, v_ref, w_ref, a_ref, g_ref))
        kks = [ks[u] * kkp_ref[SUBLANES * u:SUBLANES * (u + 1), :] for u in range(len(ks))]
        ss = kks[0] * kks[0]
        for u in range(1, len(kks)):
            ss = ss + kks[u] * kks[u]
        inv_norm = 1.0 / jnp.maximum(jnp.sqrt(jnp.sum(ss, axis=1, keepdims=True)), 1e-12)
        for u in range(len(ks)):
            rows = slice(SUBLANES * u, SUBLANES * (u + 1))
            at = (pl.ds(t0, cnt), rows, slice(None))
            kk = kks[u] * inv_norm
            wd_s[at] = jnp.exp(-jnp.exp(-_softplus(-ws[u]) - 0.5))
            kk_s[at] = kk
            b_s[at] = kk * as_[u]
            kh_s[at] = ks[u] * (1.0 + (as_[u] - 1.0) * kap_ref[rows, :])
            r_s[at] = rs[u]
            v_s[at] = vs[u]
            g_s[at] = gs[u]

    def finish(t, y):
        inv_n = 1.0 / n
        mean = jnp.sum(y, axis=0, keepdims=True) * inv_n
        yc = y - mean
        var = jnp.sum(yc * yc, axis=0, keepdims=True) * inv_n
        bonus = jnp.sum(r_s[t] * kh_s[t] * rkp_ref[...], axis=0, keepdims=True) * v_s[t]
        o = (yc * lax.rsqrt(var + RWKV_GN_EPS) * lnw_ref[...] + lnb_ref[...] + bonus) * g_s[t]
        grp = lane_group(1)[0]
        for b in range(nb):
            parts = []
            for v in range(halves):
                acc = None
                for n4 in range(nb):
                    u = n4 * halves + v
                    x = o[SUBLANES * u:SUBLANES * (u + 1), :]
                    shift = (WKV_LANE_GROUP * (n4 - b)) % LANES
                    if shift:
                        x = pltpu.roll(x, shift, axis=1)
                    acc = x if acc is None else jnp.where(grp == n4, x, acc)
                parts.append(acc)
            o_ref[b, pl.ds(t, 1)] = jnp.concatenate(parts, axis=0).astype(o_ref.dtype)[None]

    prepare(0, ahead)
    q0 = jnp.zeros((n, LANES), F32)
    for j in range(n):
        q0 = q0 + s_ref[j] * kk_s[0, j:j + 1, :]

    def step(t, q):
        tn = jnp.minimum(t + 1, steps - 1)
        v = v_s[t]
        y = jnp.zeros((n, LANES), F32)
        qn = jnp.zeros((n, LANES), F32)
        for j in range(n):
            s_new = s_ref[j] * wd_s[t, j:j + 1, :] - q * b_s[t, j:j + 1, :] + v * kh_s[t, j:j + 1, :]
            s_ref[j] = s_new
            y = y + s_new * r_s[t, j:j + 1, :]
            qn = qn + s_new * kk_s[tn, j:j + 1, :]
        finish(t, y)
        prepare(jnp.minimum(t + ahead, steps - 1), 1)
        return qn

    lax.fori_loop(0, steps, step, q0)


def _to_scan_order(x, axis):
    axis = axis % x.ndim
    shp = x.shape
    split = shp[:axis] + (RWKV_HEADS, LANES // WKV_LANE_GROUP, WKV_ROWS_PER_BATCH) + shp[axis + 1:]
    order = tuple(range(axis)) + (axis + 2, axis + 1, axis) + tuple(range(axis + 3, x.ndim + 2))
    return x.reshape(split).transpose(order).reshape(shp)


def _scan_tile(p, bsz):
    return jnp.tile(p.reshape(RWKV_HEADS, RWKV_HEAD_DIM).T, (1, bsz))


def wkv_fused(r, k, v, w_raw, a_sig, g, k_k, k_a, r_k, ln_w, ln_b, bsz, seq):
    assert bsz * RWKV_HEADS == LANES and RWKV_HEADS == WKV_LANE_GROUP
    steps = min(WKV_T, seq)
    view = lambda x: x.reshape(bsz, seq, WKV_ROWS_PER_BATCH, LANES)
    blk = pl.BlockSpec((bsz, steps, WKV_ROWS_PER_BATCH, LANES), lambda i: (0, i, 0, 0))
    par = pl.BlockSpec((RWKV_HEAD_DIM, LANES), lambda i: (0, 0))
    scr = pltpu.VMEM((steps, RWKV_HEAD_DIM, LANES), F32)
    out = pl.pallas_call(
        functools.partial(_wkv_fused_kernel, steps=steps),
        grid=(seq // steps,),
        in_specs=[blk] * 6 + [par] * 5,
        out_specs=blk,
        out_shape=jax.ShapeDtypeStruct((bsz, seq, WKV_ROWS_PER_BATCH, LANES), BF16),
        scratch_shapes=[pltpu.VMEM((RWKV_HEAD_DIM, RWKV_HEAD_DIM, LANES), F32)] + [scr] * 7,
        compiler_params=_params("arbitrary"),
        name="wkv_fused",
    )(view(r), view(k), view(v), view(w_raw), view(a_sig), view(g),
      _scan_tile(k_k, bsz), _scan_tile(k_a, bsz), _scan_tile(r_k, bsz), _scan_tile(ln_w, bsz), _scan_tile(ln_b, bsz))
    return out.reshape(bsz * seq, D_MODEL)


def _router_kernel(h_ref, g_ref, wr_ref, info_ref, tile_cnt_ref, total_ref, carry_ref, *, tt):
    E = N_EXPERTS
    i = pl.program_id(0)

    @pl.when(i == 0)
    def _():
        carry_ref[...] = jnp.zeros_like(carry_ref)

    xh, xm, xl = _split3(_rms(h_ref[...]) * g_ref[...])
    wh, wm, wl = _split3(wr_ref[...])
    logits = (_dot_nt(wh, xh) + _dot_nt(wh, xm) + _dot_nt(wm, xh)
              + _dot_nt(wh, xl) + _dot_nt(wl, xh) + _dot_nt(wm, xm))
    eidx = lax.broadcasted_iota(I32, (E, tt), 0)
    v1 = jnp.max(logits, axis=0, keepdims=True)
    i1 = jnp.min(jnp.where(logits == v1, eidx, E), axis=0, keepdims=True)
    rest = jnp.where(eidx == i1, -jnp.inf, logits)
    v2 = jnp.max(rest, axis=0, keepdims=True)
    i2 = jnp.min(jnp.where(rest == v2, eidx, E), axis=0, keepdims=True)
    e2 = jnp.exp(v2 - v1)
    g1 = 1.0 / (1.0 + e2)
    g2 = e2 / (1.0 + e2)
    hit1 = eidx == i1
    hit2 = eidx == i2
    onehot = jnp.where(hit1 | hit2, 1.0, 0.0)
    rr = lax.broadcasted_iota(I32, (tt, tt), 0)
    cc = lax.broadcasted_iota(I32, (tt, tt), 1)
    before = jnp.where(rr < cc, 1.0, 0.0).astype(BF16)
    carry = carry_ref[...]
    carry_wide = jnp.concatenate([carry] * (tt // LANES), axis=1)
    cum = _dot(onehot.astype(BF16), before) + carry_wide
    rank1 = jnp.sum(jnp.where(hit1, cum, 0.0), axis=0, keepdims=True)
    rank2 = jnp.sum(jnp.where(hit2, cum, 0.0), axis=0, keepdims=True)
    zero = jnp.zeros((1, tt), F32)
    info_ref[...] = jnp.concatenate([i1.astype(F32), i2.astype(F32), rank1, rank2, g1, g2, zero, zero], axis=0)
    tile_cnt_ref[0] = carry
    carry = carry + _dot(onehot.astype(BF16), jnp.ones((tt, LANES), BF16))
    carry_ref[...] = carry
    total_ref[...] = carry


def moe_router(h, norm_gain, w_router):
    m, d = h.shape
    tt = min(MOE_TILE, m)
    nt = m // tt
    return pl.pallas_call(
        functools.partial(_router_kernel, tt=tt),
        grid=(nt,),
        in_specs=[pl.BlockSpec((tt, d), lambda i: (i, 0)), pl.BlockSpec((1, d), lambda i: (0, 0)),
                  pl.BlockSpec((N_EXPERTS, d), lambda i: (0, 0))],
        out_specs=[pl.BlockSpec((SUBLANES, tt), lambda i: (0, i)),
                   pl.BlockSpec((1, N_EXPERTS, LANES), lambda i: (i, 0, 0)),
                   pl.BlockSpec((N_EXPERTS, LANES), lambda i: (0, 0))],
        out_shape=[jax.ShapeDtypeStruct((SUBLANES, m), F32), jax.ShapeDtypeStruct((nt, N_EXPERTS, LANES), F32),
                   jax.ShapeDtypeStruct((N_EXPERTS, LANES), F32)],
        scratch_shapes=[pltpu.VMEM((N_EXPERTS, LANES), F32)],
        compiler_params=_params("arbitrary"),
        name="moe_router",
    )(h, norm_gain.reshape(1, d), w_router.T)


def _slots_kernel(start_ref, info_ref, o_ref):
    info = info_ref[...]
    e1, e2, r1, r2 = info[0:1], info[1:2], info[2:3], info[3:4]
    s1 = r1
    s2 = r2
    for e in range(N_EXPERTS):
        st = start_ref[e].astype(F32)
        s1 = s1 + jnp.where(e1 == float(e), st, 0.0)
        s2 = s2 + jnp.where(e2 == float(e), st, 0.0)
    o_ref[...] = jnp.concatenate([s1, s2, info[4:5], info[5:6], info[6:8], info[6:8]], axis=0)


def moe_slots(info, slot_start):
    m = info.shape[1]
    tt = min(2048, m)
    return pl.pallas_call(
        _slots_kernel,
        grid_spec=pltpu.PrefetchScalarGridSpec(
            num_scalar_prefetch=1, grid=(m // tt,),
            in_specs=[pl.BlockSpec((SUBLANES, tt), lambda i, st: (0, i))],
            out_specs=pl.BlockSpec((SUBLANES, tt), lambda i, st: (0, i))),
        out_shape=jax.ShapeDtypeStruct((SUBLANES, m), F32),
        compiler_params=_params("parallel"),
        name="moe_slots",
    )(slot_start, info)


def _dispatch_kernel(blk_ref, til_ref, flg_ref, slots_ref, x_ref, o_ref, go_ref, *, bs):
    w = pl.program_id(0)
    flags = flg_ref[w]
    first = (flags & 1) != 0
    real = (flags & 4) != 0

    def gathered():
        sl = slots_ref[...]
        sid = (blk_ref[w] * bs + lax.broadcasted_iota(I32, (bs, sl.shape[1]), 0)).astype(F32)
        hit1 = sid == sl[0:1, :]
        hit2 = sid == sl[1:2, :]
        onehot = jnp.where(hit1 | hit2, 1.0, 0.0).astype(BF16)
        gate = jnp.sum(jnp.where(hit1, sl[2:3, :], 0.0) + jnp.where(hit2, sl[3:4, :], 0.0), axis=1, keepdims=True)
        return _dot(onehot, x_ref[...]).astype(o_ref.dtype), jnp.broadcast_to(gate, go_ref.shape)

    @pl.when(first & real)
    def _():
        o_ref[...], go_ref[...] = gathered()

    @pl.when(first & jnp.logical_not(real))
    def _():
        o_ref[...] = jnp.zeros_like(o_ref)
        go_ref[...] = jnp.zeros_like(go_ref)

    @pl.when(jnp.logical_not(first) & real)
    def _():
        rows, gate = gathered()
        o_ref[...] += rows
        go_ref[...] += gate


def moe_dispatch(xn, slots, wl_blk, wl_til, wl_flg, n_slots):
    m, d = xn.shape
    bs = tt = MOE_TILE
    n_items = wl_blk.shape[0]
    return pl.pallas_call(
        functools.partial(_dispatch_kernel, bs=bs),
        grid_spec=pltpu.PrefetchScalarGridSpec(
            num_scalar_prefetch=3, grid=(n_items,),
            in_specs=[pl.BlockSpec((SUBLANES, tt), lambda w, b, t, f: (0, t[w])),
                      pl.BlockSpec((tt, d), lambda w, b, t, f: (t[w], 0))],
            out_specs=[pl.BlockSpec((bs, d), lambda w, b, t, f: (b[w], 0)),
                       pl.BlockSpec((bs, LANES), lambda w, b, t, f: (b[w], 0))]),
        out_shape=[jax.ShapeDtypeStruct((n_slots, d), BF16), jax.ShapeDtypeStruct((n_slots, LANES), F32)],
        compiler_params=_params("arbitrary"),
        name="moe_dispatch",
    )(wl_blk, wl_til, wl_flg, slots, xn)


def _moe_ffn_kernel(be_ref, rows_ref, x_ref, gate_ref, w1_ref, w3_ref, w2_ref, o_ref, acc_ref, *, sub):
    b = pl.program_id(0)
    f = pl.program_id(1)
    bm = x_ref.shape[0]

    @pl.when(f == 0)
    def _():
        acc_ref[...] = jnp.zeros_like(acc_ref)

    for rows in range(sub, bm + 1, sub):
        @pl.when(rows_ref[b] == rows)
        def _(rows=rows):
            x = x_ref[0:rows, :]
            h = _silu(_dot(x, w1_ref[0].astype(BF16))) * _dot(x, w3_ref[0].astype(BF16))
            acc_ref[0:rows, :] += _dot(h.astype(BF16), w2_ref[0].astype(BF16))

    @pl.when(f == pl.num_programs(1) - 1)
    def _():
        o_ref[...] = (acc_ref[...] * gate_ref[:, 0:1]).astype(o_ref.dtype)


def moe_ffn(xs, slot_gate, w1, w3, w2, block_expert, block_rows):
    n_slots, d = xs.shape
    ff = w1.shape[2]
    bm = MOE_FFN_ROWS
    tf = MOE_TF
    nf = ff // tf

    def f_of(b, f, rows):
        return jnp.where(rows[b] > 0, f, nf - 1)

    return pl.pallas_call(
        functools.partial(_moe_ffn_kernel, sub=MOE_ROW_GROUP),
        grid_spec=pltpu.PrefetchScalarGridSpec(
            num_scalar_prefetch=2, grid=(n_slots // bm, nf),
            in_specs=[pl.BlockSpec((bm, d), lambda b, f, be, nb: (b, 0)),
                      pl.BlockSpec((bm, LANES), lambda b, f, be, nb: (b, 0)),
                      pl.BlockSpec((1, d, tf), lambda b, f, be, nb: (be[b], 0, f_of(b, f, nb))),
                      pl.BlockSpec((1, d, tf), lambda b, f, be, nb: (be[b], 0, f_of(b, f, nb))),
                      pl.BlockSpec((1, tf, d), lambda b, f, be, nb: (be[b], f_of(b, f, nb), 0))],
            out_specs=pl.BlockSpec((bm, d), lambda b, f, be, nb: (b, 0)),
            scratch_shapes=[pltpu.VMEM((bm, d), F32)]),
        out_shape=jax.ShapeDtypeStruct((n_slots, d), BF16),
        compiler_params=_params("arbitrary", "arbitrary"),
        name="moe_ffn",
    )(block_expert, block_rows, xs, slot_gate, w1, w3, w2)


def _combine_kernel(til_ref, blk_ref, flg_ref, tok_ref, y_ref, res_ref, o_ref, *, bs):
    w = pl.program_id(0)
    flags = flg_ref[w]

    @pl.when((flags & 1) != 0)
    def _():
        o_ref[...] = res_ref[...]

    @pl.when((flags & 4) != 0)
    def _():
        tok = tok_ref[...]
        tt = tok.shape[0]
        sid = (blk_ref[w] * bs + lax.broadcasted_iota(I32, (tt, bs), 1)).astype(F32)
        onehot = jnp.where((sid == tok[:, 0:1]) | (sid == tok[:, 1:2]), 1.0, 0.0).astype(BF16)
        o_ref[...] += _dot(onehot, y_ref[...])


def moe_combine(y, tok_info, residual, wl_til, wl_blk, wl_flg):
    m, d = residual.shape
    bs = tt = MOE_TILE
    n_items = wl_til.shape[0]
    return pl.pallas_call(
        functools.partial(_combine_kernel, bs=bs),
        grid_spec=pltpu.PrefetchScalarGridSpec(
            num_scalar_prefetch=3, grid=(n_items,),
            in_specs=[pl.BlockSpec((tt, SUBLANES), lambda w, t, b, f: (t[w], 0)),
                      pl.BlockSpec((bs, d), lambda w, t, b, f: (b[w], 0)),
                      pl.BlockSpec((tt, d), lambda w, t, b, f: (t[w], 0))],
            out_specs=pl.BlockSpec((tt, d), lambda w, t, b, f: (t[w], 0))),
        out_shape=jax.ShapeDtypeStruct((m, d), F32),
        compiler_params=_params("arbitrary"),
        name="moe_combine",
    )(wl_til, wl_blk, wl_flg, tok_info, y, residual)


def _work_list(overlap, n_items):
    n_major, n_minor = overlap.shape
    has = jnp.any(overlap, axis=1)
    marked = jnp.concatenate([overlap[:, :1] | ~has[:, None], overlap[:, 1:]], axis=1)
    maj, mino = jnp.nonzero(marked, size=n_items, fill_value=-1)
    valid = maj >= 0
    real = valid & overlap[jnp.maximum(maj, 0), jnp.maximum(mino, 0)]
    prev = jnp.concatenate([jnp.full((1,), -2, maj.dtype), maj[:-1]])
    nxt = jnp.concatenate([maj[1:], jnp.full((1,), -2, maj.dtype)])
    flags = (valid & (maj != prev)) * 1 + (valid & (maj != nxt)) * 2 + real * 4
    last_maj = jnp.max(jnp.where(valid, maj, 0))
    last_min = jnp.sum(jnp.where(valid & (maj == last_maj) & (maj != nxt), mino, 0))
    maj = jnp.where(valid, maj, last_maj)
    mino = jnp.where(valid, mino, last_min)
    return maj.astype(I32), mino.astype(I32), flags.astype(I32)


def moe_layer(h, norm_gain, w_router, w1, w3, w2):
    m, d = h.shape
    E = N_EXPERTS
    tile = min(MOE_TILE, m)
    n_tiles = m // tile
    ffn_rows = MOE_FFN_ROWS
    xn = rmsnorm(h, norm_gain, BF16)
    info, tile_cnt, total = moe_router(h, norm_gain, w_router)
    before = tile_cnt[:, :, 0]
    counts = total[:, 0].astype(I32)
    padded = (counts + ffn_rows - 1) // ffn_rows * ffn_rows
    ends = jnp.cumsum(padded)
    starts = ends - padded
    n_slots = (m * TOP_K // ffn_rows + E) * ffn_rows
    n_ffn_blocks = n_slots // ffn_rows
    n_dblocks = n_slots // tile
    slots = moe_slots(info, starts.astype(I32))
    db_start = jnp.arange(n_dblocks, dtype=I32) * tile
    db_e = jnp.minimum(jnp.sum(db_start[:, None] >= ends[None, :], axis=1), E - 1)
    r0 = db_start - starts[db_e]
    r1 = jnp.minimum(r0 + tile, counts[db_e])
    after = jnp.concatenate([before[1:], counts[None, :].astype(F32)], axis=0).astype(I32)
    bef = before.astype(I32)
    ov = (bef[:, db_e].T < r1[:, None]) & (after[:, db_e].T > r0[:, None]) & (db_start < ends[-1])[:, None]
    n_items = n_dblocks + E * n_tiles
    d_blk, d_til, d_flg = _work_list(ov, n_items)
    c_til, c_blk, c_flg = _work_list(ov.T, n_items)
    xs, slot_gate = moe_dispatch(xn, slots, d_blk, d_til, d_flg, n_slots)
    fb_start = jnp.arange(n_ffn_blocks, dtype=I32) * ffn_rows
    block_expert = jnp.minimum(jnp.sum(fb_start[:, None] >= ends[None, :], axis=1), E - 1).astype(I32)
    filled = counts[block_expert] - (fb_start - starts[block_expert])
    grp = MOE_ROW_GROUP
    block_rows = jnp.where(fb_start < ends[-1], (jnp.clip(filled, 0, ffn_rows) + grp - 1) // grp * grp, 0).astype(I32)
    ys = moe_ffn(xs, slot_gate, w1, w3, w2, block_expert, block_rows)
    return moe_combine(ys, slots.T, h, c_til, c_blk, c_flg)


def even_mixer_layer(h, posf, p, bsz, seq):
    m = bsz * seq
    hn = rmsnorm(h, p["norm_mix"], BF16)
    w_in = p["w_in"]
    c = np.cumsum([0, D_SSM, D_CONV, SSM_HEADS, D_NSA] + [D_NSA_KV] * 6 + [3 * NSA_HEADS])
    w_proj = jnp.concatenate([w_in[:, c[1]:c[2]], w_in[:, c[4]:c[10]], w_in[:, c[0]:c[1]], w_in[:, c[3]:c[4]]], axis=1)
    proj = matmul([(hn, w_proj.astype(BF16))], out_dtype=BF16)
    kv0 = D_CONV
    w_small = jnp.concatenate([w_in[:, c[2]:c[3]], w_in[:, c[10]:c[11]]], axis=1)
    w_small = jnp.pad(w_small, ((0, 0), (0, LANES - w_small.shape[1]))).astype(BF16)
    small = matmul([(hn, w_small)])
    y_ssm = ssd_group(proj, (D_CONV + 6 * D_NSA_KV) // D_SSM, 0, small, p["conv_w"], p["conv_b"], p["dt_bias"],
                      p["a_log"], p["d_skip"], p["ssm_norm"], bsz, seq)
    qn, ksn, kwn = rope_norm(posf, proj, (D_CONV + 6 * D_NSA_KV + D_SSM) // D_NSA, kv0 // (6 * D_NSA_KV),
                             p["q_gain"], p["ks_gain"], p["kw_gain"])
    nch = seq // CMP_STRIDE

    def chunks(col0):
        u = proj[:, kv0 + col0:kv0 + col0 + D_NSA_KV].reshape(bsz, nch, CMP_STRIDE, NSA_KV_HEADS, NSA_HEAD_DIM)
        return u.transpose(0, 3, 1, 2, 4).reshape(bsz, NSA_KV_HEADS, nch, CMP_STRIDE * NSA_HEAD_DIM)

    kc, vc = compress(chunks(0), chunks(D_NSA_KV), p["pe_k"], p["pe_v"], p["wk1"], p["wk2"], p["wv1"], p["wv2"],
                      p["kc_gain"])
    o_cmp, selt = cmp_select(qn, kc, vc, bsz, seq)
    def v_tiles(col0, tk):
        tk = min(tk, seq)
        u = proj[:, kv0 + col0:kv0 + col0 + D_NSA_KV].reshape(bsz, seq // tk, tk, NSA_KV_HEADS, NSA_HEAD_DIM)
        return u.transpose(0, 3, 1, 4, 2)

    o_sel = masked_attention(qn, ksn, 0, v_tiles(3 * D_NSA_KV, ATT_TK), bsz, seq, "sel", selt)
    o_win = masked_attention(qn, kwn, 0, v_tiles(5 * D_NSA_KV, WIN_TK), bsz, seq, "win")
    y_nsa = nsa_combine(small, o_cmp, o_sel, o_win)
    w_out = p["w_out"].astype(BF16)
    return matmul([(y_ssm, w_out[:D_SSM]), (y_nsa, w_out[D_SSM:])], residual=h)


def rwkv_layer(h, p, bsz, seq):
    m, d = h.shape
    mixes = rwkv_mix(h, p["norm_mix"], p["mu"], seq)
    bf = lambda w: w.astype(BF16)

    def lora_pad(w_a, w_b):
        r = w_a.shape[1]
        rp = -(-r // LANES) * LANES
        return bf(jnp.pad(w_a, ((0, 0), (0, rp - r)))), bf(jnp.pad(w_b, ((0, rp - r), (0, 0))))

    so = _to_scan_order
    r = matmul([(mixes[0], bf(so(p["w_r"], 1)))], out_dtype=BF16)
    k = matmul([(mixes[2], bf(so(p["w_k"], 1)))], out_dtype=BF16)
    v = matmul([(mixes[3], bf(so(p["w_v"], 1)))], out_dtype=BF16)
    w1, w2 = lora_pad(p["w1"], so(p["w2"], 1))
    w_raw = lora(mixes[1], w1, w2, so(p["w0"], 0), "tanh", None, F32)
    a1, a2 = lora_pad(p["a1"], so(p["a2"], 1))
    a_sig = lora(mixes[4], a1, a2, so(p["a0"], 0), None, "sigmoid", BF16)
    g1, g2 = lora_pad(p["g1"], so(p["g2"], 1))
    g = lora(mixes[5], g1, g2, jnp.zeros((d,), F32), "sigmoid", None, BF16)
    o = wkv_fused(r, k, v, w_raw, a_sig, g, p["k_k"], p["k_a"], p["r_k"].reshape(-1), p["ln_w"], p["ln_b"], bsz, seq)
    return matmul([(o, bf(so(p["w_o"], 0)))], residual=h)


def kernel(x, mem, positions, norm_mix, norm_xattn, norm_mem, norm_ffn, xattn_wq, xattn_wkv, xattn_wo, xattn_q_gain, xattn_k_gain, ev_w_in, ev_conv_w, ev_conv_b, ev_dt_bias, ev_a_log, ev_d_skip, ev_ssm_norm, ev_q_gain, ev_kc_gain, ev_ks_gain, ev_kw_gain, ev_pe_k, ev_pe_v, ev_cmp_wk1, ev_cmp_wk2, ev_cmp_wv1, ev_cmp_wv2, ev_w_out, ev_ffn_w1, ev_ffn_w3, ev_ffn_w2, od_mu, od_w_r, od_w_k, od_w_v, od_w_o, od_w0, od_w1, od_w2, od_a0, od_a1, od_a2, od_g1, od_g2, od_k_k, od_k_a, od_r_k, od_ln_w, od_ln_b, od_router, od_moe_w1, od_moe_w3, od_moe_w2):
    bsz, seq, d = x.shape
    m = bsz * seq
    depth = norm_mix.shape[0]
    h = x.reshape(m, d)
    mem2d = mem.reshape(-1, d)
    posf = positions.astype(F32).reshape(m, 1)
    for layer in range(depth):
        i = layer // 2
        if layer % 2 == 0:
            p = dict(norm_mix=norm_mix[layer], w_in=ev_w_in[i], conv_w=ev_conv_w[i], conv_b=ev_conv_b[i],
                     dt_bias=ev_dt_bias[i], a_log=ev_a_log[i], d_skip=ev_d_skip[i], ssm_norm=ev_ssm_norm[i],
                     q_gain=ev_q_gain[i], kc_gain=ev_kc_gain[i], ks_gain=ev_ks_gain[i], kw_gain=ev_kw_gain[i],
                     pe_k=ev_pe_k[i], pe_v=ev_pe_v[i], wk1=ev_cmp_wk1[i], wk2=ev_cmp_wk2[i], wv1=ev_cmp_wv1[i],
                     wv2=ev_cmp_wv2[i], w_out=ev_w_out[i])
            h = even_mixer_layer(h, posf, p, bsz, seq)
        else:
            p = dict(norm_mix=norm_mix[layer], mu=od_mu[i], w_r=od_w_r[i], w_k=od_w_k[i], w_v=od_w_v[i], w_o=od_w_o[i],
                     w0=od_w0[i], w1=od_w1[i], w2=od_w2[i], a0=od_a0[i], a1=od_a1[i], a2=od_a2[i], g1=od_g1[i],
                     g2=od_g2[i], k_k=od_k_k[i], k_a=od_k_a[i], r_k=od_r_k[i], ln_w=od_ln_w[i], ln_b=od_ln_b[i])
            h = rwkv_layer(h, p, bsz, seq)
        h = memory_xattn(h, mem2d, norm_xattn[layer], norm_mem[layer], xattn_wq[layer], xattn_wkv[layer],
                         xattn_wo[layer], xattn_q_gain[layer], xattn_k_gain[layer], bsz, seq)
        if layer % 2 == 0:
            h = swiglu_ffn(h, norm_ffn[layer], ev_ffn_w1[i].astype(BF16), ev_ffn_w3[i].astype(BF16),
                           ev_ffn_w2[i].astype(BF16))
        else:
            h = moe_layer(h, norm_ffn[layer], od_router[i], od_moe_w1[i], od_moe_w3[i], od_moe_w2[i])
    return h.reshape(bsz, seq, d)
```

```python
import functools
import math

import numpy as np
import jax
import jax.numpy as jnp
from jax import lax
from jax.experimental import pallas as pl
from jax.experimental.pallas import tpu as pltpu

F32 = jnp.float32
BF16 = jnp.bfloat16
I32 = jnp.int32

D_MODEL = 2048
NORM_EPS = 1e-6
NEG_BIG = -1e30

SSM_HEADS = 32
SSM_HEAD_DIM = 64
D_SSM = SSM_HEADS * SSM_HEAD_DIM
SSM_GROUPS = 4
D_STATE = 128
CONV_WIDTH = 4
SSD_CHUNK = 128
D_CONV = D_SSM + 2 * SSM_GROUPS * D_STATE
D_SSM_GROUP = D_SSM // SSM_GROUPS
SSD_CONV_TAIL = 16
SSD_EXT_ROWS = 256

NSA_HEADS = 16
NSA_KV_HEADS = 4
NSA_HEAD_DIM = 128
NSA_Q_PER_KV = NSA_HEADS // NSA_KV_HEADS
D_NSA = NSA_HEADS * NSA_HEAD_DIM
D_NSA_KV = NSA_KV_HEADS * NSA_HEAD_DIM
CMP_BLOCK = 32
CMP_STRIDE = 16
CMP_HIDDEN = 256
SEL_BLOCK = 64
N_SELECT = 16
WINDOW = 512
ROPE_DIM = NSA_HEAD_DIM // 4
ROPE_THETA = 500000.0

RWKV_HEAD_DIM = 64
RWKV_HEADS = D_MODEL // RWKV_HEAD_DIM
RWKV_GN_EPS = 1e-5 * RWKV_HEAD_DIM

XATTN_HEADS = 4
XATTN_HEAD_DIM = 128
D_XATTN = XATTN_HEADS * XATTN_HEAD_DIM

N_EXPERTS = 8
TOP_K = 2

LANES = 128
SUBLANES = 8
VMEM_LIMIT_BYTES = 56 * 1024 * 1024

ROW_TILE = 512
MM_TM = 2048
MM_TN = 512
MM_VMEM_BUDGET = 44 * 1024 * 1024
FFN_TM = 512
FFN_TF = 512
FFN_SPLIT = 256
MOE_FFN_ROWS = 1024
MOE_TF = 256
MOE_ROW_GROUP = 256
COMBINE_TM = 512
MOE_TILE = 512
MOE_SLOT_BLOCK = 256
ATT_TQ = 512
ATT_TK = 512
WIN_TK = 512
LOG2E = 1.4426950408889634
CMP_TQ = 512
WKV_T = 64
WKV_LOOKAHEAD = 8


def _params(*sem):
    return pltpu.CompilerParams(dimension_semantics=sem, vmem_limit_bytes=VMEM_LIMIT_BYTES)


def _sigmoid(x):
    return 1.0 / (1.0 + jnp.exp(-x))


def _silu(x):
    return x * _sigmoid(x)


def _softplus(x):
    return jnp.maximum(x, 0.0) + jnp.log(1.0 + jnp.exp(-jnp.abs(x)))


def _split3(x):
    h = x.astype(BF16)
    r = x - h.astype(F32)
    m = r.astype(BF16)
    l = (r - m.astype(F32)).astype(BF16)
    return h, m, l


def _dot(a, b):
    return jnp.dot(a, b, preferred_element_type=F32)


def _dot_nt(a, b):
    return lax.dot_general(a, b, (((1,), (1,)), ((), ())), preferred_element_type=F32)


def _dot_f32_sel(x, e):
    h, m, l = _split3(x)
    return _dot(h, e) + _dot(m, e) + _dot(l, e)


def _sel_dot_f32(e, x):
    h, m, l = _split3(x)
    return _dot(e, h) + _dot(e, m) + _dot(e, l)


def _rms(x, eps=NORM_EPS):
    return x * lax.rsqrt(jnp.mean(x * x, axis=-1, keepdims=True) + eps)


def _rmsnorm_kernel(x_ref, g_ref, o_ref):
    o_ref[...] = (_rms(x_ref[...]) * g_ref[...]).astype(o_ref.dtype)


def rmsnorm(x, gain, out_dtype):
    m, d = x.shape
    tm = min(ROW_TILE, m)
    return pl.pallas_call(
        _rmsnorm_kernel,
        grid=(m // tm,),
        in_specs=[pl.BlockSpec((tm, d), lambda i: (i, 0)), pl.BlockSpec((1, d), lambda i: (0, 0))],
        out_specs=pl.BlockSpec((tm, d), lambda i: (i, 0)),
        out_shape=jax.ShapeDtypeStruct((m, d), out_dtype),
        compiler_params=_params("parallel"),
        name="rmsnorm",
    )(x, gain.reshape(1, d))


def _mm_kernel(*refs, n_pairs, has_res):
    o_ref = refs[-1]
    acc = None
    for p in range(n_pairs):
        d = _dot(refs[2 * p][...], refs[2 * p + 1][...])
        acc = d if acc is None else acc + d
    if has_res:
        acc = acc + refs[2 * n_pairs][...]
    o_ref[...] = acc.astype(o_ref.dtype)


def matmul(pairs, residual=None, out_dtype=F32):
    m = pairs[0][0].shape[0]
    n = pairs[0][1].shape[1]
    k_total = sum(a.shape[1] for a, _ in pairs)
    tm = min(MM_TM if k_total <= D_MODEL else MM_TM // 2, m)
    out_bytes = jnp.dtype(out_dtype).itemsize + (4 if residual is not None else 0)

    def vmem_bytes(tn):
        return 2 * (tm * k_total * 2 + k_total * tn * 2 + tm * tn * out_bytes)

    tn = next((t for t in (2 * MM_TN, MM_TN, 256, LANES) if n % t == 0 and vmem_bytes(t) <= MM_VMEM_BUDGET), LANES)
    in_specs, args = [], []
    for a, w in pairs:
        kk = a.shape[1]
        in_specs += [pl.BlockSpec((tm, kk), lambda i, j: (i, 0)), pl.BlockSpec((kk, tn), lambda i, j: (0, j))]
        args += [a, w]
    if residual is not None:
        in_specs.append(pl.BlockSpec((tm, tn), lambda i, j: (i, j)))
        args.append(residual)
    kern = functools.partial(_mm_kernel, n_pairs=len(pairs), has_res=residual is not None)
    return pl.pallas_call(
        kern,
        grid=(m // tm, n // tn),
        in_specs=in_specs,
        out_specs=pl.BlockSpec((tm, tn), lambda i, j: (i, j)),
        out_shape=jax.ShapeDtypeStruct((m, n), out_dtype),
        compiler_params=_params("parallel", "parallel"),
        name="matmul",
    )(*args)


def _act(x, act):
    if act == "tanh":
        return jnp.tanh(x)
    if act == "sigmoid":
        return _sigmoid(x)
    return x


def _lora_kernel(a_ref, wa_ref, wb_ref, bias_ref, o_ref, *, act_mid, act_out):
    mid = _act(_dot(a_ref[...], wa_ref[...]), act_mid).astype(BF16)
    o_ref[...] = _act(_dot(mid, wb_ref[...]) + bias_ref[...], act_out).astype(o_ref.dtype)


def lora(a, w_a, w_b, bias, act_mid, act_out, out_dtype):
    m, d = a.shape
    r = w_a.shape[1]
    n = w_b.shape[1]
    tm = min(2 * ROW_TILE, m)
    return pl.pallas_call(
        functools.partial(_lora_kernel, act_mid=act_mid, act_out=act_out),
        grid=(m // tm,),
        in_specs=[pl.BlockSpec((tm, d), lambda i: (i, 0)), pl.BlockSpec((d, r), lambda i: (0, 0)),
                  pl.BlockSpec((r, n), lambda i: (0, 0)), pl.BlockSpec((1, n), lambda i: (0, 0))],
        out_specs=pl.BlockSpec((tm, n), lambda i: (i, 0)),
        out_shape=jax.ShapeDtypeStruct((m, n), out_dtype),
        compiler_params=_params("parallel"),
        name="lora",
    )(a, w_a, w_b, bias.reshape(1, n).astype(F32))


def _ffn_kernel(g_ref, w1_ref, w3_ref, w2_ref, res_ref, o_ref, x_ref):
    f = pl.program_id(1)

    @pl.when(f == 0)
    def _():
        res = res_ref[...]
        o_ref[...] = res
        x_ref[...] = (_rms(res) * g_ref[...]).astype(x_ref.dtype)

    x = x_ref[...]
    tf = w1_ref.shape[1]
    cols = [slice(c, c + FFN_SPLIT) for c in range(0, tf, FFN_SPLIT)]
    gate_up = [(_dot(x, w1_ref[:, c]), _dot(x, w3_ref[:, c])) for c in cols]
    hs = [(_silu(g) * u).astype(BF16) for g, u in gate_up]
    acc = _dot(hs[0], w2_ref[cols[0], :])
    for h, c in zip(hs[1:], cols[1:]):
        acc = acc + _dot(h, w2_ref[c, :])
    o_ref[...] += acc


def swiglu_ffn(residual, norm_gain, w1, w3, w2):
    m, d = residual.shape
    ff = w1.shape[1]
    tm = min(FFN_TM, m)
    tf = FFN_TF
    return pl.pallas_call(
        _ffn_kernel,
        grid=(m // tm, ff // tf),
        in_specs=[pl.BlockSpec((1, d), lambda i, f: (0, 0)),
                  pl.BlockSpec((d, tf), lambda i, f: (0, f)),
                  pl.BlockSpec((d, tf), lambda i, f: (0, f)),
                  pl.BlockSpec((tf, d), lambda i, f: (f, 0)),
                  pl.BlockSpec((tm, d), lambda i, f: (i, 0))],
        out_specs=pl.BlockSpec((tm, d), lambda i, f: (i, 0)),
        out_shape=jax.ShapeDtypeStruct((m, d), F32),
        scratch_shapes=[pltpu.VMEM((tm, d), BF16)],
        compiler_params=_params("parallel", "arbitrary"),
        name="swiglu_ffn",
    )(norm_gain.reshape(1, d), w1, w3, w2, residual)


def _ssd_kernel(z_ref, xbc_ref, dt_ref, shift_ref, cw_ref, cb_ref, dtb_ref, alog_ref, dskip_ref, nw_ref, eh_ref,
                o_ref, ext_ref, st_ref):
    L = SSD_CHUNK
    G = SSM_GROUPS
    GW = D_SSM_GROUP
    T = SSD_CONV_TAIL
    c = pl.program_id(1)

    @pl.when(c == 0)
    def _():
        ext_ref[...] = jnp.zeros_like(ext_ref)
        st_ref[...] = jnp.zeros_like(st_ref)

    ext_ref[0:T, :] = ext_ref[L:L + T, :]
    ext_ref[T:L + T, :] = xbc_ref[...]
    shifted = _dot(shift_ref[...], ext_ref[...])
    acc = jnp.zeros((L, D_CONV), F32) + cb_ref[...]
    for k in range(CONV_WIDTH):
        acc = acc + cw_ref[k:k + 1, :] * shifted[k * L:(k + 1) * L, :]
    xc = _silu(acc)

    dt = _softplus(dt_ref[...] + dtb_ref[...])
    a = -jnp.exp(alog_ref[...])
    adt = dt * a
    row = lax.broadcasted_iota(I32, (L, L), 0)
    col = lax.broadcasted_iota(I32, (L, L), 1)
    causal = row >= col
    tril = jnp.where(causal, 1.0, 0.0).astype(BF16)
    a_cum = _sel_dot_f32(tril, adt)
    a_cum_t = a_cum.T
    eh = eh_ref[...]
    acx = _dot_f32_sel(a_cum, eh)
    dtx = _dot_f32_sel(dt, eh)
    a_end = acx[L - 1:L, :]
    xs = xc[:, :D_SSM]
    xdt = xs * dtx
    xdte = (xdt * jnp.exp(a_end - acx)).astype(BF16)
    from_start = jnp.exp(acx)
    lane = lax.broadcasted_iota(I32, (L, LANES), 1)
    lo_half = lane < SSM_HEAD_DIM
    z = z_ref[...].astype(F32)

    for g in range(G):
        bm = xc[:, D_SSM + g * D_STATE:D_SSM + (g + 1) * D_STATE]
        cm = xc[:, D_SSM + G * D_STATE + g * D_STATE:D_SSM + G * D_STATE + (g + 1) * D_STATE]
        bm_b = bm.astype(BF16)
        cm_b = cm.astype(BF16)
        cb = _dot_nt(cm_b, bm_b)
        st = st_ref[g]
        y_off = _dot(cm_b, st.astype(BF16)) * from_start[:, g * GW:(g + 1) * GW]
        pieces = []
        for pp in range(GW // LANES):
            h0 = g * (GW // SSM_HEAD_DIM) + 2 * pp
            ms = []
            for hh in (h0, h0 + 1):
                diff = a_cum[:, hh:hh + 1] - a_cum_t[hh:hh + 1, :]
                dec = jnp.exp(jnp.where(causal, diff, NEG_BIG))
                ms.append((cb * dec).astype(BF16))
            lhs = jnp.concatenate(ms, axis=1)
            c0 = g * GW + pp * LANES
            xp = xdt[:, c0:c0 + LANES]
            rhs = jnp.concatenate([jnp.where(lo_half, xp, 0.0), jnp.where(lo_half, 0.0, xp)], axis=0)
            pieces.append(_dot(lhs, rhs.astype(BF16)))
        y = jnp.concatenate(pieces, axis=1) + y_off
        y = y + xs[:, g * GW:(g + 1) * GW] * dskip_ref[:, g * GW:(g + 1) * GW]
        y = y * _silu(z[:, g * GW:(g + 1) * GW])
        y = _rms(y) * nw_ref[:, g * GW:(g + 1) * GW]
        o_ref[:, g * GW:(g + 1) * GW] = y.astype(o_ref.dtype)
        s_new = _dot(bm.T.astype(BF16), xdte[:, g * GW:(g + 1) * GW])
        st_ref[g] = st * jnp.exp(a_end[:, g * GW:(g + 1) * GW]) + s_new


def ssd_group(proj, z_col, xbc_col, small, conv_w, conv_b, dt_bias, a_log, d_skip, norm_w, bsz, seq):
    m = bsz * seq
    L = SSD_CHUNK
    nc = seq // L
    pad = LANES - SSM_HEADS
    eh = np.zeros((LANES, D_SSM), np.float32)
    for h in range(SSM_HEADS):
        eh[h, h * SSM_HEAD_DIM:(h + 1) * SSM_HEAD_DIM] = 1.0
    shift = np.zeros((CONV_WIDTH * L, SSD_EXT_ROWS), np.float32)
    for k in range(CONV_WIDTH):
        shift[k * L + np.arange(L), np.arange(L) + SSD_CONV_TAIL - (CONV_WIDTH - 1) + k] = 1.0
    row_spec = lambda w, col=0: pl.BlockSpec((L, w), lambda b, c: (b * nc + c, col))
    const = lambda r, w: pl.BlockSpec((r, w), lambda b, c: (0, 0))
    return pl.pallas_call(
        _ssd_kernel,
        grid=(bsz, nc),
        in_specs=[row_spec(D_SSM, z_col), row_spec(D_CONV, xbc_col), row_spec(LANES),
                  const(CONV_WIDTH * L, SSD_EXT_ROWS),
                  const(CONV_WIDTH, D_CONV), const(1, D_CONV), const(1, LANES), const(1, LANES),
                  const(1, D_SSM), const(1, D_SSM), const(LANES, D_SSM)],
        out_specs=row_spec(D_SSM),
        out_shape=jax.ShapeDtypeStruct((m, D_SSM), BF16),
        scratch_shapes=[pltpu.VMEM((SSD_EXT_ROWS, D_CONV), BF16),
                        pltpu.VMEM((SSM_GROUPS, D_STATE, D_SSM_GROUP), F32)],
        compiler_params=_params("parallel", "arbitrary"),
        name="ssd_group",
    )(proj, proj, small, jnp.asarray(shift, BF16), conv_w.T, conv_b.reshape(1, D_CONV),
      jnp.pad(dt_bias, (0, pad)).reshape(1, LANES), jnp.pad(a_log, (0, pad)).reshape(1, LANES),
      jnp.repeat(d_skip, SSM_HEAD_DIM).reshape(1, D_SSM), norm_w.reshape(1, D_SSM), jnp.asarray(eh, BF16))


def _rope_norm_kernel(pos_ref, invf_ref, q_ref, kv_ref, qg_ref, ksg_ref, kwg_ref, qo_ref, kso_ref, kwo_ref):
    ang = pos_ref[...] * invf_ref[...]
    cos = jnp.cos(ang)
    sin = jnp.sin(ang)
    lane = lax.broadcasted_iota(I32, ang.shape, 1)
    first = lane < ROPE_DIM // 2
    sin_signed = jnp.where(first, -sin, sin)

    def head(x, gain):
        y = _rms(x.astype(F32)) * gain
        partner = jnp.where(first, pltpu.roll(y, LANES - ROPE_DIM // 2, axis=1), pltpu.roll(y, ROPE_DIM // 2, axis=1))
        return y * cos + partner * sin_signed

    hd = NSA_HEAD_DIM
    scale = hd ** -0.5 * LOG2E
    for h in range(NSA_HEADS):
        qo_ref[:, h * hd:(h + 1) * hd] = (head(q_ref[:, h * hd:(h + 1) * hd], qg_ref[...]) * scale).astype(qo_ref.dtype)
    for h in range(NSA_KV_HEADS):
        ks = kv_ref[:, 2 * D_NSA_KV + h * hd:2 * D_NSA_KV + (h + 1) * hd]
        kw = kv_ref[:, 4 * D_NSA_KV + h * hd:4 * D_NSA_KV + (h + 1) * hd]
        kso_ref[:, h * hd:(h + 1) * hd] = head(ks, ksg_ref[...]).astype(kso_ref.dtype)
        kwo_ref[:, h * hd:(h + 1) * hd] = head(kw, kwg_ref[...]).astype(kwo_ref.dtype)


def rope_norm(posf, proj, q_col, kv_col, q_gain, ks_gain, kw_gain):
    m = proj.shape[0]
    tm = min(ROW_TILE, m)
    half = ROPE_DIM // 2
    inv = np.exp(-math.log(ROPE_THETA) * np.arange(0, ROPE_DIM, 2, dtype=np.float32) / ROPE_DIM).astype(np.float32)
    invf = np.zeros((1, LANES), np.float32)
    invf[0, :half] = inv
    invf[0, half:ROPE_DIM] = inv
    row = lambda w, col=0: pl.BlockSpec((tm, w), lambda i: (i, col))
    const = pl.BlockSpec((1, LANES), lambda i: (0, 0))
    return pl.pallas_call(
        _rope_norm_kernel,
        grid=(m // tm,),
        in_specs=[row(1), const, row(D_NSA, q_col), row(6 * D_NSA_KV, kv_col), const, const, const],
        out_specs=[row(D_NSA), row(D_NSA_KV), row(D_NSA_KV)],
        out_shape=[jax.ShapeDtypeStruct((m, D_NSA), BF16), jax.ShapeDtypeStruct((m, D_NSA_KV), BF16),
                   jax.ShapeDtypeStruct((m, D_NSA_KV), BF16)],
        compiler_params=_params("parallel"),
        name="rope_norm",
    )(posf, jnp.asarray(invf), proj, proj, q_gain.reshape(1, LANES), ks_gain.reshape(1, LANES), kw_gain.reshape(1, LANES))


def _compress_kernel(uk_ref, uv_ref, pek_ref, pev_ref, wk1_ref, wk2_ref, wv1_ref, wv2_ref, g_ref, kc_ref, vc_ref):
    def mlp(u, pe_ref, w1_ref, w2_ref):
        u = u.astype(F32)
        n = u.shape[0]
        h1 = _dot((u + pe_ref[0:1, :]).astype(BF16), w1_ref[0])
        h2 = _dot((u + pe_ref[1:2, :]).astype(BF16), w1_ref[1])
        pre = h1 + pltpu.roll(h2, n - 1, axis=0)
        return _dot(_silu(pre).astype(BF16), w2_ref[...])

    kc = mlp(uk_ref[0, 0], pek_ref, wk1_ref, wk2_ref)
    kc_ref[0, 0] = _rms(kc) * g_ref[...]
    vc_ref[0, 0] = mlp(uv_ref[0, 0], pev_ref, wv1_ref, wv2_ref)


def compress(uk, uv, pe_k, pe_v, wk1, wk2, wv1, wv2, kc_gain):
    bsz, nkv, nch, width = uk.shape
    half = CMP_STRIDE * NSA_HEAD_DIM
    u_spec = pl.BlockSpec((1, 1, nch, width), lambda b, g: (b, g, 0, 0))
    o_spec = pl.BlockSpec((1, 1, nch, NSA_HEAD_DIM), lambda b, g: (b, g, 0, 0))
    c2 = lambda s: pl.BlockSpec(s, lambda b, g: (0, 0))
    c3 = lambda s: pl.BlockSpec(s, lambda b, g: (0, 0, 0))
    o_shape = jax.ShapeDtypeStruct((bsz, nkv, nch, NSA_HEAD_DIM), F32)
    return pl.pallas_call(
        _compress_kernel,
        grid=(bsz, nkv),
        in_specs=[u_spec, u_spec, c2((2, half)), c2((2, half)),
                  c3((2, half, CMP_HIDDEN)), c2((CMP_HIDDEN, NSA_HEAD_DIM)),
                  c3((2, half, CMP_HIDDEN)), c2((CMP_HIDDEN, NSA_HEAD_DIM)), c2((1, NSA_HEAD_DIM))],
        out_specs=[o_spec, o_spec],
        out_shape=[o_shape, o_shape],
        compiler_params=_params("parallel", "parallel"),
        name="nsa_compress",
    )(uk, uv, pe_k.reshape(2, half), pe_v.reshape(2, half),
      wk1.reshape(2, half, CMP_HIDDEN).astype(BF16), wk2.astype(BF16),
      wv1.reshape(2, half, CMP_HIDDEN).astype(BF16), wv2.astype(BF16), kc_gain.reshape(1, NSA_HEAD_DIM))


def _stack_heads(q):
    return jnp.concatenate([q[:, r * NSA_HEAD_DIM:(r + 1) * NSA_HEAD_DIM] for r in range(NSA_Q_PER_KV)], axis=0)


def _unstack_heads(o, tq):
    return jnp.concatenate([o[r * tq:(r + 1) * tq, :] for r in range(NSA_Q_PER_KV)], axis=1)


def _cmp_select_kernel(q_ref, kc_ref, vc_ref, ovt_ref, o_ref, sel_ref, *, tq, n_blk):
    R = NSA_Q_PER_KV
    q0 = pl.program_id(2) * tq
    q4 = _stack_heads(q_ref[...])
    kc = kc_ref[0, 0].astype(BF16)
    ncmp = kc.shape[0]
    s = _dot_nt(q4, kc)
    t_row = q0 + lax.broadcasted_iota(I32, (tq, ncmp), 0)
    c_col = lax.broadcasted_iota(I32, (tq, ncmp), 1)
    visf = jnp.where((c_col * CMP_STRIDE + CMP_BLOCK - 1) <= t_row, 1.0, 0.0)
    vis = jnp.concatenate([visf] * R, axis=0) > 0.5
    s = jnp.where(vis, s, NEG_BIG)
    mx = jnp.max(s, axis=-1, keepdims=True)
    e = jnp.where(vis, jnp.exp2(s - mx), 0.0)
    den = jnp.sum(e, axis=-1, keepdims=True)
    p = e / jnp.maximum(den, 1e-30)
    o = _dot(p.astype(BF16), vc_ref[0, 0].astype(BF16))
    o_ref[...] = _unstack_heads(o, tq).astype(o_ref.dtype)

    psum = p[0:tq]
    for r in range(1, R):
        psum = psum + p[r * tq:(r + 1) * tq]
    h, m, l = _split3(psum)
    ovt = ovt_ref[...]
    imp = _dot_nt(ovt, h) + _dot_nt(ovt, m) + _dot_nt(ovt, l)
    jdx = lax.broadcasted_iota(I32, (n_blk, tq), 0)
    t_lane = q0 + lax.broadcasted_iota(I32, (n_blk, tq), 1)
    cur = t_lane // SEL_BLOCK
    causal = jdx <= cur
    forced = ((jdx == 0) | (jdx >= cur - 1)) & causal
    score = jnp.where(forced, jnp.inf, jnp.where(causal, imp, -jnp.inf))
    jf = jdx.astype(F32)
    picked = jnp.zeros((n_blk, tq), F32)
    for _ in range(min(N_SELECT, n_blk)):
        best = jnp.max(score, axis=0, keepdims=True)
        first = jnp.min(jnp.where(score == best, jf, float(n_blk)), axis=0, keepdims=True)
        hit = jf == first
        picked = jnp.where(hit, 1.0, picked)
        score = jnp.where(hit, -jnp.inf, score)
    sel = jnp.where(causal, picked, 0.0)
    if n_blk < LANES:
        sel = jnp.concatenate([sel, jnp.zeros((LANES - n_blk, tq), F32)], axis=0)
    sel_ref[0, 0] = sel.astype(sel_ref.dtype)


def cmp_select(qn, kc, vc, bsz, seq):
    m = bsz * seq
    G = NSA_KV_HEADS
    tq = CMP_TQ
    nq = seq // tq
    ncmp = kc.shape[2]
    n_blk = seq // SEL_BLOCK
    assert n_blk <= LANES and tq % LANES == 0
    c0 = np.arange(ncmp)[None, :] * CMP_STRIDE
    s0 = np.arange(n_blk)[:, None] * SEL_BLOCK
    ov = np.clip(np.minimum(c0 + CMP_BLOCK, s0 + SEL_BLOCK) - np.maximum(c0, s0), 0, None) / CMP_STRIDE
    n_cmp_valid = (seq - CMP_BLOCK) // CMP_STRIDE + 1
    ov[:, n_cmp_valid:] = 0.0
    kv_spec = pl.BlockSpec((1, 1, ncmp, NSA_HEAD_DIM), lambda b, g, i: (b, g, 0, 0))
    kern = functools.partial(_cmp_select_kernel, tq=tq, n_blk=n_blk)
    return pl.pallas_call(
        kern,
        grid=(bsz, G, nq),
        in_specs=[pl.BlockSpec((tq, NSA_Q_PER_KV * NSA_HEAD_DIM), lambda b, g, i: (b * nq + i, g)),
                  kv_spec, kv_spec, pl.BlockSpec((n_blk, ncmp), lambda b, g, i: (0, 0))],
        out_specs=[pl.BlockSpec((tq, NSA_Q_PER_KV * NSA_HEAD_DIM), lambda b, g, i: (b * nq + i, g)),
                   pl.BlockSpec((1, 1, LANES, tq), lambda b, g, i: (b, g, 0, i))],
        out_shape=[jax.ShapeDtypeStruct((m, D_NSA), BF16), jax.ShapeDtypeStruct((bsz, G, LANES, seq), BF16)],
        compiler_params=_params("parallel", "parallel", "parallel"),
        name="nsa_cmp_select",
    )(qn, kc, vc, jnp.asarray(ov, BF16))


def _masked_attn_kernel(q_ref, k_ref, vt_ref, *rest, mode, tq, tk):
    if mode == "sel":
        selt_ref, o_ref = rest
    else:
        (o_ref,) = rest
    R = NSA_Q_PER_KV
    hd = NSA_HEAD_DIM
    q0 = pl.program_id(2) * tq
    key_row = lax.broadcasted_iota(I32, (tk, tq), 0)
    t_lane = q0 + lax.broadcasted_iota(I32, (tk, tq), 1)
    hi = (q0 + tq - 1) // tk + 1
    q = q_ref[...]
    if mode == "sel":
        key_row_b = lax.broadcasted_iota(I32, (tk, LANES), 0)
        blk_lane = lax.broadcasted_iota(I32, (tk, LANES), 1)
        penalty = ((1.0 - selt_ref[0, 0].astype(F32)) * NEG_BIG).astype(BF16)
        rhs = [jnp.concatenate([q[:, r * hd:(r + 1) * hd].astype(F32).T.astype(BF16), penalty], axis=0)
               for r in range(R)]
        lo = 0
    else:
        lo = jnp.maximum(q0 - (WINDOW - 1), 0) // tk

    def body(kt, carry, diagonal=True):
        k0 = kt * tk
        k_t = k_ref[pl.ds(pl.multiple_of(k0, tk), tk), :]
        v_t = vt_ref[0, 0, kt]
        key = k0 + key_row
        if mode == "sel":
            expand = jnp.where((k0 + key_row_b) // SEL_BLOCK == blk_lane, 1.0, 0.0).astype(BF16)
            lhs = jnp.concatenate([k_t, expand], axis=1)
            ss = [_dot(lhs, rhs[r]) for r in range(R)]
            if diagonal:
                ss = [jnp.where(key <= t_lane, s, NEG_BIG) for s in ss]
        else:
            bias = jnp.where((key <= t_lane) & (key > t_lane - WINDOW), 0.0, NEG_BIG)
            ss = [_dot_nt(k_t, q[:, r * hd:(r + 1) * hd]) + bias for r in range(R)]
        stats = []
        for r in range(R):
            m_prev, l_prev = carry[3 * r], carry[3 * r + 1]
            m_new = jnp.maximum(m_prev, jnp.max(ss[r], axis=0, keepdims=True))
            p = jnp.exp2(ss[r] - m_new)
            alpha = jnp.exp2(m_prev - m_new)
            stats.append((m_new, alpha * l_prev + jnp.sum(p, axis=0, keepdims=True), alpha, p.astype(BF16)))
        new = []
        for r in range(R):
            m_new, l_new, alpha, p = stats[r]
            new += [m_new, l_new, alpha * carry[3 * r + 2] + _dot(v_t, p)]
        return tuple(new)

    init = (jnp.full((1, tq), NEG_BIG, F32), jnp.zeros((1, tq), F32), jnp.zeros((hd, tq), F32)) * R
    if mode == "sel":
        out = lax.fori_loop(lo, hi - 1, functools.partial(body, diagonal=False), init)
        out = body(hi - 1, out)
    else:
        out = lax.fori_loop(lo, hi, body, init)
    o = jnp.concatenate([(out[3 * r + 2] / out[3 * r + 1]).T for r in range(R)], axis=1)
    o_ref[...] = o.astype(o_ref.dtype)


def masked_attention(qn, k, k_col0, vt, bsz, seq, mode, selt=None):
    m = bsz * seq
    G = NSA_KV_HEADS
    tq = min(ATT_TQ, seq)
    tk = vt.shape[-1]
    assert tk % tq == 0
    nq = seq // tq
    hd = NSA_HEAD_DIM
    q_spec = pl.BlockSpec((tq, NSA_Q_PER_KV * hd), lambda b, g, i: (b * nq + i, g))
    in_specs = [q_spec,
                pl.BlockSpec((seq, hd), lambda b, g, i: (b, k_col0 + g)),
                pl.BlockSpec((1, 1, seq // tk, hd, tk), lambda b, g, i: (b, g, 0, 0, 0))]
    args = [qn, k, vt]
    if mode == "sel":
        in_specs.append(pl.BlockSpec((1, 1, LANES, tq), lambda b, g, i: (b, g, 0, i)))
        args.append(selt)
    return pl.pallas_call(
        functools.partial(_masked_attn_kernel, mode=mode, tq=tq, tk=tk),
        grid=(bsz, G, nq),
        in_specs=in_specs,
        out_specs=q_spec,
        out_shape=jax.ShapeDtypeStruct((m, D_NSA), BF16),
        compiler_params=_params("parallel", "parallel", "arbitrary"),
        name="nsa_attn_" + mode,
    )(*args)


def _nsa_combine_kernel(g_ref, e_ref, oc_ref, os_ref, ow_ref, o_ref):
    gate = _sigmoid(g_ref[...])
    gx = _dot_f32_sel(gate, e_ref[...])
    y = (gx[:, :D_NSA] * oc_ref[...].astype(F32) + gx[:, D_NSA:2 * D_NSA] * os_ref[...].astype(F32)
         + gx[:, 2 * D_NSA:] * ow_ref[...].astype(F32))
    o_ref[...] = y.astype(o_ref.dtype)


def nsa_combine(small, o_cmp, o_sel, o_win):
    m = small.shape[0]
    tm = min(COMBINE_TM, m)
    e = np.zeros((LANES, 3 * D_NSA), np.float32)
    for h in range(NSA_HEADS):
        for br in range(3):
            e[SSM_HEADS + 3 * h + br, br * D_NSA + h * NSA_HEAD_DIM:br * D_NSA + (h + 1) * NSA_HEAD_DIM] = 1.0
    row = lambda w: pl.BlockSpec((tm, w), lambda i: (i, 0))
    return pl.pallas_call(
        _nsa_combine_kernel,
        grid=(m // tm,),
        in_specs=[row(LANES), pl.BlockSpec((LANES, 3 * D_NSA), lambda i: (0, 0)), row(D_NSA), row(D_NSA), row(D_NSA)],
        out_specs=row(D_NSA),
        out_shape=jax.ShapeDtypeStruct((m, D_NSA), BF16),
        compiler_params=_params("parallel"),
        name="nsa_combine",
    )(small, jnp.asarray(e, BF16), o_cmp, o_sel, o_win)


def _xattn_kernel(q_ref, kv_ref, qg_ref, kg_ref, o_ref):
    hd = XATTN_HEAD_DIM
    scale = hd ** -0.5
    outs = []
    for h in range(XATTN_HEADS):
        q = (_rms(q_ref[:, h * hd:(h + 1) * hd]) * qg_ref[...]).astype(BF16)
        k = (_rms(kv_ref[:, h * hd:(h + 1) * hd]) * kg_ref[...]).astype(BF16)
        v = kv_ref[:, D_XATTN + h * hd:D_XATTN + (h + 1) * hd].astype(BF16)
        s = _dot_nt(q, k) * scale
        e = jnp.exp(s - jnp.max(s, axis=-1, keepdims=True))
        p = e / jnp.sum(e, axis=-1, keepdims=True)
        outs.append(_dot(p.astype(BF16), v))
    o_ref[...] = jnp.concatenate(outs, axis=1).astype(o_ref.dtype)


def xattn_core(q, kv, q_gain, k_gain, bsz, seq):
    m = bsz * seq
    n_mem = kv.shape[0] // bsz
    tm = min(ROW_TILE, seq)
    nt = seq // tm
    return pl.pallas_call(
        _xattn_kernel,
        grid=(bsz, nt),
        in_specs=[pl.BlockSpec((tm, D_XATTN), lambda b, i: (b * nt + i, 0)),
                  pl.BlockSpec((n_mem, 2 * D_XATTN), lambda b, i: (b, 0)),
                  pl.BlockSpec((1, XATTN_HEAD_DIM), lambda b, i: (0, 0)),
                  pl.BlockSpec((1, XATTN_HEAD_DIM), lambda b, i: (0, 0))],
        out_specs=pl.BlockSpec((tm, D_XATTN), lambda b, i: (b * nt + i, 0)),
        out_shape=jax.ShapeDtypeStruct((m, D_XATTN), BF16),
        compiler_params=_params("parallel", "parallel"),
        name="xattn_core",
    )(q, kv, q_gain.reshape(1, XATTN_HEAD_DIM), k_gain.reshape(1, XATTN_HEAD_DIM))


def _norm_mm_kernel(h_ref, g_ref, w_ref, o_ref):
    x = (_rms(h_ref[...]) * g_ref[...]).astype(BF16)
    o_ref[...] = _dot(x, w_ref[...]).astype(o_ref.dtype)


def norm_matmul(h, gain, w):
    m, d = h.shape
    n = w.shape[1]
    tm = min(ROW_TILE, m)
    return pl.pallas_call(
        _norm_mm_kernel,
        grid=(m // tm,),
        in_specs=[pl.BlockSpec((tm, d), lambda i: (i, 0)), pl.BlockSpec((1, d), lambda i: (0, 0)),
                  pl.BlockSpec((d, n), lambda i: (0, 0))],
        out_specs=pl.BlockSpec((tm, n), lambda i: (i, 0)),
        out_shape=jax.ShapeDtypeStruct((m, n), F32),
        compiler_params=_params("parallel"),
        name="norm_matmul",
    )(h, gain.reshape(1, d), w)


def memory_xattn(h, mem2d, norm_x, norm_m, wq, wkv, wo, q_gain, k_gain, bsz, seq):
    q = norm_matmul(h, norm_x, wq.astype(BF16))
    kv = norm_matmul(mem2d, norm_m, wkv.astype(BF16))
    o = xattn_core(q, kv, q_gain, k_gain, bsz, seq)
    return matmul([(o, wo.astype(BF16))], residual=h)


def _mix_kernel(h_ref, hp_ref, g_ref, mu_ref, *rest, tm, tiles_per_seq):
    outs, ext_ref = rest[:-1], rest[-1]
    i = pl.program_id(0)
    hn = _rms(h_ref[...]) * g_ref[...]
    hp = _rms(hp_ref[...]) * g_ref[...]
    hp = jnp.where(i % tiles_per_seq == 0, 0.0, hp)
    ext_ref[0:SUBLANES, :] = hp
    ext_ref[SUBLANES:tm + SUBLANES, :] = hn
    xx = ext_ref[SUBLANES - 1:tm + SUBLANES - 1, :] - hn
    for j, o_ref in enumerate(outs):
        o_ref[...] = (hn + xx * mu_ref[j:j + 1, :]).astype(o_ref.dtype)


def rwkv_mix(h, gain, mu, seq):
    m, d = h.shape
    tm = min(ROW_TILE, seq)
    n_mix = mu.shape[0]
    kern = functools.partial(_mix_kernel, tm=tm, tiles_per_seq=seq // tm)
    row = pl.BlockSpec((tm, d), lambda i: (i, 0))
    return pl.pallas_call(
        kern,
        grid=(m // tm,),
        in_specs=[row, pl.BlockSpec((SUBLANES, d), lambda i: (jnp.maximum(i * (tm // SUBLANES) - 1, 0), 0)),
                  pl.BlockSpec((1, d), lambda i: (0, 0)), pl.BlockSpec((n_mix, d), lambda i: (0, 0))],
        out_specs=[row] * n_mix,
        out_shape=[jax.ShapeDtypeStruct((m, d), BF16)] * n_mix,
        scratch_shapes=[pltpu.VMEM((tm + SUBLANES, d), F32)],
        compiler_params=_params("parallel"),
        name="rwkv_mix",
    )(h, h, gain.reshape(1, d), mu)


WKV_LANE_GROUP = LANES // 4
WKV_ROWS_PER_BATCH = D_MODEL // LANES


def _wkv_fused_kernel(r_ref, k_ref, v_ref, w_ref, a_ref, g_ref, kkp_ref, kap_ref, rkp_ref, lnw_ref, lnb_ref,
                      o_ref, s_ref, wd_s, kh_s, kk_s, b_s, r_s, v_s, g_s, *, steps):
    n = RWKV_HEAD_DIM
    nb = LANES // WKV_LANE_GROUP
    halves = WKV_ROWS_PER_BATCH // SUBLANES
    ahead = min(WKV_LOOKAHEAD, steps)

    @pl.when(pl.program_id(0) == 0)
    def _():
        s_ref[...] = jnp.zeros_like(s_ref)

    def lane_group(cnt):
        return lax.broadcasted_iota(I32, (cnt, SUBLANES, LANES), 2) // WKV_LANE_GROUP

    def to_scan(x_ref, t0, cnt):
        grp = lane_group(cnt)
        rows = [x_ref[b, pl.ds(t0, cnt)].astype(F32) for b in range(nb)]
        pieces = []
        for n4 in range(nb):
            for v in range(halves):
                acc = None
                for b in range(nb):
                    x = rows[b][:, SUBLANES * v:SUBLANES * (v + 1), :]
                    shift = (WKV_LANE_GROUP * (b - n4)) % LANES
                    if shift:
                        x = pltpu.roll(x, shift, axis=2)
                    acc = x if acc is None else jnp.where(grp == b, x, acc)
                pieces.append(acc)
        return pieces

    def prepare(t0, cnt):
        rs, ks, vs, ws, as_, gs = (to_scan(x, t0, cnt) for x in (r_ref, k_ref, v_ref, w_ref, a_ref, g_ref))
        kks = [ks[u] * kkp_ref[SUBLANES * u:SUBLANES * (u + 1), :] for u in range(len(ks))]
        ss = kks[0] * kks[0]
        for u in range(1, len(kks)):
            ss = ss + kks[u] * kks[u]
        inv_norm = 1.0 / jnp.maximum(jnp.sqrt(jnp.sum(ss, axis=1, keepdims=True)), 1e-12)
        for u in range(len(ks)):
            rows = slice(SUBLANES * u, SUBLANES * (u + 1))
            at = (pl.ds(t0, cnt), rows, slice(None))
            kk = kks[u] * inv_norm
            wd_s[at] = jnp.exp(-jnp.exp(-_softplus(-ws[u]) - 0.5))
            kk_s[at] = kk
            b_s[at] = kk * as_[u]
            kh_s[at] = ks[u] * (1.0 + (as_[u] - 1.0) * kap_ref[rows, :])
            r_s[at] = rs[u]
            v_s[at] = vs[u]
            g_s[at] = gs[u]

    def finish(t, y):
        inv_n = 1.0 / n
        mean = jnp.sum(y, axis=0, keepdims=True) * inv_n
        yc = y - mean
        var = jnp.sum(yc * yc, axis=0, keepdims=True) * inv_n
        bonus = jnp.sum(r_s[t] * kh_s[t] * rkp_ref[...], axis=0, keepdims=True) * v_s[t]
        o = (yc * lax.rsqrt(var + RWKV_GN_EPS) * lnw_ref[...] + lnb_ref[...] + bonus) * g_s[t]
        grp = lane_group(1)[0]
        for b in range(nb):
            parts = []
            for v in range(halves):
                acc = None
                for n4 in range(nb):
                    u = n4 * halves + v
                    x = o[SUBLANES * u:SUBLANES * (u + 1), :]
                    shift = (WKV_LANE_GROUP * (n4 - b)) % LANES
                    if shift:
                        x = pltpu.roll(x, shift, axis=1)
                    acc = x if acc is None else jnp.where(grp == n4, x, acc)
                parts.append(acc)
            o_ref[b, pl.ds(t, 1)] = jnp.concatenate(parts, axis=0).astype(o_ref.dtype)[None]

    prepare(0, ahead)
    q0 = jnp.zeros((n, LANES), F32)
    for j in range(n):
        q0 = q0 + s_ref[j] * kk_s[0, j:j + 1, :]

    def step(t, q):
        tn = jnp.minimum(t + 1, steps - 1)
        v = v_s[t]
        y = jnp.zeros((n, LANES), F32)
        qn = jnp.zeros((n, LANES), F32)
        for j in range(n):
            s_new = s_ref[j] * wd_s[t, j:j + 1, :] - q * b_s[t, j:j + 1, :] + v * kh_s[t, j:j + 1, :]
            s_ref[j] = s_new
            y = y + s_new * r_s[t, j:j + 1, :]
            qn = qn + s_new * kk_s[tn, j:j + 1, :]
        finish(t, y)
        prepare(jnp.minimum(t + ahead, steps - 1), 1)
        return qn

    lax.fori_loop(0, steps, step, q0)


def _to_scan_order(x, axis):
    axis = axis % x.ndim
    shp = x.shape
    split = shp[:axis] + (RWKV_HEADS, LANES // WKV_LANE_GROUP, WKV_ROWS_PER_BATCH) + shp[axis + 1:]
    order = tuple(range(axis)) + (axis + 2, axis + 1, axis) + tuple(range(axis + 3, x.ndim + 2))
    return x.reshape(split).transpose(order).reshape(shp)


def _scan_tile(p, bsz):
    return jnp.tile(p.reshape(RWKV_HEADS, RWKV_HEAD_DIM).T, (1, bsz))


def wkv_fused(r, k, v, w_raw, a_sig, g, k_k, k_a, r_k, ln_w, ln_b, bsz, seq):
    assert bsz * RWKV_HEADS == LANES and RWKV_HEADS == WKV_LANE_GROUP
    steps = min(WKV_T, seq)
    view = lambda x: x.reshape(bsz, seq, WKV_ROWS_PER_BATCH, LANES)
    blk = pl.BlockSpec((bsz, steps, WKV_ROWS_PER_BATCH, LANES), lambda i: (0, i, 0, 0))
    par = pl.BlockSpec((RWKV_HEAD_DIM, LANES), lambda i: (0, 0))
    scr = pltpu.VMEM((steps, RWKV_HEAD_DIM, LANES), F32)
    out = pl.pallas_call(
        functools.partial(_wkv_fused_kernel, steps=steps),
        grid=(seq // steps,),
        in_specs=[blk] * 6 + [par] * 5,
        out_specs=blk,
        out_shape=jax.ShapeDtypeStruct((bsz, seq, WKV_ROWS_PER_BATCH, LANES), BF16),
        scratch_shapes=[pltpu.VMEM((RWKV_HEAD_DIM, RWKV_HEAD_DIM, LANES), F32)] + [scr] * 7,
        compiler_params=_params("arbitrary"),
        name="wkv_fused",
    )(view(r), view(k), view(v), view(w_raw), view(a_sig), view(g),
      _scan_tile(k_k, bsz), _scan_tile(k_a, bsz), _scan_tile(r_k, bsz), _scan_tile(ln_w, bsz), _scan_tile(ln_b, bsz))
    return out.reshape(bsz * seq, D_MODEL)


def _router_kernel(h_ref, g_ref, wr_ref, info_ref, tile_cnt_ref, total_ref, carry_ref, *, tt):
    E = N_EXPERTS
    i = pl.program_id(0)

    @pl.when(i == 0)
    def _():
        carry_ref[...] = jnp.zeros_like(carry_ref)

    xh, xm, xl = _split3(_rms(h_ref[...]) * g_ref[...])
    wh, wm, wl = _split3(wr_ref[...])
    logits = (_dot_nt(wh, xh) + _dot_nt(wh, xm) + _dot_nt(wm, xh)
              + _dot_nt(wh, xl) + _dot_nt(wl, xh) + _dot_nt(wm, xm))
    eidx = lax.broadcasted_iota(I32, (E, tt), 0)
    v1 = jnp.max(logits, axis=0, keepdims=True)
    i1 = jnp.min(jnp.where(logits == v1, eidx, E), axis=0, keepdims=True)
    rest = jnp.where(eidx == i1, -jnp.inf, logits)
    v2 = jnp.max(rest, axis=0, keepdims=True)
    i2 = jnp.min(jnp.where(rest == v2, eidx, E), axis=0, keepdims=True)
    e2 = jnp.exp(v2 - v1)
    g1 = 1.0 / (1.0 + e2)
    g2 = e2 / (1.0 + e2)
    hit1 = eidx == i1
    hit2 = eidx == i2
    onehot = jnp.where(hit1 | hit2, 1.0, 0.0)
    rr = lax.broadcasted_iota(I32, (tt, tt), 0)
    cc = lax.broadcasted_iota(I32, (tt, tt), 1)
    before = jnp.where(rr < cc, 1.0, 0.0).astype(BF16)
    carry = carry_ref[...]
    carry_wide = jnp.concatenate([carry] * (tt // LANES), axis=1)
    cum = _dot(onehot.astype(BF16), before) + carry_wide
    rank1 = jnp.sum(jnp.where(hit1, cum, 0.0), axis=0, keepdims=True)
    rank2 = jnp.sum(jnp.where(hit2, cum, 0.0), axis=0, keepdims=True)
    zero = jnp.zeros((1, tt), F32)
    info_ref[...] = jnp.concatenate([i1.astype(F32), i2.astype(F32), rank1, rank2, g1, g2, zero, zero], axis=0)
    tile_cnt_ref[0] = carry
    carry = carry + _dot(onehot.astype(BF16), jnp.ones((tt, LANES), BF16))
    carry_ref[...] = carry
    total_ref[...] = carry


def moe_router(h, norm_gain, w_router):
    m, d = h.shape
    tt = min(MOE_TILE, m)
    nt = m // tt
    return pl.pallas_call(
        functools.partial(_router_kernel, tt=tt),
        grid=(nt,),
        in_specs=[pl.BlockSpec((tt, d), lambda i: (i, 0)), pl.BlockSpec((1, d), lambda i: (0, 0)),
                  pl.BlockSpec((N_EXPERTS, d), lambda i: (0, 0))],
        out_specs=[pl.BlockSpec((SUBLANES, tt), lambda i: (0, i)),
                   pl.BlockSpec((1, N_EXPERTS, LANES), lambda i: (i, 0, 0)),
                   pl.BlockSpec((N_EXPERTS, LANES), lambda i: (0, 0))],
        out_shape=[jax.ShapeDtypeStruct((SUBLANES, m), F32), jax.ShapeDtypeStruct((nt, N_EXPERTS, LANES), F32),
                   jax.ShapeDtypeStruct((N_EXPERTS, LANES), F32)],
        scratch_shapes=[pltpu.VMEM((N_EXPERTS, LANES), F32)],
        compiler_params=_params("arbitrary"),
        name="moe_router",
    )(h, norm_gain.reshape(1, d), w_router.T)


def _slots_kernel(start_ref, info_ref, o_ref):
    info = info_ref[...]
    e1, e2, r1, r2 = info[0:1], info[1:2], info[2:3], info[3:4]
    s1 = r1
    s2 = r2
    for e in range(N_EXPERTS):
        st = start_ref[e].astype(F32)
        s1 = s1 + jnp.where(e1 == float(e), st, 0.0)
        s2 = s2 + jnp.where(e2 == float(e), st, 0.0)
    o_ref[...] = jnp.concatenate([s1, s2, info[4:5], info[5:6], info[6:8], info[6:8]], axis=0)


def moe_slots(info, slot_start):
    m = info.shape[1]
    tt = min(2048, m)
    return pl.pallas_call(
        _slots_kernel,
        grid_spec=pltpu.PrefetchScalarGridSpec(
            num_scalar_prefetch=1, grid=(m // tt,),
            in_specs=[pl.BlockSpec((SUBLANES, tt), lambda i, st: (0, i))],
            out_specs=pl.BlockSpec((SUBLANES, tt), lambda i, st: (0, i))),
        out_shape=jax.ShapeDtypeStruct((SUBLANES, m), F32),
        compiler_params=_params("parallel"),
        name="moe_slots",
    )(slot_start, info)


def _dispatch_kernel(blk_ref, til_ref, flg_ref, slots_ref, x_ref, o_ref, go_ref, *, bs):
    w = pl.program_id(0)
    flags = flg_ref[w]
    first = (flags & 1) != 0
    real = (flags & 4) != 0

    def gathered():
        sl = slots_ref[...]
        sid = (blk_ref[w] * bs + lax.broadcasted_iota(I32, (bs, sl.shape[1]), 0)).astype(F32)
        hit1 = sid == sl[0:1, :]
        hit2 = sid == sl[1:2, :]
        onehot = jnp.where(hit1 | hit2, 1.0, 0.0).astype(BF16)
        gate = jnp.sum(jnp.where(hit1, sl[2:3, :], 0.0) + jnp.where(hit2, sl[3:4, :], 0.0), axis=1, keepdims=True)
        return _dot(onehot, x_ref[...]).astype(o_ref.dtype), jnp.broadcast_to(gate, go_ref.shape)

    @pl.when(first & real)
    def _():
        o_ref[...], go_ref[...] = gathered()

    @pl.when(first & jnp.logical_not(real))
    def _():
        o_ref[...] = jnp.zeros_like(o_ref)
        go_ref[...] = jnp.zeros_like(go_ref)

    @pl.when(jnp.logical_not(first) & real)
    def _():
        rows, gate = gathered()
        o_ref[...] += rows
        go_ref[...] += gate


def moe_dispatch(xn, slots, wl_blk, wl_til, wl_flg, n_slots):
    m, d = xn.shape
    bs, tt = MOE_SLOT_BLOCK, MOE_TILE
    n_items = wl_blk.shape[0]
    return pl.pallas_call(
        functools.partial(_dispatch_kernel, bs=bs),
        grid_spec=pltpu.PrefetchScalarGridSpec(
            num_scalar_prefetch=3, grid=(n_items,),
            in_specs=[pl.BlockSpec((SUBLANES, tt), lambda w, b, t, f: (0, t[w])),
                      pl.BlockSpec((tt, d), lambda w, b, t, f: (t[w], 0))],
            out_specs=[pl.BlockSpec((bs, d), lambda w, b, t, f: (b[w], 0)),
                       pl.BlockSpec((bs, LANES), lambda w, b, t, f: (b[w], 0))]),
        out_shape=[jax.ShapeDtypeStruct((n_slots, d), BF16), jax.ShapeDtypeStruct((n_slots, LANES), F32)],
        compiler_params=_params("arbitrary"),
        name="moe_dispatch",
    )(wl_blk, wl_til, wl_flg, slots, xn)


def _moe_ffn_kernel(be_ref, rows_ref, x_ref, gate_ref, w1_ref, w3_ref, w2_ref, o_ref, acc_ref, *, sub):
    b = pl.program_id(0)
    f = pl.program_id(1)
    bm = x_ref.shape[0]

    @pl.when(f == 0)
    def _():
        acc_ref[...] = jnp.zeros_like(acc_ref)

    for rows in range(sub, bm + 1, sub):
        @pl.when(rows_ref[b] == rows)
        def _(rows=rows):
            x = x_ref[0:rows, :]
            h = _silu(_dot(x, w1_ref[0].astype(BF16))) * _dot(x, w3_ref[0].astype(BF16))
            acc_ref[0:rows, :] += _dot(h.astype(BF16), w2_ref[0].astype(BF16))

    @pl.when(f == pl.num_programs(1) - 1)
    def _():
        o_ref[...] = (acc_ref[...] * gate_ref[:, 0:1]).astype(o_ref.dtype)


def moe_ffn(xs, slot_gate, w1, w3, w2, block_expert, block_rows):
    n_slots, d = xs.shape
    ff = w1.shape[2]
    bm = MOE_FFN_ROWS
    tf = MOE_TF
    nf = ff // tf

    def f_of(b, f, rows):
        return jnp.where(rows[b] > 0, f, nf - 1)

    return pl.pallas_call(
        functools.partial(_moe_ffn_kernel, sub=MOE_ROW_GROUP),
        grid_spec=pltpu.PrefetchScalarGridSpec(
            num_scalar_prefetch=2, grid=(n_slots // bm, nf),
            in_specs=[pl.BlockSpec((bm, d), lambda b, f, be, nb: (b, 0)),
                      pl.BlockSpec((bm, LANES), lambda b, f, be, nb: (b, 0)),
                      pl.BlockSpec((1, d, tf), lambda b, f, be, nb: (be[b], 0, f_of(b, f, nb))),
                      pl.BlockSpec((1, d, tf), lambda b, f, be, nb: (be[b], 0, f_of(b, f, nb))),
                      pl.BlockSpec((1, tf, d), lambda b, f, be, nb: (be[b], f_of(b, f, nb), 0))],
            out_specs=pl.BlockSpec((bm, d), lambda b, f, be, nb: (b, 0)),
            scratch_shapes=[pltpu.VMEM((bm, d), F32)]),
        out_shape=jax.ShapeDtypeStruct((n_slots, d), BF16),
        compiler_params=_params("arbitrary", "arbitrary"),
        name="moe_ffn",
    )(block_expert, block_rows, xs, slot_gate, w1, w3, w2)


def _combine_kernel(til_ref, blk_ref, flg_ref, tok_ref, y_ref, res_ref, o_ref, *, bs):
    w = pl.program_id(0)
    flags = flg_ref[w]

    @pl.when((flags & 1) != 0)
    def _():
        o_ref[...] = res_ref[...]

    @pl.when((flags & 4) != 0)
    def _():
        tok = tok_ref[...]
        tt = tok.shape[0]
        sid = (blk_ref[w] * bs + lax.broadcasted_iota(I32, (tt, bs), 1)).astype(F32)
        onehot = jnp.where((sid == tok[:, 0:1]) | (sid == tok[:, 1:2]), 1.0, 0.0).astype(BF16)
        o_ref[...] += _dot(onehot, y_ref[...])


def moe_combine(y, tok_info, residual, wl_til, wl_blk, wl_flg):
    m, d = residual.shape
    bs, tt = MOE_SLOT_BLOCK, MOE_TILE
    n_items = wl_til.shape[0]
    return pl.pallas_call(
        functools.partial(_combine_kernel, bs=bs),
        grid_spec=pltpu.PrefetchScalarGridSpec(
            num_scalar_prefetch=3, grid=(n_items,),
            in_specs=[pl.BlockSpec((tt, SUBLANES), lambda w, t, b, f: (t[w], 0)),
                      pl.BlockSpec((bs, d), lambda w, t, b, f: (b[w], 0)),
                      pl.BlockSpec((tt, d), lambda w, t, b, f: (t[w], 0))],
            out_specs=pl.BlockSpec((tt, d), lambda w, t, b, f: (t[w], 0))),
        out_shape=jax.ShapeDtypeStruct((m, d), F32),
        compiler_params=_params("arbitrary"),
        name="moe_combine",
    )(wl_til, wl_blk, wl_flg, tok_info, y, residual)


def _work_list(overlap, n_items):
    n_major, n_minor = overlap.shape
    has = jnp.any(overlap, axis=1)
    marked = jnp.concatenate([overlap[:, :1] | ~has[:, None], overlap[:, 1:]], axis=1)
    maj, mino = jnp.nonzero(marked, size=n_items, fill_value=-1)
    valid = maj >= 0
    real = valid & overlap[jnp.maximum(maj, 0), jnp.maximum(mino, 0)]
    prev = jnp.concatenate([jnp.full((1,), -2, maj.dtype), maj[:-1]])
    nxt = jnp.concatenate([maj[1:], jnp.full((1,), -2, maj.dtype)])
    flags = (valid & (maj != prev)) * 1 + (valid & (maj != nxt)) * 2 + real * 4
    last_maj = jnp.max(jnp.where(valid, maj, 0))
    last_min = jnp.sum(jnp.where(valid & (maj == last_maj) & (maj != nxt), mino, 0))
    maj = jnp.where(valid, maj, last_maj)
    mino = jnp.where(valid, mino, last_min)
    return maj.astype(I32), mino.astype(I32), flags.astype(I32)


def moe_layer(h, norm_gain, w_router, w1, w3, w2):
    m, d = h.shape
    E = N_EXPERTS
    tile = min(MOE_TILE, m)
    n_tiles = m // tile
    ffn_rows = MOE_FFN_ROWS
    xn = rmsnorm(h, norm_gain, BF16)
    info, tile_cnt, total = moe_router(h, norm_gain, w_router)
    before = tile_cnt[:, :, 0]
    counts = total[:, 0].astype(I32)
    padded = (counts + ffn_rows - 1) // ffn_rows * ffn_rows
    ends = jnp.cumsum(padded)
    starts = ends - padded
    n_slots = (m * TOP_K // ffn_rows + E) * ffn_rows
    n_ffn_blocks = n_slots // ffn_rows
    sblk = MOE_SLOT_BLOCK
    n_dblocks = n_slots // sblk
    slots = moe_slots(info, starts.astype(I32))
    db_start = jnp.arange(n_dblocks, dtype=I32) * sblk
    db_e = jnp.minimum(jnp.sum(db_start[:, None] >= ends[None, :], axis=1), E - 1)
    r0 = db_start - starts[db_e]
    r1 = jnp.minimum(r0 + sblk, counts[db_e])
    after = jnp.concatenate([before[1:], counts[None, :].astype(F32)], axis=0).astype(I32)
    bef = before.astype(I32)
    ov = (bef[:, db_e].T < r1[:, None]) & (after[:, db_e].T > r0[:, None]) & (db_start < ends[-1])[:, None]
    n_items = n_dblocks + E * n_tiles
    d_blk, d_til, d_flg = _work_list(ov, n_items)
    c_til, c_blk, c_flg = _work_list(ov.T, n_items)
    xs, slot_gate = moe_dispatch(xn, slots, d_blk, d_til, d_flg, n_slots)
    fb_start = jnp.arange(n_ffn_blocks, dtype=I32) * ffn_rows
    block_expert = jnp.minimum(jnp.sum(fb_start[:, None] >= ends[None, :], axis=1), E - 1).astype(I32)
    filled = counts[block_expert] - (fb_start - starts[block_expert])
    grp = MOE_ROW_GROUP
    block_rows = jnp.where(fb_start < ends[-1], (jnp.clip(filled, 0, ffn_rows) + grp - 1) // grp * grp, 0).astype(I32)
    ys = moe_ffn(xs, slot_gate, w1, w3, w2, block_expert, block_rows)
    return moe_combine(ys, slots.T, h, c_til, c_blk, c_flg)


def even_mixer_layer(h, posf, p, bsz, seq):
    m = bsz * seq
    hn = rmsnorm(h, p["norm_mix"], BF16)
    w_in = p["w_in"]
    c = np.cumsum([0, D_SSM, D_CONV, SSM_HEADS, D_NSA] + [D_NSA_KV] * 6 + [3 * NSA_HEADS])
    w_proj = jnp.concatenate([w_in[:, c[1]:c[2]], w_in[:, c[4]:c[10]], w_in[:, c[0]:c[1]], w_in[:, c[3]:c[4]]], axis=1)
    proj = matmul([(hn, w_proj.astype(BF16))], out_dtype=BF16)
    kv0 = D_CONV
    w_small = jnp.concatenate([w_in[:, c[2]:c[3]], w_in[:, c[10]:c[11]]], axis=1)
    w_small = jnp.pad(w_small, ((0, 0), (0, LANES - w_small.shape[1]))).astype(BF16)
    small = matmul([(hn, w_small)])
    y_ssm = ssd_group(proj, (D_CONV + 6 * D_NSA_KV) // D_SSM, 0, small, p["conv_w"], p["conv_b"], p["dt_bias"],
                      p["a_log"], p["d_skip"], p["ssm_norm"], bsz, seq)
    qn, ksn, kwn = rope_norm(posf, proj, (D_CONV + 6 * D_NSA_KV + D_SSM) // D_NSA, kv0 // (6 * D_NSA_KV),
                             p["q_gain"], p["ks_gain"], p["kw_gain"])
    nch = seq // CMP_STRIDE

    def chunks(col0):
        u = proj[:, kv0 + col0:kv0 + col0 + D_NSA_KV].reshape(bsz, nch, CMP_STRIDE, NSA_KV_HEADS, NSA_HEAD_DIM)
        return u.transpose(0, 3, 1, 2, 4).reshape(bsz, NSA_KV_HEADS, nch, CMP_STRIDE * NSA_HEAD_DIM)

    kc, vc = compress(chunks(0), chunks(D_NSA_KV), p["pe_k"], p["pe_v"], p["wk1"], p["wk2"], p["wv1"], p["wv2"],
                      p["kc_gain"])
    o_cmp, selt = cmp_select(qn, kc, vc, bsz, seq)
    def v_tiles(col0, tk):
        tk = min(tk, seq)
        u = proj[:, kv0 + col0:kv0 + col0 + D_NSA_KV].reshape(bsz, seq // tk, tk, NSA_KV_HEADS, NSA_HEAD_DIM)
        return u.transpose(0, 3, 1, 4, 2)

    o_sel = masked_attention(qn, ksn, 0, v_tiles(3 * D_NSA_KV, ATT_TK), bsz, seq, "sel", selt)
    o_win = masked_attention(qn, kwn, 0, v_tiles(5 * D_NSA_KV, WIN_TK), bsz, seq, "win")
    y_nsa = nsa_combine(small, o_cmp, o_sel, o_win)
    w_out = p["w_out"].astype(BF16)
    return matmul([(y_ssm, w_out[:D_SSM]), (y_nsa, w_out[D_SSM:])], residual=h)


def rwkv_layer(h, p, bsz, seq):
    m, d = h.shape
    mixes = rwkv_mix(h, p["norm_mix"], p["mu"], seq)
    bf = lambda w: w.astype(BF16)

    def lora_pad(w_a, w_b):
        r = w_a.shape[1]
        rp = -(-r // LANES) * LANES
        return bf(jnp.pad(w_a, ((0, 0), (0, rp - r)))), bf(jnp.pad(w_b, ((0, rp - r), (0, 0))))

    so = _to_scan_order
    r = matmul([(mixes[0], bf(so(p["w_r"], 1)))], out_dtype=BF16)
    k = matmul([(mixes[2], bf(so(p["w_k"], 1)))], out_dtype=BF16)
    v = matmul([(mixes[3], bf(so(p["w_v"], 1)))], out_dtype=BF16)
    w1, w2 = lora_pad(p["w1"], so(p["w2"], 1))
    w_raw = lora(mixes[1], w1, w2, so(p["w0"], 0), "tanh", None, F32)
    a1, a2 = lora_pad(p["a1"], so(p["a2"], 1))
    a_sig = lora(mixes[4], a1, a2, so(p["a0"], 0), None, "sigmoid", BF16)
    g1, g2 = lora_pad(p["g1"], so(p["g2"], 1))
    g = lora(mixes[5], g1, g2, jnp.zeros((d,), F32), "sigmoid", None, BF16)
    o = wkv_fused(r, k, v, w_raw, a_sig, g, p["k_k"], p["k_a"], p["r_k"].reshape(-1), p["ln_w"], p["ln_b"], bsz, seq)
    return matmul([(o, bf(so(p["w_o"], 0)))], residual=h)


def kernel(x, mem, positions, norm_mix, norm_xattn, norm_mem, norm_ffn, xattn_wq, xattn_wkv, xattn_wo, xattn_q_gain, xattn_k_gain, ev_w_in, ev_conv_w, ev_conv_b, ev_dt_bias, ev_a_log, ev_d_skip, ev_ssm_norm, ev_q_gain, ev_kc_gain, ev_ks_gain, ev_kw_gain, ev_pe_k, ev_pe_v, ev_cmp_wk1, ev_cmp_wk2, ev_cmp_wv1, ev_cmp_wv2, ev_w_out, ev_ffn_w1, ev_ffn_w3, ev_ffn_w2, od_mu, od_w_r, od_w_k, od_w_v, od_w_o, od_w0, od_w1, od_w2, od_a0, od_a1, od_a2, od_g1, od_g2, od_k_k, od_k_a, od_r_k, od_ln_w, od_ln_b, od_router, od_moe_w1, od_moe_w3, od_moe_w2):
    bsz, seq, d = x.shape
    m = bsz * seq
    depth = norm_mix.shape[0]
    h = x.reshape(m, d)
    mem2d = mem.reshape(-1, d)
    posf = positions.astype(F32).reshape(m, 1)
    for layer in range(depth):
        i = layer // 2
        if layer % 2 == 0:
            p = dict(norm_mix=norm_mix[layer], w_in=ev_w_in[i], conv_w=ev_conv_w[i], conv_b=ev_conv_b[i],
                     dt_bias=ev_dt_bias[i], a_log=ev_a_log[i], d_skip=ev_d_skip[i], ssm_norm=ev_ssm_norm[i],
                     q_gain=ev_q_gain[i], kc_gain=ev_kc_gain[i], ks_gain=ev_ks_gain[i], kw_gain=ev_kw_gain[i],
                     pe_k=ev_pe_k[i], pe_v=ev_pe_v[i], wk1=ev_cmp_wk1[i], wk2=ev_cmp_wk2[i], wv1=ev_cmp_wv1[i],
                     wv2=ev_cmp_wv2[i], w_out=ev_w_out[i])
            h = even_mixer_layer(h, posf, p, bsz, seq)
        else:
            p = dict(norm_mix=norm_mix[layer], mu=od_mu[i], w_r=od_w_r[i], w_k=od_w_k[i], w_v=od_w_v[i], w_o=od_w_o[i],
                     w0=od_w0[i], w1=od_w1[i], w2=od_w2[i], a0=od_a0[i], a1=od_a1[i], a2=od_a2[i], g1=od_g1[i],
                     g2=od_g2[i], k_k=od_k_k[i], k_a=od_k_a[i], r_k=od_r_k[i], ln_w=od_ln_w[i], ln_b=od_ln_b[i])
            h = rwkv_layer(h, p, bsz, seq)
        h = memory_xattn(h, mem2d, norm_xattn[layer], norm_mem[layer], xattn_wq[layer], xattn_wkv[layer],
                         xattn_wo[layer], xattn_q_gain[layer], xattn_k_gain[layer], bsz, seq)
        if layer % 2 == 0:
            h = swiglu_ffn(h, norm_ffn[layer], ev_ffn_w1[i].astype(BF16), ev_ffn_w3[i].astype(BF16),
                           ev_ffn_w2[i].astype(BF16))
        else:
            h = moe_layer(h, norm_ffn[layer], od_router[i], od_moe_w1[i], od_moe_w3[i], od_moe_w2[i])
    return h.reshape(bsz, seq, d)
```

```python
import functools
import math

import numpy as np
import jax
import jax.numpy as jnp
from jax import lax
from jax.experimental import pallas as pl
from jax.experimental.pallas import tpu as pltpu

F32 = jnp.float32
BF16 = jnp.bfloat16
I32 = jnp.int32

D_MODEL = 2048
NORM_EPS = 1e-6
NEG_BIG = -1e30

SSM_HEADS = 32
SSM_HEAD_DIM = 64
D_SSM = SSM_HEADS * SSM_HEAD_DIM
SSM_GROUPS = 4
D_STATE = 128
CONV_WIDTH = 4
SSD_CHUNK = 128
D_CONV = D_SSM + 2 * SSM_GROUPS * D_STATE
D_SSM_GROUP = D_SSM // SSM_GROUPS
SSD_CONV_TAIL = 16
SSD_EXT_ROWS = 256

NSA_HEADS = 16
NSA_KV_HEADS = 4
NSA_HEAD_DIM = 128
NSA_Q_PER_KV = NSA_HEADS // NSA_KV_HEADS
D_NSA = NSA_HEADS * NSA_HEAD_DIM
D_NSA_KV = NSA_KV_HEADS * NSA_HEAD_DIM
CMP_BLOCK = 32
CMP_STRIDE = 16
CMP_HIDDEN = 256
SEL_BLOCK = 64
N_SELECT = 16
WINDOW = 512
ROPE_DIM = NSA_HEAD_DIM // 4
ROPE_THETA = 500000.0

RWKV_HEAD_DIM = 64
RWKV_HEADS = D_MODEL // RWKV_HEAD_DIM
RWKV_GN_EPS = 1e-5 * RWKV_HEAD_DIM

XATTN_HEADS = 4
XATTN_HEAD_DIM = 128
D_XATTN = XATTN_HEADS * XATTN_HEAD_DIM

N_EXPERTS = 8
TOP_K = 2

LANES = 128
SUBLANES = 8
VMEM_LIMIT_BYTES = 56 * 1024 * 1024

ROW_TILE = 512
MM_TM = 2048
MM_TN = 512
MM_VMEM_BUDGET = 44 * 1024 * 1024
FFN_TM = 512
FFN_TF = 512
FFN_SPLIT = 256
MOE_FFN_ROWS = 1024
MOE_TF = 256
MOE_ROW_GROUP = 256
COMBINE_TM = 512
MOE_TILE = 512
MOE_SLOT_BLOCK = 256
ATT_TQ = 512
ATT_TK = 512
WIN_TK = 512
ATT_HEAD_WAVE = 2
LOG2E = 1.4426950408889634
CMP_TQ = 512
WKV_T = 64
WKV_LOOKAHEAD = 8


def _params(*sem):
    return pltpu.CompilerParams(dimension_semantics=sem, vmem_limit_bytes=VMEM_LIMIT_BYTES)


def _sigmoid(x):
    return 1.0 / (1.0 + jnp.exp(-x))


def _silu(x):
    return x * _sigmoid(x)


def _softplus(x):
    return jnp.maximum(x, 0.0) + jnp.log(1.0 + jnp.exp(-jnp.abs(x)))


def _split3(x):
    h = x.astype(BF16)
    r = x - h.astype(F32)
    m = r.astype(BF16)
    l = (r - m.astype(F32)).astype(BF16)
    return h, m, l


def _dot(a, b):
    return jnp.dot(a, b, preferred_element_type=F32)


def _dot_nt(a, b):
    return lax.dot_general(a, b, (((1,), (1,)), ((), ())), preferred_element_type=F32)


def _dot_f32_sel(x, e):
    h, m, l = _split3(x)
    return _dot(h, e) + _dot(m, e) + _dot(l, e)


def _sel_dot_f32(e, x):
    h, m, l = _split3(x)
    return _dot(e, h) + _dot(e, m) + _dot(e, l)


def _rms(x, eps=NORM_EPS):
    return x * lax.rsqrt(jnp.mean(x * x, axis=-1, keepdims=True) + eps)


def _rmsnorm_kernel(x_ref, g_ref, o_ref):
    o_ref[...] = (_rms(x_ref[...]) * g_ref[...]).astype(o_ref.dtype)


def rmsnorm(x, gain, out_dtype):
    m, d = x.shape
    tm = min(ROW_TILE, m)
    return pl.pallas_call(
        _rmsnorm_kernel,
        grid=(m // tm,),
        in_specs=[pl.BlockSpec((tm, d), lambda i: (i, 0)), pl.BlockSpec((1, d), lambda i: (0, 0))],
        out_specs=pl.BlockSpec((tm, d), lambda i: (i, 0)),
        out_shape=jax.ShapeDtypeStruct((m, d), out_dtype),
        compiler_params=_params("parallel"),
        name="rmsnorm",
    )(x, gain.reshape(1, d))


def _mm_kernel(*refs, n_pairs, has_res):
    o_ref = refs[-1]
    acc = None
    for p in range(n_pairs):
        d = _dot(refs[2 * p][...], refs[2 * p + 1][...])
        acc = d if acc is None else acc + d
    if has_res:
        acc = acc + refs[2 * n_pairs][...]
    o_ref[...] = acc.astype(o_ref.dtype)


def matmul(pairs, residual=None, out_dtype=F32):
    m = pairs[0][0].shape[0]
    n = pairs[0][1].shape[1]
    k_total = sum(a.shape[1] for a, _ in pairs)
    tm = min(MM_TM if k_total <= D_MODEL else MM_TM // 2, m)
    out_bytes = jnp.dtype(out_dtype).itemsize + (4 if residual is not None else 0)

    def vmem_bytes(tn):
        return 2 * (tm * k_total * 2 + k_total * tn * 2 + tm * tn * out_bytes)

    tn = next((t for t in (2 * MM_TN, MM_TN, 256, LANES) if n % t == 0 and vmem_bytes(t) <= MM_VMEM_BUDGET), LANES)
    in_specs, args = [], []
    for a, w in pairs:
        kk = a.shape[1]
        in_specs += [pl.BlockSpec((tm, kk), lambda i, j: (i, 0)), pl.BlockSpec((kk, tn), lambda i, j: (0, j))]
        args += [a, w]
    if residual is not None:
        in_specs.append(pl.BlockSpec((tm, tn), lambda i, j: (i, j)))
        args.append(residual)
    kern = functools.partial(_mm_kernel, n_pairs=len(pairs), has_res=residual is not None)
    return pl.pallas_call(
        kern,
        grid=(m // tm, n // tn),
        in_specs=in_specs,
        out_specs=pl.BlockSpec((tm, tn), lambda i, j: (i, j)),
        out_shape=jax.ShapeDtypeStruct((m, n), out_dtype),
        compiler_params=_params("parallel", "parallel"),
        name="matmul",
    )(*args)


def _act(x, act):
    if act == "tanh":
        return jnp.tanh(x)
    if act == "sigmoid":
        return _sigmoid(x)
    return x


def _lora_kernel(a_ref, wa_ref, wb_ref, bias_ref, o_ref, *, act_mid, act_out):
    mid = _act(_dot(a_ref[...], wa_ref[...]), act_mid).astype(BF16)
    o_ref[...] = _act(_dot(mid, wb_ref[...]) + bias_ref[...], act_out).astype(o_ref.dtype)


def lora(a, w_a, w_b, bias, act_mid, act_out, out_dtype):
    m, d = a.shape
    r = w_a.shape[1]
    n = w_b.shape[1]
    tm = min(2 * ROW_TILE, m)
    return pl.pallas_call(
        functools.partial(_lora_kernel, act_mid=act_mid, act_out=act_out),
        grid=(m // tm,),
        in_specs=[pl.BlockSpec((tm, d), lambda i: (i, 0)), pl.BlockSpec((d, r), lambda i: (0, 0)),
                  pl.BlockSpec((r, n), lambda i: (0, 0)), pl.BlockSpec((1, n), lambda i: (0, 0))],
        out_specs=pl.BlockSpec((tm, n), lambda i: (i, 0)),
        out_shape=jax.ShapeDtypeStruct((m, n), out_dtype),
        compiler_params=_params("parallel"),
        name="lora",
    )(a, w_a, w_b, bias.reshape(1, n).astype(F32))


def _ffn_kernel(g_ref, w1_ref, w3_ref, w2_ref, res_ref, o_ref, x_ref):
    f = pl.program_id(1)

    @pl.when(f == 0)
    def _():
        res = res_ref[...]
        o_ref[...] = res
        x_ref[...] = (_rms(res) * g_ref[...]).astype(x_ref.dtype)

    x = x_ref[...]
    tf = w1_ref.shape[1]
    cols = [slice(c, c + FFN_SPLIT) for c in range(0, tf, FFN_SPLIT)]
    gate_up = [(_dot(x, w1_ref[:, c]), _dot(x, w3_ref[:, c])) for c in cols]
    hs = [(_silu(g) * u).astype(BF16) for g, u in gate_up]
    acc = _dot(hs[0], w2_ref[cols[0], :])
    for h, c in zip(hs[1:], cols[1:]):
        acc = acc + _dot(h, w2_ref[c, :])
    o_ref[...] += acc


def swiglu_ffn(residual, norm_gain, w1, w3, w2):
    m, d = residual.shape
    ff = w1.shape[1]
    tm = min(FFN_TM, m)
    tf = FFN_TF
    return pl.pallas_call(
        _ffn_kernel,
        grid=(m // tm, ff // tf),
        in_specs=[pl.BlockSpec((1, d), lambda i, f: (0, 0)),
                  pl.BlockSpec((d, tf), lambda i, f: (0, f)),
                  pl.BlockSpec((d, tf), lambda i, f: (0, f)),
                  pl.BlockSpec((tf, d), lambda i, f: (f, 0)),
                  pl.BlockSpec((tm, d), lambda i, f: (i, 0))],
        out_specs=pl.BlockSpec((tm, d), lambda i, f: (i, 0)),
        out_shape=jax.ShapeDtypeStruct((m, d), F32),
        scratch_shapes=[pltpu.VMEM((tm, d), BF16)],
        compiler_params=_params("parallel", "arbitrary"),
        name="swiglu_ffn",
    )(norm_gain.reshape(1, d), w1, w3, w2, residual)


def _ssd_kernel(z_ref, xbc_ref, dt_ref, shift_ref, cw_ref, cb_ref, dtb_ref, alog_ref, dskip_ref, nw_ref, eh_ref,
                o_ref, ext_ref, st_ref):
    L = SSD_CHUNK
    G = SSM_GROUPS
    GW = D_SSM_GROUP
    T = SSD_CONV_TAIL
    c = pl.program_id(1)

    @pl.when(c == 0)
    def _():
        ext_ref[...] = jnp.zeros_like(ext_ref)
        st_ref[...] = jnp.zeros_like(st_ref)

    ext_ref[0:T, :] = ext_ref[L:L + T, :]
    ext_ref[T:L + T, :] = xbc_ref[...]
    shifted = _dot(shift_ref[...], ext_ref[...])
    acc = jnp.zeros((L, D_CONV), F32) + cb_ref[...]
    for k in range(CONV_WIDTH):
        acc = acc + cw_ref[k:k + 1, :] * shifted[k * L:(k + 1) * L, :]
    xc = _silu(acc)

    dt = _softplus(dt_ref[...] + dtb_ref[...])
    a = -jnp.exp(alog_ref[...])
    adt = dt * a
    row = lax.broadcasted_iota(I32, (L, L), 0)
    col = lax.broadcasted_iota(I32, (L, L), 1)
    causal = row >= col
    tril = jnp.where(causal, 1.0, 0.0).astype(BF16)
    a_cum = _sel_dot_f32(tril, adt)
    a_cum_t = a_cum.T
    eh = eh_ref[...]
    acx = _dot_f32_sel(a_cum, eh)
    dtx = _dot_f32_sel(dt, eh)
    a_end = acx[L - 1:L, :]
    xs = xc[:, :D_SSM]
    xdt = xs * dtx
    xdte = (xdt * jnp.exp(a_end - acx)).astype(BF16)
    from_start = jnp.exp(acx)
    lane = lax.broadcasted_iota(I32, (L, LANES), 1)
    lo_half = lane < SSM_HEAD_DIM
    z = z_ref[...].astype(F32)

    for g in range(G):
        bm = xc[:, D_SSM + g * D_STATE:D_SSM + (g + 1) * D_STATE]
        cm = xc[:, D_SSM + G * D_STATE + g * D_STATE:D_SSM + G * D_STATE + (g + 1) * D_STATE]
        bm_b = bm.astype(BF16)
        cm_b = cm.astype(BF16)
        cb = _dot_nt(cm_b, bm_b)
        st = st_ref[g]
        y_off = _dot(cm_b, st.astype(BF16)) * from_start[:, g * GW:(g + 1) * GW]
        pieces = []
        for pp in range(GW // LANES):
            h0 = g * (GW // SSM_HEAD_DIM) + 2 * pp
            ms = []
            for hh in (h0, h0 + 1):
                diff = a_cum[:, hh:hh + 1] - a_cum_t[hh:hh + 1, :]
                dec = jnp.exp(jnp.where(causal, diff, NEG_BIG))
                ms.append((cb * dec).astype(BF16))
            lhs = jnp.concatenate(ms, axis=1)
            c0 = g * GW + pp * LANES
            xp = xdt[:, c0:c0 + LANES]
            rhs = jnp.concatenate([jnp.where(lo_half, xp, 0.0), jnp.where(lo_half, 0.0, xp)], axis=0)
            pieces.append(_dot(lhs, rhs.astype(BF16)))
        y = jnp.concatenate(pieces, axis=1) + y_off
        y = y + xs[:, g * GW:(g + 1) * GW] * dskip_ref[:, g * GW:(g + 1) * GW]
        y = y * _silu(z[:, g * GW:(g + 1) * GW])
        y = _rms(y) * nw_ref[:, g * GW:(g + 1) * GW]
        o_ref[:, g * GW:(g + 1) * GW] = y.astype(o_ref.dtype)
        s_new = _dot(bm.T.astype(BF16), xdte[:, g * GW:(g + 1) * GW])
        st_ref[g] = st * jnp.exp(a_end[:, g * GW:(g + 1) * GW]) + s_new


def ssd_group(proj, z_col, xbc_col, small, conv_w, conv_b, dt_bias, a_log, d_skip, norm_w, bsz, seq):
    m = bsz * seq
    L = SSD_CHUNK
    nc = seq // L
    pad = LANES - SSM_HEADS
    eh = np.zeros((LANES, D_SSM), np.float32)
    for h in range(SSM_HEADS):
        eh[h, h * SSM_HEAD_DIM:(h + 1) * SSM_HEAD_DIM] = 1.0
    shift = np.zeros((CONV_WIDTH * L, SSD_EXT_ROWS), np.float32)
    for k in range(CONV_WIDTH):
        shift[k * L + np.arange(L), np.arange(L) + SSD_CONV_TAIL - (CONV_WIDTH - 1) + k] = 1.0
    row_spec = lambda w, col=0: pl.BlockSpec((L, w), lambda b, c: (b * nc + c, col))
    const = lambda r, w: pl.BlockSpec((r, w), lambda b, c: (0, 0))
    return pl.pallas_call(
        _ssd_kernel,
        grid=(bsz, nc),
        in_specs=[row_spec(D_SSM, z_col), row_spec(D_CONV, xbc_col), row_spec(LANES),
                  const(CONV_WIDTH * L, SSD_EXT_ROWS),
                  const(CONV_WIDTH, D_CONV), const(1, D_CONV), const(1, LANES), const(1, LANES),
                  const(1, D_SSM), const(1, D_SSM), const(LANES, D_SSM)],
        out_specs=row_spec(D_SSM),
        out_shape=jax.ShapeDtypeStruct((m, D_SSM), BF16),
        scratch_shapes=[pltpu.VMEM((SSD_EXT_ROWS, D_CONV), BF16),
                        pltpu.VMEM((SSM_GROUPS, D_STATE, D_SSM_GROUP), F32)],
        compiler_params=_params("parallel", "arbitrary"),
        name="ssd_group",
    )(proj, proj, small, jnp.asarray(shift, BF16), conv_w.T, conv_b.reshape(1, D_CONV),
      jnp.pad(dt_bias, (0, pad)).reshape(1, LANES), jnp.pad(a_log, (0, pad)).reshape(1, LANES),
      jnp.repeat(d_skip, SSM_HEAD_DIM).reshape(1, D_SSM), norm_w.reshape(1, D_SSM), jnp.asarray(eh, BF16))


def _rope_norm_kernel(pos_ref, invf_ref, q_ref, kv_ref, qg_ref, ksg_ref, kwg_ref, qo_ref, kso_ref, kwo_ref):
    ang = pos_ref[...] * invf_ref[...]
    cos = jnp.cos(ang)
    sin = jnp.sin(ang)
    lane = lax.broadcasted_iota(I32, ang.shape, 1)
    first = lane < ROPE_DIM // 2
    sin_signed = jnp.where(first, -sin, sin)

    def head(x, gain):
        y = _rms(x.astype(F32)) * gain
        partner = jnp.where(first, pltpu.roll(y, LANES - ROPE_DIM // 2, axis=1), pltpu.roll(y, ROPE_DIM // 2, axis=1))
        return y * cos + partner * sin_signed

    hd = NSA_HEAD_DIM
    scale = hd ** -0.5 * LOG2E
    for h in range(NSA_HEADS):
        qo_ref[:, h * hd:(h + 1) * hd] = (head(q_ref[:, h * hd:(h + 1) * hd], qg_ref[...]) * scale).astype(qo_ref.dtype)
    for h in range(NSA_KV_HEADS):
        ks = kv_ref[:, 2 * D_NSA_KV + h * hd:2 * D_NSA_KV + (h + 1) * hd]
        kw = kv_ref[:, 4 * D_NSA_KV + h * hd:4 * D_NSA_KV + (h + 1) * hd]
        kso_ref[:, h * hd:(h + 1) * hd] = head(ks, ksg_ref[...]).astype(kso_ref.dtype)
        kwo_ref[:, h * hd:(h + 1) * hd] = head(kw, kwg_ref[...]).astype(kwo_ref.dtype)


def rope_norm(posf, proj, q_col, kv_col, q_gain, ks_gain, kw_gain):
    m = proj.shape[0]
    tm = min(ROW_TILE, m)
    half = ROPE_DIM // 2
    inv = np.exp(-math.log(ROPE_THETA) * np.arange(0, ROPE_DIM, 2, dtype=np.float32) / ROPE_DIM).astype(np.float32)
    invf = np.zeros((1, LANES), np.float32)
    invf[0, :half] = inv
    invf[0, half:ROPE_DIM] = inv
    row = lambda w, col=0: pl.BlockSpec((tm, w), lambda i: (i, col))
    const = pl.BlockSpec((1, LANES), lambda i: (0, 0))
    return pl.pallas_call(
        _rope_norm_kernel,
        grid=(m // tm,),
        in_specs=[row(1), const, row(D_NSA, q_col), row(6 * D_NSA_KV, kv_col), const, const, const],
        out_specs=[row(D_NSA), row(D_NSA_KV), row(D_NSA_KV)],
        out_shape=[jax.ShapeDtypeStruct((m, D_NSA), BF16), jax.ShapeDtypeStruct((m, D_NSA_KV), BF16),
                   jax.ShapeDtypeStruct((m, D_NSA_KV), BF16)],
        compiler_params=_params("parallel"),
        name="rope_norm",
    )(posf, jnp.asarray(invf), proj, proj, q_gain.reshape(1, LANES), ks_gain.reshape(1, LANES), kw_gain.reshape(1, LANES))


def _compress_kernel(uk_ref, uv_ref, pek_ref, pev_ref, wk1_ref, wk2_ref, wv1_ref, wv2_ref, g_ref, kc_ref, vc_ref):
    def mlp(u, pe_ref, w1_ref, w2_ref):
        u = u.astype(F32)
        n = u.shape[0]
        h1 = _dot((u + pe_ref[0:1, :]).astype(BF16), w1_ref[0])
        h2 = _dot((u + pe_ref[1:2, :]).astype(BF16), w1_ref[1])
        pre = h1 + pltpu.roll(h2, n - 1, axis=0)
        return _dot(_silu(pre).astype(BF16), w2_ref[...])

    kc = mlp(uk_ref[0, 0], pek_ref, wk1_ref, wk2_ref)
    kc_ref[0, 0] = _rms(kc) * g_ref[...]
    vc_ref[0, 0] = mlp(uv_ref[0, 0], pev_ref, wv1_ref, wv2_ref)


def compress(uk, uv, pe_k, pe_v, wk1, wk2, wv1, wv2, kc_gain):
    bsz, nkv, nch, width = uk.shape
    half = CMP_STRIDE * NSA_HEAD_DIM
    u_spec = pl.BlockSpec((1, 1, nch, width), lambda b, g: (b, g, 0, 0))
    o_spec = pl.BlockSpec((1, 1, nch, NSA_HEAD_DIM), lambda b, g: (b, g, 0, 0))
    c2 = lambda s: pl.BlockSpec(s, lambda b, g: (0, 0))
    c3 = lambda s: pl.BlockSpec(s, lambda b, g: (0, 0, 0))
    o_shape = jax.ShapeDtypeStruct((bsz, nkv, nch, NSA_HEAD_DIM), F32)
    return pl.pallas_call(
        _compress_kernel,
        grid=(bsz, nkv),
        in_specs=[u_spec, u_spec, c2((2, half)), c2((2, half)),
                  c3((2, half, CMP_HIDDEN)), c2((CMP_HIDDEN, NSA_HEAD_DIM)),
                  c3((2, half, CMP_HIDDEN)), c2((CMP_HIDDEN, NSA_HEAD_DIM)), c2((1, NSA_HEAD_DIM))],
        out_specs=[o_spec, o_spec],
        out_shape=[o_shape, o_shape],
        compiler_params=_params("parallel", "parallel"),
        name="nsa_compress",
    )(uk, uv, pe_k.reshape(2, half), pe_v.reshape(2, half),
      wk1.reshape(2, half, CMP_HIDDEN).astype(BF16), wk2.astype(BF16),
      wv1.reshape(2, half, CMP_HIDDEN).astype(BF16), wv2.astype(BF16), kc_gain.reshape(1, NSA_HEAD_DIM))


def _stack_heads(q):
    return jnp.concatenate([q[:, r * NSA_HEAD_DIM:(r + 1) * NSA_HEAD_DIM] for r in range(NSA_Q_PER_KV)], axis=0)


def _unstack_heads(o, tq):
    return jnp.concatenate([o[r * tq:(r + 1) * tq, :] for r in range(NSA_Q_PER_KV)], axis=1)


def _cmp_select_kernel(q_ref, kc_ref, vc_ref, ovt_ref, o_ref, sel_ref, *, tq, n_blk):
    R = NSA_Q_PER_KV
    q0 = pl.program_id(2) * tq
    q4 = _stack_heads(q_ref[...])
    kc = kc_ref[0, 0].astype(BF16)
    ncmp = kc.shape[0]
    s = _dot_nt(q4, kc)
    t_row = q0 + lax.broadcasted_iota(I32, (tq, ncmp), 0)
    c_col = lax.broadcasted_iota(I32, (tq, ncmp), 1)
    visf = jnp.where((c_col * CMP_STRIDE + CMP_BLOCK - 1) <= t_row, 1.0, 0.0)
    vis = jnp.concatenate([visf] * R, axis=0) > 0.5
    s = jnp.where(vis, s, NEG_BIG)
    mx = jnp.max(s, axis=-1, keepdims=True)
    e = jnp.where(vis, jnp.exp2(s - mx), 0.0)
    den = jnp.sum(e, axis=-1, keepdims=True)
    p = e / jnp.maximum(den, 1e-30)
    o = _dot(p.astype(BF16), vc_ref[0, 0].astype(BF16))
    o_ref[...] = _unstack_heads(o, tq).astype(o_ref.dtype)

    psum = p[0:tq]
    for r in range(1, R):
        psum = psum + p[r * tq:(r + 1) * tq]
    h, m, l = _split3(psum)
    ovt = ovt_ref[...]
    imp = _dot_nt(ovt, h) + _dot_nt(ovt, m) + _dot_nt(ovt, l)
    jdx = lax.broadcasted_iota(I32, (n_blk, tq), 0)
    t_lane = q0 + lax.broadcasted_iota(I32, (n_blk, tq), 1)
    cur = t_lane // SEL_BLOCK
    causal = jdx <= cur
    forced = ((jdx == 0) | (jdx >= cur - 1)) & causal
    score = jnp.where(forced, jnp.inf, jnp.where(causal, imp, -jnp.inf))
    jf = jdx.astype(F32)
    picked = jnp.zeros((n_blk, tq), F32)
    for _ in range(min(N_SELECT, n_blk)):
        best = jnp.max(score, axis=0, keepdims=True)
        first = jnp.min(jnp.where(score == best, jf, float(n_blk)), axis=0, keepdims=True)
        hit = jf == first
        picked = jnp.where(hit, 1.0, picked)
        score = jnp.where(hit, -jnp.inf, score)
    sel = jnp.where(causal, picked, 0.0)
    if n_blk < LANES:
        sel = jnp.concatenate([sel, jnp.zeros((LANES - n_blk, tq), F32)], axis=0)
    sel_ref[0, 0] = sel.astype(sel_ref.dtype)


def cmp_select(qn, kc, vc, bsz, seq):
    m = bsz * seq
    G = NSA_KV_HEADS
    tq = CMP_TQ
    nq = seq // tq
    ncmp = kc.shape[2]
    n_blk = seq // SEL_BLOCK
    assert n_blk <= LANES and tq % LANES == 0
    c0 = np.arange(ncmp)[None, :] * CMP_STRIDE
    s0 = np.arange(n_blk)[:, None] * SEL_BLOCK
    ov = np.clip(np.minimum(c0 + CMP_BLOCK, s0 + SEL_BLOCK) - np.maximum(c0, s0), 0, None) / CMP_STRIDE
    n_cmp_valid = (seq - CMP_BLOCK) // CMP_STRIDE + 1
    ov[:, n_cmp_valid:] = 0.0
    kv_spec = pl.BlockSpec((1, 1, ncmp, NSA_HEAD_DIM), lambda b, g, i: (b, g, 0, 0))
    kern = functools.partial(_cmp_select_kernel, tq=tq, n_blk=n_blk)
    return pl.pallas_call(
        kern,
        grid=(bsz, G, nq),
        in_specs=[pl.BlockSpec((tq, NSA_Q_PER_KV * NSA_HEAD_DIM), lambda b, g, i: (b * nq + i, g)),
                  kv_spec, kv_spec, pl.BlockSpec((n_blk, ncmp), lambda b, g, i: (0, 0))],
        out_specs=[pl.BlockSpec((tq, NSA_Q_PER_KV * NSA_HEAD_DIM), lambda b, g, i: (b * nq + i, g)),
                   pl.BlockSpec((1, 1, LANES, tq), lambda b, g, i: (b, g, 0, i))],
        out_shape=[jax.ShapeDtypeStruct((m, D_NSA), BF16), jax.ShapeDtypeStruct((bsz, G, LANES, seq), BF16)],
        compiler_params=_params("parallel", "parallel", "parallel"),
        name="nsa_cmp_select",
    )(qn, kc, vc, jnp.asarray(ov, BF16))


def _masked_attn_kernel(q_ref, k_ref, vt_ref, *rest, mode, tq, tk):
    if mode == "sel":
        selt_ref, o_ref = rest
    else:
        (o_ref,) = rest
    R = NSA_Q_PER_KV
    hd = NSA_HEAD_DIM
    q0 = pl.program_id(2) * tq
    key_row = lax.broadcasted_iota(I32, (tk, tq), 0)
    t_lane = q0 + lax.broadcasted_iota(I32, (tk, tq), 1)
    hi = (q0 + tq - 1) // tk + 1
    q = q_ref[...]
    if mode == "sel":
        key_row_b = lax.broadcasted_iota(I32, (tk, LANES), 0)
        blk_lane = lax.broadcasted_iota(I32, (tk, LANES), 1)
        penalty = ((1.0 - selt_ref[0, 0].astype(F32)) * NEG_BIG).astype(BF16)
        rhs = [jnp.concatenate([q[:, r * hd:(r + 1) * hd].astype(F32).T.astype(BF16), penalty], axis=0)
               for r in range(R)]
        lo = 0
    else:
        lo = jnp.maximum(q0 - (WINDOW - 1), 0) // tk

    def body(kt, carry, diagonal=True):
        k0 = kt * tk
        k_t = k_ref[pl.ds(pl.multiple_of(k0, tk), tk), :]
        v_t = vt_ref[0, 0, kt]
        key = k0 + key_row
        if mode == "sel":
            expand = jnp.where((k0 + key_row_b) // SEL_BLOCK == blk_lane, 1.0, 0.0).astype(BF16)
            lhs = jnp.concatenate([k_t, expand], axis=1)
        else:
            bias = jnp.where((key <= t_lane) & (key > t_lane - WINDOW), 0.0, NEG_BIG)

        def score(r):
            if mode == "sel":
                s = _dot(lhs, rhs[r])
                return jnp.where(key <= t_lane, s, NEG_BIG) if diagonal else s
            return _dot_nt(k_t, q[:, r * hd:(r + 1) * hd]) + bias

        new = [None] * (3 * R)
        for r0 in range(0, R, ATT_HEAD_WAVE):
            wave = range(r0, min(r0 + ATT_HEAD_WAVE, R))
            ss = {r: score(r) for r in wave}
            stats = {}
            for r in wave:
                m_prev, l_prev = carry[3 * r], carry[3 * r + 1]
                m_new = jnp.maximum(m_prev, jnp.max(ss[r], axis=0, keepdims=True))
                p = jnp.exp2(ss[r] - m_new)
                alpha = jnp.exp2(m_prev - m_new)
                stats[r] = (m_new, alpha * l_prev + jnp.sum(p, axis=0, keepdims=True), alpha, p.astype(BF16))
            for r in wave:
                m_new, l_new, alpha, p = stats[r]
                new[3 * r:3 * r + 3] = [m_new, l_new, alpha * carry[3 * r + 2] + _dot(v_t, p)]
        return tuple(new)

    init = (jnp.full((1, tq), NEG_BIG, F32), jnp.zeros((1, tq), F32), jnp.zeros((hd, tq), F32)) * R
    if mode == "sel":
        out = lax.fori_loop(lo, hi - 1, functools.partial(body, diagonal=False), init)
        out = body(hi - 1, out)
    else:
        out = lax.fori_loop(lo, hi, body, init)
    o = jnp.concatenate([(out[3 * r + 2] / out[3 * r + 1]).T for r in range(R)], axis=1)
    o_ref[...] = o.astype(o_ref.dtype)


def masked_attention(qn, k, k_col0, vt, bsz, seq, mode, selt=None):
    m = bsz * seq
    G = NSA_KV_HEADS
    tq = min(ATT_TQ, seq)
    tk = vt.shape[-1]
    assert tk % tq == 0
    nq = seq // tq
    hd = NSA_HEAD_DIM
    q_spec = pl.BlockSpec((tq, NSA_Q_PER_KV * hd), lambda b, g, i: (b * nq + i, g))
    in_specs = [q_spec,
                pl.BlockSpec((seq, hd), lambda b, g, i: (b, k_col0 + g)),
                pl.BlockSpec((1, 1, seq // tk, hd, tk), lambda b, g, i: (b, g, 0, 0, 0))]
    args = [qn, k, vt]
    if mode == "sel":
        in_specs.append(pl.BlockSpec((1, 1, LANES, tq), lambda b, g, i: (b, g, 0, i)))
        args.append(selt)
    return pl.pallas_call(
        functools.partial(_masked_attn_kernel, mode=mode, tq=tq, tk=tk),
        grid=(bsz, G, nq),
        in_specs=in_specs,
        out_specs=q_spec,
        out_shape=jax.ShapeDtypeStruct((m, D_NSA), BF16),
        compiler_params=_params("parallel", "parallel", "arbitrary"),
        name="nsa_attn_" + mode,
    )(*args)


def _nsa_combine_kernel(g_ref, e_ref, oc_ref, os_ref, ow_ref, o_ref):
    gate = _sigmoid(g_ref[...])
    gx = _dot_f32_sel(gate, e_ref[...])
    y = (gx[:, :D_NSA] * oc_ref[...].astype(F32) + gx[:, D_NSA:2 * D_NSA] * os_ref[...].astype(F32)
         + gx[:, 2 * D_NSA:] * ow_ref[...].astype(F32))
    o_ref[...] = y.astype(o_ref.dtype)


def nsa_combine(small, o_cmp, o_sel, o_win):
    m = small.shape[0]
    tm = min(COMBINE_TM, m)
    e = np.zeros((LANES, 3 * D_NSA), np.float32)
    for h in range(NSA_HEADS):
        for br in range(3):
            e[SSM_HEADS + 3 * h + br, br * D_NSA + h * NSA_HEAD_DIM:br * D_NSA + (h + 1) * NSA_HEAD_DIM] = 1.0
    row = lambda w: pl.BlockSpec((tm, w), lambda i: (i, 0))
    return pl.pallas_call(
        _nsa_combine_kernel,
        grid=(m // tm,),
        in_specs=[row(LANES), pl.BlockSpec((LANES, 3 * D_NSA), lambda i: (0, 0)), row(D_NSA), row(D_NSA), row(D_NSA)],
        out_specs=row(D_NSA),
        out_shape=jax.ShapeDtypeStruct((m, D_NSA), BF16),
        compiler_params=_params("parallel"),
        name="nsa_combine",
    )(small, jnp.asarray(e, BF16), o_cmp, o_sel, o_win)


def _xattn_kernel(q_ref, kv_ref, qg_ref, kg_ref, o_ref):
    hd = XATTN_HEAD_DIM
    scale = hd ** -0.5
    outs = []
    for h in range(XATTN_HEADS):
        q = (_rms(q_ref[:, h * hd:(h + 1) * hd]) * qg_ref[...]).astype(BF16)
        k = (_rms(kv_ref[:, h * hd:(h + 1) * hd]) * kg_ref[...]).astype(BF16)
        v = kv_ref[:, D_XATTN + h * hd:D_XATTN + (h + 1) * hd].astype(BF16)
        s = _dot_nt(q, k) * scale
        e = jnp.exp(s - jnp.max(s, axis=-1, keepdims=True))
        p = e / jnp.sum(e, axis=-1, keepdims=True)
        outs.append(_dot(p.astype(BF16), v))
    o_ref[...] = jnp.concatenate(outs, axis=1).astype(o_ref.dtype)


def xattn_core(q, kv, q_gain, k_gain, bsz, seq):
    m = bsz * seq
    n_mem = kv.shape[0] // bsz
    tm = min(ROW_TILE, seq)
    nt = seq // tm
    return pl.pallas_call(
        _xattn_kernel,
        grid=(bsz, nt),
        in_specs=[pl.BlockSpec((tm, D_XATTN), lambda b, i: (b * nt + i, 0)),
                  pl.BlockSpec((n_mem, 2 * D_XATTN), lambda b, i: (b, 0)),
                  pl.BlockSpec((1, XATTN_HEAD_DIM), lambda b, i: (0, 0)),
                  pl.BlockSpec((1, XATTN_HEAD_DIM), lambda b, i: (0, 0))],
        out_specs=pl.BlockSpec((tm, D_XATTN), lambda b, i: (b * nt + i, 0)),
        out_shape=jax.ShapeDtypeStruct((m, D_XATTN), BF16),
        compiler_params=_params("parallel", "parallel"),
        name="xattn_core",
    )(q, kv, q_gain.reshape(1, XATTN_HEAD_DIM), k_gain.reshape(1, XATTN_HEAD_DIM))


def _norm_mm_kernel(h_ref, g_ref, w_ref, o_ref):
    x = (_rms(h_ref[...]) * g_ref[...]).astype(BF16)
    o_ref[...] = _dot(x, w_ref[...]).astype(o_ref.dtype)


def norm_matmul(h, gain, w):
    m, d = h.shape
    n = w.shape[1]
    tm = min(ROW_TILE, m)
    return pl.pallas_call(
        _norm_mm_kernel,
        grid=(m // tm,),
        in_specs=[pl.BlockSpec((tm, d), lambda i: (i, 0)), pl.BlockSpec((1, d), lambda i: (0, 0)),
                  pl.BlockSpec((d, n), lambda i: (0, 0))],
        out_specs=pl.BlockSpec((tm, n), lambda i: (i, 0)),
        out_shape=jax.ShapeDtypeStruct((m, n), F32),
        compiler_params=_params("parallel"),
        name="norm_matmul",
    )(h, gain.reshape(1, d), w)


def memory_xattn(h, mem2d, norm_x, norm_m, wq, wkv, wo, q_gain, k_gain, bsz, seq):
    q = norm_matmul(h, norm_x, wq.astype(BF16))
    kv = norm_matmul(mem2d, norm_m, wkv.astype(BF16))
    o = xattn_core(q, kv, q_gain, k_gain, bsz, seq)
    return matmul([(o, wo.astype(BF16))], residual=h)


def _mix_kernel(h_ref, hp_ref, g_ref, mu_ref, *rest, tm, tiles_per_seq):
    outs, ext_ref = rest[:-1], rest[-1]
    i = pl.program_id(0)
    hn = _rms(h_ref[...]) * g_ref[...]
    hp = _rms(hp_ref[...]) * g_ref[...]
    hp = jnp.where(i % tiles_per_seq == 0, 0.0, hp)
    ext_ref[0:SUBLANES, :] = hp
    ext_ref[SUBLANES:tm + SUBLANES, :] = hn
    xx = ext_ref[SUBLANES - 1:tm + SUBLANES - 1, :] - hn
    for j, o_ref in enumerate(outs):
        o_ref[...] = (hn + xx * mu_ref[j:j + 1, :]).astype(o_ref.dtype)


def rwkv_mix(h, gain, mu, seq):
    m, d = h.shape
    tm = min(ROW_TILE, seq)
    n_mix = mu.shape[0]
    kern = functools.partial(_mix_kernel, tm=tm, tiles_per_seq=seq // tm)
    row = pl.BlockSpec((tm, d), lambda i: (i, 0))
    return pl.pallas_call(
        kern,
        grid=(m // tm,),
        in_specs=[row, pl.BlockSpec((SUBLANES, d), lambda i: (jnp.maximum(i * (tm // SUBLANES) - 1, 0), 0)),
                  pl.BlockSpec((1, d), lambda i: (0, 0)), pl.BlockSpec((n_mix, d), lambda i: (0, 0))],
        out_specs=[row] * n_mix,
        out_shape=[jax.ShapeDtypeStruct((m, d), BF16)] * n_mix,
        scratch_shapes=[pltpu.VMEM((tm + SUBLANES, d), F32)],
        compiler_params=_params("parallel"),
        name="rwkv_mix",
    )(h, h, gain.reshape(1, d), mu)


WKV_LANE_GROUP = LANES // 4
WKV_ROWS_PER_BATCH = D_MODEL // LANES


def _wkv_fused_kernel(r_ref, k_ref, v_ref, w_ref, a_ref, g_ref, kkp_ref, kap_ref, rkp_ref, lnw_ref, lnb_ref,
                      o_ref, s_ref, wd_s, kh_s, kk_s, b_s, r_s, v_s, g_s, *, steps):
    n = RWKV_HEAD_DIM
    nb = LANES // WKV_LANE_GROUP
    halves = WKV_ROWS_PER_BATCH // SUBLANES
    ahead = min(WKV_LOOKAHEAD, steps)

    @pl.when(pl.program_id(0) == 0)
    def _():
        s_ref[...] = jnp.zeros_like(s_ref)

    def lane_group(cnt):
        return lax.broadcasted_iota(I32, (cnt, SUBLANES, LANES), 2) // WKV_LANE_GROUP

    def to_scan(x_ref, t0, cnt):
        grp = lane_group(cnt)
        rows = [x_ref[b, pl.ds(t0, cnt)].astype(F32) for b in range(nb)]
        pieces = []
        for n4 in range(nb):
            for v in range(halves):
                acc = None
                for b in range(nb):
                    x = rows[b][:, SUBLANES * v:SUBLANES * (v + 1), :]
                    shift = (WKV_LANE_GROUP * (b - n4)) % LANES
                    if shift:
                        x = pltpu.roll(x, shift, axis=2)
                    acc = x if acc is None else jnp.where(grp == b, x, acc)
                pieces.append(acc)
        return pieces

    def prepare(t0, cnt):
        rs, ks, vs, ws, as_, gs = (to_scan(x, t0, cnt) for x in (r_ref, k_ref, v_ref, w_ref, a_ref, g_ref))
        kks = [ks[u] * kkp_ref[SUBLANES * u:SUBLANES * (u + 1), :] for u in range(len(ks))]
        ss = kks[0] * kks[0]
        for u in range(1, len(kks)):
            ss = ss + kks[u] * kks[u]
        inv_norm = 1.0 / jnp.maximum(jnp.sqrt(jnp.sum(ss, axis=1, keepdims=True)), 1e-12)
        for u in range(len(ks)):
            rows = slice(SUBLANES * u, SUBLANES * (u + 1))
            at = (pl.ds(t0, cnt), rows, slice(None))
            kk = kks[u] * inv_norm
            wd_s[at] = jnp.exp(-jnp.exp(-_softplus(-ws[u]) - 0.5))
            kk_s[at] = kk
            b_s[at] = kk * as_[u]
            kh_s[at] = ks[u] * (1.0 + (as_[u] - 1.0) * kap_ref[rows, :])
            r_s[at] = rs[u]
            v_s[at] = vs[u]
            g_s[at] = gs[u]

    def finish(t, y):
        inv_n = 1.0 / n
        mean = jnp.sum(y, axis=0, keepdims=True) * inv_n
        yc = y - mean
        var = jnp.sum(yc * yc, axis=0, keepdims=True) * inv_n
        bonus = jnp.sum(r_s[t] * kh_s[t] * rkp_ref[...], axis=0, keepdims=True) * v_s[t]
        o = (yc * lax.rsqrt(var + RWKV_GN_EPS) * lnw_ref[...] + lnb_ref[...] + bonus) * g_s[t]
        grp = lane_group(1)[0]
        for b in range(nb):
            parts = []
            for v in range(halves):
                acc = None
                for n4 in range(nb):
                    u = n4 * halves + v
                    x = o[SUBLANES * u:SUBLANES * (u + 1), :]
                    shift = (WKV_LANE_GROUP * (n4 - b)) % LANES
                    if shift:
                        x = pltpu.roll(x, shift, axis=1)
                    acc = x if acc is None else jnp.where(grp == n4, x, acc)
                parts.append(acc)
            o_ref[b, pl.ds(t, 1)] = jnp.concatenate(parts, axis=0).astype(o_ref.dtype)[None]

    prepare(0, ahead)
    q0 = jnp.zeros((n, LANES), F32)
    for j in range(n):
        q0 = q0 + s_ref[j] * kk_s[0, j:j + 1, :]

    def step(t, q):
        tn = jnp.minimum(t + 1, steps - 1)
        v = v_s[t]
        y = jnp.zeros((n, LANES), F32)
        qn = jnp.zeros((n, LANES), F32)
        for j in range(n):
            s_new = s_ref[j] * wd_s[t, j:j + 1, :] - q * b_s[t, j:j + 1, :] + v * kh_s[t, j:j + 1, :]
            s_ref[j] = s_new
            y = y + s_new * r_s[t, j:j + 1, :]
            qn = qn + s_new * kk_s[tn, j:j + 1, :]
        finish(t, y)
        prepare(jnp.minimum(t + ahead, steps - 1), 1)
        return qn

    lax.fori_loop(0, steps, step, q0)


def _to_scan_order(x, axis):
    axis = axis % x.ndim
    shp = x.shape
    split = shp[:axis] + (RWKV_HEADS, LANES // WKV_LANE_GROUP, WKV_ROWS_PER_BATCH) + shp[axis + 1:]
    order = tuple(range(axis)) + (axis + 2, axis + 1, axis) + tuple(range(axis + 3, x.ndim + 2))
    return x.reshape(split).transpose(order).reshape(shp)


def _scan_tile(p, bsz):
    return jnp.tile(p.reshape(RWKV_HEADS, RWKV_HEAD_DIM).T, (1, bsz))


def wkv_fused(r, k, v, w_raw, a_sig, g, k_k, k_a, r_k, ln_w, ln_b, bsz, seq):
    assert bsz * RWKV_HEADS == LANES and RWKV_HEADS == WKV_LANE_GROUP
    steps = min(WKV_T, seq)
    view = lambda x: x.reshape(bsz, seq, WKV_ROWS_PER_BATCH, LANES)
    blk = pl.BlockSpec((bsz, steps, WKV_ROWS_PER_BATCH, LANES), lambda i: (0, i, 0, 0))
    par = pl.BlockSpec((RWKV_HEAD_DIM, LANES), lambda i: (0, 0))
    scr = pltpu.VMEM((steps, RWKV_HEAD_DIM, LANES), F32)
    out = pl.pallas_call(
        functools.partial(_wkv_fused_kernel, steps=steps),
        grid=(seq // steps,),
        in_specs=[blk] * 6 + [par] * 5,
        out_specs=blk,
        out_shape=jax.ShapeDtypeStruct((bsz, seq, WKV_ROWS_PER_BATCH, LANES), BF16),
        scratch_shapes=[pltpu.VMEM((RWKV_HEAD_DIM, RWKV_HEAD_DIM, LANES), F32)] + [scr] * 7,
        compiler_params=_params("arbitrary"),
        name="wkv_fused",
    )(view(r), view(k), view(v), view(w_raw), view(a_sig), view(g),
      _scan_tile(k_k, bsz), _scan_tile(k_a, bsz), _scan_tile(r_k, bsz), _scan_tile(ln_w, bsz), _scan_tile(ln_b, bsz))
    return out.reshape(bsz * seq, D_MODEL)


def _router_kernel(h_ref, g_ref, wr_ref, info_ref, tile_cnt_ref, total_ref, carry_ref, *, tt):
    E = N_EXPERTS
    i = pl.program_id(0)

    @pl.when(i == 0)
    def _():
        carry_ref[...] = jnp.zeros_like(carry_ref)

    xh, xm, xl = _split3(_rms(h_ref[...]) * g_ref[...])
    wh, wm, wl = _split3(wr_ref[...])
    logits = (_dot_nt(wh, xh) + _dot_nt(wh, xm) + _dot_nt(wm, xh)
              + _dot_nt(wh, xl) + _dot_nt(wl, xh) + _dot_nt(wm, xm))
    eidx = lax.broadcasted_iota(I32, (E, tt), 0)
    v1 = jnp.max(logits, axis=0, keepdims=True)
    i1 = jnp.min(jnp.where(logits == v1, eidx, E), axis=0, keepdims=True)
    rest = jnp.where(eidx == i1, -jnp.inf, logits)
    v2 = jnp.max(rest, axis=0, keepdims=True)
    i2 = jnp.min(jnp.where(rest == v2, eidx, E), axis=0, keepdims=True)
    e2 = jnp.exp(v2 - v1)
    g1 = 1.0 / (1.0 + e2)
    g2 = e2 / (1.0 + e2)
    hit1 = eidx == i1
    hit2 = eidx == i2
    onehot = jnp.where(hit1 | hit2, 1.0, 0.0)
    rr = lax.broadcasted_iota(I32, (tt, tt), 0)
    cc = lax.broadcasted_iota(I32, (tt, tt), 1)
    before = jnp.where(rr < cc, 1.0, 0.0).astype(BF16)
    carry = carry_ref[...]
    carry_wide = jnp.concatenate([carry] * (tt // LANES), axis=1)
    cum = _dot(onehot.astype(BF16), before) + carry_wide
    rank1 = jnp.sum(jnp.where(hit1, cum, 0.0), axis=0, keepdims=True)
    rank2 = jnp.sum(jnp.where(hit2, cum, 0.0), axis=0, keepdims=True)
    zero = jnp.zeros((1, tt), F32)
    info_ref[...] = jnp.concatenate([i1.astype(F32), i2.astype(F32), rank1, rank2, g1, g2, zero, zero], axis=0)
    tile_cnt_ref[0] = carry
    carry = carry + _dot(onehot.astype(BF16), jnp.ones((tt, LANES), BF16))
    carry_ref[...] = carry
    total_ref[...] = carry


def moe_router(h, norm_gain, w_router):
    m, d = h.shape
    tt = min(MOE_TILE, m)
    nt = m // tt
    return pl.pallas_call(
        functools.partial(_router_kernel, tt=tt),
        grid=(nt,),
        in_specs=[pl.BlockSpec((tt, d), lambda i: (i, 0)), pl.BlockSpec((1, d), lambda i: (0, 0)),
                  pl.BlockSpec((N_EXPERTS, d), lambda i: (0, 0))],
        out_specs=[pl.BlockSpec((SUBLANES, tt), lambda i: (0, i)),
                   pl.BlockSpec((1, N_EXPERTS, LANES), lambda i: (i, 0, 0)),
                   pl.BlockSpec((N_EXPERTS, LANES), lambda i: (0, 0))],
        out_shape=[jax.ShapeDtypeStruct((SUBLANES, m), F32), jax.ShapeDtypeStruct((nt, N_EXPERTS, LANES), F32),
                   jax.ShapeDtypeStruct((N_EXPERTS, LANES), F32)],
        scratch_shapes=[pltpu.VMEM((N_EXPERTS, LANES), F32)],
        compiler_params=_params("arbitrary"),
        name="moe_router",
    )(h, norm_gain.reshape(1, d), w_router.T)


def _slots_kernel(start_ref, info_ref, o_ref):
    info = info_ref[...]
    e1, e2, r1, r2 = info[0:1], info[1:2], info[2:3], info[3:4]
    s1 = r1
    s2 = r2
    for e in range(N_EXPERTS):
        st = start_ref[e].astype(F32)
        s1 = s1 + jnp.where(e1 == float(e), st, 0.0)
        s2 = s2 + jnp.where(e2 == float(e), st, 0.0)
    o_ref[...] = jnp.concatenate([s1, s2, info[4:5], info[5:6], info[6:8], info[6:8]], axis=0)


def moe_slots(info, slot_start):
    m = info.shape[1]
    tt = min(2048, m)
    return pl.pallas_call(
        _slots_kernel,
        grid_spec=pltpu.PrefetchScalarGridSpec(
            num_scalar_prefetch=1, grid=(m // tt,),
            in_specs=[pl.BlockSpec((SUBLANES, tt), lambda i, st: (0, i))],
            out_specs=pl.BlockSpec((SUBLANES, tt), lambda i, st: (0, i))),
        out_shape=jax.ShapeDtypeStruct((SUBLANES, m), F32),
        compiler_params=_params("parallel"),
        name="moe_slots",
    )(slot_start, info)


def _dispatch_kernel(blk_ref, til_ref, flg_ref, slots_ref, x_ref, o_ref, go_ref, *, bs):
    w = pl.program_id(0)
    flags = flg_ref[w]
    first = (flags & 1) != 0
    real = (flags & 4) != 0

    def gathered():
        sl = slots_ref[...]
        sid = (blk_ref[w] * bs + lax.broadcasted_iota(I32, (bs, sl.shape[1]), 0)).astype(F32)
        hit1 = sid == sl[0:1, :]
        hit2 = sid == sl[1:2, :]
        onehot = jnp.where(hit1 | hit2, 1.0, 0.0).astype(BF16)
        gate = jnp.sum(jnp.where(hit1, sl[2:3, :], 0.0) + jnp.where(hit2, sl[3:4, :], 0.0), axis=1, keepdims=True)
        return _dot(onehot, x_ref[...]).astype(o_ref.dtype), jnp.broadcast_to(gate, go_ref.shape)

    @pl.when(first & real)
    def _():
        o_ref[...], go_ref[...] = gathered()

    @pl.when(first & jnp.logical_not(real))
    def _():
        o_ref[...] = jnp.zeros_like(o_ref)
        go_ref[...] = jnp.zeros_like(go_ref)

    @pl.when(jnp.logical_not(first) & real)
    def _():
        rows, gate = gathered()
        o_ref[...] += rows
        go_ref[...] += gate


def moe_dispatch(xn, slots, wl_blk, wl_til, wl_flg, n_slots):
    m, d = xn.shape
    bs, tt = MOE_SLOT_BLOCK, MOE_TILE
    n_items = wl_blk.shape[0]
    return pl.pallas_call(
        functools.partial(_dispatch_kernel, bs=bs),
        grid_spec=pltpu.PrefetchScalarGridSpec(
            num_scalar_prefetch=3, grid=(n_items,),
            in_specs=[pl.BlockSpec((SUBLANES, tt), lambda w, b, t, f: (0, t[w])),
                      pl.BlockSpec((tt, d), lambda w, b, t, f: (t[w], 0))],
            out_specs=[pl.BlockSpec((bs, d), lambda w, b, t, f: (b[w], 0)),
                       pl.BlockSpec((bs, LANES), lambda w, b, t, f: (b[w], 0))]),
        out_shape=[jax.ShapeDtypeStruct((n_slots, d), BF16), jax.ShapeDtypeStruct((n_slots, LANES), F32)],
        compiler_params=_params("arbitrary"),
        name="moe_dispatch",
    )(wl_blk, wl_til, wl_flg, slots, xn)


def _moe_ffn_kernel(be_ref, rows_ref, x_ref, gate_ref, w1_ref, w3_ref, w2_ref, o_ref, acc_ref, *, sub):
    b = pl.program_id(0)
    f = pl.program_id(1)
    bm = x_ref.shape[0]

    @pl.when(f == 0)
    def _():
        acc_ref[...] = jnp.zeros_like(acc_ref)

    for rows in range(sub, bm + 1, sub):
        @pl.when(rows_ref[b] == rows)
        def _(rows=rows):
            x = x_ref[0:rows, :]
            h = _silu(_dot(x, w1_ref[0].astype(BF16))) * _dot(x, w3_ref[0].astype(BF16))
            acc_ref[0:rows, :] += _dot(h.astype(BF16), w2_ref[0].astype(BF16))

    @pl.when(f == pl.num_programs(1) - 1)
    def _():
        o_ref[...] = (acc_ref[...] * gate_ref[:, 0:1]).astype(o_ref.dtype)


def moe_ffn(xs, slot_gate, w1, w3, w2, block_expert, block_rows):
    n_slots, d = xs.shape
    ff = w1.shape[2]
    bm = MOE_FFN_ROWS
    tf = MOE_TF
    nf = ff // tf

    def f_of(b, f, rows):
        return jnp.where(rows[b] > 0, f, nf - 1)

    return pl.pallas_call(
        functools.partial(_moe_ffn_kernel, sub=MOE_ROW_GROUP),
        grid_spec=pltpu.PrefetchScalarGridSpec(
            num_scalar_prefetch=2, grid=(n_slots // bm, nf),
            in_specs=[pl.BlockSpec((bm, d), lambda b, f, be, nb: (b, 0)),
                      pl.BlockSpec((bm, LANES), lambda b, f, be, nb: (b, 0)),
                      pl.BlockSpec((1, d, tf), lambda b, f, be, nb: (be[b], 0, f_of(b, f, nb))),
                      pl.BlockSpec((1, d, tf), lambda b, f, be, nb: (be[b], 0, f_of(b, f, nb))),
                      pl.BlockSpec((1, tf, d), lambda b, f, be, nb: (be[b], f_of(b, f, nb), 0))],
            out_specs=pl.BlockSpec((bm, d), lambda b, f, be, nb: (b, 0)),
            scratch_shapes=[pltpu.VMEM((bm, d), F32)]),
        out_shape=jax.ShapeDtypeStruct((n_slots, d), BF16),
        compiler_params=_params("arbitrary", "arbitrary"),
        name="moe_ffn",
    )(block_expert, block_rows, xs, slot_gate, w1, w3, w2)


def _combine_kernel(til_ref, blk_ref, flg_ref, tok_ref, y_ref, res_ref, o_ref, *, bs):
    w = pl.program_id(0)
    flags = flg_ref[w]

    @pl.when((flags & 1) != 0)
    def _():
        o_ref[...] = res_ref[...]

    @pl.when((flags & 4) != 0)
    def _():
        tok = tok_ref[...]
        tt = tok.shape[0]
        sid = (blk_ref[w] * bs + lax.broadcasted_iota(I32, (tt, bs), 1)).astype(F32)
        onehot = jnp.where((sid == tok[:, 0:1]) | (sid == tok[:, 1:2]), 1.0, 0.0).astype(BF16)
        o_ref[...] += _dot(onehot, y_ref[...])


def moe_combine(y, tok_info, residual, wl_til, wl_blk, wl_flg):
    m, d = residual.shape
    bs, tt = MOE_SLOT_BLOCK, MOE_TILE
    n_items = wl_til.shape[0]
    return pl.pallas_call(
        functools.partial(_combine_kernel, bs=bs),
        grid_spec=pltpu.PrefetchScalarGridSpec(
            num_scalar_prefetch=3, grid=(n_items,),
            in_specs=[pl.BlockSpec((tt, SUBLANES), lambda w, t, b, f: (t[w], 0)),
                      pl.BlockSpec((bs, d), lambda w, t, b, f: (b[w], 0)),
                      pl.BlockSpec((tt, d), lambda w, t, b, f: (t[w], 0))],
            out_specs=pl.BlockSpec((tt, d), lambda w, t, b, f: (t[w], 0))),
        out_shape=jax.ShapeDtypeStruct((m, d), F32),
        compiler_params=_params("arbitrary"),
        name="moe_combine",
    )(wl_til, wl_blk, wl_flg, tok_info, y, residual)


def _work_list(overlap, n_items):
    n_major, n_minor = overlap.shape
    has = jnp.any(overlap, axis=1)
    marked = jnp.concatenate([overlap[:, :1] | ~has[:, None], overlap[:, 1:]], axis=1)
    maj, mino = jnp.nonzero(marked, size=n_items, fill_value=-1)
    valid = maj >= 0
    real = valid & overlap[jnp.maximum(maj, 0), jnp.maximum(mino, 0)]
    prev = jnp.concatenate([jnp.full((1,), -2, maj.dtype), maj[:-1]])
    nxt = jnp.concatenate([maj[1:], jnp.full((1,), -2, maj.dtype)])
    flags = (valid & (maj != prev)) * 1 + (valid & (maj != nxt)) * 2 + real * 4
    last_maj = jnp.max(jnp.where(valid, maj, 0))
    last_min = jnp.sum(jnp.where(valid & (maj == last_maj) & (maj != nxt), mino, 0))
    maj = jnp.where(valid, maj, last_maj)
    mino = jnp.where(valid, mino, last_min)
    return maj.astype(I32), mino.astype(I32), flags.astype(I32)


def moe_layer(h, norm_gain, w_router, w1, w3, w2):
    m, d = h.shape
    E = N_EXPERTS
    tile = min(MOE_TILE, m)
    n_tiles = m // tile
    ffn_rows = MOE_FFN_ROWS
    xn = rmsnorm(h, norm_gain, BF16)
    info, tile_cnt, total = moe_router(h, norm_gain, w_router)
    before = tile_cnt[:, :, 0]
    counts = total[:, 0].astype(I32)
    padded = (counts + ffn_rows - 1) // ffn_rows * ffn_rows
    ends = jnp.cumsum(padded)
    starts = ends - padded
    n_slots = (m * TOP_K // ffn_rows + E) * ffn_rows
    n_ffn_blocks = n_slots // ffn_rows
    sblk = MOE_SLOT_BLOCK
    n_dblocks = n_slots // sblk
    slots = moe_slots(info, starts.astype(I32))
    db_start = jnp.arange(n_dblocks, dtype=I32) * sblk
    db_e = jnp.minimum(jnp.sum(db_start[:, None] >= ends[None, :], axis=1), E - 1)
    r0 = db_start - starts[db_e]
    r1 = jnp.minimum(r0 + sblk, counts[db_e])
    after = jnp.concatenate([before[1:], counts[None, :].astype(F32)], axis=0).astype(I32)
    bef = before.astype(I32)
    ov = (bef[:, db_e].T < r1[:, None]) & (after[:, db_e].T > r0[:, None]) & (db_start < ends[-1])[:, None]
    n_items = n_dblocks + E * n_tiles
    d_blk, d_til, d_flg = _work_list(ov, n_items)
    c_til, c_blk, c_flg = _work_list(ov.T, n_items)
    xs, slot_gate = moe_dispatch(xn, slots, d_blk, d_til, d_flg, n_slots)
    fb_start = jnp.arange(n_ffn_blocks, dtype=I32) * ffn_rows
    block_expert = jnp.minimum(jnp.sum(fb_start[:, None] >= ends[None, :], axis=1), E - 1).astype(I32)
    filled = counts[block_expert] - (fb_start - starts[block_expert])
    grp = MOE_ROW_GROUP
    block_rows = jnp.where(fb_start < ends[-1], (jnp.clip(filled, 0, ffn_rows) + grp - 1) // grp * grp, 0).astype(I32)
    ys = moe_ffn(xs, slot_gate, w1, w3, w2, block_expert, block_rows)
    return moe_combine(ys, slots.T, h, c_til, c_blk, c_flg)


def even_mixer_layer(h, posf, p, bsz, seq):
    m = bsz * seq
    hn = rmsnorm(h, p["norm_mix"], BF16)
    w_in = p["w_in"]
    c = np.cumsum([0, D_SSM, D_CONV, SSM_HEADS, D_NSA] + [D_NSA_KV] * 6 + [3 * NSA_HEADS])
    w_proj = jnp.concatenate([w_in[:, c[1]:c[2]], w_in[:, c[4]:c[10]], w_in[:, c[0]:c[1]], w_in[:, c[3]:c[4]]], axis=1)
    proj = matmul([(hn, w_proj.astype(BF16))], out_dtype=BF16)
    kv0 = D_CONV
    w_small = jnp.concatenate([w_in[:, c[2]:c[3]], w_in[:, c[10]:c[11]]], axis=1)
    w_small = jnp.pad(w_small, ((0, 0), (0, LANES - w_small.shape[1]))).astype(BF16)
    small = matmul([(hn, w_small)])
    y_ssm = ssd_group(proj, (D_CONV + 6 * D_NSA_KV) // D_SSM, 0, small, p["conv_w"], p["conv_b"], p["dt_bias"],
                      p["a_log"], p["d_skip"], p["ssm_norm"], bsz, seq)
    qn, ksn, kwn = rope_norm(posf, proj, (D_CONV + 6 * D_NSA_KV + D_SSM) // D_NSA, kv0 // (6 * D_NSA_KV),
                             p["q_gain"], p["ks_gain"], p["kw_gain"])
    nch = seq // CMP_STRIDE

    def chunks(col0):
        u = proj[:, kv0 + col0:kv0 + col0 + D_NSA_KV].reshape(bsz, nch, CMP_STRIDE, NSA_KV_HEADS, NSA_HEAD_DIM)
        return u.transpose(0, 3, 1, 2, 4).reshape(bsz, NSA_KV_HEADS, nch, CMP_STRIDE * NSA_HEAD_DIM)

    kc, vc = compress(chunks(0), chunks(D_NSA_KV), p["pe_k"], p["pe_v"], p["wk1"], p["wk2"], p["wv1"], p["wv2"],
                      p["kc_gain"])
    o_cmp, selt = cmp_select(qn, kc, vc, bsz, seq)
    def v_tiles(col0, tk):
        tk = min(tk, seq)
        u = proj[:, kv0 + col0:kv0 + col0 + D_NSA_KV].reshape(bsz, seq // tk, tk, NSA_KV_HEADS, NSA_HEAD_DIM)
        return u.transpose(0, 3, 1, 4, 2)

    o_sel = masked_attention(qn, ksn, 0, v_tiles(3 * D_NSA_KV, ATT_TK), bsz, seq, "sel", selt)
    o_win = masked_attention(qn, kwn, 0, v_tiles(5 * D_NSA_KV, WIN_TK), bsz, seq, "win")
    y_nsa = nsa_combine(small, o_cmp, o_sel, o_win)
    w_out = p["w_out"].astype(BF16)
    return matmul([(y_ssm, w_out[:D_SSM]), (y_nsa, w_out[D_SSM:])], residual=h)


def rwkv_layer(h, p, bsz, seq):
    m, d = h.shape
    mixes = rwkv_mix(h, p["norm_mix"], p["mu"], seq)
    bf = lambda w: w.astype(BF16)

    def lora_pad(w_a, w_b):
        r = w_a.shape[1]
        rp = -(-r // LANES) * LANES
        return bf(jnp.pad(w_a, ((0, 0), (0, rp - r)))), bf(jnp.pad(w_b, ((0, rp - r), (0, 0))))

    so = _to_scan_order
    r = matmul([(mixes[0], bf(so(p["w_r"], 1)))], out_dtype=BF16)
    k = matmul([(mixes[2], bf(so(p["w_k"], 1)))], out_dtype=BF16)
    v = matmul([(mixes[3], bf(so(p["w_v"], 1)))], out_dtype=BF16)
    w1, w2 = lora_pad(p["w1"], so(p["w2"], 1))
    w_raw = lora(mixes[1], w1, w2, so(p["w0"], 0), "tanh", None, F32)
    a1, a2 = lora_pad(p["a1"], so(p["a2"], 1))
    a_sig = lora(mixes[4], a1, a2, so(p["a0"], 0), None, "sigmoid", BF16)
    g1, g2 = lora_pad(p["g1"], so(p["g2"], 1))
    g = lora(mixes[5], g1, g2, jnp.zeros((d,), F32), "sigmoid", None, BF16)
    o = wkv_fused(r, k, v, w_raw, a_sig, g, p["k_k"], p["k_a"], p["r_k"].reshape(-1), p["ln_w"], p["ln_b"], bsz, seq)
    return matmul([(o, bf(so(p["w_o"], 0)))], residual=h)


def kernel(x, mem, positions, norm_mix, norm_xattn, norm_mem, norm_ffn, xattn_wq, xattn_wkv, xattn_wo, xattn_q_gain, xattn_k_gain, ev_w_in, ev_conv_w, ev_conv_b, ev_dt_bias, ev_a_log, ev_d_skip, ev_ssm_norm, ev_q_gain, ev_kc_gain, ev_ks_gain, ev_kw_gain, ev_pe_k, ev_pe_v, ev_cmp_wk1, ev_cmp_wk2, ev_cmp_wv1, ev_cmp_wv2, ev_w_out, ev_ffn_w1, ev_ffn_w3, ev_ffn_w2, od_mu, od_w_r, od_w_k, od_w_v, od_w_o, od_w0, od_w1, od_w2, od_a0, od_a1, od_a2, od_g1, od_g2, od_k_k, od_k_a, od_r_k, od_ln_w, od_ln_b, od_router, od_moe_w1, od_moe_w3, od_moe_w2):
    bsz, seq, d = x.shape
    m = bsz * seq
    depth = norm_mix.shape[0]
    h = x.reshape(m, d)
    mem2d = mem.reshape(-1, d)
    posf = positions.astype(F32).reshape(m, 1)
    for layer in range(depth):
        i = layer // 2
        if layer % 2 == 0:
            p = dict(norm_mix=norm_mix[layer], w_in=ev_w_in[i], conv_w=ev_conv_w[i], conv_b=ev_conv_b[i],
                     dt_bias=ev_dt_bias[i], a_log=ev_a_log[i], d_skip=ev_d_skip[i], ssm_norm=ev_ssm_norm[i],
                     q_gain=ev_q_gain[i], kc_gain=ev_kc_gain[i], ks_gain=ev_ks_gain[i], kw_gain=ev_kw_gain[i],
                     pe_k=ev_pe_k[i], pe_v=ev_pe_v[i], wk1=ev_cmp_wk1[i], wk2=ev_cmp_wk2[i], wv1=ev_cmp_wv1[i],
                     wv2=ev_cmp_wv2[i], w_out=ev_w_out[i])
            h = even_mixer_layer(h, posf, p, bsz, seq)
        else:
            p = dict(norm_mix=norm_mix[layer], mu=od_mu[i], w_r=od_w_r[i], w_k=od_w_k[i], w_v=od_w_v[i], w_o=od_w_o[i],
                     w0=od_w0[i], w1=od_w1[i], w2=od_w2[i], a0=od_a0[i], a1=od_a1[i], a2=od_a2[i], g1=od_g1[i],
                     g2=od_g2[i], k_k=od_k_k[i], k_a=od_k_a[i], r_k=od_r_k[i], ln_w=od_ln_w[i], ln_b=od_ln_b[i])
            h = rwkv_layer(h, p, bsz, seq)
        h = memory_xattn(h, mem2d, norm_xattn[layer], norm_mem[layer], xattn_wq[layer], xattn_wkv[layer],
                         xattn_wo[layer], xattn_q_gain[layer], xattn_k_gain[layer], bsz, seq)
        if layer % 2 == 0:
            h = swiglu_ffn(h, norm_ffn[layer], ev_ffn_w1[i].astype(BF16), ev_ffn_w3[i].astype(BF16),
                           ev_ffn_w2[i].astype(BF16))
        else:
            h = moe_layer(h, norm_ffn[layer], od_router[i], od_moe_w1[i], od_moe_w3[i], od_moe_w2[i])
    return h.reshape(bsz, seq, d)
```
